```python
import math
import jax
import jax.numpy as jnp
from jax import lax
import numpy as np

D_MODEL = 1024
BATCH = 2
SEQ = 8192
DEPTH = 4
DEC_BATCH = 32
DEC_SEQ = 4
PAST_LEN = 8192
PAGE_SIZE = 128

HEAD_DIM = 64
N_EVEN = (DEPTH + 1) // 2
N_ODD = DEPTH // 2

GLA_WIDTH = D_MODEL // 2
GLA_HEADS = 4
GLA_DV = GLA_WIDTH // GLA_HEADS
GLA_DK = GLA_DV // 2
GLA_RANK = 16
GLA_TAU = 16.0
GLA_CHUNK = 64

NSA_WIDTH = D_MODEL // 2
NSA_HEADS = NSA_WIDTH // HEAD_DIM
NSA_KV_HEADS = 2
NSA_GROUP = NSA_HEADS // NSA_KV_HEADS
CMP_STRIDE = 16
CMP_BLOCK = 2 * CMP_STRIDE
CMP_HIDDEN = 256
SEL_BLOCK = 64
SEL_RATIO = SEL_BLOCK // CMP_STRIDE
SEL_TOPN = 16
NSA_WINDOW = 512
Q_BLOCK = 128

FOX_WIDTH = D_MODEL // 2
FOX_HEADS = FOX_WIDTH // HEAD_DIM
FOX_GATE_BIAS = 2.0
CONV_CH = D_MODEL // 2
CONV_WIDTH = 31

MEM_LEN = 256
XA_HEADS = 4
XA_WIDTH = XA_HEADS * HEAD_DIM
FFN_HIDDEN = -(-8 * D_MODEL // 768) * 256
NUM_BUCKETS = 32
MAX_DISTANCE = 128
EPS = 1e-6

EVEN_SIZES = (GLA_HEADS * GLA_DK, GLA_HEADS * GLA_DK, GLA_WIDTH, GLA_RANK, GLA_WIDTH,
              NSA_WIDTH, 6 * NSA_KV_HEADS * HEAD_DIM, 3 * NSA_HEADS)
ODD_SIZES = (FOX_WIDTH, FOX_WIDTH, FOX_WIDTH, FOX_HEADS, 2 * CONV_CH)

kernel_name = 'gla_nsa_fox_conformer_hybrid_step'


def _split(h, sizes):
    return jnp.split(h, np.cumsum(sizes)[:-1].tolist(), axis=-1)


def rmsnorm(x, g):
    xf = x.astype(jnp.float32)
    y = xf * lax.rsqrt(jnp.mean(xf * xf, axis=-1, keepdims=True) + EPS)
    return (y * g.astype(jnp.float32)).astype(x.dtype)


def layernorm(x, g, b):
    xf = x.astype(jnp.float32)
    mu = jnp.mean(xf, axis=-1, keepdims=True)
    var = jnp.mean(jnp.square(xf - mu), axis=-1, keepdims=True)
    y = (xf - mu) * lax.rsqrt(var + EPS) * g.astype(jnp.float32) + b.astype(jnp.float32)
    return y.astype(x.dtype)


def masked_softmax(s, mask):
    s = jnp.where(mask, s, -1e30)
    p = jnp.exp(s - jnp.max(s, axis=-1, keepdims=True)) * mask
    return p / jnp.maximum(jnp.sum(p, axis=-1, keepdims=True), 1e-30)


def rel_bucket(dist):
    exact = NUM_BUCKETS // 2
    d = jnp.maximum(dist, 0)
    log_ratio = jnp.log(jnp.maximum(d, 1).astype(jnp.float32) / exact) / math.log(MAX_DISTANCE / exact)
    large = jnp.minimum(exact + (log_ratio * (NUM_BUCKETS - exact)).astype(jnp.int32), NUM_BUCKETS - 1)
    return jnp.where(d < exact, d, large)


def gather_pages(pool, page_table):
    g = pool[page_table]
    return g.reshape((g.shape[0], g.shape[1] * g.shape[2]) + g.shape[3:])


def gla_scan(q, k, v, log_a, s0):
    B, T, H, _ = q.shape
    C = min(GLA_CHUNK, T)
    n = -(-T // C)
    pad = n * C - T

    def prep(a):
        a = jnp.pad(a.astype(jnp.float32), ((0, 0), (0, pad), (0, 0), (0, 0)))
        return a.reshape(B, n, C, H, a.shape[-1]).transpose(1, 0, 3, 2, 4)

    causal = jnp.tril(jnp.ones((C, C), dtype=bool))

    def step(S, inp):
        qc, kc, vc, ac = inp
        b = jnp.cumsum(ac, axis=2)
        dec = jnp.exp(jnp.where(causal[:, :, None], b[:, :, :, None, :] - b[:, :, None, :, :], -jnp.inf))
        att = jnp.einsum('bhtd,bhsd,bhtsd->bhts', qc, kc, dec)
        o = jnp.einsum('bhts,bhsv->bhtv', att, vc) + jnp.einsum('bhtd,bhdv->bhtv', qc * jnp.exp(b), S)
        b_end = b[:, :, -1]
        S = S * jnp.exp(b_end)[..., None] + jnp.einsum('bhsd,bhsv->bhdv', kc * jnp.exp(b_end[:, :, None] - b), vc)
        return S, o

    S, o = lax.scan(step, s0.astype(jnp.float32), (prep(q), prep(k), prep(v), prep(log_a)))
    o = o.transpose(1, 0, 3, 2, 4).reshape(B, n * C, H, v.shape[-1])[:, :T]
    return o.astype(v.dtype), S


def nsa_prepare(kv_rows, pe, w1, w2, kn_cmp):
    B, T = kv_rows.shape[:2]
    nc = (T - CMP_BLOCK) // CMP_STRIDE + 1
    seg = kv_rows[:, :(nc + 1) * CMP_STRIDE, 0:2].reshape(B, nc + 1, CMP_STRIDE, 2, NSA_KV_HEADS, HEAD_DIM)
    blk = jnp.concatenate([seg[:, :-1], seg[:, 1:]], axis=2)
    blk = blk + pe.transpose(1, 0, 2)[None, None, :, :, None, :]
    flat = blk.transpose(0, 1, 3, 4, 2, 5).reshape(B, nc, 2, NSA_KV_HEADS, CMP_BLOCK * HEAD_DIM)
    hid = jax.nn.gelu(jnp.einsum('bnjgf,jfe->bnjge', flat, w1))
    ckv = jnp.einsum('bnjge,jed->bnjgd', hid, w2)
    ck = rmsnorm(ckv[:, :, 0], kn_cmp)
    cv = ckv[:, :, 1]
    c_end = jnp.arange(nc) * CMP_STRIDE + (CMP_BLOCK - 1)
    nbs = -(-T // SEL_BLOCK)
    sel = jnp.pad(kv_rows[:, :, 2:4], ((0, 0), (0, nbs * SEL_BLOCK - T), (0, 0), (0, 0), (0, 0)))
    sel = sel.reshape(B, nbs, SEL_BLOCK, 2, NSA_KV_HEADS, HEAD_DIM).transpose(3, 0, 4, 1, 2, 5)
    return ck, cv, c_end, sel[0], sel[1]


def nsa_attend(q, gates, q_pos, ck, cv, c_end, sk, sv, wk, wv, w_pos, rel_bias):
    B, Q = q.shape[:2]
    scale = HEAD_DIM ** -0.5
    tbl = rel_bias.astype(jnp.float32).T.reshape(NSA_KV_HEADS, NSA_GROUP, NUM_BUCKETS)
    qg = q.reshape(B, Q, NSA_KV_HEADS, NSA_GROUP, HEAD_DIM).transpose(0, 2, 3, 1, 4)
    t = q_pos[:, None]
    s = jnp.einsum('bkgqd,bnkd->bkgqn', qg, ck).astype(jnp.float32) * scale + tbl[:, :, rel_bucket(t - c_end[None])]
    p_cmp = masked_softmax(s, c_end[None] <= t)
    o_cmp = jnp.einsum('bkgqn,bnkd->bkgqd', p_cmp.astype(cv.dtype), cv)
    nbs = sk.shape[2]
    imp = jnp.sum(p_cmp, axis=2)
    nc = imp.shape[-1]
    imp = jnp.pad(imp, ((0, 0), (0, 0), (0, 0), (0, nbs * SEL_RATIO - nc)))
    imp = imp.reshape(B, NSA_KV_HEADS, Q, nbs, SEL_RATIO).sum(-1)
    blk = jnp.arange(nbs)[None]
    cur = (q_pos // SEL_BLOCK)[:, None]
    forced = (blk == 0) | (blk == cur) | (blk == cur - 1)
    score = jnp.where(blk > cur, -jnp.inf, jnp.where(forced, jnp.inf, imp))
    n_sel = min(SEL_TOPN, nbs)
    _, idx = lax.top_k(score, n_sel)
    pick = jax.vmap(jax.vmap(lambda kb, ix: kb[ix]))
    gk = pick(sk, idx).reshape(B, NSA_KV_HEADS, Q, n_sel * SEL_BLOCK, HEAD_DIM)
    gv = pick(sv, idx).reshape(B, NSA_KV_HEADS, Q, n_sel * SEL_BLOCK, HEAD_DIM)
    spos = (idx[..., None] * SEL_BLOCK + jnp.arange(SEL_BLOCK)).reshape(B, NSA_KV_HEADS, Q, n_sel * SEL_BLOCK)
    kidx = jnp.arange(NSA_KV_HEADS)[None, :, None, None, None]
    gidx = jnp.arange(NSA_GROUP)[None, None, :, None, None]
    bias = tbl[kidx, gidx, rel_bucket(t - spos)[:, :, None]]
    s = jnp.einsum('bkgqd,bkqsd->bkgqs', qg, gk).astype(jnp.float32) * scale + bias
    p = masked_softmax(s, (spos <= t)[:, :, None])
    o_sel = jnp.einsum('bkgqs,bkqsd->bkgqd', p.astype(gv.dtype), gv)
    dist = t - w_pos[None]
    s = jnp.einsum('bkgqd,bwkd->bkgqw', qg, wk).astype(jnp.float32) * scale + tbl[:, :, rel_bucket(dist)]
    p = masked_softmax(s, (dist >= 0) & (dist <= NSA_WINDOW) & (w_pos[None] >= 0))
    o_win = jnp.einsum('bkgqw,bwkd->bkgqd', p.astype(wv.dtype), wv)
    o = jnp.stack([o_cmp, o_sel, o_win], axis=-1).transpose(0, 3, 1, 2, 4, 5)
    o = o.reshape(B, Q, NSA_HEADS, HEAD_DIM, 3)
    return jnp.einsum('bqhdr,bqhr->bqhd', o, gates.astype(o.dtype)).reshape(B, Q, NSA_WIDTH)


def nsa_prompt(q, gates, kv_rows, win_rows, pe, w1, w2, kn_cmp, rel_bias):
    B, T = q.shape[:2]
    ck, cv, c_end, sk, sv = nsa_prepare(kv_rows, pe, w1, w2, kn_cmp)
    win_pad = jnp.pad(win_rows, ((0, 0), (NSA_WINDOW, 0), (0, 0), (0, 0), (0, 0)))

    def block(i):
        q0 = i * Q_BLOCK
        qb = lax.dynamic_slice_in_dim(q, q0, Q_BLOCK, axis=1)
        gb = lax.dynamic_slice_in_dim(gates, q0, Q_BLOCK, axis=1)
        wb = lax.dynamic_slice_in_dim(win_pad, q0, NSA_WINDOW + Q_BLOCK, axis=1)
        q_pos = q0 + jnp.arange(Q_BLOCK)
        w_pos = q0 - NSA_WINDOW + jnp.arange(NSA_WINDOW + Q_BLOCK)
        return nsa_attend(qb, gb, q_pos, ck, cv, c_end, sk, sv, wb[:, :, 0], wb[:, :, 1], w_pos, rel_bias)

    o = lax.map(block, jnp.arange(T // Q_BLOCK))
    return o.transpose(1, 0, 2, 3).reshape(B, T, NSA_WIDTH)


def even_inputs(xn, w_in, gla_w_gate, gla_b_gate, nsa_q_norm, nsa_k_norm):
    B, T, _ = xn.shape
    gq, gk, gv, glr, gog, nq, nkv, ng = _split(xn @ w_in, EVEN_SIZES)
    q = gq.reshape(B, T, GLA_HEADS, GLA_DK) * GLA_DK ** -0.5
    k = gk.reshape(B, T, GLA_HEADS, GLA_DK)
    v = gv.reshape(B, T, GLA_HEADS, GLA_DV)
    log_a = jax.nn.log_sigmoid((glr @ gla_w_gate + gla_b_gate).astype(jnp.float32)) / GLA_TAU
    log_a = log_a.reshape(B, T, GLA_HEADS, GLA_DK)
    out_gate = jax.nn.silu(gog).reshape(B, T, GLA_HEADS, GLA_DV)
    nq = rmsnorm(nq.reshape(B, T, NSA_HEADS, HEAD_DIM), nsa_q_norm)
    kv = nkv.reshape(B, T, 6, NSA_KV_HEADS, HEAD_DIM)
    sel_k = rmsnorm(kv[:, :, 2], nsa_k_norm[1])
    win_k = rmsnorm(kv[:, :, 4], nsa_k_norm[2])
    kv_rows = jnp.stack([kv[:, :, 0], kv[:, :, 1], sel_k, kv[:, :, 3]], axis=2)
    win_rows = jnp.stack([win_k, kv[:, :, 5]], axis=2)
    gates = jax.nn.sigmoid(ng.reshape(B, T, NSA_HEADS, 3))
    return q, k, v, log_a, out_gate, nq, gates, kv_rows, win_rows


def even_output(o_gla, out_gate, gla_out_norm, o_nsa, w_out):
    B, T = o_gla.shape[:2]
    o_gla = (rmsnorm(o_gla, gla_out_norm) * out_gate).reshape(B, T, GLA_WIDTH)
    return jnp.concatenate([o_gla, o_nsa.astype(o_gla.dtype)], axis=-1) @ w_out


def odd_inputs(xn, w_in, q_norm, k_norm, b_f):
    B, T, _ = xn.shape
    fq, fk, fv, ff, cg = _split(xn @ w_in, ODD_SIZES)
    q = rmsnorm(fq.reshape(B, T, FOX_HEADS, HEAD_DIM), q_norm)
    k = rmsnorm(fk.reshape(B, T, FOX_HEADS, HEAD_DIM), k_norm)
    v = fv.reshape(B, T, FOX_HEADS, HEAD_DIM)
    logf = jax.nn.log_sigmoid(ff.astype(jnp.float32) + b_f.astype(jnp.float32))
    u = cg[..., :CONV_CH] * jax.nn.sigmoid(cg[..., CONV_CH:])
    return q, k, v, logf, u


def fox_attend(q, cq, q_pos, k, v, ck, k_pos):
    s = jnp.einsum('bqhd,bshd->bhqs', q, k).astype(jnp.float32) * HEAD_DIM ** -0.5
    s = s + jnp.swapaxes(cq, 1, 2)[..., :, None] - jnp.swapaxes(ck, 1, 2)[..., None, :]
    p = masked_softmax(s, k_pos[None, :] <= q_pos[:, None])
    return jnp.einsum('bhqs,bshd->bqhd', p.astype(v.dtype), v)


def fox_prompt(q, k, v, logf):
    B, T = q.shape[:2]
    c = jnp.cumsum(logf, axis=1)
    k_pos = jnp.arange(T)

    def block(i):
        q0 = i * Q_BLOCK
        qb = lax.dynamic_slice_in_dim(q, q0, Q_BLOCK, axis=1)
        cqb = lax.dynamic_slice_in_dim(c, q0, Q_BLOCK, axis=1)
        return fox_attend(qb, cqb, q0 + jnp.arange(Q_BLOCK), k, v, c, k_pos)

    o = lax.map(block, jnp.arange(T // Q_BLOCK))
    return o.transpose(1, 0, 2, 3, 4).reshape(B, T, FOX_WIDTH)


def conv_module(u_ext, w, b, ln_g, ln_b):
    y = lax.conv_general_dilated(u_ext, w[:, None, :].astype(u_ext.dtype), (1,), 'VALID',
                                 dimension_numbers=('NWC', 'WIO', 'NWC'), feature_group_count=CONV_CH)
    y = y + b.astype(y.dtype)
    return jax.nn.silu(layernorm(y, ln_g, ln_b))


def mem_keys_values(mem, g, wkv, k_norm):
    B, M, _ = mem.shape
    kv = (rmsnorm(mem, g) @ wkv).reshape(B, M, 2, XA_HEADS, HEAD_DIM)
    return jnp.stack([rmsnorm(kv[:, :, 0], k_norm), kv[:, :, 1]], axis=2)


def cross_attend(xn, mem_kv, wq, wo, q_norm):
    B, T, _ = xn.shape
    q = rmsnorm((xn @ wq).reshape(B, T, XA_HEADS, HEAD_DIM), q_norm)
    s = jnp.einsum('bthd,bmhd->bhtm', q, mem_kv[:, :, 0]).astype(jnp.float32) * HEAD_DIM ** -0.5
    p = jax.nn.softmax(s, axis=-1).astype(xn.dtype)
    o = jnp.einsum('bhtm,bmhd->bthd', p, mem_kv[:, :, 1]).reshape(B, T, XA_WIDTH)
    return o @ wo


def swiglu(xn, w_in, w_out):
    gu = xn @ w_in
    return (jax.nn.silu(gu[..., :FFN_HIDDEN]) * gu[..., FFN_HIDDEN:]) @ w_out


def setup_inputs(seed: int = 0) -> dict:
    key = jax.random.key(seed)
    keys = iter(jax.random.split(key, 64))

    def nrm(shape, scale):
        return jax.random.normal(next(keys), shape, jnp.float32) * scale

    def gain(shape):
        return 1.0 + nrm(shape, 0.02)

    n_pages = PAST_LEN // PAGE_SIZE
    n_phys = (DEC_BATCH * n_pages * 5) // 4
    perm = jax.random.permutation(next(keys), n_phys)
    page_table = perm[:DEC_BATCH * n_pages].reshape(DEC_BATCH, n_pages).astype(jnp.int32)
    d_even = sum(EVEN_SIZES)
    d_odd = sum(ODD_SIZES)
    return {
        'x_prompt': nrm((BATCH, SEQ, D_MODEL), 1.0),
        'x_sample': nrm((DEC_BATCH, DEC_SEQ, D_MODEL), 1.0),
        'cache_nsa_kv': nrm((N_EVEN, n_phys, PAGE_SIZE, 4, NSA_KV_HEADS, HEAD_DIM), 1.0),
        'state_nsa_win': nrm((N_EVEN, DEC_BATCH, min(NSA_WINDOW, PAST_LEN), 2, NSA_KV_HEADS, HEAD_DIM), 1.0),
        'state_gla': nrm((N_EVEN, DEC_BATCH, GLA_HEADS, GLA_DK, GLA_DV), 1.0),
        'cache_fox_kv': nrm((N_ODD, n_phys, PAGE_SIZE, 2, FOX_HEADS, HEAD_DIM), 1.0),
        'cache_fox_logf': jax.nn.log_sigmoid(FOX_GATE_BIAS + nrm((N_ODD, n_phys, PAGE_SIZE, FOX_HEADS), 1.0)),
        'state_conv': nrm((N_ODD, DEC_BATCH, CONV_WIDTH - 1, CONV_CH), 0.5),
        'cache_mem_kv': nrm((DEPTH, DEC_BATCH, MEM_LEN, 2, XA_HEADS, HEAD_DIM), 1.0),
        'page_table': page_table,
        'mem_prompt': nrm((BATCH, MEM_LEN, D_MODEL), 1.0),
        'rel_bias': nrm((NUM_BUCKETS, NSA_HEADS), 0.5),
        'norm_mix': gain((DEPTH, D_MODEL)),
        'norm_xattn': gain((DEPTH, D_MODEL)),
        'norm_ffn': gain((DEPTH, D_MODEL)),
        'even_w_in': nrm((N_EVEN, D_MODEL, d_even), D_MODEL ** -0.5),
        'even_w_out': nrm((N_EVEN, GLA_WIDTH + NSA_WIDTH, D_MODEL), (GLA_WIDTH + NSA_WIDTH) ** -0.5),
        'gla_w_gate': nrm((N_EVEN, GLA_RANK, GLA_HEADS * GLA_DK), GLA_RANK ** -0.5),
        'gla_b_gate': nrm((N_EVEN, GLA_HEADS * GLA_DK), 0.1),
        'gla_out_norm': gain((N_EVEN, GLA_DV)),
        'nsa_q_norm': gain((N_EVEN, HEAD_DIM)),
        'nsa_k_norm': gain((N_EVEN, 3, HEAD_DIM)),
        'nsa_cmp_pe': nrm((N_EVEN, 2, CMP_BLOCK, HEAD_DIM), 0.1),
        'nsa_cmp_w1': nrm((N_EVEN, 2, CMP_BLOCK * HEAD_DIM, CMP_HIDDEN), (CMP_BLOCK * HEAD_DIM) ** -0.5),
        'nsa_cmp_w2': nrm((N_EVEN, 2, CMP_HIDDEN, HEAD_DIM), CMP_HIDDEN ** -0.5),
        'odd_w_in': nrm((N_ODD, D_MODEL, d_odd), D_MODEL ** -0.5),
        'odd_w_out': nrm((N_ODD, FOX_WIDTH + CONV_CH, D_MODEL), (FOX_WIDTH + CONV_CH) ** -0.5),
        'fox_q_norm': gain((N_ODD, HEAD_DIM)),
        'fox_k_norm': gain((N_ODD, HEAD_DIM)),
        'fox_b_f': FOX_GATE_BIAS + nrm((N_ODD, FOX_HEADS), 0.1),
        'conv_w': nrm((N_ODD, CONV_WIDTH, CONV_CH), CONV_WIDTH ** -0.5),
        'conv_b': nrm((N_ODD, CONV_CH), 0.02),
        'conv_ln_g': gain((N_ODD, CONV_CH)),
        'conv_ln_b': nrm((N_ODD, CONV_CH), 0.02),
        'mem_norm': gain((DEPTH, D_MODEL)),
        'xa_wq': nrm((DEPTH, D_MODEL, XA_WIDTH), D_MODEL ** -0.5),
        'xa_wkv': nrm((DEPTH, D_MODEL, 2 * XA_WIDTH), D_MODEL ** -0.5),
        'xa_wo': nrm((DEPTH, XA_WIDTH, D_MODEL), XA_WIDTH ** -0.5),
        'xa_q_norm': gain((DEPTH, HEAD_DIM)),
        'xa_k_norm': gain((DEPTH, HEAD_DIM)),
        'ffn_w_in': nrm((DEPTH, D_MODEL, 2 * FFN_HIDDEN), D_MODEL ** -0.5),
        'ffn_w_out': nrm((DEPTH, FFN_HIDDEN, D_MODEL), FFN_HIDDEN ** -0.5),
    }


def reference(x_prompt, x_sample, cache_nsa_kv, state_nsa_win, state_gla, cache_fox_kv, cache_fox_logf,
              state_conv, cache_mem_kv, page_table, mem_prompt, rel_bias, norm_mix, norm_xattn, norm_ffn,
              even_w_in, even_w_out, gla_w_gate, gla_b_gate, gla_out_norm, nsa_q_norm, nsa_k_norm,
              nsa_cmp_pe, nsa_cmp_w1, nsa_cmp_w2, odd_w_in, odd_w_out, fox_q_norm, fox_k_norm, fox_b_f,
              conv_w, conv_b, conv_ln_g, conv_ln_b, mem_norm, xa_wq, xa_wkv, xa_wo, xa_q_norm, xa_k_norm,
              ffn_w_in, ffn_w_out):
    past_len = page_table.shape[1] * cache_nsa_kv.shape[2]
    n_win = state_nsa_win.shape[2]
    dec_pos = past_len + jnp.arange(x_sample.shape[1])
    yp, ys = x_prompt, x_sample
    nsa_kv_p, nsa_kv_s, win_p, win_s, gla_p, gla_s = [], [], [], [], [], []
    fox_kv_p, fox_kv_s, logf_p, logf_s, conv_p, conv_s, mem_kv_p = [], [], [], [], [], [], []
    for layer in range(DEPTH):
        if layer % 2 == 0:
            e = layer // 2
            xn = rmsnorm(yp, norm_mix[layer])
            q, k, v, la, og, nq, ng, kvr, wr = even_inputs(xn, even_w_in[e], gla_w_gate[e], gla_b_gate[e],
                                                           nsa_q_norm[e], nsa_k_norm[e])
            s0 = jnp.zeros((yp.shape[0], GLA_HEADS, GLA_DK, GLA_DV), jnp.float32)
            o_gla, s_fin = gla_scan(q, k, v, la, s0)
            o_nsa = nsa_prompt(nq, ng, kvr, wr, nsa_cmp_pe[e], nsa_cmp_w1[e], nsa_cmp_w2[e], nsa_k_norm[e][0], rel_bias)
            yp = yp + even_output(o_gla, og, gla_out_norm[e], o_nsa, even_w_out[e])
            nsa_kv_p.append(kvr)
            win_p.append(wr[:, -min(NSA_WINDOW, wr.shape[1]):])
            gla_p.append(s_fin)
            xn = rmsnorm(ys, norm_mix[layer])
            q, k, v, la, og, nq, ng, kvr, wr = even_inputs(xn, even_w_in[e], gla_w_gate[e], gla_b_gate[e],
                                                           nsa_q_norm[e], nsa_k_norm[e])
            o_gla, s_new = gla_scan(q, k, v, la, state_gla[e])
            full = jnp.concatenate([gather_pages(cache_nsa_kv[e], page_table).astype(kvr.dtype), kvr], axis=1)
            wfull = jnp.concatenate([state_nsa_win[e].astype(wr.dtype), wr], axis=1)
            w_pos = past_len - n_win + jnp.arange(wfull.shape[1])
            ck, cv, c_end, sk, sv = nsa_prepare(full, nsa_cmp_pe[e], nsa_cmp_w1[e], nsa_cmp_w2[e], nsa_k_norm[e][0])
            o_nsa = nsa_attend(nq, ng, dec_pos, ck, cv, c_end, sk, sv, wfull[:, :, 0], wfull[:, :, 1], w_pos, rel_bias)
            ys = ys + even_output(o_gla, og, gla_out_norm[e], o_nsa, even_w_out[e])
            nsa_kv_s.append(kvr)
            win_s.append(wfull[:, -n_win:])
            gla_s.append(s_new)
        else:
            j = layer // 2
            xn = rmsnorm(yp, norm_mix[layer])
            q, k, v, lf, u = odd_inputs(xn, odd_w_in[j], fox_q_norm[j], fox_k_norm[j], fox_b_f[j])
            o_fox = fox_prompt(q, k, v, lf)
            u_ext = jnp.pad(u, ((0, 0), (CONV_WIDTH - 1, 0), (0, 0)))
            o_conv = conv_module(u_ext, conv_w[j], conv_b[j], conv_ln_g[j], conv_ln_b[j])
            yp = yp + jnp.concatenate([o_fox.astype(o_conv.dtype), o_conv], axis=-1) @ odd_w_out[j]
            fox_kv_p.append(jnp.stack([k, v], axis=2))
            logf_p.append(lf)
            conv_p.append(u_ext[:, -(CONV_WIDTH - 1):])
            xn = rmsnorm(ys, norm_mix[layer])
            q, k, v, lf, u = odd_inputs(xn, odd_w_in[j], fox_q_norm[j], fox_k_norm[j], fox_b_f[j])
            new_kv = jnp.stack([k, v], axis=2)
            kv_full = jnp.concatenate([gather_pages(cache_fox_kv[j], page_table).astype(new_kv.dtype), new_kv], axis=1)
            lf_full = jnp.concatenate([gather_pages(cache_fox_logf[j], page_table).astype(jnp.float32), lf], axis=1)
            c_full = jnp.cumsum(lf_full, axis=1)
            o_fox = fox_attend(q, c_full[:, past_len:], dec_pos, kv_full[:, :, 0], kv_full[:, :, 1], c_full,
                               jnp.arange(kv_full.shape[1]))
            o_fox = o_fox.reshape(ys.shape[0], ys.shape[1], FOX_WIDTH)
            u_ext = jnp.concatenate([state_conv[j].astype(u.dtype), u], axis=1)
            o_conv = conv_module(u_ext, conv_w[j], conv_b[j], conv_ln_g[j], conv_ln_b[j])
            ys = ys + jnp.concatenate([o_fox.astype(o_conv.dtype), o_conv], axis=-1) @ odd_w_out[j]
            fox_kv_s.append(new_kv)
            logf_s.append(lf)
            conv_s.append(u_ext[:, -(CONV_WIDTH - 1):])
        mkv = mem_keys_values(mem_prompt, mem_norm[layer], xa_wkv[layer], xa_k_norm[layer])
        mem_kv_p.append(mkv)
        yp = yp + cross_attend(rmsnorm(yp, norm_xattn[layer]), mkv, xa_wq[layer], xa_wo[layer], xa_q_norm[layer])
        ys = ys + cross_attend(rmsnorm(ys, norm_xattn[layer]), cache_mem_kv[layer].astype(ys.dtype),
                               xa_wq[layer], xa_wo[layer], xa_q_norm[layer])
        yp = yp + swiglu(rmsnorm(yp, norm_ffn[layer]), ffn_w_in[layer], ffn_w_out[layer])
        ys = ys + swiglu(rmsnorm(ys, norm_ffn[layer]), ffn_w_in[layer], ffn_w_out[layer])
    return (yp, ys,
            jnp.stack(nsa_kv_p), jnp.stack(nsa_kv_s), jnp.stack(win_p), jnp.stack(win_s),
            jnp.stack(gla_p), jnp.stack(gla_s), jnp.stack(fox_kv_p), jnp.stack(fox_kv_s),
            jnp.stack(logf_p), jnp.stack(logf_s), jnp.stack(conv_p), jnp.stack(conv_s),
            jnp.stack(mem_kv_p))
```

```python
import functools
import math

import jax
import jax.numpy as jnp
import numpy as np
from jax import lax
from jax.experimental import pallas as pl
from jax.experimental.pallas import tpu as pltpu

F32 = jnp.float32
BF16 = jnp.bfloat16

D_MODEL = 1024
HEAD_DIM = 64
GLA_WIDTH = 512
GLA_HEADS = 4
GLA_DV = 128
GLA_DK = 64
GLA_RANK = 16
GLA_TAU = 16.0
NSA_WIDTH = 512
NSA_HEADS = 8
NSA_KV_HEADS = 2
NSA_GROUP = 4
CMP_STRIDE = 16
CMP_BLOCK = 32
CMP_HIDDEN = 256
SEL_BLOCK = 64
SEL_RATIO = 4
SEL_TOPN = 16
NSA_WINDOW = 512
FOX_WIDTH = 512
FOX_HEADS = 8
CONV_CH = 512
CONV_WIDTH = 31
MEM_LEN = 256
XA_HEADS = 4
XA_WIDTH = 256
FFN_HIDDEN = 2816
NUM_BUCKETS = 32
MAX_DISTANCE = 128
EPS = 1e-6
SCALE = HEAD_DIM ** -0.5
NEG = -1e30

EVEN_SIZES = (256, 256, 512, 16, 512, 512, 768, 24)
ODD_SIZES = (512, 512, 512, 8, 1024)

LANES = 128
VMEM_LIMIT_BYTES = 56 * 1024 * 1024
SAMPLE_ROWS = 8
SUB = 16
N_CMP = 512
N_SELBLK = 128


def _cparams(sem, vmem=None):
    return pltpu.CompilerParams(dimension_semantics=sem, vmem_limit_bytes=vmem)


def _split(h, sizes):
    return jnp.split(h, np.cumsum(sizes)[:-1].tolist(), axis=-1)


def _rms_rows(x, g):
    return x * lax.rsqrt(jnp.mean(x * x, axis=-1, keepdims=True) + EPS) * g


def _group_rms(x, gmat, gs):
    x2 = x * x
    hi = x2.astype(BF16)
    lo = (x2 - hi.astype(F32)).astype(BF16)
    ms = (jnp.dot(hi, gmat, preferred_element_type=F32) + jnp.dot(lo, gmat, preferred_element_type=F32)) * (1.0 / gs)
    return x * lax.rsqrt(ms + EPS)


def _log_sigmoid(z):
    return -(jnp.maximum(-z, 0.0) + jnp.log1p(jnp.exp(-jnp.abs(z))))


def _sigmoid(z):
    return 1.0 / (1.0 + jnp.exp(-z))


def _dot_nt(a, b):
    return lax.dot_general(a, b, (((1,), (1,)), ((), ())), preferred_element_type=F32)


def _dot_tn(a, b):
    return lax.dot_general(a, b, (((0,), (0,)), ((), ())), preferred_element_type=F32)


def _dot_f32(a, b):
    return jnp.dot(a, b, preferred_element_type=F32, precision=lax.Precision.HIGHEST)


def _block_ones(width, gs):
    r = np.arange(width) // gs
    return jnp.asarray((r[:, None] == r[None, :]).astype(np.float32), dtype=BF16)


def _even_in_kernel(x_ref, g_ref, w_ref, wg_ref, bg_ref, qn_ref, kn1_ref, kn2_ref, gm512_ref, gm128_ref,
                    q_ref, k_ref, v_ref, la_ref, og_ref, nqs_ref, kvr_ref, wr_ref, gt_ref, selkv_ref, winkv_ref):
    xb = _rms_rows(x_ref[...], g_ref[...]).astype(BF16)

    def proj(lo, hi):
        return jnp.dot(xb, w_ref[:, lo:hi], preferred_element_type=F32)

    q_ref[...] = proj(0, 256) * (GLA_DK ** -0.5)
    k_ref[...] = proj(256, 512)
    v_ref[...] = proj(512, 1024)
    og = proj(1024, 1536)
    og_ref[...] = og * _sigmoid(og)
    nq = _group_rms(proj(1536, 2048), gm512_ref[...], HEAD_DIM) * qn_ref[...]
    nqs_ref[...] = (nq * SCALE).astype(BF16)
    kvr_ref[:, 0:256] = proj(2048, 2304)
    selk = _group_rms(proj(2304, 2432), gm128_ref[...], HEAD_DIM) * kn1_ref[...]
    selv = proj(2432, 2560)
    kvr_ref[:, 256:384] = selk
    kvr_ref[:, 384:512] = selv
    selkv_ref[:, 0:128] = selk.astype(BF16)
    selkv_ref[:, 128:256] = selv.astype(BF16)
    wink = _group_rms(proj(2560, 2688), gm128_ref[...], HEAD_DIM) * kn2_ref[...]
    winv = proj(2688, 2816)
    wr_ref[:, 0:128] = wink
    wr_ref[:, 128:256] = winv
    winkv_ref[:, 0:128] = wink.astype(BF16)
    winkv_ref[:, 128:256] = winv.astype(BF16)
    glr = proj(2816, 2944).astype(BF16)
    z = jnp.dot(glr, wg_ref[...], preferred_element_type=F32) + bg_ref[...]
    la_ref[...] = _log_sigmoid(z) * (1.0 / GLA_TAU)
    gt_ref[...] = _sigmoid(proj(2944, 3072))


def _even_in(x2d, g, w_pad, wg_pad, bg, qn, kn1, kn2, tm):
    m = x2d.shape[0]
    widths = (256, 256, 512, 256, 512, 512, 512, 256, 128, 256, 256)
    dtypes = (F32, F32, F32, F32, F32, BF16, F32, F32, F32, BF16, BF16)
    full = lambda a: pl.BlockSpec(a.shape, lambda i: (0,) * a.ndim)
    gm512 = _block_ones(512, HEAD_DIM)
    gm128 = _block_ones(128, HEAD_DIM)
    ins = (x2d, g, w_pad, wg_pad, bg, qn, kn1, kn2, gm512, gm128)
    return pl.pallas_call(
        _even_in_kernel,
        grid=(m // tm,),
        in_specs=[pl.BlockSpec((tm, D_MODEL), lambda i: (i, 0))] + [full(a) for a in ins[1:]],
        out_specs=[pl.BlockSpec((tm, w), lambda i: (i, 0)) for w in widths],
        out_shape=[jax.ShapeDtypeStruct((m, w), d) for w, d in zip(widths, dtypes)],
        compiler_params=_cparams(("arbitrary",), VMEM_LIMIT_BYTES),
        name="even_in",
    )(*ins)


def _odd_in_kernel(x_ref, g_ref, w_ref, qn_ref, kn_ref, bf_ref, gm512_ref, tri_ref,
                   qs_ref, kv_ref, kb_ref, vb_ref, lf_ref, c_ref, u_ref, carry_ref):
    @pl.when(pl.program_id(1) == 0)
    def _():
        carry_ref[...] = jnp.zeros_like(carry_ref)

    xb = _rms_rows(x_ref[...], g_ref[...]).astype(BF16)

    def proj(lo, hi):
        return jnp.dot(xb, w_ref[:, lo:hi], preferred_element_type=F32)

    q = _group_rms(proj(0, 512), gm512_ref[...], HEAD_DIM) * qn_ref[...]
    qs_ref[...] = (q * SCALE).astype(BF16)
    k = _group_rms(proj(512, 1024), gm512_ref[...], HEAD_DIM) * kn_ref[...]
    v = proj(1024, 1536)
    kv_ref[:, 0:512] = k
    kv_ref[:, 512:1024] = v
    kb_ref[...] = k.astype(BF16)
    vb_ref[...] = v.astype(BF16)
    u_ref[...] = proj(1536, 2048) * _sigmoid(proj(2048, 2560))
    lf = _log_sigmoid(proj(2560, 2688) + bf_ref[...])
    lf_ref[...] = lf
    c = _dot_f32(tri_ref[...], lf) + carry_ref[0:1, :]
    c_ref[...] = c
    carry_ref[0:1, :] = c[-1:, :]


def _odd_in(x3d, g, w_pad, qn, kn, bf_pad, tm):
    b, t, _ = x3d.shape
    widths = (512, 1024, 512, 512, 128, 128, 512)
    dtypes = (BF16, F32, BF16, BF16, F32, F32, F32)
    gm512 = _block_ones(512, HEAD_DIM)
    tri = jnp.asarray(np.tril(np.ones((tm, tm), np.float32)))
    ins = (x3d, g, w_pad, qn, kn, bf_pad, gm512, tri)
    full = lambda a: pl.BlockSpec(a.shape, lambda bi, i: (0,) * a.ndim)
    return pl.pallas_call(
        _odd_in_kernel,
        grid=(b, t // tm),
        in_specs=[pl.BlockSpec((None, tm, D_MODEL), lambda bi, i: (bi, i, 0))] + [full(a) for a in ins[1:]],
        out_specs=[pl.BlockSpec((None, tm, w), lambda bi, i: (bi, i, 0)) for w in widths],
        out_shape=[jax.ShapeDtypeStruct((b, t, w), d) for w, d in zip(widths, dtypes)],
        scratch_shapes=[pltpu.VMEM((8, 128), F32)],
        compiler_params=_cparams(("arbitrary", "arbitrary"), VMEM_LIMIT_BYTES),
        name="odd_in",
    )(*ins)


def _out_proj_kernel(res_ref, a1_ref, a2_ref, w1_ref, w2_ref, o_ref):
    acc = jnp.dot(a1_ref[...].astype(BF16), w1_ref[...], preferred_element_type=F32)
    acc = acc + jnp.dot(a2_ref[...].astype(BF16), w2_ref[...], preferred_element_type=F32)
    o_ref[...] = res_ref[...] + acc


def _out_proj(res, a1, a2, w1, w2, tm):
    m = res.shape[0]
    row = lambda a: pl.BlockSpec((tm, a.shape[1]), lambda i: (i, 0))
    full = lambda a: pl.BlockSpec(a.shape, lambda i: (0, 0))
    return pl.pallas_call(
        _out_proj_kernel,
        grid=(m // tm,),
        in_specs=[row(res), row(a1), row(a2), full(w1), full(w2)],
        out_specs=row(res),
        out_shape=jax.ShapeDtypeStruct(res.shape, F32),
        compiler_params=_cparams(("arbitrary",), VMEM_LIMIT_BYTES),
        name="out_proj",
    )(res, a1, a2, w1, w2)


def _ffn_kernel(x_ref, g_ref, wg_ref, wu_ref, wo_ref, o_ref, xn_ref, acc_ref):
    j = pl.program_id(1)

    @pl.when(j == 0)
    def _():
        xn_ref[...] = _rms_rows(x_ref[...], g_ref[...]).astype(BF16)
        acc_ref[...] = jnp.zeros_like(acc_ref)

    xb = xn_ref[...]
    gate = jnp.dot(xb, wg_ref[...], preferred_element_type=F32)
    up = jnp.dot(xb, wu_ref[...], preferred_element_type=F32)
    h = (gate * _sigmoid(gate) * up).astype(BF16)
    acc_ref[...] += jnp.dot(h, wo_ref[...], preferred_element_type=F32)

    @pl.when(j == pl.num_programs(1) - 1)
    def _():
        o_ref[...] = x_ref[...] + acc_ref[...]


def _ffn(x2d, g, w_in, w_out, tm, n_chunks=2):
    m = x2d.shape[0]
    th = FFN_HIDDEN // n_chunks
    return pl.pallas_call(
        _ffn_kernel,
        grid=(m // tm, n_chunks),
        in_specs=[pl.BlockSpec((tm, D_MODEL), lambda i, j: (i, 0)),
                  pl.BlockSpec((1, D_MODEL), lambda i, j: (0, 0)),
                  pl.BlockSpec((D_MODEL, th), lambda i, j: (0, j)),
                  pl.BlockSpec((D_MODEL, th), lambda i, j: (0, n_chunks + j)),
                  pl.BlockSpec((th, D_MODEL), lambda i, j: (j, 0))],
        out_specs=pl.BlockSpec((tm, D_MODEL), lambda i, j: (i, 0)),
        out_shape=jax.ShapeDtypeStruct(x2d.shape, F32),
        scratch_shapes=[pltpu.VMEM((tm, D_MODEL), BF16), pltpu.VMEM((tm, D_MODEL), F32)],
        compiler_params=_cparams(("arbitrary", "arbitrary"), VMEM_LIMIT_BYTES),
        name="ffn",
    )(x2d, g, w_in, w_in, w_out)


def _mem_kv_kernel(m_ref, g_ref, w_ref, kn_ref, gm_ref, o_ref):
    xb = _rms_rows(m_ref[...], g_ref[...]).astype(BF16)
    kv = jnp.dot(xb, w_ref[...], preferred_element_type=F32)
    o_ref[:, 0:XA_WIDTH] = _group_rms(kv[:, 0:XA_WIDTH], gm_ref[...], HEAD_DIM) * kn_ref[...]
    o_ref[:, XA_WIDTH:] = kv[:, XA_WIDTH:]


def _mem_kv(mem, g, wkv, kn):
    b = mem.shape[0]
    gm = _block_ones(XA_WIDTH, HEAD_DIM)
    full = lambda a: pl.BlockSpec(a.shape, lambda i: (0,) * a.ndim)
    return pl.pallas_call(
        _mem_kv_kernel,
        grid=(b,),
        in_specs=[pl.BlockSpec((None, MEM_LEN, D_MODEL), lambda i: (i, 0, 0)), full(g), full(wkv), full(kn), full(gm)],
        out_specs=pl.BlockSpec((None, MEM_LEN, 2 * XA_WIDTH), lambda i: (i, 0, 0)),
        out_shape=jax.ShapeDtypeStruct((b, MEM_LEN, 2 * XA_WIDTH), F32),
        compiler_params=_cparams(("arbitrary",)),
        name="mem_kv",
    )(mem, g, wkv, kn, gm)


def _xattn_kernel(x_ref, mkv_ref, g_ref, wq_ref, wo_ref, qn_ref, gm_ref, o_ref):
    x = x_ref[...]
    xb = _rms_rows(x, g_ref[...]).astype(BF16)
    q = jnp.dot(xb, wq_ref[...], preferred_element_type=F32)
    q = _group_rms(q, gm_ref[...], HEAD_DIM) * qn_ref[...]
    qb = (q * SCALE).astype(BF16)
    outs = []
    for h in range(XA_HEADS):
        kh = mkv_ref[:, h * HEAD_DIM:(h + 1) * HEAD_DIM].astype(BF16)
        vh = mkv_ref[:, XA_WIDTH + h * HEAD_DIM:XA_WIDTH + (h + 1) * HEAD_DIM].astype(BF16)
        s = _dot_nt(qb[:, h * HEAD_DIM:(h + 1) * HEAD_DIM], kh)
        e = jnp.exp(s - jnp.max(s, axis=-1, keepdims=True))
        p = (e / jnp.sum(e, axis=-1, keepdims=True)).astype(BF16)
        outs.append(jnp.dot(p, vh, preferred_element_type=F32))
    o = jnp.concatenate(outs, axis=-1).astype(BF16)
    o_ref[...] = x + jnp.dot(o, wo_ref[...], preferred_element_type=F32)


def _xattn(x3d, mkv, g, wq, wo, qn, tm):
    b, t, _ = x3d.shape
    gm = _block_ones(XA_WIDTH, HEAD_DIM)
    full = lambda a: pl.BlockSpec(a.shape, lambda bi, i: (0,) * a.ndim)
    return pl.pallas_call(
        _xattn_kernel,
        grid=(b, t // tm),
        in_specs=[pl.BlockSpec((None, tm, D_MODEL), lambda bi, i: (bi, i, 0)),
                  pl.BlockSpec((None, MEM_LEN, 2 * XA_WIDTH), lambda bi, i: (bi, 0, 0)),
                  full(g), full(wq), full(wo), full(qn), full(gm)],
        out_specs=pl.BlockSpec((None, tm, D_MODEL), lambda bi, i: (bi, i, 0)),
        out_shape=jax.ShapeDtypeStruct(x3d.shape, F32),
        compiler_params=_cparams(("arbitrary", "arbitrary"), VMEM_LIMIT_BYTES),
        name="xattn",
    )(x3d, mkv, g, wq, wo, qn, gm)


def _gla_kernel(q_ref, k_ref, v_ref, la_ref, og_ref, gn_ref, s0_ref, tri_ref, hsel_ref,
                o_ref, sfin_ref, st_ref, *, n_sub, n_valid):
    ti = pl.program_id(1)

    @pl.when(ti == 0)
    def _():
        st_ref[...] = s0_ref[...]

    tri = tri_ref[...]
    hsel = hsel_ref[...]
    gn = gn_ref[...]
    row = lax.broadcasted_iota(jnp.int32, (SUB, 1), 0)

    def sub_block(i, carry):
        r0 = pl.multiple_of(i * SUB, SUB)
        rows = pl.ds(r0, SUB)
        q = q_ref[rows, :]
        k = k_ref[rows, :]
        v = v_ref[rows, :]
        la = la_ref[rows, :]
        if n_valid is not None:
            live = (row + r0) < n_valid
            la = jnp.where(live, la, 0.0)
            k = jnp.where(live, k, 0.0)
        b = _dot_f32(tri, la)
        b_end = b[SUB - 1:SUB, :]
        qd = (q * jnp.exp(b)).astype(BF16)
        kd = (k * jnp.exp(b_end - b)).astype(BF16)
        vb = v.astype(BF16)
        tiles = []
        for s in range(SUB):
            e = jnp.exp(jnp.minimum(b - b[s:s + 1, :], 0.0))
            z = (q * k[s:s + 1, :]) * e
            tiles.append(jnp.where(row >= s, z, 0.0))
        att = jnp.dot(jnp.concatenate(tiles, axis=0).astype(BF16), hsel, preferred_element_type=F32)
        dec = jnp.exp(b_end)
        outs = []
        for h in range(GLA_HEADS):
            dk = slice(h * GLA_DK, (h + 1) * GLA_DK)
            dv = slice(h * GLA_DV, (h + 1) * GLA_DV)
            st = st_ref[h]
            o = _dot_nt(qd[:, dk], st.astype(BF16))
            for s in range(SUB):
                o = o + att[s * SUB:(s + 1) * SUB, h:h + 1] * v[s:s + 1, dv]
            st_ref[h] = st * dec[:, dk] + _dot_tn(vb[:, dv], kd[:, dk])
            outs.append(_rms_rows(o, gn))
        o_ref[rows, :] = jnp.concatenate(outs, axis=-1) * og_ref[rows, :]
        return carry

    lax.fori_loop(0, n_sub, sub_block, 0)

    @pl.when(ti == pl.num_programs(1) - 1)
    def _():
        sfin_ref[...] = st_ref[...]


def _gla(q, k, v, la, og, gn, s0t, tt, n_valid=None):
    b, t, _ = q.shape
    tri = jnp.asarray(np.tril(np.ones((SUB, SUB), np.float32)))
    hsel = jnp.asarray((np.arange(256)[:, None] // GLA_DK == np.arange(128)[None, :]).astype(np.float32), dtype=BF16)
    seq = lambda w: pl.BlockSpec((None, tt, w), lambda bi, i: (bi, i, 0))
    full = lambda a: pl.BlockSpec(a.shape, lambda bi, i: (0,) * a.ndim)
    st_spec = pl.BlockSpec((None, GLA_HEADS, GLA_DV, GLA_DK), lambda bi, i: (bi, 0, 0, 0))
    return pl.pallas_call(
        functools.partial(_gla_kernel, n_sub=tt // SUB, n_valid=n_valid),
        grid=(b, t // tt),
        in_specs=[seq(256), seq(256), seq(512), seq(256), seq(512), full(gn), st_spec, full(tri), full(hsel)],
        out_specs=[seq(512), st_spec],
        out_shape=[jax.ShapeDtypeStruct((b, t, GLA_WIDTH), F32),
                   jax.ShapeDtypeStruct((b, GLA_HEADS, GLA_DV, GLA_DK), F32)],
        scratch_shapes=[pltpu.VMEM((GLA_HEADS, GLA_DV, GLA_DK), F32)],
        compiler_params=_cparams(("arbitrary", "arbitrary")),
        name="gla",
    )(q, k, v, la, og, gn, s0t, tri, hsel)


def _conv_kernel(u_ref, st0_ref, w_ref, b_ref, g_ref, beta_ref, o_ref, st_ref, ext_ref, *, tt, n_valid):
    ti = pl.program_id(1)
    ctx = CONV_WIDTH - 1

    @pl.when(ti == 0)
    def _():
        ext_ref[0:8, :] = jnp.zeros((8, CONV_CH), F32)
        ext_ref[pl.ds(2, ctx), :] = st0_ref[...]

    ext_ref[pl.ds(32, tt), :] = u_ref[...]
    acc = jnp.zeros((tt, CONV_CH), F32)
    for w in range(CONV_WIDTH):
        acc = acc + ext_ref[pl.ds(2 + w, tt), :] * w_ref[w:w + 1, :]
    y = acc + b_ref[...]
    mu = jnp.mean(y, axis=-1, keepdims=True)
    var = jnp.mean(jnp.square(y - mu), axis=-1, keepdims=True)
    ln = (y - mu) * lax.rsqrt(var + EPS) * g_ref[...] + beta_ref[...]
    o_ref[...] = ln * _sigmoid(ln)

    @pl.when(ti == pl.num_programs(1) - 1)
    def _():
        st_ref[...] = ext_ref[pl.ds(32 + n_valid - ctx, ctx), :]

    ext_ref[0:32, :] = ext_ref[pl.ds(tt, 32), :]


def _conv(u, st0, w, b, g, beta, tt, n_valid):
    bsz, t, _ = u.shape
    ctx = CONV_WIDTH - 1
    full = lambda a: pl.BlockSpec(a.shape, lambda bi, i: (0,) * a.ndim)
    st_spec = pl.BlockSpec((None, ctx, CONV_CH), lambda bi, i: (bi, 0, 0))
    return pl.pallas_call(
        functools.partial(_conv_kernel, tt=tt, n_valid=n_valid),
        grid=(bsz, t // tt),
        in_specs=[pl.BlockSpec((None, tt, CONV_CH), lambda bi, i: (bi, i, 0)), st_spec,
                  full(w), full(b), full(g), full(beta)],
        out_specs=[pl.BlockSpec((None, tt, CONV_CH), lambda bi, i: (bi, i, 0)), st_spec],
        out_shape=[jax.ShapeDtypeStruct(u.shape, F32), jax.ShapeDtypeStruct((bsz, ctx, CONV_CH), F32)],
        scratch_shapes=[pltpu.VMEM((32 + max(tt, 32), CONV_CH), F32)],
        compiler_params=_cparams(("arbitrary", "arbitrary")),
        name="conv",
    )(u, st0, w, b, g, beta)


def _fox_kernel(q_ref, k_ref, v_ref, cq_ref, ckt_ref, o_ref, m_ref, l_ref, acc_ref, *, tq, tk):
    qi = pl.program_id(1)
    ki = pl.program_id(2)
    last = (qi * tq + tq - 1) // tk

    @pl.when(ki == 0)
    def _():
        m_ref[...] = jnp.full_like(m_ref, NEG)
        l_ref[...] = jnp.zeros_like(l_ref)
        acc_ref[...] = jnp.zeros_like(acc_ref)

    @pl.when(ki <= last)
    def _():
        t_pos = qi * tq + lax.broadcasted_iota(jnp.int32, (tq, tk), 0)
        s_pos = ki * tk + lax.broadcasted_iota(jnp.int32, (tq, tk), 1)
        mask = s_pos <= t_pos
        for h in range(FOX_HEADS):
            cols = slice(h * HEAD_DIM, (h + 1) * HEAD_DIM)
            s = _dot_nt(q_ref[:, cols], k_ref[:, cols])
            s = s + cq_ref[:, h:h + 1] - ckt_ref[h:h + 1, :]
            s = jnp.where(mask, s, NEG)
            m_old = m_ref[h]
            m_new = jnp.maximum(m_old, jnp.max(s, axis=-1, keepdims=True))
            p = jnp.where(mask, jnp.exp(s - m_new), 0.0)
            alpha = jnp.exp(m_old - m_new)
            l_ref[h] = alpha * l_ref[h] + jnp.sum(p, axis=-1, keepdims=True)
            acc_ref[:, cols] = alpha * acc_ref[:, cols] + jnp.dot(p.astype(BF16), v_ref[:, cols],
                                                                  preferred_element_type=F32)
            m_ref[h] = m_new

    @pl.when(ki == pl.num_programs(2) - 1)
    def _():
        for h in range(FOX_HEADS):
            cols = slice(h * HEAD_DIM, (h + 1) * HEAD_DIM)
            o_ref[:, cols] = acc_ref[:, cols] / jnp.maximum(l_ref[h], 1e-30)


def _fox_prompt(qs, kb, vb, c8, ct, tq, tk):
    b, t, _ = qs.shape
    kv_map = lambda bi, qi, ki: (bi, jnp.minimum(ki, (qi * tq + tq - 1) // tk), 0)
    return pl.pallas_call(
        functools.partial(_fox_kernel, tq=tq, tk=tk),
        grid=(b, t // tq, t // tk),
        in_specs=[pl.BlockSpec((None, tq, FOX_WIDTH), lambda bi, qi, ki: (bi, qi, 0)),
                  pl.BlockSpec((None, tk, FOX_WIDTH), kv_map),
                  pl.BlockSpec((None, tk, FOX_WIDTH), kv_map),
                  pl.BlockSpec((None, tq, FOX_HEADS), lambda bi, qi, ki: (bi, qi, 0)),
                  pl.BlockSpec((None, FOX_HEADS, tk),
                               lambda bi, qi, ki: (bi, 0, jnp.minimum(ki, (qi * tq + tq - 1) // tk)))],
        out_specs=pl.BlockSpec((None, tq, FOX_WIDTH), lambda bi, qi, ki: (bi, qi, 0)),
        out_shape=jax.ShapeDtypeStruct((b, t, FOX_WIDTH), F32),
        scratch_shapes=[pltpu.VMEM((FOX_HEADS, tq, 1), F32), pltpu.VMEM((FOX_HEADS, tq, 1), F32),
                        pltpu.VMEM((tq, FOX_WIDTH), F32)],
        compiler_params=_cparams(("arbitrary", "arbitrary", "arbitrary"), VMEM_LIMIT_BYTES),
        name="fox_prompt",
    )(qs, kb, vb, c8, ct)


def _j_rmsnorm(x, g):
    xf = x.astype(jnp.float32)
    y = xf * lax.rsqrt(jnp.mean(xf * xf, axis=-1, keepdims=True) + EPS)
    return (y * g.astype(jnp.float32)).astype(x.dtype)


def _j_masked_softmax(s, mask):
    s = jnp.where(mask, s, -1e30)
    p = jnp.exp(s - jnp.max(s, axis=-1, keepdims=True)) * mask
    return p / jnp.maximum(jnp.sum(p, axis=-1, keepdims=True), 1e-30)


def _j_rel_bucket(dist):
    exact = NUM_BUCKETS // 2
    d = jnp.maximum(dist, 0)
    log_ratio = jnp.log(jnp.maximum(d, 1).astype(jnp.float32) / exact) / math.log(MAX_DISTANCE / exact)
    large = jnp.minimum(exact + (log_ratio * (NUM_BUCKETS - exact)).astype(jnp.int32), NUM_BUCKETS - 1)
    return jnp.where(d < exact, d, large)


def _j_gather_pages(pool, page_table):
    g = pool.reshape(pool.shape[0], -1)[page_table]
    g = g.reshape(page_table.shape + pool.shape[1:])
    return g.reshape((g.shape[0], g.shape[1] * g.shape[2]) + g.shape[3:])


def _j_nsa_prepare(kv_rows, pe, w1, w2, kn_cmp):
    B, T = kv_rows.shape[:2]
    nc = (T - CMP_BLOCK) // CMP_STRIDE + 1
    seg = kv_rows[:, :(nc + 1) * CMP_STRIDE, 0:2].reshape(B, nc + 1, CMP_STRIDE, 2, NSA_KV_HEADS, HEAD_DIM)
    blk = jnp.concatenate([seg[:, :-1], seg[:, 1:]], axis=2)
    blk = blk + pe.transpose(1, 0, 2)[None, None, :, :, None, :]
    flat = blk.transpose(0, 1, 3, 4, 2, 5).reshape(B, nc, 2, NSA_KV_HEADS, CMP_BLOCK * HEAD_DIM)
    hid = jax.nn.gelu(jnp.einsum('bnjgf,jfe->bnjge', flat, w1))
    ckv = jnp.einsum('bnjge,jed->bnjgd', hid, w2)
    ck = _j_rmsnorm(ckv[:, :, 0], kn_cmp)
    cv = ckv[:, :, 1]
    c_end = jnp.arange(nc) * CMP_STRIDE + (CMP_BLOCK - 1)
    nbs = -(-T // SEL_BLOCK)
    sel = jnp.pad(kv_rows[:, :, 2:4], ((0, 0), (0, nbs * SEL_BLOCK - T), (0, 0), (0, 0), (0, 0)))
    sel = sel.reshape(B, nbs, SEL_BLOCK, 2, NSA_KV_HEADS, HEAD_DIM).transpose(3, 0, 4, 1, 2, 5)
    return ck, cv, c_end, sel[0], sel[1]


def _j_nsa_attend(q, gates, q_pos, ck, cv, c_end, sk, sv, wk, wv, w_pos, rel_bias):
    B, Q = q.shape[:2]
    scale = HEAD_DIM ** -0.5
    tbl = rel_bias.astype(jnp.float32).T.reshape(NSA_KV_HEADS, NSA_GROUP, NUM_BUCKETS)
    qg = q.reshape(B, Q, NSA_KV_HEADS, NSA_GROUP, HEAD_DIM).transpose(0, 2, 3, 1, 4)
    t = q_pos[:, None]
    s = jnp.einsum('bkgqd,bnkd->bkgqn', qg, ck).astype(jnp.float32) * scale + tbl[:, :, _j_rel_bucket(t - c_end[None])]
    p_cmp = _j_masked_softmax(s, c_end[None] <= t)
    o_cmp = jnp.einsum('bkgqn,bnkd->bkgqd', p_cmp.astype(cv.dtype), cv)
    nbs = sk.shape[2]
    imp = jnp.sum(p_cmp, axis=2)
    nc = imp.shape[-1]
    imp = jnp.pad(imp, ((0, 0), (0, 0), (0, 0), (0, nbs * SEL_RATIO - nc)))
    imp = imp.reshape(B, NSA_KV_HEADS, Q, nbs, SEL_RATIO).sum(-1)
    blk = jnp.arange(nbs)[None]
    cur = (q_pos // SEL_BLOCK)[:, None]
    forced = (blk == 0) | (blk == cur) | (blk == cur - 1)
    score = jnp.where(blk > cur, -jnp.inf, jnp.where(forced, jnp.inf, imp))
    n_sel = min(SEL_TOPN, nbs)
    _, idx = lax.top_k(score, n_sel)
    pick = jax.vmap(jax.vmap(lambda kb, ix: kb[ix]))
    gk = pick(sk, idx).reshape(B, NSA_KV_HEADS, Q, n_sel * SEL_BLOCK, HEAD_DIM)
    gv = pick(sv, idx).reshape(B, NSA_KV_HEADS, Q, n_sel * SEL_BLOCK, HEAD_DIM)
    spos = (idx[..., None] * SEL_BLOCK + jnp.arange(SEL_BLOCK)).reshape(B, NSA_KV_HEADS, Q, n_sel * SEL_BLOCK)
    kidx = jnp.arange(NSA_KV_HEADS)[None, :, None, None, None]
    gidx = jnp.arange(NSA_GROUP)[None, None, :, None, None]
    bias = tbl[kidx, gidx, _j_rel_bucket(t - spos)[:, :, None]]
    s = jnp.einsum('bkgqd,bkqsd->bkgqs', qg, gk).astype(jnp.float32) * scale + bias
    p = _j_masked_softmax(s, (spos <= t)[:, :, None])
    o_sel = jnp.einsum('bkgqs,bkqsd->bkgqd', p.astype(gv.dtype), gv)
    dist = t - w_pos[None]
    s = jnp.einsum('bkgqd,bwkd->bkgqw', qg, wk).astype(jnp.float32) * scale + tbl[:, :, _j_rel_bucket(dist)]
    p = _j_masked_softmax(s, (dist >= 0) & (dist <= NSA_WINDOW) & (w_pos[None] >= 0))
    o_win = jnp.einsum('bkgqw,bwkd->bkgqd', p.astype(wv.dtype), wv)
    o = jnp.stack([o_cmp, o_sel, o_win], axis=-1).transpose(0, 3, 1, 2, 4, 5)
    o = o.reshape(B, Q, NSA_HEADS, HEAD_DIM, 3)
    return jnp.einsum('bqhdr,bqhr->bqhd', o, gates.astype(o.dtype)).reshape(B, Q, NSA_WIDTH)


def _j_nsa_prompt(q, gates, kv_rows, win_rows, pe, w1, w2, kn_cmp, rel_bias):
    B, T = q.shape[:2]
    QB = 128
    ck, cv, c_end, sk, sv = _j_nsa_prepare(kv_rows, pe, w1, w2, kn_cmp)
    win_pad = jnp.pad(win_rows, ((0, 0), (NSA_WINDOW, 0), (0, 0), (0, 0), (0, 0)))

    def block(i):
        q0 = i * QB
        qb = lax.dynamic_slice_in_dim(q, q0, QB, axis=1)
        gb = lax.dynamic_slice_in_dim(gates, q0, QB, axis=1)
        wb = lax.dynamic_slice_in_dim(win_pad, q0, NSA_WINDOW + QB, axis=1)
        q_pos = q0 + jnp.arange(QB)
        w_pos = q0 - NSA_WINDOW + jnp.arange(NSA_WINDOW + QB)
        return _j_nsa_attend(qb, gb, q_pos, ck, cv, c_end, sk, sv, wb[:, :, 0], wb[:, :, 1], w_pos, rel_bias)

    o = lax.map(block, jnp.arange(T // QB))
    return o.transpose(1, 0, 2, 3).reshape(B, T, NSA_WIDTH)


def _j_fox_attend(q, cq, q_pos, k, v, ck, k_pos):
    s = jnp.einsum('bqhd,bshd->bhqs', q, k).astype(jnp.float32) * HEAD_DIM ** -0.5
    s = s + jnp.swapaxes(cq, 1, 2)[..., :, None] - jnp.swapaxes(ck, 1, 2)[..., None, :]
    p = _j_masked_softmax(s, k_pos[None, :] <= q_pos[:, None])
    return jnp.einsum('bhqs,bshd->bqhd', p.astype(v.dtype), v)


def _pad_cols(w, width):
    return jnp.pad(w, ((0, 0), (0, width - w.shape[1])))


def _prep_even_w(w):
    gq, gk, gv, glr, gog, nq, nkv, ng = _split(w, EVEN_SIZES)
    return jnp.concatenate([gq, gk, gv, gog, nq, nkv, _pad_cols(glr, LANES), _pad_cols(ng, LANES)], axis=1).astype(BF16)


def _prep_odd_w(w):
    fq, fk, fv, ff, cg = _split(w, ODD_SIZES)
    return jnp.concatenate([fq, fk, fv, cg, _pad_cols(ff, LANES)], axis=1).astype(BF16)


def _row(v):
    return v.reshape(1, -1).astype(F32)


def _tile_row(v, reps):
    return jnp.tile(v.astype(F32), reps).reshape(1, -1)


def kernel(x_prompt, x_sample, cache_nsa_kv, state_nsa_win, state_gla, cache_fox_kv, cache_fox_logf, state_conv, cache_mem_kv, page_table, mem_prompt, rel_bias, norm_mix, norm_xattn, norm_ffn, even_w_in, even_w_out, gla_w_gate, gla_b_gate, gla_out_norm, nsa_q_norm, nsa_k_norm, nsa_cmp_pe, nsa_cmp_w1, nsa_cmp_w2, odd_w_in, odd_w_out, fox_q_norm, fox_k_norm, fox_b_f, conv_w, conv_b, conv_ln_g, conv_ln_b, mem_norm, xa_wq, xa_wkv, xa_wo, xa_q_norm, xa_k_norm, ffn_w_in, ffn_w_out):
    B, T, _ = x_prompt.shape
    SB, SQ, _ = x_sample.shape
    depth = norm_mix.shape[0]
    past_len = page_table.shape[1] * cache_nsa_kv.shape[2]
    n_win = state_nsa_win.shape[2]
    dec_pos = past_len + jnp.arange(SQ)
    MP = B * T
    SR = SAMPLE_ROWS
    MS = SB * SR

    yp = x_prompt.reshape(MP, D_MODEL)
    ys = jnp.pad(x_sample, ((0, 0), (0, SR - SQ), (0, 0))).reshape(MS, D_MODEL)

    nsa_kv_p, nsa_kv_s, win_p, win_s, gla_p, gla_s = [], [], [], [], [], []
    fox_kv_p, fox_kv_s, logf_p, logf_s, conv_p, conv_s, mem_kv_p = [], [], [], [], [], [], []

    for layer in range(depth):
        if layer % 2 == 0:
            e = layer // 2
            w_pad = _prep_even_w(even_w_in[e])
            wg_pad = jnp.pad(gla_w_gate[e], ((0, LANES - GLA_RANK), (0, 0))).astype(BF16)
            bg = _row(gla_b_gate[e])
            qn = _tile_row(nsa_q_norm[e], NSA_HEADS)
            kn1 = _tile_row(nsa_k_norm[e, 1], NSA_KV_HEADS)
            kn2 = _tile_row(nsa_k_norm[e, 2], NSA_KV_HEADS)
            gn = _row(gla_out_norm[e])
            w_out = even_w_out[e].astype(BF16)
            g_mix = _row(norm_mix[layer])
            (q, k, v, la, og, nqs, kvr, wr, gt, selkv, winkv) = _even_in(yp, g_mix, w_pad, wg_pad, bg, qn, kn1, kn2, 256)
            r3 = lambda a: a.reshape(B, T, a.shape[-1])
            s0t = jnp.zeros((B, GLA_HEADS, GLA_DV, GLA_DK), F32)
            o_gla, sfin_t = _gla(r3(q), r3(k), r3(v), r3(la), r3(og), gn, s0t, 256)
            nq_f = (nqs.astype(F32) * (1.0 / SCALE)).reshape(B, T, NSA_HEADS, HEAD_DIM)
            gates = gt[:, :24].reshape(B, T, NSA_HEADS, 3)
            kvr5 = kvr.reshape(B, T, 4, NSA_KV_HEADS, HEAD_DIM)
            wr5 = wr.reshape(B, T, 2, NSA_KV_HEADS, HEAD_DIM)
            o_nsa = _j_nsa_prompt(nq_f, gates, kvr5, wr5, nsa_cmp_pe[e], nsa_cmp_w1[e], nsa_cmp_w2[e],
                                  nsa_k_norm[e][0], rel_bias)
            yp = _out_proj(yp, o_gla.reshape(MP, GLA_WIDTH), o_nsa.reshape(MP, NSA_WIDTH),
                           w_out[:GLA_WIDTH], w_out[GLA_WIDTH:], 512)
            nsa_kv_p.append(kvr5)
            win_p.append(wr5[:, -min(NSA_WINDOW, T):])
            gla_p.append(jnp.swapaxes(sfin_t, -1, -2))
            (q, k, v, la, og, nqs, kvr, wr, gt, selkv, winkv) = _even_in(ys, g_mix, w_pad, wg_pad, bg, qn, kn1, kn2, MS)
            pad16 = lambda a: jnp.pad(a.reshape(SB, SR, a.shape[-1]), ((0, 0), (0, SUB - SR), (0, 0)))
            s0t = jnp.swapaxes(state_gla[e], -1, -2)
            o_gla, snew_t = _gla(pad16(q), pad16(k), pad16(v), pad16(la), pad16(og), gn, s0t, SUB, n_valid=SQ)
            o_gla = o_gla[:, :SR]
            nq_f = (nqs.astype(F32) * (1.0 / SCALE)).reshape(SB, SR, NSA_HEADS, HEAD_DIM)[:, :SQ]
            gates = gt[:, :24].reshape(SB, SR, NSA_HEADS, 3)[:, :SQ]
            kvr5 = kvr.reshape(SB, SR, 4, NSA_KV_HEADS, HEAD_DIM)[:, :SQ]
            wr5 = wr.reshape(SB, SR, 2, NSA_KV_HEADS, HEAD_DIM)[:, :SQ]
            full = jnp.concatenate([_j_gather_pages(cache_nsa_kv[e], page_table), kvr5], axis=1)
            wfull = jnp.concatenate([state_nsa_win[e], wr5], axis=1)
            w_pos = past_len - n_win + jnp.arange(wfull.shape[1])
            ck, cv, c_end, sk, sv = _j_nsa_prepare(full, nsa_cmp_pe[e], nsa_cmp_w1[e], nsa_cmp_w2[e], nsa_k_norm[e][0])
            o_nsa = _j_nsa_attend(nq_f, gates, dec_pos, ck, cv, c_end, sk, sv, wfull[:, :, 0], wfull[:, :, 1], w_pos, rel_bias)
            o_nsa = jnp.pad(o_nsa, ((0, 0), (0, SR - SQ), (0, 0)))
            ys = _out_proj(ys, o_gla.reshape(MS, GLA_WIDTH), o_nsa.reshape(MS, NSA_WIDTH),
                           w_out[:GLA_WIDTH], w_out[GLA_WIDTH:], MS)
            nsa_kv_s.append(kvr5)
            win_s.append(wfull[:, -n_win:])
            gla_s.append(jnp.swapaxes(snew_t, -1, -2))
        else:
            j = layer // 2
            w_pad = _prep_odd_w(odd_w_in[j])
            qn = _tile_row(fox_q_norm[j], FOX_HEADS)
            kn = _tile_row(fox_k_norm[j], FOX_HEADS)
            bf_pad = jnp.pad(fox_b_f[j].astype(F32), (0, LANES - FOX_HEADS)).reshape(1, LANES)
            w_out = odd_w_out[j].astype(BF16)
            g_mix = _row(norm_mix[layer])
            cw = conv_w[j].astype(F32)
            cb, cg_, cbeta = _row(conv_b[j]), _row(conv_ln_g[j]), _row(conv_ln_b[j])
            qs, kv, kb, vb, lf, c, u = _odd_in(yp.reshape(B, T, D_MODEL), g_mix, w_pad, qn, kn, bf_pad, 256)
            c8 = c[:, :, :FOX_HEADS]
            o_fox = _fox_prompt(qs, kb, vb, c8, jnp.swapaxes(c8, 1, 2), 512, 1024)
            o_conv, cst = _conv(u, jnp.zeros((B, CONV_WIDTH - 1, CONV_CH), F32), cw, cb, cg_, cbeta, 512, 512)
            yp = _out_proj(yp, o_fox.reshape(MP, FOX_WIDTH), o_conv.reshape(MP, CONV_CH),
                           w_out[:FOX_WIDTH], w_out[FOX_WIDTH:], 512)
            fox_kv_p.append(kv.reshape(B, T, 2, FOX_HEADS, HEAD_DIM))
            logf_p.append(lf[:, :, :FOX_HEADS])
            conv_p.append(cst)
            qs, kv, kb, vb, lf, c, u = _odd_in(ys.reshape(1, MS, D_MODEL), g_mix, w_pad, qn, kn, bf_pad, MS)
            q_f = (qs.astype(F32) * (1.0 / SCALE)).reshape(SB, SR, FOX_HEADS, HEAD_DIM)[:, :SQ]
            new_kv = kv.reshape(SB, SR, 2, FOX_HEADS, HEAD_DIM)[:, :SQ]
            lf_new = lf.reshape(SB, SR, LANES)[:, :SQ, :FOX_HEADS]
            kv_full = jnp.concatenate([_j_gather_pages(cache_fox_kv[j], page_table), new_kv], axis=1)
            lf_full = jnp.concatenate([_j_gather_pages(cache_fox_logf[j], page_table), lf_new], axis=1)
            c_full = jnp.cumsum(lf_full, axis=1)
            o_fox = _j_fox_attend(q_f, c_full[:, past_len:], dec_pos, kv_full[:, :, 0], kv_full[:, :, 1], c_full,
                                  jnp.arange(kv_full.shape[1]))
            o_fox = jnp.pad(o_fox.reshape(SB, SQ, FOX_WIDTH), ((0, 0), (0, SR - SQ), (0, 0)))
            o_conv, cst = _conv(u.reshape(SB, SR, CONV_CH), state_conv[j], cw, cb, cg_, cbeta, SR, SQ)
            ys = _out_proj(ys, o_fox.reshape(MS, FOX_WIDTH), o_conv.reshape(MS, CONV_CH),
                           w_out[:FOX_WIDTH], w_out[FOX_WIDTH:], MS)
            fox_kv_s.append(new_kv)
            logf_s.append(lf_new)
            conv_s.append(cst)
        g_xa = _row(norm_xattn[layer])
        wq = xa_wq[layer].astype(BF16)
        wo = xa_wo[layer].astype(BF16)
        xqn = _tile_row(xa_q_norm[layer], XA_HEADS)
        mkv = _mem_kv(mem_prompt, _row(mem_norm[layer]), xa_wkv[layer].astype(BF16), _tile_row(xa_k_norm[layer], XA_HEADS))
        mem_kv_p.append(mkv.reshape(B, MEM_LEN, 2, XA_HEADS, HEAD_DIM))
        yp = _xattn(yp.reshape(B, T, D_MODEL), mkv, g_xa, wq, wo, xqn, 512).reshape(MP, D_MODEL)
        ys = _xattn(ys.reshape(SB, SR, D_MODEL), cache_mem_kv[layer].reshape(SB, MEM_LEN, 2 * XA_WIDTH),
                    g_xa, wq, wo, xqn, SR).reshape(MS, D_MODEL)
        g_ffn = _row(norm_ffn[layer])
        w_in = ffn_w_in[layer].astype(BF16)
        w_o = ffn_w_out[layer].astype(BF16)
        yp = _ffn(yp, g_ffn, w_in, w_o, 512)
        ys = _ffn(ys, g_ffn, w_in, w_o, MS)

    yp = yp.reshape(B, T, D_MODEL)
    ys = ys.reshape(SB, SR, D_MODEL)[:, :SQ]
    return (yp, ys,
            jnp.stack(nsa_kv_p), jnp.stack(nsa_kv_s), jnp.stack(win_p), jnp.stack(win_s),
            jnp.stack(gla_p), jnp.stack(gla_s), jnp.stack(fox_kv_p), jnp.stack(fox_kv_s),
            jnp.stack(logf_p), jnp.stack(logf_s), jnp.stack(conv_p), jnp.stack(conv_s),
            jnp.stack(mem_kv_p))
```

```python
import functools
import math

import jax
import jax.numpy as jnp
import numpy as np
from jax import lax
from jax.experimental import pallas as pl
from jax.experimental.pallas import tpu as pltpu

F32 = jnp.float32
BF16 = jnp.bfloat16

D_MODEL = 1024
HEAD_DIM = 64
GLA_WIDTH = 512
GLA_HEADS = 4
GLA_DV = 128
GLA_DK = 64
GLA_RANK = 16
GLA_TAU = 16.0
NSA_WIDTH = 512
NSA_HEADS = 8
NSA_KV_HEADS = 2
NSA_GROUP = 4
CMP_STRIDE = 16
CMP_BLOCK = 32
CMP_HIDDEN = 256
SEL_BLOCK = 64
SEL_RATIO = 4
SEL_TOPN = 16
NSA_WINDOW = 512
FOX_WIDTH = 512
FOX_HEADS = 8
CONV_CH = 512
CONV_WIDTH = 31
MEM_LEN = 256
XA_HEADS = 4
XA_WIDTH = 256
FFN_HIDDEN = 2816
NUM_BUCKETS = 32
MAX_DISTANCE = 128
EPS = 1e-6
SCALE = HEAD_DIM ** -0.5
NEG = -1e30

EVEN_SIZES = (256, 256, 512, 16, 512, 512, 768, 24)
ODD_SIZES = (512, 512, 512, 8, 1024)

LANES = 128
VMEM_LIMIT_BYTES = 56 * 1024 * 1024
SAMPLE_ROWS = 8
SUB = 16
N_CMP = 512
N_SELBLK = 128


def _cparams(sem, vmem=None):
    return pltpu.CompilerParams(dimension_semantics=sem, vmem_limit_bytes=vmem)


def _split(h, sizes):
    return jnp.split(h, np.cumsum(sizes)[:-1].tolist(), axis=-1)


def _rms_rows(x, g):
    return x * lax.rsqrt(jnp.mean(x * x, axis=-1, keepdims=True) + EPS) * g


def _group_rms(x, gmat, gs):
    x2 = x * x
    hi = x2.astype(BF16)
    lo = (x2 - hi.astype(F32)).astype(BF16)
    ms = (jnp.dot(hi, gmat, preferred_element_type=F32) + jnp.dot(lo, gmat, preferred_element_type=F32)) * (1.0 / gs)
    return x * lax.rsqrt(ms + EPS)


def _log_sigmoid(z):
    return -(jnp.maximum(-z, 0.0) + jnp.log1p(jnp.exp(-jnp.abs(z))))


def _sigmoid(z):
    return 1.0 / (1.0 + jnp.exp(-z))


def _dot_nt(a, b):
    return lax.dot_general(a, b, (((1,), (1,)), ((), ())), preferred_element_type=F32)


def _dot_tn(a, b):
    return lax.dot_general(a, b, (((0,), (0,)), ((), ())), preferred_element_type=F32)


def _dot_f32(a, b):
    return jnp.dot(a, b, preferred_element_type=F32, precision=lax.Precision.HIGHEST)


def _block_ones(width, gs):
    r = np.arange(width) // gs
    return jnp.asarray((r[:, None] == r[None, :]).astype(np.float32), dtype=BF16)


def _even_in_kernel(x_ref, g_ref, w_ref, wg_ref, bg_ref, qn_ref, kn1_ref, kn2_ref, gm512_ref, gm128_ref,
                    q_ref, k_ref, v_ref, la_ref, og_ref, nqs_ref, kvr_ref, wr_ref, gt_ref, selkv_ref, winkv_ref):
    xb = _rms_rows(x_ref[...], g_ref[...]).astype(BF16)

    def proj(lo, hi):
        return jnp.dot(xb, w_ref[:, lo:hi], preferred_element_type=F32)

    q_ref[...] = proj(0, 256) * (GLA_DK ** -0.5)
    k_ref[...] = proj(256, 512)
    v_ref[...] = proj(512, 1024)
    og = proj(1024, 1536)
    og_ref[...] = og * _sigmoid(og)
    nq = _group_rms(proj(1536, 2048), gm512_ref[...], HEAD_DIM) * qn_ref[...]
    nqs_ref[...] = (nq * SCALE).astype(BF16)
    kvr_ref[:, 0:256] = proj(2048, 2304)
    selk = _group_rms(proj(2304, 2432), gm128_ref[...], HEAD_DIM) * kn1_ref[...]
    selv = proj(2432, 2560)
    kvr_ref[:, 256:384] = selk
    kvr_ref[:, 384:512] = selv
    selkv_ref[:, 0:128] = selk.astype(BF16)
    selkv_ref[:, 128:256] = selv.astype(BF16)
    wink = _group_rms(proj(2560, 2688), gm128_ref[...], HEAD_DIM) * kn2_ref[...]
    winv = proj(2688, 2816)
    wr_ref[:, 0:128] = wink
    wr_ref[:, 128:256] = winv
    winkv_ref[:, 0:128] = wink.astype(BF16)
    winkv_ref[:, 128:256] = winv.astype(BF16)
    glr = proj(2816, 2944).astype(BF16)
    z = jnp.dot(glr, wg_ref[...], preferred_element_type=F32) + bg_ref[...]
    la_ref[...] = _log_sigmoid(z) * (1.0 / GLA_TAU)
    gt_ref[...] = _sigmoid(proj(2944, 3072))


def _even_in(x2d, g, w_pad, wg_pad, bg, qn, kn1, kn2, tm):
    m = x2d.shape[0]
    widths = (256, 256, 512, 256, 512, 512, 512, 256, 128, 256, 256)
    dtypes = (F32, F32, F32, F32, F32, BF16, F32, F32, F32, BF16, BF16)
    full = lambda a: pl.BlockSpec(a.shape, lambda i: (0,) * a.ndim)
    gm512 = _block_ones(512, HEAD_DIM)
    gm128 = _block_ones(128, HEAD_DIM)
    ins = (x2d, g, w_pad, wg_pad, bg, qn, kn1, kn2, gm512, gm128)
    return pl.pallas_call(
        _even_in_kernel,
        grid=(m // tm,),
        in_specs=[pl.BlockSpec((tm, D_MODEL), lambda i: (i, 0))] + [full(a) for a in ins[1:]],
        out_specs=[pl.BlockSpec((tm, w), lambda i: (i, 0)) for w in widths],
        out_shape=[jax.ShapeDtypeStruct((m, w), d) for w, d in zip(widths, dtypes)],
        compiler_params=_cparams(("arbitrary",), VMEM_LIMIT_BYTES),
        name="even_in",
    )(*ins)


def _odd_in_kernel(x_ref, g_ref, w_ref, qn_ref, kn_ref, bf_ref, gm512_ref, tri_ref,
                   qs_ref, kv_ref, kb_ref, vb_ref, lf_ref, c_ref, u_ref, carry_ref):
    @pl.when(pl.program_id(1) == 0)
    def _():
        carry_ref[...] = jnp.zeros_like(carry_ref)

    xb = _rms_rows(x_ref[...], g_ref[...]).astype(BF16)

    def proj(lo, hi):
        return jnp.dot(xb, w_ref[:, lo:hi], preferred_element_type=F32)

    q = _group_rms(proj(0, 512), gm512_ref[...], HEAD_DIM) * qn_ref[...]
    qs_ref[...] = (q * SCALE).astype(BF16)
    k = _group_rms(proj(512, 1024), gm512_ref[...], HEAD_DIM) * kn_ref[...]
    v = proj(1024, 1536)
    kv_ref[:, 0:512] = k
    kv_ref[:, 512:1024] = v
    kb_ref[...] = k.astype(BF16)
    vb_ref[...] = v.astype(BF16)
    u_ref[...] = proj(1536, 2048) * _sigmoid(proj(2048, 2560))
    lf = _log_sigmoid(proj(2560, 2688) + bf_ref[...])
    lf_ref[...] = lf
    c = _dot_f32(tri_ref[...], lf) + carry_ref[0:1, :]
    c_ref[...] = c
    carry_ref[0:1, :] = c[-1:, :]


def _odd_in(x3d, g, w_pad, qn, kn, bf_pad, tm):
    b, t, _ = x3d.shape
    widths = (512, 1024, 512, 512, 128, 128, 512)
    dtypes = (BF16, F32, BF16, BF16, F32, F32, F32)
    gm512 = _block_ones(512, HEAD_DIM)
    tri = jnp.asarray(np.tril(np.ones((tm, tm), np.float32)))
    ins = (x3d, g, w_pad, qn, kn, bf_pad, gm512, tri)
    full = lambda a: pl.BlockSpec(a.shape, lambda bi, i: (0,) * a.ndim)
    return pl.pallas_call(
        _odd_in_kernel,
        grid=(b, t // tm),
        in_specs=[pl.BlockSpec((None, tm, D_MODEL), lambda bi, i: (bi, i, 0))] + [full(a) for a in ins[1:]],
        out_specs=[pl.BlockSpec((None, tm, w), lambda bi, i: (bi, i, 0)) for w in widths],
        out_shape=[jax.ShapeDtypeStruct((b, t, w), d) for w, d in zip(widths, dtypes)],
        scratch_shapes=[pltpu.VMEM((8, 128), F32)],
        compiler_params=_cparams(("arbitrary", "arbitrary"), VMEM_LIMIT_BYTES),
        name="odd_in",
    )(*ins)


def _out_proj_kernel(res_ref, a1_ref, a2_ref, w1_ref, w2_ref, o_ref):
    acc = jnp.dot(a1_ref[...].astype(BF16), w1_ref[...], preferred_element_type=F32)
    acc = acc + jnp.dot(a2_ref[...].astype(BF16), w2_ref[...], preferred_element_type=F32)
    o_ref[...] = res_ref[...] + acc


def _out_proj(res, a1, a2, w1, w2, tm):
    m = res.shape[0]
    row = lambda a: pl.BlockSpec((tm, a.shape[1]), lambda i: (i, 0))
    full = lambda a: pl.BlockSpec(a.shape, lambda i: (0, 0))
    return pl.pallas_call(
        _out_proj_kernel,
        grid=(m // tm,),
        in_specs=[row(res), row(a1), row(a2), full(w1), full(w2)],
        out_specs=row(res),
        out_shape=jax.ShapeDtypeStruct(res.shape, F32),
        compiler_params=_cparams(("arbitrary",), VMEM_LIMIT_BYTES),
        name="out_proj",
    )(res, a1, a2, w1, w2)


def _ffn_kernel(x_ref, g_ref, wg_ref, wu_ref, wo_ref, o_ref, xn_ref, acc_ref):
    j = pl.program_id(1)

    @pl.when(j == 0)
    def _():
        xn_ref[...] = _rms_rows(x_ref[...], g_ref[...]).astype(BF16)
        acc_ref[...] = jnp.zeros_like(acc_ref)

    xb = xn_ref[...]
    gate = jnp.dot(xb, wg_ref[...], preferred_element_type=F32)
    up = jnp.dot(xb, wu_ref[...], preferred_element_type=F32)
    h = (gate * _sigmoid(gate) * up).astype(BF16)
    acc_ref[...] += jnp.dot(h, wo_ref[...], preferred_element_type=F32)

    @pl.when(j == pl.num_programs(1) - 1)
    def _():
        o_ref[...] = x_ref[...] + acc_ref[...]


def _ffn(x2d, g, w_in, w_out, tm, n_chunks=2):
    m = x2d.shape[0]
    th = FFN_HIDDEN // n_chunks
    return pl.pallas_call(
        _ffn_kernel,
        grid=(m // tm, n_chunks),
        in_specs=[pl.BlockSpec((tm, D_MODEL), lambda i, j: (i, 0)),
                  pl.BlockSpec((1, D_MODEL), lambda i, j: (0, 0)),
                  pl.BlockSpec((D_MODEL, th), lambda i, j: (0, j)),
                  pl.BlockSpec((D_MODEL, th), lambda i, j: (0, n_chunks + j)),
                  pl.BlockSpec((th, D_MODEL), lambda i, j: (j, 0))],
        out_specs=pl.BlockSpec((tm, D_MODEL), lambda i, j: (i, 0)),
        out_shape=jax.ShapeDtypeStruct(x2d.shape, F32),
        scratch_shapes=[pltpu.VMEM((tm, D_MODEL), BF16), pltpu.VMEM((tm, D_MODEL), F32)],
        compiler_params=_cparams(("arbitrary", "arbitrary"), VMEM_LIMIT_BYTES),
        name="ffn",
    )(x2d, g, w_in, w_in, w_out)


def _mem_kv_kernel(m_ref, g_ref, w_ref, kn_ref, gm_ref, o_ref):
    xb = _rms_rows(m_ref[...], g_ref[...]).astype(BF16)
    kv = jnp.dot(xb, w_ref[...], preferred_element_type=F32)
    o_ref[:, 0:XA_WIDTH] = _group_rms(kv[:, 0:XA_WIDTH], gm_ref[...], HEAD_DIM) * kn_ref[...]
    o_ref[:, XA_WIDTH:] = kv[:, XA_WIDTH:]


def _mem_kv(mem, g, wkv, kn):
    b = mem.shape[0]
    gm = _block_ones(XA_WIDTH, HEAD_DIM)
    full = lambda a: pl.BlockSpec(a.shape, lambda i: (0,) * a.ndim)
    return pl.pallas_call(
        _mem_kv_kernel,
        grid=(b,),
        in_specs=[pl.BlockSpec((None, MEM_LEN, D_MODEL), lambda i: (i, 0, 0)), full(g), full(wkv), full(kn), full(gm)],
        out_specs=pl.BlockSpec((None, MEM_LEN, 2 * XA_WIDTH), lambda i: (i, 0, 0)),
        out_shape=jax.ShapeDtypeStruct((b, MEM_LEN, 2 * XA_WIDTH), F32),
        compiler_params=_cparams(("arbitrary",)),
        name="mem_kv",
    )(mem, g, wkv, kn, gm)


def _xattn_kernel(x_ref, mkv_ref, g_ref, wq_ref, wo_ref, qn_ref, gm_ref, o_ref):
    x = x_ref[...]
    xb = _rms_rows(x, g_ref[...]).astype(BF16)
    q = jnp.dot(xb, wq_ref[...], preferred_element_type=F32)
    q = _group_rms(q, gm_ref[...], HEAD_DIM) * qn_ref[...]
    qb = (q * SCALE).astype(BF16)
    outs = []
    for h in range(XA_HEADS):
        kh = mkv_ref[:, h * HEAD_DIM:(h + 1) * HEAD_DIM].astype(BF16)
        vh = mkv_ref[:, XA_WIDTH + h * HEAD_DIM:XA_WIDTH + (h + 1) * HEAD_DIM].astype(BF16)
        s = _dot_nt(qb[:, h * HEAD_DIM:(h + 1) * HEAD_DIM], kh)
        e = jnp.exp(s - jnp.max(s, axis=-1, keepdims=True))
        p = (e / jnp.sum(e, axis=-1, keepdims=True)).astype(BF16)
        outs.append(jnp.dot(p, vh, preferred_element_type=F32))
    o = jnp.concatenate(outs, axis=-1).astype(BF16)
    o_ref[...] = x + jnp.dot(o, wo_ref[...], preferred_element_type=F32)


def _xattn(x3d, mkv, g, wq, wo, qn, tm):
    b, t, _ = x3d.shape
    gm = _block_ones(XA_WIDTH, HEAD_DIM)
    full = lambda a: pl.BlockSpec(a.shape, lambda bi, i: (0,) * a.ndim)
    return pl.pallas_call(
        _xattn_kernel,
        grid=(b, t // tm),
        in_specs=[pl.BlockSpec((None, tm, D_MODEL), lambda bi, i: (bi, i, 0)),
                  pl.BlockSpec((None, MEM_LEN, 2 * XA_WIDTH), lambda bi, i: (bi, 0, 0)),
                  full(g), full(wq), full(wo), full(qn), full(gm)],
        out_specs=pl.BlockSpec((None, tm, D_MODEL), lambda bi, i: (bi, i, 0)),
        out_shape=jax.ShapeDtypeStruct(x3d.shape, F32),
        compiler_params=_cparams(("arbitrary", "arbitrary"), VMEM_LIMIT_BYTES),
        name="xattn",
    )(x3d, mkv, g, wq, wo, qn, gm)


def _gla_kernel(q_ref, k_ref, v_ref, la_ref, og_ref, gn_ref, s0_ref, tri_ref, hsel_ref,
                o_ref, sfin_ref, st_ref, *, n_sub, n_valid):
    ti = pl.program_id(1)

    @pl.when(ti == 0)
    def _():
        st_ref[...] = s0_ref[...]

    tri = tri_ref[...]
    hsel = hsel_ref[...]
    gn = gn_ref[...]
    row = lax.broadcasted_iota(jnp.int32, (SUB, 1), 0)

    def sub_block(i, carry):
        r0 = pl.multiple_of(i * SUB, SUB)
        rows = pl.ds(r0, SUB)
        q = q_ref[rows, :]
        k = k_ref[rows, :]
        v = v_ref[rows, :]
        la = la_ref[rows, :]
        if n_valid is not None:
            live = (row + r0) < n_valid
            la = jnp.where(live, la, 0.0)
            k = jnp.where(live, k, 0.0)
        b = _dot_f32(tri, la)
        b_end = b[SUB - 1:SUB, :]
        qd = (q * jnp.exp(b)).astype(BF16)
        kd = (k * jnp.exp(b_end - b)).astype(BF16)
        vb = v.astype(BF16)
        tiles = []
        for s in range(SUB):
            e = jnp.exp(jnp.minimum(b - b[s:s + 1, :], 0.0))
            z = (q * k[s:s + 1, :]) * e
            tiles.append(jnp.where(row >= s, z, 0.0))
        att = jnp.dot(jnp.concatenate(tiles, axis=0).astype(BF16), hsel, preferred_element_type=F32)
        dec = jnp.exp(b_end)
        outs = []
        for h in range(GLA_HEADS):
            dk = slice(h * GLA_DK, (h + 1) * GLA_DK)
            dv = slice(h * GLA_DV, (h + 1) * GLA_DV)
            st = st_ref[h]
            o = _dot_nt(qd[:, dk], st.astype(BF16))
            for s in range(SUB):
                o = o + att[s * SUB:(s + 1) * SUB, h:h + 1] * v[s:s + 1, dv]
            st_ref[h] = st * dec[:, dk] + _dot_tn(vb[:, dv], kd[:, dk])
            outs.append(_rms_rows(o, gn))
        o_ref[rows, :] = jnp.concatenate(outs, axis=-1) * og_ref[rows, :]
        return carry

    lax.fori_loop(0, n_sub, sub_block, 0)

    @pl.when(ti == pl.num_programs(1) - 1)
    def _():
        sfin_ref[...] = st_ref[...]


def _gla(q, k, v, la, og, gn, s0t, tt, n_valid=None):
    b, t, _ = q.shape
    tri = jnp.asarray(np.tril(np.ones((SUB, SUB), np.float32)))
    hsel = jnp.asarray((np.arange(256)[:, None] // GLA_DK == np.arange(128)[None, :]).astype(np.float32), dtype=BF16)
    seq = lambda w: pl.BlockSpec((None, tt, w), lambda bi, i: (bi, i, 0))
    full = lambda a: pl.BlockSpec(a.shape, lambda bi, i: (0,) * a.ndim)
    st_spec = pl.BlockSpec((None, GLA_HEADS, GLA_DV, GLA_DK), lambda bi, i: (bi, 0, 0, 0))
    return pl.pallas_call(
        functools.partial(_gla_kernel, n_sub=tt // SUB, n_valid=n_valid),
        grid=(b, t // tt),
        in_specs=[seq(256), seq(256), seq(512), seq(256), seq(512), full(gn), st_spec, full(tri), full(hsel)],
        out_specs=[seq(512), st_spec],
        out_shape=[jax.ShapeDtypeStruct((b, t, GLA_WIDTH), F32),
                   jax.ShapeDtypeStruct((b, GLA_HEADS, GLA_DV, GLA_DK), F32)],
        scratch_shapes=[pltpu.VMEM((GLA_HEADS, GLA_DV, GLA_DK), F32)],
        compiler_params=_cparams(("arbitrary", "arbitrary")),
        name="gla",
    )(q, k, v, la, og, gn, s0t, tri, hsel)


def _conv_kernel(u_ref, st0_ref, w_ref, b_ref, g_ref, beta_ref, o_ref, st_ref, ext_ref, *, tt, n_valid):
    ti = pl.program_id(1)
    ctx = CONV_WIDTH - 1

    @pl.when(ti == 0)
    def _():
        ext_ref[0:8, :] = jnp.zeros((8, CONV_CH), F32)
        ext_ref[pl.ds(2, ctx), :] = st0_ref[...]

    ext_ref[pl.ds(32, tt), :] = u_ref[...]
    acc = jnp.zeros((tt, CONV_CH), F32)
    for w in range(CONV_WIDTH):
        acc = acc + ext_ref[pl.ds(2 + w, tt), :] * w_ref[w:w + 1, :]
    y = acc + b_ref[...]
    mu = jnp.mean(y, axis=-1, keepdims=True)
    var = jnp.mean(jnp.square(y - mu), axis=-1, keepdims=True)
    ln = (y - mu) * lax.rsqrt(var + EPS) * g_ref[...] + beta_ref[...]
    o_ref[...] = ln * _sigmoid(ln)

    @pl.when(ti == pl.num_programs(1) - 1)
    def _():
        st_ref[...] = ext_ref[pl.ds(32 + n_valid - ctx, ctx), :]

    ext_ref[0:32, :] = ext_ref[pl.ds(tt, 32), :]


def _conv(u, st0, w, b, g, beta, tt, n_valid):
    bsz, t, _ = u.shape
    ctx = CONV_WIDTH - 1
    full = lambda a: pl.BlockSpec(a.shape, lambda bi, i: (0,) * a.ndim)
    st_spec = pl.BlockSpec((None, ctx, CONV_CH), lambda bi, i: (bi, 0, 0))
    return pl.pallas_call(
        functools.partial(_conv_kernel, tt=tt, n_valid=n_valid),
        grid=(bsz, t // tt),
        in_specs=[pl.BlockSpec((None, tt, CONV_CH), lambda bi, i: (bi, i, 0)), st_spec,
                  full(w), full(b), full(g), full(beta)],
        out_specs=[pl.BlockSpec((None, tt, CONV_CH), lambda bi, i: (bi, i, 0)), st_spec],
        out_shape=[jax.ShapeDtypeStruct(u.shape, F32), jax.ShapeDtypeStruct((bsz, ctx, CONV_CH), F32)],
        scratch_shapes=[pltpu.VMEM((32 + max(tt, 32), CONV_CH), F32)],
        compiler_params=_cparams(("arbitrary", "arbitrary")),
        name="conv",
    )(u, st0, w, b, g, beta)


def _fox_kernel(q_ref, k_ref, v_ref, cq_ref, ckt_ref, o_ref, m_ref, l_ref, acc_ref, *, tq, tk):
    qi = pl.program_id(1)
    ki = pl.program_id(2)
    last = (qi * tq + tq - 1) // tk

    @pl.when(ki == 0)
    def _():
        m_ref[...] = jnp.full_like(m_ref, NEG)
        l_ref[...] = jnp.zeros_like(l_ref)
        acc_ref[...] = jnp.zeros_like(acc_ref)

    @pl.when(ki <= last)
    def _():
        t_pos = qi * tq + lax.broadcasted_iota(jnp.int32, (tq, tk), 0)
        s_pos = ki * tk + lax.broadcasted_iota(jnp.int32, (tq, tk), 1)
        mask = s_pos <= t_pos
        for h in range(FOX_HEADS):
            cols = slice(h * HEAD_DIM, (h + 1) * HEAD_DIM)
            s = _dot_nt(q_ref[:, cols], k_ref[:, cols])
            s = s + cq_ref[:, h:h + 1] - ckt_ref[h:h + 1, :]
            s = jnp.where(mask, s, NEG)
            m_old = m_ref[h]
            m_new = jnp.maximum(m_old, jnp.max(s, axis=-1, keepdims=True))
            p = jnp.where(mask, jnp.exp(s - m_new), 0.0)
            alpha = jnp.exp(m_old - m_new)
            l_ref[h] = alpha * l_ref[h] + jnp.sum(p, axis=-1, keepdims=True)
            acc_ref[:, cols] = alpha * acc_ref[:, cols] + jnp.dot(p.astype(BF16), v_ref[:, cols],
                                                                  preferred_element_type=F32)
            m_ref[h] = m_new

    @pl.when(ki == pl.num_programs(2) - 1)
    def _():
        for h in range(FOX_HEADS):
            cols = slice(h * HEAD_DIM, (h + 1) * HEAD_DIM)
            o_ref[:, cols] = acc_ref[:, cols] / jnp.maximum(l_ref[h], 1e-30)


def _fox_prompt(qs, kb, vb, c8, ct, tq, tk):
    b, t, _ = qs.shape
    kv_map = lambda bi, qi, ki: (bi, jnp.minimum(ki, (qi * tq + tq - 1) // tk), 0)
    return pl.pallas_call(
        functools.partial(_fox_kernel, tq=tq, tk=tk),
        grid=(b, t // tq, t // tk),
        in_specs=[pl.BlockSpec((None, tq, FOX_WIDTH), lambda bi, qi, ki: (bi, qi, 0)),
                  pl.BlockSpec((None, tk, FOX_WIDTH), kv_map),
                  pl.BlockSpec((None, tk, FOX_WIDTH), kv_map),
                  pl.BlockSpec((None, tq, FOX_HEADS), lambda bi, qi, ki: (bi, qi, 0)),
                  pl.BlockSpec((None, FOX_HEADS, tk),
                               lambda bi, qi, ki: (bi, 0, jnp.minimum(ki, (qi * tq + tq - 1) // tk)))],
        out_specs=pl.BlockSpec((None, tq, FOX_WIDTH), lambda bi, qi, ki: (bi, qi, 0)),
        out_shape=jax.ShapeDtypeStruct((b, t, FOX_WIDTH), F32),
        scratch_shapes=[pltpu.VMEM((FOX_HEADS, tq, 1), F32), pltpu.VMEM((FOX_HEADS, tq, 1), F32),
                        pltpu.VMEM((tq, FOX_WIDTH), F32)],
        compiler_params=_cparams(("arbitrary", "arbitrary", "arbitrary"), VMEM_LIMIT_BYTES),
        name="fox_prompt",
    )(qs, kb, vb, c8, ct)


def _compress_compute(src_refs, pe_ref, w1_ref, w2_ref, kn_ref, ck_ref, cv_ref, sh_ref, nseg):
    sh_ref[pl.ds(nseg, 8), :] = jnp.zeros((8, CMP_HIDDEN), F32)
    for j, src_ref in enumerate(src_refs):
        a = [jnp.zeros((nseg, CMP_HIDDEN), F32) for _ in range(NSA_KV_HEADS)]
        bm = [jnp.zeros((nseg, CMP_HIDDEN), F32) for _ in range(NSA_KV_HEADS)]
        for p in range(CMP_STRIDE):
            p2 = CMP_STRIDE + p
            x2 = src_ref[pl.ds(p, nseg, stride=CMP_STRIDE), :]
            wa = w1_ref[j, p * HEAD_DIM:(p + 1) * HEAD_DIM, :]
            wb = w1_ref[j, p2 * HEAD_DIM:(p2 + 1) * HEAD_DIM, :]
            for g in range(NSA_KV_HEADS):
                xp = x2[:, g * HEAD_DIM:(g + 1) * HEAD_DIM]
                a[g] = a[g] + jnp.dot((xp + pe_ref[j, p:p + 1, :]).astype(BF16), wa, preferred_element_type=F32)
                bm[g] = bm[g] + jnp.dot((xp + pe_ref[j, p2:p2 + 1, :]).astype(BF16), wb, preferred_element_type=F32)
        for g in range(NSA_KV_HEADS):
            sh_ref[pl.ds(0, nseg), :] = bm[g]
            x = a[g] + sh_ref[pl.ds(1, nseg), :]
            hid = x * (0.5 * (1.0 + jnp.tanh(math.sqrt(2.0 / math.pi) * (x + 0.044715 * (x * x * x)))))
            ckv = jnp.dot(hid.astype(BF16), w2_ref[j], preferred_element_type=F32)
            if j == 0:
                ck_ref[g] = _rms_rows(ckv, kn_ref[...]).astype(BF16)
            else:
                cv_ref[g] = ckv.astype(BF16)


def _compress_prompt_kernel(xk_ref, xv_ref, pe_ref, w1_ref, w2_ref, kn_ref, ck_ref, cv_ref, sh_ref, *, nseg):
    _compress_compute((xk_ref, xv_ref), pe_ref, w1_ref, w2_ref, kn_ref, ck_ref, cv_ref, sh_ref, nseg)


def _compress_prompt(kvr, pe, w1, w2, kn):
    b, t, _ = kvr.shape
    nseg = t // CMP_STRIDE
    full = lambda a: pl.BlockSpec(a.shape, lambda i: (0,) * a.ndim)
    o_spec = pl.BlockSpec((None, NSA_KV_HEADS, nseg, HEAD_DIM), lambda i: (i, 0, 0, 0))
    o_shape = jax.ShapeDtypeStruct((b, NSA_KV_HEADS, nseg, HEAD_DIM), BF16)
    return pl.pallas_call(
        functools.partial(_compress_prompt_kernel, nseg=nseg),
        grid=(b,),
        in_specs=[pl.BlockSpec((None, t, LANES), lambda i: (i, 0, 0)), pl.BlockSpec((None, t, LANES), lambda i: (i, 0, 1)),
                  full(pe), full(w1), full(w2), full(kn)],
        out_specs=[o_spec, o_spec],
        out_shape=[o_shape, o_shape],
        scratch_shapes=[pltpu.VMEM((nseg + 8, CMP_HIDDEN), F32)],
        compiler_params=_cparams(("arbitrary",), VMEM_LIMIT_BYTES),
        name="nsa_compress",
    )(kvr, kvr, pe, w1, w2, kn)


def _stack_heads(qs_ref, kh):
    parts = [qs_ref[:, (kh * NSA_GROUP + g) * HEAD_DIM:(kh * NSA_GROUP + g + 1) * HEAD_DIM].astype(F32)
             for g in range(NSA_GROUP)]
    return jnp.concatenate(parts, axis=0).astype(BF16)


def _cmp_topk_kernel(qs_ref, ck_ref, cv_ref, farcol_ref, chi_ref, clo_ref, pool_ref, ocmp_ref, msk_ref,
                     *, tq, nseg, q_base, n_pick, n_blk):
    qi = pl.program_id(1)
    G = NSA_GROUP
    q0 = q_base + qi * tq
    nbase = q0 // CMP_STRIDE - 16
    place = (lax.broadcasted_iota(jnp.int32, (32, nseg), 1) - lax.broadcasted_iota(jnp.int32, (32, nseg), 0)) == nbase
    place = jnp.where(place, 1.0, 0.0).astype(BF16)
    t1 = q0 + lax.broadcasted_iota(jnp.int32, (tq, 1), 0)
    t4 = jnp.concatenate([t1] * G, axis=0)
    n_i = lax.broadcasted_iota(jnp.int32, (G * tq, nseg), 1)
    valid = (n_i * CMP_STRIDE + (CMP_BLOCK - 1) <= t4) & (n_i <= nseg - 2)
    blk = lax.broadcasted_iota(jnp.int32, (tq, N_SELBLK), 1)
    cur = lax.shift_right_logical(t1, 6)
    forced = (blk == 0) | (blk == cur) | (blk == cur - 1)
    for kh in range(NSA_KV_HEADS):
        q4 = _stack_heads(qs_ref, kh)
        s = _dot_nt(q4, ck_ref[kh]) + farcol_ref[kh]
        s = s + jnp.dot(chi_ref[kh], place, preferred_element_type=F32) + jnp.dot(clo_ref[kh], place, preferred_element_type=F32)
        s = jnp.where(valid, s, NEG)
        e = jnp.where(valid, jnp.exp(s - jnp.max(s, axis=-1, keepdims=True)), 0.0)
        p = e / jnp.maximum(jnp.sum(e, axis=-1, keepdims=True), 1e-30)
        o = jnp.dot(p.astype(BF16), cv_ref[kh], preferred_element_type=F32)
        for g in range(G):
            h = kh * G + g
            ocmp_ref[:, h * HEAD_DIM:(h + 1) * HEAD_DIM] = o[g * tq:(g + 1) * tq]
        imp = p[0:tq] + p[tq:2 * tq] + p[2 * tq:3 * tq] + p[3 * tq:4 * tq]
        pooled = _dot_f32(imp, pool_ref[...])
        score = jnp.where((blk > cur) | (blk >= n_blk), -1e30, jnp.where(forced, 1e30, pooled))
        sel = jnp.zeros((tq, N_SELBLK), F32)
        for _ in range(n_pick):
            mx = jnp.max(score, axis=-1, keepdims=True)
            first = jnp.min(jnp.where(score == mx, blk, N_SELBLK), axis=-1, keepdims=True)
            pick = blk == first
            sel = jnp.where(pick, 1.0, sel)
            score = jnp.where(pick, -3e38, score)
        msk_ref[kh] = sel


def _cmp_bias_tables(tbl, tq):
    tr = jnp.arange(tq)[:, None]
    i = jnp.arange(32)[None, :]
    dist = tr + 16 * CMP_STRIDE - CMP_STRIDE * i - (CMP_BLOCK - 1)
    near = tbl[:, _j_rel_bucket(dist)]
    far = tbl[:, NUM_BUCKETS - 1]
    corr = (near - far[:, None, None]).reshape(NSA_KV_HEADS, NSA_GROUP * tq, 32)
    hi = corr.astype(BF16)
    lo = (corr - hi.astype(F32)).astype(BF16)
    farcol = jnp.broadcast_to(far[:, None, None], (NSA_HEADS, tq, 1)).reshape(NSA_KV_HEADS, NSA_GROUP * tq, 1)
    return farcol, hi, lo


def _cmp_topk(qs, ck, cv, tbl, tq, q_base, n_pick, n_blk):
    b, t, _ = qs.shape
    nseg = ck.shape[2]
    farcol, chi, clo = _cmp_bias_tables(tbl, tq)
    pool = jnp.asarray((np.arange(nseg)[:, None] // SEL_RATIO == np.arange(N_SELBLK)[None, :]).astype(np.float32))
    full = lambda a: pl.BlockSpec(a.shape, lambda bi, i: (0,) * a.ndim)
    c_spec = pl.BlockSpec((None, NSA_KV_HEADS, nseg, HEAD_DIM), lambda bi, i: (bi, 0, 0, 0))
    return pl.pallas_call(
        functools.partial(_cmp_topk_kernel, tq=tq, nseg=nseg, q_base=q_base, n_pick=n_pick, n_blk=n_blk),
        grid=(b, t // tq),
        in_specs=[pl.BlockSpec((None, tq, NSA_WIDTH), lambda bi, i: (bi, i, 0)), c_spec, c_spec,
                  full(farcol), full(chi), full(clo), full(pool)],
        out_specs=[pl.BlockSpec((None, tq, NSA_WIDTH), lambda bi, i: (bi, i, 0)),
                   pl.BlockSpec((None, NSA_KV_HEADS, tq, N_SELBLK), lambda bi, i: (bi, 0, i, 0))],
        out_shape=[jax.ShapeDtypeStruct((b, t, NSA_WIDTH), F32),
                   jax.ShapeDtypeStruct((b, NSA_KV_HEADS, t, N_SELBLK), F32)],
        compiler_params=_cparams(("arbitrary", "arbitrary"), VMEM_LIMIT_BYTES),
        name="nsa_cmp_topk",
    )(qs, ck, cv, farcol, chi, clo, pool)


def _online_update(s, valid, v, m_ref, l_ref, acc_ref):
    s = jnp.where(valid, s, NEG)
    m_old = m_ref[...]
    m_new = jnp.maximum(m_old, jnp.max(s, axis=-1, keepdims=True))
    p = jnp.where(valid, jnp.exp(s - m_new), 0.0)
    alpha = jnp.exp(m_old - m_new)
    l_ref[...] = alpha * l_ref[...] + jnp.sum(p, axis=-1, keepdims=True)
    acc_ref[...] = alpha * acc_ref[...] + jnp.dot(p.astype(BF16), v, preferred_element_type=F32)
    m_ref[...] = m_new


def _selwin_kernel(qs_ref, selkv_ref, winkv_ref, msk_ref, bias_ref, ocmp_ref, gt_ref, o_ref,
                   m_ref, l_ref, acc_ref, osel_ref, owin_ref, *, tq):
    qi = pl.program_id(1)
    G = NSA_GROUP
    blocks_per_tile = tq // SEL_BLOCK
    tr = lax.broadcasted_iota(jnp.int32, (G * tq, tq), 0) & (tq - 1)
    sr = lax.broadcasted_iota(jnp.int32, (G * tq, tq), 1)
    causal = sr <= tr
    eb = lax.broadcasted_iota(jnp.int32, (N_SELBLK, tq), 0)
    el = lax.shift_right_logical(lax.broadcasted_iota(jnp.int32, (N_SELBLK, tq), 1), 6)

    def init():
        m_ref[...] = jnp.full_like(m_ref, NEG)
        l_ref[...] = jnp.zeros_like(l_ref)
        acc_ref[...] = jnp.zeros_like(acc_ref)

    def finish(dst_ref, kh):
        o = acc_ref[...] / jnp.maximum(l_ref[...], 1e-30)
        for g in range(G):
            h = kh * G + g
            dst_ref[:, h * HEAD_DIM:(h + 1) * HEAD_DIM] = o[g * tq:(g + 1) * tq]

    for kh in range(NSA_KV_HEADS):
        kcols = slice(kh * HEAD_DIM, (kh + 1) * HEAD_DIM)
        vcols = slice(128 + kh * HEAD_DIM, 128 + (kh + 1) * HEAD_DIM)
        q4 = _stack_heads(qs_ref, kh)
        mskb = msk_ref[kh].astype(BF16)

        def sel_valid(j):
            expand = jnp.where(eb == j * blocks_per_tile + el, 1.0, 0.0).astype(BF16)
            mt = jnp.dot(mskb, expand, preferred_element_type=F32) > 0.5
            return jnp.concatenate([mt] * G, axis=0)

        def sel_tile(j, bias_idx, extra):
            rows = pl.ds(pl.multiple_of(j * tq, tq), tq)
            s = _dot_nt(q4, selkv_ref[rows, kcols]) + bias_ref[kh, bias_idx]
            valid = sel_valid(j)
            if extra is not None:
                valid = valid & extra
            _online_update(s, valid, selkv_ref[rows, vcols], m_ref, l_ref, acc_ref)

        def win_tile(j, bias_idx, valid):
            rows = pl.ds(pl.multiple_of(j * tq, tq), tq)
            s = _dot_nt(q4, winkv_ref[rows, kcols]) + bias_ref[kh, bias_idx]
            _online_update(s, valid, winkv_ref[rows, vcols], m_ref, l_ref, acc_ref)

        init()

        def far_body(j, c):
            sel_tile(j, 2, None)
            return c

        lax.fori_loop(0, jnp.maximum(qi - 1, 0), far_body, 0)

        @pl.when(qi >= 1)
        def _():
            sel_tile(qi - 1, 1, None)

        sel_tile(qi, 0, causal)
        finish(osel_ref, kh)

        init()

        @pl.when(qi >= 2)
        def _():
            win_tile(qi - 2, 2, sr >= tr)

        @pl.when(qi >= 1)
        def _():
            win_tile(qi - 1, 1, sr >= 0)

        win_tile(qi, 0, causal)
        finish(owin_ref, kh)

    for h in range(NSA_HEADS):
        cols = slice(h * HEAD_DIM, (h + 1) * HEAD_DIM)
        o_ref[:, cols] = (gt_ref[:, 3 * h:3 * h + 1] * ocmp_ref[:, cols]
                          + gt_ref[:, 3 * h + 1:3 * h + 2] * osel_ref[:, cols]
                          + gt_ref[:, 3 * h + 2:3 * h + 3] * owin_ref[:, cols])


def _selwin_bias_tables(tbl, tq):
    tr = jnp.arange(tq)[:, None]
    sr = jnp.arange(tq)[None, :]
    near0 = tbl[:, _j_rel_bucket(tr - sr)]
    near1 = tbl[:, _j_rel_bucket(tr - sr + tq)]
    far = jnp.broadcast_to(tbl[:, NUM_BUCKETS - 1][:, None, None], near0.shape)
    b = jnp.stack([near0, near1, far], axis=1)
    b = b.reshape(NSA_KV_HEADS, NSA_GROUP, 3, tq, tq).transpose(0, 2, 1, 3, 4)
    return b.reshape(NSA_KV_HEADS, 3, NSA_GROUP * tq, tq)


def _selwin_prompt(qs, selkv, winkv, msk, tbl, ocmp, gt, tq):
    b, t, _ = qs.shape
    assert tq >= NSA_WINDOW // 2 and tq >= MAX_DISTANCE
    bias = _selwin_bias_tables(tbl, tq)
    row = lambda w: pl.BlockSpec((None, tq, w), lambda bi, i: (bi, i, 0))
    res = lambda w: pl.BlockSpec((None, t, w), lambda bi, i: (bi, 0, 0))
    return pl.pallas_call(
        functools.partial(_selwin_kernel, tq=tq),
        grid=(b, t // tq),
        in_specs=[row(NSA_WIDTH), res(256), res(256),
                  pl.BlockSpec((None, NSA_KV_HEADS, tq, N_SELBLK), lambda bi, i: (bi, 0, i, 0)),
                  pl.BlockSpec(bias.shape, lambda bi, i: (0, 0, 0, 0)),
                  row(NSA_WIDTH), row(LANES)],
        out_specs=row(NSA_WIDTH),
        out_shape=jax.ShapeDtypeStruct((b, t, NSA_WIDTH), F32),
        scratch_shapes=[pltpu.VMEM((NSA_GROUP * tq, 1), F32), pltpu.VMEM((NSA_GROUP * tq, 1), F32),
                        pltpu.VMEM((NSA_GROUP * tq, HEAD_DIM), F32),
                        pltpu.VMEM((tq, NSA_WIDTH), F32), pltpu.VMEM((tq, NSA_WIDTH), F32)],
        compiler_params=_cparams(("arbitrary", "arbitrary"), VMEM_LIMIT_BYTES),
        name="nsa_selwin",
    )(qs, selkv, winkv, msk, bias, ocmp, gt)


def _nsa_prompt(nqs, kvr, selkv, winkv, gt, pe, w1b, w2b, kn0, tbl, tq):
    ck, cv = _compress_prompt(kvr, pe, w1b, w2b, kn0)
    ocmp, msk = _cmp_topk(nqs, ck, cv, tbl, tq, 0, SEL_TOPN, nqs.shape[1] // SEL_BLOCK)
    return _selwin_prompt(nqs, selkv, winkv, msk, tbl, ocmp, gt, tq)


PAGES_PER_STEP = 4
PAGE = 128


def _page_specs(width, col_block):
    return [pl.BlockSpec((None, PAGE, width), functools.partial(
        lambda bi, j, pt, r: (pt[bi, j * PAGES_PER_STEP + r], 0, col_block), r=r)) for r in range(PAGES_PER_STEP)]


def _compress_sample_kernel(pt_ref, p0_ref, p1_ref, p2_ref, p3_ref, pe_ref, w1_ref, w2_ref, kn_ref,
                            ck_ref, cv_ref, srck_ref, srcv_ref, sh_ref, *, nseg):
    j = pl.program_id(1)
    for r, p_ref in enumerate((p0_ref, p1_ref, p2_ref, p3_ref)):
        rows = pl.ds(pl.multiple_of((j * PAGES_PER_STEP + r) * PAGE, PAGE), PAGE)
        srck_ref[rows, :] = p_ref[:, 0:LANES]
        srcv_ref[rows, :] = p_ref[:, LANES:2 * LANES]

    @pl.when(j == pl.num_programs(1) - 1)
    def _():
        _compress_compute((srck_ref, srcv_ref), pe_ref, w1_ref, w2_ref, kn_ref, ck_ref, cv_ref, sh_ref, nseg)


def _compress_sample(cache, page_table, pe, w1, w2, kn):
    sb, n_pages = page_table.shape
    past = n_pages * PAGE
    nseg = past // CMP_STRIDE
    full = lambda a: pl.BlockSpec(a.shape, lambda bi, j, pt: (0,) * a.ndim)
    o_spec = pl.BlockSpec((None, NSA_KV_HEADS, nseg, HEAD_DIM), lambda bi, j, pt: (bi, 0, 0, 0))
    o_shape = jax.ShapeDtypeStruct((sb, NSA_KV_HEADS, nseg, HEAD_DIM), BF16)
    return pl.pallas_call(
        functools.partial(_compress_sample_kernel, nseg=nseg),
        grid_spec=pltpu.PrefetchScalarGridSpec(
            num_scalar_prefetch=1, grid=(sb, n_pages // PAGES_PER_STEP),
            in_specs=_page_specs(256, 0) + [full(pe), full(w1), full(w2), full(kn)],
            out_specs=[o_spec, o_spec],
            scratch_shapes=[pltpu.VMEM((past, LANES), F32), pltpu.VMEM((past, LANES), F32),
                            pltpu.VMEM((nseg + 8, CMP_HIDDEN), F32)]),
        out_shape=[o_shape, o_shape],
        compiler_params=_cparams(("arbitrary", "arbitrary"), VMEM_LIMIT_BYTES),
        name="nsa_compress_sample",
    )(page_table, cache, cache, cache, cache, pe, w1, w2, kn)


def _selwin_sample_kernel(pt_ref, qs_ref, p0_ref, p1_ref, p2_ref, p3_ref, msk_ref, newkv_ref, winst_ref, newwr_ref,
                          bsel_ref, bnew_ref, bwin_ref, ocmp_ref, gt_ref, o_ref,
                          kv_ref, m_ref, l_ref, acc_ref, osel_ref, owin_ref, *, tk, n_tiles):
    j = pl.program_id(1)
    G = NSA_GROUP
    R = SAMPLE_ROWS
    for r, p_ref in enumerate((p0_ref, p1_ref, p2_ref, p3_ref)):
        kv_ref[pl.ds(pl.multiple_of((j * PAGES_PER_STEP + r) * PAGE, PAGE), PAGE), :] = p_ref[...].astype(BF16)

    @pl.when(j == pl.num_programs(1) - 1)
    def _():
        blocks_per_tile = tk // SEL_BLOCK
        eb = lax.broadcasted_iota(jnp.int32, (N_SELBLK, tk), 0)
        el = lax.shift_right_logical(lax.broadcasted_iota(jnp.int32, (N_SELBLK, tk), 1), 6)
        rq = lax.broadcasted_iota(jnp.int32, (G * R, 1), 0) & (R - 1)
        new_valid = lax.broadcasted_iota(jnp.int32, (G * R, R), 1) <= rq
        win_valid = lax.broadcasted_iota(jnp.int32, (G * R, NSA_WINDOW), 1) >= rq

        def init():
            m_ref[...] = jnp.full_like(m_ref, NEG)
            l_ref[...] = jnp.zeros_like(l_ref)
            acc_ref[...] = jnp.zeros_like(acc_ref)

        def finish(dst_ref, kh):
            o = acc_ref[...] / jnp.maximum(l_ref[...], 1e-30)
            for g in range(G):
                h = kh * G + g
                dst_ref[:, h * HEAD_DIM:(h + 1) * HEAD_DIM] = o[g * R:(g + 1) * R]

        for kh in range(NSA_KV_HEADS):
            kcols = slice(kh * HEAD_DIM, (kh + 1) * HEAD_DIM)
            vcols = slice(128 + kh * HEAD_DIM, 128 + (kh + 1) * HEAD_DIM)
            q4 = _stack_heads(qs_ref, kh)
            mskb = msk_ref[kh].astype(BF16)

            def sel_tile(jt, bias):
                rows = pl.ds(jt * tk if isinstance(jt, int) else pl.multiple_of(jt * tk, tk), tk)
                expand = jnp.where(eb == jt * blocks_per_tile + el, 1.0, 0.0).astype(BF16)
                mt = jnp.dot(mskb, expand, preferred_element_type=F32) > 0.5
                valid = jnp.concatenate([mt] * G, axis=0)
                s = _dot_nt(q4, kv_ref[rows, kcols]) + bias
                _online_update(s, valid, kv_ref[rows, vcols], m_ref, l_ref, acc_ref)

            init()

            def far_body(jt, c):
                sel_tile(jt, bsel_ref[kh, 1])
                return c

            lax.fori_loop(0, n_tiles - 1, far_body, 0)
            sel_tile(n_tiles - 1, bsel_ref[kh, 0])
            knew = newkv_ref[:, 256 + kh * HEAD_DIM:256 + (kh + 1) * HEAD_DIM].astype(BF16)
            vnew = newkv_ref[:, 384 + kh * HEAD_DIM:384 + (kh + 1) * HEAD_DIM].astype(BF16)
            _online_update(_dot_nt(q4, knew) + bnew_ref[kh], new_valid, vnew, m_ref, l_ref, acc_ref)
            finish(osel_ref, kh)

            init()
            kwin = winst_ref[:, kcols].astype(BF16)
            vwin = winst_ref[:, vcols].astype(BF16)
            _online_update(_dot_nt(q4, kwin) + bwin_ref[kh], win_valid, vwin, m_ref, l_ref, acc_ref)
            knew = newwr_ref[:, kcols].astype(BF16)
            vnew = newwr_ref[:, vcols].astype(BF16)
            _online_update(_dot_nt(q4, knew) + bnew_ref[kh], new_valid, vnew, m_ref, l_ref, acc_ref)
            finish(owin_ref, kh)

        for h in range(NSA_HEADS):
            cols = slice(h * HEAD_DIM, (h + 1) * HEAD_DIM)
            o_ref[:, cols] = (gt_ref[:, 3 * h:3 * h + 1] * ocmp_ref[:, cols]
                              + gt_ref[:, 3 * h + 1:3 * h + 2] * osel_ref[:, cols]
                              + gt_ref[:, 3 * h + 2:3 * h + 3] * owin_ref[:, cols])


def _sample_bias_tables(tbl, tk):
    R = SAMPLE_ROWS
    r = jnp.arange(R)[:, None]
    stack = lambda a: a.reshape(NSA_KV_HEADS, NSA_GROUP * R, a.shape[-1])
    last = tbl[:, _j_rel_bucket(tk + r - jnp.arange(tk)[None, :])]
    far = jnp.broadcast_to(tbl[:, NUM_BUCKETS - 1][:, None, None], last.shape)
    new = tbl[:, _j_rel_bucket(r - jnp.arange(R)[None, :])]
    win = tbl[:, _j_rel_bucket(NSA_WINDOW + r - jnp.arange(NSA_WINDOW)[None, :])]
    return jnp.stack([stack(last), stack(far)], axis=1), stack(new), stack(win)


def _selwin_sample(qs, cache, page_table, msk, newkv, winst, newwr, tbl, ocmp, gt):
    sb, n_pages = page_table.shape
    past = n_pages * PAGE
    tk = 256
    assert winst.shape[1] == NSA_WINDOW and past >= NSA_WINDOW and tk >= MAX_DISTANCE
    bsel, bnew, bwin = _sample_bias_tables(tbl, tk)
    R = SAMPLE_ROWS
    full = lambda a: pl.BlockSpec(a.shape, lambda bi, j, pt: (0,) * a.ndim)
    seq = lambda a: pl.BlockSpec((None,) + a.shape[1:], lambda bi, j, pt: (bi,) + (0,) * (a.ndim - 1))
    return pl.pallas_call(
        functools.partial(_selwin_sample_kernel, tk=tk, n_tiles=past // tk),
        grid_spec=pltpu.PrefetchScalarGridSpec(
            num_scalar_prefetch=1, grid=(sb, n_pages // PAGES_PER_STEP),
            in_specs=[seq(qs)] + _page_specs(256, 1) + [seq(msk), seq(newkv), seq(winst), seq(newwr),
                                                        full(bsel), full(bnew), full(bwin), seq(ocmp), seq(gt)],
            out_specs=pl.BlockSpec((None, R, NSA_WIDTH), lambda bi, j, pt: (bi, 0, 0)),
            scratch_shapes=[pltpu.VMEM((past, 256), BF16),
                            pltpu.VMEM((NSA_GROUP * R, 1), F32), pltpu.VMEM((NSA_GROUP * R, 1), F32),
                            pltpu.VMEM((NSA_GROUP * R, HEAD_DIM), F32),
                            pltpu.VMEM((R, NSA_WIDTH), F32), pltpu.VMEM((R, NSA_WIDTH), F32)]),
        out_shape=jax.ShapeDtypeStruct((sb, R, NSA_WIDTH), F32),
        compiler_params=_cparams(("arbitrary", "arbitrary"), VMEM_LIMIT_BYTES),
        name="nsa_selwin_sample",
    )(page_table, qs, cache, cache, cache, cache, msk, newkv, winst, newwr, bsel, bnew, bwin, ocmp, gt)


def _fox_suffix_kernel(pt_ref, p0_ref, p1_ref, p2_ref, p3_ref, su_ref, d_ref, carry_ref):
    @pl.when(pl.program_id(1) == 0)
    def _():
        carry_ref[...] = jnp.zeros_like(carry_ref)

    carry = carry_ref[0:1, 0:FOX_HEADS]
    for r, p_ref in reversed(list(enumerate((p0_ref, p1_ref, p2_ref, p3_ref)))):
        lf = p_ref[...]
        d_ref[r * PAGE:(r + 1) * PAGE, :] = _dot_f32(su_ref[...], lf) + carry
        carry = carry + jnp.sum(lf, axis=0, keepdims=True)
    carry_ref[0:1, 0:FOX_HEADS] = carry


def _fox_suffix(logf_cache, page_table):
    sb, n_pages = page_table.shape
    n_steps = n_pages // PAGES_PER_STEP
    su = jnp.asarray(np.triu(np.ones((PAGE, PAGE), np.float32), 1))
    specs = [pl.BlockSpec((None, PAGE, FOX_HEADS), functools.partial(
        lambda bi, j, pt, r: (pt[bi, (n_steps - 1 - j) * PAGES_PER_STEP + r], 0, 0), r=r)) for r in range(PAGES_PER_STEP)]
    return pl.pallas_call(
        _fox_suffix_kernel,
        grid_spec=pltpu.PrefetchScalarGridSpec(
            num_scalar_prefetch=1, grid=(sb, n_steps),
            in_specs=specs + [pl.BlockSpec(su.shape, lambda bi, j, pt: (0, 0))],
            out_specs=pl.BlockSpec((None, PAGES_PER_STEP * PAGE, FOX_HEADS), lambda bi, j, pt: (bi, n_steps - 1 - j, 0)),
            scratch_shapes=[pltpu.VMEM((8, 128), F32)]),
        out_shape=jax.ShapeDtypeStruct((sb, n_pages * PAGE, FOX_HEADS), F32),
        compiler_params=_cparams(("arbitrary", "arbitrary")),
        name="fox_suffix",
    )(page_table, logf_cache, logf_cache, logf_cache, logf_cache, su)


def _fox_sample_kernel(pt_ref, qs_ref, p0_ref, p1_ref, p2_ref, p3_ref, dt_ref, newkv_ref, lfnew_ref, hmask_ref,
                       o_ref, qbd_ref, crel_ref, m_ref, l_ref, acc_ref):
    j = pl.program_id(1)
    R = SAMPLE_ROWS
    H = FOX_HEADS
    tk = PAGES_PER_STEP * PAGE

    @pl.when(j == 0)
    def _():
        q = qs_ref[...].astype(F32)
        qbd_ref[...] = (jnp.concatenate([q] * H, axis=0) * hmask_ref[...]).astype(BF16)
        tri = jnp.where(lax.broadcasted_iota(jnp.int32, (R, R), 1) <= lax.broadcasted_iota(jnp.int32, (R, R), 0), 1.0, 0.0)
        crel = _dot_f32(tri, lfnew_ref[...])
        crel_ref[...] = jnp.concatenate([crel[:, h:h + 1] for h in range(H)], axis=0)
        m_ref[...] = jnp.full_like(m_ref, NEG)
        l_ref[...] = jnp.zeros_like(l_ref)
        acc_ref[...] = jnp.zeros_like(acc_ref)

    kv = jnp.concatenate([p0_ref[...], p1_ref[...], p2_ref[...], p3_ref[...]], axis=0).astype(BF16)
    drows = jnp.concatenate([jnp.broadcast_to(dt_ref[h:h + 1, :], (R, tk)) for h in range(H)], axis=0)
    s = _dot_nt(qbd_ref[...], kv[:, 0:FOX_WIDTH]) + crel_ref[...] + drows
    _online_update(s, s > 2 * NEG, kv[:, FOX_WIDTH:], m_ref, l_ref, acc_ref)

    @pl.when(j == pl.num_programs(1) - 1)
    def _():
        lf = lfnew_ref[...]
        iu = lax.broadcasted_iota(jnp.int32, (R, R), 0)
        ir = lax.broadcasted_iota(jnp.int32, (R, R), 1)
        a_le = jnp.where(ir <= iu, 1.0, 0.0)
        b_gt = jnp.where(iu > ir, 1.0, 0.0)
        dnew = jnp.concatenate([_dot_f32(a_le, lf[:, h:h + 1] * b_gt) for h in range(H)], axis=0)
        rq = lax.broadcasted_iota(jnp.int32, (H * R, 1), 0) & (R - 1)
        valid = lax.broadcasted_iota(jnp.int32, (H * R, R), 1) <= rq
        knew = newkv_ref[:, 0:FOX_WIDTH].astype(BF16)
        vnew = newkv_ref[:, FOX_WIDTH:].astype(BF16)
        _online_update(_dot_nt(qbd_ref[...], knew) + dnew, valid, vnew, m_ref, l_ref, acc_ref)
        o = (acc_ref[...] / jnp.maximum(l_ref[...], 1e-30)) * hmask_ref[...]
        out = o[0:R]
        for h in range(1, H):
            out = out + o[h * R:(h + 1) * R]
        o_ref[...] = out


def _fox_sample(qs, kv_cache, page_table, dt, newkv, lfnew):
    sb, n_pages = page_table.shape
    R = SAMPLE_ROWS
    tk = PAGES_PER_STEP * PAGE
    hmask = jnp.asarray((np.arange(FOX_HEADS * R)[:, None] // R == np.arange(FOX_WIDTH)[None, :] // HEAD_DIM)
                        .astype(np.float32))
    seq = lambda a: pl.BlockSpec((None,) + a.shape[1:], lambda bi, j, pt: (bi,) + (0,) * (a.ndim - 1))
    return pl.pallas_call(
        _fox_sample_kernel,
        grid_spec=pltpu.PrefetchScalarGridSpec(
            num_scalar_prefetch=1, grid=(sb, n_pages // PAGES_PER_STEP),
            in_specs=[seq(qs)] + _page_specs(2 * FOX_WIDTH, 0)
                     + [pl.BlockSpec((None, FOX_HEADS, tk), lambda bi, j, pt: (bi, 0, j)), seq(newkv), seq(lfnew),
                        pl.BlockSpec(hmask.shape, lambda bi, j, pt: (0, 0))],
            out_specs=pl.BlockSpec((None, R, FOX_WIDTH), lambda bi, j, pt: (bi, 0, 0)),
            scratch_shapes=[pltpu.VMEM((FOX_HEADS * R, FOX_WIDTH), BF16), pltpu.VMEM((FOX_HEADS * R, 1), F32),
                            pltpu.VMEM((FOX_HEADS * R, 1), F32), pltpu.VMEM((FOX_HEADS * R, 1), F32),
                            pltpu.VMEM((FOX_HEADS * R, FOX_WIDTH), F32)]),
        out_shape=jax.ShapeDtypeStruct((sb, R, FOX_WIDTH), F32),
        compiler_params=_cparams(("arbitrary", "arbitrary"), VMEM_LIMIT_BYTES),
        name="fox_sample",
    )(page_table, qs, kv_cache, kv_cache, kv_cache, kv_cache, dt, newkv, lfnew, hmask)


def _j_rmsnorm(x, g):
    xf = x.astype(jnp.float32)
    y = xf * lax.rsqrt(jnp.mean(xf * xf, axis=-1, keepdims=True) + EPS)
    return (y * g.astype(jnp.float32)).astype(x.dtype)


def _j_masked_softmax(s, mask):
    s = jnp.where(mask, s, -1e30)
    p = jnp.exp(s - jnp.max(s, axis=-1, keepdims=True)) * mask
    return p / jnp.maximum(jnp.sum(p, axis=-1, keepdims=True), 1e-30)


def _j_rel_bucket(dist):
    exact = NUM_BUCKETS // 2
    d = jnp.maximum(dist, 0)
    log_ratio = jnp.log(jnp.maximum(d, 1).astype(jnp.float32) / exact) / math.log(MAX_DISTANCE / exact)
    large = jnp.minimum(exact + (log_ratio * (NUM_BUCKETS - exact)).astype(jnp.int32), NUM_BUCKETS - 1)
    return jnp.where(d < exact, d, large)


def _j_gather_pages(pool, page_table):
    g = pool.reshape(pool.shape[0], -1)[page_table]
    g = g.reshape(page_table.shape + pool.shape[1:])
    return g.reshape((g.shape[0], g.shape[1] * g.shape[2]) + g.shape[3:])


def _j_nsa_prepare(kv_rows, pe, w1, w2, kn_cmp):
    B, T = kv_rows.shape[:2]
    nc = (T - CMP_BLOCK) // CMP_STRIDE + 1
    seg = kv_rows[:, :(nc + 1) * CMP_STRIDE, 0:2].reshape(B, nc + 1, CMP_STRIDE, 2, NSA_KV_HEADS, HEAD_DIM)
    blk = jnp.concatenate([seg[:, :-1], seg[:, 1:]], axis=2)
    blk = blk + pe.transpose(1, 0, 2)[None, None, :, :, None, :]
    flat = blk.transpose(0, 1, 3, 4, 2, 5).reshape(B, nc, 2, NSA_KV_HEADS, CMP_BLOCK * HEAD_DIM)
    hid = jax.nn.gelu(jnp.einsum('bnjgf,jfe->bnjge', flat, w1))
    ckv = jnp.einsum('bnjge,jed->bnjgd', hid, w2)
    ck = _j_rmsnorm(ckv[:, :, 0], kn_cmp)
    cv = ckv[:, :, 1]
    c_end = jnp.arange(nc) * CMP_STRIDE + (CMP_BLOCK - 1)
    nbs = -(-T // SEL_BLOCK)
    sel = jnp.pad(kv_rows[:, :, 2:4], ((0, 0), (0, nbs * SEL_BLOCK - T), (0, 0), (0, 0), (0, 0)))
    sel = sel.reshape(B, nbs, SEL_BLOCK, 2, NSA_KV_HEADS, HEAD_DIM).transpose(3, 0, 4, 1, 2, 5)
    return ck, cv, c_end, sel[0], sel[1]


def _j_nsa_attend(q, gates, q_pos, ck, cv, c_end, sk, sv, wk, wv, w_pos, rel_bias):
    B, Q = q.shape[:2]
    scale = HEAD_DIM ** -0.5
    tbl = rel_bias.astype(jnp.float32).T.reshape(NSA_KV_HEADS, NSA_GROUP, NUM_BUCKETS)
    qg = q.reshape(B, Q, NSA_KV_HEADS, NSA_GROUP, HEAD_DIM).transpose(0, 2, 3, 1, 4)
    t = q_pos[:, None]
    s = jnp.einsum('bkgqd,bnkd->bkgqn', qg, ck).astype(jnp.float32) * scale + tbl[:, :, _j_rel_bucket(t - c_end[None])]
    p_cmp = _j_masked_softmax(s, c_end[None] <= t)
    o_cmp = jnp.einsum('bkgqn,bnkd->bkgqd', p_cmp.astype(cv.dtype), cv)
    nbs = sk.shape[2]
    imp = jnp.sum(p_cmp, axis=2)
    nc = imp.shape[-1]
    imp = jnp.pad(imp, ((0, 0), (0, 0), (0, 0), (0, nbs * SEL_RATIO - nc)))
    imp = imp.reshape(B, NSA_KV_HEADS, Q, nbs, SEL_RATIO).sum(-1)
    blk = jnp.arange(nbs)[None]
    cur = (q_pos // SEL_BLOCK)[:, None]
    forced = (blk == 0) | (blk == cur) | (blk == cur - 1)
    score = jnp.where(blk > cur, -jnp.inf, jnp.where(forced, jnp.inf, imp))
    n_sel = min(SEL_TOPN, nbs)
    _, idx = lax.top_k(score, n_sel)
    pick = jax.vmap(jax.vmap(lambda kb, ix: kb[ix]))
    gk = pick(sk, idx).reshape(B, NSA_KV_HEADS, Q, n_sel * SEL_BLOCK, HEAD_DIM)
    gv = pick(sv, idx).reshape(B, NSA_KV_HEADS, Q, n_sel * SEL_BLOCK, HEAD_DIM)
    spos = (idx[..., None] * SEL_BLOCK + jnp.arange(SEL_BLOCK)).reshape(B, NSA_KV_HEADS, Q, n_sel * SEL_BLOCK)
    kidx = jnp.arange(NSA_KV_HEADS)[None, :, None, None, None]
    gidx = jnp.arange(NSA_GROUP)[None, None, :, None, None]
    bias = tbl[kidx, gidx, _j_rel_bucket(t - spos)[:, :, None]]
    s = jnp.einsum('bkgqd,bkqsd->bkgqs', qg, gk).astype(jnp.float32) * scale + bias
    p = _j_masked_softmax(s, (spos <= t)[:, :, None])
    o_sel = jnp.einsum('bkgqs,bkqsd->bkgqd', p.astype(gv.dtype), gv)
    dist = t - w_pos[None]
    s = jnp.einsum('bkgqd,bwkd->bkgqw', qg, wk).astype(jnp.float32) * scale + tbl[:, :, _j_rel_bucket(dist)]
    p = _j_masked_softmax(s, (dist >= 0) & (dist <= NSA_WINDOW) & (w_pos[None] >= 0))
    o_win = jnp.einsum('bkgqw,bwkd->bkgqd', p.astype(wv.dtype), wv)
    o = jnp.stack([o_cmp, o_sel, o_win], axis=-1).transpose(0, 3, 1, 2, 4, 5)
    o = o.reshape(B, Q, NSA_HEADS, HEAD_DIM, 3)
    return jnp.einsum('bqhdr,bqhr->bqhd', o, gates.astype(o.dtype)).reshape(B, Q, NSA_WIDTH)


def _j_nsa_prompt(q, gates, kv_rows, win_rows, pe, w1, w2, kn_cmp, rel_bias):
    B, T = q.shape[:2]
    QB = 128
    ck, cv, c_end, sk, sv = _j_nsa_prepare(kv_rows, pe, w1, w2, kn_cmp)
    win_pad = jnp.pad(win_rows, ((0, 0), (NSA_WINDOW, 0), (0, 0), (0, 0), (0, 0)))

    def block(i):
        q0 = i * QB
        qb = lax.dynamic_slice_in_dim(q, q0, QB, axis=1)
        gb = lax.dynamic_slice_in_dim(gates, q0, QB, axis=1)
        wb = lax.dynamic_slice_in_dim(win_pad, q0, NSA_WINDOW + QB, axis=1)
        q_pos = q0 + jnp.arange(QB)
        w_pos = q0 - NSA_WINDOW + jnp.arange(NSA_WINDOW + QB)
        return _j_nsa_attend(qb, gb, q_pos, ck, cv, c_end, sk, sv, wb[:, :, 0], wb[:, :, 1], w_pos, rel_bias)

    o = lax.map(block, jnp.arange(T // QB))
    return o.transpose(1, 0, 2, 3).reshape(B, T, NSA_WIDTH)


def _j_fox_attend(q, cq, q_pos, k, v, ck, k_pos):
    s = jnp.einsum('bqhd,bshd->bhqs', q, k).astype(jnp.float32) * HEAD_DIM ** -0.5
    s = s + jnp.swapaxes(cq, 1, 2)[..., :, None] - jnp.swapaxes(ck, 1, 2)[..., None, :]
    p = _j_masked_softmax(s, k_pos[None, :] <= q_pos[:, None])
    return jnp.einsum('bhqs,bshd->bqhd', p.astype(v.dtype), v)


def _pad_cols(w, width):
    return jnp.pad(w, ((0, 0), (0, width - w.shape[1])))


def _prep_even_w(w):
    gq, gk, gv, glr, gog, nq, nkv, ng = _split(w, EVEN_SIZES)
    return jnp.concatenate([gq, gk, gv, gog, nq, nkv, _pad_cols(glr, LANES), _pad_cols(ng, LANES)], axis=1).astype(BF16)


def _prep_odd_w(w):
    fq, fk, fv, ff, cg = _split(w, ODD_SIZES)
    return jnp.concatenate([fq, fk, fv, cg, _pad_cols(ff, LANES)], axis=1).astype(BF16)


def _row(v):
    return v.reshape(1, -1).astype(F32)


def _tile_row(v, reps):
    return jnp.tile(v.astype(F32), reps).reshape(1, -1)


def kernel(x_prompt, x_sample, cache_nsa_kv, state_nsa_win, state_gla, cache_fox_kv, cache_fox_logf, state_conv, cache_mem_kv, page_table, mem_prompt, rel_bias, norm_mix, norm_xattn, norm_ffn, even_w_in, even_w_out, gla_w_gate, gla_b_gate, gla_out_norm, nsa_q_norm, nsa_k_norm, nsa_cmp_pe, nsa_cmp_w1, nsa_cmp_w2, odd_w_in, odd_w_out, fox_q_norm, fox_k_norm, fox_b_f, conv_w, conv_b, conv_ln_g, conv_ln_b, mem_norm, xa_wq, xa_wkv, xa_wo, xa_q_norm, xa_k_norm, ffn_w_in, ffn_w_out):
    B, T, _ = x_prompt.shape
    SB, SQ, _ = x_sample.shape
    depth = norm_mix.shape[0]
    past_len = page_table.shape[1] * cache_nsa_kv.shape[2]
    n_win = state_nsa_win.shape[2]
    dec_pos = past_len + jnp.arange(SQ)
    MP = B * T
    SR = SAMPLE_ROWS
    MS = SB * SR

    yp = x_prompt.reshape(MP, D_MODEL)
    ys = jnp.pad(x_sample, ((0, 0), (0, SR - SQ), (0, 0))).reshape(MS, D_MODEL)

    nsa_kv_p, nsa_kv_s, win_p, win_s, gla_p, gla_s = [], [], [], [], [], []
    fox_kv_p, fox_kv_s, logf_p, logf_s, conv_p, conv_s, mem_kv_p = [], [], [], [], [], [], []

    for layer in range(depth):
        if layer % 2 == 0:
            e = layer // 2
            w_pad = _prep_even_w(even_w_in[e])
            wg_pad = jnp.pad(gla_w_gate[e], ((0, LANES - GLA_RANK), (0, 0))).astype(BF16)
            bg = _row(gla_b_gate[e])
            qn = _tile_row(nsa_q_norm[e], NSA_HEADS)
            kn1 = _tile_row(nsa_k_norm[e, 1], NSA_KV_HEADS)
            kn2 = _tile_row(nsa_k_norm[e, 2], NSA_KV_HEADS)
            gn = _row(gla_out_norm[e])
            w_out = even_w_out[e].astype(BF16)
            g_mix = _row(norm_mix[layer])
            pe = nsa_cmp_pe[e].astype(F32)
            w1b = nsa_cmp_w1[e].astype(BF16)
            w2b = nsa_cmp_w2[e].astype(BF16)
            kn0 = _row(nsa_k_norm[e, 0])
            tbl = rel_bias.astype(F32).T
            (q, k, v, la, og, nqs, kvr, wr, gt, selkv, winkv) = _even_in(yp, g_mix, w_pad, wg_pad, bg, qn, kn1, kn2, 256)
            r3 = lambda a: a.reshape(B, T, a.shape[-1])
            s0t = jnp.zeros((B, GLA_HEADS, GLA_DV, GLA_DK), F32)
            o_gla, sfin_t = _gla(r3(q), r3(k), r3(v), r3(la), r3(og), gn, s0t, 256)
            kvr5 = kvr.reshape(B, T, 4, NSA_KV_HEADS, HEAD_DIM)
            wr5 = wr.reshape(B, T, 2, NSA_KV_HEADS, HEAD_DIM)
            o_nsa = _nsa_prompt(r3(nqs), r3(kvr), r3(selkv), r3(winkv), r3(gt), pe, w1b, w2b, kn0, tbl, 256)
            yp = _out_proj(yp, o_gla.reshape(MP, GLA_WIDTH), o_nsa.reshape(MP, NSA_WIDTH),
                           w_out[:GLA_WIDTH], w_out[GLA_WIDTH:], 512)
            nsa_kv_p.append(kvr5)
            win_p.append(wr5[:, -min(NSA_WINDOW, T):])
            gla_p.append(jnp.swapaxes(sfin_t, -1, -2))
            (q, k, v, la, og, nqs, kvr, wr, gt, selkv, winkv) = _even_in(ys, g_mix, w_pad, wg_pad, bg, qn, kn1, kn2, MS)
            pad16 = lambda a: jnp.pad(a.reshape(SB, SR, a.shape[-1]), ((0, 0), (0, SUB - SR), (0, 0)))
            s0t = jnp.swapaxes(state_gla[e], -1, -2)
            o_gla, snew_t = _gla(pad16(q), pad16(k), pad16(v), pad16(la), pad16(og), gn, s0t, SUB, n_valid=SQ)
            o_gla = o_gla[:, :SR]
            s3 = lambda a: a.reshape(SB, SR, a.shape[-1])
            kvr5 = kvr.reshape(SB, SR, 4, NSA_KV_HEADS, HEAD_DIM)[:, :SQ]
            wr5 = wr.reshape(SB, SR, 2, NSA_KV_HEADS, HEAD_DIM)[:, :SQ]
            cache2 = cache_nsa_kv[e].reshape(cache_nsa_kv.shape[1], PAGE, 4 * NSA_KV_HEADS * HEAD_DIM)
            ck, cv = _compress_sample(cache2, page_table, pe, w1b, w2b, kn0)
            ocmp, msk = _cmp_topk(s3(nqs), ck, cv, tbl, SR, past_len, SEL_TOPN - 1, past_len // SEL_BLOCK)
            winst = state_nsa_win[e].reshape(SB, n_win, 2 * NSA_KV_HEADS * HEAD_DIM)
            o_nsa = _selwin_sample(s3(nqs), cache2, page_table, msk, s3(kvr), winst, s3(wr), tbl, ocmp, s3(gt))
            ys = _out_proj(ys, o_gla.reshape(MS, GLA_WIDTH), o_nsa.reshape(MS, NSA_WIDTH),
                           w_out[:GLA_WIDTH], w_out[GLA_WIDTH:], MS)
            nsa_kv_s.append(kvr5)
            win_s.append(jnp.concatenate([state_nsa_win[e][:, SQ:], wr5], axis=1))
            gla_s.append(jnp.swapaxes(snew_t, -1, -2))
        else:
            j = layer // 2
            w_pad = _prep_odd_w(odd_w_in[j])
            qn = _tile_row(fox_q_norm[j], FOX_HEADS)
            kn = _tile_row(fox_k_norm[j], FOX_HEADS)
            bf_pad = jnp.pad(fox_b_f[j].astype(F32), (0, LANES - FOX_HEADS)).reshape(1, LANES)
            w_out = odd_w_out[j].astype(BF16)
            g_mix = _row(norm_mix[layer])
            cw = conv_w[j].astype(F32)
            cb, cg_, cbeta = _row(conv_b[j]), _row(conv_ln_g[j]), _row(conv_ln_b[j])
            qs, kv, kb, vb, lf, c, u = _odd_in(yp.reshape(B, T, D_MODEL), g_mix, w_pad, qn, kn, bf_pad, 256)
            c8 = c[:, :, :FOX_HEADS]
            o_fox = _fox_prompt(qs, kb, vb, c8, jnp.swapaxes(c8, 1, 2), 512, 1024)
            o_conv, cst = _conv(u, jnp.zeros((B, CONV_WIDTH - 1, CONV_CH), F32), cw, cb, cg_, cbeta, 512, 512)
            yp = _out_proj(yp, o_fox.reshape(MP, FOX_WIDTH), o_conv.reshape(MP, CONV_CH),
                           w_out[:FOX_WIDTH], w_out[FOX_WIDTH:], 512)
            fox_kv_p.append(kv.reshape(B, T, 2, FOX_HEADS, HEAD_DIM))
            logf_p.append(lf[:, :, :FOX_HEADS])
            conv_p.append(cst)
            qs, kv, kb, vb, lf, c, u = _odd_in(ys.reshape(1, MS, D_MODEL), g_mix, w_pad, qn, kn, bf_pad, MS)
            s3 = lambda a: a.reshape(SB, SR, a.shape[-1])
            new_kv = kv.reshape(SB, SR, 2, FOX_HEADS, HEAD_DIM)[:, :SQ]
            lf_new = lf.reshape(SB, SR, LANES)[:, :SQ, :FOX_HEADS]
            dsuf = _fox_suffix(cache_fox_logf[j], page_table)
            kvc = cache_fox_kv[j].reshape(cache_fox_kv.shape[1], PAGE, 2 * FOX_WIDTH)
            o_fox = _fox_sample(s3(qs), kvc, page_table, jnp.swapaxes(dsuf, 1, 2), s3(kv), s3(lf))
            o_conv, cst = _conv(u.reshape(SB, SR, CONV_CH), state_conv[j], cw, cb, cg_, cbeta, SR, SQ)
            ys = _out_proj(ys, o_fox.reshape(MS, FOX_WIDTH), o_conv.reshape(MS, CONV_CH),
                           w_out[:FOX_WIDTH], w_out[FOX_WIDTH:], MS)
            fox_kv_s.append(new_kv)
            logf_s.append(lf_new)
            conv_s.append(cst)
        g_xa = _row(norm_xattn[layer])
        wq = xa_wq[layer].astype(BF16)
        wo = xa_wo[layer].astype(BF16)
        xqn = _tile_row(xa_q_norm[layer], XA_HEADS)
        mkv = _mem_kv(mem_prompt, _row(mem_norm[layer]), xa_wkv[layer].astype(BF16), _tile_row(xa_k_norm[layer], XA_HEADS))
        mem_kv_p.append(mkv.reshape(B, MEM_LEN, 2, XA_HEADS, HEAD_DIM))
        yp = _xattn(yp.reshape(B, T, D_MODEL), mkv, g_xa, wq, wo, xqn, 512).reshape(MP, D_MODEL)
        ys = _xattn(ys.reshape(SB, SR, D_MODEL), cache_mem_kv[layer].reshape(SB, MEM_LEN, 2 * XA_WIDTH),
                    g_xa, wq, wo, xqn, SR).reshape(MS, D_MODEL)
        g_ffn = _row(norm_ffn[layer])
        w_in = ffn_w_in[layer].astype(BF16)
        w_o = ffn_w_out[layer].astype(BF16)
        yp = _ffn(yp, g_ffn, w_in, w_o, 512)
        ys = _ffn(ys, g_ffn, w_in, w_o, MS)

    yp = yp.reshape(B, T, D_MODEL)
    ys = ys.reshape(SB, SR, D_MODEL)[:, :SQ]
    return (yp, ys,
            jnp.stack(nsa_kv_p), jnp.stack(nsa_kv_s), jnp.stack(win_p), jnp.stack(win_s),
            jnp.stack(gla_p), jnp.stack(gla_s), jnp.stack(fox_kv_p), jnp.stack(fox_kv_s),
            jnp.stack(logf_p), jnp.stack(logf_s), jnp.stack(conv_p), jnp.stack(conv_s),
            jnp.stack(mem_kv_p))
```

```python
import functools
import math

import jax
import jax.numpy as jnp
import numpy as np
from jax import lax
from jax.experimental import pallas as pl
from jax.experimental.pallas import tpu as pltpu

F32 = jnp.float32
BF16 = jnp.bfloat16

D_MODEL = 1024
HEAD_DIM = 64
GLA_WIDTH = 512
GLA_HEADS = 4
GLA_DV = 128
GLA_DK = 64
GLA_RANK = 16
GLA_TAU = 16.0
NSA_WIDTH = 512
NSA_HEADS = 8
NSA_KV_HEADS = 2
NSA_GROUP = 4
CMP_STRIDE = 16
CMP_BLOCK = 32
CMP_HIDDEN = 256
SEL_BLOCK = 64
SEL_RATIO = 4
SEL_TOPN = 16
NSA_WINDOW = 512
FOX_WIDTH = 512
FOX_HEADS = 8
CONV_CH = 512
CONV_WIDTH = 31
MEM_LEN = 256
XA_HEADS = 4
XA_WIDTH = 256
FFN_HIDDEN = 2816
NUM_BUCKETS = 32
MAX_DISTANCE = 128
EPS = 1e-6
SCALE = HEAD_DIM ** -0.5
NEG = -1e30

EVEN_SIZES = (256, 256, 512, 16, 512, 512, 768, 24)
ODD_SIZES = (512, 512, 512, 8, 1024)

LANES = 128
VMEM_LIMIT_BYTES = 56 * 1024 * 1024
SAMPLE_ROWS = 8
SUB = 16
N_CMP = 512
N_SELBLK = 128


def _cparams(sem, vmem=None):
    return pltpu.CompilerParams(dimension_semantics=sem, vmem_limit_bytes=vmem)


def _split(h, sizes):
    return jnp.split(h, np.cumsum(sizes)[:-1].tolist(), axis=-1)


def _rms_rows(x, g):
    return x * lax.rsqrt(jnp.mean(x * x, axis=-1, keepdims=True) + EPS) * g


def _group_rms(x, gmat, gs):
    x2 = x * x
    hi = x2.astype(BF16)
    lo = (x2 - hi.astype(F32)).astype(BF16)
    ms = (jnp.dot(hi, gmat, preferred_element_type=F32) + jnp.dot(lo, gmat, preferred_element_type=F32)) * (1.0 / gs)
    return x * lax.rsqrt(ms + EPS)


def _log_sigmoid(z):
    return -(jnp.maximum(-z, 0.0) + jnp.log1p(jnp.exp(-jnp.abs(z))))


def _sigmoid(z):
    return 1.0 / (1.0 + jnp.exp(-z))


def _dot_nt(a, b):
    return lax.dot_general(a, b, (((1,), (1,)), ((), ())), preferred_element_type=F32)


def _dot_tn(a, b):
    return lax.dot_general(a, b, (((0,), (0,)), ((), ())), preferred_element_type=F32)


def _dot_f32(a, b):
    return jnp.dot(a, b, preferred_element_type=F32, precision=lax.Precision.HIGHEST)


def _block_ones(width, gs):
    r = np.arange(width) // gs
    return jnp.asarray((r[:, None] == r[None, :]).astype(np.float32), dtype=BF16)


def _even_in_kernel(x_ref, g_ref, w_ref, wg_ref, bg_ref, qn_ref, kn1_ref, kn2_ref, gm512_ref, gm128_ref,
                    q_ref, k_ref, v_ref, la_ref, og_ref, nqs_ref, kvr_ref, wr_ref, gt_ref, selkv_ref, winkv_ref,
                    nqst_ref, selvt_ref, winvt_ref, gtt_ref):
    xb = _rms_rows(x_ref[...], g_ref[...]).astype(BF16)

    def proj(lo, hi):
        return jnp.dot(xb, w_ref[:, lo:hi], preferred_element_type=F32)

    q_ref[...] = proj(0, 256) * (GLA_DK ** -0.5)
    k_ref[...] = proj(256, 512)
    v_ref[...] = proj(512, 1024)
    og = proj(1024, 1536)
    og_ref[...] = og * _sigmoid(og)
    nq = _group_rms(proj(1536, 2048), gm512_ref[...], HEAD_DIM) * qn_ref[...] * SCALE
    nqs_ref[...] = nq.astype(BF16)
    nqst_ref[...] = nq.T.astype(BF16)
    kvr_ref[:, 0:256] = proj(2048, 2304)
    selk = _group_rms(proj(2304, 2432), gm128_ref[...], HEAD_DIM) * kn1_ref[...]
    selv = proj(2432, 2560)
    kvr_ref[:, 256:384] = selk
    kvr_ref[:, 384:512] = selv
    selkv_ref[:, 0:128] = selk.astype(BF16)
    selkv_ref[:, 128:256] = selv.astype(BF16)
    selvt_ref[...] = selv.T.astype(BF16)
    wink = _group_rms(proj(2560, 2688), gm128_ref[...], HEAD_DIM) * kn2_ref[...]
    winv = proj(2688, 2816)
    wr_ref[:, 0:128] = wink
    wr_ref[:, 128:256] = winv
    winkv_ref[:, 0:128] = wink.astype(BF16)
    winkv_ref[:, 128:256] = winv.astype(BF16)
    winvt_ref[...] = winv.T.astype(BF16)
    glr = proj(2816, 2944).astype(BF16)
    z = jnp.dot(glr, wg_ref[...], preferred_element_type=F32) + bg_ref[...]
    la_ref[...] = _log_sigmoid(z) * (1.0 / GLA_TAU)
    gates = _sigmoid(proj(2944, 3072))
    gt_ref[...] = gates
    gtt_ref[...] = gates.T[0:32, :]


def _even_in(x2d, g, w_pad, wg_pad, bg, qn, kn1, kn2, tm):
    m = x2d.shape[0]
    widths = (256, 256, 512, 256, 512, 512, 512, 256, 128, 256, 256)
    dtypes = (F32, F32, F32, F32, F32, BF16, F32, F32, F32, BF16, BF16)
    t_heights = (512, 128, 128, 32)
    t_dtypes = (BF16, BF16, BF16, F32)
    full = lambda a: pl.BlockSpec(a.shape, lambda i: (0,) * a.ndim)
    gm512 = _block_ones(512, HEAD_DIM)
    gm128 = _block_ones(128, HEAD_DIM)
    ins = (x2d, g, w_pad, wg_pad, bg, qn, kn1, kn2, gm512, gm128)
    return pl.pallas_call(
        _even_in_kernel,
        grid=(m // tm,),
        in_specs=[pl.BlockSpec((tm, D_MODEL), lambda i: (i, 0))] + [full(a) for a in ins[1:]],
        out_specs=[pl.BlockSpec((tm, w), lambda i: (i, 0)) for w in widths]
                  + [pl.BlockSpec((h, tm), lambda i: (0, i)) for h in t_heights],
        out_shape=[jax.ShapeDtypeStruct((m, w), d) for w, d in zip(widths, dtypes)]
                  + [jax.ShapeDtypeStruct((h, m), d) for h, d in zip(t_heights, t_dtypes)],
        compiler_params=_cparams(("arbitrary",), VMEM_LIMIT_BYTES),
        name="even_in",
    )(*ins)


def _odd_in_kernel(x_ref, g_ref, w_ref, qn_ref, kn_ref, bf_ref, gm512_ref, tri_ref,
                   qs_ref, kv_ref, kb_ref, vb_ref, lf_ref, c_ref, u_ref, carry_ref):
    @pl.when(pl.program_id(1) == 0)
    def _():
        carry_ref[...] = jnp.zeros_like(carry_ref)

    xb = _rms_rows(x_ref[...], g_ref[...]).astype(BF16)

    def proj(lo, hi):
        return jnp.dot(xb, w_ref[:, lo:hi], preferred_element_type=F32)

    q = _group_rms(proj(0, 512), gm512_ref[...], HEAD_DIM) * qn_ref[...]
    qs_ref[...] = (q * SCALE).astype(BF16)
    k = _group_rms(proj(512, 1024), gm512_ref[...], HEAD_DIM) * kn_ref[...]
    v = proj(1024, 1536)
    kv_ref[:, 0:512] = k
    kv_ref[:, 512:1024] = v
    kb_ref[...] = k.astype(BF16)
    vb_ref[...] = v.astype(BF16)
    u_ref[...] = proj(1536, 2048) * _sigmoid(proj(2048, 2560))
    lf = _log_sigmoid(proj(2560, 2688) + bf_ref[...])
    lf_ref[...] = lf
    c = _dot_f32(tri_ref[...], lf) + carry_ref[0:1, :]
    c_ref[...] = c
    carry_ref[0:1, :] = c[-1:, :]


def _odd_in(x3d, g, w_pad, qn, kn, bf_pad, tm):
    b, t, _ = x3d.shape
    widths = (512, 1024, 512, 512, 128, 128, 512)
    dtypes = (BF16, F32, BF16, BF16, F32, F32, F32)
    gm512 = _block_ones(512, HEAD_DIM)
    tri = jnp.asarray(np.tril(np.ones((tm, tm), np.float32)))
    ins = (x3d, g, w_pad, qn, kn, bf_pad, gm512, tri)
    full = lambda a: pl.BlockSpec(a.shape, lambda bi, i: (0,) * a.ndim)
    return pl.pallas_call(
        _odd_in_kernel,
        grid=(b, t // tm),
        in_specs=[pl.BlockSpec((None, tm, D_MODEL), lambda bi, i: (bi, i, 0))] + [full(a) for a in ins[1:]],
        out_specs=[pl.BlockSpec((None, tm, w), lambda bi, i: (bi, i, 0)) for w in widths],
        out_shape=[jax.ShapeDtypeStruct((b, t, w), d) for w, d in zip(widths, dtypes)],
        scratch_shapes=[pltpu.VMEM((8, 128), F32)],
        compiler_params=_cparams(("arbitrary", "arbitrary"), VMEM_LIMIT_BYTES),
        name="odd_in",
    )(*ins)


def _out_proj_kernel(res_ref, a1_ref, a2_ref, w1_ref, w2_ref, o_ref):
    acc = jnp.dot(a1_ref[...].astype(BF16), w1_ref[...], preferred_element_type=F32)
    acc = acc + jnp.dot(a2_ref[...].astype(BF16), w2_ref[...], preferred_element_type=F32)
    o_ref[...] = res_ref[...] + acc


def _out_proj(res, a1, a2, w1, w2, tm):
    m = res.shape[0]
    row = lambda a: pl.BlockSpec((tm, a.shape[1]), lambda i: (i, 0))
    full = lambda a: pl.BlockSpec(a.shape, lambda i: (0, 0))
    return pl.pallas_call(
        _out_proj_kernel,
        grid=(m // tm,),
        in_specs=[row(res), row(a1), row(a2), full(w1), full(w2)],
        out_specs=row(res),
        out_shape=jax.ShapeDtypeStruct(res.shape, F32),
        compiler_params=_cparams(("arbitrary",), VMEM_LIMIT_BYTES),
        name="out_proj",
    )(res, a1, a2, w1, w2)


def _ffn_kernel(x_ref, g_ref, wg_ref, wu_ref, wo_ref, o_ref, xn_ref, acc_ref):
    j = pl.program_id(1)

    @pl.when(j == 0)
    def _():
        xn_ref[...] = _rms_rows(x_ref[...], g_ref[...]).astype(BF16)
        acc_ref[...] = jnp.zeros_like(acc_ref)

    xb = xn_ref[...]
    gate = jnp.dot(xb, wg_ref[...], preferred_element_type=F32)
    up = jnp.dot(xb, wu_ref[...], preferred_element_type=F32)
    h = (gate * _sigmoid(gate) * up).astype(BF16)
    acc_ref[...] += jnp.dot(h, wo_ref[...], preferred_element_type=F32)

    @pl.when(j == pl.num_programs(1) - 1)
    def _():
        o_ref[...] = x_ref[...] + acc_ref[...]


def _ffn(x2d, g, w_in, w_out, tm, n_chunks=2):
    m = x2d.shape[0]
    th = FFN_HIDDEN // n_chunks
    return pl.pallas_call(
        _ffn_kernel,
        grid=(m // tm, n_chunks),
        in_specs=[pl.BlockSpec((tm, D_MODEL), lambda i, j: (i, 0)),
                  pl.BlockSpec((1, D_MODEL), lambda i, j: (0, 0)),
                  pl.BlockSpec((D_MODEL, th), lambda i, j: (0, j)),
                  pl.BlockSpec((D_MODEL, th), lambda i, j: (0, n_chunks + j)),
                  pl.BlockSpec((th, D_MODEL), lambda i, j: (j, 0))],
        out_specs=pl.BlockSpec((tm, D_MODEL), lambda i, j: (i, 0)),
        out_shape=jax.ShapeDtypeStruct(x2d.shape, F32),
        scratch_shapes=[pltpu.VMEM((tm, D_MODEL), BF16), pltpu.VMEM((tm, D_MODEL), F32)],
        compiler_params=_cparams(("arbitrary", "arbitrary"), VMEM_LIMIT_BYTES),
        name="ffn",
    )(x2d, g, w_in, w_in, w_out)


def _mem_kv_kernel(m_ref, g_ref, w_ref, kn_ref, gm_ref, o_ref):
    xb = _rms_rows(m_ref[...], g_ref[...]).astype(BF16)
    kv = jnp.dot(xb, w_ref[...], preferred_element_type=F32)
    o_ref[0:XA_WIDTH, :] = (_group_rms(kv[:, 0:XA_WIDTH], gm_ref[...], HEAD_DIM) * kn_ref[...]).T
    o_ref[XA_WIDTH:, :] = kv[:, XA_WIDTH:].T


def _mem_kv(mem, g, wkv, kn):
    b = mem.shape[0]
    gm = _block_ones(XA_WIDTH, HEAD_DIM)
    full = lambda a: pl.BlockSpec(a.shape, lambda i: (0,) * a.ndim)
    return pl.pallas_call(
        _mem_kv_kernel,
        grid=(b,),
        in_specs=[pl.BlockSpec((None, MEM_LEN, D_MODEL), lambda i: (i, 0, 0)), full(g), full(wkv), full(kn), full(gm)],
        out_specs=pl.BlockSpec((None, 2 * XA_WIDTH, MEM_LEN), lambda i: (i, 0, 0)),
        out_shape=jax.ShapeDtypeStruct((b, 2 * XA_WIDTH, MEM_LEN), F32),
        compiler_params=_cparams(("arbitrary",)),
        name="mem_kv",
    )(mem, g, wkv, kn, gm)


def _xattn_kernel(x_ref, mkv_ref, g_ref, wq_ref, wo_ref, qn_ref, gm_ref, o_ref):
    x = x_ref[...]
    xb = _rms_rows(x, g_ref[...]).astype(BF16)
    q = jnp.dot(xb, wq_ref[...], preferred_element_type=F32)
    q = _group_rms(q, gm_ref[...], HEAD_DIM) * qn_ref[...]
    qb = (q * SCALE).astype(BF16)
    outs = []
    for h in range(XA_HEADS):
        kt = mkv_ref[h * HEAD_DIM:(h + 1) * HEAD_DIM, :].astype(BF16)
        vt = mkv_ref[XA_WIDTH + h * HEAD_DIM:XA_WIDTH + (h + 1) * HEAD_DIM, :].astype(BF16)
        s = jnp.dot(qb[:, h * HEAD_DIM:(h + 1) * HEAD_DIM], kt, preferred_element_type=F32)
        e = jnp.exp(s - jnp.max(s, axis=-1, keepdims=True))
        p = (e / jnp.sum(e, axis=-1, keepdims=True)).astype(BF16)
        outs.append(_dot_nt(p, vt))
    o = jnp.concatenate(outs, axis=-1).astype(BF16)
    o_ref[...] = x + jnp.dot(o, wo_ref[...], preferred_element_type=F32)


def _xattn(x3d, mkv_t, layer, g, wq, wo, qn, tm):
    b, t, _ = x3d.shape
    gm = _block_ones(XA_WIDTH, HEAD_DIM)
    full = lambda a: pl.BlockSpec(a.shape, lambda bi, i: (0,) * a.ndim)
    return pl.pallas_call(
        _xattn_kernel,
        grid=(b, t // tm),
        in_specs=[pl.BlockSpec((None, tm, D_MODEL), lambda bi, i: (bi, i, 0)),
                  pl.BlockSpec((None, None, 2 * XA_WIDTH, MEM_LEN), lambda bi, i: (layer, bi, 0, 0)),
                  full(g), full(wq), full(wo), full(qn), full(gm)],
        out_specs=pl.BlockSpec((None, tm, D_MODEL), lambda bi, i: (bi, i, 0)),
        out_shape=jax.ShapeDtypeStruct(x3d.shape, F32),
        compiler_params=_cparams(("arbitrary", "arbitrary"), VMEM_LIMIT_BYTES),
        name="xattn",
    )(x3d, mkv_t, g, wq, wo, qn, gm)


def _gla_kernel(q_ref, k_ref, v_ref, la_ref, og_ref, gn_ref, s0_ref, tri_ref, hsel_ref,
                o_ref, sfin_ref, st_ref, *, n_sub, n_valid):
    ti = pl.program_id(1)

    @pl.when(ti == 0)
    def _():
        st_ref[...] = s0_ref[...]

    tri = tri_ref[...]
    hsel = hsel_ref[...]
    gn = gn_ref[...]
    row = lax.broadcasted_iota(jnp.int32, (SUB, 1), 0)

    def sub_block(i, carry):
        r0 = pl.multiple_of(i * SUB, SUB)
        rows = pl.ds(r0, SUB)
        q = q_ref[rows, :]
        k = k_ref[rows, :]
        v = v_ref[rows, :]
        la = la_ref[rows, :]
        if n_valid is not None:
            live = (row + r0) < n_valid
            la = jnp.where(live, la, 0.0)
            k = jnp.where(live, k, 0.0)
        b = _dot_f32(tri, la)
        b_end = b[SUB - 1:SUB, :]
        qd = (q * jnp.exp(b)).astype(BF16)
        kd = (k * jnp.exp(b_end - b)).astype(BF16)
        vb = v.astype(BF16)
        tiles = []
        for s in range(SUB):
            e = jnp.exp(jnp.minimum(b - b[s:s + 1, :], 0.0))
            z = (q * k[s:s + 1, :]) * e
            tiles.append(jnp.where(row >= s, z, 0.0))
        att = jnp.dot(jnp.concatenate(tiles, axis=0).astype(BF16), hsel, preferred_element_type=F32)
        dec = jnp.exp(b_end)
        outs = []
        for h in range(GLA_HEADS):
            dk = slice(h * GLA_DK, (h + 1) * GLA_DK)
            dv = slice(h * GLA_DV, (h + 1) * GLA_DV)
            st = st_ref[h]
            o = _dot_nt(qd[:, dk], st.astype(BF16))
            for s in range(SUB):
                o = o + att[s * SUB:(s + 1) * SUB, h:h + 1] * v[s:s + 1, dv]
            st_ref[h] = st * dec[:, dk] + _dot_tn(vb[:, dv], kd[:, dk])
            outs.append(_rms_rows(o, gn))
        o_ref[rows, :] = jnp.concatenate(outs, axis=-1) * og_ref[rows, :]
        return carry

    lax.fori_loop(0, n_sub, sub_block, 0)

    @pl.when(ti == pl.num_programs(1) - 1)
    def _():
        sfin_ref[...] = st_ref[...]


def _gla(q, k, v, la, og, gn, s0t, tt, n_valid=None):
    b, t, _ = q.shape
    tri = jnp.asarray(np.tril(np.ones((SUB, SUB), np.float32)))
    hsel = jnp.asarray((np.arange(256)[:, None] // GLA_DK == np.arange(128)[None, :]).astype(np.float32), dtype=BF16)
    seq = lambda w: pl.BlockSpec((None, tt, w), lambda bi, i: (bi, i, 0))
    full = lambda a: pl.BlockSpec(a.shape, lambda bi, i: (0,) * a.ndim)
    st_spec = pl.BlockSpec((None, GLA_HEADS, GLA_DV, GLA_DK), lambda bi, i: (bi, 0, 0, 0))
    return pl.pallas_call(
        functools.partial(_gla_kernel, n_sub=tt // SUB, n_valid=n_valid),
        grid=(b, t // tt),
        in_specs=[seq(256), seq(256), seq(512), seq(256), seq(512), full(gn), st_spec, full(tri), full(hsel)],
        out_specs=[seq(512), st_spec],
        out_shape=[jax.ShapeDtypeStruct((b, t, GLA_WIDTH), F32),
                   jax.ShapeDtypeStruct((b, GLA_HEADS, GLA_DV, GLA_DK), F32)],
        scratch_shapes=[pltpu.VMEM((GLA_HEADS, GLA_DV, GLA_DK), F32)],
        compiler_params=_cparams(("arbitrary", "arbitrary")),
        name="gla",
    )(q, k, v, la, og, gn, s0t, tri, hsel)


def _conv_kernel(u_ref, st0_ref, w_ref, b_ref, g_ref, beta_ref, o_ref, st_ref, ext_ref, *, tt, n_valid):
    ti = pl.program_id(1)
    ctx = CONV_WIDTH - 1

    @pl.when(ti == 0)
    def _():
        ext_ref[0:8, :] = jnp.zeros((8, CONV_CH), F32)
        ext_ref[pl.ds(2, ctx), :] = st0_ref[...]

    ext_ref[pl.ds(32, tt), :] = u_ref[...]
    acc = jnp.zeros((tt, CONV_CH), F32)
    for w in range(CONV_WIDTH):
        acc = acc + ext_ref[pl.ds(2 + w, tt), :] * w_ref[w:w + 1, :]
    y = acc + b_ref[...]
    mu = jnp.mean(y, axis=-1, keepdims=True)
    var = jnp.mean(jnp.square(y - mu), axis=-1, keepdims=True)
    ln = (y - mu) * lax.rsqrt(var + EPS) * g_ref[...] + beta_ref[...]
    o_ref[...] = ln * _sigmoid(ln)

    @pl.when(ti == pl.num_programs(1) - 1)
    def _():
        st_ref[...] = ext_ref[pl.ds(32 + n_valid - ctx, ctx), :]

    ext_ref[0:32, :] = ext_ref[pl.ds(tt, 32), :]


def _conv(u, st0, w, b, g, beta, tt, n_valid):
    bsz, t, _ = u.shape
    ctx = CONV_WIDTH - 1
    full = lambda a: pl.BlockSpec(a.shape, lambda bi, i: (0,) * a.ndim)
    st_spec = pl.BlockSpec((None, ctx, CONV_CH), lambda bi, i: (bi, 0, 0))
    return pl.pallas_call(
        functools.partial(_conv_kernel, tt=tt, n_valid=n_valid),
        grid=(bsz, t // tt),
        in_specs=[pl.BlockSpec((None, tt, CONV_CH), lambda bi, i: (bi, i, 0)), st_spec,
                  full(w), full(b), full(g), full(beta)],
        out_specs=[pl.BlockSpec((None, tt, CONV_CH), lambda bi, i: (bi, i, 0)), st_spec],
        out_shape=[jax.ShapeDtypeStruct(u.shape, F32), jax.ShapeDtypeStruct((bsz, ctx, CONV_CH), F32)],
        scratch_shapes=[pltpu.VMEM((32 + max(tt, 32), CONV_CH), F32)],
        compiler_params=_cparams(("arbitrary", "arbitrary")),
        name="conv",
    )(u, st0, w, b, g, beta)


def _fox_kernel(q_ref, k_ref, v_ref, cq_ref, ckt_ref, o_ref, m_ref, l_ref, acc_ref, *, tq, tk):
    qi = pl.program_id(1)
    ki = pl.program_id(2)
    last = (qi * tq + tq - 1) // tk

    @pl.when(ki == 0)
    def _():
        m_ref[...] = jnp.full_like(m_ref, NEG)
        l_ref[...] = jnp.zeros_like(l_ref)
        acc_ref[...] = jnp.zeros_like(acc_ref)

    @pl.when(ki <= last)
    def _():
        t_pos = qi * tq + lax.broadcasted_iota(jnp.int32, (tq, tk), 0)
        s_pos = ki * tk + lax.broadcasted_iota(jnp.int32, (tq, tk), 1)
        mask = s_pos <= t_pos
        for h in range(FOX_HEADS):
            cols = slice(h * HEAD_DIM, (h + 1) * HEAD_DIM)
            s = _dot_nt(q_ref[:, cols], k_ref[:, cols])
            s = s + cq_ref[:, h:h + 1] - ckt_ref[h:h + 1, :]
            s = jnp.where(mask, s, NEG)
            m_old = m_ref[h]
            m_new = jnp.maximum(m_old, jnp.max(s, axis=-1, keepdims=True))
            p = jnp.where(mask, jnp.exp(s - m_new), 0.0)
            alpha = jnp.exp(m_old - m_new)
            l_ref[h] = alpha * l_ref[h] + jnp.sum(p, axis=-1, keepdims=True)
            acc_ref[:, cols] = alpha * acc_ref[:, cols] + jnp.dot(p.astype(BF16), v_ref[:, cols],
                                                                  preferred_element_type=F32)
            m_ref[h] = m_new

    @pl.when(ki == pl.num_programs(2) - 1)
    def _():
        for h in range(FOX_HEADS):
            cols = slice(h * HEAD_DIM, (h + 1) * HEAD_DIM)
            o_ref[:, cols] = acc_ref[:, cols] / jnp.maximum(l_ref[h], 1e-30)


def _fox_prompt(qs, kb, vb, c8, ct, tq, tk):
    b, t, _ = qs.shape
    kv_map = lambda bi, qi, ki: (bi, jnp.minimum(ki, (qi * tq + tq - 1) // tk), 0)
    return pl.pallas_call(
        functools.partial(_fox_kernel, tq=tq, tk=tk),
        grid=(b, t // tq, t // tk),
        in_specs=[pl.BlockSpec((None, tq, FOX_WIDTH), lambda bi, qi, ki: (bi, qi, 0)),
                  pl.BlockSpec((None, tk, FOX_WIDTH), kv_map),
                  pl.BlockSpec((None, tk, FOX_WIDTH), kv_map),
                  pl.BlockSpec((None, tq, FOX_HEADS), lambda bi, qi, ki: (bi, qi, 0)),
                  pl.BlockSpec((None, FOX_HEADS, tk),
                               lambda bi, qi, ki: (bi, 0, jnp.minimum(ki, (qi * tq + tq - 1) // tk)))],
        out_specs=pl.BlockSpec((None, tq, FOX_WIDTH), lambda bi, qi, ki: (bi, qi, 0)),
        out_shape=jax.ShapeDtypeStruct((b, t, FOX_WIDTH), F32),
        scratch_shapes=[pltpu.VMEM((FOX_HEADS, tq, 1), F32), pltpu.VMEM((FOX_HEADS, tq, 1), F32),
                        pltpu.VMEM((tq, FOX_WIDTH), F32)],
        compiler_params=_cparams(("arbitrary", "arbitrary", "arbitrary"), VMEM_LIMIT_BYTES),
        name="fox_prompt",
    )(qs, kb, vb, c8, ct)


def _compress_compute(src_refs, pe_ref, w1_ref, w2_ref, kn_ref, ck_ref, cv_ref, sh_ref, nseg):
    sh_ref[pl.ds(nseg, 8), :] = jnp.zeros((8, CMP_HIDDEN), F32)
    for j, src_ref in enumerate(src_refs):
        a = [jnp.zeros((nseg, CMP_HIDDEN), F32) for _ in range(NSA_KV_HEADS)]
        bm = [jnp.zeros((nseg, CMP_HIDDEN), F32) for _ in range(NSA_KV_HEADS)]
        for p in range(CMP_STRIDE):
            p2 = CMP_STRIDE + p
            x2 = src_ref[pl.ds(p, nseg, stride=CMP_STRIDE), :]
            wa = w1_ref[j, p * HEAD_DIM:(p + 1) * HEAD_DIM, :]
            wb = w1_ref[j, p2 * HEAD_DIM:(p2 + 1) * HEAD_DIM, :]
            for g in range(NSA_KV_HEADS):
                xp = x2[:, g * HEAD_DIM:(g + 1) * HEAD_DIM]
                a[g] = a[g] + jnp.dot((xp + pe_ref[j, p:p + 1, :]).astype(BF16), wa, preferred_element_type=F32)
                bm[g] = bm[g] + jnp.dot((xp + pe_ref[j, p2:p2 + 1, :]).astype(BF16), wb, preferred_element_type=F32)
        for g in range(NSA_KV_HEADS):
            sh_ref[pl.ds(0, nseg), :] = bm[g]
            x = a[g] + sh_ref[pl.ds(1, nseg), :]
            hid = x * (0.5 * (1.0 + jnp.tanh(math.sqrt(2.0 / math.pi) * (x + 0.044715 * (x * x * x)))))
            ckv = jnp.dot(hid.astype(BF16), w2_ref[j], preferred_element_type=F32)
            if j == 0:
                ck_ref[g] = _rms_rows(ckv, kn_ref[...]).astype(BF16)
            else:
                cv_ref[g] = ckv.astype(BF16)


def _compress_prompt_kernel(xk_ref, xv_ref, pe_ref, w1_ref, w2_ref, kn_ref, ck_ref, cv_ref, sh_ref, *, nseg):
    _compress_compute((xk_ref, xv_ref), pe_ref, w1_ref, w2_ref, kn_ref, ck_ref, cv_ref, sh_ref, nseg)


def _compress_prompt(kvr, pe, w1, w2, kn):
    b, t, _ = kvr.shape
    nseg = t // CMP_STRIDE
    full = lambda a: pl.BlockSpec(a.shape, lambda i: (0,) * a.ndim)
    o_spec = pl.BlockSpec((None, NSA_KV_HEADS, nseg, HEAD_DIM), lambda i: (i, 0, 0, 0))
    o_shape = jax.ShapeDtypeStruct((b, NSA_KV_HEADS, nseg, HEAD_DIM), BF16)
    return pl.pallas_call(
        functools.partial(_compress_prompt_kernel, nseg=nseg),
        grid=(b,),
        in_specs=[pl.BlockSpec((None, t, LANES), lambda i: (i, 0, 0)), pl.BlockSpec((None, t, LANES), lambda i: (i, 0, 1)),
                  full(pe), full(w1), full(w2), full(kn)],
        out_specs=[o_spec, o_spec],
        out_shape=[o_shape, o_shape],
        scratch_shapes=[pltpu.VMEM((nseg + 8, CMP_HIDDEN), F32)],
        compiler_params=_cparams(("arbitrary",), VMEM_LIMIT_BYTES),
        name="nsa_compress",
    )(kvr, kvr, pe, w1, w2, kn)


def _stack_heads(qs_ref, kh):
    parts = [qs_ref[:, (kh * NSA_GROUP + g) * HEAD_DIM:(kh * NSA_GROUP + g + 1) * HEAD_DIM].astype(F32)
             for g in range(NSA_GROUP)]
    return jnp.concatenate(parts, axis=0).astype(BF16)


def _cmp_topk_kernel(qs_ref, ck_ref, cv_ref, farcol_ref, chi_ref, clo_ref, pool_ref, ocmp_ref, msk_ref,
                     *, tq, nseg, q_base, n_pick, n_blk, mask_t):
    qi = pl.program_id(1)
    G = NSA_GROUP
    q0 = q_base + qi * tq
    nbase = q0 // CMP_STRIDE - 16
    place = (lax.broadcasted_iota(jnp.int32, (32, nseg), 1) - lax.broadcasted_iota(jnp.int32, (32, nseg), 0)) == nbase
    place = jnp.where(place, 1.0, 0.0).astype(BF16)
    t1 = q0 + lax.broadcasted_iota(jnp.int32, (tq, 1), 0)
    t4 = jnp.concatenate([t1] * G, axis=0)
    n_i = lax.broadcasted_iota(jnp.int32, (G * tq, nseg), 1)
    valid = (n_i * CMP_STRIDE + (CMP_BLOCK - 1) <= t4) & (n_i <= nseg - 2)
    blk = lax.broadcasted_iota(jnp.int32, (tq, N_SELBLK), 1)
    cur = lax.shift_right_logical(t1, 6)
    forced = (blk == 0) | (blk == cur) | (blk == cur - 1)
    for kh in range(NSA_KV_HEADS):
        q4 = _stack_heads(qs_ref, kh)
        s = _dot_nt(q4, ck_ref[kh]) + farcol_ref[kh]
        s = s + jnp.dot(chi_ref[kh], place, preferred_element_type=F32) + jnp.dot(clo_ref[kh], place, preferred_element_type=F32)
        s = jnp.where(valid, s, NEG)
        e = jnp.where(valid, jnp.exp(s - jnp.max(s, axis=-1, keepdims=True)), 0.0)
        p = e / jnp.maximum(jnp.sum(e, axis=-1, keepdims=True), 1e-30)
        o = jnp.dot(p.astype(BF16), cv_ref[kh], preferred_element_type=F32)
        for g in range(G):
            h = kh * G + g
            ocmp_ref[:, h * HEAD_DIM:(h + 1) * HEAD_DIM] = o[g * tq:(g + 1) * tq]
        imp = p[0:tq] + p[tq:2 * tq] + p[2 * tq:3 * tq] + p[3 * tq:4 * tq]
        pooled = _dot_f32(imp, pool_ref[...])
        score = jnp.where((blk > cur) | (blk >= n_blk), -1e30, jnp.where(forced, 1e30, pooled))
        sel = jnp.zeros((tq, N_SELBLK), F32)
        for _ in range(n_pick):
            mx = jnp.max(score, axis=-1, keepdims=True)
            first = jnp.min(jnp.where(score == mx, blk, N_SELBLK), axis=-1, keepdims=True)
            pick = blk == first
            sel = jnp.where(pick, 1.0, sel)
            score = jnp.where(pick, -3e38, score)
        msk_ref[kh] = sel.T if mask_t else sel


def _cmp_bias_tables(tbl, tq):
    tr = jnp.arange(tq)[:, None]
    i = jnp.arange(32)[None, :]
    dist = tr + 16 * CMP_STRIDE - CMP_STRIDE * i - (CMP_BLOCK - 1)
    near = tbl[:, _j_rel_bucket(dist)]
    far = tbl[:, NUM_BUCKETS - 1]
    corr = (near - far[:, None, None]).reshape(NSA_KV_HEADS, NSA_GROUP * tq, 32)
    hi = corr.astype(BF16)
    lo = (corr - hi.astype(F32)).astype(BF16)
    farcol = jnp.broadcast_to(far[:, None, None], (NSA_HEADS, tq, 1)).reshape(NSA_KV_HEADS, NSA_GROUP * tq, 1)
    return farcol, hi, lo


def _cmp_topk(qs, ck, cv, tbl, tq, q_base, n_pick, n_blk, mask_t):
    b, t, _ = qs.shape
    nseg = ck.shape[2]
    assert tq <= 256
    farcol, chi, clo = _cmp_bias_tables(tbl, tq)
    pool = jnp.asarray((np.arange(nseg)[:, None] // SEL_RATIO == np.arange(N_SELBLK)[None, :]).astype(np.float32))
    full = lambda a: pl.BlockSpec(a.shape, lambda bi, i: (0,) * a.ndim)
    c_spec = pl.BlockSpec((None, NSA_KV_HEADS, nseg, HEAD_DIM), lambda bi, i: (bi, 0, 0, 0))
    if mask_t:
        m_spec = pl.BlockSpec((None, NSA_KV_HEADS, N_SELBLK, tq), lambda bi, i: (bi, 0, 0, i))
        m_shape = (b, NSA_KV_HEADS, N_SELBLK, t)
    else:
        m_spec = pl.BlockSpec((None, NSA_KV_HEADS, tq, N_SELBLK), lambda bi, i: (bi, 0, i, 0))
        m_shape = (b, NSA_KV_HEADS, t, N_SELBLK)
    return pl.pallas_call(
        functools.partial(_cmp_topk_kernel, tq=tq, nseg=nseg, q_base=q_base, n_pick=n_pick, n_blk=n_blk,
                          mask_t=mask_t),
        grid=(b, t // tq),
        in_specs=[pl.BlockSpec((None, tq, NSA_WIDTH), lambda bi, i: (bi, i, 0)), c_spec, c_spec,
                  full(farcol), full(chi), full(clo), full(pool)],
        out_specs=[pl.BlockSpec((None, tq, NSA_WIDTH), lambda bi, i: (bi, i, 0)), m_spec],
        out_shape=[jax.ShapeDtypeStruct((b, t, NSA_WIDTH), F32), jax.ShapeDtypeStruct(m_shape, F32)],
        compiler_params=_cparams(("arbitrary", "arbitrary"), VMEM_LIMIT_BYTES),
        name="nsa_cmp_topk",
    )(qs, ck, cv, farcol, chi, clo, pool)


def _online_update(s, valid, v, m_ref, l_ref, acc_ref, v_t=False):
    s = jnp.where(valid, s, NEG)
    m_old = m_ref[...]
    m_new = jnp.maximum(m_old, jnp.max(s, axis=-1, keepdims=True))
    p = jnp.where(valid, jnp.exp(s - m_new), 0.0)
    alpha = jnp.exp(m_old - m_new)
    l_ref[...] = alpha * l_ref[...] + jnp.sum(p, axis=-1, keepdims=True)
    pb = p.astype(BF16)
    pv = _dot_nt(pb, v) if v_t else jnp.dot(pb, v, preferred_element_type=F32)
    acc_ref[...] = alpha * acc_ref[...] + pv
    m_ref[...] = m_new


def _selwin_kernel(qt_ref, selk_ref, selvt_ref, wink_ref, winvt_ref, mskt_ref, bias_ref, ocmp_ref, gtt_ref, o_ref,
                   m_ref, l_ref, acc_ref, ot_ref, *, tq):
    qi = pl.program_id(1)
    G = NSA_GROUP
    blocks_per_tile = tq // SEL_BLOCK
    s_rel = lax.broadcasted_iota(jnp.int32, (tq, tq), 0)
    t_rel = lax.broadcasted_iota(jnp.int32, (tq, tq), 1)
    causal = s_rel <= t_rel

    def init():
        m_ref[...] = jnp.full_like(m_ref, NEG)
        l_ref[...] = jnp.zeros_like(l_ref)
        acc_ref[...] = jnp.zeros_like(acc_ref)

    def update(kh, k, vt, bias_idx, valid):
        for g in range(G):
            h = kh * G + g
            cols = slice(g * tq, (g + 1) * tq)
            s = jnp.dot(k, qt_ref[h * HEAD_DIM:(h + 1) * HEAD_DIM, :], preferred_element_type=F32)
            s = s + bias_ref[kh, bias_idx, :, cols]
            if valid is not None:
                s = jnp.where(valid, s, NEG)
            m_old = m_ref[0:1, cols]
            m_new = jnp.maximum(m_old, jnp.max(s, axis=0, keepdims=True))
            p = jnp.exp(s - m_new)
            if valid is not None:
                p = jnp.where(valid, p, 0.0)
            alpha = jnp.exp(m_old - m_new)
            l_ref[0:1, cols] = alpha * l_ref[0:1, cols] + jnp.sum(p, axis=0, keepdims=True)
            acc_ref[:, cols] = alpha * acc_ref[:, cols] + jnp.dot(vt, p.astype(BF16), preferred_element_type=F32)
            m_ref[0:1, cols] = m_new

    def finish(kh, gate_row):
        for g in range(G):
            h = kh * G + g
            cols = slice(g * tq, (g + 1) * tq)
            rows = slice(h * HEAD_DIM, (h + 1) * HEAD_DIM)
            o = acc_ref[:, cols] / jnp.maximum(l_ref[0:1, cols], 1e-30)
            r = 3 * h + gate_row
            ot_ref[rows, :] += gtt_ref[r:r + 1, :] * o

    oc_t = ocmp_ref[...].T
    for h in range(NSA_HEADS):
        rows = slice(h * HEAD_DIM, (h + 1) * HEAD_DIM)
        ot_ref[rows, :] = gtt_ref[3 * h:3 * h + 1, :] * oc_t[rows, :]

    for kh in range(NSA_KV_HEADS):
        kcols = slice(kh * HEAD_DIM, (kh + 1) * HEAD_DIM)
        vrows = slice(kh * HEAD_DIM, (kh + 1) * HEAD_DIM)

        def sel_valid(j):
            parts = [jnp.broadcast_to(mskt_ref[kh, pl.ds(j * blocks_per_tile + i, 1), :], (SEL_BLOCK, tq))
                     for i in range(blocks_per_tile)]
            return jnp.concatenate(parts, axis=0) > 0.5

        def sel_tile(j, bias_idx, extra):
            start = pl.multiple_of(j * tq, tq)
            valid = sel_valid(j)
            if extra is not None:
                valid = valid & extra
            update(kh, selk_ref[pl.ds(start, tq), kcols], selvt_ref[vrows, pl.ds(start, tq)], bias_idx, valid)

        def win_tile(j, bias_idx, valid):
            start = pl.multiple_of(j * tq, tq)
            update(kh, wink_ref[pl.ds(start, tq), kcols], winvt_ref[vrows, pl.ds(start, tq)], bias_idx, valid)

        init()

        def far_body(j, c):
            sel_tile(j, 2, None)
            return c

        lax.fori_loop(0, jnp.maximum(qi - 1, 0), far_body, 0)

        @pl.when(qi >= 1)
        def _():
            sel_tile(qi - 1, 1, None)

        sel_tile(qi, 0, causal)
        finish(kh, 1)

        init()

        @pl.when(qi >= 2)
        def _():
            win_tile(qi - 2, 2, s_rel >= t_rel)

        @pl.when(qi >= 1)
        def _():
            win_tile(qi - 1, 1, None)

        win_tile(qi, 0, causal)
        finish(kh, 2)

    o_ref[...] = ot_ref[...].T


def _selwin_bias_tables(tbl, tq):
    sr = jnp.arange(tq)[:, None]
    tr = jnp.arange(tq)[None, :]
    near0 = tbl[:, _j_rel_bucket(tr - sr)]
    near1 = tbl[:, _j_rel_bucket(tr - sr + tq)]
    far = jnp.broadcast_to(tbl[:, NUM_BUCKETS - 1][:, None, None], near0.shape)
    b = jnp.stack([near0, near1, far], axis=1)
    b = b.reshape(NSA_KV_HEADS, NSA_GROUP, 3, tq, tq).transpose(0, 2, 3, 1, 4)
    return b.reshape(NSA_KV_HEADS, 3, tq, NSA_GROUP * tq)


def _selwin_prompt(nqst, selkv, selvt, winkv, winvt, mskt, tbl, ocmp, gtt, b, t, tq):
    assert tq >= NSA_WINDOW // 2 and tq >= MAX_DISTANCE and tq % SEL_BLOCK == 0
    bias = _selwin_bias_tables(tbl, tq)
    nq = t // tq
    col_tile = lambda h: pl.BlockSpec((h, tq), lambda bi, i: (0, bi * nq + i))
    row_tile = lambda w: pl.BlockSpec((tq, w), lambda bi, i: (bi * nq + i, 0))
    return pl.pallas_call(
        functools.partial(_selwin_kernel, tq=tq),
        grid=(b, nq),
        in_specs=[col_tile(NSA_WIDTH),
                  pl.BlockSpec((t, LANES), lambda bi, i: (bi, 0)), pl.BlockSpec((LANES, t), lambda bi, i: (0, bi)),
                  pl.BlockSpec((t, LANES), lambda bi, i: (bi, 0)), pl.BlockSpec((LANES, t), lambda bi, i: (0, bi)),
                  pl.BlockSpec((None, NSA_KV_HEADS, N_SELBLK, tq), lambda bi, i: (bi, 0, 0, i)),
                  pl.BlockSpec(bias.shape, lambda bi, i: (0, 0, 0, 0)),
                  row_tile(NSA_WIDTH), col_tile(32)],
        out_specs=row_tile(NSA_WIDTH),
        out_shape=jax.ShapeDtypeStruct((b * t, NSA_WIDTH), F32),
        scratch_shapes=[pltpu.VMEM((8, NSA_GROUP * tq), F32), pltpu.VMEM((8, NSA_GROUP * tq), F32),
                        pltpu.VMEM((HEAD_DIM, NSA_GROUP * tq), F32), pltpu.VMEM((NSA_WIDTH, tq), F32)],
        compiler_params=_cparams(("arbitrary", "arbitrary"), VMEM_LIMIT_BYTES),
        name="nsa_selwin",
    )(nqst, selkv, selvt, winkv, winvt, mskt, bias, ocmp, gtt)


PAGES_PER_STEP = 4
PAGE = 128


def _feature_major(cache):
    nd = cache.ndim
    t = jnp.transpose(cache, (0, 1) + tuple(range(3, nd)) + (2,))
    return t.reshape(cache.shape[0], cache.shape[1], -1, cache.shape[2])


def _page_specs(layer, rows, row_block):
    return [pl.BlockSpec((None, None, rows, PAGE), functools.partial(
        lambda bi, j, pt, r: (layer, pt[bi, j * PAGES_PER_STEP + r], row_block, 0), r=r)) for r in range(PAGES_PER_STEP)]


def _compress_sample_kernel(pt_ref, p0_ref, p1_ref, p2_ref, p3_ref, pe_ref, w1_ref, w2_ref, kn_ref,
                            ck_ref, cv_ref, srck_ref, srcv_ref, sh_ref, *, nseg):
    j = pl.program_id(1)
    for r, p_ref in enumerate((p0_ref, p1_ref, p2_ref, p3_ref)):
        rows = pl.ds(pl.multiple_of((j * PAGES_PER_STEP + r) * PAGE, PAGE), PAGE)
        srck_ref[rows, :] = p_ref[0:LANES, :].T
        srcv_ref[rows, :] = p_ref[LANES:2 * LANES, :].T

    @pl.when(j == pl.num_programs(1) - 1)
    def _():
        _compress_compute((srck_ref, srcv_ref), pe_ref, w1_ref, w2_ref, kn_ref, ck_ref, cv_ref, sh_ref, nseg)


def _compress_sample(cache_t, layer, page_table, pe, w1, w2, kn):
    sb, n_pages = page_table.shape
    past = n_pages * PAGE
    nseg = past // CMP_STRIDE
    full = lambda a: pl.BlockSpec(a.shape, lambda bi, j, pt: (0,) * a.ndim)
    o_spec = pl.BlockSpec((None, NSA_KV_HEADS, nseg, HEAD_DIM), lambda bi, j, pt: (bi, 0, 0, 0))
    o_shape = jax.ShapeDtypeStruct((sb, NSA_KV_HEADS, nseg, HEAD_DIM), BF16)
    return pl.pallas_call(
        functools.partial(_compress_sample_kernel, nseg=nseg),
        grid_spec=pltpu.PrefetchScalarGridSpec(
            num_scalar_prefetch=1, grid=(sb, n_pages // PAGES_PER_STEP),
            in_specs=_page_specs(layer, 256, 0) + [full(pe), full(w1), full(w2), full(kn)],
            out_specs=[o_spec, o_spec],
            scratch_shapes=[pltpu.VMEM((past, LANES), F32), pltpu.VMEM((past, LANES), F32),
                            pltpu.VMEM((nseg + 8, CMP_HIDDEN), F32)]),
        out_shape=[o_shape, o_shape],
        compiler_params=_cparams(("arbitrary", "arbitrary"), VMEM_LIMIT_BYTES),
        name="nsa_compress_sample",
    )(page_table, cache_t, cache_t, cache_t, cache_t, pe, w1, w2, kn)


def _selwin_sample_kernel(pt_ref, qs_ref, p0_ref, p1_ref, p2_ref, p3_ref, msk_ref, newkv_ref, winst_ref, newwr_ref,
                          bsel_ref, bnew_ref, bwin_ref, ocmp_ref, gt_ref, o_ref,
                          kv_ref, m_ref, l_ref, acc_ref, osel_ref, owin_ref, *, tk, n_tiles):
    j = pl.program_id(1)
    G = NSA_GROUP
    R = SAMPLE_ROWS
    for r, p_ref in enumerate((p0_ref, p1_ref, p2_ref, p3_ref)):
        kv_ref[:, pl.ds(pl.multiple_of((j * PAGES_PER_STEP + r) * PAGE, PAGE), PAGE)] = p_ref[...].astype(BF16)

    @pl.when(j == pl.num_programs(1) - 1)
    def _():
        blocks_per_tile = tk // SEL_BLOCK
        eb = lax.broadcasted_iota(jnp.int32, (N_SELBLK, tk), 0)
        el = lax.shift_right_logical(lax.broadcasted_iota(jnp.int32, (N_SELBLK, tk), 1), 6)
        rq = lax.broadcasted_iota(jnp.int32, (G * R, 1), 0) & (R - 1)
        new_valid = lax.broadcasted_iota(jnp.int32, (G * R, R), 1) <= rq
        win_valid = lax.broadcasted_iota(jnp.int32, (G * R, NSA_WINDOW), 1) >= rq

        def init():
            m_ref[...] = jnp.full_like(m_ref, NEG)
            l_ref[...] = jnp.zeros_like(l_ref)
            acc_ref[...] = jnp.zeros_like(acc_ref)

        def finish(dst_ref, kh):
            o = acc_ref[...] / jnp.maximum(l_ref[...], 1e-30)
            for g in range(G):
                h = kh * G + g
                dst_ref[:, h * HEAD_DIM:(h + 1) * HEAD_DIM] = o[g * R:(g + 1) * R]

        for kh in range(NSA_KV_HEADS):
            kcols = slice(kh * HEAD_DIM, (kh + 1) * HEAD_DIM)
            vcols = slice(128 + kh * HEAD_DIM, 128 + (kh + 1) * HEAD_DIM)
            q4 = _stack_heads(qs_ref, kh)
            mskb = msk_ref[kh].astype(BF16)

            def sel_tile(jt, bias):
                toks = pl.ds(jt * tk if isinstance(jt, int) else pl.multiple_of(jt * tk, tk), tk)
                expand = jnp.where(eb == jt * blocks_per_tile + el, 1.0, 0.0).astype(BF16)
                mt = jnp.dot(mskb, expand, preferred_element_type=F32) > 0.5
                valid = jnp.concatenate([mt] * G, axis=0)
                s = jnp.dot(q4, kv_ref[kcols, toks], preferred_element_type=F32) + bias
                _online_update(s, valid, kv_ref[vcols, toks], m_ref, l_ref, acc_ref, v_t=True)

            init()

            def far_body(jt, c):
                sel_tile(jt, bsel_ref[kh, 1])
                return c

            lax.fori_loop(0, n_tiles - 1, far_body, 0)
            sel_tile(n_tiles - 1, bsel_ref[kh, 0])
            knew = newkv_ref[:, 256 + kh * HEAD_DIM:256 + (kh + 1) * HEAD_DIM].astype(BF16)
            vnew = newkv_ref[:, 384 + kh * HEAD_DIM:384 + (kh + 1) * HEAD_DIM].astype(BF16)
            _online_update(_dot_nt(q4, knew) + bnew_ref[kh], new_valid, vnew, m_ref, l_ref, acc_ref)
            finish(osel_ref, kh)

            init()
            kwin = winst_ref[kcols, :].astype(BF16)
            vwin = winst_ref[vcols, :].astype(BF16)
            _online_update(jnp.dot(q4, kwin, preferred_element_type=F32) + bwin_ref[kh], win_valid, vwin,
                           m_ref, l_ref, acc_ref, v_t=True)
            knew = newwr_ref[:, kcols].astype(BF16)
            vnew = newwr_ref[:, vcols].astype(BF16)
            _online_update(_dot_nt(q4, knew) + bnew_ref[kh], new_valid, vnew, m_ref, l_ref, acc_ref)
            finish(owin_ref, kh)

        for h in range(NSA_HEADS):
            cols = slice(h * HEAD_DIM, (h + 1) * HEAD_DIM)
            o_ref[:, cols] = (gt_ref[:, 3 * h:3 * h + 1] * ocmp_ref[:, cols]
                              + gt_ref[:, 3 * h + 1:3 * h + 2] * osel_ref[:, cols]
                              + gt_ref[:, 3 * h + 2:3 * h + 3] * owin_ref[:, cols])


def _sample_bias_tables(tbl, tk):
    R = SAMPLE_ROWS
    r = jnp.arange(R)[:, None]
    stack = lambda a: a.reshape(NSA_KV_HEADS, NSA_GROUP * R, a.shape[-1])
    last = tbl[:, _j_rel_bucket(tk + r - jnp.arange(tk)[None, :])]
    far = jnp.broadcast_to(tbl[:, NUM_BUCKETS - 1][:, None, None], last.shape)
    new = tbl[:, _j_rel_bucket(r - jnp.arange(R)[None, :])]
    win = tbl[:, _j_rel_bucket(NSA_WINDOW + r - jnp.arange(NSA_WINDOW)[None, :])]
    return jnp.stack([stack(last), stack(far)], axis=1), stack(new), stack(win)


def _selwin_sample(qs, cache_t, layer, page_table, msk, newkv, win_t, newwr, tbl, ocmp, gt):
    sb, n_pages = page_table.shape
    past = n_pages * PAGE
    tk = 256
    assert win_t.shape[-1] == NSA_WINDOW and past >= NSA_WINDOW and tk >= MAX_DISTANCE
    bsel, bnew, bwin = _sample_bias_tables(tbl, tk)
    R = SAMPLE_ROWS
    full = lambda a: pl.BlockSpec(a.shape, lambda bi, j, pt: (0,) * a.ndim)
    seq = lambda a: pl.BlockSpec((None,) + a.shape[1:], lambda bi, j, pt: (bi,) + (0,) * (a.ndim - 1))
    win_spec = pl.BlockSpec((None, None) + win_t.shape[2:], lambda bi, j, pt: (layer, bi, 0, 0))
    return pl.pallas_call(
        functools.partial(_selwin_sample_kernel, tk=tk, n_tiles=past // tk),
        grid_spec=pltpu.PrefetchScalarGridSpec(
            num_scalar_prefetch=1, grid=(sb, n_pages // PAGES_PER_STEP),
            in_specs=[seq(qs)] + _page_specs(layer, 256, 1) + [seq(msk), seq(newkv), win_spec, seq(newwr),
                                                               full(bsel), full(bnew), full(bwin), seq(ocmp), seq(gt)],
            out_specs=pl.BlockSpec((None, R, NSA_WIDTH), lambda bi, j, pt: (bi, 0, 0)),
            scratch_shapes=[pltpu.VMEM((256, past), BF16),
                            pltpu.VMEM((NSA_GROUP * R, 1), F32), pltpu.VMEM((NSA_GROUP * R, 1), F32),
                            pltpu.VMEM((NSA_GROUP * R, HEAD_DIM), F32),
                            pltpu.VMEM((R, NSA_WIDTH), F32), pltpu.VMEM((R, NSA_WIDTH), F32)]),
        out_shape=jax.ShapeDtypeStruct((sb, R, NSA_WIDTH), F32),
        compiler_params=_cparams(("arbitrary", "arbitrary"), VMEM_LIMIT_BYTES),
        name="nsa_selwin_sample",
    )(page_table, qs, cache_t, cache_t, cache_t, cache_t, msk, newkv, win_t, newwr, bsel, bnew, bwin, ocmp, gt)


def _fox_suffix_kernel(pt_ref, p0_ref, p1_ref, p2_ref, p3_ref, sl_ref, d_ref, carry_ref):
    @pl.when(pl.program_id(1) == 0)
    def _():
        carry_ref[...] = jnp.zeros_like(carry_ref)

    carry = carry_ref[:, 0:1]
    for r, p_ref in reversed(list(enumerate((p0_ref, p1_ref, p2_ref, p3_ref)))):
        lf = p_ref[...]
        d_ref[:, r * PAGE:(r + 1) * PAGE] = _dot_f32(lf, sl_ref[...]) + carry
        carry = carry + jnp.sum(lf, axis=1, keepdims=True)
    carry_ref[...] = jnp.broadcast_to(carry, carry_ref.shape)


def _fox_suffix(logf_t, layer, page_table):
    sb, n_pages = page_table.shape
    n_steps = n_pages // PAGES_PER_STEP
    sl = jnp.asarray(np.tril(np.ones((PAGE, PAGE), np.float32), -1))
    specs = [pl.BlockSpec((None, None, FOX_HEADS, PAGE), functools.partial(
        lambda bi, j, pt, r: (layer, pt[bi, (n_steps - 1 - j) * PAGES_PER_STEP + r], 0, 0), r=r))
        for r in range(PAGES_PER_STEP)]
    return pl.pallas_call(
        _fox_suffix_kernel,
        grid_spec=pltpu.PrefetchScalarGridSpec(
            num_scalar_prefetch=1, grid=(sb, n_steps),
            in_specs=specs + [pl.BlockSpec(sl.shape, lambda bi, j, pt: (0, 0))],
            out_specs=pl.BlockSpec((None, FOX_HEADS, PAGES_PER_STEP * PAGE), lambda bi, j, pt: (bi, 0, n_steps - 1 - j)),
            scratch_shapes=[pltpu.VMEM((FOX_HEADS, LANES), F32)]),
        out_shape=jax.ShapeDtypeStruct((sb, FOX_HEADS, n_pages * PAGE), F32),
        compiler_params=_cparams(("arbitrary", "arbitrary")),
        name="fox_suffix",
    )(page_table, logf_t, logf_t, logf_t, logf_t, sl)


def _fox_sample_kernel(pt_ref, qs_ref, p0_ref, p1_ref, p2_ref, p3_ref, dt_ref, newkv_ref, lfnew_ref, hmask_ref,
                       o_ref, qbd_ref, crel_ref, m_ref, l_ref, acc_ref):
    j = pl.program_id(1)
    R = SAMPLE_ROWS
    H = FOX_HEADS
    tk = PAGES_PER_STEP * PAGE

    @pl.when(j == 0)
    def _():
        q = qs_ref[...].astype(F32)
        qbd_ref[...] = (jnp.concatenate([q] * H, axis=0) * hmask_ref[...]).astype(BF16)
        tri = jnp.where(lax.broadcasted_iota(jnp.int32, (R, R), 1) <= lax.broadcasted_iota(jnp.int32, (R, R), 0), 1.0, 0.0)
        crel = _dot_f32(tri, lfnew_ref[...])
        crel_ref[...] = jnp.concatenate([crel[:, h:h + 1] for h in range(H)], axis=0)
        m_ref[...] = jnp.full_like(m_ref, NEG)
        l_ref[...] = jnp.zeros_like(l_ref)
        acc_ref[...] = jnp.zeros_like(acc_ref)

    pages = (p0_ref, p1_ref, p2_ref, p3_ref)
    kt = jnp.concatenate([p[0:FOX_WIDTH, :] for p in pages], axis=1).astype(BF16)
    vt = jnp.concatenate([p[FOX_WIDTH:, :] for p in pages], axis=1).astype(BF16)
    drows = jnp.concatenate([jnp.broadcast_to(dt_ref[h:h + 1, :], (R, tk)) for h in range(H)], axis=0)
    s = jnp.dot(qbd_ref[...], kt, preferred_element_type=F32) + crel_ref[...] + drows
    _online_update(s, s > 2 * NEG, vt, m_ref, l_ref, acc_ref, v_t=True)

    @pl.when(j == pl.num_programs(1) - 1)
    def _():
        lf = lfnew_ref[...]
        iu = lax.broadcasted_iota(jnp.int32, (R, R), 0)
        ir = lax.broadcasted_iota(jnp.int32, (R, R), 1)
        a_le = jnp.where(ir <= iu, 1.0, 0.0)
        b_gt = jnp.where(iu > ir, 1.0, 0.0)
        dnew = jnp.concatenate([_dot_f32(a_le, lf[:, h:h + 1] * b_gt) for h in range(H)], axis=0)
        rq = lax.broadcasted_iota(jnp.int32, (H * R, 1), 0) & (R - 1)
        valid = lax.broadcasted_iota(jnp.int32, (H * R, R), 1) <= rq
        knew = newkv_ref[:, 0:FOX_WIDTH].astype(BF16)
        vnew = newkv_ref[:, FOX_WIDTH:].astype(BF16)
        _online_update(_dot_nt(qbd_ref[...], knew) + dnew, valid, vnew, m_ref, l_ref, acc_ref)
        o = (acc_ref[...] / jnp.maximum(l_ref[...], 1e-30)) * hmask_ref[...]
        out = o[0:R]
        for h in range(1, H):
            out = out + o[h * R:(h + 1) * R]
        o_ref[...] = out


def _fox_sample(qs, kv_t, layer, page_table, dt, newkv, lfnew):
    sb, n_pages = page_table.shape
    R = SAMPLE_ROWS
    tk = PAGES_PER_STEP * PAGE
    hmask = jnp.asarray((np.arange(FOX_HEADS * R)[:, None] // R == np.arange(FOX_WIDTH)[None, :] // HEAD_DIM)
                        .astype(np.float32))
    seq = lambda a: pl.BlockSpec((None,) + a.shape[1:], lambda bi, j, pt: (bi,) + (0,) * (a.ndim - 1))
    return pl.pallas_call(
        _fox_sample_kernel,
        grid_spec=pltpu.PrefetchScalarGridSpec(
            num_scalar_prefetch=1, grid=(sb, n_pages // PAGES_PER_STEP),
            in_specs=[seq(qs)] + _page_specs(layer, 2 * FOX_WIDTH, 0)
                     + [pl.BlockSpec((None, FOX_HEADS, tk), lambda bi, j, pt: (bi, 0, j)), seq(newkv), seq(lfnew),
                        pl.BlockSpec(hmask.shape, lambda bi, j, pt: (0, 0))],
            out_specs=pl.BlockSpec((None, R, FOX_WIDTH), lambda bi, j, pt: (bi, 0, 0)),
            scratch_shapes=[pltpu.VMEM((FOX_HEADS * R, FOX_WIDTH), BF16), pltpu.VMEM((FOX_HEADS * R, 1), F32),
                            pltpu.VMEM((FOX_HEADS * R, 1), F32), pltpu.VMEM((FOX_HEADS * R, 1), F32),
                            pltpu.VMEM((FOX_HEADS * R, FOX_WIDTH), F32)]),
        out_shape=jax.ShapeDtypeStruct((sb, R, FOX_WIDTH), F32),
        compiler_params=_cparams(("arbitrary", "arbitrary"), VMEM_LIMIT_BYTES),
        name="fox_sample",
    )(page_table, qs, kv_t, kv_t, kv_t, kv_t, dt, newkv, lfnew, hmask)


def _j_rmsnorm(x, g):
    xf = x.astype(jnp.float32)
    y = xf * lax.rsqrt(jnp.mean(xf * xf, axis=-1, keepdims=True) + EPS)
    return (y * g.astype(jnp.float32)).astype(x.dtype)


def _j_masked_softmax(s, mask):
    s = jnp.where(mask, s, -1e30)
    p = jnp.exp(s - jnp.max(s, axis=-1, keepdims=True)) * mask
    return p / jnp.maximum(jnp.sum(p, axis=-1, keepdims=True), 1e-30)


def _j_rel_bucket(dist):
    exact = NUM_BUCKETS // 2
    d = jnp.maximum(dist, 0)
    log_ratio = jnp.log(jnp.maximum(d, 1).astype(jnp.float32) / exact) / math.log(MAX_DISTANCE / exact)
    large = jnp.minimum(exact + (log_ratio * (NUM_BUCKETS - exact)).astype(jnp.int32), NUM_BUCKETS - 1)
    return jnp.where(d < exact, d, large)


def _j_gather_pages(pool, page_table):
    g = pool.reshape(pool.shape[0], -1)[page_table]
    g = g.reshape(page_table.shape + pool.shape[1:])
    return g.reshape((g.shape[0], g.shape[1] * g.shape[2]) + g.shape[3:])


def _j_nsa_prepare(kv_rows, pe, w1, w2, kn_cmp):
    B, T = kv_rows.shape[:2]
    nc = (T - CMP_BLOCK) // CMP_STRIDE + 1
    seg = kv_rows[:, :(nc + 1) * CMP_STRIDE, 0:2].reshape(B, nc + 1, CMP_STRIDE, 2, NSA_KV_HEADS, HEAD_DIM)
    blk = jnp.concatenate([seg[:, :-1], seg[:, 1:]], axis=2)
    blk = blk + pe.transpose(1, 0, 2)[None, None, :, :, None, :]
    flat = blk.transpose(0, 1, 3, 4, 2, 5).reshape(B, nc, 2, NSA_KV_HEADS, CMP_BLOCK * HEAD_DIM)
    hid = jax.nn.gelu(jnp.einsum('bnjgf,jfe->bnjge', flat, w1))
    ckv = jnp.einsum('bnjge,jed->bnjgd', hid, w2)
    ck = _j_rmsnorm(ckv[:, :, 0], kn_cmp)
    cv = ckv[:, :, 1]
    c_end = jnp.arange(nc) * CMP_STRIDE + (CMP_BLOCK - 1)
    nbs = -(-T // SEL_BLOCK)
    sel = jnp.pad(kv_rows[:, :, 2:4], ((0, 0), (0, nbs * SEL_BLOCK - T), (0, 0), (0, 0), (0, 0)))
    sel = sel.reshape(B, nbs, SEL_BLOCK, 2, NSA_KV_HEADS, HEAD_DIM).transpose(3, 0, 4, 1, 2, 5)
    return ck, cv, c_end, sel[0], sel[1]


def _j_nsa_attend(q, gates, q_pos, ck, cv, c_end, sk, sv, wk, wv, w_pos, rel_bias):
    B, Q = q.shape[:2]
    scale = HEAD_DIM ** -0.5
    tbl = rel_bias.astype(jnp.float32).T.reshape(NSA_KV_HEADS, NSA_GROUP, NUM_BUCKETS)
    qg = q.reshape(B, Q, NSA_KV_HEADS, NSA_GROUP, HEAD_DIM).transpose(0, 2, 3, 1, 4)
    t = q_pos[:, None]
    s = jnp.einsum('bkgqd,bnkd->bkgqn', qg, ck).astype(jnp.float32) * scale + tbl[:, :, _j_rel_bucket(t - c_end[None])]
    p_cmp = _j_masked_softmax(s, c_end[None] <= t)
    o_cmp = jnp.einsum('bkgqn,bnkd->bkgqd', p_cmp.astype(cv.dtype), cv)
    nbs = sk.shape[2]
    imp = jnp.sum(p_cmp, axis=2)
    nc = imp.shape[-1]
    imp = jnp.pad(imp, ((0, 0), (0, 0), (0, 0), (0, nbs * SEL_RATIO - nc)))
    imp = imp.reshape(B, NSA_KV_HEADS, Q, nbs, SEL_RATIO).sum(-1)
    blk = jnp.arange(nbs)[None]
    cur = (q_pos // SEL_BLOCK)[:, None]
    forced = (blk == 0) | (blk == cur) | (blk == cur - 1)
    score = jnp.where(blk > cur, -jnp.inf, jnp.where(forced, jnp.inf, imp))
    n_sel = min(SEL_TOPN, nbs)
    _, idx = lax.top_k(score, n_sel)
    pick = jax.vmap(jax.vmap(lambda kb, ix: kb[ix]))
    gk = pick(sk, idx).reshape(B, NSA_KV_HEADS, Q, n_sel * SEL_BLOCK, HEAD_DIM)
    gv = pick(sv, idx).reshape(B, NSA_KV_HEADS, Q, n_sel * SEL_BLOCK, HEAD_DIM)
    spos = (idx[..., None] * SEL_BLOCK + jnp.arange(SEL_BLOCK)).reshape(B, NSA_KV_HEADS, Q, n_sel * SEL_BLOCK)
    kidx = jnp.arange(NSA_KV_HEADS)[None, :, None, None, None]
    gidx = jnp.arange(NSA_GROUP)[None, None, :, None, None]
    bias = tbl[kidx, gidx, _j_rel_bucket(t - spos)[:, :, None]]
    s = jnp.einsum('bkgqd,bkqsd->bkgqs', qg, gk).astype(jnp.float32) * scale + bias
    p = _j_masked_softmax(s, (spos <= t)[:, :, None])
    o_sel = jnp.einsum('bkgqs,bkqsd->bkgqd', p.astype(gv.dtype), gv)
    dist = t - w_pos[None]
    s = jnp.einsum('bkgqd,bwkd->bkgqw', qg, wk).astype(jnp.float32) * scale + tbl[:, :, _j_rel_bucket(dist)]
    p = _j_masked_softmax(s, (dist >= 0) & (dist <= NSA_WINDOW) & (w_pos[None] >= 0))
    o_win = jnp.einsum('bkgqw,bwkd->bkgqd', p.astype(wv.dtype), wv)
    o = jnp.stack([o_cmp, o_sel, o_win], axis=-1).transpose(0, 3, 1, 2, 4, 5)
    o = o.reshape(B, Q, NSA_HEADS, HEAD_DIM, 3)
    return jnp.einsum('bqhdr,bqhr->bqhd', o, gates.astype(o.dtype)).reshape(B, Q, NSA_WIDTH)


def _j_nsa_prompt(q, gates, kv_rows, win_rows, pe, w1, w2, kn_cmp, rel_bias):
    B, T = q.shape[:2]
    QB = 128
    ck, cv, c_end, sk, sv = _j_nsa_prepare(kv_rows, pe, w1, w2, kn_cmp)
    win_pad = jnp.pad(win_rows, ((0, 0), (NSA_WINDOW, 0), (0, 0), (0, 0), (0, 0)))

    def block(i):
        q0 = i * QB
        qb = lax.dynamic_slice_in_dim(q, q0, QB, axis=1)
        gb = lax.dynamic_slice_in_dim(gates, q0, QB, axis=1)
        wb = lax.dynamic_slice_in_dim(win_pad, q0, NSA_WINDOW + QB, axis=1)
        q_pos = q0 + jnp.arange(QB)
        w_pos = q0 - NSA_WINDOW + jnp.arange(NSA_WINDOW + QB)
        return _j_nsa_attend(qb, gb, q_pos, ck, cv, c_end, sk, sv, wb[:, :, 0], wb[:, :, 1], w_pos, rel_bias)

    o = lax.map(block, jnp.arange(T // QB))
    return o.transpose(1, 0, 2, 3).reshape(B, T, NSA_WIDTH)


def _j_fox_attend(q, cq, q_pos, k, v, ck, k_pos):
    s = jnp.einsum('bqhd,bshd->bhqs', q, k).astype(jnp.float32) * HEAD_DIM ** -0.5
    s = s + jnp.swapaxes(cq, 1, 2)[..., :, None] - jnp.swapaxes(ck, 1, 2)[..., None, :]
    p = _j_masked_softmax(s, k_pos[None, :] <= q_pos[:, None])
    return jnp.einsum('bhqs,bshd->bqhd', p.astype(v.dtype), v)


def _pad_cols(w, width):
    return jnp.pad(w, ((0, 0), (0, width - w.shape[1])))


def _prep_even_w(w):
    gq, gk, gv, glr, gog, nq, nkv, ng = _split(w, EVEN_SIZES)
    return jnp.concatenate([gq, gk, gv, gog, nq, nkv, _pad_cols(glr, LANES), _pad_cols(ng, LANES)], axis=1).astype(BF16)


def _prep_odd_w(w):
    fq, fk, fv, ff, cg = _split(w, ODD_SIZES)
    return jnp.concatenate([fq, fk, fv, cg, _pad_cols(ff, LANES)], axis=1).astype(BF16)


def _row(v):
    return v.reshape(1, -1).astype(F32)


def _tile_row(v, reps):
    return jnp.tile(v.astype(F32), reps).reshape(1, -1)


def kernel(x_prompt, x_sample, cache_nsa_kv, state_nsa_win, state_gla, cache_fox_kv, cache_fox_logf, state_conv, cache_mem_kv, page_table, mem_prompt, rel_bias, norm_mix, norm_xattn, norm_ffn, even_w_in, even_w_out, gla_w_gate, gla_b_gate, gla_out_norm, nsa_q_norm, nsa_k_norm, nsa_cmp_pe, nsa_cmp_w1, nsa_cmp_w2, odd_w_in, odd_w_out, fox_q_norm, fox_k_norm, fox_b_f, conv_w, conv_b, conv_ln_g, conv_ln_b, mem_norm, xa_wq, xa_wkv, xa_wo, xa_q_norm, xa_k_norm, ffn_w_in, ffn_w_out):
    B, T, _ = x_prompt.shape
    SB, SQ, _ = x_sample.shape
    depth = norm_mix.shape[0]
    past_len = page_table.shape[1] * cache_nsa_kv.shape[2]
    n_win = state_nsa_win.shape[2]
    dec_pos = past_len + jnp.arange(SQ)
    MP = B * T
    SR = SAMPLE_ROWS
    MS = SB * SR

    yp = x_prompt.reshape(MP, D_MODEL)
    ys = jnp.pad(x_sample, ((0, 0), (0, SR - SQ), (0, 0))).reshape(MS, D_MODEL)

    nsa_cache_t = _feature_major(cache_nsa_kv)
    fox_cache_t = _feature_major(cache_fox_kv)
    logf_t = _feature_major(cache_fox_logf)
    win_t = _feature_major(state_nsa_win)
    mem_cache_t = _feature_major(cache_mem_kv)

    nsa_kv_p, nsa_kv_s, win_p, win_s, gla_p, gla_s = [], [], [], [], [], []
    fox_kv_p, fox_kv_s, logf_p, logf_s, conv_p, conv_s, mem_kv_p = [], [], [], [], [], [], []

    for layer in range(depth):
        if layer % 2 == 0:
            e = layer // 2
            w_pad = _prep_even_w(even_w_in[e])
            wg_pad = jnp.pad(gla_w_gate[e], ((0, LANES - GLA_RANK), (0, 0))).astype(BF16)
            bg = _row(gla_b_gate[e])
            qn = _tile_row(nsa_q_norm[e], NSA_HEADS)
            kn1 = _tile_row(nsa_k_norm[e, 1], NSA_KV_HEADS)
            kn2 = _tile_row(nsa_k_norm[e, 2], NSA_KV_HEADS)
            gn = _row(gla_out_norm[e])
            w_out = even_w_out[e].astype(BF16)
            g_mix = _row(norm_mix[layer])
            pe = nsa_cmp_pe[e].astype(F32)
            w1b = nsa_cmp_w1[e].astype(BF16)
            w2b = nsa_cmp_w2[e].astype(BF16)
            kn0 = _row(nsa_k_norm[e, 0])
            tbl = rel_bias.astype(F32).T
            (q, k, v, la, og, nqs, kvr, wr, gt, selkv, winkv, nqst, selvt, winvt, gtt) = _even_in(
                yp, g_mix, w_pad, wg_pad, bg, qn, kn1, kn2, 256)
            r3 = lambda a: a.reshape(B, T, a.shape[-1])
            s0t = jnp.zeros((B, GLA_HEADS, GLA_DV, GLA_DK), F32)
            o_gla, sfin_t = _gla(r3(q), r3(k), r3(v), r3(la), r3(og), gn, s0t, 256)
            kvr5 = kvr.reshape(B, T, 4, NSA_KV_HEADS, HEAD_DIM)
            wr5 = wr.reshape(B, T, 2, NSA_KV_HEADS, HEAD_DIM)
            ck, cv = _compress_prompt(r3(kvr), pe, w1b, w2b, kn0)
            ocmp, mskt = _cmp_topk(r3(nqs), ck, cv, tbl, 256, 0, SEL_TOPN, T // SEL_BLOCK, True)
            o_nsa = _selwin_prompt(nqst, selkv, selvt, winkv, winvt, mskt, tbl, ocmp.reshape(MP, NSA_WIDTH), gtt,
                                   B, T, 256)
            yp = _out_proj(yp, o_gla.reshape(MP, GLA_WIDTH), o_nsa.reshape(MP, NSA_WIDTH),
                           w_out[:GLA_WIDTH], w_out[GLA_WIDTH:], 512)
            nsa_kv_p.append(kvr5)
            win_p.append(wr5[:, -min(NSA_WINDOW, T):])
            gla_p.append(jnp.swapaxes(sfin_t, -1, -2))
            (q, k, v, la, og, nqs, kvr, wr, gt) = _even_in(ys, g_mix, w_pad, wg_pad, bg, qn, kn1, kn2, MS)[:9]
            pad16 = lambda a: jnp.pad(a.reshape(SB, SR, a.shape[-1]), ((0, 0), (0, SUB - SR), (0, 0)))
            s0t = jnp.swapaxes(state_gla[e], -1, -2)
            o_gla, snew_t = _gla(pad16(q), pad16(k), pad16(v), pad16(la), pad16(og), gn, s0t, SUB, n_valid=SQ)
            o_gla = o_gla[:, :SR]
            s3 = lambda a: a.reshape(SB, SR, a.shape[-1])
            kvr5 = kvr.reshape(SB, SR, 4, NSA_KV_HEADS, HEAD_DIM)[:, :SQ]
            wr5 = wr.reshape(SB, SR, 2, NSA_KV_HEADS, HEAD_DIM)[:, :SQ]
            ck, cv = _compress_sample(nsa_cache_t, e, page_table, pe, w1b, w2b, kn0)
            ocmp, msk = _cmp_topk(s3(nqs), ck, cv, tbl, SR, past_len, SEL_TOPN - 1, past_len // SEL_BLOCK, False)
            o_nsa = _selwin_sample(s3(nqs), nsa_cache_t, e, page_table, msk, s3(kvr), win_t, s3(wr), tbl, ocmp, s3(gt))
            ys = _out_proj(ys, o_gla.reshape(MS, GLA_WIDTH), o_nsa.reshape(MS, NSA_WIDTH),
                           w_out[:GLA_WIDTH], w_out[GLA_WIDTH:], MS)
            nsa_kv_s.append(kvr5)
            win_s.append(jnp.concatenate([state_nsa_win[e][:, SQ:], wr5], axis=1))
            gla_s.append(jnp.swapaxes(snew_t, -1, -2))
        else:
            j = layer // 2
            w_pad = _prep_odd_w(odd_w_in[j])
            qn = _tile_row(fox_q_norm[j], FOX_HEADS)
            kn = _tile_row(fox_k_norm[j], FOX_HEADS)
            bf_pad = jnp.pad(fox_b_f[j].astype(F32), (0, LANES - FOX_HEADS)).reshape(1, LANES)
            w_out = odd_w_out[j].astype(BF16)
            g_mix = _row(norm_mix[layer])
            cw = conv_w[j].astype(F32)
            cb, cg_, cbeta = _row(conv_b[j]), _row(conv_ln_g[j]), _row(conv_ln_b[j])
            qs, kv, kb, vb, lf, c, u = _odd_in(yp.reshape(B, T, D_MODEL), g_mix, w_pad, qn, kn, bf_pad, 256)
            c8 = c[:, :, :FOX_HEADS]
            o_fox = _fox_prompt(qs, kb, vb, c8, jnp.swapaxes(c8, 1, 2), 512, 1024)
            o_conv, cst = _conv(u, jnp.zeros((B, CONV_WIDTH - 1, CONV_CH), F32), cw, cb, cg_, cbeta, 512, 512)
            yp = _out_proj(yp, o_fox.reshape(MP, FOX_WIDTH), o_conv.reshape(MP, CONV_CH),
                           w_out[:FOX_WIDTH], w_out[FOX_WIDTH:], 512)
            fox_kv_p.append(kv.reshape(B, T, 2, FOX_HEADS, HEAD_DIM))
            logf_p.append(lf[:, :, :FOX_HEADS])
            conv_p.append(cst)
            qs, kv, kb, vb, lf, c, u = _odd_in(ys.reshape(1, MS, D_MODEL), g_mix, w_pad, qn, kn, bf_pad, MS)
            s3 = lambda a: a.reshape(SB, SR, a.shape[-1])
            new_kv = kv.reshape(SB, SR, 2, FOX_HEADS, HEAD_DIM)[:, :SQ]
            lf_new = lf.reshape(SB, SR, LANES)[:, :SQ, :FOX_HEADS]
            dsuf = _fox_suffix(logf_t, j, page_table)
            o_fox = _fox_sample(s3(qs), fox_cache_t, j, page_table, dsuf, s3(kv), s3(lf))
            o_conv, cst = _conv(u.reshape(SB, SR, CONV_CH), state_conv[j], cw, cb, cg_, cbeta, SR, SQ)
            ys = _out_proj(ys, o_fox.reshape(MS, FOX_WIDTH), o_conv.reshape(MS, CONV_CH),
                           w_out[:FOX_WIDTH], w_out[FOX_WIDTH:], MS)
            fox_kv_s.append(new_kv)
            logf_s.append(lf_new)
            conv_s.append(cst)
        g_xa = _row(norm_xattn[layer])
        wq = xa_wq[layer].astype(BF16)
        wo = xa_wo[layer].astype(BF16)
        xqn = _tile_row(xa_q_norm[layer], XA_HEADS)
        mkv_t = _mem_kv(mem_prompt, _row(mem_norm[layer]), xa_wkv[layer].astype(BF16), _tile_row(xa_k_norm[layer], XA_HEADS))
        mem_kv_p.append(jnp.transpose(mkv_t.reshape(B, 2, XA_HEADS, HEAD_DIM, MEM_LEN), (0, 4, 1, 2, 3)))
        yp = _xattn(yp.reshape(B, T, D_MODEL), mkv_t[None], 0, g_xa, wq, wo, xqn, 512).reshape(MP, D_MODEL)
        ys = _xattn(ys.reshape(SB, SR, D_MODEL), mem_cache_t, layer, g_xa, wq, wo, xqn, SR).reshape(MS, D_MODEL)
        g_ffn = _row(norm_ffn[layer])
        w_in = ffn_w_in[layer].astype(BF16)
        w_o = ffn_w_out[layer].astype(BF16)
        yp = _ffn(yp, g_ffn, w_in, w_o, 512)
        ys = _ffn(ys, g_ffn, w_in, w_o, MS)

    yp = yp.reshape(B, T, D_MODEL)
    ys = ys.reshape(SB, SR, D_MODEL)[:, :SQ]
    return (yp, ys,
            jnp.stack(nsa_kv_p), jnp.stack(nsa_kv_s), jnp.stack(win_p), jnp.stack(win_s),
            jnp.stack(gla_p), jnp.stack(gla_s), jnp.stack(fox_kv_p), jnp.stack(fox_kv_s),
            jnp.stack(logf_p), jnp.stack(logf_s), jnp.stack(conv_p), jnp.stack(conv_s),
            jnp.stack(mem_kv_p))
```

```python
import functools
import math

import jax
import jax.numpy as jnp
import numpy as np
from jax import lax
from jax.experimental import pallas as pl
from jax.experimental.pallas import tpu as pltpu

F32 = jnp.float32
BF16 = jnp.bfloat16

D_MODEL = 1024
HEAD_DIM = 64
GLA_WIDTH = 512
GLA_HEADS = 4
GLA_DV = 128
GLA_DK = 64
GLA_RANK = 16
GLA_TAU = 16.0
NSA_WIDTH = 512
NSA_HEADS = 8
NSA_KV_HEADS = 2
NSA_GROUP = 4
CMP_STRIDE = 16
CMP_BLOCK = 32
CMP_HIDDEN = 256
SEL_BLOCK = 64
SEL_RATIO = 4
SEL_TOPN = 16
NSA_WINDOW = 512
FOX_WIDTH = 512
FOX_HEADS = 8
CONV_CH = 512
CONV_WIDTH = 31
MEM_LEN = 256
XA_HEADS = 4
XA_WIDTH = 256
FFN_HIDDEN = 2816
NUM_BUCKETS = 32
MAX_DISTANCE = 128
EPS = 1e-6
SCALE = HEAD_DIM ** -0.5
NEG = -1e30

EVEN_SIZES = (256, 256, 512, 16, 512, 512, 768, 24)
ODD_SIZES = (512, 512, 512, 8, 1024)

LANES = 128
VMEM_LIMIT_BYTES = 56 * 1024 * 1024
SAMPLE_ROWS = 8
SUB = 16
N_CMP = 512
N_SELBLK = 128


def _cparams(sem, vmem=None):
    return pltpu.CompilerParams(dimension_semantics=sem, vmem_limit_bytes=vmem)


def _split(h, sizes):
    return jnp.split(h, np.cumsum(sizes)[:-1].tolist(), axis=-1)


def _rms_rows(x, g):
    return x * lax.rsqrt(jnp.mean(x * x, axis=-1, keepdims=True) + EPS) * g


def _group_rms(x, gmat, gs):
    x2 = x * x
    hi = x2.astype(BF16)
    lo = (x2 - hi.astype(F32)).astype(BF16)
    ms = (jnp.dot(hi, gmat, preferred_element_type=F32) + jnp.dot(lo, gmat, preferred_element_type=F32)) * (1.0 / gs)
    return x * lax.rsqrt(ms + EPS)


def _log_sigmoid(z):
    return -(jnp.maximum(-z, 0.0) + jnp.log1p(jnp.exp(-jnp.abs(z))))


def _sigmoid(z):
    return 1.0 / (1.0 + jnp.exp(-z))


def _dot_nt(a, b):
    return lax.dot_general(a, b, (((1,), (1,)), ((), ())), preferred_element_type=F32)


def _dot_tn(a, b):
    return lax.dot_general(a, b, (((0,), (0,)), ((), ())), preferred_element_type=F32)


def _dot_f32(a, b):
    return jnp.dot(a, b, preferred_element_type=F32, precision=lax.Precision.HIGHEST)


def _block_ones(width, gs):
    r = np.arange(width) // gs
    return jnp.asarray((r[:, None] == r[None, :]).astype(np.float32), dtype=BF16)


def _even_in_kernel(x_ref, g_ref, w_ref, wg_ref, bg_ref, qn_ref, kn1_ref, kn2_ref, gm512_ref, gm128_ref,
                    q_ref, k_ref, v_ref, la_ref, og_ref, nqs_ref, kvr_ref, wr_ref, gt_ref, selkv_ref, winkv_ref,
                    nqst_ref, selvt_ref, winvt_ref, gtt_ref):
    xb = _rms_rows(x_ref[...], g_ref[...]).astype(BF16)

    def proj(lo, hi):
        return jnp.dot(xb, w_ref[:, lo:hi], preferred_element_type=F32)

    q_ref[...] = proj(0, 256) * (GLA_DK ** -0.5)
    k_ref[...] = proj(256, 512)
    v_ref[...] = proj(512, 1024)
    og = proj(1024, 1536)
    og_ref[...] = og * _sigmoid(og)
    nq = _group_rms(proj(1536, 2048), gm512_ref[...], HEAD_DIM) * qn_ref[...] * SCALE
    nqs_ref[...] = nq.astype(BF16)
    nqst_ref[...] = nq.T.astype(BF16)
    kvr_ref[:, 0:256] = proj(2048, 2304)
    selk = _group_rms(proj(2304, 2432), gm128_ref[...], HEAD_DIM) * kn1_ref[...]
    selv = proj(2432, 2560)
    kvr_ref[:, 256:384] = selk
    kvr_ref[:, 384:512] = selv
    selkv_ref[:, 0:128] = selk.astype(BF16)
    selkv_ref[:, 128:256] = selv.astype(BF16)
    selvt_ref[...] = selv.T.astype(BF16)
    wink = _group_rms(proj(2560, 2688), gm128_ref[...], HEAD_DIM) * kn2_ref[...]
    winv = proj(2688, 2816)
    wr_ref[:, 0:128] = wink
    wr_ref[:, 128:256] = winv
    winkv_ref[:, 0:128] = wink.astype(BF16)
    winkv_ref[:, 128:256] = winv.astype(BF16)
    winvt_ref[...] = winv.T.astype(BF16)
    glr = proj(2816, 2944).astype(BF16)
    z = jnp.dot(glr, wg_ref[...], preferred_element_type=F32) + bg_ref[...]
    la_ref[...] = _log_sigmoid(z) * (1.0 / GLA_TAU)
    gates = _sigmoid(proj(2944, 3072))
    gt_ref[...] = gates
    gtt_ref[...] = gates.T[0:32, :]


def _even_in(x2d, g, w_pad, wg_pad, bg, qn, kn1, kn2, tm):
    m = x2d.shape[0]
    widths = (256, 256, 512, 256, 512, 512, 512, 256, 128, 256, 256)
    dtypes = (F32, F32, F32, F32, F32, BF16, F32, F32, F32, BF16, BF16)
    t_heights = (512, 128, 128, 32)
    t_dtypes = (BF16, BF16, BF16, F32)
    full = lambda a: pl.BlockSpec(a.shape, lambda i: (0,) * a.ndim)
    gm512 = _block_ones(512, HEAD_DIM)
    gm128 = _block_ones(128, HEAD_DIM)
    ins = (x2d, g, w_pad, wg_pad, bg, qn, kn1, kn2, gm512, gm128)
    return pl.pallas_call(
        _even_in_kernel,
        grid=(m // tm,),
        in_specs=[pl.BlockSpec((tm, D_MODEL), lambda i: (i, 0))] + [full(a) for a in ins[1:]],
        out_specs=[pl.BlockSpec((tm, w), lambda i: (i, 0)) for w in widths]
                  + [pl.BlockSpec((h, tm), lambda i: (0, i)) for h in t_heights],
        out_shape=[jax.ShapeDtypeStruct((m, w), d) for w, d in zip(widths, dtypes)]
                  + [jax.ShapeDtypeStruct((h, m), d) for h, d in zip(t_heights, t_dtypes)],
        compiler_params=_cparams(("arbitrary",), VMEM_LIMIT_BYTES),
        name="even_in",
    )(*ins)


def _odd_in_kernel(x_ref, g_ref, w_ref, qn_ref, kn_ref, bf_ref, gm512_ref, tri_ref,
                   qs_ref, kv_ref, kb_ref, lf_ref, c_ref, u_ref, qst_ref, vbt_ref, carry_ref):
    @pl.when(pl.program_id(1) == 0)
    def _():
        carry_ref[...] = jnp.zeros_like(carry_ref)

    xb = _rms_rows(x_ref[...], g_ref[...]).astype(BF16)

    def proj(lo, hi):
        return jnp.dot(xb, w_ref[:, lo:hi], preferred_element_type=F32)

    q = _group_rms(proj(0, 512), gm512_ref[...], HEAD_DIM) * qn_ref[...] * SCALE
    qs_ref[...] = q.astype(BF16)
    qst_ref[...] = q.T.astype(BF16)
    k = _group_rms(proj(512, 1024), gm512_ref[...], HEAD_DIM) * kn_ref[...]
    v = proj(1024, 1536)
    kv_ref[:, 0:512] = k
    kv_ref[:, 512:1024] = v
    kb_ref[...] = k.astype(BF16)
    vbt_ref[...] = v.T.astype(BF16)
    u_ref[...] = proj(1536, 2048) * _sigmoid(proj(2048, 2560))
    lf = _log_sigmoid(proj(2560, 2688) + bf_ref[...])
    lf_ref[...] = lf
    c = _dot_f32(tri_ref[...], lf) + carry_ref[0:1, :]
    c_ref[...] = c
    carry_ref[0:1, :] = c[-1:, :]


def _odd_in(x3d, g, w_pad, qn, kn, bf_pad, tm):
    b, t, _ = x3d.shape
    widths = (512, 1024, 512, 128, 128, 512)
    dtypes = (BF16, F32, BF16, F32, F32, F32)
    gm512 = _block_ones(512, HEAD_DIM)
    tri = jnp.asarray(np.tril(np.ones((tm, tm), np.float32)))
    ins = (x3d, g, w_pad, qn, kn, bf_pad, gm512, tri)
    full = lambda a: pl.BlockSpec(a.shape, lambda bi, i: (0,) * a.ndim)
    nt = t // tm
    t_spec = pl.BlockSpec((FOX_WIDTH, tm), lambda bi, i: (0, bi * nt + i))
    t_shape = jax.ShapeDtypeStruct((FOX_WIDTH, b * t), BF16)
    return pl.pallas_call(
        _odd_in_kernel,
        grid=(b, nt),
        in_specs=[pl.BlockSpec((None, tm, D_MODEL), lambda bi, i: (bi, i, 0))] + [full(a) for a in ins[1:]],
        out_specs=[pl.BlockSpec((None, tm, w), lambda bi, i: (bi, i, 0)) for w in widths] + [t_spec, t_spec],
        out_shape=[jax.ShapeDtypeStruct((b, t, w), d) for w, d in zip(widths, dtypes)] + [t_shape, t_shape],
        scratch_shapes=[pltpu.VMEM((8, 128), F32)],
        compiler_params=_cparams(("arbitrary", "arbitrary"), VMEM_LIMIT_BYTES),
        name="odd_in",
    )(*ins)


def _out_proj_kernel(res_ref, a1_ref, a2_ref, w1_ref, w2_ref, o_ref):
    acc = jnp.dot(a1_ref[...].astype(BF16), w1_ref[...], preferred_element_type=F32)
    acc = acc + jnp.dot(a2_ref[...].astype(BF16), w2_ref[...], preferred_element_type=F32)
    o_ref[...] = res_ref[...] + acc


def _out_proj(res, a1, a2, w1, w2, tm):
    m = res.shape[0]
    row = lambda a: pl.BlockSpec((tm, a.shape[1]), lambda i: (i, 0))
    full = lambda a: pl.BlockSpec(a.shape, lambda i: (0, 0))
    return pl.pallas_call(
        _out_proj_kernel,
        grid=(m // tm,),
        in_specs=[row(res), row(a1), row(a2), full(w1), full(w2)],
        out_specs=row(res),
        out_shape=jax.ShapeDtypeStruct(res.shape, F32),
        compiler_params=_cparams(("arbitrary",), VMEM_LIMIT_BYTES),
        name="out_proj",
    )(res, a1, a2, w1, w2)


def _ffn_kernel(x_ref, g_ref, wg_ref, wu_ref, wo_ref, o_ref, xn_ref, acc_ref):
    j = pl.program_id(1)

    @pl.when(j == 0)
    def _():
        xn_ref[...] = _rms_rows(x_ref[...], g_ref[...]).astype(BF16)
        acc_ref[...] = jnp.zeros_like(acc_ref)

    xb = xn_ref[...]
    gate = jnp.dot(xb, wg_ref[...], preferred_element_type=F32)
    up = jnp.dot(xb, wu_ref[...], preferred_element_type=F32)
    h = (gate * _sigmoid(gate) * up).astype(BF16)
    acc_ref[...] += jnp.dot(h, wo_ref[...], preferred_element_type=F32)

    @pl.when(j == pl.num_programs(1) - 1)
    def _():
        o_ref[...] = x_ref[...] + acc_ref[...]


def _ffn(x2d, g, w_in, w_out, tm, n_chunks=2):
    m = x2d.shape[0]
    th = FFN_HIDDEN // n_chunks
    return pl.pallas_call(
        _ffn_kernel,
        grid=(m // tm, n_chunks),
        in_specs=[pl.BlockSpec((tm, D_MODEL), lambda i, j: (i, 0)),
                  pl.BlockSpec((1, D_MODEL), lambda i, j: (0, 0)),
                  pl.BlockSpec((D_MODEL, th), lambda i, j: (0, j)),
                  pl.BlockSpec((D_MODEL, th), lambda i, j: (0, n_chunks + j)),
                  pl.BlockSpec((th, D_MODEL), lambda i, j: (j, 0))],
        out_specs=pl.BlockSpec((tm, D_MODEL), lambda i, j: (i, 0)),
        out_shape=jax.ShapeDtypeStruct(x2d.shape, F32),
        scratch_shapes=[pltpu.VMEM((tm, D_MODEL), BF16), pltpu.VMEM((tm, D_MODEL), F32)],
        compiler_params=_cparams(("arbitrary", "arbitrary"), VMEM_LIMIT_BYTES),
        name="ffn",
    )(x2d, g, w_in, w_in, w_out)


def _mem_kv_kernel(m_ref, g_ref, w_ref, kn_ref, gm_ref, o_ref):
    xb = _rms_rows(m_ref[...], g_ref[...]).astype(BF16)
    kv = jnp.dot(xb, w_ref[...], preferred_element_type=F32)
    o_ref[0:XA_WIDTH, :] = (_group_rms(kv[:, 0:XA_WIDTH], gm_ref[...], HEAD_DIM) * kn_ref[...]).T
    o_ref[XA_WIDTH:, :] = kv[:, XA_WIDTH:].T


def _mem_kv(mem, g, wkv, kn):
    b = mem.shape[0]
    gm = _block_ones(XA_WIDTH, HEAD_DIM)
    full = lambda a: pl.BlockSpec(a.shape, lambda i: (0,) * a.ndim)
    return pl.pallas_call(
        _mem_kv_kernel,
        grid=(b,),
        in_specs=[pl.BlockSpec((None, MEM_LEN, D_MODEL), lambda i: (i, 0, 0)), full(g), full(wkv), full(kn), full(gm)],
        out_specs=pl.BlockSpec((None, 2 * XA_WIDTH, MEM_LEN), lambda i: (i, 0, 0)),
        out_shape=jax.ShapeDtypeStruct((b, 2 * XA_WIDTH, MEM_LEN), F32),
        compiler_params=_cparams(("arbitrary",)),
        name="mem_kv",
    )(mem, g, wkv, kn, gm)


def _xattn_kernel(x_ref, mkv_ref, g_ref, wq_ref, wo_ref, qn_ref, gm_ref, o_ref):
    x = x_ref[...]
    xb = _rms_rows(x, g_ref[...]).astype(BF16)
    q = jnp.dot(xb, wq_ref[...], preferred_element_type=F32)
    q = _group_rms(q, gm_ref[...], HEAD_DIM) * qn_ref[...]
    qb = (q * SCALE).astype(BF16)
    outs = []
    for h in range(XA_HEADS):
        kt = mkv_ref[h * HEAD_DIM:(h + 1) * HEAD_DIM, :].astype(BF16)
        vt = mkv_ref[XA_WIDTH + h * HEAD_DIM:XA_WIDTH + (h + 1) * HEAD_DIM, :].astype(BF16)
        s = jnp.dot(qb[:, h * HEAD_DIM:(h + 1) * HEAD_DIM], kt, preferred_element_type=F32)
        e = jnp.exp(s - jnp.max(s, axis=-1, keepdims=True))
        p = (e / jnp.sum(e, axis=-1, keepdims=True)).astype(BF16)
        outs.append(_dot_nt(p, vt))
    o = jnp.concatenate(outs, axis=-1).astype(BF16)
    o_ref[...] = x + jnp.dot(o, wo_ref[...], preferred_element_type=F32)


def _xattn(x3d, mkv_t, layer, g, wq, wo, qn, tm):
    b, t, _ = x3d.shape
    gm = _block_ones(XA_WIDTH, HEAD_DIM)
    full = lambda a: pl.BlockSpec(a.shape, lambda bi, i: (0,) * a.ndim)
    return pl.pallas_call(
        _xattn_kernel,
        grid=(b, t // tm),
        in_specs=[pl.BlockSpec((None, tm, D_MODEL), lambda bi, i: (bi, i, 0)),
                  pl.BlockSpec((None, None, 2 * XA_WIDTH, MEM_LEN), lambda bi, i: (layer, bi, 0, 0)),
                  full(g), full(wq), full(wo), full(qn), full(gm)],
        out_specs=pl.BlockSpec((None, tm, D_MODEL), lambda bi, i: (bi, i, 0)),
        out_shape=jax.ShapeDtypeStruct(x3d.shape, F32),
        compiler_params=_cparams(("arbitrary", "arbitrary"), VMEM_LIMIT_BYTES),
        name="xattn",
    )(x3d, mkv_t, g, wq, wo, qn, gm)


def _gla_kernel(q_ref, k_ref, v_ref, la_ref, og_ref, gn_ref, s0_ref, tri_ref, hsel_ref,
                o_ref, sfin_ref, st_ref, *, n_sub, n_valid):
    ti = pl.program_id(1)

    @pl.when(ti == 0)
    def _():
        st_ref[...] = s0_ref[...]

    tri = tri_ref[...]
    hsel = hsel_ref[...]
    gn = gn_ref[...]
    row = lax.broadcasted_iota(jnp.int32, (SUB, 1), 0)

    def sub_block(i, carry):
        r0 = pl.multiple_of(i * SUB, SUB)
        rows = pl.ds(r0, SUB)
        q = q_ref[rows, :]
        k = k_ref[rows, :]
        v = v_ref[rows, :]
        la = la_ref[rows, :]
        if n_valid is not None:
            live = (row + r0) < n_valid
            la = jnp.where(live, la, 0.0)
            k = jnp.where(live, k, 0.0)
        b = _dot_f32(tri, la)
        b_end = b[SUB - 1:SUB, :]
        qd = (q * jnp.exp(b)).astype(BF16)
        kd = (k * jnp.exp(b_end - b)).astype(BF16)
        vb = v.astype(BF16)
        tiles = []
        for s in range(SUB):
            e = jnp.exp(jnp.minimum(b - b[s:s + 1, :], 0.0))
            z = (q * k[s:s + 1, :]) * e
            tiles.append(jnp.where(row >= s, z, 0.0))
        att = jnp.dot(jnp.concatenate(tiles, axis=0).astype(BF16), hsel, preferred_element_type=F32)
        dec = jnp.exp(b_end)
        outs = []
        for h in range(GLA_HEADS):
            dk = slice(h * GLA_DK, (h + 1) * GLA_DK)
            dv = slice(h * GLA_DV, (h + 1) * GLA_DV)
            st = st_ref[h]
            o = _dot_nt(qd[:, dk], st.astype(BF16))
            for s in range(SUB):
                o = o + att[s * SUB:(s + 1) * SUB, h:h + 1] * v[s:s + 1, dv]
            st_ref[h] = st * dec[:, dk] + _dot_tn(vb[:, dv], kd[:, dk])
            outs.append(_rms_rows(o, gn))
        o_ref[rows, :] = jnp.concatenate(outs, axis=-1) * og_ref[rows, :]
        return carry

    lax.fori_loop(0, n_sub, sub_block, 0)

    @pl.when(ti == pl.num_programs(1) - 1)
    def _():
        sfin_ref[...] = st_ref[...]


def _gla(q, k, v, la, og, gn, s0t, tt, n_valid=None):
    b, t, _ = q.shape
    tri = jnp.asarray(np.tril(np.ones((SUB, SUB), np.float32)))
    hsel = jnp.asarray((np.arange(256)[:, None] // GLA_DK == np.arange(128)[None, :]).astype(np.float32), dtype=BF16)
    seq = lambda w: pl.BlockSpec((None, tt, w), lambda bi, i: (bi, i, 0))
    full = lambda a: pl.BlockSpec(a.shape, lambda bi, i: (0,) * a.ndim)
    st_spec = pl.BlockSpec((None, GLA_HEADS, GLA_DV, GLA_DK), lambda bi, i: (bi, 0, 0, 0))
    return pl.pallas_call(
        functools.partial(_gla_kernel, n_sub=tt // SUB, n_valid=n_valid),
        grid=(b, t // tt),
        in_specs=[seq(256), seq(256), seq(512), seq(256), seq(512), full(gn), st_spec, full(tri), full(hsel)],
        out_specs=[seq(512), st_spec],
        out_shape=[jax.ShapeDtypeStruct((b, t, GLA_WIDTH), F32),
                   jax.ShapeDtypeStruct((b, GLA_HEADS, GLA_DV, GLA_DK), F32)],
        scratch_shapes=[pltpu.VMEM((GLA_HEADS, GLA_DV, GLA_DK), F32)],
        compiler_params=_cparams(("arbitrary", "arbitrary")),
        name="gla",
    )(q, k, v, la, og, gn, s0t, tri, hsel)


def _conv_kernel(u_ref, st0_ref, w_ref, b_ref, g_ref, beta_ref, o_ref, st_ref, ext_ref, *, tt, n_valid):
    ti = pl.program_id(1)
    ctx = CONV_WIDTH - 1

    @pl.when(ti == 0)
    def _():
        ext_ref[0:8, :] = jnp.zeros((8, CONV_CH), F32)
        ext_ref[pl.ds(2, ctx), :] = st0_ref[...]

    ext_ref[pl.ds(32, tt), :] = u_ref[...]
    acc = jnp.zeros((tt, CONV_CH), F32)
    for w in range(CONV_WIDTH):
        acc = acc + ext_ref[pl.ds(2 + w, tt), :] * w_ref[w:w + 1, :]
    y = acc + b_ref[...]
    mu = jnp.mean(y, axis=-1, keepdims=True)
    var = jnp.mean(jnp.square(y - mu), axis=-1, keepdims=True)
    ln = (y - mu) * lax.rsqrt(var + EPS) * g_ref[...] + beta_ref[...]
    o_ref[...] = ln * _sigmoid(ln)

    @pl.when(ti == pl.num_programs(1) - 1)
    def _():
        st_ref[...] = ext_ref[pl.ds(32 + n_valid - ctx, ctx), :]

    ext_ref[0:32, :] = ext_ref[pl.ds(tt, 32), :]


def _conv(u, st0, w, b, g, beta, tt, n_valid):
    bsz, t, _ = u.shape
    ctx = CONV_WIDTH - 1
    full = lambda a: pl.BlockSpec(a.shape, lambda bi, i: (0,) * a.ndim)
    st_spec = pl.BlockSpec((None, ctx, CONV_CH), lambda bi, i: (bi, 0, 0))
    return pl.pallas_call(
        functools.partial(_conv_kernel, tt=tt, n_valid=n_valid),
        grid=(bsz, t // tt),
        in_specs=[pl.BlockSpec((None, tt, CONV_CH), lambda bi, i: (bi, i, 0)), st_spec,
                  full(w), full(b), full(g), full(beta)],
        out_specs=[pl.BlockSpec((None, tt, CONV_CH), lambda bi, i: (bi, i, 0)), st_spec],
        out_shape=[jax.ShapeDtypeStruct(u.shape, F32), jax.ShapeDtypeStruct((bsz, ctx, CONV_CH), F32)],
        scratch_shapes=[pltpu.VMEM((32 + max(tt, 32), CONV_CH), F32)],
        compiler_params=_cparams(("arbitrary", "arbitrary")),
        name="conv",
    )(u, st0, w, b, g, beta)


def _fox_kernel(qt_ref, k_ref, vt_ref, ck_ref, cqt_ref, o_ref, m_ref, l_ref, acc_ref, *, tq):
    qi = pl.program_id(1)
    ki = pl.program_id(2)

    @pl.when(ki == 0)
    def _():
        m_ref[...] = jnp.full_like(m_ref, NEG)
        l_ref[...] = jnp.zeros_like(l_ref)
        acc_ref[...] = jnp.zeros_like(acc_ref)

    def tile(neg):
        for h in range(FOX_HEADS):
            cols = slice(h * HEAD_DIM, (h + 1) * HEAD_DIM)
            s = jnp.dot(k_ref[:, cols], qt_ref[cols, :], preferred_element_type=F32)
            s = s + cqt_ref[h:h + 1, :] - ck_ref[:, h:h + 1]
            if neg is not None:
                s = s + neg
            m_old = m_ref[h:h + 1, :]
            m_new = jnp.maximum(m_old, jnp.max(s, axis=0, keepdims=True))
            p = jnp.exp(s - m_new)
            alpha = jnp.exp(m_old - m_new)
            l_ref[h:h + 1, :] = alpha * l_ref[h:h + 1, :] + jnp.sum(p, axis=0, keepdims=True)
            acc_ref[cols, :] = alpha * acc_ref[cols, :] + jnp.dot(vt_ref[cols, :], p.astype(BF16),
                                                                  preferred_element_type=F32)
            m_ref[h:h + 1, :] = m_new

    @pl.when(ki < qi)
    def _():
        tile(None)

    @pl.when(ki == qi)
    def _():
        s_rel = lax.broadcasted_iota(jnp.int32, (tq, tq), 0)
        t_rel = lax.broadcasted_iota(jnp.int32, (tq, tq), 1)
        tile(jnp.where(s_rel <= t_rel, 0.0, NEG))

    @pl.when(ki == pl.num_programs(2) - 1)
    def _():
        for h in range(FOX_HEADS):
            cols = slice(h * HEAD_DIM, (h + 1) * HEAD_DIM)
            acc_ref[cols, :] = acc_ref[cols, :] / jnp.maximum(l_ref[h:h + 1, :], 1e-30)
        o_ref[...] = acc_ref[...].T


def _fox_prompt(qst, kb, vbt, c8, ct, tq):
    b, t, _ = kb.shape
    nq = t // tq
    kmin = lambda qi, ki: jnp.minimum(ki, qi)
    return pl.pallas_call(
        functools.partial(_fox_kernel, tq=tq),
        grid=(b, nq, nq),
        in_specs=[pl.BlockSpec((FOX_WIDTH, tq), lambda bi, qi, ki: (0, bi * nq + qi)),
                  pl.BlockSpec((None, tq, FOX_WIDTH), lambda bi, qi, ki: (bi, kmin(qi, ki), 0)),
                  pl.BlockSpec((FOX_WIDTH, tq), lambda bi, qi, ki: (0, bi * nq + kmin(qi, ki))),
                  pl.BlockSpec((None, tq, FOX_HEADS), lambda bi, qi, ki: (bi, kmin(qi, ki), 0)),
                  pl.BlockSpec((None, FOX_HEADS, tq), lambda bi, qi, ki: (bi, 0, qi))],
        out_specs=pl.BlockSpec((None, tq, FOX_WIDTH), lambda bi, qi, ki: (bi, qi, 0)),
        out_shape=jax.ShapeDtypeStruct((b, t, FOX_WIDTH), F32),
        scratch_shapes=[pltpu.VMEM((FOX_HEADS, tq), F32), pltpu.VMEM((FOX_HEADS, tq), F32),
                        pltpu.VMEM((FOX_WIDTH, tq), F32)],
        compiler_params=_cparams(("arbitrary", "arbitrary", "arbitrary"), VMEM_LIMIT_BYTES),
        name="fox_prompt",
    )(qst, kb, vbt, c8, ct)


def _compress_weights(w1):
    w = w1.reshape(2, 2, 8, 2, HEAD_DIM, CMP_HIDDEN)
    z = jnp.zeros_like(w)
    g0 = jnp.concatenate([w, z], axis=-1)
    g1 = jnp.concatenate([z, w], axis=-1)
    wbd = jnp.stack([g0, g1], axis=4)
    return wbd.reshape(2, 2, 8, 4 * HEAD_DIM, 2 * CMP_HIDDEN).astype(BF16)


def _compress_compute(src_refs, pe_ref, w1_ref, w2_ref, kn_ref, ck_ref, cv_ref, sh_ref, nseg):
    sh_ref[pl.ds(nseg, 8), :] = jnp.zeros((8, CMP_HIDDEN), F32)
    for j, src_ref in enumerate(src_refs):
        a = jnp.zeros((nseg, 2 * CMP_HIDDEN), F32)
        bm = jnp.zeros((nseg, 2 * CMP_HIDDEN), F32)
        for q in range(CMP_STRIDE // 2):
            p0, p1 = 2 * q, 2 * q + 1
            x0 = src_ref[pl.ds(p0, nseg, stride=CMP_STRIDE), :]
            x1 = src_ref[pl.ds(p1, nseg, stride=CMP_STRIDE), :]
            xa = jnp.concatenate([x0 + pe_ref[j, p0:p0 + 1, :], x1 + pe_ref[j, p1:p1 + 1, :]], axis=1)
            a = a + jnp.dot(xa.astype(BF16), w1_ref[j, 0, q], preferred_element_type=F32)
            p0, p1 = p0 + CMP_STRIDE, p1 + CMP_STRIDE
            xb = jnp.concatenate([x0 + pe_ref[j, p0:p0 + 1, :], x1 + pe_ref[j, p1:p1 + 1, :]], axis=1)
            bm = bm + jnp.dot(xb.astype(BF16), w1_ref[j, 1, q], preferred_element_type=F32)
        for g in range(NSA_KV_HEADS):
            sh_ref[pl.ds(0, nseg), :] = bm[:, g * CMP_HIDDEN:(g + 1) * CMP_HIDDEN]
            x = a[:, g * CMP_HIDDEN:(g + 1) * CMP_HIDDEN] + sh_ref[pl.ds(1, nseg), :]
            hid = x * (0.5 * (1.0 + jnp.tanh(math.sqrt(2.0 / math.pi) * (x + 0.044715 * (x * x * x)))))
            ckv = jnp.dot(hid.astype(BF16), w2_ref[j], preferred_element_type=F32)
            if j == 0:
                ck_ref[g] = _rms_rows(ckv, kn_ref[...]).astype(BF16)
            else:
                cv_ref[g] = ckv.astype(BF16)


def _compress_prompt_kernel(xk_ref, xv_ref, pe_ref, w1_ref, w2_ref, kn_ref, ck_ref, cv_ref, sh_ref, *, nseg):
    _compress_compute((xk_ref, xv_ref), pe_ref, w1_ref, w2_ref, kn_ref, ck_ref, cv_ref, sh_ref, nseg)


def _compress_prompt(kvr, pe, w1, w2, kn):
    b, t, _ = kvr.shape
    nseg = t // CMP_STRIDE
    full = lambda a: pl.BlockSpec(a.shape, lambda i: (0,) * a.ndim)
    o_spec = pl.BlockSpec((None, NSA_KV_HEADS, nseg, HEAD_DIM), lambda i: (i, 0, 0, 0))
    o_shape = jax.ShapeDtypeStruct((b, NSA_KV_HEADS, nseg, HEAD_DIM), BF16)
    return pl.pallas_call(
        functools.partial(_compress_prompt_kernel, nseg=nseg),
        grid=(b,),
        in_specs=[pl.BlockSpec((None, t, LANES), lambda i: (i, 0, 0)), pl.BlockSpec((None, t, LANES), lambda i: (i, 0, 1)),
                  full(pe), full(w1), full(w2), full(kn)],
        out_specs=[o_spec, o_spec],
        out_shape=[o_shape, o_shape],
        scratch_shapes=[pltpu.VMEM((nseg + 8, CMP_HIDDEN), F32)],
        compiler_params=_cparams(("arbitrary",), VMEM_LIMIT_BYTES),
        name="nsa_compress",
    )(kvr, kvr, pe, w1, w2, kn)


def _stack_heads(qs_ref, kh):
    parts = [qs_ref[:, (kh * NSA_GROUP + g) * HEAD_DIM:(kh * NSA_GROUP + g + 1) * HEAD_DIM].astype(F32)
             for g in range(NSA_GROUP)]
    return jnp.concatenate(parts, axis=0).astype(BF16)


def _cmp_topk_kernel(qs_ref, ck_ref, cv_ref, farcol_ref, chi_ref, clo_ref, pool_ref, ocmp_ref, msk_ref,
                     *, tq, nseg, q_base, n_pick, n_blk, mask_t):
    qi = pl.program_id(1)
    G = NSA_GROUP
    q0 = q_base + qi * tq
    nbase = q0 // CMP_STRIDE - 16
    place = (lax.broadcasted_iota(jnp.int32, (32, nseg), 1) - lax.broadcasted_iota(jnp.int32, (32, nseg), 0)) == nbase
    place = jnp.where(place, 1.0, 0.0).astype(BF16)
    t1 = q0 + lax.broadcasted_iota(jnp.int32, (tq, 1), 0)
    t4 = jnp.concatenate([t1] * G, axis=0)
    n_i = lax.broadcasted_iota(jnp.int32, (G * tq, nseg), 1)
    valid = (n_i * CMP_STRIDE + (CMP_BLOCK - 1) <= t4) & (n_i <= nseg - 2)
    blk = lax.broadcasted_iota(jnp.int32, (tq, N_SELBLK), 1)
    cur = lax.shift_right_logical(t1, 6)
    forced = (blk == 0) | (blk == cur) | (blk == cur - 1)
    for kh in range(NSA_KV_HEADS):
        q4 = _stack_heads(qs_ref, kh)
        s = _dot_nt(q4, ck_ref[kh]) + farcol_ref[kh]
        s = s + jnp.dot(chi_ref[kh], place, preferred_element_type=F32) + jnp.dot(clo_ref[kh], place, preferred_element_type=F32)
        s = jnp.where(valid, s, NEG)
        e = jnp.where(valid, jnp.exp(s - jnp.max(s, axis=-1, keepdims=True)), 0.0)
        p = e / jnp.maximum(jnp.sum(e, axis=-1, keepdims=True), 1e-30)
        o = jnp.dot(p.astype(BF16), cv_ref[kh], preferred_element_type=F32)
        for g in range(G):
            h = kh * G + g
            ocmp_ref[:, h * HEAD_DIM:(h + 1) * HEAD_DIM] = o[g * tq:(g + 1) * tq]
        imp = p[0:tq] + p[tq:2 * tq] + p[2 * tq:3 * tq] + p[3 * tq:4 * tq]
        pooled = _dot_f32(imp, pool_ref[...])
        score = jnp.where((blk > cur) | (blk >= n_blk), -1e30, jnp.where(forced, 1e30, pooled))
        sel = jnp.zeros((tq, N_SELBLK), F32)
        for _ in range(n_pick):
            mx = jnp.max(score, axis=-1, keepdims=True)
            first = jnp.min(jnp.where(score == mx, blk, N_SELBLK), axis=-1, keepdims=True)
            pick = blk == first
            sel = jnp.where(pick, 1.0, sel)
            score = jnp.where(pick, -3e38, score)
        msk_ref[kh] = sel.T if mask_t else sel


def _cmp_bias_tables(tbl, tq):
    tr = jnp.arange(tq)[:, None]
    i = jnp.arange(32)[None, :]
    dist = tr + 16 * CMP_STRIDE - CMP_STRIDE * i - (CMP_BLOCK - 1)
    near = tbl[:, _j_rel_bucket(dist)]
    far = tbl[:, NUM_BUCKETS - 1]
    corr = (near - far[:, None, None]).reshape(NSA_KV_HEADS, NSA_GROUP * tq, 32)
    hi = corr.astype(BF16)
    lo = (corr - hi.astype(F32)).astype(BF16)
    farcol = jnp.broadcast_to(far[:, None, None], (NSA_HEADS, tq, 1)).reshape(NSA_KV_HEADS, NSA_GROUP * tq, 1)
    return farcol, hi, lo


def _cmp_topk(qs, ck, cv, tbl, tq, q_base, n_pick, n_blk, mask_t):
    b, t, _ = qs.shape
    nseg = ck.shape[2]
    assert tq <= 256
    farcol, chi, clo = _cmp_bias_tables(tbl, tq)
    pool = jnp.asarray((np.arange(nseg)[:, None] // SEL_RATIO == np.arange(N_SELBLK)[None, :]).astype(np.float32))
    full = lambda a: pl.BlockSpec(a.shape, lambda bi, i: (0,) * a.ndim)
    c_spec = pl.BlockSpec((None, NSA_KV_HEADS, nseg, HEAD_DIM), lambda bi, i: (bi, 0, 0, 0))
    if mask_t:
        m_spec = pl.BlockSpec((None, NSA_KV_HEADS, N_SELBLK, tq), lambda bi, i: (bi, 0, 0, i))
        m_shape = (b, NSA_KV_HEADS, N_SELBLK, t)
    else:
        m_spec = pl.BlockSpec((None, NSA_KV_HEADS, tq, N_SELBLK), lambda bi, i: (bi, 0, i, 0))
        m_shape = (b, NSA_KV_HEADS, t, N_SELBLK)
    return pl.pallas_call(
        functools.partial(_cmp_topk_kernel, tq=tq, nseg=nseg, q_base=q_base, n_pick=n_pick, n_blk=n_blk,
                          mask_t=mask_t),
        grid=(b, t // tq),
        in_specs=[pl.BlockSpec((None, tq, NSA_WIDTH), lambda bi, i: (bi, i, 0)), c_spec, c_spec,
                  full(farcol), full(chi), full(clo), full(pool)],
        out_specs=[pl.BlockSpec((None, tq, NSA_WIDTH), lambda bi, i: (bi, i, 0)), m_spec],
        out_shape=[jax.ShapeDtypeStruct((b, t, NSA_WIDTH), F32), jax.ShapeDtypeStruct(m_shape, F32)],
        compiler_params=_cparams(("arbitrary", "arbitrary"), VMEM_LIMIT_BYTES),
        name="nsa_cmp_topk",
    )(qs, ck, cv, farcol, chi, clo, pool)


def _online_update(s, valid, v, m_ref, l_ref, acc_ref, v_t=False):
    s = jnp.where(valid, s, NEG)
    m_old = m_ref[...]
    m_new = jnp.maximum(m_old, jnp.max(s, axis=-1, keepdims=True))
    p = jnp.where(valid, jnp.exp(s - m_new), 0.0)
    alpha = jnp.exp(m_old - m_new)
    l_ref[...] = alpha * l_ref[...] + jnp.sum(p, axis=-1, keepdims=True)
    pb = p.astype(BF16)
    pv = _dot_nt(pb, v) if v_t else jnp.dot(pb, v, preferred_element_type=F32)
    acc_ref[...] = alpha * acc_ref[...] + pv
    m_ref[...] = m_new


def _selwin_kernel(qt_ref, selk_ref, selvt_ref, wink_ref, winvt_ref, mskt_ref, bias_ref, ocmp_ref, gtt_ref, o_ref,
                   m_ref, l_ref, acc_ref, ot_ref, qt4_ref, *, tq):
    qi = pl.program_id(1)
    G = NSA_GROUP
    blocks_per_tile = tq // SEL_BLOCK
    s_rel = lax.broadcasted_iota(jnp.int32, (tq, tq), 0)
    t_rel = lax.broadcasted_iota(jnp.int32, (tq, tq), 1)
    causal = s_rel <= t_rel

    def init():
        m_ref[...] = jnp.full_like(m_ref, NEG)
        l_ref[...] = jnp.zeros_like(l_ref)
        acc_ref[...] = jnp.zeros_like(acc_ref)

    def update(kh, k, vt, bias_idx, valid):
        s = jnp.dot(k, qt4_ref[...], preferred_element_type=F32) + bias_ref[kh, bias_idx]
        if valid is not None:
            neg = jnp.where(valid, 0.0, NEG)
            s = s + jnp.concatenate([neg] * G, axis=1)
        m_old = m_ref[0:1, :]
        m_new = jnp.maximum(m_old, jnp.max(s, axis=0, keepdims=True))
        p = jnp.exp(s - m_new)
        alpha = jnp.exp(m_old - m_new)
        l_ref[0:1, :] = alpha * l_ref[0:1, :] + jnp.sum(p, axis=0, keepdims=True)
        acc_ref[...] = alpha * acc_ref[...] + jnp.dot(vt, p.astype(BF16), preferred_element_type=F32)
        m_ref[0:1, :] = m_new

    def finish(kh, gate_row):
        for g in range(G):
            h = kh * G + g
            cols = slice(g * tq, (g + 1) * tq)
            rows = slice(h * HEAD_DIM, (h + 1) * HEAD_DIM)
            o = acc_ref[:, cols] / jnp.maximum(l_ref[0:1, cols], 1e-30)
            r = 3 * h + gate_row
            ot_ref[rows, :] += gtt_ref[r:r + 1, :] * o

    oc_t = ocmp_ref[...].T
    for h in range(NSA_HEADS):
        rows = slice(h * HEAD_DIM, (h + 1) * HEAD_DIM)
        ot_ref[rows, :] = gtt_ref[3 * h:3 * h + 1, :] * oc_t[rows, :]

    for kh in range(NSA_KV_HEADS):
        kcols = slice(kh * HEAD_DIM, (kh + 1) * HEAD_DIM)
        vrows = slice(kh * HEAD_DIM, (kh + 1) * HEAD_DIM)
        for g in range(G):
            h = kh * G + g
            qt4_ref[:, g * tq:(g + 1) * tq] = qt_ref[h * HEAD_DIM:(h + 1) * HEAD_DIM, :]

        def sel_valid(j):
            parts = [jnp.broadcast_to(mskt_ref[kh, pl.ds(j * blocks_per_tile + i, 1), :], (SEL_BLOCK, tq))
                     for i in range(blocks_per_tile)]
            return jnp.concatenate(parts, axis=0) > 0.5

        def sel_tile(j, bias_idx, extra):
            start = pl.multiple_of(j * tq, tq)
            valid = sel_valid(j)
            if extra is not None:
                valid = valid & extra
            update(kh, selk_ref[pl.ds(start, tq), kcols], selvt_ref[vrows, pl.ds(start, tq)], bias_idx, valid)

        def win_tile(j, bias_idx, valid):
            start = pl.multiple_of(j * tq, tq)
            update(kh, wink_ref[pl.ds(start, tq), kcols], winvt_ref[vrows, pl.ds(start, tq)], bias_idx, valid)

        init()

        def far_body(j, c):
            sel_tile(j, 2, None)
            return c

        lax.fori_loop(0, jnp.maximum(qi - 1, 0), far_body, 0)

        @pl.when(qi >= 1)
        def _():
            sel_tile(qi - 1, 1, None)

        sel_tile(qi, 0, causal)
        finish(kh, 1)

        init()

        @pl.when(qi >= 2)
        def _():
            win_tile(qi - 2, 2, s_rel >= t_rel)

        @pl.when(qi >= 1)
        def _():
            win_tile(qi - 1, 1, None)

        win_tile(qi, 0, causal)
        finish(kh, 2)

    o_ref[...] = ot_ref[...].T


def _selwin_bias_tables(tbl, tq):
    sr = jnp.arange(tq)[:, None]
    tr = jnp.arange(tq)[None, :]
    near0 = tbl[:, _j_rel_bucket(tr - sr)]
    near1 = tbl[:, _j_rel_bucket(tr - sr + tq)]
    far = jnp.broadcast_to(tbl[:, NUM_BUCKETS - 1][:, None, None], near0.shape)
    b = jnp.stack([near0, near1, far], axis=1)
    b = b.reshape(NSA_KV_HEADS, NSA_GROUP, 3, tq, tq).transpose(0, 2, 3, 1, 4)
    return b.reshape(NSA_KV_HEADS, 3, tq, NSA_GROUP * tq)


def _selwin_prompt(nqst, selkv, selvt, winkv, winvt, mskt, tbl, ocmp, gtt, b, t, tq):
    assert tq >= NSA_WINDOW // 2 and tq >= MAX_DISTANCE and tq % SEL_BLOCK == 0
    bias = _selwin_bias_tables(tbl, tq)
    nq = t // tq
    col_tile = lambda h: pl.BlockSpec((h, tq), lambda bi, i: (0, bi * nq + i))
    row_tile = lambda w: pl.BlockSpec((tq, w), lambda bi, i: (bi * nq + i, 0))
    return pl.pallas_call(
        functools.partial(_selwin_kernel, tq=tq),
        grid=(b, nq),
        in_specs=[col_tile(NSA_WIDTH),
                  pl.BlockSpec((t, LANES), lambda bi, i: (bi, 0)), pl.BlockSpec((LANES, t), lambda bi, i: (0, bi)),
                  pl.BlockSpec((t, LANES), lambda bi, i: (bi, 0)), pl.BlockSpec((LANES, t), lambda bi, i: (0, bi)),
                  pl.BlockSpec((None, NSA_KV_HEADS, N_SELBLK, tq), lambda bi, i: (bi, 0, 0, i)),
                  pl.BlockSpec(bias.shape, lambda bi, i: (0, 0, 0, 0)),
                  row_tile(NSA_WIDTH), col_tile(32)],
        out_specs=row_tile(NSA_WIDTH),
        out_shape=jax.ShapeDtypeStruct((b * t, NSA_WIDTH), F32),
        scratch_shapes=[pltpu.VMEM((8, NSA_GROUP * tq), F32), pltpu.VMEM((8, NSA_GROUP * tq), F32),
                        pltpu.VMEM((HEAD_DIM, NSA_GROUP * tq), F32), pltpu.VMEM((NSA_WIDTH, tq), F32),
                        pltpu.VMEM((HEAD_DIM, NSA_GROUP * tq), BF16)],
        compiler_params=_cparams(("arbitrary", "arbitrary"), VMEM_LIMIT_BYTES),
        name="nsa_selwin",
    )(nqst, selkv, selvt, winkv, winvt, mskt, bias, ocmp, gtt)


PAGES_PER_STEP = 4
PAGE = 128


def _feature_major(cache):
    nd = cache.ndim
    t = jnp.transpose(cache, (0, 1) + tuple(range(3, nd)) + (2,))
    return t.reshape(cache.shape[0], cache.shape[1], -1, cache.shape[2])


def _page_specs(layer, rows, row_block):
    return [pl.BlockSpec((None, None, rows, PAGE), functools.partial(
        lambda bi, j, pt, r: (layer, pt[bi, j * PAGES_PER_STEP + r], row_block, 0), r=r)) for r in range(PAGES_PER_STEP)]


def _compress_sample_kernel(pt_ref, p0_ref, p1_ref, p2_ref, p3_ref, pe_ref, w1_ref, w2_ref, kn_ref,
                            ck_ref, cv_ref, srck_ref, srcv_ref, sh_ref, *, nseg):
    j = pl.program_id(1)
    for r, p_ref in enumerate((p0_ref, p1_ref, p2_ref, p3_ref)):
        rows = pl.ds(pl.multiple_of((j * PAGES_PER_STEP + r) * PAGE, PAGE), PAGE)
        srck_ref[rows, :] = p_ref[0:LANES, :].T
        srcv_ref[rows, :] = p_ref[LANES:2 * LANES, :].T

    @pl.when(j == pl.num_programs(1) - 1)
    def _():
        _compress_compute((srck_ref, srcv_ref), pe_ref, w1_ref, w2_ref, kn_ref, ck_ref, cv_ref, sh_ref, nseg)


def _compress_sample(cache_t, layer, page_table, pe, w1, w2, kn):
    sb, n_pages = page_table.shape
    past = n_pages * PAGE
    nseg = past // CMP_STRIDE
    full = lambda a: pl.BlockSpec(a.shape, lambda bi, j, pt: (0,) * a.ndim)
    o_spec = pl.BlockSpec((None, NSA_KV_HEADS, nseg, HEAD_DIM), lambda bi, j, pt: (bi, 0, 0, 0))
    o_shape = jax.ShapeDtypeStruct((sb, NSA_KV_HEADS, nseg, HEAD_DIM), BF16)
    return pl.pallas_call(
        functools.partial(_compress_sample_kernel, nseg=nseg),
        grid_spec=pltpu.PrefetchScalarGridSpec(
            num_scalar_prefetch=1, grid=(sb, n_pages // PAGES_PER_STEP),
            in_specs=_page_specs(layer, 256, 0) + [full(pe), full(w1), full(w2), full(kn)],
            out_specs=[o_spec, o_spec],
            scratch_shapes=[pltpu.VMEM((past, LANES), F32), pltpu.VMEM((past, LANES), F32),
                            pltpu.VMEM((nseg + 8, CMP_HIDDEN), F32)]),
        out_shape=[o_shape, o_shape],
        compiler_params=_cparams(("arbitrary", "arbitrary"), VMEM_LIMIT_BYTES),
        name="nsa_compress_sample",
    )(page_table, cache_t, cache_t, cache_t, cache_t, pe, w1, w2, kn)


def _selwin_sample_kernel(pt_ref, qs_ref, p0_ref, p1_ref, p2_ref, p3_ref, msk_ref, newkv_ref, winst_ref, newwr_ref,
                          bsel_ref, bnew_ref, bwin_ref, expand_ref, ocmp_ref, gt_ref, o_ref,
                          kv_ref, m_ref, l_ref, acc_ref, osel_ref, owin_ref):
    j = pl.program_id(1)
    G = NSA_GROUP
    R = SAMPLE_ROWS
    for r, p_ref in enumerate((p0_ref, p1_ref, p2_ref, p3_ref)):
        kv_ref[:, pl.ds(pl.multiple_of((j * PAGES_PER_STEP + r) * PAGE, PAGE), PAGE)] = p_ref[...].astype(BF16)

    @pl.when(j == pl.num_programs(1) - 1)
    def _():
        rq = lax.broadcasted_iota(jnp.int32, (G * R, 1), 0) & (R - 1)
        new_valid = lax.broadcasted_iota(jnp.int32, (G * R, R), 1) <= rq
        win_valid = lax.broadcasted_iota(jnp.int32, (G * R, NSA_WINDOW), 1) >= rq

        def init():
            m_ref[...] = jnp.full_like(m_ref, NEG)
            l_ref[...] = jnp.zeros_like(l_ref)
            acc_ref[...] = jnp.zeros_like(acc_ref)

        def finish(dst_ref, kh):
            o = acc_ref[...] / jnp.maximum(l_ref[...], 1e-30)
            for g in range(G):
                h = kh * G + g
                dst_ref[:, h * HEAD_DIM:(h + 1) * HEAD_DIM] = o[g * R:(g + 1) * R]

        for kh in range(NSA_KV_HEADS):
            kcols = slice(kh * HEAD_DIM, (kh + 1) * HEAD_DIM)
            vcols = slice(128 + kh * HEAD_DIM, 128 + (kh + 1) * HEAD_DIM)
            q4 = _stack_heads(qs_ref, kh)
            mskb = msk_ref[kh].astype(BF16)

            init()
            mt = jnp.dot(mskb, expand_ref[...], preferred_element_type=F32) > 0.5
            valid = jnp.concatenate([mt] * G, axis=0)
            s = jnp.dot(q4, kv_ref[kcols, :], preferred_element_type=F32) + bsel_ref[kh]
            _online_update(s, valid, kv_ref[vcols, :], m_ref, l_ref, acc_ref, v_t=True)
            knew = newkv_ref[:, 256 + kh * HEAD_DIM:256 + (kh + 1) * HEAD_DIM].astype(BF16)
            vnew = newkv_ref[:, 384 + kh * HEAD_DIM:384 + (kh + 1) * HEAD_DIM].astype(BF16)
            _online_update(_dot_nt(q4, knew) + bnew_ref[kh], new_valid, vnew, m_ref, l_ref, acc_ref)
            finish(osel_ref, kh)

            init()
            kwin = winst_ref[kcols, :].astype(BF16)
            vwin = winst_ref[vcols, :].astype(BF16)
            _online_update(jnp.dot(q4, kwin, preferred_element_type=F32) + bwin_ref[kh], win_valid, vwin,
                           m_ref, l_ref, acc_ref, v_t=True)
            knew = newwr_ref[:, kcols].astype(BF16)
            vnew = newwr_ref[:, vcols].astype(BF16)
            _online_update(_dot_nt(q4, knew) + bnew_ref[kh], new_valid, vnew, m_ref, l_ref, acc_ref)
            finish(owin_ref, kh)

        for h in range(NSA_HEADS):
            cols = slice(h * HEAD_DIM, (h + 1) * HEAD_DIM)
            o_ref[:, cols] = (gt_ref[:, 3 * h:3 * h + 1] * ocmp_ref[:, cols]
                              + gt_ref[:, 3 * h + 1:3 * h + 2] * osel_ref[:, cols]
                              + gt_ref[:, 3 * h + 2:3 * h + 3] * owin_ref[:, cols])


def _sample_bias_tables(tbl, past):
    R = SAMPLE_ROWS
    r = jnp.arange(R)[:, None]
    stack = lambda a: a.reshape(NSA_KV_HEADS, NSA_GROUP * R, a.shape[-1])
    cached = tbl[:, _j_rel_bucket(past + r - jnp.arange(past)[None, :])]
    new = tbl[:, _j_rel_bucket(r - jnp.arange(R)[None, :])]
    win = tbl[:, _j_rel_bucket(NSA_WINDOW + r - jnp.arange(NSA_WINDOW)[None, :])]
    return stack(cached), stack(new), stack(win)


def _selwin_sample(qs, cache_t, layer, page_table, msk, newkv, win_t, newwr, tbl, ocmp, gt):
    sb, n_pages = page_table.shape
    past = n_pages * PAGE
    assert win_t.shape[-1] == NSA_WINDOW and past >= NSA_WINDOW and past // SEL_BLOCK <= N_SELBLK
    bsel, bnew, bwin = _sample_bias_tables(tbl, past)
    expand = jnp.asarray((np.arange(N_SELBLK)[:, None] == np.arange(past)[None, :] // SEL_BLOCK).astype(np.float32),
                         dtype=BF16)
    R = SAMPLE_ROWS
    full = lambda a: pl.BlockSpec(a.shape, lambda bi, j, pt: (0,) * a.ndim)
    seq = lambda a: pl.BlockSpec((None,) + a.shape[1:], lambda bi, j, pt: (bi,) + (0,) * (a.ndim - 1))
    win_spec = pl.BlockSpec((None, None) + win_t.shape[2:], lambda bi, j, pt: (layer, bi, 0, 0))
    return pl.pallas_call(
        _selwin_sample_kernel,
        grid_spec=pltpu.PrefetchScalarGridSpec(
            num_scalar_prefetch=1, grid=(sb, n_pages // PAGES_PER_STEP),
            in_specs=[seq(qs)] + _page_specs(layer, 256, 1) + [seq(msk), seq(newkv), win_spec, seq(newwr),
                                                               full(bsel), full(bnew), full(bwin), full(expand),
                                                               seq(ocmp), seq(gt)],
            out_specs=pl.BlockSpec((None, R, NSA_WIDTH), lambda bi, j, pt: (bi, 0, 0)),
            scratch_shapes=[pltpu.VMEM((256, past), BF16),
                            pltpu.VMEM((NSA_GROUP * R, 1), F32), pltpu.VMEM((NSA_GROUP * R, 1), F32),
                            pltpu.VMEM((NSA_GROUP * R, HEAD_DIM), F32),
                            pltpu.VMEM((R, NSA_WIDTH), F32), pltpu.VMEM((R, NSA_WIDTH), F32)]),
        out_shape=jax.ShapeDtypeStruct((sb, R, NSA_WIDTH), F32),
        compiler_params=_cparams(("arbitrary", "arbitrary"), VMEM_LIMIT_BYTES),
        name="nsa_selwin_sample",
    )(page_table, qs, cache_t, cache_t, cache_t, cache_t, msk, newkv, win_t, newwr, bsel, bnew, bwin, expand, ocmp, gt)


def _fox_suffix_kernel(pt_ref, p0_ref, p1_ref, p2_ref, p3_ref, sl_ref, d_ref, carry_ref):
    @pl.when(pl.program_id(1) == 0)
    def _():
        carry_ref[...] = jnp.zeros_like(carry_ref)

    carry = carry_ref[:, 0:1]
    for r, p_ref in reversed(list(enumerate((p0_ref, p1_ref, p2_ref, p3_ref)))):
        lf = p_ref[...]
        d_ref[:, r * PAGE:(r + 1) * PAGE] = _dot_f32(lf, sl_ref[...]) + carry
        carry = carry + jnp.sum(lf, axis=1, keepdims=True)
    carry_ref[...] = jnp.broadcast_to(carry, carry_ref.shape)


def _fox_suffix(logf_t, layer, page_table):
    sb, n_pages = page_table.shape
    n_steps = n_pages // PAGES_PER_STEP
    sl = jnp.asarray(np.tril(np.ones((PAGE, PAGE), np.float32), -1))
    specs = [pl.BlockSpec((None, None, FOX_HEADS, PAGE), functools.partial(
        lambda bi, j, pt, r: (layer, pt[bi, (n_steps - 1 - j) * PAGES_PER_STEP + r], 0, 0), r=r))
        for r in range(PAGES_PER_STEP)]
    return pl.pallas_call(
        _fox_suffix_kernel,
        grid_spec=pltpu.PrefetchScalarGridSpec(
            num_scalar_prefetch=1, grid=(sb, n_steps),
            in_specs=specs + [pl.BlockSpec(sl.shape, lambda bi, j, pt: (0, 0))],
            out_specs=pl.BlockSpec((None, FOX_HEADS, PAGES_PER_STEP * PAGE), lambda bi, j, pt: (bi, 0, n_steps - 1 - j)),
            scratch_shapes=[pltpu.VMEM((FOX_HEADS, LANES), F32)]),
        out_shape=jax.ShapeDtypeStruct((sb, FOX_HEADS, n_pages * PAGE), F32),
        compiler_params=_cparams(("arbitrary", "arbitrary")),
        name="fox_suffix",
    )(page_table, logf_t, logf_t, logf_t, logf_t, sl)


def _fox_sample_kernel(pt_ref, qs_ref, p0_ref, p1_ref, p2_ref, p3_ref, dt_ref, newkv_ref, lfnew_ref, hmask_ref,
                       o_ref, qbd_ref, crel_ref, m_ref, l_ref, acc_ref):
    j = pl.program_id(1)
    R = SAMPLE_ROWS
    H = FOX_HEADS
    tk = PAGES_PER_STEP * PAGE

    @pl.when(j == 0)
    def _():
        q = qs_ref[...].astype(F32)
        qbd_ref[...] = (jnp.concatenate([q] * H, axis=0) * hmask_ref[...]).astype(BF16)
        tri = jnp.where(lax.broadcasted_iota(jnp.int32, (R, R), 1) <= lax.broadcasted_iota(jnp.int32, (R, R), 0), 1.0, 0.0)
        crel = _dot_f32(tri, lfnew_ref[...])
        crel_ref[...] = jnp.concatenate([crel[:, h:h + 1] for h in range(H)], axis=0)
        m_ref[...] = jnp.full_like(m_ref, NEG)
        l_ref[...] = jnp.zeros_like(l_ref)
        acc_ref[...] = jnp.zeros_like(acc_ref)

    pages = (p0_ref, p1_ref, p2_ref, p3_ref)
    kt = jnp.concatenate([p[0:FOX_WIDTH, :] for p in pages], axis=1).astype(BF16)
    vt = jnp.concatenate([p[FOX_WIDTH:, :] for p in pages], axis=1).astype(BF16)
    drows = jnp.concatenate([jnp.broadcast_to(dt_ref[h:h + 1, :], (R, tk)) for h in range(H)], axis=0)
    s = jnp.dot(qbd_ref[...], kt, preferred_element_type=F32) + crel_ref[...] + drows
    _online_update(s, s > 2 * NEG, vt, m_ref, l_ref, acc_ref, v_t=True)

    @pl.when(j == pl.num_programs(1) - 1)
    def _():
        lf = lfnew_ref[...]
        iu = lax.broadcasted_iota(jnp.int32, (R, R), 0)
        ir = lax.broadcasted_iota(jnp.int32, (R, R), 1)
        a_le = jnp.where(ir <= iu, 1.0, 0.0)
        b_gt = jnp.where(iu > ir, 1.0, 0.0)
        dnew = jnp.concatenate([_dot_f32(a_le, lf[:, h:h + 1] * b_gt) for h in range(H)], axis=0)
        rq = lax.broadcasted_iota(jnp.int32, (H * R, 1), 0) & (R - 1)
        valid = lax.broadcasted_iota(jnp.int32, (H * R, R), 1) <= rq
        knew = newkv_ref[:, 0:FOX_WIDTH].astype(BF16)
        vnew = newkv_ref[:, FOX_WIDTH:].astype(BF16)
        _online_update(_dot_nt(qbd_ref[...], knew) + dnew, valid, vnew, m_ref, l_ref, acc_ref)
        o = (acc_ref[...] / jnp.maximum(l_ref[...], 1e-30)) * hmask_ref[...]
        out = o[0:R]
        for h in range(1, H):
            out = out + o[h * R:(h + 1) * R]
        o_ref[...] = out


def _fox_sample(qs, kv_t, layer, page_table, dt, newkv, lfnew):
    sb, n_pages = page_table.shape
    R = SAMPLE_ROWS
    tk = PAGES_PER_STEP * PAGE
    hmask = jnp.asarray((np.arange(FOX_HEADS * R)[:, None] // R == np.arange(FOX_WIDTH)[None, :] // HEAD_DIM)
                        .astype(np.float32))
    seq = lambda a: pl.BlockSpec((None,) + a.shape[1:], lambda bi, j, pt: (bi,) + (0,) * (a.ndim - 1))
    return pl.pallas_call(
        _fox_sample_kernel,
        grid_spec=pltpu.PrefetchScalarGridSpec(
            num_scalar_prefetch=1, grid=(sb, n_pages // PAGES_PER_STEP),
            in_specs=[seq(qs)] + _page_specs(layer, 2 * FOX_WIDTH, 0)
                     + [pl.BlockSpec((None, FOX_HEADS, tk), lambda bi, j, pt: (bi, 0, j)), seq(newkv), seq(lfnew),
                        pl.BlockSpec(hmask.shape, lambda bi, j, pt: (0, 0))],
            out_specs=pl.BlockSpec((None, R, FOX_WIDTH), lambda bi, j, pt: (bi, 0, 0)),
            scratch_shapes=[pltpu.VMEM((FOX_HEADS * R, FOX_WIDTH), BF16), pltpu.VMEM((FOX_HEADS * R, 1), F32),
                            pltpu.VMEM((FOX_HEADS * R, 1), F32), pltpu.VMEM((FOX_HEADS * R, 1), F32),
                            pltpu.VMEM((FOX_HEADS * R, FOX_WIDTH), F32)]),
        out_shape=jax.ShapeDtypeStruct((sb, R, FOX_WIDTH), F32),
        compiler_params=_cparams(("arbitrary", "arbitrary"), VMEM_LIMIT_BYTES),
        name="fox_sample",
    )(page_table, qs, kv_t, kv_t, kv_t, kv_t, dt, newkv, lfnew, hmask)


def _j_rmsnorm(x, g):
    xf = x.astype(jnp.float32)
    y = xf * lax.rsqrt(jnp.mean(xf * xf, axis=-1, keepdims=True) + EPS)
    return (y * g.astype(jnp.float32)).astype(x.dtype)


def _j_masked_softmax(s, mask):
    s = jnp.where(mask, s, -1e30)
    p = jnp.exp(s - jnp.max(s, axis=-1, keepdims=True)) * mask
    return p / jnp.maximum(jnp.sum(p, axis=-1, keepdims=True), 1e-30)


def _j_rel_bucket(dist):
    exact = NUM_BUCKETS // 2
    d = jnp.maximum(dist, 0)
    log_ratio = jnp.log(jnp.maximum(d, 1).astype(jnp.float32) / exact) / math.log(MAX_DISTANCE / exact)
    large = jnp.minimum(exact + (log_ratio * (NUM_BUCKETS - exact)).astype(jnp.int32), NUM_BUCKETS - 1)
    return jnp.where(d < exact, d, large)


def _j_gather_pages(pool, page_table):
    g = pool.reshape(pool.shape[0], -1)[page_table]
    g = g.reshape(page_table.shape + pool.shape[1:])
    return g.reshape((g.shape[0], g.shape[1] * g.shape[2]) + g.shape[3:])


def _j_nsa_prepare(kv_rows, pe, w1, w2, kn_cmp):
    B, T = kv_rows.shape[:2]
    nc = (T - CMP_BLOCK) // CMP_STRIDE + 1
    seg = kv_rows[:, :(nc + 1) * CMP_STRIDE, 0:2].reshape(B, nc + 1, CMP_STRIDE, 2, NSA_KV_HEADS, HEAD_DIM)
    blk = jnp.concatenate([seg[:, :-1], seg[:, 1:]], axis=2)
    blk = blk + pe.transpose(1, 0, 2)[None, None, :, :, None, :]
    flat = blk.transpose(0, 1, 3, 4, 2, 5).reshape(B, nc, 2, NSA_KV_HEADS, CMP_BLOCK * HEAD_DIM)
    hid = jax.nn.gelu(jnp.einsum('bnjgf,jfe->bnjge', flat, w1))
    ckv = jnp.einsum('bnjge,jed->bnjgd', hid, w2)
    ck = _j_rmsnorm(ckv[:, :, 0], kn_cmp)
    cv = ckv[:, :, 1]
    c_end = jnp.arange(nc) * CMP_STRIDE + (CMP_BLOCK - 1)
    nbs = -(-T // SEL_BLOCK)
    sel = jnp.pad(kv_rows[:, :, 2:4], ((0, 0), (0, nbs * SEL_BLOCK - T), (0, 0), (0, 0), (0, 0)))
    sel = sel.reshape(B, nbs, SEL_BLOCK, 2, NSA_KV_HEADS, HEAD_DIM).transpose(3, 0, 4, 1, 2, 5)
    return ck, cv, c_end, sel[0], sel[1]


def _j_nsa_attend(q, gates, q_pos, ck, cv, c_end, sk, sv, wk, wv, w_pos, rel_bias):
    B, Q = q.shape[:2]
    scale = HEAD_DIM ** -0.5
    tbl = rel_bias.astype(jnp.float32).T.reshape(NSA_KV_HEADS, NSA_GROUP, NUM_BUCKETS)
    qg = q.reshape(B, Q, NSA_KV_HEADS, NSA_GROUP, HEAD_DIM).transpose(0, 2, 3, 1, 4)
    t = q_pos[:, None]
    s = jnp.einsum('bkgqd,bnkd->bkgqn', qg, ck).astype(jnp.float32) * scale + tbl[:, :, _j_rel_bucket(t - c_end[None])]
    p_cmp = _j_masked_softmax(s, c_end[None] <= t)
    o_cmp = jnp.einsum('bkgqn,bnkd->bkgqd', p_cmp.astype(cv.dtype), cv)
    nbs = sk.shape[2]
    imp = jnp.sum(p_cmp, axis=2)
    nc = imp.shape[-1]
    imp = jnp.pad(imp, ((0, 0), (0, 0), (0, 0), (0, nbs * SEL_RATIO - nc)))
    imp = imp.reshape(B, NSA_KV_HEADS, Q, nbs, SEL_RATIO).sum(-1)
    blk = jnp.arange(nbs)[None]
    cur = (q_pos // SEL_BLOCK)[:, None]
    forced = (blk == 0) | (blk == cur) | (blk == cur - 1)
    score = jnp.where(blk > cur, -jnp.inf, jnp.where(forced, jnp.inf, imp))
    n_sel = min(SEL_TOPN, nbs)
    _, idx = lax.top_k(score, n_sel)
    pick = jax.vmap(jax.vmap(lambda kb, ix: kb[ix]))
    gk = pick(sk, idx).reshape(B, NSA_KV_HEADS, Q, n_sel * SEL_BLOCK, HEAD_DIM)
    gv = pick(sv, idx).reshape(B, NSA_KV_HEADS, Q, n_sel * SEL_BLOCK, HEAD_DIM)
    spos = (idx[..., None] * SEL_BLOCK + jnp.arange(SEL_BLOCK)).reshape(B, NSA_KV_HEADS, Q, n_sel * SEL_BLOCK)
    kidx = jnp.arange(NSA_KV_HEADS)[None, :, None, None, None]
    gidx = jnp.arange(NSA_GROUP)[None, None, :, None, None]
    bias = tbl[kidx, gidx, _j_rel_bucket(t - spos)[:, :, None]]
    s = jnp.einsum('bkgqd,bkqsd->bkgqs', qg, gk).astype(jnp.float32) * scale + bias
    p = _j_masked_softmax(s, (spos <= t)[:, :, None])
    o_sel = jnp.einsum('bkgqs,bkqsd->bkgqd', p.astype(gv.dtype), gv)
    dist = t - w_pos[None]
    s = jnp.einsum('bkgqd,bwkd->bkgqw', qg, wk).astype(jnp.float32) * scale + tbl[:, :, _j_rel_bucket(dist)]
    p = _j_masked_softmax(s, (dist >= 0) & (dist <= NSA_WINDOW) & (w_pos[None] >= 0))
    o_win = jnp.einsum('bkgqw,bwkd->bkgqd', p.astype(wv.dtype), wv)
    o = jnp.stack([o_cmp, o_sel, o_win], axis=-1).transpose(0, 3, 1, 2, 4, 5)
    o = o.reshape(B, Q, NSA_HEADS, HEAD_DIM, 3)
    return jnp.einsum('bqhdr,bqhr->bqhd', o, gates.astype(o.dtype)).reshape(B, Q, NSA_WIDTH)


def _j_nsa_prompt(q, gates, kv_rows, win_rows, pe, w1, w2, kn_cmp, rel_bias):
    B, T = q.shape[:2]
    QB = 128
    ck, cv, c_end, sk, sv = _j_nsa_prepare(kv_rows, pe, w1, w2, kn_cmp)
    win_pad = jnp.pad(win_rows, ((0, 0), (NSA_WINDOW, 0), (0, 0), (0, 0), (0, 0)))

    def block(i):
        q0 = i * QB
        qb = lax.dynamic_slice_in_dim(q, q0, QB, axis=1)
        gb = lax.dynamic_slice_in_dim(gates, q0, QB, axis=1)
        wb = lax.dynamic_slice_in_dim(win_pad, q0, NSA_WINDOW + QB, axis=1)
        q_pos = q0 + jnp.arange(QB)
        w_pos = q0 - NSA_WINDOW + jnp.arange(NSA_WINDOW + QB)
        return _j_nsa_attend(qb, gb, q_pos, ck, cv, c_end, sk, sv, wb[:, :, 0], wb[:, :, 1], w_pos, rel_bias)

    o = lax.map(block, jnp.arange(T // QB))
    return o.transpose(1, 0, 2, 3).reshape(B, T, NSA_WIDTH)


def _j_fox_attend(q, cq, q_pos, k, v, ck, k_pos):
    s = jnp.einsum('bqhd,bshd->bhqs', q, k).astype(jnp.float32) * HEAD_DIM ** -0.5
    s = s + jnp.swapaxes(cq, 1, 2)[..., :, None] - jnp.swapaxes(ck, 1, 2)[..., None, :]
    p = _j_masked_softmax(s, k_pos[None, :] <= q_pos[:, None])
    return jnp.einsum('bhqs,bshd->bqhd', p.astype(v.dtype), v)


def _pad_cols(w, width):
    return jnp.pad(w, ((0, 0), (0, width - w.shape[1])))


def _prep_even_w(w):
    gq, gk, gv, glr, gog, nq, nkv, ng = _split(w, EVEN_SIZES)
    return jnp.concatenate([gq, gk, gv, gog, nq, nkv, _pad_cols(glr, LANES), _pad_cols(ng, LANES)], axis=1).astype(BF16)


def _prep_odd_w(w):
    fq, fk, fv, ff, cg = _split(w, ODD_SIZES)
    return jnp.concatenate([fq, fk, fv, cg, _pad_cols(ff, LANES)], axis=1).astype(BF16)


def _row(v):
    return v.reshape(1, -1).astype(F32)


def _tile_row(v, reps):
    return jnp.tile(v.astype(F32), reps).reshape(1, -1)


def kernel(x_prompt, x_sample, cache_nsa_kv, state_nsa_win, state_gla, cache_fox_kv, cache_fox_logf, state_conv, cache_mem_kv, page_table, mem_prompt, rel_bias, norm_mix, norm_xattn, norm_ffn, even_w_in, even_w_out, gla_w_gate, gla_b_gate, gla_out_norm, nsa_q_norm, nsa_k_norm, nsa_cmp_pe, nsa_cmp_w1, nsa_cmp_w2, odd_w_in, odd_w_out, fox_q_norm, fox_k_norm, fox_b_f, conv_w, conv_b, conv_ln_g, conv_ln_b, mem_norm, xa_wq, xa_wkv, xa_wo, xa_q_norm, xa_k_norm, ffn_w_in, ffn_w_out):
    B, T, _ = x_prompt.shape
    SB, SQ, _ = x_sample.shape
    depth = norm_mix.shape[0]
    past_len = page_table.shape[1] * cache_nsa_kv.shape[2]
    n_win = state_nsa_win.shape[2]
    dec_pos = past_len + jnp.arange(SQ)
    MP = B * T
    SR = SAMPLE_ROWS
    MS = SB * SR

    yp = x_prompt.reshape(MP, D_MODEL)
    ys = jnp.pad(x_sample, ((0, 0), (0, SR - SQ), (0, 0))).reshape(MS, D_MODEL)

    nsa_cache_t = _feature_major(cache_nsa_kv)
    fox_cache_t = _feature_major(cache_fox_kv)
    logf_t = _feature_major(cache_fox_logf)
    win_t = _feature_major(state_nsa_win)
    mem_cache_t = _feature_major(cache_mem_kv)

    nsa_kv_p, nsa_kv_s, win_p, win_s, gla_p, gla_s = [], [], [], [], [], []
    fox_kv_p, fox_kv_s, logf_p, logf_s, conv_p, conv_s, mem_kv_p = [], [], [], [], [], [], []

    for layer in range(depth):
        if layer % 2 == 0:
            e = layer // 2
            w_pad = _prep_even_w(even_w_in[e])
            wg_pad = jnp.pad(gla_w_gate[e], ((0, LANES - GLA_RANK), (0, 0))).astype(BF16)
            bg = _row(gla_b_gate[e])
            qn = _tile_row(nsa_q_norm[e], NSA_HEADS)
            kn1 = _tile_row(nsa_k_norm[e, 1], NSA_KV_HEADS)
            kn2 = _tile_row(nsa_k_norm[e, 2], NSA_KV_HEADS)
            gn = _row(gla_out_norm[e])
            w_out = even_w_out[e].astype(BF16)
            g_mix = _row(norm_mix[layer])
            pe = jnp.tile(nsa_cmp_pe[e].astype(F32), (1, 1, NSA_KV_HEADS))
            w1b = _compress_weights(nsa_cmp_w1[e])
            w2b = nsa_cmp_w2[e].astype(BF16)
            kn0 = _row(nsa_k_norm[e, 0])
            tbl = rel_bias.astype(F32).T
            (q, k, v, la, og, nqs, kvr, wr, gt, selkv, winkv, nqst, selvt, winvt, gtt) = _even_in(
                yp, g_mix, w_pad, wg_pad, bg, qn, kn1, kn2, 256)
            r3 = lambda a: a.reshape(B, T, a.shape[-1])
            s0t = jnp.zeros((B, GLA_HEADS, GLA_DV, GLA_DK), F32)
            o_gla, sfin_t = _gla(r3(q), r3(k), r3(v), r3(la), r3(og), gn, s0t, 256)
            kvr5 = kvr.reshape(B, T, 4, NSA_KV_HEADS, HEAD_DIM)
            wr5 = wr.reshape(B, T, 2, NSA_KV_HEADS, HEAD_DIM)
            ck, cv = _compress_prompt(r3(kvr), pe, w1b, w2b, kn0)
            ocmp, mskt = _cmp_topk(r3(nqs), ck, cv, tbl, 256, 0, SEL_TOPN, T // SEL_BLOCK, True)
            o_nsa = _selwin_prompt(nqst, selkv, selvt, winkv, winvt, mskt, tbl, ocmp.reshape(MP, NSA_WIDTH), gtt,
                                   B, T, 256)
            yp = _out_proj(yp, o_gla.reshape(MP, GLA_WIDTH), o_nsa.reshape(MP, NSA_WIDTH),
                           w_out[:GLA_WIDTH], w_out[GLA_WIDTH:], 512)
            nsa_kv_p.append(kvr5)
            win_p.append(wr5[:, -min(NSA_WINDOW, T):])
            gla_p.append(jnp.swapaxes(sfin_t, -1, -2))
            (q, k, v, la, og, nqs, kvr, wr, gt) = _even_in(ys, g_mix, w_pad, wg_pad, bg, qn, kn1, kn2, MS)[:9]
            pad16 = lambda a: jnp.pad(a.reshape(SB, SR, a.shape[-1]), ((0, 0), (0, SUB - SR), (0, 0)))
            s0t = jnp.swapaxes(state_gla[e], -1, -2)
            o_gla, snew_t = _gla(pad16(q), pad16(k), pad16(v), pad16(la), pad16(og), gn, s0t, SUB, n_valid=SQ)
            o_gla = o_gla[:, :SR]
            s3 = lambda a: a.reshape(SB, SR, a.shape[-1])
            kvr5 = kvr.reshape(SB, SR, 4, NSA_KV_HEADS, HEAD_DIM)[:, :SQ]
            wr5 = wr.reshape(SB, SR, 2, NSA_KV_HEADS, HEAD_DIM)[:, :SQ]
            ck, cv = _compress_sample(nsa_cache_t, e, page_table, pe, w1b, w2b, kn0)
            ocmp, msk = _cmp_topk(s3(nqs), ck, cv, tbl, SR, past_len, SEL_TOPN - 1, past_len // SEL_BLOCK, False)
            o_nsa = _selwin_sample(s3(nqs), nsa_cache_t, e, page_table, msk, s3(kvr), win_t, s3(wr), tbl, ocmp, s3(gt))
            ys = _out_proj(ys, o_gla.reshape(MS, GLA_WIDTH), o_nsa.reshape(MS, NSA_WIDTH),
                           w_out[:GLA_WIDTH], w_out[GLA_WIDTH:], MS)
            nsa_kv_s.append(kvr5)
            win_s.append(jnp.concatenate([state_nsa_win[e][:, SQ:], wr5], axis=1))
            gla_s.append(jnp.swapaxes(snew_t, -1, -2))
        else:
            j = layer // 2
            w_pad = _prep_odd_w(odd_w_in[j])
            qn = _tile_row(fox_q_norm[j], FOX_HEADS)
            kn = _tile_row(fox_k_norm[j], FOX_HEADS)
            bf_pad = jnp.pad(fox_b_f[j].astype(F32), (0, LANES - FOX_HEADS)).reshape(1, LANES)
            w_out = odd_w_out[j].astype(BF16)
            g_mix = _row(norm_mix[layer])
            cw = conv_w[j].astype(F32)
            cb, cg_, cbeta = _row(conv_b[j]), _row(conv_ln_g[j]), _row(conv_ln_b[j])
            qs, kv, kb, lf, c, u, qst, vbt = _odd_in(yp.reshape(B, T, D_MODEL), g_mix, w_pad, qn, kn, bf_pad, 256)
            c8 = c[:, :, :FOX_HEADS]
            o_fox = _fox_prompt(qst, kb, vbt, c8, jnp.swapaxes(c8, 1, 2), 512)
            o_conv, cst = _conv(u, jnp.zeros((B, CONV_WIDTH - 1, CONV_CH), F32), cw, cb, cg_, cbeta, 512, 512)
            yp = _out_proj(yp, o_fox.reshape(MP, FOX_WIDTH), o_conv.reshape(MP, CONV_CH),
                           w_out[:FOX_WIDTH], w_out[FOX_WIDTH:], 512)
            fox_kv_p.append(kv.reshape(B, T, 2, FOX_HEADS, HEAD_DIM))
            logf_p.append(lf[:, :, :FOX_HEADS])
            conv_p.append(cst)
            qs, kv, kb, lf, c, u = _odd_in(ys.reshape(1, MS, D_MODEL), g_mix, w_pad, qn, kn, bf_pad, MS)[:6]
            s3 = lambda a: a.reshape(SB, SR, a.shape[-1])
            new_kv = kv.reshape(SB, SR, 2, FOX_HEADS, HEAD_DIM)[:, :SQ]
            lf_new = lf.reshape(SB, SR, LANES)[:, :SQ, :FOX_HEADS]
            dsuf = _fox_suffix(logf_t, j, page_table)
            o_fox = _fox_sample(s3(qs), fox_cache_t, j, page_table, dsuf, s3(kv), s3(lf))
            o_conv, cst = _conv(u.reshape(SB, SR, CONV_CH), state_conv[j], cw, cb, cg_, cbeta, SR, SQ)
            ys = _out_proj(ys, o_fox.reshape(MS, FOX_WIDTH), o_conv.reshape(MS, CONV_CH),
                           w_out[:FOX_WIDTH], w_out[FOX_WIDTH:], MS)
            fox_kv_s.append(new_kv)
            logf_s.append(lf_new)
            conv_s.append(cst)
        g_xa = _row(norm_xattn[layer])
        wq = xa_wq[layer].astype(BF16)
        wo = xa_wo[layer].astype(BF16)
        xqn = _tile_row(xa_q_norm[layer], XA_HEADS)
        mkv_t = _mem_kv(mem_prompt, _row(mem_norm[layer]), xa_wkv[layer].astype(BF16), _tile_row(xa_k_norm[layer], XA_HEADS))
        mem_kv_p.append(jnp.transpose(mkv_t.reshape(B, 2, XA_HEADS, HEAD_DIM, MEM_LEN), (0, 4, 1, 2, 3)))
        yp = _xattn(yp.reshape(B, T, D_MODEL), mkv_t[None], 0, g_xa, wq, wo, xqn, 512).reshape(MP, D_MODEL)
        ys = _xattn(ys.reshape(SB, SR, D_MODEL), mem_cache_t, layer, g_xa, wq, wo, xqn, SR).reshape(MS, D_MODEL)
        g_ffn = _row(norm_ffn[layer])
        w_in = ffn_w_in[layer].astype(BF16)
        w_o = ffn_w_out[layer].astype(BF16)
        yp = _ffn(yp, g_ffn, w_in, w_o, 512)
        ys = _ffn(ys, g_ffn, w_in, w_o, MS)

    yp = yp.reshape(B, T, D_MODEL)
    ys = ys.reshape(SB, SR, D_MODEL)[:, :SQ]
    return (yp, ys,
            jnp.stack(nsa_kv_p), jnp.stack(nsa_kv_s), jnp.stack(win_p), jnp.stack(win_s),
            jnp.stack(gla_p), jnp.stack(gla_s), jnp.stack(fox_kv_p), jnp.stack(fox_kv_s),
            jnp.stack(logf_p), jnp.stack(logf_s), jnp.stack(conv_p), jnp.stack(conv_s),
            jnp.stack(mem_kv_p))
```

```python
import functools
import math

import jax
import jax.numpy as jnp
import numpy as np
from jax import lax
from jax.experimental import pallas as pl
from jax.experimental.pallas import tpu as pltpu

F32 = jnp.float32
BF16 = jnp.bfloat16

D_MODEL = 1024
HEAD_DIM = 64
GLA_WIDTH = 512
GLA_HEADS = 4
GLA_DV = 128
GLA_DK = 64
GLA_RANK = 16
GLA_TAU = 16.0
NSA_WIDTH = 512
NSA_HEADS = 8
NSA_KV_HEADS = 2
NSA_GROUP = 4
CMP_STRIDE = 16
CMP_BLOCK = 32
CMP_HIDDEN = 256
SEL_BLOCK = 64
SEL_RATIO = 4
SEL_TOPN = 16
NSA_WINDOW = 512
FOX_WIDTH = 512
FOX_HEADS = 8
CONV_CH = 512
CONV_WIDTH = 31
MEM_LEN = 256
XA_HEADS = 4
XA_WIDTH = 256
FFN_HIDDEN = 2816
NUM_BUCKETS = 32
MAX_DISTANCE = 128
EPS = 1e-6
SCALE = HEAD_DIM ** -0.5
NEG = -1e30

EVEN_SIZES = (256, 256, 512, 16, 512, 512, 768, 24)
ODD_SIZES = (512, 512, 512, 8, 1024)

LANES = 128
VMEM_LIMIT_BYTES = 56 * 1024 * 1024
SAMPLE_ROWS = 8
SUB = 16
N_CMP = 512
N_SELBLK = 128


def _cparams(sem, vmem=None):
    return pltpu.CompilerParams(dimension_semantics=sem, vmem_limit_bytes=vmem)


def _split(h, sizes):
    return jnp.split(h, np.cumsum(sizes)[:-1].tolist(), axis=-1)


def _rms_rows(x, g):
    return x * lax.rsqrt(jnp.mean(x * x, axis=-1, keepdims=True) + EPS) * g


def _group_rms(x, gmat, gs):
    x2 = x * x
    hi = x2.astype(BF16)
    lo = (x2 - hi.astype(F32)).astype(BF16)
    ms = (jnp.dot(hi, gmat, preferred_element_type=F32) + jnp.dot(lo, gmat, preferred_element_type=F32)) * (1.0 / gs)
    return x * lax.rsqrt(ms + EPS)


def _log_sigmoid(z):
    return -(jnp.maximum(-z, 0.0) + jnp.log1p(jnp.exp(-jnp.abs(z))))


def _sigmoid(z):
    return 1.0 / (1.0 + jnp.exp(-z))


def _dot_nt(a, b):
    return lax.dot_general(a, b, (((1,), (1,)), ((), ())), preferred_element_type=F32)


def _dot_tn(a, b):
    return lax.dot_general(a, b, (((0,), (0,)), ((), ())), preferred_element_type=F32)


def _dot_f32(a, b):
    return jnp.dot(a, b, preferred_element_type=F32, precision=lax.Precision.HIGHEST)


def _block_ones(width, gs):
    r = np.arange(width) // gs
    return jnp.asarray((r[:, None] == r[None, :]).astype(np.float32), dtype=BF16)


def _even_in_kernel(x_ref, g_ref, w_ref, wg_ref, bg_ref, qn_ref, kn1_ref, kn2_ref, gm512_ref, gm128_ref,
                    q_ref, k_ref, v_ref, la_ref, og_ref, nqs_ref, kvr_ref, wr_ref, gt_ref, selkv_ref, winkv_ref,
                    nqst_ref, selvt_ref, winvt_ref, gtt_ref):
    xb = _rms_rows(x_ref[...], g_ref[...]).astype(BF16)

    def proj(lo, hi):
        return jnp.dot(xb, w_ref[:, lo:hi], preferred_element_type=F32)

    q_ref[...] = proj(0, 256) * (GLA_DK ** -0.5)
    k_ref[...] = proj(256, 512)
    v_ref[...] = proj(512, 1024)
    og = proj(1024, 1536)
    og_ref[...] = og * _sigmoid(og)
    nq = _group_rms(proj(1536, 2048), gm512_ref[...], HEAD_DIM) * qn_ref[...] * SCALE
    nqs_ref[...] = nq.astype(BF16)
    nqst_ref[...] = nq.T.astype(BF16)
    kvr_ref[:, 0:256] = proj(2048, 2304)
    selk = _group_rms(proj(2304, 2432), gm128_ref[...], HEAD_DIM) * kn1_ref[...]
    selv = proj(2432, 2560)
    kvr_ref[:, 256:384] = selk
    kvr_ref[:, 384:512] = selv
    selkv_ref[:, 0:128] = selk.astype(BF16)
    selkv_ref[:, 128:256] = selv.astype(BF16)
    selvt_ref[...] = selv.T.astype(BF16)
    wink = _group_rms(proj(2560, 2688), gm128_ref[...], HEAD_DIM) * kn2_ref[...]
    winv = proj(2688, 2816)
    wr_ref[:, 0:128] = wink
    wr_ref[:, 128:256] = winv
    winkv_ref[:, 0:128] = wink.astype(BF16)
    winkv_ref[:, 128:256] = winv.astype(BF16)
    winvt_ref[...] = winv.T.astype(BF16)
    glr = proj(2816, 2944).astype(BF16)
    z = jnp.dot(glr, wg_ref[...], preferred_element_type=F32) + bg_ref[...]
    la_ref[...] = _log_sigmoid(z) * (1.0 / GLA_TAU)
    gates = _sigmoid(proj(2944, 3072))
    gt_ref[...] = gates
    gtt_ref[...] = gates.T[0:32, :]


def _even_in(x2d, g, w_pad, wg_pad, bg, qn, kn1, kn2, tm):
    m = x2d.shape[0]
    widths = (256, 256, 512, 256, 512, 512, 512, 256, 128, 256, 256)
    dtypes = (F32, F32, F32, F32, F32, BF16, F32, F32, F32, BF16, BF16)
    t_heights = (512, 128, 128, 32)
    t_dtypes = (BF16, BF16, BF16, F32)
    full = lambda a: pl.BlockSpec(a.shape, lambda i: (0,) * a.ndim)
    gm512 = _block_ones(512, HEAD_DIM)
    gm128 = _block_ones(128, HEAD_DIM)
    ins = (x2d, g, w_pad, wg_pad, bg, qn, kn1, kn2, gm512, gm128)
    return pl.pallas_call(
        _even_in_kernel,
        grid=(m // tm,),
        in_specs=[pl.BlockSpec((tm, D_MODEL), lambda i: (i, 0))] + [full(a) for a in ins[1:]],
        out_specs=[pl.BlockSpec((tm, w), lambda i: (i, 0)) for w in widths]
                  + [pl.BlockSpec((h, tm), lambda i: (0, i)) for h in t_heights],
        out_shape=[jax.ShapeDtypeStruct((m, w), d) for w, d in zip(widths, dtypes)]
                  + [jax.ShapeDtypeStruct((h, m), d) for h, d in zip(t_heights, t_dtypes)],
        compiler_params=_cparams(("arbitrary",), VMEM_LIMIT_BYTES),
        name="even_in",
    )(*ins)


def _odd_in_kernel(x_ref, g_ref, w_ref, qn_ref, kn_ref, bf_ref, gm512_ref, tri_ref,
                   qs_ref, kv_ref, kb_ref, lf_ref, c_ref, u_ref, qst_ref, vbt_ref, carry_ref):
    @pl.when(pl.program_id(1) == 0)
    def _():
        carry_ref[...] = jnp.zeros_like(carry_ref)

    xb = _rms_rows(x_ref[...], g_ref[...]).astype(BF16)

    def proj(lo, hi):
        return jnp.dot(xb, w_ref[:, lo:hi], preferred_element_type=F32)

    q = _group_rms(proj(0, 512), gm512_ref[...], HEAD_DIM) * qn_ref[...] * SCALE
    qs_ref[...] = q.astype(BF16)
    qst_ref[...] = q.T.astype(BF16)
    k = _group_rms(proj(512, 1024), gm512_ref[...], HEAD_DIM) * kn_ref[...]
    v = proj(1024, 1536)
    kv_ref[:, 0:512] = k
    kv_ref[:, 512:1024] = v
    kb_ref[...] = k.astype(BF16)
    vbt_ref[...] = v.T.astype(BF16)
    u_ref[...] = proj(1536, 2048) * _sigmoid(proj(2048, 2560))
    lf = _log_sigmoid(proj(2560, 2688) + bf_ref[...])
    lf_ref[...] = lf
    c = _dot_f32(tri_ref[...], lf) + carry_ref[0:1, :]
    c_ref[...] = c
    carry_ref[0:1, :] = c[-1:, :]


def _odd_in(x3d, g, w_pad, qn, kn, bf_pad, tm):
    b, t, _ = x3d.shape
    widths = (512, 1024, 512, 128, 128, 512)
    dtypes = (BF16, F32, BF16, F32, F32, F32)
    gm512 = _block_ones(512, HEAD_DIM)
    tri = jnp.asarray(np.tril(np.ones((tm, tm), np.float32)))
    ins = (x3d, g, w_pad, qn, kn, bf_pad, gm512, tri)
    full = lambda a: pl.BlockSpec(a.shape, lambda bi, i: (0,) * a.ndim)
    nt = t // tm
    t_spec = pl.BlockSpec((FOX_WIDTH, tm), lambda bi, i: (0, bi * nt + i))
    t_shape = jax.ShapeDtypeStruct((FOX_WIDTH, b * t), BF16)
    return pl.pallas_call(
        _odd_in_kernel,
        grid=(b, nt),
        in_specs=[pl.BlockSpec((None, tm, D_MODEL), lambda bi, i: (bi, i, 0))] + [full(a) for a in ins[1:]],
        out_specs=[pl.BlockSpec((None, tm, w), lambda bi, i: (bi, i, 0)) for w in widths] + [t_spec, t_spec],
        out_shape=[jax.ShapeDtypeStruct((b, t, w), d) for w, d in zip(widths, dtypes)] + [t_shape, t_shape],
        scratch_shapes=[pltpu.VMEM((8, 128), F32)],
        compiler_params=_cparams(("arbitrary", "arbitrary"), VMEM_LIMIT_BYTES),
        name="odd_in",
    )(*ins)


def _out_proj_kernel(res_ref, a1_ref, a2_ref, w1_ref, w2_ref, o_ref):
    acc = jnp.dot(a1_ref[...].astype(BF16), w1_ref[...], preferred_element_type=F32)
    acc = acc + jnp.dot(a2_ref[...].astype(BF16), w2_ref[...], preferred_element_type=F32)
    o_ref[...] = res_ref[...] + acc


def _out_proj(res, a1, a2, w1, w2, tm):
    m = res.shape[0]
    row = lambda a: pl.BlockSpec((tm, a.shape[1]), lambda i: (i, 0))
    full = lambda a: pl.BlockSpec(a.shape, lambda i: (0, 0))
    return pl.pallas_call(
        _out_proj_kernel,
        grid=(m // tm,),
        in_specs=[row(res), row(a1), row(a2), full(w1), full(w2)],
        out_specs=row(res),
        out_shape=jax.ShapeDtypeStruct(res.shape, F32),
        compiler_params=_cparams(("arbitrary",), VMEM_LIMIT_BYTES),
        name="out_proj",
    )(res, a1, a2, w1, w2)


def _ffn_kernel(x_ref, g_ref, wg_ref, wu_ref, wo_ref, o_ref, xn_ref, acc_ref):
    j = pl.program_id(1)

    @pl.when(j == 0)
    def _():
        xn_ref[...] = _rms_rows(x_ref[...], g_ref[...]).astype(BF16)
        acc_ref[...] = jnp.zeros_like(acc_ref)

    xb = xn_ref[...]
    gate = jnp.dot(xb, wg_ref[...], preferred_element_type=F32)
    up = jnp.dot(xb, wu_ref[...], preferred_element_type=F32)
    h = (gate * _sigmoid(gate) * up).astype(BF16)
    acc_ref[...] += jnp.dot(h, wo_ref[...], preferred_element_type=F32)

    @pl.when(j == pl.num_programs(1) - 1)
    def _():
        o_ref[...] = x_ref[...] + acc_ref[...]


def _ffn(x2d, g, w_in, w_out, tm, n_chunks=2):
    m = x2d.shape[0]
    th = FFN_HIDDEN // n_chunks
    return pl.pallas_call(
        _ffn_kernel,
        grid=(m // tm, n_chunks),
        in_specs=[pl.BlockSpec((tm, D_MODEL), lambda i, j: (i, 0)),
                  pl.BlockSpec((1, D_MODEL), lambda i, j: (0, 0)),
                  pl.BlockSpec((D_MODEL, th), lambda i, j: (0, j)),
                  pl.BlockSpec((D_MODEL, th), lambda i, j: (0, n_chunks + j)),
                  pl.BlockSpec((th, D_MODEL), lambda i, j: (j, 0))],
        out_specs=pl.BlockSpec((tm, D_MODEL), lambda i, j: (i, 0)),
        out_shape=jax.ShapeDtypeStruct(x2d.shape, F32),
        scratch_shapes=[pltpu.VMEM((tm, D_MODEL), BF16), pltpu.VMEM((tm, D_MODEL), F32)],
        compiler_params=_cparams(("arbitrary", "arbitrary"), VMEM_LIMIT_BYTES),
        name="ffn",
    )(x2d, g, w_in, w_in, w_out)


def _mem_kv_kernel(m_ref, g_ref, w_ref, kn_ref, gm_ref, o_ref):
    xb = _rms_rows(m_ref[...], g_ref[...]).astype(BF16)
    kv = jnp.dot(xb, w_ref[...], preferred_element_type=F32)
    o_ref[0:XA_WIDTH, :] = (_group_rms(kv[:, 0:XA_WIDTH], gm_ref[...], HEAD_DIM) * kn_ref[...]).T
    o_ref[XA_WIDTH:, :] = kv[:, XA_WIDTH:].T


def _mem_kv(mem, g, wkv, kn):
    b = mem.shape[0]
    gm = _block_ones(XA_WIDTH, HEAD_DIM)
    full = lambda a: pl.BlockSpec(a.shape, lambda i: (0,) * a.ndim)
    return pl.pallas_call(
        _mem_kv_kernel,
        grid=(b,),
        in_specs=[pl.BlockSpec((None, MEM_LEN, D_MODEL), lambda i: (i, 0, 0)), full(g), full(wkv), full(kn), full(gm)],
        out_specs=pl.BlockSpec((None, 2 * XA_WIDTH, MEM_LEN), lambda i: (i, 0, 0)),
        out_shape=jax.ShapeDtypeStruct((b, 2 * XA_WIDTH, MEM_LEN), F32),
        compiler_params=_cparams(("arbitrary",)),
        name="mem_kv",
    )(mem, g, wkv, kn, gm)


def _xattn_kernel(x_ref, mkv_ref, g_ref, wq_ref, wo_ref, qn_ref, gm_ref, o_ref):
    x = x_ref[...]
    xb = _rms_rows(x, g_ref[...]).astype(BF16)
    q = jnp.dot(xb, wq_ref[...], preferred_element_type=F32)
    q = _group_rms(q, gm_ref[...], HEAD_DIM) * qn_ref[...]
    qb = (q * SCALE).astype(BF16)
    outs = []
    for h in range(XA_HEADS):
        kt = mkv_ref[h * HEAD_DIM:(h + 1) * HEAD_DIM, :].astype(BF16)
        vt = mkv_ref[XA_WIDTH + h * HEAD_DIM:XA_WIDTH + (h + 1) * HEAD_DIM, :].astype(BF16)
        s = jnp.dot(qb[:, h * HEAD_DIM:(h + 1) * HEAD_DIM], kt, preferred_element_type=F32)
        e = jnp.exp(s - jnp.max(s, axis=-1, keepdims=True))
        p = (e / jnp.sum(e, axis=-1, keepdims=True)).astype(BF16)
        outs.append(_dot_nt(p, vt))
    o = jnp.concatenate(outs, axis=-1).astype(BF16)
    o_ref[...] = x + jnp.dot(o, wo_ref[...], preferred_element_type=F32)


def _xattn(x3d, mkv_t, layer, g, wq, wo, qn, tm):
    b, t, _ = x3d.shape
    gm = _block_ones(XA_WIDTH, HEAD_DIM)
    full = lambda a: pl.BlockSpec(a.shape, lambda bi, i: (0,) * a.ndim)
    return pl.pallas_call(
        _xattn_kernel,
        grid=(b, t // tm),
        in_specs=[pl.BlockSpec((None, tm, D_MODEL), lambda bi, i: (bi, i, 0)),
                  pl.BlockSpec((None, None, 2 * XA_WIDTH, MEM_LEN), lambda bi, i: (layer, bi, 0, 0)),
                  full(g), full(wq), full(wo), full(qn), full(gm)],
        out_specs=pl.BlockSpec((None, tm, D_MODEL), lambda bi, i: (bi, i, 0)),
        out_shape=jax.ShapeDtypeStruct(x3d.shape, F32),
        compiler_params=_cparams(("arbitrary", "arbitrary"), VMEM_LIMIT_BYTES),
        name="xattn",
    )(x3d, mkv_t, g, wq, wo, qn, gm)


def _gla_kernel(q_ref, k_ref, v_ref, la_ref, og_ref, gn_ref, s0_ref, tri_ref, hsel_ref,
                o_ref, sfin_ref, st_ref, *, n_sub, n_valid, bb):
    ti = pl.program_id(1)

    @pl.when(ti == 0)
    def _():
        st_ref[...] = s0_ref[...]

    tri = tri_ref[...]
    hsel = hsel_ref[...]
    gn = gn_ref[...]
    row = lax.broadcasted_iota(jnp.int32, (SUB, 1), 0)

    def sub_block(i, carry):
        for bi in range(bb):
            one_sequence(i, bi)
        return carry

    def one_sequence(i, bi):
        r0 = pl.multiple_of(i * SUB, SUB)
        rows = pl.ds(r0, SUB)
        q = q_ref[bi, rows, :]
        k = k_ref[bi, rows, :]
        v = v_ref[bi, rows, :]
        la = la_ref[bi, rows, :]
        if n_valid is not None:
            live = (row + r0) < n_valid
            la = jnp.where(live, la, 0.0)
            k = jnp.where(live, k, 0.0)
        b = _dot_f32(tri, la)
        b_end = b[SUB - 1:SUB, :]
        qd = (q * jnp.exp(b)).astype(BF16)
        kd = (k * jnp.exp(b_end - b)).astype(BF16)
        vb = v.astype(BF16)
        tiles = []
        for s in range(SUB):
            e = jnp.exp(jnp.minimum(b - b[s:s + 1, :], 0.0))
            z = (q * k[s:s + 1, :]) * e
            tiles.append(jnp.where(row >= s, z, 0.0))
        att = jnp.dot(jnp.concatenate(tiles, axis=0).astype(BF16), hsel, preferred_element_type=F32)
        dec = jnp.exp(b_end)
        outs = []
        for h in range(GLA_HEADS):
            dk = slice(h * GLA_DK, (h + 1) * GLA_DK)
            dv = slice(h * GLA_DV, (h + 1) * GLA_DV)
            st = st_ref[bi, h]
            o = _dot_nt(qd[:, dk], st.astype(BF16))
            for s in range(SUB):
                o = o + att[s * SUB:(s + 1) * SUB, h:h + 1] * v[s:s + 1, dv]
            st_ref[bi, h] = st * dec[:, dk] + _dot_tn(vb[:, dv], kd[:, dk])
            outs.append(_rms_rows(o, gn))
        o_ref[bi, rows, :] = jnp.concatenate(outs, axis=-1) * og_ref[bi, rows, :]

    lax.fori_loop(0, n_sub, sub_block, 0)

    @pl.when(ti == pl.num_programs(1) - 1)
    def _():
        sfin_ref[...] = st_ref[...]


def _gla(q, k, v, la, og, gn, s0t, tt, bb, n_valid=None):
    b, t, _ = q.shape
    tri = jnp.asarray(np.tril(np.ones((SUB, SUB), np.float32)))
    hsel = jnp.asarray((np.arange(256)[:, None] // GLA_DK == np.arange(128)[None, :]).astype(np.float32), dtype=BF16)
    seq = lambda w: pl.BlockSpec((bb, tt, w), lambda bi, i: (bi, i, 0))
    full = lambda a: pl.BlockSpec(a.shape, lambda bi, i: (0,) * a.ndim)
    st_spec = pl.BlockSpec((bb, GLA_HEADS, GLA_DV, GLA_DK), lambda bi, i: (bi, 0, 0, 0))
    return pl.pallas_call(
        functools.partial(_gla_kernel, n_sub=tt // SUB, n_valid=n_valid, bb=bb),
        grid=(b // bb, t // tt),
        in_specs=[seq(256), seq(256), seq(512), seq(256), seq(512), full(gn), st_spec, full(tri), full(hsel)],
        out_specs=[seq(512), st_spec],
        out_shape=[jax.ShapeDtypeStruct((b, t, GLA_WIDTH), F32),
                   jax.ShapeDtypeStruct((b, GLA_HEADS, GLA_DV, GLA_DK), F32)],
        scratch_shapes=[pltpu.VMEM((bb, GLA_HEADS, GLA_DV, GLA_DK), F32)],
        compiler_params=_cparams(("arbitrary", "arbitrary")),
        name="gla",
    )(q, k, v, la, og, gn, s0t, tri, hsel)


def _conv_kernel(u_ref, st0_ref, w_ref, b_ref, g_ref, beta_ref, o_ref, st_ref, ext_ref, *, tt, n_valid):
    ti = pl.program_id(1)
    ctx = CONV_WIDTH - 1

    @pl.when(ti == 0)
    def _():
        ext_ref[0:8, :] = jnp.zeros((8, CONV_CH), F32)
        ext_ref[pl.ds(2, ctx), :] = st0_ref[...]

    ext_ref[pl.ds(32, tt), :] = u_ref[...]
    acc = jnp.zeros((tt, CONV_CH), F32)
    for w in range(CONV_WIDTH):
        acc = acc + ext_ref[pl.ds(2 + w, tt), :] * w_ref[w:w + 1, :]
    y = acc + b_ref[...]
    mu = jnp.mean(y, axis=-1, keepdims=True)
    var = jnp.mean(jnp.square(y - mu), axis=-1, keepdims=True)
    ln = (y - mu) * lax.rsqrt(var + EPS) * g_ref[...] + beta_ref[...]
    o_ref[...] = ln * _sigmoid(ln)

    @pl.when(ti == pl.num_programs(1) - 1)
    def _():
        st_ref[...] = ext_ref[pl.ds(32 + n_valid - ctx, ctx), :]

    ext_ref[0:32, :] = ext_ref[pl.ds(tt, 32), :]


def _conv(u, st0, w, b, g, beta, tt, n_valid):
    bsz, t, _ = u.shape
    ctx = CONV_WIDTH - 1
    full = lambda a: pl.BlockSpec(a.shape, lambda bi, i: (0,) * a.ndim)
    st_spec = pl.BlockSpec((None, ctx, CONV_CH), lambda bi, i: (bi, 0, 0))
    return pl.pallas_call(
        functools.partial(_conv_kernel, tt=tt, n_valid=n_valid),
        grid=(bsz, t // tt),
        in_specs=[pl.BlockSpec((None, tt, CONV_CH), lambda bi, i: (bi, i, 0)), st_spec,
                  full(w), full(b), full(g), full(beta)],
        out_specs=[pl.BlockSpec((None, tt, CONV_CH), lambda bi, i: (bi, i, 0)), st_spec],
        out_shape=[jax.ShapeDtypeStruct(u.shape, F32), jax.ShapeDtypeStruct((bsz, ctx, CONV_CH), F32)],
        scratch_shapes=[pltpu.VMEM((32 + max(tt, 32), CONV_CH), F32)],
        compiler_params=_cparams(("arbitrary", "arbitrary")),
        name="conv",
    )(u, st0, w, b, g, beta)


def _fox_kernel(qt_ref, k_ref, vt_ref, ck_ref, cqt_ref, o_ref, m_ref, l_ref, acc_ref, *, tq):
    qi = pl.program_id(1)
    ki = pl.program_id(2)

    @pl.when(ki == 0)
    def _():
        m_ref[...] = jnp.full_like(m_ref, NEG)
        l_ref[...] = jnp.zeros_like(l_ref)
        acc_ref[...] = jnp.zeros_like(acc_ref)

    def tile(neg):
        for h in range(FOX_HEADS):
            cols = slice(h * HEAD_DIM, (h + 1) * HEAD_DIM)
            s = jnp.dot(k_ref[:, cols], qt_ref[cols, :], preferred_element_type=F32)
            s = s + cqt_ref[h:h + 1, :] - ck_ref[:, h:h + 1]
            if neg is not None:
                s = s + neg
            m_old = m_ref[h:h + 1, :]
            m_new = jnp.maximum(m_old, jnp.max(s, axis=0, keepdims=True))
            p = jnp.exp(s - m_new)
            alpha = jnp.exp(m_old - m_new)
            l_ref[h:h + 1, :] = alpha * l_ref[h:h + 1, :] + jnp.sum(p, axis=0, keepdims=True)
            acc_ref[cols, :] = alpha * acc_ref[cols, :] + jnp.dot(vt_ref[cols, :], p.astype(BF16),
                                                                  preferred_element_type=F32)
            m_ref[h:h + 1, :] = m_new

    @pl.when(ki < qi)
    def _():
        tile(None)

    @pl.when(ki == qi)
    def _():
        s_rel = lax.broadcasted_iota(jnp.int32, (tq, tq), 0)
        t_rel = lax.broadcasted_iota(jnp.int32, (tq, tq), 1)
        tile(jnp.where(s_rel <= t_rel, 0.0, NEG))

    @pl.when(ki == pl.num_programs(2) - 1)
    def _():
        for h in range(FOX_HEADS):
            cols = slice(h * HEAD_DIM, (h + 1) * HEAD_DIM)
            acc_ref[cols, :] = acc_ref[cols, :] / jnp.maximum(l_ref[h:h + 1, :], 1e-30)
        o_ref[...] = acc_ref[...].T


def _fox_prompt(qst, kb, vbt, c8, ct, tq):
    b, t, _ = kb.shape
    nq = t // tq
    kmin = lambda qi, ki: jnp.minimum(ki, qi)
    return pl.pallas_call(
        functools.partial(_fox_kernel, tq=tq),
        grid=(b, nq, nq),
        in_specs=[pl.BlockSpec((FOX_WIDTH, tq), lambda bi, qi, ki: (0, bi * nq + qi)),
                  pl.BlockSpec((None, tq, FOX_WIDTH), lambda bi, qi, ki: (bi, kmin(qi, ki), 0)),
                  pl.BlockSpec((FOX_WIDTH, tq), lambda bi, qi, ki: (0, bi * nq + kmin(qi, ki))),
                  pl.BlockSpec((None, tq, FOX_HEADS), lambda bi, qi, ki: (bi, kmin(qi, ki), 0)),
                  pl.BlockSpec((None, FOX_HEADS, tq), lambda bi, qi, ki: (bi, 0, qi))],
        out_specs=pl.BlockSpec((None, tq, FOX_WIDTH), lambda bi, qi, ki: (bi, qi, 0)),
        out_shape=jax.ShapeDtypeStruct((b, t, FOX_WIDTH), F32),
        scratch_shapes=[pltpu.VMEM((FOX_HEADS, tq), F32), pltpu.VMEM((FOX_HEADS, tq), F32),
                        pltpu.VMEM((FOX_WIDTH, tq), F32)],
        compiler_params=_cparams(("arbitrary", "arbitrary", "arbitrary"), VMEM_LIMIT_BYTES),
        name="fox_prompt",
    )(qst, kb, vbt, c8, ct)


def _compress_weights(w1):
    w = w1.reshape(2, 2, 8, 2, HEAD_DIM, CMP_HIDDEN)
    z = jnp.zeros_like(w)
    g0 = jnp.concatenate([w, z], axis=-1)
    g1 = jnp.concatenate([z, w], axis=-1)
    wbd = jnp.stack([g0, g1], axis=4)
    return wbd.reshape(2, 2, 8, 4 * HEAD_DIM, 2 * CMP_HIDDEN).astype(BF16)


def _compress_compute(src_refs, pe_ref, w1_ref, w2_ref, kn_ref, ck_ref, cv_ref, sh_ref, nseg):
    sh_ref[pl.ds(nseg, 8), :] = jnp.zeros((8, CMP_HIDDEN), F32)
    for j, src_ref in enumerate(src_refs):
        a = jnp.zeros((nseg, 2 * CMP_HIDDEN), F32)
        bm = jnp.zeros((nseg, 2 * CMP_HIDDEN), F32)
        for q in range(CMP_STRIDE // 2):
            p0, p1 = 2 * q, 2 * q + 1
            x0 = src_ref[pl.ds(p0, nseg, stride=CMP_STRIDE), :]
            x1 = src_ref[pl.ds(p1, nseg, stride=CMP_STRIDE), :]
            xa = jnp.concatenate([x0 + pe_ref[j, p0:p0 + 1, :], x1 + pe_ref[j, p1:p1 + 1, :]], axis=1)
            a = a + jnp.dot(xa.astype(BF16), w1_ref[j, 0, q], preferred_element_type=F32)
            p0, p1 = p0 + CMP_STRIDE, p1 + CMP_STRIDE
            xb = jnp.concatenate([x0 + pe_ref[j, p0:p0 + 1, :], x1 + pe_ref[j, p1:p1 + 1, :]], axis=1)
            bm = bm + jnp.dot(xb.astype(BF16), w1_ref[j, 1, q], preferred_element_type=F32)
        for g in range(NSA_KV_HEADS):
            sh_ref[pl.ds(0, nseg), :] = bm[:, g * CMP_HIDDEN:(g + 1) * CMP_HIDDEN]
            x = a[:, g * CMP_HIDDEN:(g + 1) * CMP_HIDDEN] + sh_ref[pl.ds(1, nseg), :]
            hid = x * (0.5 * (1.0 + jnp.tanh(math.sqrt(2.0 / math.pi) * (x + 0.044715 * (x * x * x)))))
            ckv = jnp.dot(hid.astype(BF16), w2_ref[j], preferred_element_type=F32)
            if j == 0:
                ck_ref[g] = _rms_rows(ckv, kn_ref[...]).astype(BF16)
            else:
                cv_ref[g] = ckv.astype(BF16)


def _compress_prompt_kernel(xk_ref, xv_ref, pe_ref, w1_ref, w2_ref, kn_ref, ck_ref, cv_ref, sh_ref, *, nseg):
    _compress_compute((xk_ref, xv_ref), pe_ref, w1_ref, w2_ref, kn_ref, ck_ref, cv_ref, sh_ref, nseg)


def _compress_prompt(kvr, pe, w1, w2, kn):
    b, t, _ = kvr.shape
    nseg = t // CMP_STRIDE
    full = lambda a: pl.BlockSpec(a.shape, lambda i: (0,) * a.ndim)
    o_spec = pl.BlockSpec((None, NSA_KV_HEADS, nseg, HEAD_DIM), lambda i: (i, 0, 0, 0))
    o_shape = jax.ShapeDtypeStruct((b, NSA_KV_HEADS, nseg, HEAD_DIM), BF16)
    return pl.pallas_call(
        functools.partial(_compress_prompt_kernel, nseg=nseg),
        grid=(b,),
        in_specs=[pl.BlockSpec((None, t, LANES), lambda i: (i, 0, 0)), pl.BlockSpec((None, t, LANES), lambda i: (i, 0, 1)),
                  full(pe), full(w1), full(w2), full(kn)],
        out_specs=[o_spec, o_spec],
        out_shape=[o_shape, o_shape],
        scratch_shapes=[pltpu.VMEM((nseg + 8, CMP_HIDDEN), F32)],
        compiler_params=_cparams(("arbitrary",), VMEM_LIMIT_BYTES),
        name="nsa_compress",
    )(kvr, kvr, pe, w1, w2, kn)


def _stack_heads(qs_ref, kh):
    parts = [qs_ref[:, (kh * NSA_GROUP + g) * HEAD_DIM:(kh * NSA_GROUP + g + 1) * HEAD_DIM].astype(F32)
             for g in range(NSA_GROUP)]
    return jnp.concatenate(parts, axis=0).astype(BF16)


def _cmp_topk_kernel(qs_ref, ck_ref, cv_ref, farcol_ref, chi_ref, clo_ref, pool_ref, ocmp_ref, msk_ref,
                     *, tq, nseg, q_base, n_pick, n_blk, mask_t):
    qi = pl.program_id(1)
    G = NSA_GROUP
    q0 = q_base + qi * tq
    nbase = q0 // CMP_STRIDE - 16
    place = (lax.broadcasted_iota(jnp.int32, (32, nseg), 1) - lax.broadcasted_iota(jnp.int32, (32, nseg), 0)) == nbase
    place = jnp.where(place, 1.0, 0.0).astype(BF16)
    t1 = q0 + lax.broadcasted_iota(jnp.int32, (tq, 1), 0)
    t4 = jnp.concatenate([t1] * G, axis=0)
    n_i = lax.broadcasted_iota(jnp.int32, (G * tq, nseg), 1)
    valid = (n_i * CMP_STRIDE + (CMP_BLOCK - 1) <= t4) & (n_i <= nseg - 2)
    blk = lax.broadcasted_iota(jnp.int32, (tq, N_SELBLK), 1)
    cur = lax.shift_right_logical(t1, 6)
    forced = (blk == 0) | (blk == cur) | (blk == cur - 1)
    for kh in range(NSA_KV_HEADS):
        q4 = _stack_heads(qs_ref, kh)
        s = _dot_nt(q4, ck_ref[kh]) + farcol_ref[kh]
        s = s + jnp.dot(chi_ref[kh], place, preferred_element_type=F32) + jnp.dot(clo_ref[kh], place, preferred_element_type=F32)
        s = jnp.where(valid, s, NEG)
        e = jnp.where(valid, jnp.exp(s - jnp.max(s, axis=-1, keepdims=True)), 0.0)
        p = e / jnp.maximum(jnp.sum(e, axis=-1, keepdims=True), 1e-30)
        o = jnp.dot(p.astype(BF16), cv_ref[kh], preferred_element_type=F32)
        for g in range(G):
            h = kh * G + g
            ocmp_ref[:, h * HEAD_DIM:(h + 1) * HEAD_DIM] = o[g * tq:(g + 1) * tq]
        imp = p[0:tq] + p[tq:2 * tq] + p[2 * tq:3 * tq] + p[3 * tq:4 * tq]
        pooled = _dot_f32(imp, pool_ref[...])
        score = jnp.where((blk > cur) | (blk >= n_blk), -1e30, jnp.where(forced, 1e30, pooled))
        sel = jnp.zeros((tq, N_SELBLK), F32)
        for _ in range(n_pick):
            mx = jnp.max(score, axis=-1, keepdims=True)
            first = jnp.min(jnp.where(score == mx, blk, N_SELBLK), axis=-1, keepdims=True)
            pick = blk == first
            sel = jnp.where(pick, 1.0, sel)
            score = jnp.where(pick, -3e38, score)
        msk_ref[kh] = sel.T if mask_t else sel


def _cmp_topk_t_kernel(qt_ref, ck_ref, cv_ref, farrow_ref, chit_ref, clot_ref, poolt_ref, ocmpt_ref, mskt_ref,
                       *, tq, nseg, n_pick, n_blk):
    qi = pl.program_id(1)
    G = NSA_GROUP
    q0 = qi * tq
    nbase = q0 // CMP_STRIDE - 16
    place_t = (lax.broadcasted_iota(jnp.int32, (nseg, 32), 0) - lax.broadcasted_iota(jnp.int32, (nseg, 32), 1)) == nbase
    place_t = jnp.where(place_t, 1.0, 0.0).astype(BF16)
    t1 = q0 + lax.broadcasted_iota(jnp.int32, (1, tq), 1)
    n_i = lax.broadcasted_iota(jnp.int32, (nseg, tq), 0)
    valid1 = (n_i * CMP_STRIDE + (CMP_BLOCK - 1) <= t1) & (n_i <= nseg - 2)
    valid = jnp.concatenate([valid1] * G, axis=1)
    blk = lax.broadcasted_iota(jnp.int32, (N_SELBLK, tq), 0)
    cur = lax.shift_right_logical(t1, 6)
    forced = (blk == 0) | (blk == cur) | (blk == cur - 1)
    for kh in range(NSA_KV_HEADS):
        qt4 = jnp.concatenate([qt_ref[(kh * G + g) * HEAD_DIM:(kh * G + g + 1) * HEAD_DIM, :] for g in range(G)], axis=1)
        s = jnp.dot(ck_ref[kh], qt4, preferred_element_type=F32) + farrow_ref[kh]
        s = s + jnp.dot(place_t, chit_ref[kh], preferred_element_type=F32) + jnp.dot(place_t, clot_ref[kh], preferred_element_type=F32)
        s = jnp.where(valid, s, NEG)
        e = jnp.where(valid, jnp.exp(s - jnp.max(s, axis=0, keepdims=True)), 0.0)
        p = e * (1.0 / jnp.maximum(jnp.sum(e, axis=0, keepdims=True), 1e-30))
        o_t = _dot_tn(cv_ref[kh], p.astype(BF16))
        for g in range(G):
            h = kh * G + g
            ocmpt_ref[h * HEAD_DIM:(h + 1) * HEAD_DIM, :] = o_t[:, g * tq:(g + 1) * tq]
        imp = p[:, 0:tq] + p[:, tq:2 * tq] + p[:, 2 * tq:3 * tq] + p[:, 3 * tq:4 * tq]
        pooled = _dot_f32(poolt_ref[...], imp)
        score = jnp.where((blk > cur) | (blk >= n_blk), -1e30, jnp.where(forced, 1e30, pooled))
        sel = jnp.zeros((N_SELBLK, tq), F32)
        for _ in range(n_pick):
            mx = jnp.max(score, axis=0, keepdims=True)
            first = jnp.min(jnp.where(score == mx, blk, N_SELBLK), axis=0, keepdims=True)
            pick = blk == first
            sel = jnp.where(pick, 1.0, sel)
            score = jnp.where(pick, -3e38, score)
        mskt_ref[kh] = sel


def _cmp_topk_prompt(nqst, ck, cv, tbl, b, t, tq, n_pick):
    nseg = ck.shape[2]
    assert tq <= 256 and ck.shape[0] == b
    farcol, chi, clo = _cmp_bias_tables(tbl, tq)
    farrow, chit, clot = (jnp.swapaxes(a, 1, 2) for a in (farcol, chi, clo))
    poolt = jnp.asarray((np.arange(N_SELBLK)[:, None] == np.arange(nseg)[None, :] // SEL_RATIO).astype(np.float32))
    full = lambda a: pl.BlockSpec(a.shape, lambda bi, i: (0,) * a.ndim)
    c_spec = pl.BlockSpec((None, NSA_KV_HEADS, nseg, HEAD_DIM), lambda bi, i: (bi, 0, 0, 0))
    nq = t // tq
    col_tile = pl.BlockSpec((NSA_WIDTH, tq), lambda bi, i: (0, bi * nq + i))
    return pl.pallas_call(
        functools.partial(_cmp_topk_t_kernel, tq=tq, nseg=nseg, n_pick=n_pick, n_blk=t // SEL_BLOCK),
        grid=(b, nq),
        in_specs=[col_tile, c_spec, c_spec, full(farrow), full(chit), full(clot), full(poolt)],
        out_specs=[col_tile, pl.BlockSpec((None, NSA_KV_HEADS, N_SELBLK, tq), lambda bi, i: (bi, 0, 0, i))],
        out_shape=[jax.ShapeDtypeStruct((NSA_WIDTH, b * t), F32),
                   jax.ShapeDtypeStruct((b, NSA_KV_HEADS, N_SELBLK, t), F32)],
        compiler_params=_cparams(("arbitrary", "arbitrary"), VMEM_LIMIT_BYTES),
        name="nsa_cmp_topk",
    )(nqst, ck, cv, farrow, chit, clot, poolt)


def _rel_bucket(dist):
    exact = NUM_BUCKETS // 2
    d = jnp.maximum(dist, 0)
    log_ratio = jnp.log(jnp.maximum(d, 1).astype(jnp.float32) / exact) / math.log(MAX_DISTANCE / exact)
    large = jnp.minimum(exact + (log_ratio * (NUM_BUCKETS - exact)).astype(jnp.int32), NUM_BUCKETS - 1)
    return jnp.where(d < exact, d, large)


def _bias_lookup(tbl, dist):
    onehot = _rel_bucket(dist)[..., None] == jnp.arange(NUM_BUCKETS)
    t = tbl.reshape((tbl.shape[0],) + (1,) * dist.ndim + (NUM_BUCKETS,))
    return jnp.sum(jnp.where(onehot[None], t, 0.0), axis=-1)


def _cmp_bias_tables(tbl, tq):
    tr = jnp.arange(tq)[:, None]
    i = jnp.arange(32)[None, :]
    dist = tr + 16 * CMP_STRIDE - CMP_STRIDE * i - (CMP_BLOCK - 1)
    near = _bias_lookup(tbl, dist)
    far = tbl[:, NUM_BUCKETS - 1]
    corr = (near - far[:, None, None]).reshape(NSA_KV_HEADS, NSA_GROUP * tq, 32)
    hi = corr.astype(BF16)
    lo = (corr - hi.astype(F32)).astype(BF16)
    farcol = jnp.broadcast_to(far[:, None, None], (NSA_HEADS, tq, 1)).reshape(NSA_KV_HEADS, NSA_GROUP * tq, 1)
    return farcol, hi, lo


def _cmp_topk(qs, ck, cv, tbl, tq, q_base, n_pick, n_blk, mask_t):
    b, t, _ = qs.shape
    nseg = ck.shape[2]
    assert tq <= 256
    farcol, chi, clo = _cmp_bias_tables(tbl, tq)
    pool = jnp.asarray((np.arange(nseg)[:, None] // SEL_RATIO == np.arange(N_SELBLK)[None, :]).astype(np.float32))
    full = lambda a: pl.BlockSpec(a.shape, lambda bi, i: (0,) * a.ndim)
    c_spec = pl.BlockSpec((None, NSA_KV_HEADS, nseg, HEAD_DIM), lambda bi, i: (bi, 0, 0, 0))
    if mask_t:
        m_spec = pl.BlockSpec((None, NSA_KV_HEADS, N_SELBLK, tq), lambda bi, i: (bi, 0, 0, i))
        m_shape = (b, NSA_KV_HEADS, N_SELBLK, t)
    else:
        m_spec = pl.BlockSpec((None, NSA_KV_HEADS, tq, N_SELBLK), lambda bi, i: (bi, 0, i, 0))
        m_shape = (b, NSA_KV_HEADS, t, N_SELBLK)
    return pl.pallas_call(
        functools.partial(_cmp_topk_kernel, tq=tq, nseg=nseg, q_base=q_base, n_pick=n_pick, n_blk=n_blk,
                          mask_t=mask_t),
        grid=(b, t // tq),
        in_specs=[pl.BlockSpec((None, tq, NSA_WIDTH), lambda bi, i: (bi, i, 0)), c_spec, c_spec,
                  full(farcol), full(chi), full(clo), full(pool)],
        out_specs=[pl.BlockSpec((None, tq, NSA_WIDTH), lambda bi, i: (bi, i, 0)), m_spec],
        out_shape=[jax.ShapeDtypeStruct((b, t, NSA_WIDTH), F32), jax.ShapeDtypeStruct(m_shape, F32)],
        compiler_params=_cparams(("arbitrary", "arbitrary"), VMEM_LIMIT_BYTES),
        name="nsa_cmp_topk",
    )(qs, ck, cv, farcol, chi, clo, pool)


def _online_update(s, valid, v, m_ref, l_ref, acc_ref, v_t=False):
    s = jnp.where(valid, s, NEG)
    m_old = m_ref[...]
    m_new = jnp.maximum(m_old, jnp.max(s, axis=-1, keepdims=True))
    p = jnp.where(valid, jnp.exp(s - m_new), 0.0)
    alpha = jnp.exp(m_old - m_new)
    l_ref[...] = alpha * l_ref[...] + jnp.sum(p, axis=-1, keepdims=True)
    pb = p.astype(BF16)
    pv = _dot_nt(pb, v) if v_t else jnp.dot(pb, v, preferred_element_type=F32)
    acc_ref[...] = alpha * acc_ref[...] + pv
    m_ref[...] = m_new


def _selwin_kernel(qt_ref, selk_ref, selvt_ref, wink_ref, winvt_ref, mskt_ref, bias_ref, ocmpt_ref, gtt_ref, o_ref,
                   m_ref, l_ref, acc_ref, ot_ref, qt4_ref, *, tq):
    qi = pl.program_id(1)
    G = NSA_GROUP
    blocks_per_tile = tq // SEL_BLOCK
    s_rel = lax.broadcasted_iota(jnp.int32, (tq, tq), 0)
    t_rel = lax.broadcasted_iota(jnp.int32, (tq, tq), 1)
    causal = s_rel <= t_rel

    def init():
        m_ref[...] = jnp.full_like(m_ref, NEG)
        l_ref[...] = jnp.zeros_like(l_ref)
        acc_ref[...] = jnp.zeros_like(acc_ref)

    def update(kh, k, vt, bias_idx, valid):
        s = jnp.dot(k, qt4_ref[...], preferred_element_type=F32) + bias_ref[kh, bias_idx]
        if valid is not None:
            neg = jnp.where(valid, 0.0, NEG)
            s = s + jnp.concatenate([neg] * G, axis=1)
        m_old = m_ref[0:1, :]
        m_new = jnp.maximum(m_old, jnp.max(s, axis=0, keepdims=True))
        p = jnp.exp(s - m_new)
        alpha = jnp.exp(m_old - m_new)
        l_ref[0:1, :] = alpha * l_ref[0:1, :] + jnp.sum(p, axis=0, keepdims=True)
        acc_ref[...] = alpha * acc_ref[...] + jnp.dot(vt, p.astype(BF16), preferred_element_type=F32)
        m_ref[0:1, :] = m_new

    def finish(kh, gate_row):
        for g in range(G):
            h = kh * G + g
            cols = slice(g * tq, (g + 1) * tq)
            rows = slice(h * HEAD_DIM, (h + 1) * HEAD_DIM)
            o = acc_ref[:, cols] / jnp.maximum(l_ref[0:1, cols], 1e-30)
            r = 3 * h + gate_row
            ot_ref[rows, :] += gtt_ref[r:r + 1, :] * o

    for h in range(NSA_HEADS):
        rows = slice(h * HEAD_DIM, (h + 1) * HEAD_DIM)
        ot_ref[rows, :] = gtt_ref[3 * h:3 * h + 1, :] * ocmpt_ref[rows, :]

    for kh in range(NSA_KV_HEADS):
        kcols = slice(kh * HEAD_DIM, (kh + 1) * HEAD_DIM)
        vrows = slice(kh * HEAD_DIM, (kh + 1) * HEAD_DIM)
        for g in range(G):
            h = kh * G + g
            qt4_ref[:, g * tq:(g + 1) * tq] = qt_ref[h * HEAD_DIM:(h + 1) * HEAD_DIM, :]

        def sel_valid(j):
            parts = [jnp.broadcast_to(mskt_ref[kh, pl.ds(j * blocks_per_tile + i, 1), :], (SEL_BLOCK, tq))
                     for i in range(blocks_per_tile)]
            return jnp.concatenate(parts, axis=0) > 0.5

        def sel_tile(j, bias_idx, extra):
            start = pl.multiple_of(j * tq, tq)
            valid = sel_valid(j)
            if extra is not None:
                valid = valid & extra
            update(kh, selk_ref[pl.ds(start, tq), kcols], selvt_ref[vrows, pl.ds(start, tq)], bias_idx, valid)

        def win_tile(j, bias_idx, valid):
            start = pl.multiple_of(j * tq, tq)
            update(kh, wink_ref[pl.ds(start, tq), kcols], winvt_ref[vrows, pl.ds(start, tq)], bias_idx, valid)

        init()

        def far_body(j, c):
            sel_tile(j, 2, None)
            return c

        lax.fori_loop(0, jnp.maximum(qi - 1, 0), far_body, 0)

        @pl.when(qi >= 1)
        def _():
            sel_tile(qi - 1, 1, None)

        sel_tile(qi, 0, causal)
        finish(kh, 1)

        init()

        @pl.when(qi >= 2)
        def _():
            win_tile(qi - 2, 2, s_rel >= t_rel)

        @pl.when(qi >= 1)
        def _():
            win_tile(qi - 1, 1, None)

        win_tile(qi, 0, causal)
        finish(kh, 2)

    o_ref[...] = ot_ref[...].T


def _selwin_bias_tables(tbl, tq):
    sr = jnp.arange(tq)[:, None]
    tr = jnp.arange(tq)[None, :]
    near0 = _bias_lookup(tbl, tr - sr)
    near1 = _bias_lookup(tbl, tr - sr + tq)
    far = jnp.broadcast_to(tbl[:, NUM_BUCKETS - 1][:, None, None], near0.shape)
    b = jnp.stack([near0, near1, far], axis=1)
    b = b.reshape(NSA_KV_HEADS, NSA_GROUP, 3, tq, tq).transpose(0, 2, 3, 1, 4)
    return b.reshape(NSA_KV_HEADS, 3, tq, NSA_GROUP * tq)


def _selwin_prompt(nqst, selkv, selvt, winkv, winvt, mskt, tbl, ocmp, gtt, b, t, tq):
    assert tq >= NSA_WINDOW // 2 and tq >= MAX_DISTANCE and tq % SEL_BLOCK == 0
    bias = _selwin_bias_tables(tbl, tq)
    nq = t // tq
    col_tile = lambda h: pl.BlockSpec((h, tq), lambda bi, i: (0, bi * nq + i))
    row_tile = lambda w: pl.BlockSpec((tq, w), lambda bi, i: (bi * nq + i, 0))
    return pl.pallas_call(
        functools.partial(_selwin_kernel, tq=tq),
        grid=(b, nq),
        in_specs=[col_tile(NSA_WIDTH),
                  pl.BlockSpec((t, LANES), lambda bi, i: (bi, 0)), pl.BlockSpec((LANES, t), lambda bi, i: (0, bi)),
                  pl.BlockSpec((t, LANES), lambda bi, i: (bi, 0)), pl.BlockSpec((LANES, t), lambda bi, i: (0, bi)),
                  pl.BlockSpec((None, NSA_KV_HEADS, N_SELBLK, tq), lambda bi, i: (bi, 0, 0, i)),
                  pl.BlockSpec(bias.shape, lambda bi, i: (0, 0, 0, 0)),
                  col_tile(NSA_WIDTH), col_tile(32)],
        out_specs=row_tile(NSA_WIDTH),
        out_shape=jax.ShapeDtypeStruct((b * t, NSA_WIDTH), F32),
        scratch_shapes=[pltpu.VMEM((8, NSA_GROUP * tq), F32), pltpu.VMEM((8, NSA_GROUP * tq), F32),
                        pltpu.VMEM((HEAD_DIM, NSA_GROUP * tq), F32), pltpu.VMEM((NSA_WIDTH, tq), F32),
                        pltpu.VMEM((HEAD_DIM, NSA_GROUP * tq), BF16)],
        compiler_params=_cparams(("arbitrary", "arbitrary"), VMEM_LIMIT_BYTES),
        name="nsa_selwin",
    )(nqst, selkv, selvt, winkv, winvt, mskt, bias, ocmp, gtt)


PAGES_PER_STEP = 8
PAGE = 128


def _feature_major(cache):
    nd = cache.ndim
    t = jnp.transpose(cache, (0, 1) + tuple(range(3, nd)) + (2,))
    return t.reshape(cache.shape[0], cache.shape[1], -1, cache.shape[2])


def _page_specs(layer, rows, row_block):
    return [pl.BlockSpec((None, None, rows, PAGE), functools.partial(
        lambda bi, j, pt, r: (layer, pt[bi, j * PAGES_PER_STEP + r], row_block, 0), r=r)) for r in range(PAGES_PER_STEP)]


def _compress_sample_kernel(pt_ref, *refs, nseg):
    pages = refs[:PAGES_PER_STEP]
    pe_ref, w1_ref, w2_ref, kn_ref, ck_ref, cv_ref, srck_ref, srcv_ref, sh_ref = refs[PAGES_PER_STEP:]
    j = pl.program_id(1)
    for r, p_ref in enumerate(pages):
        rows = pl.ds(pl.multiple_of((j * PAGES_PER_STEP + r) * PAGE, PAGE), PAGE)
        srck_ref[rows, :] = p_ref[0:LANES, :].T
        srcv_ref[rows, :] = p_ref[LANES:2 * LANES, :].T

    @pl.when(j == pl.num_programs(1) - 1)
    def _():
        _compress_compute((srck_ref, srcv_ref), pe_ref, w1_ref, w2_ref, kn_ref, ck_ref, cv_ref, sh_ref, nseg)


def _compress_sample(cache_t, layer, page_table, pe, w1, w2, kn):
    sb, n_pages = page_table.shape
    past = n_pages * PAGE
    nseg = past // CMP_STRIDE
    full = lambda a: pl.BlockSpec(a.shape, lambda bi, j, pt: (0,) * a.ndim)
    o_spec = pl.BlockSpec((None, NSA_KV_HEADS, nseg, HEAD_DIM), lambda bi, j, pt: (bi, 0, 0, 0))
    o_shape = jax.ShapeDtypeStruct((sb, NSA_KV_HEADS, nseg, HEAD_DIM), BF16)
    return pl.pallas_call(
        functools.partial(_compress_sample_kernel, nseg=nseg),
        grid_spec=pltpu.PrefetchScalarGridSpec(
            num_scalar_prefetch=1, grid=(sb, n_pages // PAGES_PER_STEP),
            in_specs=_page_specs(layer, 256, 0) + [full(pe), full(w1), full(w2), full(kn)],
            out_specs=[o_spec, o_spec],
            scratch_shapes=[pltpu.VMEM((past, LANES), F32), pltpu.VMEM((past, LANES), F32),
                            pltpu.VMEM((nseg + 8, CMP_HIDDEN), F32)]),
        out_shape=[o_shape, o_shape],
        compiler_params=_cparams(("arbitrary", "arbitrary"), VMEM_LIMIT_BYTES),
        name="nsa_compress_sample",
    )(page_table, *([cache_t] * PAGES_PER_STEP), pe, w1, w2, kn)


def _selwin_sample_kernel(pt_ref, qs_ref, *refs):
    pages = refs[:PAGES_PER_STEP]
    (msk_ref, newkv_ref, winst_ref, newwr_ref, bsel_ref, bnew_ref, bwin_ref, expand_ref, ocmp_ref, gt_ref, o_ref,
     kv_ref, m_ref, l_ref, acc_ref, osel_ref, owin_ref) = refs[PAGES_PER_STEP:]
    j = pl.program_id(1)
    G = NSA_GROUP
    R = SAMPLE_ROWS
    for r, p_ref in enumerate(pages):
        kv_ref[:, pl.ds(pl.multiple_of((j * PAGES_PER_STEP + r) * PAGE, PAGE), PAGE)] = p_ref[...].astype(BF16)

    @pl.when(j == pl.num_programs(1) - 1)
    def _():
        rq = lax.broadcasted_iota(jnp.int32, (G * R, 1), 0) & (R - 1)
        new_valid = lax.broadcasted_iota(jnp.int32, (G * R, R), 1) <= rq
        win_valid = lax.broadcasted_iota(jnp.int32, (G * R, NSA_WINDOW), 1) >= rq

        def init():
            m_ref[...] = jnp.full_like(m_ref, NEG)
            l_ref[...] = jnp.zeros_like(l_ref)
            acc_ref[...] = jnp.zeros_like(acc_ref)

        def finish(dst_ref, kh):
            o = acc_ref[...] / jnp.maximum(l_ref[...], 1e-30)
            for g in range(G):
                h = kh * G + g
                dst_ref[:, h * HEAD_DIM:(h + 1) * HEAD_DIM] = o[g * R:(g + 1) * R]

        for kh in range(NSA_KV_HEADS):
            kcols = slice(kh * HEAD_DIM, (kh + 1) * HEAD_DIM)
            vcols = slice(128 + kh * HEAD_DIM, 128 + (kh + 1) * HEAD_DIM)
            q4 = _stack_heads(qs_ref, kh)
            mskb = msk_ref[kh].astype(BF16)

            init()
            mt = jnp.dot(mskb, expand_ref[...], preferred_element_type=F32) > 0.5
            valid = jnp.concatenate([mt] * G, axis=0)
            s = jnp.dot(q4, kv_ref[kcols, :], preferred_element_type=F32) + bsel_ref[kh]
            _online_update(s, valid, kv_ref[vcols, :], m_ref, l_ref, acc_ref, v_t=True)
            knew = newkv_ref[:, 256 + kh * HEAD_DIM:256 + (kh + 1) * HEAD_DIM].astype(BF16)
            vnew = newkv_ref[:, 384 + kh * HEAD_DIM:384 + (kh + 1) * HEAD_DIM].astype(BF16)
            _online_update(_dot_nt(q4, knew) + bnew_ref[kh], new_valid, vnew, m_ref, l_ref, acc_ref)
            finish(osel_ref, kh)

            init()
            kwin = winst_ref[kcols, :].astype(BF16)
            vwin = winst_ref[vcols, :].astype(BF16)
            _online_update(jnp.dot(q4, kwin, preferred_element_type=F32) + bwin_ref[kh], win_valid, vwin,
                           m_ref, l_ref, acc_ref, v_t=True)
            knew = newwr_ref[:, kcols].astype(BF16)
            vnew = newwr_ref[:, vcols].astype(BF16)
            _online_update(_dot_nt(q4, knew) + bnew_ref[kh], new_valid, vnew, m_ref, l_ref, acc_ref)
            finish(owin_ref, kh)

        for h in range(NSA_HEADS):
            cols = slice(h * HEAD_DIM, (h + 1) * HEAD_DIM)
            o_ref[:, cols] = (gt_ref[:, 3 * h:3 * h + 1] * ocmp_ref[:, cols]
                              + gt_ref[:, 3 * h + 1:3 * h + 2] * osel_ref[:, cols]
                              + gt_ref[:, 3 * h + 2:3 * h + 3] * owin_ref[:, cols])


def _sample_bias_tables(tbl, past):
    R = SAMPLE_ROWS
    r = jnp.arange(R)[:, None]
    stack = lambda a: a.reshape(NSA_KV_HEADS, NSA_GROUP * R, a.shape[-1])
    cached = _bias_lookup(tbl, past + r - jnp.arange(past)[None, :])
    new = _bias_lookup(tbl, r - jnp.arange(R)[None, :])
    win = _bias_lookup(tbl, NSA_WINDOW + r - jnp.arange(NSA_WINDOW)[None, :])
    return stack(cached), stack(new), stack(win)


def _selwin_sample(qs, cache_t, layer, page_table, msk, newkv, win_t, newwr, tbl, ocmp, gt):
    sb, n_pages = page_table.shape
    past = n_pages * PAGE
    assert win_t.shape[-1] == NSA_WINDOW and past >= NSA_WINDOW and past // SEL_BLOCK <= N_SELBLK
    bsel, bnew, bwin = _sample_bias_tables(tbl, past)
    expand = jnp.asarray((np.arange(N_SELBLK)[:, None] == np.arange(past)[None, :] // SEL_BLOCK).astype(np.float32),
                         dtype=BF16)
    R = SAMPLE_ROWS
    full = lambda a: pl.BlockSpec(a.shape, lambda bi, j, pt: (0,) * a.ndim)
    seq = lambda a: pl.BlockSpec((None,) + a.shape[1:], lambda bi, j, pt: (bi,) + (0,) * (a.ndim - 1))
    win_spec = pl.BlockSpec((None, None) + win_t.shape[2:], lambda bi, j, pt: (layer, bi, 0, 0))
    return pl.pallas_call(
        _selwin_sample_kernel,
        grid_spec=pltpu.PrefetchScalarGridSpec(
            num_scalar_prefetch=1, grid=(sb, n_pages // PAGES_PER_STEP),
            in_specs=[seq(qs)] + _page_specs(layer, 256, 1) + [seq(msk), seq(newkv), win_spec, seq(newwr),
                                                               full(bsel), full(bnew), full(bwin), full(expand),
                                                               seq(ocmp), seq(gt)],
            out_specs=pl.BlockSpec((None, R, NSA_WIDTH), lambda bi, j, pt: (bi, 0, 0)),
            scratch_shapes=[pltpu.VMEM((256, past), BF16),
                            pltpu.VMEM((NSA_GROUP * R, 1), F32), pltpu.VMEM((NSA_GROUP * R, 1), F32),
                            pltpu.VMEM((NSA_GROUP * R, HEAD_DIM), F32),
                            pltpu.VMEM((R, NSA_WIDTH), F32), pltpu.VMEM((R, NSA_WIDTH), F32)]),
        out_shape=jax.ShapeDtypeStruct((sb, R, NSA_WIDTH), F32),
        compiler_params=_cparams(("arbitrary", "arbitrary"), VMEM_LIMIT_BYTES),
        name="nsa_selwin_sample",
    )(page_table, qs, *([cache_t] * PAGES_PER_STEP), msk, newkv, win_t, newwr, bsel, bnew, bwin, expand, ocmp, gt)


def _fox_suffix_kernel(pt_ref, *refs):
    pages = refs[:PAGES_PER_STEP]
    sl_ref, d_ref, carry_ref = refs[PAGES_PER_STEP:]
    @pl.when(pl.program_id(1) == 0)
    def _():
        carry_ref[...] = jnp.zeros_like(carry_ref)

    carry = carry_ref[:, 0:1]
    for r, p_ref in reversed(list(enumerate(pages))):
        lf = p_ref[...]
        d_ref[:, r * PAGE:(r + 1) * PAGE] = _dot_f32(lf, sl_ref[...]) + carry
        carry = carry + jnp.sum(lf, axis=1, keepdims=True)
    carry_ref[...] = jnp.broadcast_to(carry, carry_ref.shape)


def _fox_suffix(logf_t, layer, page_table):
    sb, n_pages = page_table.shape
    n_steps = n_pages // PAGES_PER_STEP
    sl = jnp.asarray(np.tril(np.ones((PAGE, PAGE), np.float32), -1))
    specs = [pl.BlockSpec((None, None, FOX_HEADS, PAGE), functools.partial(
        lambda bi, j, pt, r: (layer, pt[bi, (n_steps - 1 - j) * PAGES_PER_STEP + r], 0, 0), r=r))
        for r in range(PAGES_PER_STEP)]
    return pl.pallas_call(
        _fox_suffix_kernel,
        grid_spec=pltpu.PrefetchScalarGridSpec(
            num_scalar_prefetch=1, grid=(sb, n_steps),
            in_specs=specs + [pl.BlockSpec(sl.shape, lambda bi, j, pt: (0, 0))],
            out_specs=pl.BlockSpec((None, FOX_HEADS, PAGES_PER_STEP * PAGE), lambda bi, j, pt: (bi, 0, n_steps - 1 - j)),
            scratch_shapes=[pltpu.VMEM((FOX_HEADS, LANES), F32)]),
        out_shape=jax.ShapeDtypeStruct((sb, FOX_HEADS, n_pages * PAGE), F32),
        compiler_params=_cparams(("arbitrary", "arbitrary")),
        name="fox_suffix",
    )(page_table, *([logf_t] * PAGES_PER_STEP), sl)


def _fox_sample_kernel(pt_ref, qs_ref, *refs):
    pages = refs[:PAGES_PER_STEP]
    dt_ref, newkv_ref, lfnew_ref, hmask_ref, o_ref, qbd_ref, crel_ref, m_ref, l_ref, acc_ref = refs[PAGES_PER_STEP:]
    j = pl.program_id(1)
    R = SAMPLE_ROWS
    H = FOX_HEADS
    tk = PAGES_PER_STEP * PAGE

    @pl.when(j == 0)
    def _():
        q = qs_ref[...].astype(F32)
        qbd_ref[...] = (jnp.concatenate([q] * H, axis=0) * hmask_ref[...]).astype(BF16)
        tri = jnp.where(lax.broadcasted_iota(jnp.int32, (R, R), 1) <= lax.broadcasted_iota(jnp.int32, (R, R), 0), 1.0, 0.0)
        crel = _dot_f32(tri, lfnew_ref[...])
        crel_ref[...] = jnp.concatenate([crel[:, h:h + 1] for h in range(H)], axis=0)
        m_ref[...] = jnp.full_like(m_ref, NEG)
        l_ref[...] = jnp.zeros_like(l_ref)
        acc_ref[...] = jnp.zeros_like(acc_ref)

    kt =jnp.concatenate([p[0:FOX_WIDTH, :] for p in pages], axis=1).astype(BF16)
    vt = jnp.concatenate([p[FOX_WIDTH:, :] for p in pages], axis=1).astype(BF16)
    drows = jnp.concatenate([jnp.broadcast_to(dt_ref[h:h + 1, :], (R, tk)) for h in range(H)], axis=0)
    s = jnp.dot(qbd_ref[...], kt, preferred_element_type=F32) + crel_ref[...] + drows
    _online_update(s, s > 2 * NEG, vt, m_ref, l_ref, acc_ref, v_t=True)

    @pl.when(j == pl.num_programs(1) - 1)
    def _():
        lf = lfnew_ref[...]
        iu = lax.broadcasted_iota(jnp.int32, (R, R), 0)
        ir = lax.broadcasted_iota(jnp.int32, (R, R), 1)
        a_le = jnp.where(ir <= iu, 1.0, 0.0)
        b_gt = jnp.where(iu > ir, 1.0, 0.0)
        dnew = jnp.concatenate([_dot_f32(a_le, lf[:, h:h + 1] * b_gt) for h in range(H)], axis=0)
        rq = lax.broadcasted_iota(jnp.int32, (H * R, 1), 0) & (R - 1)
        valid = lax.broadcasted_iota(jnp.int32, (H * R, R), 1) <= rq
        knew = newkv_ref[:, 0:FOX_WIDTH].astype(BF16)
        vnew = newkv_ref[:, FOX_WIDTH:].astype(BF16)
        _online_update(_dot_nt(qbd_ref[...], knew) + dnew, valid, vnew, m_ref, l_ref, acc_ref)
        o = (acc_ref[...] / jnp.maximum(l_ref[...], 1e-30)) * hmask_ref[...]
        out = o[0:R]
        for h in range(1, H):
            out = out + o[h * R:(h + 1) * R]
        o_ref[...] = out


def _fox_sample(qs, kv_t, layer, page_table, dt, newkv, lfnew):
    sb, n_pages = page_table.shape
    R = SAMPLE_ROWS
    tk = PAGES_PER_STEP * PAGE
    hmask = jnp.asarray((np.arange(FOX_HEADS * R)[:, None] // R == np.arange(FOX_WIDTH)[None, :] // HEAD_DIM)
                        .astype(np.float32))
    seq = lambda a: pl.BlockSpec((None,) + a.shape[1:], lambda bi, j, pt: (bi,) + (0,) * (a.ndim - 1))
    return pl.pallas_call(
        _fox_sample_kernel,
        grid_spec=pltpu.PrefetchScalarGridSpec(
            num_scalar_prefetch=1, grid=(sb, n_pages // PAGES_PER_STEP),
            in_specs=[seq(qs)] + _page_specs(layer, 2 * FOX_WIDTH, 0)
                     + [pl.BlockSpec((None, FOX_HEADS, tk), lambda bi, j, pt: (bi, 0, j)), seq(newkv), seq(lfnew),
                        pl.BlockSpec(hmask.shape, lambda bi, j, pt: (0, 0))],
            out_specs=pl.BlockSpec((None, R, FOX_WIDTH), lambda bi, j, pt: (bi, 0, 0)),
            scratch_shapes=[pltpu.VMEM((FOX_HEADS * R, FOX_WIDTH), BF16), pltpu.VMEM((FOX_HEADS * R, 1), F32),
                            pltpu.VMEM((FOX_HEADS * R, 1), F32), pltpu.VMEM((FOX_HEADS * R, 1), F32),
                            pltpu.VMEM((FOX_HEADS * R, FOX_WIDTH), F32)]),
        out_shape=jax.ShapeDtypeStruct((sb, R, FOX_WIDTH), F32),
        compiler_params=_cparams(("arbitrary", "arbitrary"), VMEM_LIMIT_BYTES),
        name="fox_sample",
    )(page_table, qs, *([kv_t] * PAGES_PER_STEP), dt, newkv, lfnew, hmask)


def _j_rmsnorm(x, g):
    xf = x.astype(jnp.float32)
    y = xf * lax.rsqrt(jnp.mean(xf * xf, axis=-1, keepdims=True) + EPS)
    return (y * g.astype(jnp.float32)).astype(x.dtype)


def _j_masked_softmax(s, mask):
    s = jnp.where(mask, s, -1e30)
    p = jnp.exp(s - jnp.max(s, axis=-1, keepdims=True)) * mask
    return p / jnp.maximum(jnp.sum(p, axis=-1, keepdims=True), 1e-30)


def _j_rel_bucket(dist):
    exact = NUM_BUCKETS // 2
    d = jnp.maximum(dist, 0)
    log_ratio = jnp.log(jnp.maximum(d, 1).astype(jnp.float32) / exact) / math.log(MAX_DISTANCE / exact)
    large = jnp.minimum(exact + (log_ratio * (NUM_BUCKETS - exact)).astype(jnp.int32), NUM_BUCKETS - 1)
    return jnp.where(d < exact, d, large)


def _j_gather_pages(pool, page_table):
    g = pool.reshape(pool.shape[0], -1)[page_table]
    g = g.reshape(page_table.shape + pool.shape[1:])
    return g.reshape((g.shape[0], g.shape[1] * g.shape[2]) + g.shape[3:])


def _j_nsa_prepare(kv_rows, pe, w1, w2, kn_cmp):
    B, T = kv_rows.shape[:2]
    nc = (T - CMP_BLOCK) // CMP_STRIDE + 1
    seg = kv_rows[:, :(nc + 1) * CMP_STRIDE, 0:2].reshape(B, nc + 1, CMP_STRIDE, 2, NSA_KV_HEADS, HEAD_DIM)
    blk = jnp.concatenate([seg[:, :-1], seg[:, 1:]], axis=2)
    blk = blk + pe.transpose(1, 0, 2)[None, None, :, :, None, :]
    flat = blk.transpose(0, 1, 3, 4, 2, 5).reshape(B, nc, 2, NSA_KV_HEADS, CMP_BLOCK * HEAD_DIM)
    hid = jax.nn.gelu(jnp.einsum('bnjgf,jfe->bnjge', flat, w1))
    ckv = jnp.einsum('bnjge,jed->bnjgd', hid, w2)
    ck = _j_rmsnorm(ckv[:, :, 0], kn_cmp)
    cv = ckv[:, :, 1]
    c_end = jnp.arange(nc) * CMP_STRIDE + (CMP_BLOCK - 1)
    nbs = -(-T // SEL_BLOCK)
    sel = jnp.pad(kv_rows[:, :, 2:4], ((0, 0), (0, nbs * SEL_BLOCK - T), (0, 0), (0, 0), (0, 0)))
    sel = sel.reshape(B, nbs, SEL_BLOCK, 2, NSA_KV_HEADS, HEAD_DIM).transpose(3, 0, 4, 1, 2, 5)
    return ck, cv, c_end, sel[0], sel[1]


def _j_nsa_attend(q, gates, q_pos, ck, cv, c_end, sk, sv, wk, wv, w_pos, rel_bias):
    B, Q = q.shape[:2]
    scale = HEAD_DIM ** -0.5
    tbl = rel_bias.astype(jnp.float32).T.reshape(NSA_KV_HEADS, NSA_GROUP, NUM_BUCKETS)
    qg = q.reshape(B, Q, NSA_KV_HEADS, NSA_GROUP, HEAD_DIM).transpose(0, 2, 3, 1, 4)
    t = q_pos[:, None]
    s = jnp.einsum('bkgqd,bnkd->bkgqn', qg, ck).astype(jnp.float32) * scale + tbl[:, :, _j_rel_bucket(t - c_end[None])]
    p_cmp = _j_masked_softmax(s, c_end[None] <= t)
    o_cmp = jnp.einsum('bkgqn,bnkd->bkgqd', p_cmp.astype(cv.dtype), cv)
    nbs = sk.shape[2]
    imp = jnp.sum(p_cmp, axis=2)
    nc = imp.shape[-1]
    imp = jnp.pad(imp, ((0, 0), (0, 0), (0, 0), (0, nbs * SEL_RATIO - nc)))
    imp = imp.reshape(B, NSA_KV_HEADS, Q, nbs, SEL_RATIO).sum(-1)
    blk = jnp.arange(nbs)[None]
    cur = (q_pos // SEL_BLOCK)[:, None]
    forced = (blk == 0) | (blk == cur) | (blk == cur - 1)
    score = jnp.where(blk > cur, -jnp.inf, jnp.where(forced, jnp.inf, imp))
    n_sel = min(SEL_TOPN, nbs)
    _, idx = lax.top_k(score, n_sel)
    pick = jax.vmap(jax.vmap(lambda kb, ix: kb[ix]))
    gk = pick(sk, idx).reshape(B, NSA_KV_HEADS, Q, n_sel * SEL_BLOCK, HEAD_DIM)
    gv = pick(sv, idx).reshape(B, NSA_KV_HEADS, Q, n_sel * SEL_BLOCK, HEAD_DIM)
    spos = (idx[..., None] * SEL_BLOCK + jnp.arange(SEL_BLOCK)).reshape(B, NSA_KV_HEADS, Q, n_sel * SEL_BLOCK)
    kidx = jnp.arange(NSA_KV_HEADS)[None, :, None, None, None]
    gidx = jnp.arange(NSA_GROUP)[None, None, :, None, None]
    bias = tbl[kidx, gidx, _j_rel_bucket(t - spos)[:, :, None]]
    s = jnp.einsum('bkgqd,bkqsd->bkgqs', qg, gk).astype(jnp.float32) * scale + bias
    p = _j_masked_softmax(s, (spos <= t)[:, :, None])
    o_sel = jnp.einsum('bkgqs,bkqsd->bkgqd', p.astype(gv.dtype), gv)
    dist = t - w_pos[None]
    s = jnp.einsum('bkgqd,bwkd->bkgqw', qg, wk).astype(jnp.float32) * scale + tbl[:, :, _j_rel_bucket(dist)]
    p = _j_masked_softmax(s, (dist >= 0) & (dist <= NSA_WINDOW) & (w_pos[None] >= 0))
    o_win = jnp.einsum('bkgqw,bwkd->bkgqd', p.astype(wv.dtype), wv)
    o = jnp.stack([o_cmp, o_sel, o_win], axis=-1).transpose(0, 3, 1, 2, 4, 5)
    o = o.reshape(B, Q, NSA_HEADS, HEAD_DIM, 3)
    return jnp.einsum('bqhdr,bqhr->bqhd', o, gates.astype(o.dtype)).reshape(B, Q, NSA_WIDTH)


def _j_nsa_prompt(q, gates, kv_rows, win_rows, pe, w1, w2, kn_cmp, rel_bias):
    B, T = q.shape[:2]
    QB = 128
    ck, cv, c_end, sk, sv = _j_nsa_prepare(kv_rows, pe, w1, w2, kn_cmp)
    win_pad = jnp.pad(win_rows, ((0, 0), (NSA_WINDOW, 0), (0, 0), (0, 0), (0, 0)))

    def block(i):
        q0 = i * QB
        qb = lax.dynamic_slice_in_dim(q, q0, QB, axis=1)
        gb = lax.dynamic_slice_in_dim(gates, q0, QB, axis=1)
        wb = lax.dynamic_slice_in_dim(win_pad, q0, NSA_WINDOW + QB, axis=1)
        q_pos = q0 + jnp.arange(QB)
        w_pos = q0 - NSA_WINDOW + jnp.arange(NSA_WINDOW + QB)
        return _j_nsa_attend(qb, gb, q_pos, ck, cv, c_end, sk, sv, wb[:, :, 0], wb[:, :, 1], w_pos, rel_bias)

    o = lax.map(block, jnp.arange(T // QB))
    return o.transpose(1, 0, 2, 3).reshape(B, T, NSA_WIDTH)


def _j_fox_attend(q, cq, q_pos, k, v, ck, k_pos):
    s = jnp.einsum('bqhd,bshd->bhqs', q, k).astype(jnp.float32) * HEAD_DIM ** -0.5
    s = s + jnp.swapaxes(cq, 1, 2)[..., :, None] - jnp.swapaxes(ck, 1, 2)[..., None, :]
    p = _j_masked_softmax(s, k_pos[None, :] <= q_pos[:, None])
    return jnp.einsum('bhqs,bshd->bqhd', p.astype(v.dtype), v)


def _pad_cols(w, width):
    return jnp.pad(w, ((0, 0), (0, width - w.shape[1])))


def _prep_even_w(w):
    gq, gk, gv, glr, gog, nq, nkv, ng = _split(w, EVEN_SIZES)
    return jnp.concatenate([gq, gk, gv, gog, nq, nkv, _pad_cols(glr, LANES), _pad_cols(ng, LANES)], axis=1).astype(BF16)


def _prep_odd_w(w):
    fq, fk, fv, ff, cg = _split(w, ODD_SIZES)
    return jnp.concatenate([fq, fk, fv, cg, _pad_cols(ff, LANES)], axis=1).astype(BF16)


def _row(v):
    return v.reshape(1, -1).astype(F32)


def _tile_row(v, reps):
    return jnp.tile(v.astype(F32), reps).reshape(1, -1)


def kernel(x_prompt, x_sample, cache_nsa_kv, state_nsa_win, state_gla, cache_fox_kv, cache_fox_logf, state_conv, cache_mem_kv, page_table, mem_prompt, rel_bias, norm_mix, norm_xattn, norm_ffn, even_w_in, even_w_out, gla_w_gate, gla_b_gate, gla_out_norm, nsa_q_norm, nsa_k_norm, nsa_cmp_pe, nsa_cmp_w1, nsa_cmp_w2, odd_w_in, odd_w_out, fox_q_norm, fox_k_norm, fox_b_f, conv_w, conv_b, conv_ln_g, conv_ln_b, mem_norm, xa_wq, xa_wkv, xa_wo, xa_q_norm, xa_k_norm, ffn_w_in, ffn_w_out):
    B, T, _ = x_prompt.shape
    SB, SQ, _ = x_sample.shape
    depth = norm_mix.shape[0]
    past_len = page_table.shape[1] * cache_nsa_kv.shape[2]
    n_win = state_nsa_win.shape[2]
    dec_pos = past_len + jnp.arange(SQ)
    MP = B * T
    SR = SAMPLE_ROWS
    MS = SB * SR

    yp = x_prompt.reshape(MP, D_MODEL)
    ys = jnp.pad(x_sample, ((0, 0), (0, SR - SQ), (0, 0))).reshape(MS, D_MODEL)

    nsa_cache_t = _feature_major(cache_nsa_kv)
    fox_cache_t = _feature_major(cache_fox_kv)
    logf_t = _feature_major(cache_fox_logf)
    win_t = _feature_major(state_nsa_win)
    mem_cache_t = _feature_major(cache_mem_kv)

    nsa_kv_p, nsa_kv_s, win_p, win_s, gla_p, gla_s = [], [], [], [], [], []
    fox_kv_p, fox_kv_s, logf_p, logf_s, conv_p, conv_s, mem_kv_p = [], [], [], [], [], [], []

    for layer in range(depth):
        if layer % 2 == 0:
            e = layer // 2
            w_pad = _prep_even_w(even_w_in[e])
            wg_pad = jnp.pad(gla_w_gate[e], ((0, LANES - GLA_RANK), (0, 0))).astype(BF16)
            bg = _row(gla_b_gate[e])
            qn = _tile_row(nsa_q_norm[e], NSA_HEADS)
            kn1 = _tile_row(nsa_k_norm[e, 1], NSA_KV_HEADS)
            kn2 = _tile_row(nsa_k_norm[e, 2], NSA_KV_HEADS)
            gn = _row(gla_out_norm[e])
            w_out = even_w_out[e].astype(BF16)
            g_mix = _row(norm_mix[layer])
            pe = jnp.tile(nsa_cmp_pe[e].astype(F32), (1, 1, NSA_KV_HEADS))
            w1b = _compress_weights(nsa_cmp_w1[e])
            w2b = nsa_cmp_w2[e].astype(BF16)
            kn0 = _row(nsa_k_norm[e, 0])
            tbl = rel_bias.astype(F32).T
            (q, k, v, la, og, nqs, kvr, wr, gt, selkv, winkv, nqst, selvt, winvt, gtt) = _even_in(
                yp, g_mix, w_pad, wg_pad, bg, qn, kn1, kn2, 256)
            r3 = lambda a: a.reshape(B, T, a.shape[-1])
            s0t = jnp.zeros((B, GLA_HEADS, GLA_DV, GLA_DK), F32)
            o_gla, sfin_t = _gla(r3(q), r3(k), r3(v), r3(la), r3(og), gn, s0t, 256, B)
            kvr5 = kvr.reshape(B, T, 4, NSA_KV_HEADS, HEAD_DIM)
            wr5 = wr.reshape(B, T, 2, NSA_KV_HEADS, HEAD_DIM)
            ck, cv = _compress_prompt(r3(kvr), pe, w1b, w2b, kn0)
            ocmp_t, mskt = _cmp_topk_prompt(nqst, ck, cv, tbl, B, T, 256, SEL_TOPN)
            o_nsa = _selwin_prompt(nqst, selkv, selvt, winkv, winvt, mskt, tbl, ocmp_t, gtt, B, T, 256)
            yp = _out_proj(yp, o_gla.reshape(MP, GLA_WIDTH), o_nsa.reshape(MP, NSA_WIDTH),
                           w_out[:GLA_WIDTH], w_out[GLA_WIDTH:], 512)
            nsa_kv_p.append(kvr5)
            win_p.append(wr5[:, -min(NSA_WINDOW, T):])
            gla_p.append(jnp.swapaxes(sfin_t, -1, -2))
            (q, k, v, la, og, nqs, kvr, wr, gt) = _even_in(ys, g_mix, w_pad, wg_pad, bg, qn, kn1, kn2, MS)[:9]
            pad16 = lambda a: jnp.pad(a.reshape(SB, SR, a.shape[-1]), ((0, 0), (0, SUB - SR), (0, 0)))
            s0t = jnp.swapaxes(state_gla[e], -1, -2)
            o_gla, snew_t = _gla(pad16(q), pad16(k), pad16(v), pad16(la), pad16(og), gn, s0t, SUB, 4, n_valid=SQ)
            o_gla = o_gla[:, :SR]
            s3 = lambda a: a.reshape(SB, SR, a.shape[-1])
            kvr5 = kvr.reshape(SB, SR, 4, NSA_KV_HEADS, HEAD_DIM)[:, :SQ]
            wr5 = wr.reshape(SB, SR, 2, NSA_KV_HEADS, HEAD_DIM)[:, :SQ]
            ck, cv = _compress_sample(nsa_cache_t, e, page_table, pe, w1b, w2b, kn0)
            ocmp, msk = _cmp_topk(s3(nqs), ck, cv, tbl, SR, past_len, SEL_TOPN - 1, past_len // SEL_BLOCK, False)
            o_nsa = _selwin_sample(s3(nqs), nsa_cache_t, e, page_table, msk, s3(kvr), win_t, s3(wr), tbl, ocmp, s3(gt))
            ys = _out_proj(ys, o_gla.reshape(MS, GLA_WIDTH), o_nsa.reshape(MS, NSA_WIDTH),
                           w_out[:GLA_WIDTH], w_out[GLA_WIDTH:], MS)
            nsa_kv_s.append(kvr5)
            win_s.append(jnp.concatenate([state_nsa_win[e][:, SQ:], wr5], axis=1))
            gla_s.append(jnp.swapaxes(snew_t, -1, -2))
        else:
            j = layer // 2
            w_pad = _prep_odd_w(odd_w_in[j])
            qn = _tile_row(fox_q_norm[j], FOX_HEADS)
            kn = _tile_row(fox_k_norm[j], FOX_HEADS)
            bf_pad = jnp.pad(fox_b_f[j].astype(F32), (0, LANES - FOX_HEADS)).reshape(1, LANES)
            w_out = odd_w_out[j].astype(BF16)
            g_mix = _row(norm_mix[layer])
            cw = conv_w[j].astype(F32)
            cb, cg_, cbeta = _row(conv_b[j]), _row(conv_ln_g[j]), _row(conv_ln_b[j])
            qs, kv, kb, lf, c, u, qst, vbt = _odd_in(yp.reshape(B, T, D_MODEL), g_mix, w_pad, qn, kn, bf_pad, 256)
            c8 = c[:, :, :FOX_HEADS]
            o_fox = _fox_prompt(qst, kb, vbt, c8, jnp.swapaxes(c8, 1, 2), 512)
            o_conv, cst = _conv(u, jnp.zeros((B, CONV_WIDTH - 1, CONV_CH), F32), cw, cb, cg_, cbeta, 512, 512)
            yp = _out_proj(yp, o_fox.reshape(MP, FOX_WIDTH), o_conv.reshape(MP, CONV_CH),
                           w_out[:FOX_WIDTH], w_out[FOX_WIDTH:], 512)
            fox_kv_p.append(kv.reshape(B, T, 2, FOX_HEADS, HEAD_DIM))
            logf_p.append(lf[:, :, :FOX_HEADS])
            conv_p.append(cst)
            qs, kv, kb, lf, c, u = _odd_in(ys.reshape(1, MS, D_MODEL), g_mix, w_pad, qn, kn, bf_pad, MS)[:6]
            s3 = lambda a: a.reshape(SB, SR, a.shape[-1])
            new_kv = kv.reshape(SB, SR, 2, FOX_HEADS, HEAD_DIM)[:, :SQ]
            lf_new = lf.reshape(SB, SR, LANES)[:, :SQ, :FOX_HEADS]
            dsuf = _fox_suffix(logf_t, j, page_table)
            o_fox = _fox_sample(s3(qs), fox_cache_t, j, page_table, dsuf, s3(kv), s3(lf))
            o_conv, cst = _conv(u.reshape(SB, SR, CONV_CH), state_conv[j], cw, cb, cg_, cbeta, SR, SQ)
            ys = _out_proj(ys, o_fox.reshape(MS, FOX_WIDTH), o_conv.reshape(MS, CONV_CH),
                           w_out[:FOX_WIDTH], w_out[FOX_WIDTH:], MS)
            fox_kv_s.append(new_kv)
            logf_s.append(lf_new)
            conv_s.append(cst)
        g_xa = _row(norm_xattn[layer])
        wq = xa_wq[layer].astype(BF16)
        wo = xa_wo[layer].astype(BF16)
        xqn = _tile_row(xa_q_norm[layer], XA_HEADS)
        mkv_t = _mem_kv(mem_prompt, _row(mem_norm[layer]), xa_wkv[layer].astype(BF16), _tile_row(xa_k_norm[layer], XA_HEADS))
        mem_kv_p.append(jnp.transpose(mkv_t.reshape(B, 2, XA_HEADS, HEAD_DIM, MEM_LEN), (0, 4, 1, 2, 3)))
        yp = _xattn(yp.reshape(B, T, D_MODEL), mkv_t[None], 0, g_xa, wq, wo, xqn, 512).reshape(MP, D_MODEL)
        ys = _xattn(ys.reshape(SB, SR, D_MODEL), mem_cache_t, layer, g_xa, wq, wo, xqn, SR).reshape(MS, D_MODEL)
        g_ffn = _row(norm_ffn[layer])
        w_in = ffn_w_in[layer].astype(BF16)
        w_o = ffn_w_out[layer].astype(BF16)
        yp = _ffn(yp, g_ffn, w_in, w_o, 512)
        ys = _ffn(ys, g_ffn, w_in, w_o, MS)

    yp = yp.reshape(B, T, D_MODEL)
    ys = ys.reshape(SB, SR, D_MODEL)[:, :SQ]
    return (yp, ys,
            jnp.stack(nsa_kv_p), jnp.stack(nsa_kv_s), jnp.stack(win_p), jnp.stack(win_s),
            jnp.stack(gla_p), jnp.stack(gla_s), jnp.stack(fox_kv_p), jnp.stack(fox_kv_s),
            jnp.stack(logf_p), jnp.stack(logf_s), jnp.stack(conv_p), jnp.stack(conv_s),
            jnp.stack(mem_kv_p))
```

```python
import functools
import math

import jax
import jax.numpy as jnp
import numpy as np
from jax import lax
from jax.experimental import pallas as pl
from jax.experimental.pallas import tpu as pltpu

F32 = jnp.float32
BF16 = jnp.bfloat16

D_MODEL = 1024
HEAD_DIM = 64
GLA_WIDTH = 512
GLA_HEADS = 4
GLA_DV = 128
GLA_DK = 64
GLA_RANK = 16
GLA_TAU = 16.0
NSA_WIDTH = 512
NSA_HEADS = 8
NSA_KV_HEADS = 2
NSA_GROUP = 4
CMP_STRIDE = 16
CMP_BLOCK = 32
CMP_HIDDEN = 256
SEL_BLOCK = 64
SEL_RATIO = 4
SEL_TOPN = 16
NSA_WINDOW = 512
FOX_WIDTH = 512
FOX_HEADS = 8
CONV_CH = 512
CONV_WIDTH = 31
MEM_LEN = 256
XA_HEADS = 4
XA_WIDTH = 256
FFN_HIDDEN = 2816
NUM_BUCKETS = 32
MAX_DISTANCE = 128
EPS = 1e-6
SCALE = HEAD_DIM ** -0.5
NEG = -1e30

EVEN_SIZES = (256, 256, 512, 16, 512, 512, 768, 24)
ODD_SIZES = (512, 512, 512, 8, 1024)

LANES = 128
VMEM_LIMIT_BYTES = 56 * 1024 * 1024
SAMPLE_ROWS = 8
SUB = 16
FAR_GROUP = 4
N_SELBLK = 128


def _cparams(sem, vmem=None):
    return pltpu.CompilerParams(dimension_semantics=sem, vmem_limit_bytes=vmem)


def _split(h, sizes):
    return jnp.split(h, np.cumsum(sizes)[:-1].tolist(), axis=-1)


def _rms_rows(x, g):
    return x * lax.rsqrt(jnp.mean(x * x, axis=-1, keepdims=True) + EPS) * g


def _group_rms(x, gmat, gs):
    x2 = x * x
    hi = x2.astype(BF16)
    lo = (x2 - hi.astype(F32)).astype(BF16)
    ms = (jnp.dot(hi, gmat, preferred_element_type=F32) + jnp.dot(lo, gmat, preferred_element_type=F32)) * (1.0 / gs)
    return x * lax.rsqrt(ms + EPS)


def _log_sigmoid(z):
    return -(jnp.maximum(-z, 0.0) + jnp.log1p(jnp.exp(-jnp.abs(z))))


def _sigmoid(z):
    return 1.0 / (1.0 + jnp.exp(-z))


def _dot_nt(a, b):
    return lax.dot_general(a, b, (((1,), (1,)), ((), ())), preferred_element_type=F32)


def _dot_tn(a, b):
    return lax.dot_general(a, b, (((0,), (0,)), ((), ())), preferred_element_type=F32)


def _dot_f32(a, b):
    return jnp.dot(a, b, preferred_element_type=F32, precision=lax.Precision.HIGHEST)


def _block_ones(width, gs):
    r = np.arange(width) // gs
    return jnp.asarray((r[:, None] == r[None, :]).astype(np.float32), dtype=BF16)


def _even_in_kernel(x_ref, g_ref, w_ref, wg_ref, bg_ref, qn_ref, kn1_ref, kn2_ref, gm512_ref, gm128_ref,
                    q_ref, k_ref, v_ref, la_ref, og_ref, nqs_ref, kvr_ref, wr_ref, gt_ref, selkv_ref, winkv_ref,
                    nqst_ref, selvt_ref, winvt_ref, gtt_ref):
    xb = _rms_rows(x_ref[...], g_ref[...]).astype(BF16)

    def proj(lo, hi):
        return jnp.dot(xb, w_ref[:, lo:hi], preferred_element_type=F32)

    q_ref[...] = proj(0, 256) * (GLA_DK ** -0.5)
    k_ref[...] = proj(256, 512)
    v_ref[...] = proj(512, 1024)
    og = proj(1024, 1536)
    og_ref[...] = og * _sigmoid(og)
    nq = _group_rms(proj(1536, 2048), gm512_ref[...], HEAD_DIM) * qn_ref[...] * SCALE
    nqs_ref[...] = nq.astype(BF16)
    nqst_ref[...] = nq.T.astype(BF16)
    kvr_ref[:, 0:256] = proj(2048, 2304)
    selk = _group_rms(proj(2304, 2432), gm128_ref[...], HEAD_DIM) * kn1_ref[...]
    selv = proj(2432, 2560)
    kvr_ref[:, 256:384] = selk
    kvr_ref[:, 384:512] = selv
    selkv_ref[:, 0:128] = selk.astype(BF16)
    selkv_ref[:, 128:256] = selv.astype(BF16)
    selvt_ref[...] = selv.T.astype(BF16)
    wink = _group_rms(proj(2560, 2688), gm128_ref[...], HEAD_DIM) * kn2_ref[...]
    winv = proj(2688, 2816)
    wr_ref[:, 0:128] = wink
    wr_ref[:, 128:256] = winv
    winkv_ref[:, 0:128] = wink.astype(BF16)
    winkv_ref[:, 128:256] = winv.astype(BF16)
    winvt_ref[...] = winv.T.astype(BF16)
    glr = proj(2816, 2944).astype(BF16)
    z = jnp.dot(glr, wg_ref[...], preferred_element_type=F32) + bg_ref[...]
    la_ref[...] = _log_sigmoid(z) * (1.0 / GLA_TAU)
    gates = _sigmoid(proj(2944, 3072))
    gt_ref[...] = gates
    gtt_ref[...] = gates.T[0:32, :]


def _even_in(x2d, g, w_pad, wg_pad, bg, qn, kn1, kn2, tm):
    m = x2d.shape[0]
    widths = (256, 256, 512, 256, 512, 512, 512, 256, 128, 256, 256)
    dtypes = (F32, F32, F32, F32, F32, BF16, F32, F32, F32, BF16, BF16)
    t_heights = (512, 128, 128, 32)
    t_dtypes = (BF16, BF16, BF16, F32)
    full = lambda a: pl.BlockSpec(a.shape, lambda i: (0,) * a.ndim)
    gm512 = _block_ones(512, HEAD_DIM)
    gm128 = _block_ones(128, HEAD_DIM)
    ins = (x2d, g, w_pad, wg_pad, bg, qn, kn1, kn2, gm512, gm128)
    return pl.pallas_call(
        _even_in_kernel,
        grid=(m // tm,),
        in_specs=[pl.BlockSpec((tm, D_MODEL), lambda i: (i, 0))] + [full(a) for a in ins[1:]],
        out_specs=[pl.BlockSpec((tm, w), lambda i: (i, 0)) for w in widths]
                  + [pl.BlockSpec((h, tm), lambda i: (0, i)) for h in t_heights],
        out_shape=[jax.ShapeDtypeStruct((m, w), d) for w, d in zip(widths, dtypes)]
                  + [jax.ShapeDtypeStruct((h, m), d) for h, d in zip(t_heights, t_dtypes)],
        compiler_params=_cparams(("arbitrary",), VMEM_LIMIT_BYTES),
        name="even_in",
    )(*ins)


def _odd_in_kernel(x_ref, g_ref, w_ref, qn_ref, kn_ref, bf_ref, gm512_ref, tri_ref,
                   qs_ref, kv_ref, kb_ref, lf_ref, c_ref, u_ref, qst_ref, vbt_ref, carry_ref):
    @pl.when(pl.program_id(1) == 0)
    def _():
        carry_ref[...] = jnp.zeros_like(carry_ref)

    xb = _rms_rows(x_ref[...], g_ref[...]).astype(BF16)

    def proj(lo, hi):
        return jnp.dot(xb, w_ref[:, lo:hi], preferred_element_type=F32)

    q = _group_rms(proj(0, 512), gm512_ref[...], HEAD_DIM) * qn_ref[...] * SCALE
    qs_ref[...] = q.astype(BF16)
    qst_ref[...] = q.T.astype(BF16)
    k = _group_rms(proj(512, 1024), gm512_ref[...], HEAD_DIM) * kn_ref[...]
    v = proj(1024, 1536)
    kv_ref[:, 0:512] = k
    kv_ref[:, 512:1024] = v
    kb_ref[...] = k.astype(BF16)
    vbt_ref[...] = v.T.astype(BF16)
    u_ref[...] = proj(1536, 2048) * _sigmoid(proj(2048, 2560))
    lf = _log_sigmoid(proj(2560, 2688) + bf_ref[...])
    lf_ref[...] = lf
    c = _dot_f32(tri_ref[...], lf) + carry_ref[0:1, :]
    c_ref[...] = c
    carry_ref[0:1, :] = c[-1:, :]


def _odd_in(x3d, g, w_pad, qn, kn, bf_pad, tm):
    b, t, _ = x3d.shape
    widths = (512, 1024, 512, 128, 128, 512)
    dtypes = (BF16, F32, BF16, F32, F32, F32)
    gm512 = _block_ones(512, HEAD_DIM)
    tri = jnp.asarray(np.tril(np.ones((tm, tm), np.float32)))
    ins = (x3d, g, w_pad, qn, kn, bf_pad, gm512, tri)
    full = lambda a: pl.BlockSpec(a.shape, lambda bi, i: (0,) * a.ndim)
    nt = t // tm
    t_spec = pl.BlockSpec((FOX_WIDTH, tm), lambda bi, i: (0, bi * nt + i))
    t_shape = jax.ShapeDtypeStruct((FOX_WIDTH, b * t), BF16)
    return pl.pallas_call(
        _odd_in_kernel,
        grid=(b, nt),
        in_specs=[pl.BlockSpec((None, tm, D_MODEL), lambda bi, i: (bi, i, 0))] + [full(a) for a in ins[1:]],
        out_specs=[pl.BlockSpec((None, tm, w), lambda bi, i: (bi, i, 0)) for w in widths] + [t_spec, t_spec],
        out_shape=[jax.ShapeDtypeStruct((b, t, w), d) for w, d in zip(widths, dtypes)] + [t_shape, t_shape],
        scratch_shapes=[pltpu.VMEM((8, 128), F32)],
        compiler_params=_cparams(("arbitrary", "arbitrary"), VMEM_LIMIT_BYTES),
        name="odd_in",
    )(*ins)


def _out_proj_kernel(res_ref, a1_ref, a2_ref, w1_ref, w2_ref, o_ref):
    acc = jnp.dot(a1_ref[...].astype(BF16), w1_ref[...], preferred_element_type=F32)
    acc = acc + jnp.dot(a2_ref[...].astype(BF16), w2_ref[...], preferred_element_type=F32)
    o_ref[...] = res_ref[...] + acc


def _out_proj(res, a1, a2, w1, w2, tm):
    m = res.shape[0]
    row = lambda a: pl.BlockSpec((tm, a.shape[1]), lambda i: (i, 0))
    full = lambda a: pl.BlockSpec(a.shape, lambda i: (0, 0))
    return pl.pallas_call(
        _out_proj_kernel,
        grid=(m // tm,),
        in_specs=[row(res), row(a1), row(a2), full(w1), full(w2)],
        out_specs=row(res),
        out_shape=jax.ShapeDtypeStruct(res.shape, F32),
        compiler_params=_cparams(("arbitrary",), VMEM_LIMIT_BYTES),
        name="out_proj",
    )(res, a1, a2, w1, w2)


def _ffn_kernel(x_ref, g_ref, wg_ref, wu_ref, wo_ref, o_ref, xn_ref, acc_ref):
    j = pl.program_id(1)

    @pl.when(j == 0)
    def _():
        xn_ref[...] = _rms_rows(x_ref[...], g_ref[...]).astype(BF16)
        acc_ref[...] = jnp.zeros_like(acc_ref)

    xb = xn_ref[...]
    gate = jnp.dot(xb, wg_ref[...], preferred_element_type=F32)
    up = jnp.dot(xb, wu_ref[...], preferred_element_type=F32)
    h = (gate * _sigmoid(gate) * up).astype(BF16)
    acc_ref[...] += jnp.dot(h, wo_ref[...], preferred_element_type=F32)

    @pl.when(j == pl.num_programs(1) - 1)
    def _():
        o_ref[...] = x_ref[...] + acc_ref[...]


def _ffn(x2d, g, w_in, w_out, tm, n_chunks=2):
    m = x2d.shape[0]
    th = FFN_HIDDEN // n_chunks
    return pl.pallas_call(
        _ffn_kernel,
        grid=(m // tm, n_chunks),
        in_specs=[pl.BlockSpec((tm, D_MODEL), lambda i, j: (i, 0)),
                  pl.BlockSpec((1, D_MODEL), lambda i, j: (0, 0)),
                  pl.BlockSpec((D_MODEL, th), lambda i, j: (0, j)),
                  pl.BlockSpec((D_MODEL, th), lambda i, j: (0, n_chunks + j)),
                  pl.BlockSpec((th, D_MODEL), lambda i, j: (j, 0))],
        out_specs=pl.BlockSpec((tm, D_MODEL), lambda i, j: (i, 0)),
        out_shape=jax.ShapeDtypeStruct(x2d.shape, F32),
        scratch_shapes=[pltpu.VMEM((tm, D_MODEL), BF16), pltpu.VMEM((tm, D_MODEL), F32)],
        compiler_params=_cparams(("arbitrary", "arbitrary"), VMEM_LIMIT_BYTES),
        name="ffn",
    )(x2d, g, w_in, w_in, w_out)


def _mem_kv_kernel(m_ref, g_ref, w_ref, kn_ref, gm_ref, o_ref):
    xb = _rms_rows(m_ref[...], g_ref[...]).astype(BF16)
    kv = jnp.dot(xb, w_ref[...], preferred_element_type=F32)
    o_ref[0:XA_WIDTH, :] = (_group_rms(kv[:, 0:XA_WIDTH], gm_ref[...], HEAD_DIM) * kn_ref[...]).T
    o_ref[XA_WIDTH:, :] = kv[:, XA_WIDTH:].T


def _mem_kv(mem, g, wkv, kn):
    b = mem.shape[0]
    gm = _block_ones(XA_WIDTH, HEAD_DIM)
    full = lambda a: pl.BlockSpec(a.shape, lambda i: (0,) * a.ndim)
    return pl.pallas_call(
        _mem_kv_kernel,
        grid=(b,),
        in_specs=[pl.BlockSpec((None, MEM_LEN, D_MODEL), lambda i: (i, 0, 0)), full(g), full(wkv), full(kn), full(gm)],
        out_specs=pl.BlockSpec((None, 2 * XA_WIDTH, MEM_LEN), lambda i: (i, 0, 0)),
        out_shape=jax.ShapeDtypeStruct((b, 2 * XA_WIDTH, MEM_LEN), F32),
        compiler_params=_cparams(("arbitrary",)),
        name="mem_kv",
    )(mem, g, wkv, kn, gm)


def _xattn_kernel(x_ref, mkv_ref, g_ref, wq_ref, wo_ref, qn_ref, gm_ref, o_ref):
    x = x_ref[...]
    xb = _rms_rows(x, g_ref[...]).astype(BF16)
    q = jnp.dot(xb, wq_ref[...], preferred_element_type=F32)
    q = _group_rms(q, gm_ref[...], HEAD_DIM) * qn_ref[...]
    qb = (q * SCALE).astype(BF16)
    outs = []
    for h in range(XA_HEADS):
        kt = mkv_ref[h * HEAD_DIM:(h + 1) * HEAD_DIM, :].astype(BF16)
        vt = mkv_ref[XA_WIDTH + h * HEAD_DIM:XA_WIDTH + (h + 1) * HEAD_DIM, :].astype(BF16)
        s = jnp.dot(qb[:, h * HEAD_DIM:(h + 1) * HEAD_DIM], kt, preferred_element_type=F32)
        e = jnp.exp(s - jnp.max(s, axis=-1, keepdims=True))
        p = (e / jnp.sum(e, axis=-1, keepdims=True)).astype(BF16)
        outs.append(_dot_nt(p, vt))
    o = jnp.concatenate(outs, axis=-1).astype(BF16)
    o_ref[...] = x + jnp.dot(o, wo_ref[...], preferred_element_type=F32)


def _xattn(x3d, mkv_t, layer, g, wq, wo, qn, tm):
    b, t, _ = x3d.shape
    gm = _block_ones(XA_WIDTH, HEAD_DIM)
    full = lambda a: pl.BlockSpec(a.shape, lambda bi, i: (0,) * a.ndim)
    return pl.pallas_call(
        _xattn_kernel,
        grid=(b, t // tm),
        in_specs=[pl.BlockSpec((None, tm, D_MODEL), lambda bi, i: (bi, i, 0)),
                  pl.BlockSpec((None, None, 2 * XA_WIDTH, MEM_LEN), lambda bi, i: (layer, bi, 0, 0)),
                  full(g), full(wq), full(wo), full(qn), full(gm)],
        out_specs=pl.BlockSpec((None, tm, D_MODEL), lambda bi, i: (bi, i, 0)),
        out_shape=jax.ShapeDtypeStruct(x3d.shape, F32),
        compiler_params=_cparams(("arbitrary", "arbitrary"), VMEM_LIMIT_BYTES),
        name="xattn",
    )(x3d, mkv_t, g, wq, wo, qn, gm)


def _gla_kernel(q_ref, k_ref, v_ref, la_ref, og_ref, gn_ref, s0_ref, tri_ref, hsel_ref,
                o_ref, sfin_ref, st_ref, *, n_sub, n_valid, bb):
    ti = pl.program_id(1)

    @pl.when(ti == 0)
    def _():
        st_ref[...] = s0_ref[...]

    tri = tri_ref[...]
    hsel = hsel_ref[...]
    gn = gn_ref[...]
    row = lax.broadcasted_iota(jnp.int32, (SUB, 1), 0)

    def sub_block(i, carry):
        for bi in range(bb):
            one_sequence(i, bi)
        return carry

    def one_sequence(i, bi):
        r0 = pl.multiple_of(i * SUB, SUB)
        rows = pl.ds(r0, SUB)
        q = q_ref[bi, rows, :]
        k = k_ref[bi, rows, :]
        v = v_ref[bi, rows, :]
        la = la_ref[bi, rows, :]
        if n_valid is not None:
            live = (row + r0) < n_valid
            la = jnp.where(live, la, 0.0)
            k = jnp.where(live, k, 0.0)
        b = _dot_f32(tri, la)
        b_end = b[SUB - 1:SUB, :]
        qd = (q * jnp.exp(b)).astype(BF16)
        kd = (k * jnp.exp(b_end - b)).astype(BF16)
        vb = v.astype(BF16)
        tiles = []
        for s in range(SUB):
            e = jnp.exp(jnp.minimum(b - b[s:s + 1, :], 0.0))
            z = (q * k[s:s + 1, :]) * e
            tiles.append(jnp.where(row >= s, z, 0.0))
        att = jnp.dot(jnp.concatenate(tiles, axis=0).astype(BF16), hsel, preferred_element_type=F32)
        dec = jnp.exp(b_end)
        outs = []
        for h in range(GLA_HEADS):
            dk = slice(h * GLA_DK, (h + 1) * GLA_DK)
            dv = slice(h * GLA_DV, (h + 1) * GLA_DV)
            st = st_ref[bi, h]
            o = _dot_nt(qd[:, dk], st.astype(BF16))
            for s in range(SUB):
                o = o + att[s * SUB:(s + 1) * SUB, h:h + 1] * v[s:s + 1, dv]
            st_ref[bi, h] = st * dec[:, dk] + _dot_tn(vb[:, dv], kd[:, dk])
            outs.append(_rms_rows(o, gn))
        o_ref[bi, rows, :] = jnp.concatenate(outs, axis=-1) * og_ref[bi, rows, :]

    lax.fori_loop(0, n_sub, sub_block, 0)

    @pl.when(ti == pl.num_programs(1) - 1)
    def _():
        sfin_ref[...] = st_ref[...]


def _gla(q, k, v, la, og, gn, s0t, tt, bb, n_valid=None):
    b, t, _ = q.shape
    tri = jnp.asarray(np.tril(np.ones((SUB, SUB), np.float32)))
    hsel = jnp.asarray((np.arange(256)[:, None] // GLA_DK == np.arange(128)[None, :]).astype(np.float32), dtype=BF16)
    seq = lambda w: pl.BlockSpec((bb, tt, w), lambda bi, i: (bi, i, 0))
    full = lambda a: pl.BlockSpec(a.shape, lambda bi, i: (0,) * a.ndim)
    st_spec = pl.BlockSpec((bb, GLA_HEADS, GLA_DV, GLA_DK), lambda bi, i: (bi, 0, 0, 0))
    return pl.pallas_call(
        functools.partial(_gla_kernel, n_sub=tt // SUB, n_valid=n_valid, bb=bb),
        grid=(b // bb, t // tt),
        in_specs=[seq(256), seq(256), seq(512), seq(256), seq(512), full(gn), st_spec, full(tri), full(hsel)],
        out_specs=[seq(512), st_spec],
        out_shape=[jax.ShapeDtypeStruct((b, t, GLA_WIDTH), F32),
                   jax.ShapeDtypeStruct((b, GLA_HEADS, GLA_DV, GLA_DK), F32)],
        scratch_shapes=[pltpu.VMEM((bb, GLA_HEADS, GLA_DV, GLA_DK), F32)],
        compiler_params=_cparams(("arbitrary", "arbitrary")),
        name="gla",
    )(q, k, v, la, og, gn, s0t, tri, hsel)


def _conv_kernel(u_ref, st0_ref, w_ref, b_ref, g_ref, beta_ref, o_ref, st_ref, ext_ref, *, tt, n_valid):
    ti = pl.program_id(1)
    ctx = CONV_WIDTH - 1

    @pl.when(ti == 0)
    def _():
        ext_ref[0:8, :] = jnp.zeros((8, CONV_CH), F32)
        ext_ref[pl.ds(2, ctx), :] = st0_ref[...]

    ext_ref[pl.ds(32, tt), :] = u_ref[...]
    acc = jnp.zeros((tt, CONV_CH), F32)
    for w in range(CONV_WIDTH):
        acc = acc + ext_ref[pl.ds(2 + w, tt), :] * w_ref[w:w + 1, :]
    y = acc + b_ref[...]
    mu = jnp.mean(y, axis=-1, keepdims=True)
    var = jnp.mean(jnp.square(y - mu), axis=-1, keepdims=True)
    ln = (y - mu) * lax.rsqrt(var + EPS) * g_ref[...] + beta_ref[...]
    o_ref[...] = ln * _sigmoid(ln)

    @pl.when(ti == pl.num_programs(1) - 1)
    def _():
        st_ref[...] = ext_ref[pl.ds(32 + n_valid - ctx, ctx), :]

    ext_ref[0:32, :] = ext_ref[pl.ds(tt, 32), :]


def _conv(u, st0, w, b, g, beta, tt, n_valid):
    bsz, t, _ = u.shape
    ctx = CONV_WIDTH - 1
    full = lambda a: pl.BlockSpec(a.shape, lambda bi, i: (0,) * a.ndim)
    st_spec = pl.BlockSpec((None, ctx, CONV_CH), lambda bi, i: (bi, 0, 0))
    return pl.pallas_call(
        functools.partial(_conv_kernel, tt=tt, n_valid=n_valid),
        grid=(bsz, t // tt),
        in_specs=[pl.BlockSpec((None, tt, CONV_CH), lambda bi, i: (bi, i, 0)), st_spec,
                  full(w), full(b), full(g), full(beta)],
        out_specs=[pl.BlockSpec((None, tt, CONV_CH), lambda bi, i: (bi, i, 0)), st_spec],
        out_shape=[jax.ShapeDtypeStruct(u.shape, F32), jax.ShapeDtypeStruct((bsz, ctx, CONV_CH), F32)],
        scratch_shapes=[pltpu.VMEM((32 + max(tt, 32), CONV_CH), F32)],
        compiler_params=_cparams(("arbitrary", "arbitrary")),
        name="conv",
    )(u, st0, w, b, g, beta)


def _fox_kernel(qt_ref, k_ref, vt_ref, ck_ref, cqt_ref, o_ref, m_ref, l_ref, acc_ref, *, tq, tk):
    qi = pl.program_id(1)
    ki = pl.program_id(2)
    last = (qi * tq + tq - 1) // tk

    @pl.when(ki == 0)
    def _():
        m_ref[...] = jnp.full_like(m_ref, NEG)
        l_ref[...] = jnp.zeros_like(l_ref)
        acc_ref[...] = jnp.zeros_like(acc_ref)

    def tile(neg):
        for h in range(FOX_HEADS):
            cols = slice(h * HEAD_DIM, (h + 1) * HEAD_DIM)
            s = jnp.dot(k_ref[:, cols], qt_ref[cols, :], preferred_element_type=F32)
            s = s + cqt_ref[h:h + 1, :] - ck_ref[:, h:h + 1]
            if neg is not None:
                s = s + neg
            m_old = m_ref[h:h + 1, :]
            m_new = jnp.maximum(m_old, jnp.max(s, axis=0, keepdims=True))
            p = jnp.exp(s - m_new)
            alpha = jnp.exp(m_old - m_new)
            l_ref[h:h + 1, :] = alpha * l_ref[h:h + 1, :] + jnp.sum(p, axis=0, keepdims=True)
            acc_ref[cols, :] = alpha * acc_ref[cols, :] + jnp.dot(vt_ref[cols, :], p.astype(BF16),
                                                                  preferred_element_type=F32)
            m_ref[h:h + 1, :] = m_new

    @pl.when(ki < last)
    def _():
        tile(None)

    @pl.when(ki == last)
    def _():
        s_pos = ki * tk + lax.broadcasted_iota(jnp.int32, (tk, tq), 0)
        t_pos = qi * tq + lax.broadcasted_iota(jnp.int32, (tk, tq), 1)
        tile(jnp.where(s_pos <= t_pos, 0.0, NEG))

    @pl.when(ki == pl.num_programs(2) - 1)
    def _():
        for h in range(FOX_HEADS):
            cols = slice(h * HEAD_DIM, (h + 1) * HEAD_DIM)
            acc_ref[cols, :] = acc_ref[cols, :] / jnp.maximum(l_ref[h:h + 1, :], 1e-30)
        o_ref[...] = acc_ref[...].T


def _fox_prompt(qst, kb, vbt, c8, ct, tq, tk):
    b, t, _ = kb.shape
    nq, nk = t // tq, t // tk
    kmin = lambda qi, ki: jnp.minimum(ki, (qi * tq + tq - 1) // tk)
    return pl.pallas_call(
        functools.partial(_fox_kernel, tq=tq, tk=tk),
        grid=(b, nq, nk),
        in_specs=[pl.BlockSpec((FOX_WIDTH, tq), lambda bi, qi, ki: (0, bi * nq + qi)),
                  pl.BlockSpec((None, tk, FOX_WIDTH), lambda bi, qi, ki: (bi, kmin(qi, ki), 0)),
                  pl.BlockSpec((FOX_WIDTH, tk), lambda bi, qi, ki: (0, bi * nk + kmin(qi, ki))),
                  pl.BlockSpec((None, tk, FOX_HEADS), lambda bi, qi, ki: (bi, kmin(qi, ki), 0)),
                  pl.BlockSpec((None, FOX_HEADS, tq), lambda bi, qi, ki: (bi, 0, qi))],
        out_specs=pl.BlockSpec((None, tq, FOX_WIDTH), lambda bi, qi, ki: (bi, qi, 0)),
        out_shape=jax.ShapeDtypeStruct((b, t, FOX_WIDTH), F32),
        scratch_shapes=[pltpu.VMEM((FOX_HEADS, tq), F32), pltpu.VMEM((FOX_HEADS, tq), F32),
                        pltpu.VMEM((FOX_WIDTH, tq), F32)],
        compiler_params=_cparams(("arbitrary", "arbitrary", "arbitrary"), VMEM_LIMIT_BYTES),
        name="fox_prompt",
    )(qst, kb, vbt, c8, ct)


def _compress_weights(w1):
    w = w1.reshape(2, 2, 8, 2, HEAD_DIM, CMP_HIDDEN)
    z = jnp.zeros_like(w)
    g0 = jnp.concatenate([w, z], axis=-1)
    g1 = jnp.concatenate([z, w], axis=-1)
    wbd = jnp.stack([g0, g1], axis=4)
    return wbd.reshape(2, 2, 8, 4 * HEAD_DIM, 2 * CMP_HIDDEN).astype(BF16)


def _compress_compute(src_refs, pe_ref, w1_ref, w2_ref, kn_ref, ck_ref, cv_ref, sh_ref, nseg):
    sh_ref[pl.ds(nseg, 8), :] = jnp.zeros((8, CMP_HIDDEN), F32)
    for j, src_ref in enumerate(src_refs):
        a = jnp.zeros((nseg, 2 * CMP_HIDDEN), F32)
        bm = jnp.zeros((nseg, 2 * CMP_HIDDEN), F32)
        for q in range(CMP_STRIDE // 2):
            p0, p1 = 2 * q, 2 * q + 1
            x0 = src_ref[pl.ds(p0, nseg, stride=CMP_STRIDE), :]
            x1 = src_ref[pl.ds(p1, nseg, stride=CMP_STRIDE), :]
            xa = jnp.concatenate([x0 + pe_ref[j, p0:p0 + 1, :], x1 + pe_ref[j, p1:p1 + 1, :]], axis=1)
            a = a + jnp.dot(xa.astype(BF16), w1_ref[j, 0, q], preferred_element_type=F32)
            p0, p1 = p0 + CMP_STRIDE, p1 + CMP_STRIDE
            xb = jnp.concatenate([x0 + pe_ref[j, p0:p0 + 1, :], x1 + pe_ref[j, p1:p1 + 1, :]], axis=1)
            bm = bm + jnp.dot(xb.astype(BF16), w1_ref[j, 1, q], preferred_element_type=F32)
        for g in range(NSA_KV_HEADS):
            sh_ref[pl.ds(0, nseg), :] = bm[:, g * CMP_HIDDEN:(g + 1) * CMP_HIDDEN]
            x = a[:, g * CMP_HIDDEN:(g + 1) * CMP_HIDDEN] + sh_ref[pl.ds(1, nseg), :]
            hid = x * (0.5 * (1.0 + jnp.tanh(math.sqrt(2.0 / math.pi) * (x + 0.044715 * (x * x * x)))))
            ckv = jnp.dot(hid.astype(BF16), w2_ref[j], preferred_element_type=F32)
            if j == 0:
                ck_ref[g] = _rms_rows(ckv, kn_ref[...]).astype(BF16)
            else:
                cv_ref[g] = ckv.astype(BF16)


def _compress_prompt_kernel(xk_ref, xv_ref, pe_ref, w1_ref, w2_ref, kn_ref, ck_ref, cv_ref, sh_ref, *, nseg):
    _compress_compute((xk_ref, xv_ref), pe_ref, w1_ref, w2_ref, kn_ref, ck_ref, cv_ref, sh_ref, nseg)


def _compress_prompt(kvr, pe, w1, w2, kn):
    b, t, _ = kvr.shape
    nseg = t // CMP_STRIDE
    full = lambda a: pl.BlockSpec(a.shape, lambda i: (0,) * a.ndim)
    o_spec = pl.BlockSpec((None, NSA_KV_HEADS, nseg, HEAD_DIM), lambda i: (i, 0, 0, 0))
    o_shape = jax.ShapeDtypeStruct((b, NSA_KV_HEADS, nseg, HEAD_DIM), BF16)
    return pl.pallas_call(
        functools.partial(_compress_prompt_kernel, nseg=nseg),
        grid=(b,),
        in_specs=[pl.BlockSpec((None, t, LANES), lambda i: (i, 0, 0)), pl.BlockSpec((None, t, LANES), lambda i: (i, 0, 1)),
                  full(pe), full(w1), full(w2), full(kn)],
        out_specs=[o_spec, o_spec],
        out_shape=[o_shape, o_shape],
        scratch_shapes=[pltpu.VMEM((nseg + 8, CMP_HIDDEN), F32)],
        compiler_params=_cparams(("arbitrary",), VMEM_LIMIT_BYTES),
        name="nsa_compress",
    )(kvr, kvr, pe, w1, w2, kn)


def _stack_heads(qs_ref, kh):
    parts = [qs_ref[:, (kh * NSA_GROUP + g) * HEAD_DIM:(kh * NSA_GROUP + g + 1) * HEAD_DIM].astype(F32)
             for g in range(NSA_GROUP)]
    return jnp.concatenate(parts, axis=0).astype(BF16)


def _cmp_topk_kernel(qs_ref, ck_ref, cv_ref, farcol_ref, chi_ref, clo_ref, pool_ref, ocmp_ref, msk_ref,
                     *, tq, nseg, q_base, n_pick, n_blk):
    qi = pl.program_id(1)
    G = NSA_GROUP
    q0 = q_base + qi * tq
    nbase = q0 // CMP_STRIDE - 16
    place = (lax.broadcasted_iota(jnp.int32, (32, nseg), 1) - lax.broadcasted_iota(jnp.int32, (32, nseg), 0)) == nbase
    place = jnp.where(place, 1.0, 0.0).astype(BF16)
    t1 = q0 + lax.broadcasted_iota(jnp.int32, (tq, 1), 0)
    t4 = jnp.concatenate([t1] * G, axis=0)
    n_i = lax.broadcasted_iota(jnp.int32, (G * tq, nseg), 1)
    valid = (n_i * CMP_STRIDE + (CMP_BLOCK - 1) <= t4) & (n_i <= nseg - 2)
    blk = lax.broadcasted_iota(jnp.int32, (tq, N_SELBLK), 1)
    cur = lax.shift_right_logical(t1, 6)
    forced = (blk == 0) | (blk == cur) | (blk == cur - 1)
    for kh in range(NSA_KV_HEADS):
        q4 = _stack_heads(qs_ref, kh)
        s = _dot_nt(q4, ck_ref[kh]) + farcol_ref[kh]
        s = s + jnp.dot(chi_ref[kh], place, preferred_element_type=F32) + jnp.dot(clo_ref[kh], place, preferred_element_type=F32)
        s = jnp.where(valid, s, NEG)
        e = jnp.where(valid, jnp.exp(s - jnp.max(s, axis=-1, keepdims=True)), 0.0)
        p = e / jnp.maximum(jnp.sum(e, axis=-1, keepdims=True), 1e-30)
        o = jnp.dot(p.astype(BF16), cv_ref[kh], preferred_element_type=F32)
        for g in range(G):
            h = kh * G + g
            ocmp_ref[:, h * HEAD_DIM:(h + 1) * HEAD_DIM] = o[g * tq:(g + 1) * tq]
        imp = p[0:tq] + p[tq:2 * tq] + p[2 * tq:3 * tq] + p[3 * tq:4 * tq]
        pooled = _dot_f32(imp, pool_ref[...])
        score = jnp.where((blk > cur) | (blk >= n_blk), -1e30, jnp.where(forced, 1e30, pooled))
        sel = jnp.zeros((tq, N_SELBLK), F32)
        for _ in range(n_pick):
            mx = jnp.max(score, axis=-1, keepdims=True)
            first = jnp.min(jnp.where(score == mx, blk, N_SELBLK), axis=-1, keepdims=True)
            pick = blk == first
            sel = jnp.where(pick, 1.0, sel)
            score = jnp.where(pick, -3e38, score)
        msk_ref[kh] = sel


def _cmp_topk_t_kernel(qt_ref, ck_ref, cv_ref, farrow_ref, chit_ref, clot_ref, poolt_ref, ocmpt_ref, mskt_ref,
                       *, tq, nseg, n_pick, n_blk):
    qi = pl.program_id(1)
    G = NSA_GROUP
    q0 = qi * tq
    nbase = q0 // CMP_STRIDE - 16
    place_t = (lax.broadcasted_iota(jnp.int32, (nseg, 32), 0) - lax.broadcasted_iota(jnp.int32, (nseg, 32), 1)) == nbase
    place_t = jnp.where(place_t, 1.0, 0.0).astype(BF16)
    t1 = q0 + lax.broadcasted_iota(jnp.int32, (1, tq), 1)
    n_i = lax.broadcasted_iota(jnp.int32, (nseg, tq), 0)
    valid1 = (n_i * CMP_STRIDE + (CMP_BLOCK - 1) <= t1) & (n_i <= nseg - 2)
    valid = jnp.concatenate([valid1] * G, axis=1)
    blk = lax.broadcasted_iota(jnp.int32, (N_SELBLK, tq), 0)
    cur = lax.shift_right_logical(t1, 6)
    forced = (blk == 0) | (blk == cur) | (blk == cur - 1)
    for kh in range(NSA_KV_HEADS):
        qt4 = jnp.concatenate([qt_ref[(kh * G + g) * HEAD_DIM:(kh * G + g + 1) * HEAD_DIM, :] for g in range(G)], axis=1)
        s = jnp.dot(ck_ref[kh], qt4, preferred_element_type=F32) + farrow_ref[kh]
        s = s + jnp.dot(place_t, chit_ref[kh], preferred_element_type=F32) + jnp.dot(place_t, clot_ref[kh], preferred_element_type=F32)
        s = jnp.where(valid, s, NEG)
        e = jnp.where(valid, jnp.exp(s - jnp.max(s, axis=0, keepdims=True)), 0.0)
        p = e * (1.0 / jnp.maximum(jnp.sum(e, axis=0, keepdims=True), 1e-30))
        o_t = _dot_tn(cv_ref[kh], p.astype(BF16))
        for g in range(G):
            h = kh * G + g
            ocmpt_ref[h * HEAD_DIM:(h + 1) * HEAD_DIM, :] = o_t[:, g * tq:(g + 1) * tq]
        imp = p[:, 0:tq] + p[:, tq:2 * tq] + p[:, 2 * tq:3 * tq] + p[:, 3 * tq:4 * tq]
        pooled = _dot_f32(poolt_ref[...], imp)
        score = jnp.where((blk > cur) | (blk >= n_blk), -1e30, jnp.where(forced, 1e30, pooled))
        sel = jnp.zeros((N_SELBLK, tq), F32)
        for _ in range(n_pick):
            mx = jnp.max(score, axis=0, keepdims=True)
            first = jnp.min(jnp.where(score == mx, blk, N_SELBLK), axis=0, keepdims=True)
            pick = blk == first
            sel = jnp.where(pick, 1.0, sel)
            score = jnp.where(pick, -3e38, score)
        mskt_ref[kh] = sel


def _cmp_topk_prompt(nqst, ck, cv, tbl, b, t, tq, n_pick):
    nseg = ck.shape[2]
    assert tq <= 256 and ck.shape[0] == b
    farcol, chi, clo = _cmp_bias_tables(tbl, tq)
    farrow, chit, clot = (jnp.swapaxes(a, 1, 2) for a in (farcol, chi, clo))
    poolt = jnp.asarray((np.arange(N_SELBLK)[:, None] == np.arange(nseg)[None, :] // SEL_RATIO).astype(np.float32))
    full = lambda a: pl.BlockSpec(a.shape, lambda bi, i: (0,) * a.ndim)
    c_spec = pl.BlockSpec((None, NSA_KV_HEADS, nseg, HEAD_DIM), lambda bi, i: (bi, 0, 0, 0))
    nq = t // tq
    col_tile = pl.BlockSpec((NSA_WIDTH, tq), lambda bi, i: (0, bi * nq + i))
    return pl.pallas_call(
        functools.partial(_cmp_topk_t_kernel, tq=tq, nseg=nseg, n_pick=n_pick, n_blk=t // SEL_BLOCK),
        grid=(b, nq),
        in_specs=[col_tile, c_spec, c_spec, full(farrow), full(chit), full(clot), full(poolt)],
        out_specs=[col_tile, pl.BlockSpec((None, NSA_KV_HEADS, N_SELBLK, tq), lambda bi, i: (bi, 0, 0, i))],
        out_shape=[jax.ShapeDtypeStruct((NSA_WIDTH, b * t), F32),
                   jax.ShapeDtypeStruct((b, NSA_KV_HEADS, N_SELBLK, t), F32)],
        compiler_params=_cparams(("arbitrary", "arbitrary"), VMEM_LIMIT_BYTES),
        name="nsa_cmp_topk",
    )(nqst, ck, cv, farrow, chit, clot, poolt)


def _rel_bucket(dist):
    exact = NUM_BUCKETS // 2
    d = jnp.maximum(dist, 0)
    log_ratio = jnp.log(jnp.maximum(d, 1).astype(jnp.float32) / exact) / math.log(MAX_DISTANCE / exact)
    large = jnp.minimum(exact + (log_ratio * (NUM_BUCKETS - exact)).astype(jnp.int32), NUM_BUCKETS - 1)
    return jnp.where(d < exact, d, large)


def _bias_lookup(tbl, dist):
    onehot = _rel_bucket(dist)[..., None] == jnp.arange(NUM_BUCKETS)
    t = tbl.reshape((tbl.shape[0],) + (1,) * dist.ndim + (NUM_BUCKETS,))
    return jnp.sum(jnp.where(onehot[None], t, 0.0), axis=-1)


def _cmp_bias_tables(tbl, tq):
    tr = jnp.arange(tq)[:, None]
    i = jnp.arange(32)[None, :]
    dist = tr + 16 * CMP_STRIDE - CMP_STRIDE * i - (CMP_BLOCK - 1)
    near = _bias_lookup(tbl, dist)
    far = tbl[:, NUM_BUCKETS - 1]
    corr = (near - far[:, None, None]).reshape(NSA_KV_HEADS, NSA_GROUP * tq, 32)
    hi = corr.astype(BF16)
    lo = (corr - hi.astype(F32)).astype(BF16)
    farcol = jnp.broadcast_to(far[:, None, None], (NSA_HEADS, tq, 1)).reshape(NSA_KV_HEADS, NSA_GROUP * tq, 1)
    return farcol, hi, lo


def _cmp_topk_sample(qs, ck, cv, tbl, tq, q_base, n_pick, n_blk):
    b, t, _ = qs.shape
    nseg = ck.shape[2]
    assert tq <= 256
    farcol, chi, clo = _cmp_bias_tables(tbl, tq)
    pool = jnp.asarray((np.arange(nseg)[:, None] // SEL_RATIO == np.arange(N_SELBLK)[None, :]).astype(np.float32))
    full = lambda a: pl.BlockSpec(a.shape, lambda bi, i: (0,) * a.ndim)
    c_spec = pl.BlockSpec((None, NSA_KV_HEADS, nseg, HEAD_DIM), lambda bi, i: (bi, 0, 0, 0))
    m_spec = pl.BlockSpec((None, NSA_KV_HEADS, tq, N_SELBLK), lambda bi, i: (bi, 0, i, 0))
    m_shape = (b, NSA_KV_HEADS, t, N_SELBLK)
    return pl.pallas_call(
        functools.partial(_cmp_topk_kernel, tq=tq, nseg=nseg, q_base=q_base, n_pick=n_pick, n_blk=n_blk),
        grid=(b, t // tq),
        in_specs=[pl.BlockSpec((None, tq, NSA_WIDTH), lambda bi, i: (bi, i, 0)), c_spec, c_spec,
                  full(farcol), full(chi), full(clo), full(pool)],
        out_specs=[pl.BlockSpec((None, tq, NSA_WIDTH), lambda bi, i: (bi, i, 0)), m_spec],
        out_shape=[jax.ShapeDtypeStruct((b, t, NSA_WIDTH), F32), jax.ShapeDtypeStruct(m_shape, F32)],
        compiler_params=_cparams(("arbitrary", "arbitrary"), VMEM_LIMIT_BYTES),
        name="nsa_cmp_topk",
    )(qs, ck, cv, farcol, chi, clo, pool)


def _online_update(s, valid, v, m_ref, l_ref, acc_ref, v_t=False):
    s = jnp.where(valid, s, NEG)
    m_old = m_ref[...]
    m_new = jnp.maximum(m_old, jnp.max(s, axis=-1, keepdims=True))
    p = jnp.where(valid, jnp.exp(s - m_new), 0.0)
    alpha = jnp.exp(m_old - m_new)
    l_ref[...] = alpha * l_ref[...] + jnp.sum(p, axis=-1, keepdims=True)
    pb = p.astype(BF16)
    pv = _dot_nt(pb, v) if v_t else jnp.dot(pb, v, preferred_element_type=F32)
    acc_ref[...] = alpha * acc_ref[...] + pv
    m_ref[...] = m_new


def _selwin_kernel(qt_ref, selk_ref, selvt_ref, wink_ref, winvt_ref, mskt_ref, bias_ref, ocmpt_ref, gtt_ref, o_ref,
                   m_ref, l_ref, acc_ref, ot_ref, qt4_ref, *, tq):
    qi = pl.program_id(1)
    G = NSA_GROUP
    blocks_per_tile = tq // SEL_BLOCK
    s_rel = lax.broadcasted_iota(jnp.int32, (tq, tq), 0)
    t_rel = lax.broadcasted_iota(jnp.int32, (tq, tq), 1)
    causal = s_rel <= t_rel

    def init():
        m_ref[...] = jnp.full_like(m_ref, NEG)
        l_ref[...] = jnp.zeros_like(l_ref)
        acc_ref[...] = jnp.zeros_like(acc_ref)

    def update(kh, k, vt, bias_idx, valid):
        s = jnp.dot(k, qt4_ref[...], preferred_element_type=F32)
        if bias_idx is not None:
            s = s + bias_ref[kh, bias_idx]
        if valid is not None:
            neg = jnp.where(valid, 0.0, NEG)
            s = s + jnp.concatenate([neg] * G, axis=1)
        m_old = m_ref[0:1, :]
        m_new = jnp.maximum(m_old, jnp.max(s, axis=0, keepdims=True))
        p = jnp.exp(s - m_new)
        alpha = jnp.exp(m_old - m_new)
        l_ref[0:1, :] = alpha * l_ref[0:1, :] + jnp.sum(p, axis=0, keepdims=True)
        acc_ref[...] = alpha * acc_ref[...] + jnp.dot(vt, p.astype(BF16), preferred_element_type=F32)
        m_ref[0:1, :] = m_new

    def finish(kh, gate_row):
        for g in range(G):
            h = kh * G + g
            cols = slice(g * tq, (g + 1) * tq)
            rows = slice(h * HEAD_DIM, (h + 1) * HEAD_DIM)
            o = acc_ref[:, cols] / jnp.maximum(l_ref[0:1, cols], 1e-30)
            r = 3 * h + gate_row
            ot_ref[rows, :] += gtt_ref[r:r + 1, :] * o

    for h in range(NSA_HEADS):
        rows = slice(h * HEAD_DIM, (h + 1) * HEAD_DIM)
        ot_ref[rows, :] = gtt_ref[3 * h:3 * h + 1, :] * ocmpt_ref[rows, :]

    for kh in range(NSA_KV_HEADS):
        kcols = slice(kh * HEAD_DIM, (kh + 1) * HEAD_DIM)
        vrows = slice(kh * HEAD_DIM, (kh + 1) * HEAD_DIM)
        for g in range(G):
            h = kh * G + g
            qt4_ref[:, g * tq:(g + 1) * tq] = qt_ref[h * HEAD_DIM:(h + 1) * HEAD_DIM, :]

        def sel_valid(j):
            parts = [jnp.broadcast_to(mskt_ref[kh, pl.ds(j * blocks_per_tile + i, 1), :], (SEL_BLOCK, tq))
                     for i in range(blocks_per_tile)]
            return jnp.concatenate(parts, axis=0) > 0.5

        def sel_tile(j, bias_idx, extra):
            start = pl.multiple_of(j * tq, tq)
            valid = sel_valid(j)
            if extra is not None:
                valid = valid & extra
            update(kh, selk_ref[pl.ds(start, tq), kcols], selvt_ref[vrows, pl.ds(start, tq)], bias_idx, valid)

        def win_tile(j, bias_idx, valid):
            start = pl.multiple_of(j * tq, tq)
            update(kh, wink_ref[pl.ds(start, tq), kcols], winvt_ref[vrows, pl.ds(start, tq)], bias_idx, valid)

        init()

        def far_group(jg, c):
            start = pl.multiple_of(jg * (FAR_GROUP * tq), FAR_GROUP * tq)
            valid = jnp.concatenate([sel_valid(FAR_GROUP * jg + i) for i in range(FAR_GROUP)], axis=0)
            update(kh, selk_ref[pl.ds(start, FAR_GROUP * tq), kcols],
                   selvt_ref[vrows, pl.ds(start, FAR_GROUP * tq)], None, valid)
            return c

        def far_single(j, c):
            sel_tile(j, None, None)
            return c

        n_far = jnp.maximum(qi - 1, 0)
        n_grp = n_far // FAR_GROUP
        lax.fori_loop(0, n_grp, far_group, 0)
        lax.fori_loop(n_grp * FAR_GROUP, n_far, far_single, 0)

        @pl.when(qi >= 1)
        def _():
            sel_tile(qi - 1, 1, None)

        sel_tile(qi, 0, causal)
        finish(kh, 1)

        init()

        @pl.when(qi >= 2)
        def _():
            win_tile(qi - 2, None, s_rel >= t_rel)

        @pl.when(qi >= 1)
        def _():
            win_tile(qi - 1, 1, None)

        win_tile(qi, 0, causal)
        finish(kh, 2)

    o_ref[...] = ot_ref[...].T


def _selwin_bias_tables(tbl, tq):
    sr = jnp.arange(tq)[:, None]
    tr = jnp.arange(tq)[None, :]
    far = tbl[:, NUM_BUCKETS - 1][:, None, None]
    near0 = _bias_lookup(tbl, tr - sr) - far
    near1 = _bias_lookup(tbl, tr - sr + tq) - far
    b = jnp.stack([near0, near1], axis=1)
    b = b.reshape(NSA_KV_HEADS, NSA_GROUP, 2, tq, tq).transpose(0, 2, 3, 1, 4)
    return b.reshape(NSA_KV_HEADS, 2, tq, NSA_GROUP * tq)


def _selwin_prompt(nqst, selkv, selvt, winkv, winvt, mskt, tbl, ocmp, gtt, b, t, tq):
    assert tq >= NSA_WINDOW // 2 and tq >= MAX_DISTANCE and tq % SEL_BLOCK == 0
    bias = _selwin_bias_tables(tbl, tq)
    nq = t // tq
    col_tile = lambda h: pl.BlockSpec((h, tq), lambda bi, i: (0, bi * nq + i))
    row_tile = lambda w: pl.BlockSpec((tq, w), lambda bi, i: (bi * nq + i, 0))
    return pl.pallas_call(
        functools.partial(_selwin_kernel, tq=tq),
        grid=(b, nq),
        in_specs=[col_tile(NSA_WIDTH),
                  pl.BlockSpec((t, LANES), lambda bi, i: (bi, 0)), pl.BlockSpec((LANES, t), lambda bi, i: (0, bi)),
                  pl.BlockSpec((t, LANES), lambda bi, i: (bi, 0)), pl.BlockSpec((LANES, t), lambda bi, i: (0, bi)),
                  pl.BlockSpec((None, NSA_KV_HEADS, N_SELBLK, tq), lambda bi, i: (bi, 0, 0, i)),
                  pl.BlockSpec(bias.shape, lambda bi, i: (0, 0, 0, 0)),
                  col_tile(NSA_WIDTH), col_tile(32)],
        out_specs=row_tile(NSA_WIDTH),
        out_shape=jax.ShapeDtypeStruct((b * t, NSA_WIDTH), F32),
        scratch_shapes=[pltpu.VMEM((8, NSA_GROUP * tq), F32), pltpu.VMEM((8, NSA_GROUP * tq), F32),
                        pltpu.VMEM((HEAD_DIM, NSA_GROUP * tq), F32), pltpu.VMEM((NSA_WIDTH, tq), F32),
                        pltpu.VMEM((HEAD_DIM, NSA_GROUP * tq), BF16)],
        compiler_params=_cparams(("arbitrary", "arbitrary"), VMEM_LIMIT_BYTES),
        name="nsa_selwin",
    )(nqst, selkv, selvt, winkv, winvt, mskt, bias, ocmp, gtt)


PAGES_PER_STEP = 8
PAGE = 128


def _feature_major(cache):
    nd = cache.ndim
    t = jnp.transpose(cache, (0, 1) + tuple(range(3, nd)) + (2,))
    return t.reshape(cache.shape[0], cache.shape[1], -1, cache.shape[2])


def _page_specs(layer, rows, row_block):
    return [pl.BlockSpec((None, None, rows, PAGE), functools.partial(
        lambda bi, j, pt, r: (layer, pt[bi, j * PAGES_PER_STEP + r], row_block, 0), r=r)) for r in range(PAGES_PER_STEP)]


def _compress_sample_kernel(pt_ref, *refs, nseg):
    pages = refs[:PAGES_PER_STEP]
    pe_ref, w1_ref, w2_ref, kn_ref, ck_ref, cv_ref, srck_ref, srcv_ref, sh_ref = refs[PAGES_PER_STEP:]
    j = pl.program_id(1)
    for r, p_ref in enumerate(pages):
        rows = pl.ds(pl.multiple_of((j * PAGES_PER_STEP + r) * PAGE, PAGE), PAGE)
        srck_ref[rows, :] = p_ref[0:LANES, :].T
        srcv_ref[rows, :] = p_ref[LANES:2 * LANES, :].T

    @pl.when(j == pl.num_programs(1) - 1)
    def _():
        _compress_compute((srck_ref, srcv_ref), pe_ref, w1_ref, w2_ref, kn_ref, ck_ref, cv_ref, sh_ref, nseg)


def _compress_sample(cache_t, layer, page_table, pe, w1, w2, kn):
    sb, n_pages = page_table.shape
    past = n_pages * PAGE
    nseg = past // CMP_STRIDE
    full = lambda a: pl.BlockSpec(a.shape, lambda bi, j, pt: (0,) * a.ndim)
    o_spec = pl.BlockSpec((None, NSA_KV_HEADS, nseg, HEAD_DIM), lambda bi, j, pt: (bi, 0, 0, 0))
    o_shape = jax.ShapeDtypeStruct((sb, NSA_KV_HEADS, nseg, HEAD_DIM), BF16)
    return pl.pallas_call(
        functools.partial(_compress_sample_kernel, nseg=nseg),
        grid_spec=pltpu.PrefetchScalarGridSpec(
            num_scalar_prefetch=1, grid=(sb, n_pages // PAGES_PER_STEP),
            in_specs=_page_specs(layer, 256, 0) + [full(pe), full(w1), full(w2), full(kn)],
            out_specs=[o_spec, o_spec],
            scratch_shapes=[pltpu.VMEM((past, LANES), F32), pltpu.VMEM((past, LANES), F32),
                            pltpu.VMEM((nseg + 8, CMP_HIDDEN), F32)]),
        out_shape=[o_shape, o_shape],
        compiler_params=_cparams(("arbitrary", "arbitrary"), VMEM_LIMIT_BYTES),
        name="nsa_compress_sample",
    )(page_table, *([cache_t] * PAGES_PER_STEP), pe, w1, w2, kn)


def _selwin_sample_kernel(pt_ref, qs_ref, *refs):
    pages = refs[:PAGES_PER_STEP]
    (msk_ref, newkv_ref, winst_ref, newwr_ref, bsel_ref, bnew_ref, bwin_ref, expand_ref, ocmp_ref, gt_ref, o_ref,
     kv_ref, m_ref, l_ref, acc_ref, osel_ref, owin_ref) = refs[PAGES_PER_STEP:]
    j = pl.program_id(1)
    G = NSA_GROUP
    R = SAMPLE_ROWS
    for r, p_ref in enumerate(pages):
        kv_ref[:, pl.ds(pl.multiple_of((j * PAGES_PER_STEP + r) * PAGE, PAGE), PAGE)] = p_ref[...].astype(BF16)

    @pl.when(j == pl.num_programs(1) - 1)
    def _():
        rq = lax.broadcasted_iota(jnp.int32, (G * R, 1), 0) & (R - 1)
        new_valid = lax.broadcasted_iota(jnp.int32, (G * R, R), 1) <= rq
        win_valid = lax.broadcasted_iota(jnp.int32, (G * R, NSA_WINDOW), 1) >= rq

        def init():
            m_ref[...] = jnp.full_like(m_ref, NEG)
            l_ref[...] = jnp.zeros_like(l_ref)
            acc_ref[...] = jnp.zeros_like(acc_ref)

        def finish(dst_ref, kh):
            o = acc_ref[...] / jnp.maximum(l_ref[...], 1e-30)
            for g in range(G):
                h = kh * G + g
                dst_ref[:, h * HEAD_DIM:(h + 1) * HEAD_DIM] = o[g * R:(g + 1) * R]

        for kh in range(NSA_KV_HEADS):
            kcols = slice(kh * HEAD_DIM, (kh + 1) * HEAD_DIM)
            vcols = slice(128 + kh * HEAD_DIM, 128 + (kh + 1) * HEAD_DIM)
            q4 = _stack_heads(qs_ref, kh)
            mskb = msk_ref[kh].astype(BF16)

            init()
            mt = jnp.dot(mskb, expand_ref[...], preferred_element_type=F32) > 0.5
            valid = jnp.concatenate([mt] * G, axis=0)
            s = jnp.dot(q4, kv_ref[kcols, :], preferred_element_type=F32) + bsel_ref[kh]
            _online_update(s, valid, kv_ref[vcols, :], m_ref, l_ref, acc_ref, v_t=True)
            knew = newkv_ref[:, 256 + kh * HEAD_DIM:256 + (kh + 1) * HEAD_DIM].astype(BF16)
            vnew = newkv_ref[:, 384 + kh * HEAD_DIM:384 + (kh + 1) * HEAD_DIM].astype(BF16)
            _online_update(_dot_nt(q4, knew) + bnew_ref[kh], new_valid, vnew, m_ref, l_ref, acc_ref)
            finish(osel_ref, kh)

            init()
            kwin = winst_ref[kcols, :].astype(BF16)
            vwin = winst_ref[vcols, :].astype(BF16)
            _online_update(jnp.dot(q4, kwin, preferred_element_type=F32) + bwin_ref[kh], win_valid, vwin,
                           m_ref, l_ref, acc_ref, v_t=True)
            knew = newwr_ref[:, kcols].astype(BF16)
            vnew = newwr_ref[:, vcols].astype(BF16)
            _online_update(_dot_nt(q4, knew) + bnew_ref[kh], new_valid, vnew, m_ref, l_ref, acc_ref)
            finish(owin_ref, kh)

        for h in range(NSA_HEADS):
            cols = slice(h * HEAD_DIM, (h + 1) * HEAD_DIM)
            o_ref[:, cols] = (gt_ref[:, 3 * h:3 * h + 1] * ocmp_ref[:, cols]
                              + gt_ref[:, 3 * h + 1:3 * h + 2] * osel_ref[:, cols]
                              + gt_ref[:, 3 * h + 2:3 * h + 3] * owin_ref[:, cols])


def _sample_bias_tables(tbl, past):
    R = SAMPLE_ROWS
    r = jnp.arange(R)[:, None]
    stack = lambda a: a.reshape(NSA_KV_HEADS, NSA_GROUP * R, a.shape[-1])
    cached = _bias_lookup(tbl, past + r - jnp.arange(past)[None, :])
    new = _bias_lookup(tbl, r - jnp.arange(R)[None, :])
    win = _bias_lookup(tbl, NSA_WINDOW + r - jnp.arange(NSA_WINDOW)[None, :])
    return stack(cached), stack(new), stack(win)


def _selwin_sample(qs, cache_t, layer, page_table, msk, newkv, win_t, newwr, tbl, ocmp, gt):
    sb, n_pages = page_table.shape
    past = n_pages * PAGE
    assert win_t.shape[-1] == NSA_WINDOW and past >= NSA_WINDOW and past // SEL_BLOCK <= N_SELBLK
    bsel, bnew, bwin = _sample_bias_tables(tbl, past)
    expand = jnp.asarray((np.arange(N_SELBLK)[:, None] == np.arange(past)[None, :] // SEL_BLOCK).astype(np.float32),
                         dtype=BF16)
    R = SAMPLE_ROWS
    full = lambda a: pl.BlockSpec(a.shape, lambda bi, j, pt: (0,) * a.ndim)
    seq = lambda a: pl.BlockSpec((None,) + a.shape[1:], lambda bi, j, pt: (bi,) + (0,) * (a.ndim - 1))
    win_spec = pl.BlockSpec((None, None) + win_t.shape[2:], lambda bi, j, pt: (layer, bi, 0, 0))
    return pl.pallas_call(
        _selwin_sample_kernel,
        grid_spec=pltpu.PrefetchScalarGridSpec(
            num_scalar_prefetch=1, grid=(sb, n_pages // PAGES_PER_STEP),
            in_specs=[seq(qs)] + _page_specs(layer, 256, 1) + [seq(msk), seq(newkv), win_spec, seq(newwr),
                                                               full(bsel), full(bnew), full(bwin), full(expand),
                                                               seq(ocmp), seq(gt)],
            out_specs=pl.BlockSpec((None, R, NSA_WIDTH), lambda bi, j, pt: (bi, 0, 0)),
            scratch_shapes=[pltpu.VMEM((256, past), BF16),
                            pltpu.VMEM((NSA_GROUP * R, 1), F32), pltpu.VMEM((NSA_GROUP * R, 1), F32),
                            pltpu.VMEM((NSA_GROUP * R, HEAD_DIM), F32),
                            pltpu.VMEM((R, NSA_WIDTH), F32), pltpu.VMEM((R, NSA_WIDTH), F32)]),
        out_shape=jax.ShapeDtypeStruct((sb, R, NSA_WIDTH), F32),
        compiler_params=_cparams(("arbitrary", "arbitrary"), VMEM_LIMIT_BYTES),
        name="nsa_selwin_sample",
    )(page_table, qs, *([cache_t] * PAGES_PER_STEP), msk, newkv, win_t, newwr, bsel, bnew, bwin, expand, ocmp, gt)


def _fox_suffix_kernel(pt_ref, *refs):
    pages = refs[:PAGES_PER_STEP]
    sl_ref, d_ref, carry_ref = refs[PAGES_PER_STEP:]
    @pl.when(pl.program_id(1) == 0)
    def _():
        carry_ref[...] = jnp.zeros_like(carry_ref)

    carry = carry_ref[:, 0:1]
    for r, p_ref in reversed(list(enumerate(pages))):
        lf = p_ref[...]
        d_ref[:, r * PAGE:(r + 1) * PAGE] = _dot_f32(lf, sl_ref[...]) + carry
        carry = carry + jnp.sum(lf, axis=1, keepdims=True)
    carry_ref[...] = jnp.broadcast_to(carry, carry_ref.shape)


def _fox_suffix(logf_t, layer, page_table):
    sb, n_pages = page_table.shape
    n_steps = n_pages // PAGES_PER_STEP
    sl = jnp.asarray(np.tril(np.ones((PAGE, PAGE), np.float32), -1))
    specs = [pl.BlockSpec((None, None, FOX_HEADS, PAGE), functools.partial(
        lambda bi, j, pt, r: (layer, pt[bi, (n_steps - 1 - j) * PAGES_PER_STEP + r], 0, 0), r=r))
        for r in range(PAGES_PER_STEP)]
    return pl.pallas_call(
        _fox_suffix_kernel,
        grid_spec=pltpu.PrefetchScalarGridSpec(
            num_scalar_prefetch=1, grid=(sb, n_steps),
            in_specs=specs + [pl.BlockSpec(sl.shape, lambda bi, j, pt: (0, 0))],
            out_specs=pl.BlockSpec((None, FOX_HEADS, PAGES_PER_STEP * PAGE), lambda bi, j, pt: (bi, 0, n_steps - 1 - j)),
            scratch_shapes=[pltpu.VMEM((FOX_HEADS, LANES), F32)]),
        out_shape=jax.ShapeDtypeStruct((sb, FOX_HEADS, n_pages * PAGE), F32),
        compiler_params=_cparams(("arbitrary", "arbitrary")),
        name="fox_suffix",
    )(page_table, *([logf_t] * PAGES_PER_STEP), sl)


def _fox_sample_kernel(pt_ref, qs_ref, *refs):
    pages = refs[:PAGES_PER_STEP]
    dt_ref, newkv_ref, lfnew_ref, hmask_ref, o_ref, qbd_ref, crel_ref, m_ref, l_ref, acc_ref = refs[PAGES_PER_STEP:]
    j = pl.program_id(1)
    R = SAMPLE_ROWS
    H = FOX_HEADS
    tk = PAGES_PER_STEP * PAGE

    @pl.when(j == 0)
    def _():
        q = qs_ref[...].astype(F32)
        qbd_ref[...] = (jnp.concatenate([q] * H, axis=0) * hmask_ref[...]).astype(BF16)
        tri = jnp.where(lax.broadcasted_iota(jnp.int32, (R, R), 1) <= lax.broadcasted_iota(jnp.int32, (R, R), 0), 1.0, 0.0)
        crel = _dot_f32(tri, lfnew_ref[...])
        crel_ref[...] = jnp.concatenate([crel[:, h:h + 1] for h in range(H)], axis=0)
        m_ref[...] = jnp.full_like(m_ref, NEG)
        l_ref[...] = jnp.zeros_like(l_ref)
        acc_ref[...] = jnp.zeros_like(acc_ref)

    kt =jnp.concatenate([p[0:FOX_WIDTH, :] for p in pages], axis=1).astype(BF16)
    vt = jnp.concatenate([p[FOX_WIDTH:, :] for p in pages], axis=1).astype(BF16)
    drows = jnp.concatenate([jnp.broadcast_to(dt_ref[h:h + 1, :], (R, tk)) for h in range(H)], axis=0)
    s = jnp.dot(qbd_ref[...], kt, preferred_element_type=F32) + crel_ref[...] + drows
    _online_update(s, s > 2 * NEG, vt, m_ref, l_ref, acc_ref, v_t=True)

    @pl.when(j == pl.num_programs(1) - 1)
    def _():
        lf = lfnew_ref[...]
        iu = lax.broadcasted_iota(jnp.int32, (R, R), 0)
        ir = lax.broadcasted_iota(jnp.int32, (R, R), 1)
        a_le = jnp.where(ir <= iu, 1.0, 0.0)
        b_gt = jnp.where(iu > ir, 1.0, 0.0)
        dnew = jnp.concatenate([_dot_f32(a_le, lf[:, h:h + 1] * b_gt) for h in range(H)], axis=0)
        rq = lax.broadcasted_iota(jnp.int32, (H * R, 1), 0) & (R - 1)
        valid = lax.broadcasted_iota(jnp.int32, (H * R, R), 1) <= rq
        knew = newkv_ref[:, 0:FOX_WIDTH].astype(BF16)
        vnew = newkv_ref[:, FOX_WIDTH:].astype(BF16)
        _online_update(_dot_nt(qbd_ref[...], knew) + dnew, valid, vnew, m_ref, l_ref, acc_ref)
        o = (acc_ref[...] / jnp.maximum(l_ref[...], 1e-30)) * hmask_ref[...]
        out = o[0:R]
        for h in range(1, H):
            out = out + o[h * R:(h + 1) * R]
        o_ref[...] = out


def _fox_sample(qs, kv_t, layer, page_table, dt, newkv, lfnew):
    sb, n_pages = page_table.shape
    R = SAMPLE_ROWS
    tk = PAGES_PER_STEP * PAGE
    hmask = jnp.asarray((np.arange(FOX_HEADS * R)[:, None] // R == np.arange(FOX_WIDTH)[None, :] // HEAD_DIM)
                        .astype(np.float32))
    seq = lambda a: pl.BlockSpec((None,) + a.shape[1:], lambda bi, j, pt: (bi,) + (0,) * (a.ndim - 1))
    return pl.pallas_call(
        _fox_sample_kernel,
        grid_spec=pltpu.PrefetchScalarGridSpec(
            num_scalar_prefetch=1, grid=(sb, n_pages // PAGES_PER_STEP),
            in_specs=[seq(qs)] + _page_specs(layer, 2 * FOX_WIDTH, 0)
                     + [pl.BlockSpec((None, FOX_HEADS, tk), lambda bi, j, pt: (bi, 0, j)), seq(newkv), seq(lfnew),
                        pl.BlockSpec(hmask.shape, lambda bi, j, pt: (0, 0))],
            out_specs=pl.BlockSpec((None, R, FOX_WIDTH), lambda bi, j, pt: (bi, 0, 0)),
            scratch_shapes=[pltpu.VMEM((FOX_HEADS * R, FOX_WIDTH), BF16), pltpu.VMEM((FOX_HEADS * R, 1), F32),
                            pltpu.VMEM((FOX_HEADS * R, 1), F32), pltpu.VMEM((FOX_HEADS * R, 1), F32),
                            pltpu.VMEM((FOX_HEADS * R, FOX_WIDTH), F32)]),
        out_shape=jax.ShapeDtypeStruct((sb, R, FOX_WIDTH), F32),
        compiler_params=_cparams(("arbitrary", "arbitrary"), VMEM_LIMIT_BYTES),
        name="fox_sample",
    )(page_table, qs, *([kv_t] * PAGES_PER_STEP), dt, newkv, lfnew, hmask)


def _pad_cols(w, width):
    return jnp.pad(w, ((0, 0), (0, width - w.shape[1])))


def _prep_even_w(w):
    gq, gk, gv, glr, gog, nq, nkv, ng = _split(w, EVEN_SIZES)
    return jnp.concatenate([gq, gk, gv, gog, nq, nkv, _pad_cols(glr, LANES), _pad_cols(ng, LANES)], axis=1).astype(BF16)


def _prep_odd_w(w):
    fq, fk, fv, ff, cg = _split(w, ODD_SIZES)
    return jnp.concatenate([fq, fk, fv, cg, _pad_cols(ff, LANES)], axis=1).astype(BF16)


def _row(v):
    return v.reshape(1, -1).astype(F32)


def _tile_row(v, reps):
    return jnp.tile(v.astype(F32), reps).reshape(1, -1)


def kernel(x_prompt, x_sample, cache_nsa_kv, state_nsa_win, state_gla, cache_fox_kv, cache_fox_logf, state_conv, cache_mem_kv, page_table, mem_prompt, rel_bias, norm_mix, norm_xattn, norm_ffn, even_w_in, even_w_out, gla_w_gate, gla_b_gate, gla_out_norm, nsa_q_norm, nsa_k_norm, nsa_cmp_pe, nsa_cmp_w1, nsa_cmp_w2, odd_w_in, odd_w_out, fox_q_norm, fox_k_norm, fox_b_f, conv_w, conv_b, conv_ln_g, conv_ln_b, mem_norm, xa_wq, xa_wkv, xa_wo, xa_q_norm, xa_k_norm, ffn_w_in, ffn_w_out):
    B, T, _ = x_prompt.shape
    SB, SQ, _ = x_sample.shape
    depth = norm_mix.shape[0]
    past_len = page_table.shape[1] * cache_nsa_kv.shape[2]
    MP = B * T
    SR = SAMPLE_ROWS
    MS = SB * SR

    yp = x_prompt.reshape(MP, D_MODEL)
    ys = jnp.pad(x_sample, ((0, 0), (0, SR - SQ), (0, 0))).reshape(MS, D_MODEL)

    nsa_cache_t = _feature_major(cache_nsa_kv)
    fox_cache_t = _feature_major(cache_fox_kv)
    logf_t = _feature_major(cache_fox_logf)
    win_t = _feature_major(state_nsa_win)
    mem_cache_t = _feature_major(cache_mem_kv)

    nsa_kv_p, nsa_kv_s, win_p, win_s, gla_p, gla_s = [], [], [], [], [], []
    fox_kv_p, fox_kv_s, logf_p, logf_s, conv_p, conv_s, mem_kv_p = [], [], [], [], [], [], []

    for layer in range(depth):
        if layer % 2 == 0:
            e = layer // 2
            w_pad = _prep_even_w(even_w_in[e])
            wg_pad = jnp.pad(gla_w_gate[e], ((0, LANES - GLA_RANK), (0, 0))).astype(BF16)
            bg = _row(gla_b_gate[e])
            qn = _tile_row(nsa_q_norm[e], NSA_HEADS)
            kn1 = _tile_row(nsa_k_norm[e, 1], NSA_KV_HEADS)
            kn2 = _tile_row(nsa_k_norm[e, 2], NSA_KV_HEADS)
            gn = _row(gla_out_norm[e])
            w_out = even_w_out[e].astype(BF16)
            g_mix = _row(norm_mix[layer])
            pe = jnp.tile(nsa_cmp_pe[e].astype(F32), (1, 1, NSA_KV_HEADS))
            w1b = _compress_weights(nsa_cmp_w1[e])
            w2b = nsa_cmp_w2[e].astype(BF16)
            kn0 = _row(nsa_k_norm[e, 0])
            tbl = rel_bias.astype(F32).T
            (q, k, v, la, og, nqs, kvr, wr, gt, selkv, winkv, nqst, selvt, winvt, gtt) = _even_in(
                yp, g_mix, w_pad, wg_pad, bg, qn, kn1, kn2, 256)
            r3 = lambda a: a.reshape(B, T, a.shape[-1])
            s0t = jnp.zeros((B, GLA_HEADS, GLA_DV, GLA_DK), F32)
            o_gla, sfin_t = _gla(r3(q), r3(k), r3(v), r3(la), r3(og), gn, s0t, 256, B)
            kvr5 = kvr.reshape(B, T, 4, NSA_KV_HEADS, HEAD_DIM)
            wr5 = wr.reshape(B, T, 2, NSA_KV_HEADS, HEAD_DIM)
            ck, cv = _compress_prompt(r3(kvr), pe, w1b, w2b, kn0)
            ocmp_t, mskt = _cmp_topk_prompt(nqst, ck, cv, tbl, B, T, 256, SEL_TOPN)
            o_nsa = _selwin_prompt(nqst, selkv, selvt, winkv, winvt, mskt, tbl, ocmp_t, gtt, B, T, 256)
            yp = _out_proj(yp, o_gla.reshape(MP, GLA_WIDTH), o_nsa.reshape(MP, NSA_WIDTH),
                           w_out[:GLA_WIDTH], w_out[GLA_WIDTH:], 512)
            nsa_kv_p.append(kvr5)
            win_p.append(wr5[:, -min(NSA_WINDOW, T):])
            gla_p.append(jnp.swapaxes(sfin_t, -1, -2))
            (q, k, v, la, og, nqs, kvr, wr, gt) = _even_in(ys, g_mix, w_pad, wg_pad, bg, qn, kn1, kn2, MS)[:9]
            pad16 = lambda a: jnp.pad(a.reshape(SB, SR, a.shape[-1]), ((0, 0), (0, SUB - SR), (0, 0)))
            s0t = jnp.swapaxes(state_gla[e], -1, -2)
            o_gla, snew_t = _gla(pad16(q), pad16(k), pad16(v), pad16(la), pad16(og), gn, s0t, SUB, 4, n_valid=SQ)
            o_gla = o_gla[:, :SR]
            s3 = lambda a: a.reshape(SB, SR, a.shape[-1])
            kvr5 = kvr.reshape(SB, SR, 4, NSA_KV_HEADS, HEAD_DIM)[:, :SQ]
            wr5 = wr.reshape(SB, SR, 2, NSA_KV_HEADS, HEAD_DIM)[:, :SQ]
            ck, cv = _compress_sample(nsa_cache_t, e, page_table, pe, w1b, w2b, kn0)
            ocmp, msk = _cmp_topk_sample(s3(nqs), ck, cv, tbl, SR, past_len, SEL_TOPN - 1, past_len // SEL_BLOCK)
            o_nsa = _selwin_sample(s3(nqs), nsa_cache_t, e, page_table, msk, s3(kvr), win_t, s3(wr), tbl, ocmp, s3(gt))
            ys = _out_proj(ys, o_gla.reshape(MS, GLA_WIDTH), o_nsa.reshape(MS, NSA_WIDTH),
                           w_out[:GLA_WIDTH], w_out[GLA_WIDTH:], MS)
            nsa_kv_s.append(kvr5)
            win_s.append(jnp.concatenate([state_nsa_win[e][:, SQ:], wr5], axis=1))
            gla_s.append(jnp.swapaxes(snew_t, -1, -2))
        else:
            j = layer // 2
            w_pad = _prep_odd_w(odd_w_in[j])
            qn = _tile_row(fox_q_norm[j], FOX_HEADS)
            kn = _tile_row(fox_k_norm[j], FOX_HEADS)
            bf_pad = jnp.pad(fox_b_f[j].astype(F32), (0, LANES - FOX_HEADS)).reshape(1, LANES)
            w_out = odd_w_out[j].astype(BF16)
            g_mix = _row(norm_mix[layer])
            cw = conv_w[j].astype(F32)
            cb, cg_, cbeta = _row(conv_b[j]), _row(conv_ln_g[j]), _row(conv_ln_b[j])
            qs, kv, kb, lf, c, u, qst, vbt = _odd_in(yp.reshape(B, T, D_MODEL), g_mix, w_pad, qn, kn, bf_pad, 256)
            c8 = c[:, :, :FOX_HEADS]
            o_fox = _fox_prompt(qst, kb, vbt, c8, jnp.swapaxes(c8, 1, 2), 512, 1024)
            o_conv, cst = _conv(u, jnp.zeros((B, CONV_WIDTH - 1, CONV_CH), F32), cw, cb, cg_, cbeta, 512, 512)
            yp = _out_proj(yp, o_fox.reshape(MP, FOX_WIDTH), o_conv.reshape(MP, CONV_CH),
                           w_out[:FOX_WIDTH], w_out[FOX_WIDTH:], 512)
            fox_kv_p.append(kv.reshape(B, T, 2, FOX_HEADS, HEAD_DIM))
            logf_p.append(lf[:, :, :FOX_HEADS])
            conv_p.append(cst)
            qs, kv, kb, lf, c, u = _odd_in(ys.reshape(1, MS, D_MODEL), g_mix, w_pad, qn, kn, bf_pad, MS)[:6]
            s3 = lambda a: a.reshape(SB, SR, a.shape[-1])
            new_kv = kv.reshape(SB, SR, 2, FOX_HEADS, HEAD_DIM)[:, :SQ]
            lf_new = lf.reshape(SB, SR, LANES)[:, :SQ, :FOX_HEADS]
            dsuf = _fox_suffix(logf_t, j, page_table)
            o_fox = _fox_sample(s3(qs), fox_cache_t, j, page_table, dsuf, s3(kv), s3(lf))
            o_conv, cst = _conv(u.reshape(SB, SR, CONV_CH), state_conv[j], cw, cb, cg_, cbeta, SR, SQ)
            ys = _out_proj(ys, o_fox.reshape(MS, FOX_WIDTH), o_conv.reshape(MS, CONV_CH),
                           w_out[:FOX_WIDTH], w_out[FOX_WIDTH:], MS)
            fox_kv_s.append(new_kv)
            logf_s.append(lf_new)
            conv_s.append(cst)
        g_xa = _row(norm_xattn[layer])
        wq = xa_wq[layer].astype(BF16)
        wo = xa_wo[layer].astype(BF16)
        xqn = _tile_row(xa_q_norm[layer], XA_HEADS)
        mkv_t = _mem_kv(mem_prompt, _row(mem_norm[layer]), xa_wkv[layer].astype(BF16), _tile_row(xa_k_norm[layer], XA_HEADS))
        mem_kv_p.append(jnp.transpose(mkv_t.reshape(B, 2, XA_HEADS, HEAD_DIM, MEM_LEN), (0, 4, 1, 2, 3)))
        yp = _xattn(yp.reshape(B, T, D_MODEL), mkv_t[None], 0, g_xa, wq, wo, xqn, 512).reshape(MP, D_MODEL)
        ys = _xattn(ys.reshape(SB, SR, D_MODEL), mem_cache_t, layer, g_xa, wq, wo, xqn, SR).reshape(MS, D_MODEL)
        g_ffn = _row(norm_ffn[layer])
        w_in = ffn_w_in[layer].astype(BF16)
        w_o = ffn_w_out[layer].astype(BF16)
        yp = _ffn(yp, g_ffn, w_in, w_o, 512)
        ys = _ffn(ys, g_ffn, w_in, w_o, MS)

    yp = yp.reshape(B, T, D_MODEL)
    ys = ys.reshape(SB, SR, D_MODEL)[:, :SQ]
    return (yp, ys,
            jnp.stack(nsa_kv_p), jnp.stack(nsa_kv_s), jnp.stack(win_p), jnp.stack(win_s),
            jnp.stack(gla_p), jnp.stack(gla_s), jnp.stack(fox_kv_p), jnp.stack(fox_kv_s),
            jnp.stack(logf_p), jnp.stack(logf_s), jnp.stack(conv_p), jnp.stack(conv_s),
            jnp.stack(mem_kv_p))
```

```python
import functools
import math

import jax
import jax.numpy as jnp
import numpy as np
from jax import lax
from jax.experimental import pallas as pl
from jax.experimental.pallas import tpu as pltpu

F32 = jnp.float32
BF16 = jnp.bfloat16

D_MODEL = 1024
HEAD_DIM = 64
GLA_WIDTH = 512
GLA_HEADS = 4
GLA_DV = 128
GLA_DK = 64
GLA_RANK = 16
GLA_TAU = 16.0
NSA_WIDTH = 512
NSA_HEADS = 8
NSA_KV_HEADS = 2
NSA_GROUP = 4
CMP_STRIDE = 16
CMP_BLOCK = 32
CMP_HIDDEN = 256
SEL_BLOCK = 64
SEL_RATIO = 4
SEL_TOPN = 16
NSA_WINDOW = 512
FOX_WIDTH = 512
FOX_HEADS = 8
CONV_CH = 512
CONV_WIDTH = 31
MEM_LEN = 256
XA_HEADS = 4
XA_WIDTH = 256
FFN_HIDDEN = 2816
NUM_BUCKETS = 32
MAX_DISTANCE = 128
EPS = 1e-6
SCALE = HEAD_DIM ** -0.5
NEG = -1e30

EVEN_SIZES = (256, 256, 512, 16, 512, 512, 768, 24)
ODD_SIZES = (512, 512, 512, 8, 1024)

LANES = 128
VMEM_LIMIT_BYTES = 56 * 1024 * 1024
SAMPLE_ROWS = 8
SUB = 16
FAR_GROUP = 4
N_SELBLK = 128


def _cparams(sem, vmem=None):
    return pltpu.CompilerParams(dimension_semantics=sem, vmem_limit_bytes=vmem)


def _split(h, sizes):
    return jnp.split(h, np.cumsum(sizes)[:-1].tolist(), axis=-1)


def _rms_rows(x, g):
    return x * lax.rsqrt(jnp.mean(x * x, axis=-1, keepdims=True) + EPS) * g


def _group_rms(x, gmat, gs):
    x2 = x * x
    hi = x2.astype(BF16)
    lo = (x2 - hi.astype(F32)).astype(BF16)
    ms = (jnp.dot(hi, gmat, preferred_element_type=F32) + jnp.dot(lo, gmat, preferred_element_type=F32)) * (1.0 / gs)
    return x * lax.rsqrt(ms + EPS)


def _log_sigmoid(z):
    return -(jnp.maximum(-z, 0.0) + jnp.log1p(jnp.exp(-jnp.abs(z))))


def _sigmoid(z):
    return 1.0 / (1.0 + jnp.exp(-z))


def _dot_nt(a, b):
    return lax.dot_general(a, b, (((1,), (1,)), ((), ())), preferred_element_type=F32)


def _dot_tn(a, b):
    return lax.dot_general(a, b, (((0,), (0,)), ((), ())), preferred_element_type=F32)


def _dot_f32(a, b):
    return jnp.dot(a, b, preferred_element_type=F32, precision=lax.Precision.HIGHEST)


def _block_ones(width, gs):
    r = np.arange(width) // gs
    return jnp.asarray((r[:, None] == r[None, :]).astype(np.float32), dtype=BF16)


def _even_in_kernel(x_ref, g_ref, w_ref, wg_ref, bg_ref, qn_ref, kn1_ref, kn2_ref, gm512_ref, gm128_ref,
                    q_ref, k_ref, v_ref, la_ref, og_ref, nqs_ref, kvr_ref, wr_ref, gt_ref, selkv_ref, winkv_ref,
                    nqst_ref, selvt_ref, winvt_ref, gtt_ref):
    xb = _rms_rows(x_ref[...], g_ref[...]).astype(BF16)

    def proj(lo, hi):
        return jnp.dot(xb, w_ref[:, lo:hi], preferred_element_type=F32)

    q_ref[...] = proj(0, 256) * (GLA_DK ** -0.5)
    k_ref[...] = proj(256, 512)
    v_ref[...] = proj(512, 1024)
    og = proj(1024, 1536)
    og_ref[...] = og * _sigmoid(og)
    nq = _group_rms(proj(1536, 2048), gm512_ref[...], HEAD_DIM) * qn_ref[...] * SCALE
    nqs_ref[...] = nq.astype(BF16)
    nqst_ref[...] = nq.T.astype(BF16)
    kvr_ref[:, 0:256] = proj(2048, 2304)
    selk = _group_rms(proj(2304, 2432), gm128_ref[...], HEAD_DIM) * kn1_ref[...]
    selv = proj(2432, 2560)
    kvr_ref[:, 256:384] = selk
    kvr_ref[:, 384:512] = selv
    selkv_ref[:, 0:128] = selk.astype(BF16)
    selkv_ref[:, 128:256] = selv.astype(BF16)
    selvt_ref[...] = selv.T.astype(BF16)
    wink = _group_rms(proj(2560, 2688), gm128_ref[...], HEAD_DIM) * kn2_ref[...]
    winv = proj(2688, 2816)
    wr_ref[:, 0:128] = wink
    wr_ref[:, 128:256] = winv
    winkv_ref[:, 0:128] = wink.astype(BF16)
    winkv_ref[:, 128:256] = winv.astype(BF16)
    winvt_ref[...] = winv.T.astype(BF16)
    glr = proj(2816, 2944).astype(BF16)
    z = jnp.dot(glr, wg_ref[...], preferred_element_type=F32) + bg_ref[...]
    la_ref[...] = _log_sigmoid(z) * (1.0 / GLA_TAU)
    gates = _sigmoid(proj(2944, 3072))
    gt_ref[...] = gates
    gtt_ref[...] = gates.T[0:32, :]


def _even_in(x2d, g, w_pad, wg_pad, bg, qn, kn1, kn2, tm):
    m = x2d.shape[0]
    widths = (256, 256, 512, 256, 512, 512, 512, 256, 128, 256, 256)
    dtypes = (F32, F32, F32, F32, F32, BF16, F32, F32, F32, BF16, BF16)
    t_heights = (512, 128, 128, 32)
    t_dtypes = (BF16, BF16, BF16, F32)
    full = lambda a: pl.BlockSpec(a.shape, lambda i: (0,) * a.ndim)
    gm512 = _block_ones(512, HEAD_DIM)
    gm128 = _block_ones(128, HEAD_DIM)
    ins = (x2d, g, w_pad, wg_pad, bg, qn, kn1, kn2, gm512, gm128)
    return pl.pallas_call(
        _even_in_kernel,
        grid=(m // tm,),
        in_specs=[pl.BlockSpec((tm, D_MODEL), lambda i: (i, 0))] + [full(a) for a in ins[1:]],
        out_specs=[pl.BlockSpec((tm, w), lambda i: (i, 0)) for w in widths]
                  + [pl.BlockSpec((h, tm), lambda i: (0, i)) for h in t_heights],
        out_shape=[jax.ShapeDtypeStruct((m, w), d) for w, d in zip(widths, dtypes)]
                  + [jax.ShapeDtypeStruct((h, m), d) for h, d in zip(t_heights, t_dtypes)],
        compiler_params=_cparams(("arbitrary",), VMEM_LIMIT_BYTES),
        name="even_in",
    )(*ins)


def _odd_in_kernel(x_ref, g_ref, w_ref, qn_ref, kn_ref, bf_ref, gm512_ref, tri_ref,
                   qs_ref, kv_ref, kb_ref, lf_ref, c_ref, u_ref, qst_ref, vbt_ref, carry_ref):
    @pl.when(pl.program_id(1) == 0)
    def _():
        carry_ref[...] = jnp.zeros_like(carry_ref)

    xb = _rms_rows(x_ref[...], g_ref[...]).astype(BF16)

    def proj(lo, hi):
        return jnp.dot(xb, w_ref[:, lo:hi], preferred_element_type=F32)

    q = _group_rms(proj(0, 512), gm512_ref[...], HEAD_DIM) * qn_ref[...] * SCALE
    qs_ref[...] = q.astype(BF16)
    qst_ref[...] = q.T.astype(BF16)
    k = _group_rms(proj(512, 1024), gm512_ref[...], HEAD_DIM) * kn_ref[...]
    v = proj(1024, 1536)
    kv_ref[:, 0:512] = k
    kv_ref[:, 512:1024] = v
    kb_ref[...] = k.astype(BF16)
    vbt_ref[...] = v.T.astype(BF16)
    u_ref[...] = proj(1536, 2048) * _sigmoid(proj(2048, 2560))
    lf = _log_sigmoid(proj(2560, 2688) + bf_ref[...])
    lf_ref[...] = lf
    c = _dot_f32(tri_ref[...], lf) + carry_ref[0:1, :]
    c_ref[...] = c
    carry_ref[0:1, :] = c[-1:, :]


def _odd_in(x3d, g, w_pad, qn, kn, bf_pad, tm):
    b, t, _ = x3d.shape
    widths = (512, 1024, 512, 128, 128, 512)
    dtypes = (BF16, F32, BF16, F32, F32, F32)
    gm512 = _block_ones(512, HEAD_DIM)
    tri = jnp.asarray(np.tril(np.ones((tm, tm), np.float32)))
    ins = (x3d, g, w_pad, qn, kn, bf_pad, gm512, tri)
    full = lambda a: pl.BlockSpec(a.shape, lambda bi, i: (0,) * a.ndim)
    nt = t // tm
    t_spec = pl.BlockSpec((FOX_WIDTH, tm), lambda bi, i: (0, bi * nt + i))
    t_shape = jax.ShapeDtypeStruct((FOX_WIDTH, b * t), BF16)
    return pl.pallas_call(
        _odd_in_kernel,
        grid=(b, nt),
        in_specs=[pl.BlockSpec((None, tm, D_MODEL), lambda bi, i: (bi, i, 0))] + [full(a) for a in ins[1:]],
        out_specs=[pl.BlockSpec((None, tm, w), lambda bi, i: (bi, i, 0)) for w in widths] + [t_spec, t_spec],
        out_shape=[jax.ShapeDtypeStruct((b, t, w), d) for w, d in zip(widths, dtypes)] + [t_shape, t_shape],
        scratch_shapes=[pltpu.VMEM((8, 128), F32)],
        compiler_params=_cparams(("arbitrary", "arbitrary"), VMEM_LIMIT_BYTES),
        name="odd_in",
    )(*ins)


def _out_proj_kernel(res_ref, a1_ref, a2_ref, w1_ref, w2_ref, o_ref):
    acc = jnp.dot(a1_ref[...].astype(BF16), w1_ref[...], preferred_element_type=F32)
    acc = acc + jnp.dot(a2_ref[...].astype(BF16), w2_ref[...], preferred_element_type=F32)
    o_ref[...] = res_ref[...] + acc


def _out_proj(res, a1, a2, w1, w2, tm):
    m = res.shape[0]
    row = lambda a: pl.BlockSpec((tm, a.shape[1]), lambda i: (i, 0))
    full = lambda a: pl.BlockSpec(a.shape, lambda i: (0, 0))
    return pl.pallas_call(
        _out_proj_kernel,
        grid=(m // tm,),
        in_specs=[row(res), row(a1), row(a2), full(w1), full(w2)],
        out_specs=row(res),
        out_shape=jax.ShapeDtypeStruct(res.shape, F32),
        compiler_params=_cparams(("arbitrary",), VMEM_LIMIT_BYTES),
        name="out_proj",
    )(res, a1, a2, w1, w2)


def _ffn_kernel(x_ref, g_ref, wg_ref, wu_ref, wo_ref, o_ref, xn_ref, acc_ref):
    j = pl.program_id(1)

    @pl.when(j == 0)
    def _():
        xn_ref[...] = _rms_rows(x_ref[...], g_ref[...]).astype(BF16)
        acc_ref[...] = jnp.zeros_like(acc_ref)

    xb = xn_ref[...]
    gate = jnp.dot(xb, wg_ref[...], preferred_element_type=F32)
    up = jnp.dot(xb, wu_ref[...], preferred_element_type=F32)
    h = (gate * _sigmoid(gate) * up).astype(BF16)
    acc_ref[...] += jnp.dot(h, wo_ref[...], preferred_element_type=F32)

    @pl.when(j == pl.num_programs(1) - 1)
    def _():
        o_ref[...] = x_ref[...] + acc_ref[...]


def _ffn(x2d, g, w_in, w_out, tm, n_chunks=2):
    m = x2d.shape[0]
    th = FFN_HIDDEN // n_chunks
    return pl.pallas_call(
        _ffn_kernel,
        grid=(m // tm, n_chunks),
        in_specs=[pl.BlockSpec((tm, D_MODEL), lambda i, j: (i, 0)),
                  pl.BlockSpec((1, D_MODEL), lambda i, j: (0, 0)),
                  pl.BlockSpec((D_MODEL, th), lambda i, j: (0, j)),
                  pl.BlockSpec((D_MODEL, th), lambda i, j: (0, n_chunks + j)),
                  pl.BlockSpec((th, D_MODEL), lambda i, j: (j, 0))],
        out_specs=pl.BlockSpec((tm, D_MODEL), lambda i, j: (i, 0)),
        out_shape=jax.ShapeDtypeStruct(x2d.shape, F32),
        scratch_shapes=[pltpu.VMEM((tm, D_MODEL), BF16), pltpu.VMEM((tm, D_MODEL), F32)],
        compiler_params=_cparams(("arbitrary", "arbitrary"), VMEM_LIMIT_BYTES),
        name="ffn",
    )(x2d, g, w_in, w_in, w_out)


def _mem_kv_kernel(m_ref, g_ref, w_ref, kn_ref, gm_ref, o_ref):
    xb = _rms_rows(m_ref[...], g_ref[...]).astype(BF16)
    kv = jnp.dot(xb, w_ref[...], preferred_element_type=F32)
    o_ref[0:XA_WIDTH, :] = (_group_rms(kv[:, 0:XA_WIDTH], gm_ref[...], HEAD_DIM) * kn_ref[...]).T
    o_ref[XA_WIDTH:, :] = kv[:, XA_WIDTH:].T


def _mem_kv(mem, g, wkv, kn):
    b = mem.shape[0]
    gm = _block_ones(XA_WIDTH, HEAD_DIM)
    full = lambda a: pl.BlockSpec(a.shape, lambda i: (0,) * a.ndim)
    return pl.pallas_call(
        _mem_kv_kernel,
        grid=(b,),
        in_specs=[pl.BlockSpec((None, MEM_LEN, D_MODEL), lambda i: (i, 0, 0)), full(g), full(wkv), full(kn), full(gm)],
        out_specs=pl.BlockSpec((None, 2 * XA_WIDTH, MEM_LEN), lambda i: (i, 0, 0)),
        out_shape=jax.ShapeDtypeStruct((b, 2 * XA_WIDTH, MEM_LEN), F32),
        compiler_params=_cparams(("arbitrary",)),
        name="mem_kv",
    )(mem, g, wkv, kn, gm)


def _xattn_kernel(x_ref, mkv_ref, g_ref, wq_ref, wo_ref, qn_ref, gm_ref, o_ref):
    x = x_ref[...]
    xb = _rms_rows(x, g_ref[...]).astype(BF16)
    q = jnp.dot(xb, wq_ref[...], preferred_element_type=F32)
    q = _group_rms(q, gm_ref[...], HEAD_DIM) * qn_ref[...]
    qb = (q * SCALE).astype(BF16)
    outs = []
    for h in range(XA_HEADS):
        kt = mkv_ref[h * HEAD_DIM:(h + 1) * HEAD_DIM, :].astype(BF16)
        vt = mkv_ref[XA_WIDTH + h * HEAD_DIM:XA_WIDTH + (h + 1) * HEAD_DIM, :].astype(BF16)
        s = jnp.dot(qb[:, h * HEAD_DIM:(h + 1) * HEAD_DIM], kt, preferred_element_type=F32)
        e = jnp.exp(s - jnp.max(s, axis=-1, keepdims=True))
        p = (e / jnp.sum(e, axis=-1, keepdims=True)).astype(BF16)
        outs.append(_dot_nt(p, vt))
    o = jnp.concatenate(outs, axis=-1).astype(BF16)
    o_ref[...] = x + jnp.dot(o, wo_ref[...], preferred_element_type=F32)


def _xattn(x3d, mkv_t, layer, g, wq, wo, qn, tm):
    b, t, _ = x3d.shape
    gm = _block_ones(XA_WIDTH, HEAD_DIM)
    full = lambda a: pl.BlockSpec(a.shape, lambda bi, i: (0,) * a.ndim)
    return pl.pallas_call(
        _xattn_kernel,
        grid=(b, t // tm),
        in_specs=[pl.BlockSpec((None, tm, D_MODEL), lambda bi, i: (bi, i, 0)),
                  pl.BlockSpec((None, None, 2 * XA_WIDTH, MEM_LEN), lambda bi, i: (layer, bi, 0, 0)),
                  full(g), full(wq), full(wo), full(qn), full(gm)],
        out_specs=pl.BlockSpec((None, tm, D_MODEL), lambda bi, i: (bi, i, 0)),
        out_shape=jax.ShapeDtypeStruct(x3d.shape, F32),
        compiler_params=_cparams(("arbitrary", "arbitrary"), VMEM_LIMIT_BYTES),
        name="xattn",
    )(x3d, mkv_t, g, wq, wo, qn, gm)


def _gla_kernel(q_ref, k_ref, v_ref, la_ref, og_ref, gn_ref, s0_ref, tri_ref, hsel_ref,
                o_ref, sfin_ref, st_ref, *, n_sub, n_valid, bb):
    ti = pl.program_id(1)

    @pl.when(ti == 0)
    def _():
        st_ref[...] = s0_ref[...]

    tri = tri_ref[...]
    hsel = hsel_ref[...]
    gn = gn_ref[...]
    row = lax.broadcasted_iota(jnp.int32, (SUB, 1), 0)

    def sub_block(i, carry):
        for bi in range(bb):
            one_sequence(i, bi)
        return carry

    def one_sequence(i, bi):
        r0 = pl.multiple_of(i * SUB, SUB)
        rows = pl.ds(r0, SUB)
        q = q_ref[bi, rows, :]
        k = k_ref[bi, rows, :]
        v = v_ref[bi, rows, :]
        la = la_ref[bi, rows, :]
        if n_valid is not None:
            live = (row + r0) < n_valid
            la = jnp.where(live, la, 0.0)
            k = jnp.where(live, k, 0.0)
        b = _dot_f32(tri, la)
        b_end = b[SUB - 1:SUB, :]
        qd = (q * jnp.exp(b)).astype(BF16)
        kd = (k * jnp.exp(b_end - b)).astype(BF16)
        vb = v.astype(BF16)
        tiles = []
        for s in range(SUB):
            e = jnp.exp(jnp.minimum(b - b[s:s + 1, :], 0.0))
            z = (q * k[s:s + 1, :]) * e
            tiles.append(jnp.where(row >= s, z, 0.0))
        att = jnp.dot(jnp.concatenate(tiles, axis=0).astype(BF16), hsel, preferred_element_type=F32)
        dec = jnp.exp(b_end)
        outs = []
        for h in range(GLA_HEADS):
            dk = slice(h * GLA_DK, (h + 1) * GLA_DK)
            dv = slice(h * GLA_DV, (h + 1) * GLA_DV)
            st = st_ref[bi, h]
            o = _dot_nt(qd[:, dk], st.astype(BF16))
            for s in range(SUB):
                o = o + att[s * SUB:(s + 1) * SUB, h:h + 1] * v[s:s + 1, dv]
            st_ref[bi, h] = st * dec[:, dk] + _dot_tn(vb[:, dv], kd[:, dk])
            outs.append(_rms_rows(o, gn))
        o_ref[bi, rows, :] = jnp.concatenate(outs, axis=-1) * og_ref[bi, rows, :]

    lax.fori_loop(0, n_sub, sub_block, 0)

    @pl.when(ti == pl.num_programs(1) - 1)
    def _():
        sfin_ref[...] = st_ref[...]


def _gla(q, k, v, la, og, gn, s0t, tt, bb, n_valid=None):
    b, t, _ = q.shape
    tri = jnp.asarray(np.tril(np.ones((SUB, SUB), np.float32)))
    hsel = jnp.asarray((np.arange(256)[:, None] // GLA_DK == np.arange(128)[None, :]).astype(np.float32), dtype=BF16)
    seq = lambda w: pl.BlockSpec((bb, tt, w), lambda bi, i: (bi, i, 0))
    full = lambda a: pl.BlockSpec(a.shape, lambda bi, i: (0,) * a.ndim)
    st_spec = pl.BlockSpec((bb, GLA_HEADS, GLA_DV, GLA_DK), lambda bi, i: (bi, 0, 0, 0))
    return pl.pallas_call(
        functools.partial(_gla_kernel, n_sub=tt // SUB, n_valid=n_valid, bb=bb),
        grid=(b // bb, t // tt),
        in_specs=[seq(256), seq(256), seq(512), seq(256), seq(512), full(gn), st_spec, full(tri), full(hsel)],
        out_specs=[seq(512), st_spec],
        out_shape=[jax.ShapeDtypeStruct((b, t, GLA_WIDTH), F32),
                   jax.ShapeDtypeStruct((b, GLA_HEADS, GLA_DV, GLA_DK), F32)],
        scratch_shapes=[pltpu.VMEM((bb, GLA_HEADS, GLA_DV, GLA_DK), F32)],
        compiler_params=_cparams(("arbitrary", "arbitrary")),
        name="gla",
    )(q, k, v, la, og, gn, s0t, tri, hsel)


def _conv_kernel(u_ref, st0_ref, w_ref, b_ref, g_ref, beta_ref, o_ref, st_ref, ext_ref, *, tt, n_valid):
    ti = pl.program_id(1)
    ctx = CONV_WIDTH - 1

    @pl.when(ti == 0)
    def _():
        ext_ref[0:8, :] = jnp.zeros((8, CONV_CH), F32)
        ext_ref[pl.ds(2, ctx), :] = st0_ref[...]

    ext_ref[pl.ds(32, tt), :] = u_ref[...]
    acc = jnp.zeros((tt, CONV_CH), F32)
    for w in range(CONV_WIDTH):
        acc = acc + ext_ref[pl.ds(2 + w, tt), :] * w_ref[w:w + 1, :]
    y = acc + b_ref[...]
    mu = jnp.mean(y, axis=-1, keepdims=True)
    var = jnp.mean(jnp.square(y - mu), axis=-1, keepdims=True)
    ln = (y - mu) * lax.rsqrt(var + EPS) * g_ref[...] + beta_ref[...]
    o_ref[...] = ln * _sigmoid(ln)

    @pl.when(ti == pl.num_programs(1) - 1)
    def _():
        st_ref[...] = ext_ref[pl.ds(32 + n_valid - ctx, ctx), :]

    ext_ref[0:32, :] = ext_ref[pl.ds(tt, 32), :]


def _conv(u, st0, w, b, g, beta, tt, n_valid):
    bsz, t, _ = u.shape
    ctx = CONV_WIDTH - 1
    full = lambda a: pl.BlockSpec(a.shape, lambda bi, i: (0,) * a.ndim)
    st_spec = pl.BlockSpec((None, ctx, CONV_CH), lambda bi, i: (bi, 0, 0))
    return pl.pallas_call(
        functools.partial(_conv_kernel, tt=tt, n_valid=n_valid),
        grid=(bsz, t // tt),
        in_specs=[pl.BlockSpec((None, tt, CONV_CH), lambda bi, i: (bi, i, 0)), st_spec,
                  full(w), full(b), full(g), full(beta)],
        out_specs=[pl.BlockSpec((None, tt, CONV_CH), lambda bi, i: (bi, i, 0)), st_spec],
        out_shape=[jax.ShapeDtypeStruct(u.shape, F32), jax.ShapeDtypeStruct((bsz, ctx, CONV_CH), F32)],
        scratch_shapes=[pltpu.VMEM((32 + max(tt, 32), CONV_CH), F32)],
        compiler_params=_cparams(("arbitrary", "arbitrary")),
        name="conv",
    )(u, st0, w, b, g, beta)


def _fox_kernel(qt_ref, k_ref, vt_ref, ck_ref, cqt_ref, o_ref, m_ref, l_ref, acc_ref, *, tq, tk):
    qi = pl.program_id(1)
    ki = pl.program_id(2)
    last = (qi * tq + tq - 1) // tk

    @pl.when(ki == 0)
    def _():
        m_ref[...] = jnp.full_like(m_ref, NEG)
        l_ref[...] = jnp.zeros_like(l_ref)
        acc_ref[...] = jnp.zeros_like(acc_ref)

    def tile(neg):
        for h in range(FOX_HEADS):
            cols = slice(h * HEAD_DIM, (h + 1) * HEAD_DIM)
            s = jnp.dot(k_ref[:, cols], qt_ref[cols, :], preferred_element_type=F32)
            s = s + cqt_ref[h:h + 1, :] - ck_ref[:, h:h + 1]
            if neg is not None:
                s = s + neg
            m_old = m_ref[h:h + 1, :]
            m_new = jnp.maximum(m_old, jnp.max(s, axis=0, keepdims=True))
            p = jnp.exp(s - m_new)
            alpha = jnp.exp(m_old - m_new)
            l_ref[h:h + 1, :] = alpha * l_ref[h:h + 1, :] + jnp.sum(p, axis=0, keepdims=True)
            acc_ref[cols, :] = alpha * acc_ref[cols, :] + jnp.dot(vt_ref[cols, :], p.astype(BF16),
                                                                  preferred_element_type=F32)
            m_ref[h:h + 1, :] = m_new

    @pl.when(ki < last)
    def _():
        tile(None)

    @pl.when(ki == last)
    def _():
        s_pos = ki * tk + lax.broadcasted_iota(jnp.int32, (tk, tq), 0)
        t_pos = qi * tq + lax.broadcasted_iota(jnp.int32, (tk, tq), 1)
        tile(jnp.where(s_pos <= t_pos, 0.0, NEG))

    @pl.when(ki == pl.num_programs(2) - 1)
    def _():
        for h in range(FOX_HEADS):
            cols = slice(h * HEAD_DIM, (h + 1) * HEAD_DIM)
            acc_ref[cols, :] = acc_ref[cols, :] / jnp.maximum(l_ref[h:h + 1, :], 1e-30)
        o_ref[...] = acc_ref[...].T


def _fox_prompt(qst, kb, vbt, c8, ct, tq, tk):
    b, t, _ = kb.shape
    nq, nk = t // tq, t // tk
    kmin = lambda qi, ki: jnp.minimum(ki, (qi * tq + tq - 1) // tk)
    return pl.pallas_call(
        functools.partial(_fox_kernel, tq=tq, tk=tk),
        grid=(b, nq, nk),
        in_specs=[pl.BlockSpec((FOX_WIDTH, tq), lambda bi, qi, ki: (0, bi * nq + qi)),
                  pl.BlockSpec((None, tk, FOX_WIDTH), lambda bi, qi, ki: (bi, kmin(qi, ki), 0)),
                  pl.BlockSpec((FOX_WIDTH, tk), lambda bi, qi, ki: (0, bi * nk + kmin(qi, ki))),
                  pl.BlockSpec((None, tk, FOX_HEADS), lambda bi, qi, ki: (bi, kmin(qi, ki), 0)),
                  pl.BlockSpec((None, FOX_HEADS, tq), lambda bi, qi, ki: (bi, 0, qi))],
        out_specs=pl.BlockSpec((None, tq, FOX_WIDTH), lambda bi, qi, ki: (bi, qi, 0)),
        out_shape=jax.ShapeDtypeStruct((b, t, FOX_WIDTH), F32),
        scratch_shapes=[pltpu.VMEM((FOX_HEADS, tq), F32), pltpu.VMEM((FOX_HEADS, tq), F32),
                        pltpu.VMEM((FOX_WIDTH, tq), F32)],
        compiler_params=_cparams(("arbitrary", "arbitrary", "arbitrary"), VMEM_LIMIT_BYTES),
        name="fox_prompt",
    )(qst, kb, vbt, c8, ct)


def _compress_weights(w1):
    w = w1.reshape(2, 2, 8, 2, HEAD_DIM, CMP_HIDDEN)
    z = jnp.zeros_like(w)
    g0 = jnp.concatenate([w, z], axis=-1)
    g1 = jnp.concatenate([z, w], axis=-1)
    wbd = jnp.stack([g0, g1], axis=4)
    return wbd.reshape(2, 2, 8, 4 * HEAD_DIM, 2 * CMP_HIDDEN).astype(BF16)


def _compress_compute(src_refs, pe_ref, w1_ref, w2_ref, kn_ref, ck_ref, cv_ref, sh_ref, nseg):
    sh_ref[pl.ds(nseg, 8), :] = jnp.zeros((8, CMP_HIDDEN), F32)
    for j, src_ref in enumerate(src_refs):
        a = jnp.zeros((nseg, 2 * CMP_HIDDEN), F32)
        bm = jnp.zeros((nseg, 2 * CMP_HIDDEN), F32)
        for q in range(CMP_STRIDE // 2):
            p0, p1 = 2 * q, 2 * q + 1
            x0 = src_ref[pl.ds(p0, nseg, stride=CMP_STRIDE), :]
            x1 = src_ref[pl.ds(p1, nseg, stride=CMP_STRIDE), :]
            xa = jnp.concatenate([x0 + pe_ref[j, p0:p0 + 1, :], x1 + pe_ref[j, p1:p1 + 1, :]], axis=1)
            a = a + jnp.dot(xa.astype(BF16), w1_ref[j, 0, q], preferred_element_type=F32)
            p0, p1 = p0 + CMP_STRIDE, p1 + CMP_STRIDE
            xb = jnp.concatenate([x0 + pe_ref[j, p0:p0 + 1, :], x1 + pe_ref[j, p1:p1 + 1, :]], axis=1)
            bm = bm + jnp.dot(xb.astype(BF16), w1_ref[j, 1, q], preferred_element_type=F32)
        for g in range(NSA_KV_HEADS):
            sh_ref[pl.ds(0, nseg), :] = bm[:, g * CMP_HIDDEN:(g + 1) * CMP_HIDDEN]
            x = a[:, g * CMP_HIDDEN:(g + 1) * CMP_HIDDEN] + sh_ref[pl.ds(1, nseg), :]
            hid = x * (0.5 * (1.0 + jnp.tanh(math.sqrt(2.0 / math.pi) * (x + 0.044715 * (x * x * x)))))
            ckv = jnp.dot(hid.astype(BF16), w2_ref[j], preferred_element_type=F32)
            if j == 0:
                ck_ref[g] = _rms_rows(ckv, kn_ref[...]).astype(BF16)
            else:
                cv_ref[g] = ckv.astype(BF16)


def _compress_prompt_kernel(xk_ref, xv_ref, pe_ref, w1_ref, w2_ref, kn_ref, ck_ref, cv_ref, sh_ref, *, nseg):
    _compress_compute((xk_ref, xv_ref), pe_ref, w1_ref, w2_ref, kn_ref, ck_ref, cv_ref, sh_ref, nseg)


def _compress_prompt(kvr, pe, w1, w2, kn):
    b, t, _ = kvr.shape
    nseg = t // CMP_STRIDE
    full = lambda a: pl.BlockSpec(a.shape, lambda i: (0,) * a.ndim)
    o_spec = pl.BlockSpec((None, NSA_KV_HEADS, nseg, HEAD_DIM), lambda i: (i, 0, 0, 0))
    o_shape = jax.ShapeDtypeStruct((b, NSA_KV_HEADS, nseg, HEAD_DIM), BF16)
    return pl.pallas_call(
        functools.partial(_compress_prompt_kernel, nseg=nseg),
        grid=(b,),
        in_specs=[pl.BlockSpec((None, t, LANES), lambda i: (i, 0, 0)), pl.BlockSpec((None, t, LANES), lambda i: (i, 0, 1)),
                  full(pe), full(w1), full(w2), full(kn)],
        out_specs=[o_spec, o_spec],
        out_shape=[o_shape, o_shape],
        scratch_shapes=[pltpu.VMEM((nseg + 8, CMP_HIDDEN), F32)],
        compiler_params=_cparams(("arbitrary",), VMEM_LIMIT_BYTES),
        name="nsa_compress",
    )(kvr, kvr, pe, w1, w2, kn)


def _stack_heads(qs_ref, kh):
    parts = [qs_ref[:, (kh * NSA_GROUP + g) * HEAD_DIM:(kh * NSA_GROUP + g + 1) * HEAD_DIM].astype(F32)
             for g in range(NSA_GROUP)]
    return jnp.concatenate(parts, axis=0).astype(BF16)


def _cmp_topk_kernel(qs_ref, ck_ref, cv_ref, farcol_ref, chi_ref, clo_ref, pool_ref, ocmp_ref, msk_ref,
                     *, tq, nseg, q_base, n_pick, n_blk):
    qi = pl.program_id(1)
    G = NSA_GROUP
    q0 = q_base + qi * tq
    nbase = q0 // CMP_STRIDE - 16
    place = (lax.broadcasted_iota(jnp.int32, (32, nseg), 1) - lax.broadcasted_iota(jnp.int32, (32, nseg), 0)) == nbase
    place = jnp.where(place, 1.0, 0.0).astype(BF16)
    t1 = q0 + lax.broadcasted_iota(jnp.int32, (tq, 1), 0)
    t4 = jnp.concatenate([t1] * G, axis=0)
    n_i = lax.broadcasted_iota(jnp.int32, (G * tq, nseg), 1)
    valid = (n_i * CMP_STRIDE + (CMP_BLOCK - 1) <= t4) & (n_i <= nseg - 2)
    blk = lax.broadcasted_iota(jnp.int32, (tq, N_SELBLK), 1)
    cur = lax.shift_right_logical(t1, 6)
    forced = (blk == 0) | (blk == cur) | (blk == cur - 1)
    for kh in range(NSA_KV_HEADS):
        q4 = _stack_heads(qs_ref, kh)
        s = _dot_nt(q4, ck_ref[kh]) + farcol_ref[kh]
        s = s + jnp.dot(chi_ref[kh], place, preferred_element_type=F32) + jnp.dot(clo_ref[kh], place, preferred_element_type=F32)
        s = jnp.where(valid, s, NEG)
        e = jnp.where(valid, jnp.exp(s - jnp.max(s, axis=-1, keepdims=True)), 0.0)
        p = e / jnp.maximum(jnp.sum(e, axis=-1, keepdims=True), 1e-30)
        o = jnp.dot(p.astype(BF16), cv_ref[kh], preferred_element_type=F32)
        for g in range(G):
            h = kh * G + g
            ocmp_ref[:, h * HEAD_DIM:(h + 1) * HEAD_DIM] = o[g * tq:(g + 1) * tq]
        imp = p[0:tq] + p[tq:2 * tq] + p[2 * tq:3 * tq] + p[3 * tq:4 * tq]
        pooled = _dot_f32(imp, pool_ref[...])
        score = jnp.where((blk > cur) | (blk >= n_blk), -1e30, jnp.where(forced, 1e30, pooled))
        sel = jnp.zeros((tq, N_SELBLK), F32)
        for _ in range(n_pick):
            mx = jnp.max(score, axis=-1, keepdims=True)
            first = jnp.min(jnp.where(score == mx, blk, N_SELBLK), axis=-1, keepdims=True)
            pick = blk == first
            sel = jnp.where(pick, 1.0, sel)
            score = jnp.where(pick, -3e38, score)
        msk_ref[kh] = sel


def _cmp_topk_t_kernel(qt_ref, ck_ref, cv_ref, farrow_ref, chit_ref, clot_ref, poolt_ref, ocmpt_ref, mskt_ref,
                       *, tq, nseg, n_pick, n_blk):
    qi = pl.program_id(1)
    G = NSA_GROUP
    q0 = qi * tq
    nbase = q0 // CMP_STRIDE - 16
    place_t = (lax.broadcasted_iota(jnp.int32, (nseg, 32), 0) - lax.broadcasted_iota(jnp.int32, (nseg, 32), 1)) == nbase
    place_t = jnp.where(place_t, 1.0, 0.0).astype(BF16)
    t1 = q0 + lax.broadcasted_iota(jnp.int32, (1, tq), 1)
    n_i = lax.broadcasted_iota(jnp.int32, (nseg, tq), 0)
    valid1 = (n_i * CMP_STRIDE + (CMP_BLOCK - 1) <= t1) & (n_i <= nseg - 2)
    valid = jnp.concatenate([valid1] * G, axis=1)
    blk = lax.broadcasted_iota(jnp.int32, (N_SELBLK, tq), 0)
    cur = lax.shift_right_logical(t1, 6)
    forced = (blk == 0) | (blk == cur) | (blk == cur - 1)
    for kh in range(NSA_KV_HEADS):
        qt4 = jnp.concatenate([qt_ref[(kh * G + g) * HEAD_DIM:(kh * G + g + 1) * HEAD_DIM, :] for g in range(G)], axis=1)
        s = jnp.dot(ck_ref[kh], qt4, preferred_element_type=F32) + farrow_ref[kh]
        s = s + jnp.dot(place_t, chit_ref[kh], preferred_element_type=F32) + jnp.dot(place_t, clot_ref[kh], preferred_element_type=F32)
        s = jnp.where(valid, s, NEG)
        e = jnp.where(valid, jnp.exp(s - jnp.max(s, axis=0, keepdims=True)), 0.0)
        p = e * (1.0 / jnp.maximum(jnp.sum(e, axis=0, keepdims=True), 1e-30))
        o_t = _dot_tn(cv_ref[kh], p.astype(BF16))
        for g in range(G):
            h = kh * G + g
            ocmpt_ref[h * HEAD_DIM:(h + 1) * HEAD_DIM, :] = o_t[:, g * tq:(g + 1) * tq]
        imp = p[:, 0:tq] + p[:, tq:2 * tq] + p[:, 2 * tq:3 * tq] + p[:, 3 * tq:4 * tq]
        pooled = _dot_f32(poolt_ref[...], imp)
        score = jnp.where((blk > cur) | (blk >= n_blk), -1e30, jnp.where(forced, 1e30, pooled))
        sel = jnp.zeros((N_SELBLK, tq), F32)
        for _ in range(n_pick):
            mx = jnp.max(score, axis=0, keepdims=True)
            first = jnp.min(jnp.where(score == mx, blk, N_SELBLK), axis=0, keepdims=True)
            pick = blk == first
            sel = jnp.where(pick, 1.0, sel)
            score = jnp.where(pick, -3e38, score)
        mskt_ref[kh] = sel


def _cmp_topk_prompt(nqst, ck, cv, tbl, b, t, tq, n_pick):
    nseg = ck.shape[2]
    assert tq <= 256 and ck.shape[0] == b
    farcol, chi, clo = _cmp_bias_tables(tbl, tq)
    farrow, chit, clot = (jnp.swapaxes(a, 1, 2) for a in (farcol, chi, clo))
    poolt = jnp.asarray((np.arange(N_SELBLK)[:, None] == np.arange(nseg)[None, :] // SEL_RATIO).astype(np.float32))
    full = lambda a: pl.BlockSpec(a.shape, lambda bi, i: (0,) * a.ndim)
    c_spec = pl.BlockSpec((None, NSA_KV_HEADS, nseg, HEAD_DIM), lambda bi, i: (bi, 0, 0, 0))
    nq = t // tq
    col_tile = pl.BlockSpec((NSA_WIDTH, tq), lambda bi, i: (0, bi * nq + i))
    return pl.pallas_call(
        functools.partial(_cmp_topk_t_kernel, tq=tq, nseg=nseg, n_pick=n_pick, n_blk=t // SEL_BLOCK),
        grid=(b, nq),
        in_specs=[col_tile, c_spec, c_spec, full(farrow), full(chit), full(clot), full(poolt)],
        out_specs=[col_tile, pl.BlockSpec((None, NSA_KV_HEADS, N_SELBLK, tq), lambda bi, i: (bi, 0, 0, i))],
        out_shape=[jax.ShapeDtypeStruct((NSA_WIDTH, b * t), F32),
                   jax.ShapeDtypeStruct((b, NSA_KV_HEADS, N_SELBLK, t), F32)],
        compiler_params=_cparams(("arbitrary", "arbitrary"), VMEM_LIMIT_BYTES),
        name="nsa_cmp_topk",
    )(nqst, ck, cv, farrow, chit, clot, poolt)


def _rel_bucket(dist):
    exact = NUM_BUCKETS // 2
    d = jnp.maximum(dist, 0)
    log_ratio = jnp.log(jnp.maximum(d, 1).astype(jnp.float32) / exact) / math.log(MAX_DISTANCE / exact)
    large = jnp.minimum(exact + (log_ratio * (NUM_BUCKETS - exact)).astype(jnp.int32), NUM_BUCKETS - 1)
    return jnp.where(d < exact, d, large)


def _bias_lookup(tbl, dist):
    onehot = _rel_bucket(dist)[..., None] == jnp.arange(NUM_BUCKETS)
    t = tbl.reshape((tbl.shape[0],) + (1,) * dist.ndim + (NUM_BUCKETS,))
    return jnp.sum(jnp.where(onehot[None], t, 0.0), axis=-1)


def _cmp_bias_tables(tbl, tq):
    tr = jnp.arange(tq)[:, None]
    i = jnp.arange(32)[None, :]
    dist = tr + 16 * CMP_STRIDE - CMP_STRIDE * i - (CMP_BLOCK - 1)
    near = _bias_lookup(tbl, dist)
    far = tbl[:, NUM_BUCKETS - 1]
    corr = (near - far[:, None, None]).reshape(NSA_KV_HEADS, NSA_GROUP * tq, 32)
    hi = corr.astype(BF16)
    lo = (corr - hi.astype(F32)).astype(BF16)
    farcol = jnp.broadcast_to(far[:, None, None], (NSA_HEADS, tq, 1)).reshape(NSA_KV_HEADS, NSA_GROUP * tq, 1)
    return farcol, hi, lo


def _cmp_topk_sample(qs, ck, cv, tbl, tq, q_base, n_pick, n_blk):
    b, t, _ = qs.shape
    nseg = ck.shape[2]
    assert tq <= 256
    farcol, chi, clo = _cmp_bias_tables(tbl, tq)
    pool = jnp.asarray((np.arange(nseg)[:, None] // SEL_RATIO == np.arange(N_SELBLK)[None, :]).astype(np.float32))
    full = lambda a: pl.BlockSpec(a.shape, lambda bi, i: (0,) * a.ndim)
    c_spec = pl.BlockSpec((None, NSA_KV_HEADS, nseg, HEAD_DIM), lambda bi, i: (bi, 0, 0, 0))
    m_spec = pl.BlockSpec((None, NSA_KV_HEADS, tq, N_SELBLK), lambda bi, i: (bi, 0, i, 0))
    m_shape = (b, NSA_KV_HEADS, t, N_SELBLK)
    return pl.pallas_call(
        functools.partial(_cmp_topk_kernel, tq=tq, nseg=nseg, q_base=q_base, n_pick=n_pick, n_blk=n_blk),
        grid=(b, t // tq),
        in_specs=[pl.BlockSpec((None, tq, NSA_WIDTH), lambda bi, i: (bi, i, 0)), c_spec, c_spec,
                  full(farcol), full(chi), full(clo), full(pool)],
        out_specs=[pl.BlockSpec((None, tq, NSA_WIDTH), lambda bi, i: (bi, i, 0)), m_spec],
        out_shape=[jax.ShapeDtypeStruct((b, t, NSA_WIDTH), F32), jax.ShapeDtypeStruct(m_shape, F32)],
        compiler_params=_cparams(("arbitrary", "arbitrary"), VMEM_LIMIT_BYTES),
        name="nsa_cmp_topk",
    )(qs, ck, cv, farcol, chi, clo, pool)


def _online_update(s, valid, v, m_ref, l_ref, acc_ref, v_t=False):
    s = jnp.where(valid, s, NEG)
    m_old = m_ref[...]
    m_new = jnp.maximum(m_old, jnp.max(s, axis=-1, keepdims=True))
    p = jnp.where(valid, jnp.exp(s - m_new), 0.0)
    alpha = jnp.exp(m_old - m_new)
    l_ref[...] = alpha * l_ref[...] + jnp.sum(p, axis=-1, keepdims=True)
    pb = p.astype(BF16)
    pv = _dot_nt(pb, v) if v_t else jnp.dot(pb, v, preferred_element_type=F32)
    acc_ref[...] = alpha * acc_ref[...] + pv
    m_ref[...] = m_new


def _selwin_kernel(qt_ref, selk_ref, selvt_ref, wink_ref, winvt_ref, mskt_ref, bias_ref, ocmpt_ref, gtt_ref, o_ref,
                   m_ref, l_ref, acc_ref, ot_ref, qt4_ref, *, tq):
    qi = pl.program_id(1)
    G = NSA_GROUP
    blocks_per_tile = tq // SEL_BLOCK
    s_rel = lax.broadcasted_iota(jnp.int32, (tq, tq), 0)
    t_rel = lax.broadcasted_iota(jnp.int32, (tq, tq), 1)
    causal = s_rel <= t_rel
    all_rows, near_rows, diag_rows = slice(0, 3 * tq), slice(tq, 3 * tq), slice(2 * tq, 3 * tq)

    def init():
        m_ref[...] = jnp.full_like(m_ref, NEG)
        l_ref[...] = jnp.zeros_like(l_ref)
        acc_ref[...] = jnp.zeros_like(acc_ref)

    def update(kh, k, vt, bias_rows, valid):
        s = jnp.dot(k, qt4_ref[...], preferred_element_type=F32)
        if bias_rows is not None:
            s = s + bias_ref[kh, bias_rows, :]
        if valid is not None:
            neg = jnp.where(valid, 0.0, NEG)
            s = s + jnp.concatenate([neg] * G, axis=1)
        m_old = m_ref[0:1, :]
        m_new = jnp.maximum(m_old, jnp.max(s, axis=0, keepdims=True))
        p = jnp.exp(s - m_new)
        alpha = jnp.exp(m_old - m_new)
        l_ref[0:1, :] = alpha * l_ref[0:1, :] + jnp.sum(p, axis=0, keepdims=True)
        acc_ref[...] = alpha * acc_ref[...] + jnp.dot(vt, p.astype(BF16), preferred_element_type=F32)
        m_ref[0:1, :] = m_new

    def finish(kh, gate_row):
        for g in range(G):
            h = kh * G + g
            cols = slice(g * tq, (g + 1) * tq)
            rows = slice(h * HEAD_DIM, (h + 1) * HEAD_DIM)
            o = acc_ref[:, cols] / jnp.maximum(l_ref[0:1, cols], 1e-30)
            r = 3 * h + gate_row
            ot_ref[rows, :] += gtt_ref[r:r + 1, :] * o

    for h in range(NSA_HEADS):
        rows = slice(h * HEAD_DIM, (h + 1) * HEAD_DIM)
        ot_ref[rows, :] = gtt_ref[3 * h:3 * h + 1, :] * ocmpt_ref[rows, :]

    for kh in range(NSA_KV_HEADS):
        kcols = slice(kh * HEAD_DIM, (kh + 1) * HEAD_DIM)
        vrows = slice(kh * HEAD_DIM, (kh + 1) * HEAD_DIM)
        for g in range(G):
            h = kh * G + g
            qt4_ref[:, g * tq:(g + 1) * tq] = qt_ref[h * HEAD_DIM:(h + 1) * HEAD_DIM, :]

        def sel_valid(j):
            parts = [jnp.broadcast_to(mskt_ref[kh, pl.ds(j * blocks_per_tile + i, 1), :], (SEL_BLOCK, tq))
                     for i in range(blocks_per_tile)]
            return jnp.concatenate(parts, axis=0) > 0.5

        def sel_update(j0, n_tiles, bias_rows, last_is_diag):
            start = pl.multiple_of(j0 * tq, tq)
            parts = [sel_valid(j0 + i) for i in range(n_tiles)]
            if last_is_diag:
                parts[-1] = parts[-1] & causal
            update(kh, selk_ref[pl.ds(start, n_tiles * tq), kcols], selvt_ref[vrows, pl.ds(start, n_tiles * tq)],
                   bias_rows, jnp.concatenate(parts, axis=0))

        def win_update(j0, valid_parts, bias_rows):
            n = len(valid_parts) * tq
            start = pl.multiple_of(j0 * tq, tq)
            update(kh, wink_ref[pl.ds(start, n), kcols], winvt_ref[vrows, pl.ds(start, n)], bias_rows,
                   jnp.concatenate(valid_parts, axis=0))

        init()

        def far_group(jg, c):
            sel_update(jg * FAR_GROUP, FAR_GROUP, None, False)
            return c

        def far_single(j, c):
            sel_update(j, 1, None, False)
            return c

        n_far = jnp.maximum(qi - 1, 0)
        n_grp = n_far // FAR_GROUP
        lax.fori_loop(0, n_grp, far_group, 0)
        lax.fori_loop(n_grp * FAR_GROUP, n_far, far_single, 0)

        @pl.when(qi >= 1)
        def _():
            sel_update(qi - 1, 2, near_rows, True)

        @pl.when(qi == 0)
        def _():
            sel_update(0, 1, diag_rows, True)

        finish(kh, 1)

        init()
        all_valid = s_rel >= 0

        @pl.when(qi >= 2)
        def _():
            win_update(qi - 2, [s_rel >= t_rel, all_valid, causal], all_rows)

        @pl.when(qi == 1)
        def _():
            win_update(0, [all_valid, causal], near_rows)

        @pl.when(qi == 0)
        def _():
            win_update(0, [causal], diag_rows)

        finish(kh, 2)

    o_ref[...] = ot_ref[...].T


def _selwin_bias_tables(tbl, tq):
    sr = jnp.arange(tq)[:, None]
    tr = jnp.arange(tq)[None, :]
    far = tbl[:, NUM_BUCKETS - 1][:, None, None]
    near0 = _bias_lookup(tbl, tr - sr) - far
    near1 = _bias_lookup(tbl, tr - sr + tq) - far
    b = jnp.concatenate([jnp.zeros_like(near1), near1, near0], axis=1)
    b = b.reshape(NSA_KV_HEADS, NSA_GROUP, 3 * tq, tq).transpose(0, 2, 1, 3)
    return b.reshape(NSA_KV_HEADS, 3 * tq, NSA_GROUP * tq)


def _selwin_prompt(nqst, selkv, selvt, winkv, winvt, mskt, tbl, ocmp, gtt, b, t, tq):
    assert tq >= NSA_WINDOW // 2 and tq >= MAX_DISTANCE and tq % SEL_BLOCK == 0
    bias = _selwin_bias_tables(tbl, tq)
    nq = t // tq
    col_tile = lambda h: pl.BlockSpec((h, tq), lambda bi, i: (0, bi * nq + i))
    row_tile = lambda w: pl.BlockSpec((tq, w), lambda bi, i: (bi * nq + i, 0))
    return pl.pallas_call(
        functools.partial(_selwin_kernel, tq=tq),
        grid=(b, nq),
        in_specs=[col_tile(NSA_WIDTH),
                  pl.BlockSpec((t, LANES), lambda bi, i: (bi, 0)), pl.BlockSpec((LANES, t), lambda bi, i: (0, bi)),
                  pl.BlockSpec((t, LANES), lambda bi, i: (bi, 0)), pl.BlockSpec((LANES, t), lambda bi, i: (0, bi)),
                  pl.BlockSpec((None, NSA_KV_HEADS, N_SELBLK, tq), lambda bi, i: (bi, 0, 0, i)),
                  pl.BlockSpec(bias.shape, lambda bi, i: (0, 0, 0)),
                  col_tile(NSA_WIDTH), col_tile(32)],
        out_specs=row_tile(NSA_WIDTH),
        out_shape=jax.ShapeDtypeStruct((b * t, NSA_WIDTH), F32),
        scratch_shapes=[pltpu.VMEM((8, NSA_GROUP * tq), F32), pltpu.VMEM((8, NSA_GROUP * tq), F32),
                        pltpu.VMEM((HEAD_DIM, NSA_GROUP * tq), F32), pltpu.VMEM((NSA_WIDTH, tq), F32),
                        pltpu.VMEM((HEAD_DIM, NSA_GROUP * tq), BF16)],
        compiler_params=_cparams(("arbitrary", "arbitrary"), VMEM_LIMIT_BYTES),
        name="nsa_selwin",
    )(nqst, selkv, selvt, winkv, winvt, mskt, bias, ocmp, gtt)


PAGES_PER_STEP = 8
PAGE = 128


def _feature_major(cache):
    nd = cache.ndim
    t = jnp.transpose(cache, (0, 1) + tuple(range(3, nd)) + (2,))
    return t.reshape(cache.shape[0], cache.shape[1], -1, cache.shape[2])


def _page_specs(layer, rows, row_block, reverse_steps=None):
    group = (lambda j: j) if reverse_steps is None else (lambda j: reverse_steps - 1 - j)
    return [pl.BlockSpec((None, None, rows, PAGE), functools.partial(
        lambda bi, j, pt, r: (layer, pt[bi, group(j) * PAGES_PER_STEP + r], row_block, 0), r=r))
        for r in range(PAGES_PER_STEP)]


def _compress_sample_kernel(pt_ref, *refs, nseg):
    pages = refs[:PAGES_PER_STEP]
    pe_ref, w1_ref, w2_ref, kn_ref, ck_ref, cv_ref, srck_ref, srcv_ref, sh_ref = refs[PAGES_PER_STEP:]
    j = pl.program_id(1)
    for r, p_ref in enumerate(pages):
        rows = pl.ds(pl.multiple_of((j * PAGES_PER_STEP + r) * PAGE, PAGE), PAGE)
        srck_ref[rows, :] = p_ref[0:LANES, :].T
        srcv_ref[rows, :] = p_ref[LANES:2 * LANES, :].T

    @pl.when(j == pl.num_programs(1) - 1)
    def _():
        _compress_compute((srck_ref, srcv_ref), pe_ref, w1_ref, w2_ref, kn_ref, ck_ref, cv_ref, sh_ref, nseg)


def _compress_sample(cache_t, layer, page_table, pe, w1, w2, kn):
    sb, n_pages = page_table.shape
    past = n_pages * PAGE
    nseg = past // CMP_STRIDE
    full = lambda a: pl.BlockSpec(a.shape, lambda bi, j, pt: (0,) * a.ndim)
    o_spec = pl.BlockSpec((None, NSA_KV_HEADS, nseg, HEAD_DIM), lambda bi, j, pt: (bi, 0, 0, 0))
    o_shape = jax.ShapeDtypeStruct((sb, NSA_KV_HEADS, nseg, HEAD_DIM), BF16)
    return pl.pallas_call(
        functools.partial(_compress_sample_kernel, nseg=nseg),
        grid_spec=pltpu.PrefetchScalarGridSpec(
            num_scalar_prefetch=1, grid=(sb, n_pages // PAGES_PER_STEP),
            in_specs=_page_specs(layer, 256, 0) + [full(pe), full(w1), full(w2), full(kn)],
            out_specs=[o_spec, o_spec],
            scratch_shapes=[pltpu.VMEM((past, LANES), F32), pltpu.VMEM((past, LANES), F32),
                            pltpu.VMEM((nseg + 8, CMP_HIDDEN), F32)]),
        out_shape=[o_shape, o_shape],
        compiler_params=_cparams(("arbitrary", "arbitrary"), VMEM_LIMIT_BYTES),
        name="nsa_compress_sample",
    )(page_table, *([cache_t] * PAGES_PER_STEP), pe, w1, w2, kn)


def _selwin_sample_kernel(pt_ref, qs_ref, *refs):
    pages = refs[:PAGES_PER_STEP]
    (msk_ref, newkv_ref, winst_ref, newwr_ref, bsel_ref, bnew_ref, bwin_ref, expand_ref, ocmp_ref, gt_ref, o_ref,
     kv_ref, m_ref, l_ref, acc_ref, osel_ref, owin_ref) = refs[PAGES_PER_STEP:]
    j = pl.program_id(1)
    G = NSA_GROUP
    R = SAMPLE_ROWS
    for r, p_ref in enumerate(pages):
        kv_ref[:, pl.ds(pl.multiple_of((j * PAGES_PER_STEP + r) * PAGE, PAGE), PAGE)] = p_ref[...].astype(BF16)

    @pl.when(j == pl.num_programs(1) - 1)
    def _():
        rq = lax.broadcasted_iota(jnp.int32, (G * R, 1), 0) & (R - 1)
        new_valid = lax.broadcasted_iota(jnp.int32, (G * R, R), 1) <= rq
        win_valid = lax.broadcasted_iota(jnp.int32, (G * R, NSA_WINDOW), 1) >= rq

        def init():
            m_ref[...] = jnp.full_like(m_ref, NEG)
            l_ref[...] = jnp.zeros_like(l_ref)
            acc_ref[...] = jnp.zeros_like(acc_ref)

        def finish(dst_ref, kh):
            o = acc_ref[...] / jnp.maximum(l_ref[...], 1e-30)
            for g in range(G):
                h = kh * G + g
                dst_ref[:, h * HEAD_DIM:(h + 1) * HEAD_DIM] = o[g * R:(g + 1) * R]

        for kh in range(NSA_KV_HEADS):
            kcols = slice(kh * HEAD_DIM, (kh + 1) * HEAD_DIM)
            vcols = slice(128 + kh * HEAD_DIM, 128 + (kh + 1) * HEAD_DIM)
            q4 = _stack_heads(qs_ref, kh)
            mskb = msk_ref[kh].astype(BF16)

            init()
            mt = jnp.dot(mskb, expand_ref[...], preferred_element_type=F32) > 0.5
            valid = jnp.concatenate([mt] * G, axis=0)
            s = jnp.dot(q4, kv_ref[kcols, :], preferred_element_type=F32) + bsel_ref[kh]
            _online_update(s, valid, kv_ref[vcols, :], m_ref, l_ref, acc_ref, v_t=True)
            knew = newkv_ref[:, 256 + kh * HEAD_DIM:256 + (kh + 1) * HEAD_DIM].astype(BF16)
            vnew = newkv_ref[:, 384 + kh * HEAD_DIM:384 + (kh + 1) * HEAD_DIM].astype(BF16)
            _online_update(_dot_nt(q4, knew) + bnew_ref[kh], new_valid, vnew, m_ref, l_ref, acc_ref)
            finish(osel_ref, kh)

            init()
            kwin = winst_ref[kcols, :].astype(BF16)
            vwin = winst_ref[vcols, :].astype(BF16)
            _online_update(jnp.dot(q4, kwin, preferred_element_type=F32) + bwin_ref[kh], win_valid, vwin,
                           m_ref, l_ref, acc_ref, v_t=True)
            knew = newwr_ref[:, kcols].astype(BF16)
            vnew = newwr_ref[:, vcols].astype(BF16)
            _online_update(_dot_nt(q4, knew) + bnew_ref[kh], new_valid, vnew, m_ref, l_ref, acc_ref)
            finish(owin_ref, kh)

        for h in range(NSA_HEADS):
            cols = slice(h * HEAD_DIM, (h + 1) * HEAD_DIM)
            o_ref[:, cols] = (gt_ref[:, 3 * h:3 * h + 1] * ocmp_ref[:, cols]
                              + gt_ref[:, 3 * h + 1:3 * h + 2] * osel_ref[:, cols]
                              + gt_ref[:, 3 * h + 2:3 * h + 3] * owin_ref[:, cols])


def _sample_bias_tables(tbl, past):
    R = SAMPLE_ROWS
    r = jnp.arange(R)[:, None]
    stack = lambda a: a.reshape(NSA_KV_HEADS, NSA_GROUP * R, a.shape[-1])
    cached = _bias_lookup(tbl, past + r - jnp.arange(past)[None, :])
    new = _bias_lookup(tbl, r - jnp.arange(R)[None, :])
    win = _bias_lookup(tbl, NSA_WINDOW + r - jnp.arange(NSA_WINDOW)[None, :])
    return stack(cached), stack(new), stack(win)


def _selwin_sample(qs, cache_t, layer, page_table, msk, newkv, win_t, newwr, tbl, ocmp, gt):
    sb, n_pages = page_table.shape
    past = n_pages * PAGE
    assert win_t.shape[-1] == NSA_WINDOW and past >= NSA_WINDOW and past // SEL_BLOCK <= N_SELBLK
    bsel, bnew, bwin = _sample_bias_tables(tbl, past)
    expand = jnp.asarray((np.arange(N_SELBLK)[:, None] == np.arange(past)[None, :] // SEL_BLOCK).astype(np.float32),
                         dtype=BF16)
    R = SAMPLE_ROWS
    full = lambda a: pl.BlockSpec(a.shape, lambda bi, j, pt: (0,) * a.ndim)
    seq = lambda a: pl.BlockSpec((None,) + a.shape[1:], lambda bi, j, pt: (bi,) + (0,) * (a.ndim - 1))
    win_spec = pl.BlockSpec((None, None) + win_t.shape[2:], lambda bi, j, pt: (layer, bi, 0, 0))
    return pl.pallas_call(
        _selwin_sample_kernel,
        grid_spec=pltpu.PrefetchScalarGridSpec(
            num_scalar_prefetch=1, grid=(sb, n_pages // PAGES_PER_STEP),
            in_specs=[seq(qs)] + _page_specs(layer, 256, 1) + [seq(msk), seq(newkv), win_spec, seq(newwr),
                                                               full(bsel), full(bnew), full(bwin), full(expand),
                                                               seq(ocmp), seq(gt)],
            out_specs=pl.BlockSpec((None, R, NSA_WIDTH), lambda bi, j, pt: (bi, 0, 0)),
            scratch_shapes=[pltpu.VMEM((256, past), BF16),
                            pltpu.VMEM((NSA_GROUP * R, 1), F32), pltpu.VMEM((NSA_GROUP * R, 1), F32),
                            pltpu.VMEM((NSA_GROUP * R, HEAD_DIM), F32),
                            pltpu.VMEM((R, NSA_WIDTH), F32), pltpu.VMEM((R, NSA_WIDTH), F32)]),
        out_shape=jax.ShapeDtypeStruct((sb, R, NSA_WIDTH), F32),
        compiler_params=_cparams(("arbitrary", "arbitrary"), VMEM_LIMIT_BYTES),
        name="nsa_selwin_sample",
    )(page_table, qs, *([cache_t] * PAGES_PER_STEP), msk, newkv, win_t, newwr, bsel, bnew, bwin, expand, ocmp, gt)


def _fox_sample_kernel(pt_ref, qs_ref, *refs):
    P = PAGES_PER_STEP
    pages = refs[:P]
    lf_pages = refs[P:2 * P]
    (sl_ref, newkv_ref, lfnew_ref, hmask_ref, o_ref,
     qbd_ref, crel_ref, m_ref, l_ref, acc_ref, carry_ref) = refs[2 * P:]
    j = pl.program_id(1)
    R = SAMPLE_ROWS
    H = FOX_HEADS
    tk = P * PAGE

    @pl.when(j == 0)
    def _():
        carry_ref[...] = jnp.zeros_like(carry_ref)
        q = qs_ref[...].astype(F32)
        qbd_ref[...] = (jnp.concatenate([q] * H, axis=0) * hmask_ref[...]).astype(BF16)
        tri = jnp.where(lax.broadcasted_iota(jnp.int32, (R, R), 1) <= lax.broadcasted_iota(jnp.int32, (R, R), 0), 1.0, 0.0)
        crel = _dot_f32(tri, lfnew_ref[...])
        crel_ref[...] = jnp.concatenate([crel[:, h:h + 1] for h in range(H)], axis=0)
        m_ref[...] = jnp.full_like(m_ref, NEG)
        l_ref[...] = jnp.zeros_like(l_ref)
        acc_ref[...] = jnp.zeros_like(acc_ref)

    carry = carry_ref[:, 0:1]
    d_parts = [None] * P
    for r in reversed(range(P)):
        lf = lf_pages[r][...]
        d_parts[r] = _dot_f32(lf, sl_ref[...]) + carry
        carry = carry + jnp.sum(lf, axis=1, keepdims=True)
    carry_ref[...] = jnp.broadcast_to(carry, carry_ref.shape)
    dt = jnp.concatenate(d_parts, axis=1)

    kt = jnp.concatenate([p[0:FOX_WIDTH, :] for p in pages], axis=1).astype(BF16)
    vt = jnp.concatenate([p[FOX_WIDTH:, :] for p in pages], axis=1).astype(BF16)
    drows = jnp.concatenate([jnp.broadcast_to(dt[h:h + 1, :], (R, tk)) for h in range(H)], axis=0)
    s = jnp.dot(qbd_ref[...], kt, preferred_element_type=F32) + crel_ref[...] + drows
    _online_update(s, s > 2 * NEG, vt, m_ref, l_ref, acc_ref, v_t=True)

    @pl.when(j == pl.num_programs(1) - 1)
    def _():
        lf = lfnew_ref[...]
        iu = lax.broadcasted_iota(jnp.int32, (R, R), 0)
        ir = lax.broadcasted_iota(jnp.int32, (R, R), 1)
        a_le = jnp.where(ir <= iu, 1.0, 0.0)
        b_gt = jnp.where(iu > ir, 1.0, 0.0)
        dnew = jnp.concatenate([_dot_f32(a_le, lf[:, h:h + 1] * b_gt) for h in range(H)], axis=0)
        rq = lax.broadcasted_iota(jnp.int32, (H * R, 1), 0) & (R - 1)
        valid = lax.broadcasted_iota(jnp.int32, (H * R, R), 1) <= rq
        knew = newkv_ref[:, 0:FOX_WIDTH].astype(BF16)
        vnew = newkv_ref[:, FOX_WIDTH:].astype(BF16)
        _online_update(_dot_nt(qbd_ref[...], knew) + dnew, valid, vnew, m_ref, l_ref, acc_ref)
        o = (acc_ref[...] / jnp.maximum(l_ref[...], 1e-30)) * hmask_ref[...]
        out = o[0:R]
        for h in range(1, H):
            out = out + o[h * R:(h + 1) * R]
        o_ref[...] = out


def _fox_sample(qs, kv_t, logf_t, layer, page_table, newkv, lfnew):
    sb, n_pages = page_table.shape
    R = SAMPLE_ROWS
    n_steps = n_pages // PAGES_PER_STEP
    hmask = jnp.asarray((np.arange(FOX_HEADS * R)[:, None] // R == np.arange(FOX_WIDTH)[None, :] // HEAD_DIM)
                        .astype(np.float32))
    sl = jnp.asarray(np.tril(np.ones((PAGE, PAGE), np.float32), -1))
    seq = lambda a: pl.BlockSpec((None,) + a.shape[1:], lambda bi, j, pt: (bi,) + (0,) * (a.ndim - 1))
    full = lambda a: pl.BlockSpec(a.shape, lambda bi, j, pt: (0,) * a.ndim)
    return pl.pallas_call(
        _fox_sample_kernel,
        grid_spec=pltpu.PrefetchScalarGridSpec(
            num_scalar_prefetch=1, grid=(sb, n_steps),
            in_specs=[seq(qs)] + _page_specs(layer, 2 * FOX_WIDTH, 0, n_steps) + _page_specs(layer, FOX_HEADS, 0, n_steps)
                     + [full(sl), seq(newkv), seq(lfnew), full(hmask)],
            out_specs=pl.BlockSpec((None, R, FOX_WIDTH), lambda bi, j, pt: (bi, 0, 0)),
            scratch_shapes=[pltpu.VMEM((FOX_HEADS * R, FOX_WIDTH), BF16), pltpu.VMEM((FOX_HEADS * R, 1), F32),
                            pltpu.VMEM((FOX_HEADS * R, 1), F32), pltpu.VMEM((FOX_HEADS * R, 1), F32),
                            pltpu.VMEM((FOX_HEADS * R, FOX_WIDTH), F32), pltpu.VMEM((FOX_HEADS, LANES), F32)]),
        out_shape=jax.ShapeDtypeStruct((sb, R, FOX_WIDTH), F32),
        compiler_params=_cparams(("arbitrary", "arbitrary"), VMEM_LIMIT_BYTES),
        name="fox_sample",
    )(page_table, qs, *([kv_t] * PAGES_PER_STEP), *([logf_t] * PAGES_PER_STEP), sl, newkv, lfnew, hmask)


def _pad_cols(w, width):
    return jnp.pad(w, ((0, 0), (0, width - w.shape[1])))


def _prep_even_w(w):
    gq, gk, gv, glr, gog, nq, nkv, ng = _split(w, EVEN_SIZES)
    return jnp.concatenate([gq, gk, gv, gog, nq, nkv, _pad_cols(glr, LANES), _pad_cols(ng, LANES)], axis=1).astype(BF16)


def _prep_odd_w(w):
    fq, fk, fv, ff, cg = _split(w, ODD_SIZES)
    return jnp.concatenate([fq, fk, fv, cg, _pad_cols(ff, LANES)], axis=1).astype(BF16)


def _row(v):
    return v.reshape(1, -1).astype(F32)


def _tile_row(v, reps):
    return jnp.tile(v.astype(F32), reps).reshape(1, -1)


def kernel(x_prompt, x_sample, cache_nsa_kv, state_nsa_win, state_gla, cache_fox_kv, cache_fox_logf, state_conv, cache_mem_kv, page_table, mem_prompt, rel_bias, norm_mix, norm_xattn, norm_ffn, even_w_in, even_w_out, gla_w_gate, gla_b_gate, gla_out_norm, nsa_q_norm, nsa_k_norm, nsa_cmp_pe, nsa_cmp_w1, nsa_cmp_w2, odd_w_in, odd_w_out, fox_q_norm, fox_k_norm, fox_b_f, conv_w, conv_b, conv_ln_g, conv_ln_b, mem_norm, xa_wq, xa_wkv, xa_wo, xa_q_norm, xa_k_norm, ffn_w_in, ffn_w_out):
    B, T, _ = x_prompt.shape
    SB, SQ, _ = x_sample.shape
    depth = norm_mix.shape[0]
    past_len = page_table.shape[1] * cache_nsa_kv.shape[2]
    MP = B * T
    SR = SAMPLE_ROWS
    MS = SB * SR

    yp = x_prompt.reshape(MP, D_MODEL)
    ys = jnp.pad(x_sample, ((0, 0), (0, SR - SQ), (0, 0))).reshape(MS, D_MODEL)

    nsa_cache_t = _feature_major(cache_nsa_kv)
    fox_cache_t = _feature_major(cache_fox_kv)
    logf_t = _feature_major(cache_fox_logf)
    win_t = _feature_major(state_nsa_win)
    mem_cache_t = _feature_major(cache_mem_kv)

    nsa_kv_p, nsa_kv_s, win_p, win_s, gla_p, gla_s = [], [], [], [], [], []
    fox_kv_p, fox_kv_s, logf_p, logf_s, conv_p, conv_s, mem_kv_p = [], [], [], [], [], [], []

    for layer in range(depth):
        if layer % 2 == 0:
            e = layer // 2
            w_pad = _prep_even_w(even_w_in[e])
            wg_pad = jnp.pad(gla_w_gate[e], ((0, LANES - GLA_RANK), (0, 0))).astype(BF16)
            bg = _row(gla_b_gate[e])
            qn = _tile_row(nsa_q_norm[e], NSA_HEADS)
            kn1 = _tile_row(nsa_k_norm[e, 1], NSA_KV_HEADS)
            kn2 = _tile_row(nsa_k_norm[e, 2], NSA_KV_HEADS)
            gn = _row(gla_out_norm[e])
            w_out = even_w_out[e].astype(BF16)
            g_mix = _row(norm_mix[layer])
            pe = jnp.tile(nsa_cmp_pe[e].astype(F32), (1, 1, NSA_KV_HEADS))
            w1b = _compress_weights(nsa_cmp_w1[e])
            w2b = nsa_cmp_w2[e].astype(BF16)
            kn0 = _row(nsa_k_norm[e, 0])
            tbl = rel_bias.astype(F32).T
            (q, k, v, la, og, nqs, kvr, wr, gt, selkv, winkv, nqst, selvt, winvt, gtt) = _even_in(
                yp, g_mix, w_pad, wg_pad, bg, qn, kn1, kn2, 256)
            r3 = lambda a: a.reshape(B, T, a.shape[-1])
            s0t = jnp.zeros((B, GLA_HEADS, GLA_DV, GLA_DK), F32)
            o_gla, sfin_t = _gla(r3(q), r3(k), r3(v), r3(la), r3(og), gn, s0t, 256, B)
            kvr5 = kvr.reshape(B, T, 4, NSA_KV_HEADS, HEAD_DIM)
            wr5 = wr.reshape(B, T, 2, NSA_KV_HEADS, HEAD_DIM)
            ck, cv = _compress_prompt(r3(kvr), pe, w1b, w2b, kn0)
            ocmp_t, mskt = _cmp_topk_prompt(nqst, ck, cv, tbl, B, T, 256, SEL_TOPN)
            o_nsa = _selwin_prompt(nqst, selkv, selvt, winkv, winvt, mskt, tbl, ocmp_t, gtt, B, T, 256)
            yp = _out_proj(yp, o_gla.reshape(MP, GLA_WIDTH), o_nsa.reshape(MP, NSA_WIDTH),
                           w_out[:GLA_WIDTH], w_out[GLA_WIDTH:], 512)
            nsa_kv_p.append(kvr5)
            win_p.append(wr5[:, -min(NSA_WINDOW, T):])
            gla_p.append(jnp.swapaxes(sfin_t, -1, -2))
            (q, k, v, la, og, nqs, kvr, wr, gt) = _even_in(ys, g_mix, w_pad, wg_pad, bg, qn, kn1, kn2, MS)[:9]
            pad16 = lambda a: jnp.pad(a.reshape(SB, SR, a.shape[-1]), ((0, 0), (0, SUB - SR), (0, 0)))
            s0t = jnp.swapaxes(state_gla[e], -1, -2)
            o_gla, snew_t = _gla(pad16(q), pad16(k), pad16(v), pad16(la), pad16(og), gn, s0t, SUB, 4, n_valid=SQ)
            o_gla = o_gla[:, :SR]
            s3 = lambda a: a.reshape(SB, SR, a.shape[-1])
            kvr5 = kvr.reshape(SB, SR, 4, NSA_KV_HEADS, HEAD_DIM)[:, :SQ]
            wr5 = wr.reshape(SB, SR, 2, NSA_KV_HEADS, HEAD_DIM)[:, :SQ]
            ck, cv = _compress_sample(nsa_cache_t, e, page_table, pe, w1b, w2b, kn0)
            ocmp, msk = _cmp_topk_sample(s3(nqs), ck, cv, tbl, SR, past_len, SEL_TOPN - 1, past_len // SEL_BLOCK)
            o_nsa = _selwin_sample(s3(nqs), nsa_cache_t, e, page_table, msk, s3(kvr), win_t, s3(wr), tbl, ocmp, s3(gt))
            ys = _out_proj(ys, o_gla.reshape(MS, GLA_WIDTH), o_nsa.reshape(MS, NSA_WIDTH),
                           w_out[:GLA_WIDTH], w_out[GLA_WIDTH:], MS)
            nsa_kv_s.append(kvr5)
            win_s.append(jnp.concatenate([state_nsa_win[e][:, SQ:], wr5], axis=1))
            gla_s.append(jnp.swapaxes(snew_t, -1, -2))
        else:
            j = layer // 2
            w_pad = _prep_odd_w(odd_w_in[j])
            qn = _tile_row(fox_q_norm[j], FOX_HEADS)
            kn = _tile_row(fox_k_norm[j], FOX_HEADS)
            bf_pad = jnp.pad(fox_b_f[j].astype(F32), (0, LANES - FOX_HEADS)).reshape(1, LANES)
            w_out = odd_w_out[j].astype(BF16)
            g_mix = _row(norm_mix[layer])
            cw = conv_w[j].astype(F32)
            cb, cg_, cbeta = _row(conv_b[j]), _row(conv_ln_g[j]), _row(conv_ln_b[j])
            qs, kv, kb, lf, c, u, qst, vbt = _odd_in(yp.reshape(B, T, D_MODEL), g_mix, w_pad, qn, kn, bf_pad, 256)
            c8 = c[:, :, :FOX_HEADS]
            o_fox = _fox_prompt(qst, kb, vbt, c8, jnp.swapaxes(c8, 1, 2), 512, 1024)
            o_conv, cst = _conv(u, jnp.zeros((B, CONV_WIDTH - 1, CONV_CH), F32), cw, cb, cg_, cbeta, 512, 512)
            yp = _out_proj(yp, o_fox.reshape(MP, FOX_WIDTH), o_conv.reshape(MP, CONV_CH),
                           w_out[:FOX_WIDTH], w_out[FOX_WIDTH:], 512)
            fox_kv_p.append(kv.reshape(B, T, 2, FOX_HEADS, HEAD_DIM))
            logf_p.append(lf[:, :, :FOX_HEADS])
            conv_p.append(cst)
            qs, kv, kb, lf, c, u = _odd_in(ys.reshape(1, MS, D_MODEL), g_mix, w_pad, qn, kn, bf_pad, MS)[:6]
            s3 = lambda a: a.reshape(SB, SR, a.shape[-1])
            new_kv = kv.reshape(SB, SR, 2, FOX_HEADS, HEAD_DIM)[:, :SQ]
            lf_new = lf.reshape(SB, SR, LANES)[:, :SQ, :FOX_HEADS]
            o_fox = _fox_sample(s3(qs), fox_cache_t, logf_t, j, page_table, s3(kv), s3(lf))
            o_conv, cst = _conv(u.reshape(SB, SR, CONV_CH), state_conv[j], cw, cb, cg_, cbeta, SR, SQ)
            ys = _out_proj(ys, o_fox.reshape(MS, FOX_WIDTH), o_conv.reshape(MS, CONV_CH),
                           w_out[:FOX_WIDTH], w_out[FOX_WIDTH:], MS)
            fox_kv_s.append(new_kv)
            logf_s.append(lf_new)
            conv_s.append(cst)
        g_xa = _row(norm_xattn[layer])
        wq = xa_wq[layer].astype(BF16)
        wo = xa_wo[layer].astype(BF16)
        xqn = _tile_row(xa_q_norm[layer], XA_HEADS)
        mkv_t = _mem_kv(mem_prompt, _row(mem_norm[layer]), xa_wkv[layer].astype(BF16), _tile_row(xa_k_norm[layer], XA_HEADS))
        mem_kv_p.append(jnp.transpose(mkv_t.reshape(B, 2, XA_HEADS, HEAD_DIM, MEM_LEN), (0, 4, 1, 2, 3)))
        yp = _xattn(yp.reshape(B, T, D_MODEL), mkv_t[None], 0, g_xa, wq, wo, xqn, 512).reshape(MP, D_MODEL)
        ys = _xattn(ys.reshape(SB, SR, D_MODEL), mem_cache_t, layer, g_xa, wq, wo, xqn, SR).reshape(MS, D_MODEL)
        g_ffn = _row(norm_ffn[layer])
        w_in = ffn_w_in[layer].astype(BF16)
        w_o = ffn_w_out[layer].astype(BF16)
        yp = _ffn(yp, g_ffn, w_in, w_o, 512)
        ys = _ffn(ys, g_ffn, w_in, w_o, MS)

    yp = yp.reshape(B, T, D_MODEL)
    ys = ys.reshape(SB, SR, D_MODEL)[:, :SQ]
    return (yp, ys,
            jnp.stack(nsa_kv_p), jnp.stack(nsa_kv_s), jnp.stack(win_p), jnp.stack(win_s),
            jnp.stack(gla_p), jnp.stack(gla_s), jnp.stack(fox_kv_p), jnp.stack(fox_kv_s),
            jnp.stack(logf_p), jnp.stack(logf_s), jnp.stack(conv_p), jnp.stack(conv_s),
            jnp.stack(mem_kv_p))
```

```python
import functools
import math

import jax
import jax.numpy as jnp
import numpy as np
from jax import lax
from jax.experimental import pallas as pl
from jax.experimental.pallas import tpu as pltpu

F32 = jnp.float32
BF16 = jnp.bfloat16

D_MODEL = 1024
HEAD_DIM = 64
GLA_WIDTH = 512
GLA_HEADS = 4
GLA_DV = 128
GLA_DK = 64
GLA_RANK = 16
GLA_TAU = 16.0
NSA_WIDTH = 512
NSA_HEADS = 8
NSA_KV_HEADS = 2
NSA_GROUP = 4
CMP_STRIDE = 16
CMP_BLOCK = 32
CMP_HIDDEN = 256
SEL_BLOCK = 64
SEL_RATIO = 4
SEL_TOPN = 16
NSA_WINDOW = 512
FOX_WIDTH = 512
FOX_HEADS = 8
CONV_CH = 512
CONV_WIDTH = 31
MEM_LEN = 256
XA_HEADS = 4
XA_WIDTH = 256
FFN_HIDDEN = 2816
NUM_BUCKETS = 32
MAX_DISTANCE = 128
EPS = 1e-6
SCALE = HEAD_DIM ** -0.5
NEG = -1e30

EVEN_SIZES = (256, 256, 512, 16, 512, 512, 768, 24)
ODD_SIZES = (512, 512, 512, 8, 1024)

LANES = 128
VMEM_LIMIT_BYTES = 56 * 1024 * 1024
SAMPLE_ROWS = 8
SUB = 16
FAR_GROUP = 4
N_SELBLK = 128


def _cparams(sem, vmem=None):
    return pltpu.CompilerParams(dimension_semantics=sem, vmem_limit_bytes=vmem)


def _split(h, sizes):
    return jnp.split(h, np.cumsum(sizes)[:-1].tolist(), axis=-1)


def _rms_rows(x, g):
    return x * lax.rsqrt(jnp.mean(x * x, axis=-1, keepdims=True) + EPS) * g


def _group_rms(x, gmat, gs):
    x2 = x * x
    hi = x2.astype(BF16)
    lo = (x2 - hi.astype(F32)).astype(BF16)
    ms = (jnp.dot(hi, gmat, preferred_element_type=F32) + jnp.dot(lo, gmat, preferred_element_type=F32)) * (1.0 / gs)
    return x * lax.rsqrt(ms + EPS)


def _log_sigmoid(z):
    return -(jnp.maximum(-z, 0.0) + jnp.log1p(jnp.exp(-jnp.abs(z))))


def _sigmoid(z):
    return 1.0 / (1.0 + jnp.exp(-z))


def _dot_nt(a, b):
    return lax.dot_general(a, b, (((1,), (1,)), ((), ())), preferred_element_type=F32)


def _dot_tn(a, b):
    return lax.dot_general(a, b, (((0,), (0,)), ((), ())), preferred_element_type=F32)


def _dot_f32(a, b):
    return jnp.dot(a, b, preferred_element_type=F32, precision=lax.Precision.HIGHEST)


def _block_ones(width, gs):
    r = np.arange(width) // gs
    return jnp.asarray((r[:, None] == r[None, :]).astype(np.float32), dtype=BF16)


def _even_in_kernel(x_ref, g_ref, w_ref, wg_ref, bg_ref, qn_ref, kn1_ref, kn2_ref, gm512_ref, gm128_ref,
                    q_ref, k_ref, v_ref, la_ref, og_ref, nqs_ref, kvr_ref, wr_ref, gt_ref, selkv_ref, winkv_ref,
                    nqst_ref, selvt_ref, winvt_ref, gtt_ref):
    xb = _rms_rows(x_ref[...], g_ref[...]).astype(BF16)

    def proj(lo, hi):
        return jnp.dot(xb, w_ref[:, lo:hi], preferred_element_type=F32)

    q_ref[...] = proj(0, 256) * (GLA_DK ** -0.5)
    k_ref[...] = proj(256, 512)
    v_ref[...] = proj(512, 1024)
    og = proj(1024, 1536)
    og_ref[...] = og * _sigmoid(og)
    nq = _group_rms(proj(1536, 2048), gm512_ref[...], HEAD_DIM) * qn_ref[...] * SCALE
    nqs_ref[...] = nq.astype(BF16)
    nqst_ref[...] = nq.T.astype(BF16)
    kvr_ref[:, 0:256] = proj(2048, 2304)
    selk = _group_rms(proj(2304, 2432), gm128_ref[...], HEAD_DIM) * kn1_ref[...]
    selv = proj(2432, 2560)
    kvr_ref[:, 256:384] = selk
    kvr_ref[:, 384:512] = selv
    selkv_ref[:, 0:128] = selk.astype(BF16)
    selkv_ref[:, 128:256] = selv.astype(BF16)
    selvt_ref[...] = selv.T.astype(BF16)
    wink = _group_rms(proj(2560, 2688), gm128_ref[...], HEAD_DIM) * kn2_ref[...]
    winv = proj(2688, 2816)
    wr_ref[:, 0:128] = wink
    wr_ref[:, 128:256] = winv
    winkv_ref[:, 0:128] = wink.astype(BF16)
    winkv_ref[:, 128:256] = winv.astype(BF16)
    winvt_ref[...] = winv.T.astype(BF16)
    glr = proj(2816, 2944).astype(BF16)
    z = jnp.dot(glr, wg_ref[...], preferred_element_type=F32) + bg_ref[...]
    la_ref[...] = _log_sigmoid(z) * (1.0 / GLA_TAU)
    gates = _sigmoid(proj(2944, 3072))
    gt_ref[...] = gates
    gtt_ref[...] = gates.T[0:32, :]


def _even_in(x2d, g, w_pad, wg_pad, bg, qn, kn1, kn2, tm):
    m = x2d.shape[0]
    widths = (256, 256, 512, 256, 512, 512, 512, 256, 128, 256, 256)
    dtypes = (F32, F32, F32, F32, F32, BF16, F32, F32, F32, BF16, BF16)
    t_heights = (512, 128, 128, 32)
    t_dtypes = (BF16, BF16, BF16, F32)
    full = lambda a: pl.BlockSpec(a.shape, lambda i: (0,) * a.ndim)
    gm512 = _block_ones(512, HEAD_DIM)
    gm128 = _block_ones(128, HEAD_DIM)
    ins = (x2d, g, w_pad, wg_pad, bg, qn, kn1, kn2, gm512, gm128)
    return pl.pallas_call(
        _even_in_kernel,
        grid=(m // tm,),
        in_specs=[pl.BlockSpec((tm, D_MODEL), lambda i: (i, 0))] + [full(a) for a in ins[1:]],
        out_specs=[pl.BlockSpec((tm, w), lambda i: (i, 0)) for w in widths]
                  + [pl.BlockSpec((h, tm), lambda i: (0, i)) for h in t_heights],
        out_shape=[jax.ShapeDtypeStruct((m, w), d) for w, d in zip(widths, dtypes)]
                  + [jax.ShapeDtypeStruct((h, m), d) for h, d in zip(t_heights, t_dtypes)],
        compiler_params=_cparams(("arbitrary",), VMEM_LIMIT_BYTES),
        name="even_in",
    )(*ins)


def _odd_in_kernel(x_ref, g_ref, w_ref, qn_ref, kn_ref, bf_ref, gm512_ref, tri_ref,
                   qs_ref, kv_ref, kb_ref, lf_ref, c_ref, u_ref, qst_ref, vbt_ref, carry_ref):
    @pl.when(pl.program_id(1) == 0)
    def _():
        carry_ref[...] = jnp.zeros_like(carry_ref)

    xb = _rms_rows(x_ref[...], g_ref[...]).astype(BF16)

    def proj(lo, hi):
        return jnp.dot(xb, w_ref[:, lo:hi], preferred_element_type=F32)

    q = _group_rms(proj(0, 512), gm512_ref[...], HEAD_DIM) * qn_ref[...] * SCALE
    qs_ref[...] = q.astype(BF16)
    qst_ref[...] = q.T.astype(BF16)
    k = _group_rms(proj(512, 1024), gm512_ref[...], HEAD_DIM) * kn_ref[...]
    v = proj(1024, 1536)
    kv_ref[:, 0:512] = k
    kv_ref[:, 512:1024] = v
    kb_ref[...] = k.astype(BF16)
    vbt_ref[...] = v.T.astype(BF16)
    u_ref[...] = proj(1536, 2048) * _sigmoid(proj(2048, 2560))
    lf = _log_sigmoid(proj(2560, 2688) + bf_ref[...])
    lf_ref[...] = lf
    c = _dot_f32(tri_ref[...], lf) + carry_ref[0:1, :]
    c_ref[...] = c
    carry_ref[0:1, :] = c[-1:, :]


def _odd_in(x3d, g, w_pad, qn, kn, bf_pad, tm):
    b, t, _ = x3d.shape
    widths = (512, 1024, 512, 128, 128, 512)
    dtypes = (BF16, F32, BF16, F32, F32, F32)
    gm512 = _block_ones(512, HEAD_DIM)
    tri = jnp.asarray(np.tril(np.ones((tm, tm), np.float32)))
    ins = (x3d, g, w_pad, qn, kn, bf_pad, gm512, tri)
    full = lambda a: pl.BlockSpec(a.shape, lambda bi, i: (0,) * a.ndim)
    nt = t // tm
    t_spec = pl.BlockSpec((FOX_WIDTH, tm), lambda bi, i: (0, bi * nt + i))
    t_shape = jax.ShapeDtypeStruct((FOX_WIDTH, b * t), BF16)
    return pl.pallas_call(
        _odd_in_kernel,
        grid=(b, nt),
        in_specs=[pl.BlockSpec((None, tm, D_MODEL), lambda bi, i: (bi, i, 0))] + [full(a) for a in ins[1:]],
        out_specs=[pl.BlockSpec((None, tm, w), lambda bi, i: (bi, i, 0)) for w in widths] + [t_spec, t_spec],
        out_shape=[jax.ShapeDtypeStruct((b, t, w), d) for w, d in zip(widths, dtypes)] + [t_shape, t_shape],
        scratch_shapes=[pltpu.VMEM((8, 128), F32)],
        compiler_params=_cparams(("arbitrary", "arbitrary"), VMEM_LIMIT_BYTES),
        name="odd_in",
    )(*ins)


def _out_proj_kernel(res_ref, a1_ref, a2_ref, w1_ref, w2_ref, o_ref):
    acc = jnp.dot(a1_ref[...].astype(BF16), w1_ref[...], preferred_element_type=F32)
    acc = acc + jnp.dot(a2_ref[...].astype(BF16), w2_ref[...], preferred_element_type=F32)
    o_ref[...] = res_ref[...] + acc


def _out_proj(res, a1, a2, w1, w2, tm):
    m = res.shape[0]
    row = lambda a: pl.BlockSpec((tm, a.shape[1]), lambda i: (i, 0))
    full = lambda a: pl.BlockSpec(a.shape, lambda i: (0, 0))
    return pl.pallas_call(
        _out_proj_kernel,
        grid=(m // tm,),
        in_specs=[row(res), row(a1), row(a2), full(w1), full(w2)],
        out_specs=row(res),
        out_shape=jax.ShapeDtypeStruct(res.shape, F32),
        compiler_params=_cparams(("arbitrary",), VMEM_LIMIT_BYTES),
        name="out_proj",
    )(res, a1, a2, w1, w2)


def _ffn_kernel(x_ref, g_ref, wg_ref, wu_ref, wo_ref, o_ref, xn_ref, acc_ref):
    j = pl.program_id(1)

    @pl.when(j == 0)
    def _():
        xn_ref[...] = _rms_rows(x_ref[...], g_ref[...]).astype(BF16)
        acc_ref[...] = jnp.zeros_like(acc_ref)

    xb = xn_ref[...]
    gate = jnp.dot(xb, wg_ref[...], preferred_element_type=F32)
    up = jnp.dot(xb, wu_ref[...], preferred_element_type=F32)
    h = (gate * _sigmoid(gate) * up).astype(BF16)
    acc_ref[...] += jnp.dot(h, wo_ref[...], preferred_element_type=F32)

    @pl.when(j == pl.num_programs(1) - 1)
    def _():
        o_ref[...] = x_ref[...] + acc_ref[...]


def _ffn(x2d, g, w_in, w_out, tm, n_chunks=2):
    m = x2d.shape[0]
    th = FFN_HIDDEN // n_chunks
    return pl.pallas_call(
        _ffn_kernel,
        grid=(m // tm, n_chunks),
        in_specs=[pl.BlockSpec((tm, D_MODEL), lambda i, j: (i, 0)),
                  pl.BlockSpec((1, D_MODEL), lambda i, j: (0, 0)),
                  pl.BlockSpec((D_MODEL, th), lambda i, j: (0, j)),
                  pl.BlockSpec((D_MODEL, th), lambda i, j: (0, n_chunks + j)),
                  pl.BlockSpec((th, D_MODEL), lambda i, j: (j, 0))],
        out_specs=pl.BlockSpec((tm, D_MODEL), lambda i, j: (i, 0)),
        out_shape=jax.ShapeDtypeStruct(x2d.shape, F32),
        scratch_shapes=[pltpu.VMEM((tm, D_MODEL), BF16), pltpu.VMEM((tm, D_MODEL), F32)],
        compiler_params=_cparams(("arbitrary", "arbitrary"), VMEM_LIMIT_BYTES),
        name="ffn",
    )(x2d, g, w_in, w_in, w_out)


def _mem_kv_kernel(m_ref, g_ref, w_ref, kn_ref, gm_ref, o_ref):
    xb = _rms_rows(m_ref[...], g_ref[...]).astype(BF16)
    kv = jnp.dot(xb, w_ref[...], preferred_element_type=F32)
    o_ref[0:XA_WIDTH, :] = (_group_rms(kv[:, 0:XA_WIDTH], gm_ref[...], HEAD_DIM) * kn_ref[...]).T
    o_ref[XA_WIDTH:, :] = kv[:, XA_WIDTH:].T


def _mem_kv(mem, g, wkv, kn):
    b = mem.shape[0]
    gm = _block_ones(XA_WIDTH, HEAD_DIM)
    full = lambda a: pl.BlockSpec(a.shape, lambda i: (0,) * a.ndim)
    return pl.pallas_call(
        _mem_kv_kernel,
        grid=(b,),
        in_specs=[pl.BlockSpec((None, MEM_LEN, D_MODEL), lambda i: (i, 0, 0)), full(g), full(wkv), full(kn), full(gm)],
        out_specs=pl.BlockSpec((None, 2 * XA_WIDTH, MEM_LEN), lambda i: (i, 0, 0)),
        out_shape=jax.ShapeDtypeStruct((b, 2 * XA_WIDTH, MEM_LEN), F32),
        compiler_params=_cparams(("arbitrary",)),
        name="mem_kv",
    )(mem, g, wkv, kn, gm)


def _xattn_kernel(x_ref, mkv_ref, g_ref, wq_ref, wo_ref, qn_ref, gm_ref, o_ref):
    x = x_ref[...]
    xb = _rms_rows(x, g_ref[...]).astype(BF16)
    q = jnp.dot(xb, wq_ref[...], preferred_element_type=F32)
    q = _group_rms(q, gm_ref[...], HEAD_DIM) * qn_ref[...]
    qb = (q * SCALE).astype(BF16)
    outs = []
    for h in range(XA_HEADS):
        kt = mkv_ref[h * HEAD_DIM:(h + 1) * HEAD_DIM, :].astype(BF16)
        vt = mkv_ref[XA_WIDTH + h * HEAD_DIM:XA_WIDTH + (h + 1) * HEAD_DIM, :].astype(BF16)
        s = jnp.dot(qb[:, h * HEAD_DIM:(h + 1) * HEAD_DIM], kt, preferred_element_type=F32)
        e = jnp.exp(s - jnp.max(s, axis=-1, keepdims=True))
        p = (e / jnp.sum(e, axis=-1, keepdims=True)).astype(BF16)
        outs.append(_dot_nt(p, vt))
    o = jnp.concatenate(outs, axis=-1).astype(BF16)
    o_ref[...] = x + jnp.dot(o, wo_ref[...], preferred_element_type=F32)


def _xattn(x3d, mkv_t, layer, g, wq, wo, qn, tm):
    b, t, _ = x3d.shape
    gm = _block_ones(XA_WIDTH, HEAD_DIM)
    full = lambda a: pl.BlockSpec(a.shape, lambda bi, i: (0,) * a.ndim)
    return pl.pallas_call(
        _xattn_kernel,
        grid=(b, t // tm),
        in_specs=[pl.BlockSpec((None, tm, D_MODEL), lambda bi, i: (bi, i, 0)),
                  pl.BlockSpec((None, None, 2 * XA_WIDTH, MEM_LEN), lambda bi, i: (layer, bi, 0, 0)),
                  full(g), full(wq), full(wo), full(qn), full(gm)],
        out_specs=pl.BlockSpec((None, tm, D_MODEL), lambda bi, i: (bi, i, 0)),
        out_shape=jax.ShapeDtypeStruct(x3d.shape, F32),
        compiler_params=_cparams(("arbitrary", "arbitrary"), VMEM_LIMIT_BYTES),
        name="xattn",
    )(x3d, mkv_t, g, wq, wo, qn, gm)


def _gla_kernel(q_ref, k_ref, v_ref, la_ref, og_ref, gn_ref, s0_ref, tri_ref, hsel_ref,
                o_ref, sfin_ref, st_ref, *, n_sub, n_valid, bb):
    ti = pl.program_id(1)

    @pl.when(ti == 0)
    def _():
        st_ref[...] = s0_ref[...]

    tri = tri_ref[...]
    hsel = hsel_ref[...]
    gn = gn_ref[...]
    row = lax.broadcasted_iota(jnp.int32, (SUB, 1), 0)

    def sub_block(i, carry):
        for bi in range(bb):
            one_sequence(i, bi)
        return carry

    def one_sequence(i, bi):
        r0 = pl.multiple_of(i * SUB, SUB)
        rows = pl.ds(r0, SUB)
        q = q_ref[bi, rows, :]
        k = k_ref[bi, rows, :]
        v = v_ref[bi, rows, :]
        la = la_ref[bi, rows, :]
        if n_valid is not None:
            live = (row + r0) < n_valid
            la = jnp.where(live, la, 0.0)
            k = jnp.where(live, k, 0.0)
        b = _dot_f32(tri, la)
        b_end = b[SUB - 1:SUB, :]
        qd = (q * jnp.exp(b)).astype(BF16)
        kd = (k * jnp.exp(b_end - b)).astype(BF16)
        vb = v.astype(BF16)
        tiles = []
        for s in range(SUB):
            e = jnp.exp(jnp.minimum(b - b[s:s + 1, :], 0.0))
            z = (q * k[s:s + 1, :]) * e
            tiles.append(jnp.where(row >= s, z, 0.0))
        att = jnp.dot(jnp.concatenate(tiles, axis=0).astype(BF16), hsel, preferred_element_type=F32)
        dec = jnp.exp(b_end)
        outs = []
        for h in range(GLA_HEADS):
            dk = slice(h * GLA_DK, (h + 1) * GLA_DK)
            dv = slice(h * GLA_DV, (h + 1) * GLA_DV)
            st = st_ref[bi, h]
            o = _dot_nt(qd[:, dk], st.astype(BF16))
            for s in range(SUB):
                o = o + att[s * SUB:(s + 1) * SUB, h:h + 1] * v[s:s + 1, dv]
            st_ref[bi, h] = st * dec[:, dk] + _dot_tn(vb[:, dv], kd[:, dk])
            outs.append(_rms_rows(o, gn))
        o_ref[bi, rows, :] = jnp.concatenate(outs, axis=-1) * og_ref[bi, rows, :]

    lax.fori_loop(0, n_sub, sub_block, 0)

    @pl.when(ti == pl.num_programs(1) - 1)
    def _():
        sfin_ref[...] = st_ref[...]


def _gla(q, k, v, la, og, gn, s0t, tt, bb, n_valid=None):
    b, t, _ = q.shape
    tri = jnp.asarray(np.tril(np.ones((SUB, SUB), np.float32)))
    hsel = jnp.asarray((np.arange(256)[:, None] // GLA_DK == np.arange(128)[None, :]).astype(np.float32), dtype=BF16)
    seq = lambda w: pl.BlockSpec((bb, tt, w), lambda bi, i: (bi, i, 0))
    full = lambda a: pl.BlockSpec(a.shape, lambda bi, i: (0,) * a.ndim)
    st_spec = pl.BlockSpec((bb, GLA_HEADS, GLA_DV, GLA_DK), lambda bi, i: (bi, 0, 0, 0))
    return pl.pallas_call(
        functools.partial(_gla_kernel, n_sub=tt // SUB, n_valid=n_valid, bb=bb),
        grid=(b // bb, t // tt),
        in_specs=[seq(256), seq(256), seq(512), seq(256), seq(512), full(gn), st_spec, full(tri), full(hsel)],
        out_specs=[seq(512), st_spec],
        out_shape=[jax.ShapeDtypeStruct((b, t, GLA_WIDTH), F32),
                   jax.ShapeDtypeStruct((b, GLA_HEADS, GLA_DV, GLA_DK), F32)],
        scratch_shapes=[pltpu.VMEM((bb, GLA_HEADS, GLA_DV, GLA_DK), F32)],
        compiler_params=_cparams(("arbitrary", "arbitrary")),
        name="gla",
    )(q, k, v, la, og, gn, s0t, tri, hsel)


def _conv_kernel(u_ref, st0_ref, w_ref, b_ref, g_ref, beta_ref, o_ref, st_ref, ext_ref, *, tt, n_valid):
    ti = pl.program_id(1)
    ctx = CONV_WIDTH - 1

    @pl.when(ti == 0)
    def _():
        ext_ref[0:8, :] = jnp.zeros((8, CONV_CH), F32)
        ext_ref[pl.ds(2, ctx), :] = st0_ref[...]

    ext_ref[pl.ds(32, tt), :] = u_ref[...]
    acc = jnp.zeros((tt, CONV_CH), F32)
    for w in range(CONV_WIDTH):
        acc = acc + ext_ref[pl.ds(2 + w, tt), :] * w_ref[w:w + 1, :]
    y = acc + b_ref[...]
    mu = jnp.mean(y, axis=-1, keepdims=True)
    var = jnp.mean(jnp.square(y - mu), axis=-1, keepdims=True)
    ln = (y - mu) * lax.rsqrt(var + EPS) * g_ref[...] + beta_ref[...]
    o_ref[...] = ln * _sigmoid(ln)

    @pl.when(ti == pl.num_programs(1) - 1)
    def _():
        st_ref[...] = ext_ref[pl.ds(32 + n_valid - ctx, ctx), :]

    ext_ref[0:32, :] = ext_ref[pl.ds(tt, 32), :]


def _conv(u, st0, w, b, g, beta, tt, n_valid):
    bsz, t, _ = u.shape
    ctx = CONV_WIDTH - 1
    full = lambda a: pl.BlockSpec(a.shape, lambda bi, i: (0,) * a.ndim)
    st_spec = pl.BlockSpec((None, ctx, CONV_CH), lambda bi, i: (bi, 0, 0))
    return pl.pallas_call(
        functools.partial(_conv_kernel, tt=tt, n_valid=n_valid),
        grid=(bsz, t // tt),
        in_specs=[pl.BlockSpec((None, tt, CONV_CH), lambda bi, i: (bi, i, 0)), st_spec,
                  full(w), full(b), full(g), full(beta)],
        out_specs=[pl.BlockSpec((None, tt, CONV_CH), lambda bi, i: (bi, i, 0)), st_spec],
        out_shape=[jax.ShapeDtypeStruct(u.shape, F32), jax.ShapeDtypeStruct((bsz, ctx, CONV_CH), F32)],
        scratch_shapes=[pltpu.VMEM((32 + max(tt, 32), CONV_CH), F32)],
        compiler_params=_cparams(("arbitrary", "arbitrary")),
        name="conv",
    )(u, st0, w, b, g, beta)


def _fox_kernel(qt_ref, k_ref, vt_ref, ck_ref, cqt_ref, o_ref, m_ref, l_ref, acc_ref, *, tq, tk):
    qi = pl.program_id(1)
    ki = pl.program_id(2)
    last = (qi * tq + tq - 1) // tk

    @pl.when(ki == 0)
    def _():
        m_ref[...] = jnp.full_like(m_ref, NEG)
        l_ref[...] = jnp.zeros_like(l_ref)
        acc_ref[...] = jnp.zeros_like(acc_ref)

    def tile(neg):
        for h in range(FOX_HEADS):
            cols = slice(h * HEAD_DIM, (h + 1) * HEAD_DIM)
            s = jnp.dot(k_ref[:, cols], qt_ref[cols, :], preferred_element_type=F32)
            s = s + cqt_ref[h:h + 1, :] - ck_ref[:, h:h + 1]
            if neg is not None:
                s = s + neg
            m_old = m_ref[h:h + 1, :]
            m_new = jnp.maximum(m_old, jnp.max(s, axis=0, keepdims=True))
            p = jnp.exp(s - m_new)
            alpha = jnp.exp(m_old - m_new)
            l_ref[h:h + 1, :] = alpha * l_ref[h:h + 1, :] + jnp.sum(p, axis=0, keepdims=True)
            acc_ref[cols, :] = alpha * acc_ref[cols, :] + jnp.dot(vt_ref[cols, :], p.astype(BF16),
                                                                  preferred_element_type=F32)
            m_ref[h:h + 1, :] = m_new

    @pl.when(ki < last)
    def _():
        tile(None)

    @pl.when(ki == last)
    def _():
        s_pos = ki * tk + lax.broadcasted_iota(jnp.int32, (tk, tq), 0)
        t_pos = qi * tq + lax.broadcasted_iota(jnp.int32, (tk, tq), 1)
        tile(jnp.where(s_pos <= t_pos, 0.0, NEG))

    @pl.when(ki == pl.num_programs(2) - 1)
    def _():
        for h in range(FOX_HEADS):
            cols = slice(h * HEAD_DIM, (h + 1) * HEAD_DIM)
            acc_ref[cols, :] = acc_ref[cols, :] / jnp.maximum(l_ref[h:h + 1, :], 1e-30)
        o_ref[...] = acc_ref[...].T


def _fox_prompt(qst, kb, vbt, c8, ct, tq, tk):
    b, t, _ = kb.shape
    nq, nk = t // tq, t // tk
    kmin = lambda qi, ki: jnp.minimum(ki, (qi * tq + tq - 1) // tk)
    return pl.pallas_call(
        functools.partial(_fox_kernel, tq=tq, tk=tk),
        grid=(b, nq, nk),
        in_specs=[pl.BlockSpec((FOX_WIDTH, tq), lambda bi, qi, ki: (0, bi * nq + qi)),
                  pl.BlockSpec((None, tk, FOX_WIDTH), lambda bi, qi, ki: (bi, kmin(qi, ki), 0)),
                  pl.BlockSpec((FOX_WIDTH, tk), lambda bi, qi, ki: (0, bi * nk + kmin(qi, ki))),
                  pl.BlockSpec((None, tk, FOX_HEADS), lambda bi, qi, ki: (bi, kmin(qi, ki), 0)),
                  pl.BlockSpec((None, FOX_HEADS, tq), lambda bi, qi, ki: (bi, 0, qi))],
        out_specs=pl.BlockSpec((None, tq, FOX_WIDTH), lambda bi, qi, ki: (bi, qi, 0)),
        out_shape=jax.ShapeDtypeStruct((b, t, FOX_WIDTH), F32),
        scratch_shapes=[pltpu.VMEM((FOX_HEADS, tq), F32), pltpu.VMEM((FOX_HEADS, tq), F32),
                        pltpu.VMEM((FOX_WIDTH, tq), F32)],
        compiler_params=_cparams(("arbitrary", "arbitrary", "arbitrary"), VMEM_LIMIT_BYTES),
        name="fox_prompt",
    )(qst, kb, vbt, c8, ct)


def _compress_weights(w1):
    w = w1.reshape(2, 2, 8, 2, HEAD_DIM, CMP_HIDDEN)
    z = jnp.zeros_like(w)
    g0 = jnp.concatenate([w, z], axis=-1)
    g1 = jnp.concatenate([z, w], axis=-1)
    wbd = jnp.stack([g0, g1], axis=4)
    return wbd.reshape(2, 2, 8, 4 * HEAD_DIM, 2 * CMP_HIDDEN).astype(BF16)


def _compress_compute(src_refs, pe_ref, w1_ref, w2_ref, kn_ref, ck_ref, cv_ref, sh_ref, nseg):
    sh_ref[pl.ds(nseg, 8), :] = jnp.zeros((8, CMP_HIDDEN), F32)
    for j, src_ref in enumerate(src_refs):
        a = jnp.zeros((nseg, 2 * CMP_HIDDEN), F32)
        bm = jnp.zeros((nseg, 2 * CMP_HIDDEN), F32)
        for q in range(CMP_STRIDE // 2):
            p0, p1 = 2 * q, 2 * q + 1
            x0 = src_ref[pl.ds(p0, nseg, stride=CMP_STRIDE), :]
            x1 = src_ref[pl.ds(p1, nseg, stride=CMP_STRIDE), :]
            xa = jnp.concatenate([x0 + pe_ref[j, p0:p0 + 1, :], x1 + pe_ref[j, p1:p1 + 1, :]], axis=1)
            a = a + jnp.dot(xa.astype(BF16), w1_ref[j, 0, q], preferred_element_type=F32)
            p0, p1 = p0 + CMP_STRIDE, p1 + CMP_STRIDE
            xb = jnp.concatenate([x0 + pe_ref[j, p0:p0 + 1, :], x1 + pe_ref[j, p1:p1 + 1, :]], axis=1)
            bm = bm + jnp.dot(xb.astype(BF16), w1_ref[j, 1, q], preferred_element_type=F32)
        for g in range(NSA_KV_HEADS):
            sh_ref[pl.ds(0, nseg), :] = bm[:, g * CMP_HIDDEN:(g + 1) * CMP_HIDDEN]
            x = a[:, g * CMP_HIDDEN:(g + 1) * CMP_HIDDEN] + sh_ref[pl.ds(1, nseg), :]
            hid = x * (0.5 * (1.0 + jnp.tanh(math.sqrt(2.0 / math.pi) * (x + 0.044715 * (x * x * x)))))
            ckv = jnp.dot(hid.astype(BF16), w2_ref[j], preferred_element_type=F32)
            if j == 0:
                ck_ref[g] = _rms_rows(ckv, kn_ref[...]).astype(BF16)
            else:
                cv_ref[g] = ckv.astype(BF16)


def _compress_prompt_kernel(xk_ref, xv_ref, pe_ref, w1_ref, w2_ref, kn_ref, ck_ref, cv_ref, sh_ref, *, nseg):
    _compress_compute((xk_ref, xv_ref), pe_ref, w1_ref, w2_ref, kn_ref, ck_ref, cv_ref, sh_ref, nseg)


def _compress_prompt(kvr, pe, w1, w2, kn):
    b, t, _ = kvr.shape
    nseg = t // CMP_STRIDE
    full = lambda a: pl.BlockSpec(a.shape, lambda i: (0,) * a.ndim)
    o_spec = pl.BlockSpec((None, NSA_KV_HEADS, nseg, HEAD_DIM), lambda i: (i, 0, 0, 0))
    o_shape = jax.ShapeDtypeStruct((b, NSA_KV_HEADS, nseg, HEAD_DIM), BF16)
    return pl.pallas_call(
        functools.partial(_compress_prompt_kernel, nseg=nseg),
        grid=(b,),
        in_specs=[pl.BlockSpec((None, t, LANES), lambda i: (i, 0, 0)), pl.BlockSpec((None, t, LANES), lambda i: (i, 0, 1)),
                  full(pe), full(w1), full(w2), full(kn)],
        out_specs=[o_spec, o_spec],
        out_shape=[o_shape, o_shape],
        scratch_shapes=[pltpu.VMEM((nseg + 8, CMP_HIDDEN), F32)],
        compiler_params=_cparams(("arbitrary",), VMEM_LIMIT_BYTES),
        name="nsa_compress",
    )(kvr, kvr, pe, w1, w2, kn)


def _stack_heads(qs_ref, kh):
    parts = [qs_ref[:, (kh * NSA_GROUP + g) * HEAD_DIM:(kh * NSA_GROUP + g + 1) * HEAD_DIM].astype(F32)
             for g in range(NSA_GROUP)]
    return jnp.concatenate(parts, axis=0).astype(BF16)


def _cmp_topk_kernel(qs_ref, ck_ref, cv_ref, farcol_ref, chi_ref, clo_ref, pool_ref, ocmp_ref, msk_ref,
                     *, tq, nseg, q_base, n_pick, n_blk):
    qi = pl.program_id(1)
    G = NSA_GROUP
    q0 = q_base + qi * tq
    nbase = q0 // CMP_STRIDE - 16
    place = (lax.broadcasted_iota(jnp.int32, (32, nseg), 1) - lax.broadcasted_iota(jnp.int32, (32, nseg), 0)) == nbase
    place = jnp.where(place, 1.0, 0.0).astype(BF16)
    t1 = q0 + lax.broadcasted_iota(jnp.int32, (tq, 1), 0)
    t4 = jnp.concatenate([t1] * G, axis=0)
    n_i = lax.broadcasted_iota(jnp.int32, (G * tq, nseg), 1)
    valid = (n_i * CMP_STRIDE + (CMP_BLOCK - 1) <= t4) & (n_i <= nseg - 2)
    blk = lax.broadcasted_iota(jnp.int32, (tq, N_SELBLK), 1)
    cur = lax.shift_right_logical(t1, 6)
    forced = (blk == 0) | (blk == cur) | (blk == cur - 1)
    for kh in range(NSA_KV_HEADS):
        q4 = _stack_heads(qs_ref, kh)
        s = _dot_nt(q4, ck_ref[kh]) + farcol_ref[kh]
        s = s + jnp.dot(chi_ref[kh], place, preferred_element_type=F32) + jnp.dot(clo_ref[kh], place, preferred_element_type=F32)
        s = jnp.where(valid, s, NEG)
        e = jnp.where(valid, jnp.exp(s - jnp.max(s, axis=-1, keepdims=True)), 0.0)
        p = e / jnp.maximum(jnp.sum(e, axis=-1, keepdims=True), 1e-30)
        o = jnp.dot(p.astype(BF16), cv_ref[kh], preferred_element_type=F32)
        for g in range(G):
            h = kh * G + g
            ocmp_ref[:, h * HEAD_DIM:(h + 1) * HEAD_DIM] = o[g * tq:(g + 1) * tq]
        imp = p[0:tq] + p[tq:2 * tq] + p[2 * tq:3 * tq] + p[3 * tq:4 * tq]
        pooled = _dot_f32(imp, pool_ref[...])
        score = jnp.where((blk > cur) | (blk >= n_blk), -1e30, jnp.where(forced, 1e30, pooled))
        sel = jnp.zeros((tq, N_SELBLK), F32)
        for _ in range(n_pick):
            mx = jnp.max(score, axis=-1, keepdims=True)
            first = jnp.min(jnp.where(score == mx, blk, N_SELBLK), axis=-1, keepdims=True)
            pick = blk == first
            sel = jnp.where(pick, 1.0, sel)
            score = jnp.where(pick, -3e38, score)
        msk_ref[kh] = sel


def _cmp_topk_t_kernel(qt_ref, ck_ref, cv_ref, farrow_ref, chit_ref, clot_ref, poolt_ref, ocmpt_ref, mskt_ref,
                       *, tq, nseg, n_pick, n_blk):
    qi = pl.program_id(1)
    G = NSA_GROUP
    q0 = qi * tq
    nbase = q0 // CMP_STRIDE - 16
    place_t = (lax.broadcasted_iota(jnp.int32, (nseg, 32), 0) - lax.broadcasted_iota(jnp.int32, (nseg, 32), 1)) == nbase
    place_t = jnp.where(place_t, 1.0, 0.0).astype(BF16)
    t1 = q0 + lax.broadcasted_iota(jnp.int32, (1, tq), 1)
    n_i = lax.broadcasted_iota(jnp.int32, (nseg, tq), 0)
    valid1 = (n_i * CMP_STRIDE + (CMP_BLOCK - 1) <= t1) & (n_i <= nseg - 2)
    valid = jnp.concatenate([valid1] * G, axis=1)
    blk = lax.broadcasted_iota(jnp.int32, (N_SELBLK, tq), 0)
    cur = lax.shift_right_logical(t1, 6)
    forced = (blk == 0) | (blk == cur) | (blk == cur - 1)
    for kh in range(NSA_KV_HEADS):
        qt4 = jnp.concatenate([qt_ref[(kh * G + g) * HEAD_DIM:(kh * G + g + 1) * HEAD_DIM, :] for g in range(G)], axis=1)
        s = jnp.dot(ck_ref[kh], qt4, preferred_element_type=F32) + farrow_ref[kh]
        s = s + jnp.dot(place_t, chit_ref[kh], preferred_element_type=F32) + jnp.dot(place_t, clot_ref[kh], preferred_element_type=F32)
        s = jnp.where(valid, s, NEG)
        e = jnp.where(valid, jnp.exp(s - jnp.max(s, axis=0, keepdims=True)), 0.0)
        p = e * (1.0 / jnp.maximum(jnp.sum(e, axis=0, keepdims=True), 1e-30))
        o_t = _dot_tn(cv_ref[kh], p.astype(BF16))
        for g in range(G):
            h = kh * G + g
            ocmpt_ref[h * HEAD_DIM:(h + 1) * HEAD_DIM, :] = o_t[:, g * tq:(g + 1) * tq]
        imp = p[:, 0:tq] + p[:, tq:2 * tq] + p[:, 2 * tq:3 * tq] + p[:, 3 * tq:4 * tq]
        pooled = _dot_f32(poolt_ref[...], imp)
        score = jnp.where((blk > cur) | (blk >= n_blk), -1e30, jnp.where(forced, 1e30, pooled))
        sel = jnp.zeros((N_SELBLK, tq), F32)
        for _ in range(n_pick):
            mx = jnp.max(score, axis=0, keepdims=True)
            first = jnp.min(jnp.where(score == mx, blk, N_SELBLK), axis=0, keepdims=True)
            pick = blk == first
            sel = jnp.where(pick, 1.0, sel)
            score = jnp.where(pick, -3e38, score)
        mskt_ref[kh] = sel


def _cmp_topk_prompt(nqst, ck, cv, tbl, b, t, tq, n_pick):
    nseg = ck.shape[2]
    assert tq <= 256 and ck.shape[0] == b
    farcol, chi, clo = _cmp_bias_tables(tbl, tq)
    farrow, chit, clot = (jnp.swapaxes(a, 1, 2) for a in (farcol, chi, clo))
    poolt = jnp.asarray((np.arange(N_SELBLK)[:, None] == np.arange(nseg)[None, :] // SEL_RATIO).astype(np.float32))
    full = lambda a: pl.BlockSpec(a.shape, lambda bi, i: (0,) * a.ndim)
    c_spec = pl.BlockSpec((None, NSA_KV_HEADS, nseg, HEAD_DIM), lambda bi, i: (bi, 0, 0, 0))
    nq = t // tq
    col_tile = pl.BlockSpec((NSA_WIDTH, tq), lambda bi, i: (0, bi * nq + i))
    return pl.pallas_call(
        functools.partial(_cmp_topk_t_kernel, tq=tq, nseg=nseg, n_pick=n_pick, n_blk=t // SEL_BLOCK),
        grid=(b, nq),
        in_specs=[col_tile, c_spec, c_spec, full(farrow), full(chit), full(clot), full(poolt)],
        out_specs=[col_tile, pl.BlockSpec((None, NSA_KV_HEADS, N_SELBLK, tq), lambda bi, i: (bi, 0, 0, i))],
        out_shape=[jax.ShapeDtypeStruct((NSA_WIDTH, b * t), F32),
                   jax.ShapeDtypeStruct((b, NSA_KV_HEADS, N_SELBLK, t), F32)],
        compiler_params=_cparams(("arbitrary", "arbitrary"), VMEM_LIMIT_BYTES),
        name="nsa_cmp_topk",
    )(nqst, ck, cv, farrow, chit, clot, poolt)


def _rel_bucket(dist):
    exact = NUM_BUCKETS // 2
    d = jnp.maximum(dist, 0)
    log_ratio = jnp.log(jnp.maximum(d, 1).astype(jnp.float32) / exact) / math.log(MAX_DISTANCE / exact)
    large = jnp.minimum(exact + (log_ratio * (NUM_BUCKETS - exact)).astype(jnp.int32), NUM_BUCKETS - 1)
    return jnp.where(d < exact, d, large)


def _bias_lookup(tbl, dist):
    onehot = _rel_bucket(dist)[..., None] == jnp.arange(NUM_BUCKETS)
    t = tbl.reshape((tbl.shape[0],) + (1,) * dist.ndim + (NUM_BUCKETS,))
    return jnp.sum(jnp.where(onehot[None], t, 0.0), axis=-1)


def _cmp_bias_tables(tbl, tq):
    tr = jnp.arange(tq)[:, None]
    i = jnp.arange(32)[None, :]
    dist = tr + 16 * CMP_STRIDE - CMP_STRIDE * i - (CMP_BLOCK - 1)
    near = _bias_lookup(tbl, dist)
    far = tbl[:, NUM_BUCKETS - 1]
    corr = (near - far[:, None, None]).reshape(NSA_KV_HEADS, NSA_GROUP * tq, 32)
    hi = corr.astype(BF16)
    lo = (corr - hi.astype(F32)).astype(BF16)
    farcol = jnp.broadcast_to(far[:, None, None], (NSA_HEADS, tq, 1)).reshape(NSA_KV_HEADS, NSA_GROUP * tq, 1)
    return farcol, hi, lo


def _cmp_topk_sample(qs, ck, cv, tbl, tq, q_base, n_pick, n_blk):
    b, t, _ = qs.shape
    nseg = ck.shape[2]
    assert tq <= 256
    farcol, chi, clo = _cmp_bias_tables(tbl, tq)
    pool = jnp.asarray((np.arange(nseg)[:, None] // SEL_RATIO == np.arange(N_SELBLK)[None, :]).astype(np.float32))
    full = lambda a: pl.BlockSpec(a.shape, lambda bi, i: (0,) * a.ndim)
    c_spec = pl.BlockSpec((None, NSA_KV_HEADS, nseg, HEAD_DIM), lambda bi, i: (bi, 0, 0, 0))
    m_spec = pl.BlockSpec((None, NSA_KV_HEADS, tq, N_SELBLK), lambda bi, i: (bi, 0, i, 0))
    m_shape = (b, NSA_KV_HEADS, t, N_SELBLK)
    return pl.pallas_call(
        functools.partial(_cmp_topk_kernel, tq=tq, nseg=nseg, q_base=q_base, n_pick=n_pick, n_blk=n_blk),
        grid=(b, t // tq),
        in_specs=[pl.BlockSpec((None, tq, NSA_WIDTH), lambda bi, i: (bi, i, 0)), c_spec, c_spec,
                  full(farcol), full(chi), full(clo), full(pool)],
        out_specs=[pl.BlockSpec((None, tq, NSA_WIDTH), lambda bi, i: (bi, i, 0)), m_spec],
        out_shape=[jax.ShapeDtypeStruct((b, t, NSA_WIDTH), F32), jax.ShapeDtypeStruct(m_shape, F32)],
        compiler_params=_cparams(("arbitrary", "arbitrary"), VMEM_LIMIT_BYTES),
        name="nsa_cmp_topk",
    )(qs, ck, cv, farcol, chi, clo, pool)


def _online_update(s, valid, v, m_ref, l_ref, acc_ref, v_t=False):
    s = jnp.where(valid, s, NEG)
    m_old = m_ref[...]
    m_new = jnp.maximum(m_old, jnp.max(s, axis=-1, keepdims=True))
    p = jnp.where(valid, jnp.exp(s - m_new), 0.0)
    alpha = jnp.exp(m_old - m_new)
    l_ref[...] = alpha * l_ref[...] + jnp.sum(p, axis=-1, keepdims=True)
    pb = p.astype(BF16)
    pv = _dot_nt(pb, v) if v_t else jnp.dot(pb, v, preferred_element_type=F32)
    acc_ref[...] = alpha * acc_ref[...] + pv
    m_ref[...] = m_new


def _selwin_kernel(qt_ref, selk_ref, selvt_ref, wink_ref, winvt_ref, mskt_ref, bias_ref, ocmpt_ref, gtt_ref, o_ref,
                   m_ref, l_ref, acc_ref, ot_ref, qt4_ref, *, tq):
    qi = pl.program_id(1)
    G = NSA_GROUP
    blocks_per_tile = tq // SEL_BLOCK
    s_rel = lax.broadcasted_iota(jnp.int32, (tq, tq), 0)
    t_rel = lax.broadcasted_iota(jnp.int32, (tq, tq), 1)
    causal = s_rel <= t_rel
    all_rows, near_rows, diag_rows = slice(0, 3 * tq), slice(tq, 3 * tq), slice(2 * tq, 3 * tq)

    def init():
        m_ref[...] = jnp.full_like(m_ref, NEG)
        l_ref[...] = jnp.zeros_like(l_ref)
        acc_ref[...] = jnp.zeros_like(acc_ref)

    def update(kh, k, vt, bias_rows, valid):
        s = jnp.dot(k, qt4_ref[...], preferred_element_type=F32)
        if bias_rows is not None:
            s = s + bias_ref[kh, bias_rows, :]
        if valid is not None:
            neg = jnp.where(valid, 0.0, NEG)
            s = s + jnp.concatenate([neg] * G, axis=1)
        m_old = m_ref[0:1, :]
        m_new = jnp.maximum(m_old, jnp.max(s, axis=0, keepdims=True))
        p = jnp.exp(s - m_new)
        alpha = jnp.exp(m_old - m_new)
        l_ref[0:1, :] = alpha * l_ref[0:1, :] + jnp.sum(p, axis=0, keepdims=True)
        acc_ref[...] = alpha * acc_ref[...] + jnp.dot(vt, p.astype(BF16), preferred_element_type=F32)
        m_ref[0:1, :] = m_new

    def finish(kh, gate_row):
        for g in range(G):
            h = kh * G + g
            cols = slice(g * tq, (g + 1) * tq)
            rows = slice(h * HEAD_DIM, (h + 1) * HEAD_DIM)
            o = acc_ref[:, cols] / jnp.maximum(l_ref[0:1, cols], 1e-30)
            r = 3 * h + gate_row
            ot_ref[rows, :] += gtt_ref[r:r + 1, :] * o

    for h in range(NSA_HEADS):
        rows = slice(h * HEAD_DIM, (h + 1) * HEAD_DIM)
        ot_ref[rows, :] = gtt_ref[3 * h:3 * h + 1, :] * ocmpt_ref[rows, :]

    for kh in range(NSA_KV_HEADS):
        kcols = slice(kh * HEAD_DIM, (kh + 1) * HEAD_DIM)
        vrows = slice(kh * HEAD_DIM, (kh + 1) * HEAD_DIM)
        for g in range(G):
            h = kh * G + g
            qt4_ref[:, g * tq:(g + 1) * tq] = qt_ref[h * HEAD_DIM:(h + 1) * HEAD_DIM, :]

        def sel_valid(j):
            parts = [jnp.broadcast_to(mskt_ref[kh, pl.ds(j * blocks_per_tile + i, 1), :], (SEL_BLOCK, tq))
                     for i in range(blocks_per_tile)]
            return jnp.concatenate(parts, axis=0) > 0.5

        def sel_update(j0, n_tiles, bias_rows, last_is_diag):
            start = pl.multiple_of(j0 * tq, tq)
            parts = [sel_valid(j0 + i) for i in range(n_tiles)]
            if last_is_diag:
                parts[-1] = parts[-1] & causal
            update(kh, selk_ref[pl.ds(start, n_tiles * tq), kcols], selvt_ref[vrows, pl.ds(start, n_tiles * tq)],
                   bias_rows, jnp.concatenate(parts, axis=0))

        def win_update(j0, valid_parts, bias_rows):
            n = len(valid_parts) * tq
            start = pl.multiple_of(j0 * tq, tq)
            update(kh, wink_ref[pl.ds(start, n), kcols], winvt_ref[vrows, pl.ds(start, n)], bias_rows,
                   jnp.concatenate(valid_parts, axis=0))

        init()

        def far_group(jg, c):
            sel_update(jg * FAR_GROUP, FAR_GROUP, None, False)
            return c

        def far_single(j, c):
            sel_update(j, 1, None, False)
            return c

        n_far = jnp.maximum(qi - 1, 0)
        n_grp = n_far // FAR_GROUP
        lax.fori_loop(0, n_grp, far_group, 0)
        lax.fori_loop(n_grp * FAR_GROUP, n_far, far_single, 0)

        @pl.when(qi >= 1)
        def _():
            sel_update(qi - 1, 2, near_rows, True)

        @pl.when(qi == 0)
        def _():
            sel_update(0, 1, diag_rows, True)

        finish(kh, 1)

        init()
        all_valid = s_rel >= 0

        @pl.when(qi >= 2)
        def _():
            win_update(qi - 2, [s_rel >= t_rel, all_valid, causal], all_rows)

        @pl.when(qi == 1)
        def _():
            win_update(0, [all_valid, causal], near_rows)

        @pl.when(qi == 0)
        def _():
            win_update(0, [causal], diag_rows)

        finish(kh, 2)

    o_ref[...] = ot_ref[...].T


def _selwin_bias_tables(tbl, tq):
    sr = jnp.arange(tq)[:, None]
    tr = jnp.arange(tq)[None, :]
    far = tbl[:, NUM_BUCKETS - 1][:, None, None]
    near0 = _bias_lookup(tbl, tr - sr) - far
    near1 = _bias_lookup(tbl, tr - sr + tq) - far
    b = jnp.concatenate([jnp.zeros_like(near1), near1, near0], axis=1)
    b = b.reshape(NSA_KV_HEADS, NSA_GROUP, 3 * tq, tq).transpose(0, 2, 1, 3)
    return b.reshape(NSA_KV_HEADS, 3 * tq, NSA_GROUP * tq)


def _selwin_prompt(nqst, selkv, selvt, winkv, winvt, mskt, tbl, ocmp, gtt, b, t, tq):
    assert tq >= NSA_WINDOW // 2 and tq >= MAX_DISTANCE and tq % SEL_BLOCK == 0
    bias = _selwin_bias_tables(tbl, tq)
    nq = t // tq
    col_tile = lambda h: pl.BlockSpec((h, tq), lambda bi, i: (0, bi * nq + i))
    row_tile = lambda w: pl.BlockSpec((tq, w), lambda bi, i: (bi * nq + i, 0))
    return pl.pallas_call(
        functools.partial(_selwin_kernel, tq=tq),
        grid=(b, nq),
        in_specs=[col_tile(NSA_WIDTH),
                  pl.BlockSpec((t, LANES), lambda bi, i: (bi, 0)), pl.BlockSpec((LANES, t), lambda bi, i: (0, bi)),
                  pl.BlockSpec((t, LANES), lambda bi, i: (bi, 0)), pl.BlockSpec((LANES, t), lambda bi, i: (0, bi)),
                  pl.BlockSpec((None, NSA_KV_HEADS, N_SELBLK, tq), lambda bi, i: (bi, 0, 0, i)),
                  pl.BlockSpec(bias.shape, lambda bi, i: (0, 0, 0)),
                  col_tile(NSA_WIDTH), col_tile(32)],
        out_specs=row_tile(NSA_WIDTH),
        out_shape=jax.ShapeDtypeStruct((b * t, NSA_WIDTH), F32),
        scratch_shapes=[pltpu.VMEM((8, NSA_GROUP * tq), F32), pltpu.VMEM((8, NSA_GROUP * tq), F32),
                        pltpu.VMEM((HEAD_DIM, NSA_GROUP * tq), F32), pltpu.VMEM((NSA_WIDTH, tq), F32),
                        pltpu.VMEM((HEAD_DIM, NSA_GROUP * tq), BF16)],
        compiler_params=_cparams(("arbitrary", "arbitrary"), VMEM_LIMIT_BYTES),
        name="nsa_selwin",
    )(nqst, selkv, selvt, winkv, winvt, mskt, bias, ocmp, gtt)


PAGES_PER_STEP = 16
PAGE = 128


def _feature_major(cache):
    nd = cache.ndim
    t = jnp.transpose(cache, (0, 1) + tuple(range(3, nd)) + (2,))
    return t.reshape(cache.shape[0], cache.shape[1], -1, cache.shape[2])


def _page_specs(layer, rows, row_block, reverse_steps=None):
    group = (lambda j: j) if reverse_steps is None else (lambda j: reverse_steps - 1 - j)
    return [pl.BlockSpec((None, None, rows, PAGE), functools.partial(
        lambda bi, j, pt, r: (layer, pt[bi, group(j) * PAGES_PER_STEP + r], row_block, 0), r=r))
        for r in range(PAGES_PER_STEP)]


def _compress_sample_kernel(pt_ref, *refs, nseg):
    pages = refs[:PAGES_PER_STEP]
    pe_ref, w1_ref, w2_ref, kn_ref, ck_ref, cv_ref, srck_ref, srcv_ref, sh_ref = refs[PAGES_PER_STEP:]
    j = pl.program_id(1)
    for r, p_ref in enumerate(pages):
        rows = pl.ds(pl.multiple_of((j * PAGES_PER_STEP + r) * PAGE, PAGE), PAGE)
        srck_ref[rows, :] = p_ref[0:LANES, :].T
        srcv_ref[rows, :] = p_ref[LANES:2 * LANES, :].T

    @pl.when(j == pl.num_programs(1) - 1)
    def _():
        _compress_compute((srck_ref, srcv_ref), pe_ref, w1_ref, w2_ref, kn_ref, ck_ref, cv_ref, sh_ref, nseg)


def _compress_sample(cache_t, layer, page_table, pe, w1, w2, kn):
    sb, n_pages = page_table.shape
    past = n_pages * PAGE
    nseg = past // CMP_STRIDE
    full = lambda a: pl.BlockSpec(a.shape, lambda bi, j, pt: (0,) * a.ndim)
    o_spec = pl.BlockSpec((None, NSA_KV_HEADS, nseg, HEAD_DIM), lambda bi, j, pt: (bi, 0, 0, 0))
    o_shape = jax.ShapeDtypeStruct((sb, NSA_KV_HEADS, nseg, HEAD_DIM), BF16)
    return pl.pallas_call(
        functools.partial(_compress_sample_kernel, nseg=nseg),
        grid_spec=pltpu.PrefetchScalarGridSpec(
            num_scalar_prefetch=1, grid=(sb, n_pages // PAGES_PER_STEP),
            in_specs=_page_specs(layer, 256, 0) + [full(pe), full(w1), full(w2), full(kn)],
            out_specs=[o_spec, o_spec],
            scratch_shapes=[pltpu.VMEM((past, LANES), F32), pltpu.VMEM((past, LANES), F32),
                            pltpu.VMEM((nseg + 8, CMP_HIDDEN), F32)]),
        out_shape=[o_shape, o_shape],
        compiler_params=_cparams(("arbitrary", "arbitrary"), VMEM_LIMIT_BYTES),
        name="nsa_compress_sample",
    )(page_table, *([cache_t] * PAGES_PER_STEP), pe, w1, w2, kn)


def _selwin_sample_kernel(pt_ref, qs_ref, *refs):
    pages = refs[:PAGES_PER_STEP]
    (msk_ref, newkv_ref, winst_ref, newwr_ref, bsel_ref, bnew_ref, bwin_ref, expand_ref, ocmp_ref, gt_ref, o_ref,
     kv_ref, m_ref, l_ref, acc_ref, osel_ref, owin_ref) = refs[PAGES_PER_STEP:]
    j = pl.program_id(1)
    G = NSA_GROUP
    R = SAMPLE_ROWS
    for r, p_ref in enumerate(pages):
        kv_ref[:, pl.ds(pl.multiple_of((j * PAGES_PER_STEP + r) * PAGE, PAGE), PAGE)] = p_ref[...].astype(BF16)

    @pl.when(j == pl.num_programs(1) - 1)
    def _():
        rq = lax.broadcasted_iota(jnp.int32, (G * R, 1), 0) & (R - 1)
        new_valid = lax.broadcasted_iota(jnp.int32, (G * R, R), 1) <= rq
        win_valid = lax.broadcasted_iota(jnp.int32, (G * R, NSA_WINDOW), 1) >= rq

        def init():
            m_ref[...] = jnp.full_like(m_ref, NEG)
            l_ref[...] = jnp.zeros_like(l_ref)
            acc_ref[...] = jnp.zeros_like(acc_ref)

        def finish(dst_ref, kh):
            o = acc_ref[...] / jnp.maximum(l_ref[...], 1e-30)
            for g in range(G):
                h = kh * G + g
                dst_ref[:, h * HEAD_DIM:(h + 1) * HEAD_DIM] = o[g * R:(g + 1) * R]

        for kh in range(NSA_KV_HEADS):
            kcols = slice(kh * HEAD_DIM, (kh + 1) * HEAD_DIM)
            vcols = slice(128 + kh * HEAD_DIM, 128 + (kh + 1) * HEAD_DIM)
            q4 = _stack_heads(qs_ref, kh)
            mskb = msk_ref[kh].astype(BF16)

            init()
            mt = jnp.dot(mskb, expand_ref[...], preferred_element_type=F32) > 0.5
            valid = jnp.concatenate([mt] * G, axis=0)
            s = jnp.dot(q4, kv_ref[kcols, :], preferred_element_type=F32) + bsel_ref[kh]
            _online_update(s, valid, kv_ref[vcols, :], m_ref, l_ref, acc_ref, v_t=True)
            knew = newkv_ref[:, 256 + kh * HEAD_DIM:256 + (kh + 1) * HEAD_DIM].astype(BF16)
            vnew = newkv_ref[:, 384 + kh * HEAD_DIM:384 + (kh + 1) * HEAD_DIM].astype(BF16)
            _online_update(_dot_nt(q4, knew) + bnew_ref[kh], new_valid, vnew, m_ref, l_ref, acc_ref)
            finish(osel_ref, kh)

            init()
            kwin = winst_ref[kcols, :].astype(BF16)
            vwin = winst_ref[vcols, :].astype(BF16)
            _online_update(jnp.dot(q4, kwin, preferred_element_type=F32) + bwin_ref[kh], win_valid, vwin,
                           m_ref, l_ref, acc_ref, v_t=True)
            knew = newwr_ref[:, kcols].astype(BF16)
            vnew = newwr_ref[:, vcols].astype(BF16)
            _online_update(_dot_nt(q4, knew) + bnew_ref[kh], new_valid, vnew, m_ref, l_ref, acc_ref)
            finish(owin_ref, kh)

        for h in range(NSA_HEADS):
            cols = slice(h * HEAD_DIM, (h + 1) * HEAD_DIM)
            o_ref[:, cols] = (gt_ref[:, 3 * h:3 * h + 1] * ocmp_ref[:, cols]
                              + gt_ref[:, 3 * h + 1:3 * h + 2] * osel_ref[:, cols]
                              + gt_ref[:, 3 * h + 2:3 * h + 3] * owin_ref[:, cols])


def _sample_bias_tables(tbl, past):
    R = SAMPLE_ROWS
    r = jnp.arange(R)[:, None]
    stack = lambda a: a.reshape(NSA_KV_HEADS, NSA_GROUP * R, a.shape[-1])
    cached = _bias_lookup(tbl, past + r - jnp.arange(past)[None, :])
    new = _bias_lookup(tbl, r - jnp.arange(R)[None, :])
    win = _bias_lookup(tbl, NSA_WINDOW + r - jnp.arange(NSA_WINDOW)[None, :])
    return stack(cached), stack(new), stack(win)


def _selwin_sample(qs, cache_t, layer, page_table, msk, newkv, win_t, newwr, tbl, ocmp, gt):
    sb, n_pages = page_table.shape
    past = n_pages * PAGE
    assert win_t.shape[-1] == NSA_WINDOW and past >= NSA_WINDOW and past // SEL_BLOCK <= N_SELBLK
    bsel, bnew, bwin = _sample_bias_tables(tbl, past)
    expand = jnp.asarray((np.arange(N_SELBLK)[:, None] == np.arange(past)[None, :] // SEL_BLOCK).astype(np.float32),
                         dtype=BF16)
    R = SAMPLE_ROWS
    full = lambda a: pl.BlockSpec(a.shape, lambda bi, j, pt: (0,) * a.ndim)
    seq = lambda a: pl.BlockSpec((None,) + a.shape[1:], lambda bi, j, pt: (bi,) + (0,) * (a.ndim - 1))
    win_spec = pl.BlockSpec((None, None) + win_t.shape[2:], lambda bi, j, pt: (layer, bi, 0, 0))
    return pl.pallas_call(
        _selwin_sample_kernel,
        grid_spec=pltpu.PrefetchScalarGridSpec(
            num_scalar_prefetch=1, grid=(sb, n_pages // PAGES_PER_STEP),
            in_specs=[seq(qs)] + _page_specs(layer, 256, 1) + [seq(msk), seq(newkv), win_spec, seq(newwr),
                                                               full(bsel), full(bnew), full(bwin), full(expand),
                                                               seq(ocmp), seq(gt)],
            out_specs=pl.BlockSpec((None, R, NSA_WIDTH), lambda bi, j, pt: (bi, 0, 0)),
            scratch_shapes=[pltpu.VMEM((256, past), BF16),
                            pltpu.VMEM((NSA_GROUP * R, 1), F32), pltpu.VMEM((NSA_GROUP * R, 1), F32),
                            pltpu.VMEM((NSA_GROUP * R, HEAD_DIM), F32),
                            pltpu.VMEM((R, NSA_WIDTH), F32), pltpu.VMEM((R, NSA_WIDTH), F32)]),
        out_shape=jax.ShapeDtypeStruct((sb, R, NSA_WIDTH), F32),
        compiler_params=_cparams(("arbitrary", "arbitrary"), VMEM_LIMIT_BYTES),
        name="nsa_selwin_sample",
    )(page_table, qs, *([cache_t] * PAGES_PER_STEP), msk, newkv, win_t, newwr, bsel, bnew, bwin, expand, ocmp, gt)


def _fox_sample_kernel(pt_ref, qs_ref, *refs):
    P = PAGES_PER_STEP
    pages = refs[:P]
    lf_pages = refs[P:2 * P]
    (sl_ref, newkv_ref, lfnew_ref, hmask_ref, o_ref,
     qbd_ref, crel_ref, m_ref, l_ref, acc_ref, carry_ref) = refs[2 * P:]
    j = pl.program_id(1)
    R = SAMPLE_ROWS
    H = FOX_HEADS
    tk = P * PAGE

    @pl.when(j == 0)
    def _():
        carry_ref[...] = jnp.zeros_like(carry_ref)
        q = qs_ref[...].astype(F32)
        qbd_ref[...] = (jnp.concatenate([q] * H, axis=0) * hmask_ref[...]).astype(BF16)
        tri = jnp.where(lax.broadcasted_iota(jnp.int32, (R, R), 1) <= lax.broadcasted_iota(jnp.int32, (R, R), 0), 1.0, 0.0)
        crel = _dot_f32(tri, lfnew_ref[...])
        crel_ref[...] = jnp.concatenate([crel[:, h:h + 1] for h in range(H)], axis=0)
        m_ref[...] = jnp.full_like(m_ref, NEG)
        l_ref[...] = jnp.zeros_like(l_ref)
        acc_ref[...] = jnp.zeros_like(acc_ref)

    carry = carry_ref[:, 0:1]
    d_parts = [None] * P
    for r in reversed(range(P)):
        lf = lf_pages[r][...]
        d_parts[r] = _dot_f32(lf, sl_ref[...]) + carry
        carry = carry + jnp.sum(lf, axis=1, keepdims=True)
    carry_ref[...] = jnp.broadcast_to(carry, carry_ref.shape)
    dt = jnp.concatenate(d_parts, axis=1)

    kt = jnp.concatenate([p[0:FOX_WIDTH, :] for p in pages], axis=1).astype(BF16)
    vt = jnp.concatenate([p[FOX_WIDTH:, :] for p in pages], axis=1).astype(BF16)
    drows = jnp.concatenate([jnp.broadcast_to(dt[h:h + 1, :], (R, tk)) for h in range(H)], axis=0)
    s = jnp.dot(qbd_ref[...], kt, preferred_element_type=F32) + crel_ref[...] + drows
    _online_update(s, s > 2 * NEG, vt, m_ref, l_ref, acc_ref, v_t=True)

    @pl.when(j == pl.num_programs(1) - 1)
    def _():
        lf = lfnew_ref[...]
        iu = lax.broadcasted_iota(jnp.int32, (R, R), 0)
        ir = lax.broadcasted_iota(jnp.int32, (R, R), 1)
        a_le = jnp.where(ir <= iu, 1.0, 0.0)
        b_gt = jnp.where(iu > ir, 1.0, 0.0)
        dnew = jnp.concatenate([_dot_f32(a_le, lf[:, h:h + 1] * b_gt) for h in range(H)], axis=0)
        rq = lax.broadcasted_iota(jnp.int32, (H * R, 1), 0) & (R - 1)
        valid = lax.broadcasted_iota(jnp.int32, (H * R, R), 1) <= rq
        knew = newkv_ref[:, 0:FOX_WIDTH].astype(BF16)
        vnew = newkv_ref[:, FOX_WIDTH:].astype(BF16)
        _online_update(_dot_nt(qbd_ref[...], knew) + dnew, valid, vnew, m_ref, l_ref, acc_ref)
        o = (acc_ref[...] / jnp.maximum(l_ref[...], 1e-30)) * hmask_ref[...]
        out = o[0:R]
        for h in range(1, H):
            out = out + o[h * R:(h + 1) * R]
        o_ref[...] = out


def _fox_sample(qs, kv_t, logf_t, layer, page_table, newkv, lfnew):
    sb, n_pages = page_table.shape
    R = SAMPLE_ROWS
    n_steps = n_pages // PAGES_PER_STEP
    hmask = jnp.asarray((np.arange(FOX_HEADS * R)[:, None] // R == np.arange(FOX_WIDTH)[None, :] // HEAD_DIM)
                        .astype(np.float32))
    sl = jnp.asarray(np.tril(np.ones((PAGE, PAGE), np.float32), -1))
    seq = lambda a: pl.BlockSpec((None,) + a.shape[1:], lambda bi, j, pt: (bi,) + (0,) * (a.ndim - 1))
    full = lambda a: pl.BlockSpec(a.shape, lambda bi, j, pt: (0,) * a.ndim)
    return pl.pallas_call(
        _fox_sample_kernel,
        grid_spec=pltpu.PrefetchScalarGridSpec(
            num_scalar_prefetch=1, grid=(sb, n_steps),
            in_specs=[seq(qs)] + _page_specs(layer, 2 * FOX_WIDTH, 0, n_steps) + _page_specs(layer, FOX_HEADS, 0, n_steps)
                     + [full(sl), seq(newkv), seq(lfnew), full(hmask)],
            out_specs=pl.BlockSpec((None, R, FOX_WIDTH), lambda bi, j, pt: (bi, 0, 0)),
            scratch_shapes=[pltpu.VMEM((FOX_HEADS * R, FOX_WIDTH), BF16), pltpu.VMEM((FOX_HEADS * R, 1), F32),
                            pltpu.VMEM((FOX_HEADS * R, 1), F32), pltpu.VMEM((FOX_HEADS * R, 1), F32),
                            pltpu.VMEM((FOX_HEADS * R, FOX_WIDTH), F32), pltpu.VMEM((FOX_HEADS, LANES), F32)]),
        out_shape=jax.ShapeDtypeStruct((sb, R, FOX_WIDTH), F32),
        compiler_params=_cparams(("arbitrary", "arbitrary"), VMEM_LIMIT_BYTES),
        name="fox_sample",
    )(page_table, qs, *([kv_t] * PAGES_PER_STEP), *([logf_t] * PAGES_PER_STEP), sl, newkv, lfnew, hmask)


def _pad_cols(w, width):
    return jnp.pad(w, ((0, 0), (0, width - w.shape[1])))


def _prep_even_w(w):
    gq, gk, gv, glr, gog, nq, nkv, ng = _split(w, EVEN_SIZES)
    return jnp.concatenate([gq, gk, gv, gog, nq, nkv, _pad_cols(glr, LANES), _pad_cols(ng, LANES)], axis=1).astype(BF16)


def _prep_odd_w(w):
    fq, fk, fv, ff, cg = _split(w, ODD_SIZES)
    return jnp.concatenate([fq, fk, fv, cg, _pad_cols(ff, LANES)], axis=1).astype(BF16)


def _row(v):
    return v.reshape(1, -1).astype(F32)


def _tile_row(v, reps):
    return jnp.tile(v.astype(F32), reps).reshape(1, -1)


def kernel(x_prompt, x_sample, cache_nsa_kv, state_nsa_win, state_gla, cache_fox_kv, cache_fox_logf, state_conv, cache_mem_kv, page_table, mem_prompt, rel_bias, norm_mix, norm_xattn, norm_ffn, even_w_in, even_w_out, gla_w_gate, gla_b_gate, gla_out_norm, nsa_q_norm, nsa_k_norm, nsa_cmp_pe, nsa_cmp_w1, nsa_cmp_w2, odd_w_in, odd_w_out, fox_q_norm, fox_k_norm, fox_b_f, conv_w, conv_b, conv_ln_g, conv_ln_b, mem_norm, xa_wq, xa_wkv, xa_wo, xa_q_norm, xa_k_norm, ffn_w_in, ffn_w_out):
    B, T, _ = x_prompt.shape
    SB, SQ, _ = x_sample.shape
    depth = norm_mix.shape[0]
    past_len = page_table.shape[1] * cache_nsa_kv.shape[2]
    MP = B * T
    SR = SAMPLE_ROWS
    MS = SB * SR

    yp = x_prompt.reshape(MP, D_MODEL)
    ys = jnp.pad(x_sample, ((0, 0), (0, SR - SQ), (0, 0))).reshape(MS, D_MODEL)

    nsa_cache_t = _feature_major(cache_nsa_kv)
    fox_cache_t = _feature_major(cache_fox_kv)
    logf_t = _feature_major(cache_fox_logf)
    win_t = _feature_major(state_nsa_win)
    mem_cache_t = _feature_major(cache_mem_kv)

    nsa_kv_p, nsa_kv_s, win_p, win_s, gla_p, gla_s = [], [], [], [], [], []
    fox_kv_p, fox_kv_s, logf_p, logf_s, conv_p, conv_s, mem_kv_p = [], [], [], [], [], [], []

    for layer in range(depth):
        if layer % 2 == 0:
            e = layer // 2
            w_pad = _prep_even_w(even_w_in[e])
            wg_pad = jnp.pad(gla_w_gate[e], ((0, LANES - GLA_RANK), (0, 0))).astype(BF16)
            bg = _row(gla_b_gate[e])
            qn = _tile_row(nsa_q_norm[e], NSA_HEADS)
            kn1 = _tile_row(nsa_k_norm[e, 1], NSA_KV_HEADS)
            kn2 = _tile_row(nsa_k_norm[e, 2], NSA_KV_HEADS)
            gn = _row(gla_out_norm[e])
            w_out = even_w_out[e].astype(BF16)
            g_mix = _row(norm_mix[layer])
            pe = jnp.tile(nsa_cmp_pe[e].astype(F32), (1, 1, NSA_KV_HEADS))
            w1b = _compress_weights(nsa_cmp_w1[e])
            w2b = nsa_cmp_w2[e].astype(BF16)
            kn0 = _row(nsa_k_norm[e, 0])
            tbl = rel_bias.astype(F32).T
            (q, k, v, la, og, nqs, kvr, wr, gt, selkv, winkv, nqst, selvt, winvt, gtt) = _even_in(
                yp, g_mix, w_pad, wg_pad, bg, qn, kn1, kn2, 512)
            r3 = lambda a: a.reshape(B, T, a.shape[-1])
            s0t = jnp.zeros((B, GLA_HEADS, GLA_DV, GLA_DK), F32)
            o_gla, sfin_t = _gla(r3(q), r3(k), r3(v), r3(la), r3(og), gn, s0t, 256, B)
            kvr5 = kvr.reshape(B, T, 4, NSA_KV_HEADS, HEAD_DIM)
            wr5 = wr.reshape(B, T, 2, NSA_KV_HEADS, HEAD_DIM)
            ck, cv = _compress_prompt(r3(kvr), pe, w1b, w2b, kn0)
            ocmp_t, mskt = _cmp_topk_prompt(nqst, ck, cv, tbl, B, T, 256, SEL_TOPN)
            o_nsa = _selwin_prompt(nqst, selkv, selvt, winkv, winvt, mskt, tbl, ocmp_t, gtt, B, T, 256)
            yp = _out_proj(yp, o_gla.reshape(MP, GLA_WIDTH), o_nsa.reshape(MP, NSA_WIDTH),
                           w_out[:GLA_WIDTH], w_out[GLA_WIDTH:], 512)
            nsa_kv_p.append(kvr5)
            win_p.append(wr5[:, -min(NSA_WINDOW, T):])
            gla_p.append(jnp.swapaxes(sfin_t, -1, -2))
            (q, k, v, la, og, nqs, kvr, wr, gt) = _even_in(ys, g_mix, w_pad, wg_pad, bg, qn, kn1, kn2, MS)[:9]
            pad16 = lambda a: jnp.pad(a.reshape(SB, SR, a.shape[-1]), ((0, 0), (0, SUB - SR), (0, 0)))
            s0t = jnp.swapaxes(state_gla[e], -1, -2)
            o_gla, snew_t = _gla(pad16(q), pad16(k), pad16(v), pad16(la), pad16(og), gn, s0t, SUB, 4, n_valid=SQ)
            o_gla = o_gla[:, :SR]
            s3 = lambda a: a.reshape(SB, SR, a.shape[-1])
            kvr5 = kvr.reshape(SB, SR, 4, NSA_KV_HEADS, HEAD_DIM)[:, :SQ]
            wr5 = wr.reshape(SB, SR, 2, NSA_KV_HEADS, HEAD_DIM)[:, :SQ]
            ck, cv = _compress_sample(nsa_cache_t, e, page_table, pe, w1b, w2b, kn0)
            ocmp, msk = _cmp_topk_sample(s3(nqs), ck, cv, tbl, SR, past_len, SEL_TOPN - 1, past_len // SEL_BLOCK)
            o_nsa = _selwin_sample(s3(nqs), nsa_cache_t, e, page_table, msk, s3(kvr), win_t, s3(wr), tbl, ocmp, s3(gt))
            ys = _out_proj(ys, o_gla.reshape(MS, GLA_WIDTH), o_nsa.reshape(MS, NSA_WIDTH),
                           w_out[:GLA_WIDTH], w_out[GLA_WIDTH:], MS)
            nsa_kv_s.append(kvr5)
            win_s.append(jnp.concatenate([state_nsa_win[e][:, SQ:], wr5], axis=1))
            gla_s.append(jnp.swapaxes(snew_t, -1, -2))
        else:
            j = layer // 2
            w_pad = _prep_odd_w(odd_w_in[j])
            qn = _tile_row(fox_q_norm[j], FOX_HEADS)
            kn = _tile_row(fox_k_norm[j], FOX_HEADS)
            bf_pad = jnp.pad(fox_b_f[j].astype(F32), (0, LANES - FOX_HEADS)).reshape(1, LANES)
            w_out = odd_w_out[j].astype(BF16)
            g_mix = _row(norm_mix[layer])
            cw = conv_w[j].astype(F32)
            cb, cg_, cbeta = _row(conv_b[j]), _row(conv_ln_g[j]), _row(conv_ln_b[j])
            qs, kv, kb, lf, c, u, qst, vbt = _odd_in(yp.reshape(B, T, D_MODEL), g_mix, w_pad, qn, kn, bf_pad, 512)
            c8 = c[:, :, :FOX_HEADS]
            o_fox = _fox_prompt(qst, kb, vbt, c8, jnp.swapaxes(c8, 1, 2), 512, 1024)
            o_conv, cst = _conv(u, jnp.zeros((B, CONV_WIDTH - 1, CONV_CH), F32), cw, cb, cg_, cbeta, 512, 512)
            yp = _out_proj(yp, o_fox.reshape(MP, FOX_WIDTH), o_conv.reshape(MP, CONV_CH),
                           w_out[:FOX_WIDTH], w_out[FOX_WIDTH:], 512)
            fox_kv_p.append(kv.reshape(B, T, 2, FOX_HEADS, HEAD_DIM))
            logf_p.append(lf[:, :, :FOX_HEADS])
            conv_p.append(cst)
            qs, kv, kb, lf, c, u = _odd_in(ys.reshape(1, MS, D_MODEL), g_mix, w_pad, qn, kn, bf_pad, MS)[:6]
            s3 = lambda a: a.reshape(SB, SR, a.shape[-1])
            new_kv = kv.reshape(SB, SR, 2, FOX_HEADS, HEAD_DIM)[:, :SQ]
            lf_new = lf.reshape(SB, SR, LANES)[:, :SQ, :FOX_HEADS]
            o_fox = _fox_sample(s3(qs), fox_cache_t, logf_t, j, page_table, s3(kv), s3(lf))
            o_conv, cst = _conv(u.reshape(SB, SR, CONV_CH), state_conv[j], cw, cb, cg_, cbeta, SR, SQ)
            ys = _out_proj(ys, o_fox.reshape(MS, FOX_WIDTH), o_conv.reshape(MS, CONV_CH),
                           w_out[:FOX_WIDTH], w_out[FOX_WIDTH:], MS)
            fox_kv_s.append(new_kv)
            logf_s.append(lf_new)
            conv_s.append(cst)
        g_xa = _row(norm_xattn[layer])
        wq = xa_wq[layer].astype(BF16)
        wo = xa_wo[layer].astype(BF16)
        xqn = _tile_row(xa_q_norm[layer], XA_HEADS)
        mkv_t = _mem_kv(mem_prompt, _row(mem_norm[layer]), xa_wkv[layer].astype(BF16), _tile_row(xa_k_norm[layer], XA_HEADS))
        mem_kv_p.append(jnp.transpose(mkv_t.reshape(B, 2, XA_HEADS, HEAD_DIM, MEM_LEN), (0, 4, 1, 2, 3)))
        yp = _xattn(yp.reshape(B, T, D_MODEL), mkv_t[None], 0, g_xa, wq, wo, xqn, 512).reshape(MP, D_MODEL)
        ys = _xattn(ys.reshape(SB, SR, D_MODEL), mem_cache_t, layer, g_xa, wq, wo, xqn, SR).reshape(MS, D_MODEL)
        g_ffn = _row(norm_ffn[layer])
        w_in = ffn_w_in[layer].astype(BF16)
        w_o = ffn_w_out[layer].astype(BF16)
        yp = _ffn(yp, g_ffn, w_in, w_o, 512)
        ys = _ffn(ys, g_ffn, w_in, w_o, MS)

    yp = yp.reshape(B, T, D_MODEL)
    ys = ys.reshape(SB, SR, D_MODEL)[:, :SQ]
    return (yp, ys,
            jnp.stack(nsa_kv_p), jnp.stack(nsa_kv_s), jnp.stack(win_p), jnp.stack(win_s),
            jnp.stack(gla_p), jnp.stack(gla_s), jnp.stack(fox_kv_p), jnp.stack(fox_kv_s),
            jnp.stack(logf_p), jnp.stack(logf_s), jnp.stack(conv_p), jnp.stack(conv_s),
            jnp.stack(mem_kv_p))
```

```python
import functools
import math

import jax
import jax.numpy as jnp
import numpy as np
from jax import lax
from jax.experimental import pallas as pl
from jax.experimental.pallas import tpu as pltpu

F32 = jnp.float32
BF16 = jnp.bfloat16

D_MODEL = 1024
HEAD_DIM = 64
GLA_WIDTH = 512
GLA_HEADS = 4
GLA_DV = 128
GLA_DK = 64
GLA_RANK = 16
GLA_TAU = 16.0
NSA_WIDTH = 512
NSA_HEADS = 8
NSA_KV_HEADS = 2
NSA_GROUP = 4
CMP_STRIDE = 16
CMP_BLOCK = 32
CMP_HIDDEN = 256
SEL_BLOCK = 64
SEL_RATIO = 4
SEL_TOPN = 16
NSA_WINDOW = 512
FOX_WIDTH = 512
FOX_HEADS = 8
CONV_CH = 512
CONV_WIDTH = 31
MEM_LEN = 256
XA_HEADS = 4
XA_WIDTH = 256
FFN_HIDDEN = 2816
NUM_BUCKETS = 32
MAX_DISTANCE = 128
EPS = 1e-6
SCALE = HEAD_DIM ** -0.5
NEG = -1e30

EVEN_SIZES = (256, 256, 512, 16, 512, 512, 768, 24)
ODD_SIZES = (512, 512, 512, 8, 1024)

LANES = 128
VMEM_LIMIT_BYTES = 56 * 1024 * 1024
SAMPLE_ROWS = 8
SUB = 16
FAR_GROUP = 4
CMP_CHUNK = 128
N_SELBLK = 128


def _cparams(sem, vmem=None):
    return pltpu.CompilerParams(dimension_semantics=sem, vmem_limit_bytes=vmem)


def _split(h, sizes):
    return jnp.split(h, np.cumsum(sizes)[:-1].tolist(), axis=-1)


def _rms_rows(x, g):
    return x * lax.rsqrt(jnp.mean(x * x, axis=-1, keepdims=True) + EPS) * g


def _group_rms(x, gmat, gs):
    x2 = x * x
    hi = x2.astype(BF16)
    lo = (x2 - hi.astype(F32)).astype(BF16)
    ms = (jnp.dot(hi, gmat, preferred_element_type=F32) + jnp.dot(lo, gmat, preferred_element_type=F32)) * (1.0 / gs)
    return x * lax.rsqrt(ms + EPS)


def _log_sigmoid(z):
    return -(jnp.maximum(-z, 0.0) + jnp.log1p(jnp.exp(-jnp.abs(z))))


def _sigmoid(z):
    return 1.0 / (1.0 + jnp.exp(-z))


def _dot_nt(a, b):
    return lax.dot_general(a, b, (((1,), (1,)), ((), ())), preferred_element_type=F32)


def _dot_tn(a, b):
    return lax.dot_general(a, b, (((0,), (0,)), ((), ())), preferred_element_type=F32)


def _dot_f32(a, b):
    return jnp.dot(a, b, preferred_element_type=F32, precision=lax.Precision.HIGHEST)


def _block_ones(width, gs):
    r = np.arange(width) // gs
    return jnp.asarray((r[:, None] == r[None, :]).astype(np.float32), dtype=BF16)


def _even_in_kernel(x_ref, g_ref, w_ref, wg_ref, bg_ref, qn_ref, kn1_ref, kn2_ref, gm512_ref, gm128_ref,
                    q_ref, k_ref, v_ref, la_ref, og_ref, nqs_ref, kvr_ref, wr_ref, gt_ref, selkv_ref, winkv_ref,
                    nqst_ref, selvt_ref, winvt_ref, gtt_ref):
    xb = _rms_rows(x_ref[...], g_ref[...]).astype(BF16)

    def proj(lo, hi):
        return jnp.dot(xb, w_ref[:, lo:hi], preferred_element_type=F32)

    q_ref[...] = proj(0, 256) * (GLA_DK ** -0.5)
    k_ref[...] = proj(256, 512)
    v_ref[...] = proj(512, 1024)
    og = proj(1024, 1536)
    og_ref[...] = og * _sigmoid(og)
    nq = _group_rms(proj(1536, 2048), gm512_ref[...], HEAD_DIM) * qn_ref[...] * SCALE
    nqs_ref[...] = nq.astype(BF16)
    nqst_ref[...] = nq.T.astype(BF16)
    kvr_ref[:, 0:256] = proj(2048, 2304)
    selk = _group_rms(proj(2304, 2432), gm128_ref[...], HEAD_DIM) * kn1_ref[...]
    selv = proj(2432, 2560)
    kvr_ref[:, 256:384] = selk
    kvr_ref[:, 384:512] = selv
    selkv_ref[:, 0:128] = selk.astype(BF16)
    selkv_ref[:, 128:256] = selv.astype(BF16)
    selvt_ref[...] = selv.T.astype(BF16)
    wink = _group_rms(proj(2560, 2688), gm128_ref[...], HEAD_DIM) * kn2_ref[...]
    winv = proj(2688, 2816)
    wr_ref[:, 0:128] = wink
    wr_ref[:, 128:256] = winv
    winkv_ref[:, 0:128] = wink.astype(BF16)
    winkv_ref[:, 128:256] = winv.astype(BF16)
    winvt_ref[...] = winv.T.astype(BF16)
    glr = proj(2816, 2944).astype(BF16)
    z = jnp.dot(glr, wg_ref[...], preferred_element_type=F32) + bg_ref[...]
    la_ref[...] = _log_sigmoid(z) * (1.0 / GLA_TAU)
    gates = _sigmoid(proj(2944, 3072))
    gt_ref[...] = gates
    gtt_ref[...] = gates.T[0:32, :]


def _even_in(x2d, g, w_pad, wg_pad, bg, qn, kn1, kn2, tm):
    m = x2d.shape[0]
    widths = (256, 256, 512, 256, 512, 512, 512, 256, 128, 256, 256)
    dtypes = (F32, F32, F32, F32, F32, BF16, F32, F32, F32, BF16, BF16)
    t_heights = (512, 128, 128, 32)
    t_dtypes = (BF16, BF16, BF16, F32)
    full = lambda a: pl.BlockSpec(a.shape, lambda i: (0,) * a.ndim)
    gm512 = _block_ones(512, HEAD_DIM)
    gm128 = _block_ones(128, HEAD_DIM)
    ins = (x2d, g, w_pad, wg_pad, bg, qn, kn1, kn2, gm512, gm128)
    return pl.pallas_call(
        _even_in_kernel,
        grid=(m // tm,),
        in_specs=[pl.BlockSpec((tm, D_MODEL), lambda i: (i, 0))] + [full(a) for a in ins[1:]],
        out_specs=[pl.BlockSpec((tm, w), lambda i: (i, 0)) for w in widths]
                  + [pl.BlockSpec((h, tm), lambda i: (0, i)) for h in t_heights],
        out_shape=[jax.ShapeDtypeStruct((m, w), d) for w, d in zip(widths, dtypes)]
                  + [jax.ShapeDtypeStruct((h, m), d) for h, d in zip(t_heights, t_dtypes)],
        compiler_params=_cparams(("arbitrary",), VMEM_LIMIT_BYTES),
        name="even_in",
    )(*ins)


def _odd_in_kernel(x_ref, g_ref, w_ref, qn_ref, kn_ref, bf_ref, gm512_ref, tri_ref,
                   qs_ref, kv_ref, kb_ref, lf_ref, c_ref, u_ref, qst_ref, vbt_ref, carry_ref):
    @pl.when(pl.program_id(1) == 0)
    def _():
        carry_ref[...] = jnp.zeros_like(carry_ref)

    xb = _rms_rows(x_ref[...], g_ref[...]).astype(BF16)

    def proj(lo, hi):
        return jnp.dot(xb, w_ref[:, lo:hi], preferred_element_type=F32)

    q = _group_rms(proj(0, 512), gm512_ref[...], HEAD_DIM) * qn_ref[...] * SCALE
    qs_ref[...] = q.astype(BF16)
    qst_ref[...] = q.T.astype(BF16)
    k = _group_rms(proj(512, 1024), gm512_ref[...], HEAD_DIM) * kn_ref[...]
    v = proj(1024, 1536)
    kv_ref[:, 0:512] = k
    kv_ref[:, 512:1024] = v
    kb_ref[...] = k.astype(BF16)
    vbt_ref[...] = v.T.astype(BF16)
    u_ref[...] = proj(1536, 2048) * _sigmoid(proj(2048, 2560))
    lf = _log_sigmoid(proj(2560, 2688) + bf_ref[...])
    lf_ref[...] = lf
    c = _dot_f32(tri_ref[...], lf) + carry_ref[0:1, :]
    c_ref[...] = c
    carry_ref[0:1, :] = c[-1:, :]


def _odd_in(x3d, g, w_pad, qn, kn, bf_pad, tm):
    b, t, _ = x3d.shape
    widths = (512, 1024, 512, 128, 128, 512)
    dtypes = (BF16, F32, BF16, F32, F32, F32)
    gm512 = _block_ones(512, HEAD_DIM)
    tri = jnp.asarray(np.tril(np.ones((tm, tm), np.float32)))
    ins = (x3d, g, w_pad, qn, kn, bf_pad, gm512, tri)
    full = lambda a: pl.BlockSpec(a.shape, lambda bi, i: (0,) * a.ndim)
    nt = t // tm
    t_spec = pl.BlockSpec((FOX_WIDTH, tm), lambda bi, i: (0, bi * nt + i))
    t_shape = jax.ShapeDtypeStruct((FOX_WIDTH, b * t), BF16)
    return pl.pallas_call(
        _odd_in_kernel,
        grid=(b, nt),
        in_specs=[pl.BlockSpec((None, tm, D_MODEL), lambda bi, i: (bi, i, 0))] + [full(a) for a in ins[1:]],
        out_specs=[pl.BlockSpec((None, tm, w), lambda bi, i: (bi, i, 0)) for w in widths] + [t_spec, t_spec],
        out_shape=[jax.ShapeDtypeStruct((b, t, w), d) for w, d in zip(widths, dtypes)] + [t_shape, t_shape],
        scratch_shapes=[pltpu.VMEM((8, 128), F32)],
        compiler_params=_cparams(("arbitrary", "arbitrary"), VMEM_LIMIT_BYTES),
        name="odd_in",
    )(*ins)


def _out_proj_kernel(res_ref, a1_ref, a2_ref, w1_ref, w2_ref, o_ref):
    acc = jnp.dot(a1_ref[...].astype(BF16), w1_ref[...], preferred_element_type=F32)
    acc = acc + jnp.dot(a2_ref[...].astype(BF16), w2_ref[...], preferred_element_type=F32)
    o_ref[...] = res_ref[...] + acc


def _out_proj(res, a1, a2, w1, w2, tm):
    m = res.shape[0]
    row = lambda a: pl.BlockSpec((tm, a.shape[1]), lambda i: (i, 0))
    full = lambda a: pl.BlockSpec(a.shape, lambda i: (0, 0))
    return pl.pallas_call(
        _out_proj_kernel,
        grid=(m // tm,),
        in_specs=[row(res), row(a1), row(a2), full(w1), full(w2)],
        out_specs=row(res),
        out_shape=jax.ShapeDtypeStruct(res.shape, F32),
        compiler_params=_cparams(("arbitrary",), VMEM_LIMIT_BYTES),
        name="out_proj",
    )(res, a1, a2, w1, w2)


def _ffn_kernel(x_ref, g_ref, wg_ref, wu_ref, wo_ref, o_ref, xn_ref, acc_ref):
    j = pl.program_id(1)

    @pl.when(j == 0)
    def _():
        xn_ref[...] = _rms_rows(x_ref[...], g_ref[...]).astype(BF16)
        acc_ref[...] = jnp.zeros_like(acc_ref)

    xb = xn_ref[...]
    gate = jnp.dot(xb, wg_ref[...], preferred_element_type=F32)
    up = jnp.dot(xb, wu_ref[...], preferred_element_type=F32)
    h = (gate * _sigmoid(gate) * up).astype(BF16)
    acc_ref[...] += jnp.dot(h, wo_ref[...], preferred_element_type=F32)

    @pl.when(j == pl.num_programs(1) - 1)
    def _():
        o_ref[...] = x_ref[...] + acc_ref[...]


def _ffn(x2d, g, w_in, w_out, tm, n_chunks=2):
    m = x2d.shape[0]
    th = FFN_HIDDEN // n_chunks
    return pl.pallas_call(
        _ffn_kernel,
        grid=(m // tm, n_chunks),
        in_specs=[pl.BlockSpec((tm, D_MODEL), lambda i, j: (i, 0)),
                  pl.BlockSpec((1, D_MODEL), lambda i, j: (0, 0)),
                  pl.BlockSpec((D_MODEL, th), lambda i, j: (0, j)),
                  pl.BlockSpec((D_MODEL, th), lambda i, j: (0, n_chunks + j)),
                  pl.BlockSpec((th, D_MODEL), lambda i, j: (j, 0))],
        out_specs=pl.BlockSpec((tm, D_MODEL), lambda i, j: (i, 0)),
        out_shape=jax.ShapeDtypeStruct(x2d.shape, F32),
        scratch_shapes=[pltpu.VMEM((tm, D_MODEL), BF16), pltpu.VMEM((tm, D_MODEL), F32)],
        compiler_params=_cparams(("arbitrary", "arbitrary"), VMEM_LIMIT_BYTES),
        name="ffn",
    )(x2d, g, w_in, w_in, w_out)


def _mem_kv_kernel(m_ref, g_ref, w_ref, kn_ref, gm_ref, o_ref):
    xb = _rms_rows(m_ref[...], g_ref[...]).astype(BF16)
    kv = jnp.dot(xb, w_ref[...], preferred_element_type=F32)
    o_ref[0:XA_WIDTH, :] = (_group_rms(kv[:, 0:XA_WIDTH], gm_ref[...], HEAD_DIM) * kn_ref[...]).T
    o_ref[XA_WIDTH:, :] = kv[:, XA_WIDTH:].T


def _mem_kv(mem, g, wkv, kn):
    b = mem.shape[0]
    gm = _block_ones(XA_WIDTH, HEAD_DIM)
    full = lambda a: pl.BlockSpec(a.shape, lambda i: (0,) * a.ndim)
    return pl.pallas_call(
        _mem_kv_kernel,
        grid=(b,),
        in_specs=[pl.BlockSpec((None, MEM_LEN, D_MODEL), lambda i: (i, 0, 0)), full(g), full(wkv), full(kn), full(gm)],
        out_specs=pl.BlockSpec((None, 2 * XA_WIDTH, MEM_LEN), lambda i: (i, 0, 0)),
        out_shape=jax.ShapeDtypeStruct((b, 2 * XA_WIDTH, MEM_LEN), F32),
        compiler_params=_cparams(("arbitrary",)),
        name="mem_kv",
    )(mem, g, wkv, kn, gm)


def _xattn_kernel(x_ref, mkv_ref, g_ref, wq_ref, wo_ref, qn_ref, gm_ref, o_ref):
    x = x_ref[...]
    xb = _rms_rows(x, g_ref[...]).astype(BF16)
    q = jnp.dot(xb, wq_ref[...], preferred_element_type=F32)
    q = _group_rms(q, gm_ref[...], HEAD_DIM) * qn_ref[...]
    qb = (q * SCALE).astype(BF16)
    outs = []
    for h in range(XA_HEADS):
        kt = mkv_ref[h * HEAD_DIM:(h + 1) * HEAD_DIM, :].astype(BF16)
        vt = mkv_ref[XA_WIDTH + h * HEAD_DIM:XA_WIDTH + (h + 1) * HEAD_DIM, :].astype(BF16)
        s = jnp.dot(qb[:, h * HEAD_DIM:(h + 1) * HEAD_DIM], kt, preferred_element_type=F32)
        e = jnp.exp(s - jnp.max(s, axis=-1, keepdims=True))
        p = (e / jnp.sum(e, axis=-1, keepdims=True)).astype(BF16)
        outs.append(_dot_nt(p, vt))
    o = jnp.concatenate(outs, axis=-1).astype(BF16)
    o_ref[...] = x + jnp.dot(o, wo_ref[...], preferred_element_type=F32)


def _xattn(x3d, mkv_t, layer, g, wq, wo, qn, tm):
    b, t, _ = x3d.shape
    gm = _block_ones(XA_WIDTH, HEAD_DIM)
    full = lambda a: pl.BlockSpec(a.shape, lambda bi, i: (0,) * a.ndim)
    return pl.pallas_call(
        _xattn_kernel,
        grid=(b, t // tm),
        in_specs=[pl.BlockSpec((None, tm, D_MODEL), lambda bi, i: (bi, i, 0)),
                  pl.BlockSpec((None, None, 2 * XA_WIDTH, MEM_LEN), lambda bi, i: (layer, bi, 0, 0)),
                  full(g), full(wq), full(wo), full(qn), full(gm)],
        out_specs=pl.BlockSpec((None, tm, D_MODEL), lambda bi, i: (bi, i, 0)),
        out_shape=jax.ShapeDtypeStruct(x3d.shape, F32),
        compiler_params=_cparams(("arbitrary", "arbitrary"), VMEM_LIMIT_BYTES),
        name="xattn",
    )(x3d, mkv_t, g, wq, wo, qn, gm)


def _gla_kernel(q_ref, k_ref, v_ref, la_ref, og_ref, gn_ref, s0_ref, tri_ref, hsel_ref,
                o_ref, sfin_ref, st_ref, *, n_sub, n_valid, bb):
    ti = pl.program_id(1)

    @pl.when(ti == 0)
    def _():
        st_ref[...] = s0_ref[...]

    tri = tri_ref[...]
    hsel = hsel_ref[...]
    gn = gn_ref[...]
    row = lax.broadcasted_iota(jnp.int32, (SUB, 1), 0)

    def sub_block(i, carry):
        for bi in range(bb):
            one_sequence(i, bi)
        return carry

    def one_sequence(i, bi):
        r0 = pl.multiple_of(i * SUB, SUB)
        rows = pl.ds(r0, SUB)
        q = q_ref[bi, rows, :]
        k = k_ref[bi, rows, :]
        v = v_ref[bi, rows, :]
        la = la_ref[bi, rows, :]
        if n_valid is not None:
            live = (row + r0) < n_valid
            la = jnp.where(live, la, 0.0)
            k = jnp.where(live, k, 0.0)
        b = _dot_f32(tri, la)
        b_end = b[SUB - 1:SUB, :]
        qd = (q * jnp.exp(b)).astype(BF16)
        kd = (k * jnp.exp(b_end - b)).astype(BF16)
        vb = v.astype(BF16)
        tiles = []
        for s in range(SUB):
            e = jnp.exp(jnp.minimum(b - b[s:s + 1, :], 0.0))
            z = (q * k[s:s + 1, :]) * e
            tiles.append(jnp.where(row >= s, z, 0.0))
        att = jnp.dot(jnp.concatenate(tiles, axis=0).astype(BF16), hsel, preferred_element_type=F32)
        dec = jnp.exp(b_end)
        outs = []
        for h in range(GLA_HEADS):
            dk = slice(h * GLA_DK, (h + 1) * GLA_DK)
            dv = slice(h * GLA_DV, (h + 1) * GLA_DV)
            st = st_ref[bi, h]
            o = _dot_nt(qd[:, dk], st.astype(BF16))
            for s in range(SUB):
                o = o + att[s * SUB:(s + 1) * SUB, h:h + 1] * v[s:s + 1, dv]
            st_ref[bi, h] = st * dec[:, dk] + _dot_tn(vb[:, dv], kd[:, dk])
            outs.append(_rms_rows(o, gn))
        o_ref[bi, rows, :] = jnp.concatenate(outs, axis=-1) * og_ref[bi, rows, :]

    lax.fori_loop(0, n_sub, sub_block, 0)

    @pl.when(ti == pl.num_programs(1) - 1)
    def _():
        sfin_ref[...] = st_ref[...]


def _gla(q, k, v, la, og, gn, s0t, tt, bb, n_valid=None):
    b, t, _ = q.shape
    tri = jnp.asarray(np.tril(np.ones((SUB, SUB), np.float32)))
    hsel = jnp.asarray((np.arange(256)[:, None] // GLA_DK == np.arange(128)[None, :]).astype(np.float32), dtype=BF16)
    seq = lambda w: pl.BlockSpec((bb, tt, w), lambda bi, i: (bi, i, 0))
    full = lambda a: pl.BlockSpec(a.shape, lambda bi, i: (0,) * a.ndim)
    st_spec = pl.BlockSpec((bb, GLA_HEADS, GLA_DV, GLA_DK), lambda bi, i: (bi, 0, 0, 0))
    return pl.pallas_call(
        functools.partial(_gla_kernel, n_sub=tt // SUB, n_valid=n_valid, bb=bb),
        grid=(b // bb, t // tt),
        in_specs=[seq(256), seq(256), seq(512), seq(256), seq(512), full(gn), st_spec, full(tri), full(hsel)],
        out_specs=[seq(512), st_spec],
        out_shape=[jax.ShapeDtypeStruct((b, t, GLA_WIDTH), F32),
                   jax.ShapeDtypeStruct((b, GLA_HEADS, GLA_DV, GLA_DK), F32)],
        scratch_shapes=[pltpu.VMEM((bb, GLA_HEADS, GLA_DV, GLA_DK), F32)],
        compiler_params=_cparams(("arbitrary", "arbitrary")),
        name="gla",
    )(q, k, v, la, og, gn, s0t, tri, hsel)


def _conv_kernel(u_ref, st0_ref, w_ref, b_ref, g_ref, beta_ref, o_ref, st_ref, ext_ref, *, tt, n_valid):
    ti = pl.program_id(1)
    ctx = CONV_WIDTH - 1

    @pl.when(ti == 0)
    def _():
        ext_ref[0:8, :] = jnp.zeros((8, CONV_CH), F32)
        ext_ref[pl.ds(2, ctx), :] = st0_ref[...]

    ext_ref[pl.ds(32, tt), :] = u_ref[...]
    acc = jnp.zeros((tt, CONV_CH), F32)
    for w in range(CONV_WIDTH):
        acc = acc + ext_ref[pl.ds(2 + w, tt), :] * w_ref[w:w + 1, :]
    y = acc + b_ref[...]
    mu = jnp.mean(y, axis=-1, keepdims=True)
    var = jnp.mean(jnp.square(y - mu), axis=-1, keepdims=True)
    ln = (y - mu) * lax.rsqrt(var + EPS) * g_ref[...] + beta_ref[...]
    o_ref[...] = ln * _sigmoid(ln)

    @pl.when(ti == pl.num_programs(1) - 1)
    def _():
        st_ref[...] = ext_ref[pl.ds(32 + n_valid - ctx, ctx), :]

    ext_ref[0:32, :] = ext_ref[pl.ds(tt, 32), :]


def _conv(u, st0, w, b, g, beta, tt, n_valid):
    bsz, t, _ = u.shape
    ctx = CONV_WIDTH - 1
    full = lambda a: pl.BlockSpec(a.shape, lambda bi, i: (0,) * a.ndim)
    st_spec = pl.BlockSpec((None, ctx, CONV_CH), lambda bi, i: (bi, 0, 0))
    return pl.pallas_call(
        functools.partial(_conv_kernel, tt=tt, n_valid=n_valid),
        grid=(bsz, t // tt),
        in_specs=[pl.BlockSpec((None, tt, CONV_CH), lambda bi, i: (bi, i, 0)), st_spec,
                  full(w), full(b), full(g), full(beta)],
        out_specs=[pl.BlockSpec((None, tt, CONV_CH), lambda bi, i: (bi, i, 0)), st_spec],
        out_shape=[jax.ShapeDtypeStruct(u.shape, F32), jax.ShapeDtypeStruct((bsz, ctx, CONV_CH), F32)],
        scratch_shapes=[pltpu.VMEM((32 + max(tt, 32), CONV_CH), F32)],
        compiler_params=_cparams(("arbitrary", "arbitrary")),
        name="conv",
    )(u, st0, w, b, g, beta)


def _fox_kernel(qt_ref, k_ref, vt_ref, ck_ref, cqt_ref, o_ref, m_ref, l_ref, acc_ref, *, tq, tk):
    qi = pl.program_id(1)
    ki = pl.program_id(2)
    last = (qi * tq + tq - 1) // tk

    @pl.when(ki == 0)
    def _():
        m_ref[...] = jnp.full_like(m_ref, NEG)
        l_ref[...] = jnp.zeros_like(l_ref)
        acc_ref[...] = jnp.zeros_like(acc_ref)

    def tile(neg):
        for h in range(FOX_HEADS):
            cols = slice(h * HEAD_DIM, (h + 1) * HEAD_DIM)
            s = jnp.dot(k_ref[:, cols], qt_ref[cols, :], preferred_element_type=F32)
            s = s + cqt_ref[h:h + 1, :] - ck_ref[:, h:h + 1]
            if neg is not None:
                s = s + neg
            m_old = m_ref[h:h + 1, :]
            m_new = jnp.maximum(m_old, jnp.max(s, axis=0, keepdims=True))
            p = jnp.exp(s - m_new)
            alpha = jnp.exp(m_old - m_new)
            l_ref[h:h + 1, :] = alpha * l_ref[h:h + 1, :] + jnp.sum(p, axis=0, keepdims=True)
            acc_ref[cols, :] = alpha * acc_ref[cols, :] + jnp.dot(vt_ref[cols, :], p.astype(BF16),
                                                                  preferred_element_type=F32)
            m_ref[h:h + 1, :] = m_new

    @pl.when(ki < last)
    def _():
        tile(None)

    @pl.when(ki == last)
    def _():
        s_pos = ki * tk + lax.broadcasted_iota(jnp.int32, (tk, tq), 0)
        t_pos = qi * tq + lax.broadcasted_iota(jnp.int32, (tk, tq), 1)
        tile(jnp.where(s_pos <= t_pos, 0.0, NEG))

    @pl.when(ki == pl.num_programs(2) - 1)
    def _():
        for h in range(FOX_HEADS):
            cols = slice(h * HEAD_DIM, (h + 1) * HEAD_DIM)
            acc_ref[cols, :] = acc_ref[cols, :] / jnp.maximum(l_ref[h:h + 1, :], 1e-30)
        o_ref[...] = acc_ref[...].T


def _fox_prompt(qst, kb, vbt, c8, ct, tq, tk):
    b, t, _ = kb.shape
    nq, nk = t // tq, t // tk
    kmin = lambda qi, ki: jnp.minimum(ki, (qi * tq + tq - 1) // tk)
    return pl.pallas_call(
        functools.partial(_fox_kernel, tq=tq, tk=tk),
        grid=(b, nq, nk),
        in_specs=[pl.BlockSpec((FOX_WIDTH, tq), lambda bi, qi, ki: (0, bi * nq + qi)),
                  pl.BlockSpec((None, tk, FOX_WIDTH), lambda bi, qi, ki: (bi, kmin(qi, ki), 0)),
                  pl.BlockSpec((FOX_WIDTH, tk), lambda bi, qi, ki: (0, bi * nk + kmin(qi, ki))),
                  pl.BlockSpec((None, tk, FOX_HEADS), lambda bi, qi, ki: (bi, kmin(qi, ki), 0)),
                  pl.BlockSpec((None, FOX_HEADS, tq), lambda bi, qi, ki: (bi, 0, qi))],
        out_specs=pl.BlockSpec((None, tq, FOX_WIDTH), lambda bi, qi, ki: (bi, qi, 0)),
        out_shape=jax.ShapeDtypeStruct((b, t, FOX_WIDTH), F32),
        scratch_shapes=[pltpu.VMEM((FOX_HEADS, tq), F32), pltpu.VMEM((FOX_HEADS, tq), F32),
                        pltpu.VMEM((FOX_WIDTH, tq), F32)],
        compiler_params=_cparams(("arbitrary", "arbitrary", "arbitrary"), VMEM_LIMIT_BYTES),
        name="fox_prompt",
    )(qst, kb, vbt, c8, ct)


def _compress_weights(w1):
    w = w1.reshape(2, 2, 8, 2, HEAD_DIM, CMP_HIDDEN)
    z = jnp.zeros_like(w)
    g0 = jnp.concatenate([w, z], axis=-1)
    g1 = jnp.concatenate([z, w], axis=-1)
    wbd = jnp.stack([g0, g1], axis=4)
    return wbd.reshape(2, 2, 8, 4 * HEAD_DIM, 2 * CMP_HIDDEN).astype(BF16)


def _compress_compute(src_refs, pe_ref, w1_ref, w2_ref, kn_ref, ck_ref, cv_ref, sh_ref, nseg):
    sh_ref[pl.ds(nseg, 8), :] = jnp.zeros((8, CMP_HIDDEN), F32)
    for j, src_ref in enumerate(src_refs):
        a = jnp.zeros((nseg, 2 * CMP_HIDDEN), F32)
        bm = jnp.zeros((nseg, 2 * CMP_HIDDEN), F32)
        for q in range(CMP_STRIDE // 2):
            p0, p1 = 2 * q, 2 * q + 1
            x0 = src_ref[pl.ds(p0, nseg, stride=CMP_STRIDE), :]
            x1 = src_ref[pl.ds(p1, nseg, stride=CMP_STRIDE), :]
            xa = jnp.concatenate([x0 + pe_ref[j, p0:p0 + 1, :], x1 + pe_ref[j, p1:p1 + 1, :]], axis=1)
            a = a + jnp.dot(xa.astype(BF16), w1_ref[j, 0, q], preferred_element_type=F32)
            p0, p1 = p0 + CMP_STRIDE, p1 + CMP_STRIDE
            xb = jnp.concatenate([x0 + pe_ref[j, p0:p0 + 1, :], x1 + pe_ref[j, p1:p1 + 1, :]], axis=1)
            bm = bm + jnp.dot(xb.astype(BF16), w1_ref[j, 1, q], preferred_element_type=F32)
        for g in range(NSA_KV_HEADS):
            sh_ref[pl.ds(0, nseg), :] = bm[:, g * CMP_HIDDEN:(g + 1) * CMP_HIDDEN]
            x = a[:, g * CMP_HIDDEN:(g + 1) * CMP_HIDDEN] + sh_ref[pl.ds(1, nseg), :]
            hid = x * (0.5 * (1.0 + jnp.tanh(math.sqrt(2.0 / math.pi) * (x + 0.044715 * (x * x * x)))))
            ckv = jnp.dot(hid.astype(BF16), w2_ref[j], preferred_element_type=F32)
            if j == 0:
                ck_ref[g] = _rms_rows(ckv, kn_ref[...]).astype(BF16)
            else:
                cv_ref[g] = ckv.astype(BF16)


def _compress_prompt_kernel(xk_ref, xv_ref, pe_ref, w1_ref, w2_ref, kn_ref, ck_ref, cv_ref, sh_ref, *, nseg):
    _compress_compute((xk_ref, xv_ref), pe_ref, w1_ref, w2_ref, kn_ref, ck_ref, cv_ref, sh_ref, nseg)


def _compress_prompt(kvr, pe, w1, w2, kn):
    b, t, _ = kvr.shape
    nseg = t // CMP_STRIDE
    full = lambda a: pl.BlockSpec(a.shape, lambda i: (0,) * a.ndim)
    o_spec = pl.BlockSpec((None, NSA_KV_HEADS, nseg, HEAD_DIM), lambda i: (i, 0, 0, 0))
    o_shape = jax.ShapeDtypeStruct((b, NSA_KV_HEADS, nseg, HEAD_DIM), BF16)
    return pl.pallas_call(
        functools.partial(_compress_prompt_kernel, nseg=nseg),
        grid=(b,),
        in_specs=[pl.BlockSpec((None, t, LANES), lambda i: (i, 0, 0)), pl.BlockSpec((None, t, LANES), lambda i: (i, 0, 1)),
                  full(pe), full(w1), full(w2), full(kn)],
        out_specs=[o_spec, o_spec],
        out_shape=[o_shape, o_shape],
        scratch_shapes=[pltpu.VMEM((nseg + 8, CMP_HIDDEN), F32)],
        compiler_params=_cparams(("arbitrary",), VMEM_LIMIT_BYTES),
        name="nsa_compress",
    )(kvr, kvr, pe, w1, w2, kn)


def _stack_heads(qs_ref, kh):
    parts = [qs_ref[:, (kh * NSA_GROUP + g) * HEAD_DIM:(kh * NSA_GROUP + g + 1) * HEAD_DIM].astype(F32)
             for g in range(NSA_GROUP)]
    return jnp.concatenate(parts, axis=0).astype(BF16)


def _cmp_topk_kernel(qs_ref, ck_ref, cv_ref, farcol_ref, chi_ref, clo_ref, pool_ref, ocmp_ref, msk_ref,
                     *, tq, nseg, q_base, n_pick, n_blk):
    qi = pl.program_id(1)
    G = NSA_GROUP
    q0 = q_base + qi * tq
    nbase = q0 // CMP_STRIDE - 16
    place = (lax.broadcasted_iota(jnp.int32, (32, nseg), 1) - lax.broadcasted_iota(jnp.int32, (32, nseg), 0)) == nbase
    place = jnp.where(place, 1.0, 0.0).astype(BF16)
    t1 = q0 + lax.broadcasted_iota(jnp.int32, (tq, 1), 0)
    t4 = jnp.concatenate([t1] * G, axis=0)
    n_i = lax.broadcasted_iota(jnp.int32, (G * tq, nseg), 1)
    valid = (n_i * CMP_STRIDE + (CMP_BLOCK - 1) <= t4) & (n_i <= nseg - 2)
    blk = lax.broadcasted_iota(jnp.int32, (tq, N_SELBLK), 1)
    cur = lax.shift_right_logical(t1, 6)
    forced = (blk == 0) | (blk == cur) | (blk == cur - 1)
    for kh in range(NSA_KV_HEADS):
        q4 = _stack_heads(qs_ref, kh)
        s = _dot_nt(q4, ck_ref[kh]) + farcol_ref[kh]
        s = s + jnp.dot(chi_ref[kh], place, preferred_element_type=F32) + jnp.dot(clo_ref[kh], place, preferred_element_type=F32)
        s = jnp.where(valid, s, NEG)
        e = jnp.where(valid, jnp.exp(s - jnp.max(s, axis=-1, keepdims=True)), 0.0)
        p = e / jnp.maximum(jnp.sum(e, axis=-1, keepdims=True), 1e-30)
        o = jnp.dot(p.astype(BF16), cv_ref[kh], preferred_element_type=F32)
        for g in range(G):
            h = kh * G + g
            ocmp_ref[:, h * HEAD_DIM:(h + 1) * HEAD_DIM] = o[g * tq:(g + 1) * tq]
        imp = p[0:tq] + p[tq:2 * tq] + p[2 * tq:3 * tq] + p[3 * tq:4 * tq]
        pooled = _dot_f32(imp, pool_ref[...])
        score = jnp.where((blk > cur) | (blk >= n_blk), -1e30, jnp.where(forced, 1e30, pooled))
        sel = jnp.zeros((tq, N_SELBLK), F32)
        for _ in range(n_pick):
            mx = jnp.max(score, axis=-1, keepdims=True)
            first = jnp.min(jnp.where(score == mx, blk, N_SELBLK), axis=-1, keepdims=True)
            pick = blk == first
            sel = jnp.where(pick, 1.0, sel)
            score = jnp.where(pick, -3e38, score)
        msk_ref[kh] = sel


def _cmp_topk_t_kernel(qt_ref, ck_ref, cv_ref, farrow_ref, chit_ref, clot_ref, poolt_ref, ocmpt_ref, mskt_ref,
                       *, tq, nseg, n_pick, n_blk):
    qi = pl.program_id(1)
    G = NSA_GROUP
    q0 = qi * tq
    nbase = q0 // CMP_STRIDE - 16
    t1 = q0 + lax.broadcasted_iota(jnp.int32, (1, tq), 1)
    blk = lax.broadcasted_iota(jnp.int32, (N_SELBLK, tq), 0)
    cur = lax.shift_right_logical(t1, 6)
    forced = (blk == 0) | (blk == cur) | (blk == cur - 1)

    def body(nv):
        place_t = (lax.broadcasted_iota(jnp.int32, (nv, 32), 0) - lax.broadcasted_iota(jnp.int32, (nv, 32), 1)) == nbase
        place_t = jnp.where(place_t, 1.0, 0.0).astype(BF16)
        n_i = lax.broadcasted_iota(jnp.int32, (nv, tq), 0)
        valid1 = (n_i * CMP_STRIDE + (CMP_BLOCK - 1) <= t1) & (n_i <= nseg - 2)
        valid = jnp.concatenate([valid1] * G, axis=1)
        for kh in range(NSA_KV_HEADS):
            qt4 = jnp.concatenate([qt_ref[(kh * G + g) * HEAD_DIM:(kh * G + g + 1) * HEAD_DIM, :] for g in range(G)],
                                  axis=1)
            s = jnp.dot(ck_ref[kh, 0:nv, :], qt4, preferred_element_type=F32) + farrow_ref[kh]
            s = s + jnp.dot(place_t, chit_ref[kh], preferred_element_type=F32) \
                  + jnp.dot(place_t, clot_ref[kh], preferred_element_type=F32)
            s = jnp.where(valid, s, NEG)
            e = jnp.where(valid, jnp.exp(s - jnp.max(s, axis=0, keepdims=True)), 0.0)
            p = e * (1.0 / jnp.maximum(jnp.sum(e, axis=0, keepdims=True), 1e-30))
            o_t = _dot_tn(cv_ref[kh, 0:nv, :], p.astype(BF16))
            for g in range(G):
                h = kh * G + g
                ocmpt_ref[h * HEAD_DIM:(h + 1) * HEAD_DIM, :] = o_t[:, g * tq:(g + 1) * tq]
            imp = p[:, 0:tq] + p[:, tq:2 * tq] + p[:, 2 * tq:3 * tq] + p[:, 3 * tq:4 * tq]
            pooled = _dot_f32(poolt_ref[:, 0:nv], imp)
            score = jnp.where((blk > cur) | (blk >= n_blk), -1e30, jnp.where(forced, 1e30, pooled))
            sel = jnp.zeros((N_SELBLK, tq), F32)
            for _ in range(n_pick):
                mx = jnp.max(score, axis=0, keepdims=True)
                first = jnp.min(jnp.where(score == mx, blk, N_SELBLK), axis=0, keepdims=True)
                pick = blk == first
                sel = jnp.where(pick, 1.0, sel)
                score = jnp.where(pick, -3e38, score)
            mskt_ref[kh] = sel

    n_chunks = nseg // CMP_CHUNK
    need = jnp.minimum(jnp.maximum(q0 + tq - CMP_BLOCK, 0) // CMP_STRIDE // CMP_CHUNK + 1, n_chunks)
    for v in range(1, n_chunks + 1):
        pl.when(need == v)(functools.partial(body, v * CMP_CHUNK))


def _cmp_topk_prompt(nqst, ck, cv, tbl, b, t, tq, n_pick):
    nseg = ck.shape[2]
    assert tq <= 256 and ck.shape[0] == b and nseg % CMP_CHUNK == 0
    farcol, chi, clo = _cmp_bias_tables(tbl, tq)
    farrow, chit, clot = (jnp.swapaxes(a, 1, 2) for a in (farcol, chi, clo))
    poolt = jnp.asarray((np.arange(N_SELBLK)[:, None] == np.arange(nseg)[None, :] // SEL_RATIO).astype(np.float32))
    full = lambda a: pl.BlockSpec(a.shape, lambda bi, i: (0,) * a.ndim)
    c_spec = pl.BlockSpec((None, NSA_KV_HEADS, nseg, HEAD_DIM), lambda bi, i: (bi, 0, 0, 0))
    nq = t // tq
    col_tile = pl.BlockSpec((NSA_WIDTH, tq), lambda bi, i: (0, bi * nq + i))
    return pl.pallas_call(
        functools.partial(_cmp_topk_t_kernel, tq=tq, nseg=nseg, n_pick=n_pick, n_blk=t // SEL_BLOCK),
        grid=(b, nq),
        in_specs=[col_tile, c_spec, c_spec, full(farrow), full(chit), full(clot), full(poolt)],
        out_specs=[col_tile, pl.BlockSpec((None, NSA_KV_HEADS, N_SELBLK, tq), lambda bi, i: (bi, 0, 0, i))],
        out_shape=[jax.ShapeDtypeStruct((NSA_WIDTH, b * t), F32),
                   jax.ShapeDtypeStruct((b, NSA_KV_HEADS, N_SELBLK, t), F32)],
        compiler_params=_cparams(("arbitrary", "arbitrary"), VMEM_LIMIT_BYTES),
        name="nsa_cmp_topk",
    )(nqst, ck, cv, farrow, chit, clot, poolt)


def _rel_bucket(dist):
    exact = NUM_BUCKETS // 2
    d = jnp.maximum(dist, 0)
    log_ratio = jnp.log(jnp.maximum(d, 1).astype(jnp.float32) / exact) / math.log(MAX_DISTANCE / exact)
    large = jnp.minimum(exact + (log_ratio * (NUM_BUCKETS - exact)).astype(jnp.int32), NUM_BUCKETS - 1)
    return jnp.where(d < exact, d, large)


def _bias_lookup(tbl, dist):
    onehot = _rel_bucket(dist)[..., None] == jnp.arange(NUM_BUCKETS)
    t = tbl.reshape((tbl.shape[0],) + (1,) * dist.ndim + (NUM_BUCKETS,))
    return jnp.sum(jnp.where(onehot[None], t, 0.0), axis=-1)


def _cmp_bias_tables(tbl, tq):
    tr = jnp.arange(tq)[:, None]
    i = jnp.arange(32)[None, :]
    dist = tr + 16 * CMP_STRIDE - CMP_STRIDE * i - (CMP_BLOCK - 1)
    near = _bias_lookup(tbl, dist)
    far = tbl[:, NUM_BUCKETS - 1]
    corr = (near - far[:, None, None]).reshape(NSA_KV_HEADS, NSA_GROUP * tq, 32)
    hi = corr.astype(BF16)
    lo = (corr - hi.astype(F32)).astype(BF16)
    farcol = jnp.broadcast_to(far[:, None, None], (NSA_HEADS, tq, 1)).reshape(NSA_KV_HEADS, NSA_GROUP * tq, 1)
    return farcol, hi, lo


def _cmp_topk_sample(qs, ck, cv, tbl, tq, q_base, n_pick, n_blk):
    b, t, _ = qs.shape
    nseg = ck.shape[2]
    assert tq <= 256
    farcol, chi, clo = _cmp_bias_tables(tbl, tq)
    pool = jnp.asarray((np.arange(nseg)[:, None] // SEL_RATIO == np.arange(N_SELBLK)[None, :]).astype(np.float32))
    full = lambda a: pl.BlockSpec(a.shape, lambda bi, i: (0,) * a.ndim)
    c_spec = pl.BlockSpec((None, NSA_KV_HEADS, nseg, HEAD_DIM), lambda bi, i: (bi, 0, 0, 0))
    m_spec = pl.BlockSpec((None, NSA_KV_HEADS, tq, N_SELBLK), lambda bi, i: (bi, 0, i, 0))
    m_shape = (b, NSA_KV_HEADS, t, N_SELBLK)
    return pl.pallas_call(
        functools.partial(_cmp_topk_kernel, tq=tq, nseg=nseg, q_base=q_base, n_pick=n_pick, n_blk=n_blk),
        grid=(b, t // tq),
        in_specs=[pl.BlockSpec((None, tq, NSA_WIDTH), lambda bi, i: (bi, i, 0)), c_spec, c_spec,
                  full(farcol), full(chi), full(clo), full(pool)],
        out_specs=[pl.BlockSpec((None, tq, NSA_WIDTH), lambda bi, i: (bi, i, 0)), m_spec],
        out_shape=[jax.ShapeDtypeStruct((b, t, NSA_WIDTH), F32), jax.ShapeDtypeStruct(m_shape, F32)],
        compiler_params=_cparams(("arbitrary", "arbitrary"), VMEM_LIMIT_BYTES),
        name="nsa_cmp_topk",
    )(qs, ck, cv, farcol, chi, clo, pool)


def _online_update(s, valid, v, m_ref, l_ref, acc_ref, v_t=False):
    s = jnp.where(valid, s, NEG)
    m_old = m_ref[...]
    m_new = jnp.maximum(m_old, jnp.max(s, axis=-1, keepdims=True))
    p = jnp.where(valid, jnp.exp(s - m_new), 0.0)
    alpha = jnp.exp(m_old - m_new)
    l_ref[...] = alpha * l_ref[...] + jnp.sum(p, axis=-1, keepdims=True)
    pb = p.astype(BF16)
    pv = _dot_nt(pb, v) if v_t else jnp.dot(pb, v, preferred_element_type=F32)
    acc_ref[...] = alpha * acc_ref[...] + pv
    m_ref[...] = m_new


def _selwin_kernel(qt_ref, selk_ref, selvt_ref, wink_ref, winvt_ref, mskt_ref, bias_ref, ocmpt_ref, gtt_ref, o_ref,
                   m_ref, l_ref, acc_ref, ot_ref, qt4_ref, *, tq):
    qi = pl.program_id(1)
    G = NSA_GROUP
    blocks_per_tile = tq // SEL_BLOCK
    s_rel = lax.broadcasted_iota(jnp.int32, (tq, tq), 0)
    t_rel = lax.broadcasted_iota(jnp.int32, (tq, tq), 1)
    causal = s_rel <= t_rel
    all_rows, near_rows, diag_rows = slice(0, 3 * tq), slice(tq, 3 * tq), slice(2 * tq, 3 * tq)

    def init():
        m_ref[...] = jnp.full_like(m_ref, NEG)
        l_ref[...] = jnp.zeros_like(l_ref)
        acc_ref[...] = jnp.zeros_like(acc_ref)

    def update(kh, k, vt, bias_rows, valid):
        s = jnp.dot(k, qt4_ref[...], preferred_element_type=F32)
        if bias_rows is not None:
            s = s + bias_ref[kh, bias_rows, :]
        if valid is not None:
            neg = jnp.where(valid, 0.0, NEG)
            s = s + jnp.concatenate([neg] * G, axis=1)
        m_old = m_ref[0:1, :]
        m_new = jnp.maximum(m_old, jnp.max(s, axis=0, keepdims=True))
        p = jnp.exp(s - m_new)
        alpha = jnp.exp(m_old - m_new)
        l_ref[0:1, :] = alpha * l_ref[0:1, :] + jnp.sum(p, axis=0, keepdims=True)
        acc_ref[...] = alpha * acc_ref[...] + jnp.dot(vt, p.astype(BF16), preferred_element_type=F32)
        m_ref[0:1, :] = m_new

    def finish(kh, gate_row):
        for g in range(G):
            h = kh * G + g
            cols = slice(g * tq, (g + 1) * tq)
            rows = slice(h * HEAD_DIM, (h + 1) * HEAD_DIM)
            o = acc_ref[:, cols] / jnp.maximum(l_ref[0:1, cols], 1e-30)
            r = 3 * h + gate_row
            ot_ref[rows, :] += gtt_ref[r:r + 1, :] * o

    for h in range(NSA_HEADS):
        rows = slice(h * HEAD_DIM, (h + 1) * HEAD_DIM)
        ot_ref[rows, :] = gtt_ref[3 * h:3 * h + 1, :] * ocmpt_ref[rows, :]

    for kh in range(NSA_KV_HEADS):
        kcols = slice(kh * HEAD_DIM, (kh + 1) * HEAD_DIM)
        vrows = slice(kh * HEAD_DIM, (kh + 1) * HEAD_DIM)
        for g in range(G):
            h = kh * G + g
            qt4_ref[:, g * tq:(g + 1) * tq] = qt_ref[h * HEAD_DIM:(h + 1) * HEAD_DIM, :]

        def sel_valid(j):
            parts = [jnp.broadcast_to(mskt_ref[kh, pl.ds(j * blocks_per_tile + i, 1), :], (SEL_BLOCK, tq))
                     for i in range(blocks_per_tile)]
            return jnp.concatenate(parts, axis=0) > 0.5

        def sel_update(j0, n_tiles, bias_rows, last_is_diag):
            start = pl.multiple_of(j0 * tq, tq)
            parts = [sel_valid(j0 + i) for i in range(n_tiles)]
            if last_is_diag:
                parts[-1] = parts[-1] & causal
            update(kh, selk_ref[pl.ds(start, n_tiles * tq), kcols], selvt_ref[vrows, pl.ds(start, n_tiles * tq)],
                   bias_rows, jnp.concatenate(parts, axis=0))

        def win_update(j0, valid_parts, bias_rows):
            n = len(valid_parts) * tq
            start = pl.multiple_of(j0 * tq, tq)
            update(kh, wink_ref[pl.ds(start, n), kcols], winvt_ref[vrows, pl.ds(start, n)], bias_rows,
                   jnp.concatenate(valid_parts, axis=0))

        init()

        def far_group(jg, c):
            sel_update(jg * FAR_GROUP, FAR_GROUP, None, False)
            return c

        def far_single(j, c):
            sel_update(j, 1, None, False)
            return c

        n_far = jnp.maximum(qi - 1, 0)
        n_grp = n_far // FAR_GROUP
        lax.fori_loop(0, n_grp, far_group, 0)
        lax.fori_loop(n_grp * FAR_GROUP, n_far, far_single, 0)

        @pl.when(qi >= 1)
        def _():
            sel_update(qi - 1, 2, near_rows, True)

        @pl.when(qi == 0)
        def _():
            sel_update(0, 1, diag_rows, True)

        finish(kh, 1)

        init()
        all_valid = s_rel >= 0

        @pl.when(qi >= 2)
        def _():
            win_update(qi - 2, [s_rel >= t_rel, all_valid, causal], all_rows)

        @pl.when(qi == 1)
        def _():
            win_update(0, [all_valid, causal], near_rows)

        @pl.when(qi == 0)
        def _():
            win_update(0, [causal], diag_rows)

        finish(kh, 2)

    o_ref[...] = ot_ref[...].T


def _selwin_bias_tables(tbl, tq):
    sr = jnp.arange(tq)[:, None]
    tr = jnp.arange(tq)[None, :]
    far = tbl[:, NUM_BUCKETS - 1][:, None, None]
    near0 = _bias_lookup(tbl, tr - sr) - far
    near1 = _bias_lookup(tbl, tr - sr + tq) - far
    b = jnp.concatenate([jnp.zeros_like(near1), near1, near0], axis=1)
    b = b.reshape(NSA_KV_HEADS, NSA_GROUP, 3 * tq, tq).transpose(0, 2, 1, 3)
    return b.reshape(NSA_KV_HEADS, 3 * tq, NSA_GROUP * tq)


def _selwin_prompt(nqst, selkv, selvt, winkv, winvt, mskt, tbl, ocmp, gtt, b, t, tq):
    assert tq >= NSA_WINDOW // 2 and tq >= MAX_DISTANCE and tq % SEL_BLOCK == 0
    bias = _selwin_bias_tables(tbl, tq)
    nq = t // tq
    col_tile = lambda h: pl.BlockSpec((h, tq), lambda bi, i: (0, bi * nq + i))
    row_tile = lambda w: pl.BlockSpec((tq, w), lambda bi, i: (bi * nq + i, 0))
    return pl.pallas_call(
        functools.partial(_selwin_kernel, tq=tq),
        grid=(b, nq),
        in_specs=[col_tile(NSA_WIDTH),
                  pl.BlockSpec((t, LANES), lambda bi, i: (bi, 0)), pl.BlockSpec((LANES, t), lambda bi, i: (0, bi)),
                  pl.BlockSpec((t, LANES), lambda bi, i: (bi, 0)), pl.BlockSpec((LANES, t), lambda bi, i: (0, bi)),
                  pl.BlockSpec((None, NSA_KV_HEADS, N_SELBLK, tq), lambda bi, i: (bi, 0, 0, i)),
                  pl.BlockSpec(bias.shape, lambda bi, i: (0, 0, 0)),
                  col_tile(NSA_WIDTH), col_tile(32)],
        out_specs=row_tile(NSA_WIDTH),
        out_shape=jax.ShapeDtypeStruct((b * t, NSA_WIDTH), F32),
        scratch_shapes=[pltpu.VMEM((8, NSA_GROUP * tq), F32), pltpu.VMEM((8, NSA_GROUP * tq), F32),
                        pltpu.VMEM((HEAD_DIM, NSA_GROUP * tq), F32), pltpu.VMEM((NSA_WIDTH, tq), F32),
                        pltpu.VMEM((HEAD_DIM, NSA_GROUP * tq), BF16)],
        compiler_params=_cparams(("arbitrary", "arbitrary"), VMEM_LIMIT_BYTES),
        name="nsa_selwin",
    )(nqst, selkv, selvt, winkv, winvt, mskt, bias, ocmp, gtt)


PAGES_PER_STEP = 16
PAGE = 128


def _feature_major(cache):
    nd = cache.ndim
    t = jnp.transpose(cache, (0, 1) + tuple(range(3, nd)) + (2,))
    return t.reshape(cache.shape[0], cache.shape[1], -1, cache.shape[2])


def _page_specs(layer, rows, row_block, reverse_steps=None):
    group = (lambda j: j) if reverse_steps is None else (lambda j: reverse_steps - 1 - j)
    return [pl.BlockSpec((None, None, rows, PAGE), functools.partial(
        lambda bi, j, pt, r: (layer, pt[bi, group(j) * PAGES_PER_STEP + r], row_block, 0), r=r))
        for r in range(PAGES_PER_STEP)]


def _compress_sample_kernel(pt_ref, *refs, nseg):
    pages = refs[:PAGES_PER_STEP]
    pe_ref, w1_ref, w2_ref, kn_ref, ck_ref, cv_ref, srck_ref, srcv_ref, sh_ref = refs[PAGES_PER_STEP:]
    j = pl.program_id(1)
    for r, p_ref in enumerate(pages):
        rows = pl.ds(pl.multiple_of((j * PAGES_PER_STEP + r) * PAGE, PAGE), PAGE)
        srck_ref[rows, :] = p_ref[0:LANES, :].T
        srcv_ref[rows, :] = p_ref[LANES:2 * LANES, :].T

    @pl.when(j == pl.num_programs(1) - 1)
    def _():
        _compress_compute((srck_ref, srcv_ref), pe_ref, w1_ref, w2_ref, kn_ref, ck_ref, cv_ref, sh_ref, nseg)


def _compress_sample(cache_t, layer, page_table, pe, w1, w2, kn):
    sb, n_pages = page_table.shape
    past = n_pages * PAGE
    nseg = past // CMP_STRIDE
    full = lambda a: pl.BlockSpec(a.shape, lambda bi, j, pt: (0,) * a.ndim)
    o_spec = pl.BlockSpec((None, NSA_KV_HEADS, nseg, HEAD_DIM), lambda bi, j, pt: (bi, 0, 0, 0))
    o_shape = jax.ShapeDtypeStruct((sb, NSA_KV_HEADS, nseg, HEAD_DIM), BF16)
    return pl.pallas_call(
        functools.partial(_compress_sample_kernel, nseg=nseg),
        grid_spec=pltpu.PrefetchScalarGridSpec(
            num_scalar_prefetch=1, grid=(sb, n_pages // PAGES_PER_STEP),
            in_specs=_page_specs(layer, 256, 0) + [full(pe), full(w1), full(w2), full(kn)],
            out_specs=[o_spec, o_spec],
            scratch_shapes=[pltpu.VMEM((past, LANES), F32), pltpu.VMEM((past, LANES), F32),
                            pltpu.VMEM((nseg + 8, CMP_HIDDEN), F32)]),
        out_shape=[o_shape, o_shape],
        compiler_params=_cparams(("arbitrary", "arbitrary"), VMEM_LIMIT_BYTES),
        name="nsa_compress_sample",
    )(page_table, *([cache_t] * PAGES_PER_STEP), pe, w1, w2, kn)


def _selwin_sample_kernel(pt_ref, qs_ref, *refs):
    pages = refs[:PAGES_PER_STEP]
    (msk_ref, newkv_ref, winst_ref, newwr_ref, bsel_ref, bnew_ref, bwin_ref, expand_ref, ocmp_ref, gt_ref, o_ref,
     kv_ref, m_ref, l_ref, acc_ref, osel_ref, owin_ref) = refs[PAGES_PER_STEP:]
    j = pl.program_id(1)
    G = NSA_GROUP
    R = SAMPLE_ROWS
    for r, p_ref in enumerate(pages):
        kv_ref[:, pl.ds(pl.multiple_of((j * PAGES_PER_STEP + r) * PAGE, PAGE), PAGE)] = p_ref[...].astype(BF16)

    @pl.when(j == pl.num_programs(1) - 1)
    def _():
        rq = lax.broadcasted_iota(jnp.int32, (G * R, 1), 0) & (R - 1)
        new_valid = lax.broadcasted_iota(jnp.int32, (G * R, R), 1) <= rq
        win_valid = lax.broadcasted_iota(jnp.int32, (G * R, NSA_WINDOW), 1) >= rq

        def init():
            m_ref[...] = jnp.full_like(m_ref, NEG)
            l_ref[...] = jnp.zeros_like(l_ref)
            acc_ref[...] = jnp.zeros_like(acc_ref)

        def finish(dst_ref, kh):
            o = acc_ref[...] / jnp.maximum(l_ref[...], 1e-30)
            for g in range(G):
                h = kh * G + g
                dst_ref[:, h * HEAD_DIM:(h + 1) * HEAD_DIM] = o[g * R:(g + 1) * R]

        for kh in range(NSA_KV_HEADS):
            kcols = slice(kh * HEAD_DIM, (kh + 1) * HEAD_DIM)
            vcols = slice(128 + kh * HEAD_DIM, 128 + (kh + 1) * HEAD_DIM)
            q4 = _stack_heads(qs_ref, kh)
            mskb = msk_ref[kh].astype(BF16)

            init()
            mt = jnp.dot(mskb, expand_ref[...], preferred_element_type=F32) > 0.5
            valid = jnp.concatenate([mt] * G, axis=0)
            s = jnp.dot(q4, kv_ref[kcols, :], preferred_element_type=F32) + bsel_ref[kh]
            _online_update(s, valid, kv_ref[vcols, :], m_ref, l_ref, acc_ref, v_t=True)
            knew = newkv_ref[:, 256 + kh * HEAD_DIM:256 + (kh + 1) * HEAD_DIM].astype(BF16)
            vnew = newkv_ref[:, 384 + kh * HEAD_DIM:384 + (kh + 1) * HEAD_DIM].astype(BF16)
            _online_update(_dot_nt(q4, knew) + bnew_ref[kh], new_valid, vnew, m_ref, l_ref, acc_ref)
            finish(osel_ref, kh)

            init()
            kwin = winst_ref[kcols, :].astype(BF16)
            vwin = winst_ref[vcols, :].astype(BF16)
            _online_update(jnp.dot(q4, kwin, preferred_element_type=F32) + bwin_ref[kh], win_valid, vwin,
                           m_ref, l_ref, acc_ref, v_t=True)
            knew = newwr_ref[:, kcols].astype(BF16)
            vnew = newwr_ref[:, vcols].astype(BF16)
            _online_update(_dot_nt(q4, knew) + bnew_ref[kh], new_valid, vnew, m_ref, l_ref, acc_ref)
            finish(owin_ref, kh)

        for h in range(NSA_HEADS):
            cols = slice(h * HEAD_DIM, (h + 1) * HEAD_DIM)
            o_ref[:, cols] = (gt_ref[:, 3 * h:3 * h + 1] * ocmp_ref[:, cols]
                              + gt_ref[:, 3 * h + 1:3 * h + 2] * osel_ref[:, cols]
                              + gt_ref[:, 3 * h + 2:3 * h + 3] * owin_ref[:, cols])


def _sample_bias_tables(tbl, past):
    R = SAMPLE_ROWS
    r = jnp.arange(R)[:, None]
    stack = lambda a: a.reshape(NSA_KV_HEADS, NSA_GROUP * R, a.shape[-1])
    cached = _bias_lookup(tbl, past + r - jnp.arange(past)[None, :])
    new = _bias_lookup(tbl, r - jnp.arange(R)[None, :])
    win = _bias_lookup(tbl, NSA_WINDOW + r - jnp.arange(NSA_WINDOW)[None, :])
    return stack(cached), stack(new), stack(win)


def _selwin_sample(qs, cache_t, layer, page_table, msk, newkv, win_t, newwr, tbl, ocmp, gt):
    sb, n_pages = page_table.shape
    past = n_pages * PAGE
    assert win_t.shape[-1] == NSA_WINDOW and past >= NSA_WINDOW and past // SEL_BLOCK <= N_SELBLK
    bsel, bnew, bwin = _sample_bias_tables(tbl, past)
    expand = jnp.asarray((np.arange(N_SELBLK)[:, None] == np.arange(past)[None, :] // SEL_BLOCK).astype(np.float32),
                         dtype=BF16)
    R = SAMPLE_ROWS
    full = lambda a: pl.BlockSpec(a.shape, lambda bi, j, pt: (0,) * a.ndim)
    seq = lambda a: pl.BlockSpec((None,) + a.shape[1:], lambda bi, j, pt: (bi,) + (0,) * (a.ndim - 1))
    win_spec = pl.BlockSpec((None, None) + win_t.shape[2:], lambda bi, j, pt: (layer, bi, 0, 0))
    return pl.pallas_call(
        _selwin_sample_kernel,
        grid_spec=pltpu.PrefetchScalarGridSpec(
            num_scalar_prefetch=1, grid=(sb, n_pages // PAGES_PER_STEP),
            in_specs=[seq(qs)] + _page_specs(layer, 256, 1) + [seq(msk), seq(newkv), win_spec, seq(newwr),
                                                               full(bsel), full(bnew), full(bwin), full(expand),
                                                               seq(ocmp), seq(gt)],
            out_specs=pl.BlockSpec((None, R, NSA_WIDTH), lambda bi, j, pt: (bi, 0, 0)),
            scratch_shapes=[pltpu.VMEM((256, past), BF16),
                            pltpu.VMEM((NSA_GROUP * R, 1), F32), pltpu.VMEM((NSA_GROUP * R, 1), F32),
                            pltpu.VMEM((NSA_GROUP * R, HEAD_DIM), F32),
                            pltpu.VMEM((R, NSA_WIDTH), F32), pltpu.VMEM((R, NSA_WIDTH), F32)]),
        out_shape=jax.ShapeDtypeStruct((sb, R, NSA_WIDTH), F32),
        compiler_params=_cparams(("arbitrary", "arbitrary"), VMEM_LIMIT_BYTES),
        name="nsa_selwin_sample",
    )(page_table, qs, *([cache_t] * PAGES_PER_STEP), msk, newkv, win_t, newwr, bsel, bnew, bwin, expand, ocmp, gt)


def _fox_sample_kernel(pt_ref, qs_ref, *refs):
    P = PAGES_PER_STEP
    pages = refs[:P]
    lf_pages = refs[P:2 * P]
    (sl_ref, newkv_ref, lfnew_ref, hmask_ref, o_ref,
     qbd_ref, crel_ref, m_ref, l_ref, acc_ref, carry_ref) = refs[2 * P:]
    j = pl.program_id(1)
    R = SAMPLE_ROWS
    H = FOX_HEADS
    tk = P * PAGE

    @pl.when(j == 0)
    def _():
        carry_ref[...] = jnp.zeros_like(carry_ref)
        q = qs_ref[...].astype(F32)
        qbd_ref[...] = (jnp.concatenate([q] * H, axis=0) * hmask_ref[...]).astype(BF16)
        tri = jnp.where(lax.broadcasted_iota(jnp.int32, (R, R), 1) <= lax.broadcasted_iota(jnp.int32, (R, R), 0), 1.0, 0.0)
        crel = _dot_f32(tri, lfnew_ref[...])
        crel_ref[...] = jnp.concatenate([crel[:, h:h + 1] for h in range(H)], axis=0)
        m_ref[...] = jnp.full_like(m_ref, NEG)
        l_ref[...] = jnp.zeros_like(l_ref)
        acc_ref[...] = jnp.zeros_like(acc_ref)

    carry = carry_ref[:, 0:1]
    d_parts = [None] * P
    for r in reversed(range(P)):
        lf = lf_pages[r][...]
        d_parts[r] = _dot_f32(lf, sl_ref[...]) + carry
        carry = carry + jnp.sum(lf, axis=1, keepdims=True)
    carry_ref[...] = jnp.broadcast_to(carry, carry_ref.shape)
    dt = jnp.concatenate(d_parts, axis=1)

    kt = jnp.concatenate([p[0:FOX_WIDTH, :] for p in pages], axis=1).astype(BF16)
    vt = jnp.concatenate([p[FOX_WIDTH:, :] for p in pages], axis=1).astype(BF16)
    drows = jnp.concatenate([jnp.broadcast_to(dt[h:h + 1, :], (R, tk)) for h in range(H)], axis=0)
    s = jnp.dot(qbd_ref[...], kt, preferred_element_type=F32) + crel_ref[...] + drows
    _online_update(s, s > 2 * NEG, vt, m_ref, l_ref, acc_ref, v_t=True)

    @pl.when(j == pl.num_programs(1) - 1)
    def _():
        lf = lfnew_ref[...]
        iu = lax.broadcasted_iota(jnp.int32, (R, R), 0)
        ir = lax.broadcasted_iota(jnp.int32, (R, R), 1)
        a_le = jnp.where(ir <= iu, 1.0, 0.0)
        b_gt = jnp.where(iu > ir, 1.0, 0.0)
        dnew = jnp.concatenate([_dot_f32(a_le, lf[:, h:h + 1] * b_gt) for h in range(H)], axis=0)
        rq = lax.broadcasted_iota(jnp.int32, (H * R, 1), 0) & (R - 1)
        valid = lax.broadcasted_iota(jnp.int32, (H * R, R), 1) <= rq
        knew = newkv_ref[:, 0:FOX_WIDTH].astype(BF16)
        vnew = newkv_ref[:, FOX_WIDTH:].astype(BF16)
        _online_update(_dot_nt(qbd_ref[...], knew) + dnew, valid, vnew, m_ref, l_ref, acc_ref)
        o = (acc_ref[...] / jnp.maximum(l_ref[...], 1e-30)) * hmask_ref[...]
        out = o[0:R]
        for h in range(1, H):
            out = out + o[h * R:(h + 1) * R]
        o_ref[...] = out


def _fox_sample(qs, kv_t, logf_t, layer, page_table, newkv, lfnew):
    sb, n_pages = page_table.shape
    R = SAMPLE_ROWS
    n_steps = n_pages // PAGES_PER_STEP
    hmask = jnp.asarray((np.arange(FOX_HEADS * R)[:, None] // R == np.arange(FOX_WIDTH)[None, :] // HEAD_DIM)
                        .astype(np.float32))
    sl = jnp.asarray(np.tril(np.ones((PAGE, PAGE), np.float32), -1))
    seq = lambda a: pl.BlockSpec((None,) + a.shape[1:], lambda bi, j, pt: (bi,) + (0,) * (a.ndim - 1))
    full = lambda a: pl.BlockSpec(a.shape, lambda bi, j, pt: (0,) * a.ndim)
    return pl.pallas_call(
        _fox_sample_kernel,
        grid_spec=pltpu.PrefetchScalarGridSpec(
            num_scalar_prefetch=1, grid=(sb, n_steps),
            in_specs=[seq(qs)] + _page_specs(layer, 2 * FOX_WIDTH, 0, n_steps) + _page_specs(layer, FOX_HEADS, 0, n_steps)
                     + [full(sl), seq(newkv), seq(lfnew), full(hmask)],
            out_specs=pl.BlockSpec((None, R, FOX_WIDTH), lambda bi, j, pt: (bi, 0, 0)),
            scratch_shapes=[pltpu.VMEM((FOX_HEADS * R, FOX_WIDTH), BF16), pltpu.VMEM((FOX_HEADS * R, 1), F32),
                            pltpu.VMEM((FOX_HEADS * R, 1), F32), pltpu.VMEM((FOX_HEADS * R, 1), F32),
                            pltpu.VMEM((FOX_HEADS * R, FOX_WIDTH), F32), pltpu.VMEM((FOX_HEADS, LANES), F32)]),
        out_shape=jax.ShapeDtypeStruct((sb, R, FOX_WIDTH), F32),
        compiler_params=_cparams(("arbitrary", "arbitrary"), VMEM_LIMIT_BYTES),
        name="fox_sample",
    )(page_table, qs, *([kv_t] * PAGES_PER_STEP), *([logf_t] * PAGES_PER_STEP), sl, newkv, lfnew, hmask)


def _pad_cols(w, width):
    return jnp.pad(w, ((0, 0), (0, width - w.shape[1])))


def _prep_even_w(w):
    gq, gk, gv, glr, gog, nq, nkv, ng = _split(w, EVEN_SIZES)
    return jnp.concatenate([gq, gk, gv, gog, nq, nkv, _pad_cols(glr, LANES), _pad_cols(ng, LANES)], axis=1).astype(BF16)


def _prep_odd_w(w):
    fq, fk, fv, ff, cg = _split(w, ODD_SIZES)
    return jnp.concatenate([fq, fk, fv, cg, _pad_cols(ff, LANES)], axis=1).astype(BF16)


def _row(v):
    return v.reshape(1, -1).astype(F32)


def _tile_row(v, reps):
    return jnp.tile(v.astype(F32), reps).reshape(1, -1)


def kernel(x_prompt, x_sample, cache_nsa_kv, state_nsa_win, state_gla, cache_fox_kv, cache_fox_logf, state_conv, cache_mem_kv, page_table, mem_prompt, rel_bias, norm_mix, norm_xattn, norm_ffn, even_w_in, even_w_out, gla_w_gate, gla_b_gate, gla_out_norm, nsa_q_norm, nsa_k_norm, nsa_cmp_pe, nsa_cmp_w1, nsa_cmp_w2, odd_w_in, odd_w_out, fox_q_norm, fox_k_norm, fox_b_f, conv_w, conv_b, conv_ln_g, conv_ln_b, mem_norm, xa_wq, xa_wkv, xa_wo, xa_q_norm, xa_k_norm, ffn_w_in, ffn_w_out):
    B, T, _ = x_prompt.shape
    SB, SQ, _ = x_sample.shape
    depth = norm_mix.shape[0]
    past_len = page_table.shape[1] * cache_nsa_kv.shape[2]
    MP = B * T
    SR = SAMPLE_ROWS
    MS = SB * SR

    yp = x_prompt.reshape(MP, D_MODEL)
    ys = jnp.pad(x_sample, ((0, 0), (0, SR - SQ), (0, 0))).reshape(MS, D_MODEL)

    nsa_cache_t = _feature_major(cache_nsa_kv)
    fox_cache_t = _feature_major(cache_fox_kv)
    logf_t = _feature_major(cache_fox_logf)
    win_t = _feature_major(state_nsa_win)
    mem_cache_t = _feature_major(cache_mem_kv)

    nsa_kv_p, nsa_kv_s, win_p, win_s, gla_p, gla_s = [], [], [], [], [], []
    fox_kv_p, fox_kv_s, logf_p, logf_s, conv_p, conv_s, mem_kv_p = [], [], [], [], [], [], []

    for layer in range(depth):
        if layer % 2 == 0:
            e = layer // 2
            w_pad = _prep_even_w(even_w_in[e])
            wg_pad = jnp.pad(gla_w_gate[e], ((0, LANES - GLA_RANK), (0, 0))).astype(BF16)
            bg = _row(gla_b_gate[e])
            qn = _tile_row(nsa_q_norm[e], NSA_HEADS)
            kn1 = _tile_row(nsa_k_norm[e, 1], NSA_KV_HEADS)
            kn2 = _tile_row(nsa_k_norm[e, 2], NSA_KV_HEADS)
            gn = _row(gla_out_norm[e])
            w_out = even_w_out[e].astype(BF16)
            g_mix = _row(norm_mix[layer])
            pe = jnp.tile(nsa_cmp_pe[e].astype(F32), (1, 1, NSA_KV_HEADS))
            w1b = _compress_weights(nsa_cmp_w1[e])
            w2b = nsa_cmp_w2[e].astype(BF16)
            kn0 = _row(nsa_k_norm[e, 0])
            tbl = rel_bias.astype(F32).T
            (q, k, v, la, og, nqs, kvr, wr, gt, selkv, winkv, nqst, selvt, winvt, gtt) = _even_in(
                yp, g_mix, w_pad, wg_pad, bg, qn, kn1, kn2, 512)
            r3 = lambda a: a.reshape(B, T, a.shape[-1])
            s0t = jnp.zeros((B, GLA_HEADS, GLA_DV, GLA_DK), F32)
            o_gla, sfin_t = _gla(r3(q), r3(k), r3(v), r3(la), r3(og), gn, s0t, 256, B)
            kvr5 = kvr.reshape(B, T, 4, NSA_KV_HEADS, HEAD_DIM)
            wr5 = wr.reshape(B, T, 2, NSA_KV_HEADS, HEAD_DIM)
            ck, cv = _compress_prompt(r3(kvr), pe, w1b, w2b, kn0)
            ocmp_t, mskt = _cmp_topk_prompt(nqst, ck, cv, tbl, B, T, 256, SEL_TOPN)
            o_nsa = _selwin_prompt(nqst, selkv, selvt, winkv, winvt, mskt, tbl, ocmp_t, gtt, B, T, 256)
            yp = _out_proj(yp, o_gla.reshape(MP, GLA_WIDTH), o_nsa.reshape(MP, NSA_WIDTH),
                           w_out[:GLA_WIDTH], w_out[GLA_WIDTH:], 512)
            nsa_kv_p.append(kvr5)
            win_p.append(wr5[:, -min(NSA_WINDOW, T):])
            gla_p.append(jnp.swapaxes(sfin_t, -1, -2))
            (q, k, v, la, og, nqs, kvr, wr, gt) = _even_in(ys, g_mix, w_pad, wg_pad, bg, qn, kn1, kn2, MS)[:9]
            pad16 = lambda a: jnp.pad(a.reshape(SB, SR, a.shape[-1]), ((0, 0), (0, SUB - SR), (0, 0)))
            s0t = jnp.swapaxes(state_gla[e], -1, -2)
            o_gla, snew_t = _gla(pad16(q), pad16(k), pad16(v), pad16(la), pad16(og), gn, s0t, SUB, 4, n_valid=SQ)
            o_gla = o_gla[:, :SR]
            s3 = lambda a: a.reshape(SB, SR, a.shape[-1])
            kvr5 = kvr.reshape(SB, SR, 4, NSA_KV_HEADS, HEAD_DIM)[:, :SQ]
            wr5 = wr.reshape(SB, SR, 2, NSA_KV_HEADS, HEAD_DIM)[:, :SQ]
            ck, cv = _compress_sample(nsa_cache_t, e, page_table, pe, w1b, w2b, kn0)
            ocmp, msk = _cmp_topk_sample(s3(nqs), ck, cv, tbl, SR, past_len, SEL_TOPN - 1, past_len // SEL_BLOCK)
            o_nsa = _selwin_sample(s3(nqs), nsa_cache_t, e, page_table, msk, s3(kvr), win_t, s3(wr), tbl, ocmp, s3(gt))
            ys = _out_proj(ys, o_gla.reshape(MS, GLA_WIDTH), o_nsa.reshape(MS, NSA_WIDTH),
                           w_out[:GLA_WIDTH], w_out[GLA_WIDTH:], MS)
            nsa_kv_s.append(kvr5)
            win_s.append(jnp.concatenate([state_nsa_win[e][:, SQ:], wr5], axis=1))
            gla_s.append(jnp.swapaxes(snew_t, -1, -2))
        else:
            j = layer // 2
            w_pad = _prep_odd_w(odd_w_in[j])
            qn = _tile_row(fox_q_norm[j], FOX_HEADS)
            kn = _tile_row(fox_k_norm[j], FOX_HEADS)
            bf_pad = jnp.pad(fox_b_f[j].astype(F32), (0, LANES - FOX_HEADS)).reshape(1, LANES)
            w_out = odd_w_out[j].astype(BF16)
            g_mix = _row(norm_mix[layer])
            cw = conv_w[j].astype(F32)
            cb, cg_, cbeta = _row(conv_b[j]), _row(conv_ln_g[j]), _row(conv_ln_b[j])
            qs, kv, kb, lf, c, u, qst, vbt = _odd_in(yp.reshape(B, T, D_MODEL), g_mix, w_pad, qn, kn, bf_pad, 256)
            c8 = c[:, :, :FOX_HEADS]
            o_fox = _fox_prompt(qst, kb, vbt, c8, jnp.swapaxes(c8, 1, 2), 512, 1024)
            o_conv, cst = _conv(u, jnp.zeros((B, CONV_WIDTH - 1, CONV_CH), F32), cw, cb, cg_, cbeta, 512, 512)
            yp = _out_proj(yp, o_fox.reshape(MP, FOX_WIDTH), o_conv.reshape(MP, CONV_CH),
                           w_out[:FOX_WIDTH], w_out[FOX_WIDTH:], 512)
            fox_kv_p.append(kv.reshape(B, T, 2, FOX_HEADS, HEAD_DIM))
            logf_p.append(lf[:, :, :FOX_HEADS])
            conv_p.append(cst)
            qs, kv, kb, lf, c, u = _odd_in(ys.reshape(1, MS, D_MODEL), g_mix, w_pad, qn, kn, bf_pad, MS)[:6]
            s3 = lambda a: a.reshape(SB, SR, a.shape[-1])
            new_kv = kv.reshape(SB, SR, 2, FOX_HEADS, HEAD_DIM)[:, :SQ]
            lf_new = lf.reshape(SB, SR, LANES)[:, :SQ, :FOX_HEADS]
            o_fox = _fox_sample(s3(qs), fox_cache_t, logf_t, j, page_table, s3(kv), s3(lf))
            o_conv, cst = _conv(u.reshape(SB, SR, CONV_CH), state_conv[j], cw, cb, cg_, cbeta, SR, SQ)
            ys = _out_proj(ys, o_fox.reshape(MS, FOX_WIDTH), o_conv.reshape(MS, CONV_CH),
                           w_out[:FOX_WIDTH], w_out[FOX_WIDTH:], MS)
            fox_kv_s.append(new_kv)
            logf_s.append(lf_new)
            conv_s.append(cst)
        g_xa = _row(norm_xattn[layer])
        wq = xa_wq[layer].astype(BF16)
        wo = xa_wo[layer].astype(BF16)
        xqn = _tile_row(xa_q_norm[layer], XA_HEADS)
        mkv_t = _mem_kv(mem_prompt, _row(mem_norm[layer]), xa_wkv[layer].astype(BF16), _tile_row(xa_k_norm[layer], XA_HEADS))
        mem_kv_p.append(jnp.transpose(mkv_t.reshape(B, 2, XA_HEADS, HEAD_DIM, MEM_LEN), (0, 4, 1, 2, 3)))
        yp = _xattn(yp.reshape(B, T, D_MODEL), mkv_t[None], 0, g_xa, wq, wo, xqn, 512).reshape(MP, D_MODEL)
        ys = _xattn(ys.reshape(SB, SR, D_MODEL), mem_cache_t, layer, g_xa, wq, wo, xqn, SR).reshape(MS, D_MODEL)
        g_ffn = _row(norm_ffn[layer])
        w_in = ffn_w_in[layer].astype(BF16)
        w_o = ffn_w_out[layer].astype(BF16)
        yp = _ffn(yp, g_ffn, w_in, w_o, 512)
        ys = _ffn(ys, g_ffn, w_in, w_o, MS)

    yp = yp.reshape(B, T, D_MODEL)
    ys = ys.reshape(SB, SR, D_MODEL)[:, :SQ]
    return (yp, ys,
            jnp.stack(nsa_kv_p), jnp.stack(nsa_kv_s), jnp.stack(win_p), jnp.stack(win_s),
            jnp.stack(gla_p), jnp.stack(gla_s), jnp.stack(fox_kv_p), jnp.stack(fox_kv_s),
            jnp.stack(logf_p), jnp.stack(logf_s), jnp.stack(conv_p), jnp.stack(conv_s),
            jnp.stack(mem_kv_p))
```

```python
import functools
import math

import jax
import jax.numpy as jnp
import numpy as np
from jax import lax
from jax.experimental import pallas as pl
from jax.experimental.pallas import tpu as pltpu

F32 = jnp.float32
BF16 = jnp.bfloat16

D_MODEL = 1024
HEAD_DIM = 64
GLA_WIDTH = 512
GLA_HEADS = 4
GLA_DV = 128
GLA_DK = 64
GLA_RANK = 16
GLA_TAU = 16.0
NSA_WIDTH = 512
NSA_HEADS = 8
NSA_KV_HEADS = 2
NSA_GROUP = 4
CMP_STRIDE = 16
CMP_BLOCK = 32
CMP_HIDDEN = 256
SEL_BLOCK = 64
SEL_RATIO = 4
SEL_TOPN = 16
NSA_WINDOW = 512
FOX_WIDTH = 512
FOX_HEADS = 8
CONV_CH = 512
CONV_WIDTH = 31
MEM_LEN = 256
XA_HEADS = 4
XA_WIDTH = 256
FFN_HIDDEN = 2816
NUM_BUCKETS = 32
MAX_DISTANCE = 128
EPS = 1e-6
SCALE = HEAD_DIM ** -0.5
NEG = -1e30

EVEN_SIZES = (256, 256, 512, 16, 512, 512, 768, 24)
ODD_SIZES = (512, 512, 512, 8, 1024)

LANES = 128
VMEM_LIMIT_BYTES = 56 * 1024 * 1024
SAMPLE_ROWS = 8
SUB = 16
FAR_GROUP = 4
CMP_CHUNK = 128
N_SELBLK = 128


def _cparams(sem, vmem=None):
    return pltpu.CompilerParams(dimension_semantics=sem, vmem_limit_bytes=vmem)


def _split(h, sizes):
    return jnp.split(h, np.cumsum(sizes)[:-1].tolist(), axis=-1)


def _rms_rows(x, g):
    return x * lax.rsqrt(jnp.mean(x * x, axis=-1, keepdims=True) + EPS) * g


def _group_rms(x, gmat, gs):
    x2 = x * x
    hi = x2.astype(BF16)
    lo = (x2 - hi.astype(F32)).astype(BF16)
    ms = (jnp.dot(hi, gmat, preferred_element_type=F32) + jnp.dot(lo, gmat, preferred_element_type=F32)) * (1.0 / gs)
    return x * lax.rsqrt(ms + EPS)


def _log_sigmoid(z):
    return -(jnp.maximum(-z, 0.0) + jnp.log1p(jnp.exp(-jnp.abs(z))))


def _sigmoid(z):
    return 1.0 / (1.0 + jnp.exp(-z))


def _dot_nt(a, b):
    return lax.dot_general(a, b, (((1,), (1,)), ((), ())), preferred_element_type=F32)


def _dot_tn(a, b):
    return lax.dot_general(a, b, (((0,), (0,)), ((), ())), preferred_element_type=F32)


def _dot_f32(a, b):
    return jnp.dot(a, b, preferred_element_type=F32, precision=lax.Precision.HIGHEST)


def _block_ones(width, gs):
    r = np.arange(width) // gs
    return jnp.asarray((r[:, None] == r[None, :]).astype(np.float32), dtype=BF16)


def _even_in_kernel(x_ref, g_ref, w_ref, wg_ref, bg_ref, qn_ref, kn1_ref, kn2_ref, gm512_ref, gm128_ref,
                    q_ref, k_ref, v_ref, la_ref, og_ref, nqs_ref, kvr_ref, wr_ref, gt_ref, selkv_ref, winkv_ref,
                    nqst_ref, selvt_ref, winvt_ref, gtt_ref):
    xb = _rms_rows(x_ref[...], g_ref[...]).astype(BF16)

    def proj(lo, hi):
        return jnp.dot(xb, w_ref[:, lo:hi], preferred_element_type=F32)

    q_ref[...] = proj(0, 256) * (GLA_DK ** -0.5)
    k_ref[...] = proj(256, 512)
    v_ref[...] = proj(512, 1024)
    og = proj(1024, 1536)
    og_ref[...] = og * _sigmoid(og)
    nq = _group_rms(proj(1536, 2048), gm512_ref[...], HEAD_DIM) * qn_ref[...] * SCALE
    nqs_ref[...] = nq.astype(BF16)
    nqst_ref[...] = nq.T.astype(BF16)
    kvr_ref[:, 0:256] = proj(2048, 2304)
    selk = _group_rms(proj(2304, 2432), gm128_ref[...], HEAD_DIM) * kn1_ref[...]
    selv = proj(2432, 2560)
    kvr_ref[:, 256:384] = selk
    kvr_ref[:, 384:512] = selv
    selkv_ref[:, 0:128] = selk.astype(BF16)
    selkv_ref[:, 128:256] = selv.astype(BF16)
    selvt_ref[...] = selv.T.astype(BF16)
    wink = _group_rms(proj(2560, 2688), gm128_ref[...], HEAD_DIM) * kn2_ref[...]
    winv = proj(2688, 2816)
    wr_ref[:, 0:128] = wink
    wr_ref[:, 128:256] = winv
    winkv_ref[:, 0:128] = wink.astype(BF16)
    winkv_ref[:, 128:256] = winv.astype(BF16)
    winvt_ref[...] = winv.T.astype(BF16)
    glr = proj(2816, 2944).astype(BF16)
    z = jnp.dot(glr, wg_ref[...], preferred_element_type=F32) + bg_ref[...]
    la_ref[...] = _log_sigmoid(z) * (1.0 / GLA_TAU)
    gates = _sigmoid(proj(2944, 3072))
    gt_ref[...] = gates
    gtt_ref[...] = gates.T[0:32, :]


def _even_in(x2d, g, w_pad, wg_pad, bg, qn, kn1, kn2, tm):
    m = x2d.shape[0]
    widths = (256, 256, 512, 256, 512, 512, 512, 256, 128, 256, 256)
    dtypes = (F32, F32, F32, F32, F32, BF16, F32, F32, F32, BF16, BF16)
    t_heights = (512, 128, 128, 32)
    t_dtypes = (BF16, BF16, BF16, F32)
    full = lambda a: pl.BlockSpec(a.shape, lambda i: (0,) * a.ndim)
    gm512 = _block_ones(512, HEAD_DIM)
    gm128 = _block_ones(128, HEAD_DIM)
    ins = (x2d, g, w_pad, wg_pad, bg, qn, kn1, kn2, gm512, gm128)
    return pl.pallas_call(
        _even_in_kernel,
        grid=(m // tm,),
        in_specs=[pl.BlockSpec((tm, D_MODEL), lambda i: (i, 0))] + [full(a) for a in ins[1:]],
        out_specs=[pl.BlockSpec((tm, w), lambda i: (i, 0)) for w in widths]
                  + [pl.BlockSpec((h, tm), lambda i: (0, i)) for h in t_heights],
        out_shape=[jax.ShapeDtypeStruct((m, w), d) for w, d in zip(widths, dtypes)]
                  + [jax.ShapeDtypeStruct((h, m), d) for h, d in zip(t_heights, t_dtypes)],
        compiler_params=_cparams(("arbitrary",), VMEM_LIMIT_BYTES),
        name="even_in",
    )(*ins)


def _odd_in_kernel(x_ref, g_ref, w_ref, qn_ref, kn_ref, bf_ref, gm512_ref, tri_ref,
                   qs_ref, kv_ref, kb_ref, lf_ref, c_ref, u_ref, qst_ref, vbt_ref, carry_ref):
    @pl.when(pl.program_id(1) == 0)
    def _():
        carry_ref[...] = jnp.zeros_like(carry_ref)

    xb = _rms_rows(x_ref[...], g_ref[...]).astype(BF16)

    def proj(lo, hi):
        return jnp.dot(xb, w_ref[:, lo:hi], preferred_element_type=F32)

    q = _group_rms(proj(0, 512), gm512_ref[...], HEAD_DIM) * qn_ref[...] * SCALE
    qs_ref[...] = q.astype(BF16)
    qst_ref[...] = q.T.astype(BF16)
    k = _group_rms(proj(512, 1024), gm512_ref[...], HEAD_DIM) * kn_ref[...]
    v = proj(1024, 1536)
    kv_ref[:, 0:512] = k
    kv_ref[:, 512:1024] = v
    kb_ref[...] = k.astype(BF16)
    vbt_ref[...] = v.T.astype(BF16)
    u_ref[...] = proj(1536, 2048) * _sigmoid(proj(2048, 2560))
    lf = _log_sigmoid(proj(2560, 2688) + bf_ref[...])
    lf_ref[...] = lf
    c = _dot_f32(tri_ref[...], lf) + carry_ref[0:1, :]
    c_ref[...] = c
    carry_ref[0:1, :] = c[-1:, :]


def _odd_in(x3d, g, w_pad, qn, kn, bf_pad, tm):
    b, t, _ = x3d.shape
    widths = (512, 1024, 512, 128, 128, 512)
    dtypes = (BF16, F32, BF16, F32, F32, F32)
    gm512 = _block_ones(512, HEAD_DIM)
    tri = jnp.asarray(np.tril(np.ones((tm, tm), np.float32)))
    ins = (x3d, g, w_pad, qn, kn, bf_pad, gm512, tri)
    full = lambda a: pl.BlockSpec(a.shape, lambda bi, i: (0,) * a.ndim)
    nt = t // tm
    t_spec = pl.BlockSpec((FOX_WIDTH, tm), lambda bi, i: (0, bi * nt + i))
    t_shape = jax.ShapeDtypeStruct((FOX_WIDTH, b * t), BF16)
    return pl.pallas_call(
        _odd_in_kernel,
        grid=(b, nt),
        in_specs=[pl.BlockSpec((None, tm, D_MODEL), lambda bi, i: (bi, i, 0))] + [full(a) for a in ins[1:]],
        out_specs=[pl.BlockSpec((None, tm, w), lambda bi, i: (bi, i, 0)) for w in widths] + [t_spec, t_spec],
        out_shape=[jax.ShapeDtypeStruct((b, t, w), d) for w, d in zip(widths, dtypes)] + [t_shape, t_shape],
        scratch_shapes=[pltpu.VMEM((8, 128), F32)],
        compiler_params=_cparams(("arbitrary", "arbitrary"), VMEM_LIMIT_BYTES),
        name="odd_in",
    )(*ins)


def _out_proj_kernel(res_ref, a1_ref, a2_ref, w1_ref, w2_ref, o_ref):
    acc = jnp.dot(a1_ref[...].astype(BF16), w1_ref[...], preferred_element_type=F32)
    acc = acc + jnp.dot(a2_ref[...].astype(BF16), w2_ref[...], preferred_element_type=F32)
    o_ref[...] = res_ref[...] + acc


def _out_proj(res, a1, a2, w1, w2, tm):
    m = res.shape[0]
    row = lambda a: pl.BlockSpec((tm, a.shape[1]), lambda i: (i, 0))
    full = lambda a: pl.BlockSpec(a.shape, lambda i: (0, 0))
    return pl.pallas_call(
        _out_proj_kernel,
        grid=(m // tm,),
        in_specs=[row(res), row(a1), row(a2), full(w1), full(w2)],
        out_specs=row(res),
        out_shape=jax.ShapeDtypeStruct(res.shape, F32),
        compiler_params=_cparams(("arbitrary",), VMEM_LIMIT_BYTES),
        name="out_proj",
    )(res, a1, a2, w1, w2)


def _ffn_kernel(x_ref, g_ref, wg_ref, wu_ref, wo_ref, o_ref, xn_ref, acc_ref):
    j = pl.program_id(1)

    @pl.when(j == 0)
    def _():
        xn_ref[...] = _rms_rows(x_ref[...], g_ref[...]).astype(BF16)
        acc_ref[...] = jnp.zeros_like(acc_ref)

    xb = xn_ref[...]
    gate = jnp.dot(xb, wg_ref[...], preferred_element_type=F32)
    up = jnp.dot(xb, wu_ref[...], preferred_element_type=F32)
    h = (gate * _sigmoid(gate) * up).astype(BF16)
    acc_ref[...] += jnp.dot(h, wo_ref[...], preferred_element_type=F32)

    @pl.when(j == pl.num_programs(1) - 1)
    def _():
        o_ref[...] = x_ref[...] + acc_ref[...]


def _ffn(x2d, g, w_in, w_out, tm, n_chunks=2):
    m = x2d.shape[0]
    th = FFN_HIDDEN // n_chunks
    return pl.pallas_call(
        _ffn_kernel,
        grid=(m // tm, n_chunks),
        in_specs=[pl.BlockSpec((tm, D_MODEL), lambda i, j: (i, 0)),
                  pl.BlockSpec((1, D_MODEL), lambda i, j: (0, 0)),
                  pl.BlockSpec((D_MODEL, th), lambda i, j: (0, j)),
                  pl.BlockSpec((D_MODEL, th), lambda i, j: (0, n_chunks + j)),
                  pl.BlockSpec((th, D_MODEL), lambda i, j: (j, 0))],
        out_specs=pl.BlockSpec((tm, D_MODEL), lambda i, j: (i, 0)),
        out_shape=jax.ShapeDtypeStruct(x2d.shape, F32),
        scratch_shapes=[pltpu.VMEM((tm, D_MODEL), BF16), pltpu.VMEM((tm, D_MODEL), F32)],
        compiler_params=_cparams(("arbitrary", "arbitrary"), VMEM_LIMIT_BYTES),
        name="ffn",
    )(x2d, g, w_in, w_in, w_out)


def _mem_kv_kernel(m_ref, g_ref, w_ref, kn_ref, gm_ref, o_ref):
    xb = _rms_rows(m_ref[...], g_ref[...]).astype(BF16)
    kv = jnp.dot(xb, w_ref[...], preferred_element_type=F32)
    o_ref[0:XA_WIDTH, :] = (_group_rms(kv[:, 0:XA_WIDTH], gm_ref[...], HEAD_DIM) * kn_ref[...]).T
    o_ref[XA_WIDTH:, :] = kv[:, XA_WIDTH:].T


def _mem_kv(mem, g, wkv, kn):
    b = mem.shape[0]
    gm = _block_ones(XA_WIDTH, HEAD_DIM)
    full = lambda a: pl.BlockSpec(a.shape, lambda i: (0,) * a.ndim)
    return pl.pallas_call(
        _mem_kv_kernel,
        grid=(b,),
        in_specs=[pl.BlockSpec((None, MEM_LEN, D_MODEL), lambda i: (i, 0, 0)), full(g), full(wkv), full(kn), full(gm)],
        out_specs=pl.BlockSpec((None, 2 * XA_WIDTH, MEM_LEN), lambda i: (i, 0, 0)),
        out_shape=jax.ShapeDtypeStruct((b, 2 * XA_WIDTH, MEM_LEN), F32),
        compiler_params=_cparams(("arbitrary",)),
        name="mem_kv",
    )(mem, g, wkv, kn, gm)


def _xattn_kernel(x_ref, mkv_ref, g_ref, wq_ref, wo_ref, qn_ref, gm_ref, o_ref):
    x = x_ref[...]
    xb = _rms_rows(x, g_ref[...]).astype(BF16)
    q = jnp.dot(xb, wq_ref[...], preferred_element_type=F32)
    q = _group_rms(q, gm_ref[...], HEAD_DIM) * qn_ref[...]
    qb = (q * SCALE).astype(BF16)
    outs = []
    for h in range(XA_HEADS):
        kt = mkv_ref[h * HEAD_DIM:(h + 1) * HEAD_DIM, :].astype(BF16)
        vt = mkv_ref[XA_WIDTH + h * HEAD_DIM:XA_WIDTH + (h + 1) * HEAD_DIM, :].astype(BF16)
        s = jnp.dot(qb[:, h * HEAD_DIM:(h + 1) * HEAD_DIM], kt, preferred_element_type=F32)
        e = jnp.exp(s - jnp.max(s, axis=-1, keepdims=True))
        p = (e / jnp.sum(e, axis=-1, keepdims=True)).astype(BF16)
        outs.append(_dot_nt(p, vt))
    o = jnp.concatenate(outs, axis=-1).astype(BF16)
    o_ref[...] = x + jnp.dot(o, wo_ref[...], preferred_element_type=F32)


def _xattn(x3d, mkv_t, layer, g, wq, wo, qn, tm):
    b, t, _ = x3d.shape
    gm = _block_ones(XA_WIDTH, HEAD_DIM)
    full = lambda a: pl.BlockSpec(a.shape, lambda bi, i: (0,) * a.ndim)
    return pl.pallas_call(
        _xattn_kernel,
        grid=(b, t // tm),
        in_specs=[pl.BlockSpec((None, tm, D_MODEL), lambda bi, i: (bi, i, 0)),
                  pl.BlockSpec((None, None, 2 * XA_WIDTH, MEM_LEN), lambda bi, i: (layer, bi, 0, 0)),
                  full(g), full(wq), full(wo), full(qn), full(gm)],
        out_specs=pl.BlockSpec((None, tm, D_MODEL), lambda bi, i: (bi, i, 0)),
        out_shape=jax.ShapeDtypeStruct(x3d.shape, F32),
        compiler_params=_cparams(("arbitrary", "arbitrary"), VMEM_LIMIT_BYTES),
        name="xattn",
    )(x3d, mkv_t, g, wq, wo, qn, gm)


def _gla_kernel(q_ref, k_ref, v_ref, la_ref, og_ref, gn_ref, s0_ref, tri_ref, hsel_ref,
                o_ref, sfin_ref, st_ref, *, n_sub, n_valid, bb):
    ti = pl.program_id(1)

    @pl.when(ti == 0)
    def _():
        st_ref[...] = s0_ref[...]

    tri = tri_ref[...]
    hsel = hsel_ref[...]
    gn = gn_ref[...]
    row = lax.broadcasted_iota(jnp.int32, (SUB, 1), 0)

    def sub_block(i, carry):
        for bi in range(bb):
            one_sequence(i, bi)
        return carry

    def one_sequence(i, bi):
        r0 = pl.multiple_of(i * SUB, SUB)
        rows = pl.ds(r0, SUB)
        q = q_ref[bi, rows, :]
        k = k_ref[bi, rows, :]
        v = v_ref[bi, rows, :]
        la = la_ref[bi, rows, :]
        if n_valid is not None:
            live = (row + r0) < n_valid
            la = jnp.where(live, la, 0.0)
            k = jnp.where(live, k, 0.0)
        b = _dot_f32(tri, la)
        b_end = b[SUB - 1:SUB, :]
        qd = (q * jnp.exp(b)).astype(BF16)
        kd = (k * jnp.exp(b_end - b)).astype(BF16)
        vb = v.astype(BF16)
        tiles = []
        for s in range(SUB):
            e = jnp.exp(jnp.minimum(b - b[s:s + 1, :], 0.0))
            z = (q * k[s:s + 1, :]) * e
            tiles.append(jnp.where(row >= s, z, 0.0))
        att = jnp.dot(jnp.concatenate(tiles, axis=0).astype(BF16), hsel, preferred_element_type=F32)
        dec = jnp.exp(b_end)
        outs = []
        for h in range(GLA_HEADS):
            dk = slice(h * GLA_DK, (h + 1) * GLA_DK)
            dv = slice(h * GLA_DV, (h + 1) * GLA_DV)
            st = st_ref[bi, h]
            o = _dot_nt(qd[:, dk], st.astype(BF16))
            for s in range(SUB):
                o = o + att[s * SUB:(s + 1) * SUB, h:h + 1] * v[s:s + 1, dv]
            st_ref[bi, h] = st * dec[:, dk] + _dot_tn(vb[:, dv], kd[:, dk])
            outs.append(_rms_rows(o, gn))
        o_ref[bi, rows, :] = jnp.concatenate(outs, axis=-1) * og_ref[bi, rows, :]

    lax.fori_loop(0, n_sub, sub_block, 0)

    @pl.when(ti == pl.num_programs(1) - 1)
    def _():
        sfin_ref[...] = st_ref[...]


def _gla(q, k, v, la, og, gn, s0t, tt, bb, n_valid=None):
    b, t, _ = q.shape
    tri = jnp.asarray(np.tril(np.ones((SUB, SUB), np.float32)))
    hsel = jnp.asarray((np.arange(256)[:, None] // GLA_DK == np.arange(128)[None, :]).astype(np.float32), dtype=BF16)
    seq = lambda w: pl.BlockSpec((bb, tt, w), lambda bi, i: (bi, i, 0))
    full = lambda a: pl.BlockSpec(a.shape, lambda bi, i: (0,) * a.ndim)
    st_spec = pl.BlockSpec((bb, GLA_HEADS, GLA_DV, GLA_DK), lambda bi, i: (bi, 0, 0, 0))
    return pl.pallas_call(
        functools.partial(_gla_kernel, n_sub=tt // SUB, n_valid=n_valid, bb=bb),
        grid=(b // bb, t // tt),
        in_specs=[seq(256), seq(256), seq(512), seq(256), seq(512), full(gn), st_spec, full(tri), full(hsel)],
        out_specs=[seq(512), st_spec],
        out_shape=[jax.ShapeDtypeStruct((b, t, GLA_WIDTH), F32),
                   jax.ShapeDtypeStruct((b, GLA_HEADS, GLA_DV, GLA_DK), F32)],
        scratch_shapes=[pltpu.VMEM((bb, GLA_HEADS, GLA_DV, GLA_DK), F32)],
        compiler_params=_cparams(("arbitrary", "arbitrary")),
        name="gla",
    )(q, k, v, la, og, gn, s0t, tri, hsel)


def _conv_kernel(u_ref, st0_ref, w_ref, b_ref, g_ref, beta_ref, o_ref, st_ref, ext_ref, *, tt, n_valid):
    ti = pl.program_id(1)
    ctx = CONV_WIDTH - 1

    @pl.when(ti == 0)
    def _():
        ext_ref[0:8, :] = jnp.zeros((8, CONV_CH), F32)
        ext_ref[pl.ds(2, ctx), :] = st0_ref[...]

    ext_ref[pl.ds(32, tt), :] = u_ref[...]
    acc = jnp.zeros((tt, CONV_CH), F32)
    for w in range(CONV_WIDTH):
        acc = acc + ext_ref[pl.ds(2 + w, tt), :] * w_ref[w:w + 1, :]
    y = acc + b_ref[...]
    mu = jnp.mean(y, axis=-1, keepdims=True)
    var = jnp.mean(jnp.square(y - mu), axis=-1, keepdims=True)
    ln = (y - mu) * lax.rsqrt(var + EPS) * g_ref[...] + beta_ref[...]
    o_ref[...] = ln * _sigmoid(ln)

    @pl.when(ti == pl.num_programs(1) - 1)
    def _():
        st_ref[...] = ext_ref[pl.ds(32 + n_valid - ctx, ctx), :]

    ext_ref[0:32, :] = ext_ref[pl.ds(tt, 32), :]


def _conv(u, st0, w, b, g, beta, tt, n_valid):
    bsz, t, _ = u.shape
    ctx = CONV_WIDTH - 1
    full = lambda a: pl.BlockSpec(a.shape, lambda bi, i: (0,) * a.ndim)
    st_spec = pl.BlockSpec((None, ctx, CONV_CH), lambda bi, i: (bi, 0, 0))
    return pl.pallas_call(
        functools.partial(_conv_kernel, tt=tt, n_valid=n_valid),
        grid=(bsz, t // tt),
        in_specs=[pl.BlockSpec((None, tt, CONV_CH), lambda bi, i: (bi, i, 0)), st_spec,
                  full(w), full(b), full(g), full(beta)],
        out_specs=[pl.BlockSpec((None, tt, CONV_CH), lambda bi, i: (bi, i, 0)), st_spec],
        out_shape=[jax.ShapeDtypeStruct(u.shape, F32), jax.ShapeDtypeStruct((bsz, ctx, CONV_CH), F32)],
        scratch_shapes=[pltpu.VMEM((32 + max(tt, 32), CONV_CH), F32)],
        compiler_params=_cparams(("arbitrary", "arbitrary")),
        name="conv",
    )(u, st0, w, b, g, beta)


def _fox_kernel(qt_ref, k_ref, vt_ref, ck_ref, cqt_ref, o_ref, m_ref, l_ref, acc_ref, *, tq, tk):
    qi = pl.program_id(1)
    ki = pl.program_id(2)
    last = (qi * tq + tq - 1) // tk

    @pl.when(ki == 0)
    def _():
        m_ref[...] = jnp.full_like(m_ref, NEG)
        l_ref[...] = jnp.zeros_like(l_ref)
        acc_ref[...] = jnp.zeros_like(acc_ref)

    def tile(neg):
        for h in range(FOX_HEADS):
            cols = slice(h * HEAD_DIM, (h + 1) * HEAD_DIM)
            s = jnp.dot(k_ref[:, cols], qt_ref[cols, :], preferred_element_type=F32)
            s = s + cqt_ref[h:h + 1, :] - ck_ref[:, h:h + 1]
            if neg is not None:
                s = s + neg
            m_old = m_ref[h:h + 1, :]
            m_new = jnp.maximum(m_old, jnp.max(s, axis=0, keepdims=True))
            p = jnp.exp(s - m_new)
            alpha = jnp.exp(m_old - m_new)
            l_ref[h:h + 1, :] = alpha * l_ref[h:h + 1, :] + jnp.sum(p, axis=0, keepdims=True)
            acc_ref[cols, :] = alpha * acc_ref[cols, :] + jnp.dot(vt_ref[cols, :], p.astype(BF16),
                                                                  preferred_element_type=F32)
            m_ref[h:h + 1, :] = m_new

    @pl.when(ki < last)
    def _():
        tile(None)

    @pl.when(ki == last)
    def _():
        s_pos = ki * tk + lax.broadcasted_iota(jnp.int32, (tk, tq), 0)
        t_pos = qi * tq + lax.broadcasted_iota(jnp.int32, (tk, tq), 1)
        tile(jnp.where(s_pos <= t_pos, 0.0, NEG))

    @pl.when(ki == pl.num_programs(2) - 1)
    def _():
        for h in range(FOX_HEADS):
            cols = slice(h * HEAD_DIM, (h + 1) * HEAD_DIM)
            acc_ref[cols, :] = acc_ref[cols, :] / jnp.maximum(l_ref[h:h + 1, :], 1e-30)
        o_ref[...] = acc_ref[...].T


def _fox_prompt(qst, kb, vbt, c8, ct, tq, tk):
    b, t, _ = kb.shape
    nq, nk = t // tq, t // tk
    kmin = lambda qi, ki: jnp.minimum(ki, (qi * tq + tq - 1) // tk)
    return pl.pallas_call(
        functools.partial(_fox_kernel, tq=tq, tk=tk),
        grid=(b, nq, nk),
        in_specs=[pl.BlockSpec((FOX_WIDTH, tq), lambda bi, qi, ki: (0, bi * nq + qi)),
                  pl.BlockSpec((None, tk, FOX_WIDTH), lambda bi, qi, ki: (bi, kmin(qi, ki), 0)),
                  pl.BlockSpec((FOX_WIDTH, tk), lambda bi, qi, ki: (0, bi * nk + kmin(qi, ki))),
                  pl.BlockSpec((None, tk, FOX_HEADS), lambda bi, qi, ki: (bi, kmin(qi, ki), 0)),
                  pl.BlockSpec((None, FOX_HEADS, tq), lambda bi, qi, ki: (bi, 0, qi))],
        out_specs=pl.BlockSpec((None, tq, FOX_WIDTH), lambda bi, qi, ki: (bi, qi, 0)),
        out_shape=jax.ShapeDtypeStruct((b, t, FOX_WIDTH), F32),
        scratch_shapes=[pltpu.VMEM((FOX_HEADS, tq), F32), pltpu.VMEM((FOX_HEADS, tq), F32),
                        pltpu.VMEM((FOX_WIDTH, tq), F32)],
        compiler_params=_cparams(("arbitrary", "arbitrary", "arbitrary"), VMEM_LIMIT_BYTES),
        name="fox_prompt",
    )(qst, kb, vbt, c8, ct)


def _compress_weights(w1):
    w = w1.reshape(2, 2, 8, 2, HEAD_DIM, CMP_HIDDEN)
    z = jnp.zeros_like(w)
    g0 = jnp.concatenate([w, z], axis=-1)
    g1 = jnp.concatenate([z, w], axis=-1)
    wbd = jnp.stack([g0, g1], axis=4)
    return wbd.reshape(2, 2, 8, 4 * HEAD_DIM, 2 * CMP_HIDDEN).astype(BF16)


def _compress_compute(src_refs, pe_ref, w1_ref, w2_ref, kn_ref, ck_ref, cv_ref, sh_ref, nseg):
    sh_ref[pl.ds(nseg, 8), :] = jnp.zeros((8, CMP_HIDDEN), F32)
    for j, src_ref in enumerate(src_refs):
        a = jnp.zeros((nseg, 2 * CMP_HIDDEN), F32)
        bm = jnp.zeros((nseg, 2 * CMP_HIDDEN), F32)
        for q in range(CMP_STRIDE // 2):
            p0, p1 = 2 * q, 2 * q + 1
            x0 = src_ref[pl.ds(p0, nseg, stride=CMP_STRIDE), :]
            x1 = src_ref[pl.ds(p1, nseg, stride=CMP_STRIDE), :]
            xa = jnp.concatenate([x0 + pe_ref[j, p0:p0 + 1, :], x1 + pe_ref[j, p1:p1 + 1, :]], axis=1)
            a = a + jnp.dot(xa.astype(BF16), w1_ref[j, 0, q], preferred_element_type=F32)
            p0, p1 = p0 + CMP_STRIDE, p1 + CMP_STRIDE
            xb = jnp.concatenate([x0 + pe_ref[j, p0:p0 + 1, :], x1 + pe_ref[j, p1:p1 + 1, :]], axis=1)
            bm = bm + jnp.dot(xb.astype(BF16), w1_ref[j, 1, q], preferred_element_type=F32)
        for g in range(NSA_KV_HEADS):
            sh_ref[pl.ds(0, nseg), :] = bm[:, g * CMP_HIDDEN:(g + 1) * CMP_HIDDEN]
            x = a[:, g * CMP_HIDDEN:(g + 1) * CMP_HIDDEN] + sh_ref[pl.ds(1, nseg), :]
            hid = x * (0.5 * (1.0 + jnp.tanh(math.sqrt(2.0 / math.pi) * (x + 0.044715 * (x * x * x)))))
            ckv = jnp.dot(hid.astype(BF16), w2_ref[j], preferred_element_type=F32)
            if j == 0:
                ck_ref[g] = _rms_rows(ckv, kn_ref[...]).astype(BF16)
            else:
                cv_ref[g] = ckv.astype(BF16)


def _compress_prompt_kernel(xk_ref, xv_ref, pe_ref, w1_ref, w2_ref, kn_ref, ck_ref, cv_ref, sh_ref, *, nseg):
    _compress_compute((xk_ref, xv_ref), pe_ref, w1_ref, w2_ref, kn_ref, ck_ref, cv_ref, sh_ref, nseg)


def _compress_prompt(kvr, pe, w1, w2, kn):
    b, t, _ = kvr.shape
    nseg = t // CMP_STRIDE
    full = lambda a: pl.BlockSpec(a.shape, lambda i: (0,) * a.ndim)
    o_spec = pl.BlockSpec((None, NSA_KV_HEADS, nseg, HEAD_DIM), lambda i: (i, 0, 0, 0))
    o_shape = jax.ShapeDtypeStruct((b, NSA_KV_HEADS, nseg, HEAD_DIM), BF16)
    return pl.pallas_call(
        functools.partial(_compress_prompt_kernel, nseg=nseg),
        grid=(b,),
        in_specs=[pl.BlockSpec((None, t, LANES), lambda i: (i, 0, 0)), pl.BlockSpec((None, t, LANES), lambda i: (i, 0, 1)),
                  full(pe), full(w1), full(w2), full(kn)],
        out_specs=[o_spec, o_spec],
        out_shape=[o_shape, o_shape],
        scratch_shapes=[pltpu.VMEM((nseg + 8, CMP_HIDDEN), F32)],
        compiler_params=_cparams(("arbitrary",), VMEM_LIMIT_BYTES),
        name="nsa_compress",
    )(kvr, kvr, pe, w1, w2, kn)


def _stack_heads(qs_ref, kh):
    parts = [qs_ref[:, (kh * NSA_GROUP + g) * HEAD_DIM:(kh * NSA_GROUP + g + 1) * HEAD_DIM].astype(F32)
             for g in range(NSA_GROUP)]
    return jnp.concatenate(parts, axis=0).astype(BF16)


def _cmp_topk_kernel(qs_ref, ck_ref, cv_ref, farcol_ref, chi_ref, clo_ref, pool_ref, ocmp_ref, msk_ref,
                     *, tq, nseg, q_base, n_pick, n_blk):
    qi = pl.program_id(1)
    G = NSA_GROUP
    q0 = q_base + qi * tq
    nbase = q0 // CMP_STRIDE - 16
    place = (lax.broadcasted_iota(jnp.int32, (32, nseg), 1) - lax.broadcasted_iota(jnp.int32, (32, nseg), 0)) == nbase
    place = jnp.where(place, 1.0, 0.0).astype(BF16)
    t1 = q0 + lax.broadcasted_iota(jnp.int32, (tq, 1), 0)
    t4 = jnp.concatenate([t1] * G, axis=0)
    n_i = lax.broadcasted_iota(jnp.int32, (G * tq, nseg), 1)
    valid = (n_i * CMP_STRIDE + (CMP_BLOCK - 1) <= t4) & (n_i <= nseg - 2)
    blk = lax.broadcasted_iota(jnp.int32, (tq, N_SELBLK), 1)
    cur = lax.shift_right_logical(t1, 6)
    forced = (blk == 0) | (blk == cur) | (blk == cur - 1)
    for kh in range(NSA_KV_HEADS):
        q4 = _stack_heads(qs_ref, kh)
        s = _dot_nt(q4, ck_ref[kh]) + farcol_ref[kh]
        s = s + jnp.dot(chi_ref[kh], place, preferred_element_type=F32) + jnp.dot(clo_ref[kh], place, preferred_element_type=F32)
        s = jnp.where(valid, s, NEG)
        e = jnp.where(valid, jnp.exp(s - jnp.max(s, axis=-1, keepdims=True)), 0.0)
        p = e / jnp.maximum(jnp.sum(e, axis=-1, keepdims=True), 1e-30)
        o = jnp.dot(p.astype(BF16), cv_ref[kh], preferred_element_type=F32)
        for g in range(G):
            h = kh * G + g
            ocmp_ref[:, h * HEAD_DIM:(h + 1) * HEAD_DIM] = o[g * tq:(g + 1) * tq]
        imp = p[0:tq] + p[tq:2 * tq] + p[2 * tq:3 * tq] + p[3 * tq:4 * tq]
        pooled = _dot_f32(imp, pool_ref[...])
        score = jnp.where((blk > cur) | (blk >= n_blk), -1e30, jnp.where(forced, 1e30, pooled))
        sel = jnp.zeros((tq, N_SELBLK), F32)
        for _ in range(n_pick):
            mx = jnp.max(score, axis=-1, keepdims=True)
            first = jnp.min(jnp.where(score == mx, blk, N_SELBLK), axis=-1, keepdims=True)
            pick = blk == first
            sel = jnp.where(pick, 1.0, sel)
            score = jnp.where(pick, -3e38, score)
        msk_ref[kh] = sel


def _cmp_topk_t_kernel(qt_ref, ck_ref, cv_ref, farrow_ref, chit_ref, clot_ref, poolt_ref, ocmpt_ref, mskt_ref,
                       *, tq, nseg, n_pick, n_blk):
    qi = pl.program_id(1)
    G = NSA_GROUP
    q0 = qi * tq
    nbase = q0 // CMP_STRIDE - 16
    t1 = q0 + lax.broadcasted_iota(jnp.int32, (1, tq), 1)
    blk = lax.broadcasted_iota(jnp.int32, (N_SELBLK, tq), 0)
    cur = lax.shift_right_logical(t1, 6)
    forced = (blk == 0) | (blk == cur) | (blk == cur - 1)

    def body(nv):
        place_t = (lax.broadcasted_iota(jnp.int32, (nv, 32), 0) - lax.broadcasted_iota(jnp.int32, (nv, 32), 1)) == nbase
        place_t = jnp.where(place_t, 1.0, 0.0).astype(BF16)
        n_i = lax.broadcasted_iota(jnp.int32, (nv, tq), 0)
        valid1 = (n_i * CMP_STRIDE + (CMP_BLOCK - 1) <= t1) & (n_i <= nseg - 2)
        valid = jnp.concatenate([valid1] * G, axis=1)
        for kh in range(NSA_KV_HEADS):
            qt4 = jnp.concatenate([qt_ref[(kh * G + g) * HEAD_DIM:(kh * G + g + 1) * HEAD_DIM, :] for g in range(G)],
                                  axis=1)
            s = jnp.dot(ck_ref[kh, 0:nv, :], qt4, preferred_element_type=F32) + farrow_ref[kh]
            s = s + jnp.dot(place_t, chit_ref[kh], preferred_element_type=F32) \
                  + jnp.dot(place_t, clot_ref[kh], preferred_element_type=F32)
            s = jnp.where(valid, s, NEG)
            e = jnp.where(valid, jnp.exp(s - jnp.max(s, axis=0, keepdims=True)), 0.0)
            p = e * (1.0 / jnp.maximum(jnp.sum(e, axis=0, keepdims=True), 1e-30))
            o_t = _dot_tn(cv_ref[kh, 0:nv, :], p.astype(BF16))
            for g in range(G):
                h = kh * G + g
                ocmpt_ref[h * HEAD_DIM:(h + 1) * HEAD_DIM, :] = o_t[:, g * tq:(g + 1) * tq]
            imp = p[:, 0:tq] + p[:, tq:2 * tq] + p[:, 2 * tq:3 * tq] + p[:, 3 * tq:4 * tq]
            pooled = _dot_f32(poolt_ref[:, 0:nv], imp)
            score = jnp.where((blk > cur) | (blk >= n_blk), -1e30, jnp.where(forced, 1e30, pooled))
            sel = jnp.zeros((N_SELBLK, tq), F32)
            for _ in range(n_pick):
                mx = jnp.max(score, axis=0, keepdims=True)
                first = jnp.min(jnp.where(score == mx, blk, N_SELBLK), axis=0, keepdims=True)
                pick = blk == first
                sel = jnp.where(pick, 1.0, sel)
                score = jnp.where(pick, -3e38, score)
            mskt_ref[kh] = sel

    n_chunks = nseg // CMP_CHUNK
    need = jnp.minimum(jnp.maximum(q0 + tq - CMP_BLOCK, 0) // CMP_STRIDE // CMP_CHUNK + 1, n_chunks)
    for v in range(1, n_chunks + 1):
        pl.when(need == v)(functools.partial(body, v * CMP_CHUNK))


def _cmp_topk_prompt(nqst, ck, cv, tbl, b, t, tq, n_pick):
    nseg = ck.shape[2]
    assert tq <= 256 and ck.shape[0] == b and nseg % CMP_CHUNK == 0
    farcol, chi, clo = _cmp_bias_tables(tbl, tq)
    farrow, chit, clot = (jnp.swapaxes(a, 1, 2) for a in (farcol, chi, clo))
    poolt = jnp.asarray((np.arange(N_SELBLK)[:, None] == np.arange(nseg)[None, :] // SEL_RATIO).astype(np.float32))
    full = lambda a: pl.BlockSpec(a.shape, lambda bi, i: (0,) * a.ndim)
    c_spec = pl.BlockSpec((None, NSA_KV_HEADS, nseg, HEAD_DIM), lambda bi, i: (bi, 0, 0, 0))
    nq = t // tq
    col_tile = pl.BlockSpec((NSA_WIDTH, tq), lambda bi, i: (0, bi * nq + i))
    return pl.pallas_call(
        functools.partial(_cmp_topk_t_kernel, tq=tq, nseg=nseg, n_pick=n_pick, n_blk=t // SEL_BLOCK),
        grid=(b, nq),
        in_specs=[col_tile, c_spec, c_spec, full(farrow), full(chit), full(clot), full(poolt)],
        out_specs=[col_tile, pl.BlockSpec((None, NSA_KV_HEADS, N_SELBLK, tq), lambda bi, i: (bi, 0, 0, i))],
        out_shape=[jax.ShapeDtypeStruct((NSA_WIDTH, b * t), F32),
                   jax.ShapeDtypeStruct((b, NSA_KV_HEADS, N_SELBLK, t), F32)],
        compiler_params=_cparams(("arbitrary", "arbitrary"), VMEM_LIMIT_BYTES),
        name="nsa_cmp_topk",
    )(nqst, ck, cv, farrow, chit, clot, poolt)


def _rel_bucket(dist):
    exact = NUM_BUCKETS // 2
    d = jnp.maximum(dist, 0)
    log_ratio = jnp.log(jnp.maximum(d, 1).astype(jnp.float32) / exact) / math.log(MAX_DISTANCE / exact)
    large = jnp.minimum(exact + (log_ratio * (NUM_BUCKETS - exact)).astype(jnp.int32), NUM_BUCKETS - 1)
    return jnp.where(d < exact, d, large)


def _bias_lookup(tbl, dist):
    onehot = _rel_bucket(dist)[..., None] == jnp.arange(NUM_BUCKETS)
    t = tbl.reshape((tbl.shape[0],) + (1,) * dist.ndim + (NUM_BUCKETS,))
    return jnp.sum(jnp.where(onehot[None], t, 0.0), axis=-1)


def _cmp_bias_tables(tbl, tq):
    tr = jnp.arange(tq)[:, None]
    i = jnp.arange(32)[None, :]
    dist = tr + 16 * CMP_STRIDE - CMP_STRIDE * i - (CMP_BLOCK - 1)
    near = _bias_lookup(tbl, dist)
    far = tbl[:, NUM_BUCKETS - 1]
    corr = (near - far[:, None, None]).reshape(NSA_KV_HEADS, NSA_GROUP * tq, 32)
    hi = corr.astype(BF16)
    lo = (corr - hi.astype(F32)).astype(BF16)
    farcol = jnp.broadcast_to(far[:, None, None], (NSA_HEADS, tq, 1)).reshape(NSA_KV_HEADS, NSA_GROUP * tq, 1)
    return farcol, hi, lo


def _cmp_topk_sample(qs, ck, cv, tbl, tq, q_base, n_pick, n_blk):
    b, t, _ = qs.shape
    nseg = ck.shape[2]
    assert tq <= 256
    farcol, chi, clo = _cmp_bias_tables(tbl, tq)
    pool = jnp.asarray((np.arange(nseg)[:, None] // SEL_RATIO == np.arange(N_SELBLK)[None, :]).astype(np.float32))
    full = lambda a: pl.BlockSpec(a.shape, lambda bi, i: (0,) * a.ndim)
    c_spec = pl.BlockSpec((None, NSA_KV_HEADS, nseg, HEAD_DIM), lambda bi, i: (bi, 0, 0, 0))
    m_spec = pl.BlockSpec((None, NSA_KV_HEADS, tq, N_SELBLK), lambda bi, i: (bi, 0, i, 0))
    m_shape = (b, NSA_KV_HEADS, t, N_SELBLK)
    return pl.pallas_call(
        functools.partial(_cmp_topk_kernel, tq=tq, nseg=nseg, q_base=q_base, n_pick=n_pick, n_blk=n_blk),
        grid=(b, t // tq),
        in_specs=[pl.BlockSpec((None, tq, NSA_WIDTH), lambda bi, i: (bi, i, 0)), c_spec, c_spec,
                  full(farcol), full(chi), full(clo), full(pool)],
        out_specs=[pl.BlockSpec((None, tq, NSA_WIDTH), lambda bi, i: (bi, i, 0)), m_spec],
        out_shape=[jax.ShapeDtypeStruct((b, t, NSA_WIDTH), F32), jax.ShapeDtypeStruct(m_shape, F32)],
        compiler_params=_cparams(("arbitrary", "arbitrary"), VMEM_LIMIT_BYTES),
        name="nsa_cmp_topk",
    )(qs, ck, cv, farcol, chi, clo, pool)


def _online_update(s, valid, v, m_ref, l_ref, acc_ref, v_t=False):
    s = jnp.where(valid, s, NEG)
    m_old = m_ref[...]
    m_new = jnp.maximum(m_old, jnp.max(s, axis=-1, keepdims=True))
    p = jnp.where(valid, jnp.exp(s - m_new), 0.0)
    alpha = jnp.exp(m_old - m_new)
    l_ref[...] = alpha * l_ref[...] + jnp.sum(p, axis=-1, keepdims=True)
    pb = p.astype(BF16)
    pv = _dot_nt(pb, v) if v_t else jnp.dot(pb, v, preferred_element_type=F32)
    acc_ref[...] = alpha * acc_ref[...] + pv
    m_ref[...] = m_new


def _selwin_kernel(qt_ref, selk_ref, selvt_ref, wink_ref, winvt_ref, mskt_ref, bias_ref, ocmpt_ref, gtt_ref, o_ref,
                   m_ref, l_ref, acc_ref, ot_ref, qt4_ref, *, tq):
    qi = pl.program_id(1)
    G = NSA_GROUP
    blocks_per_tile = tq // SEL_BLOCK
    s_rel = lax.broadcasted_iota(jnp.int32, (tq, tq), 0)
    t_rel = lax.broadcasted_iota(jnp.int32, (tq, tq), 1)
    causal = s_rel <= t_rel
    all_rows, near_rows, diag_rows = slice(0, 3 * tq), slice(tq, 3 * tq), slice(2 * tq, 3 * tq)

    def init():
        m_ref[...] = jnp.full_like(m_ref, NEG)
        l_ref[...] = jnp.zeros_like(l_ref)
        acc_ref[...] = jnp.zeros_like(acc_ref)

    def update(kh, k, vt, bias_rows, valid):
        s = jnp.dot(k, qt4_ref[...], preferred_element_type=F32)
        if bias_rows is not None:
            s = s + bias_ref[kh, bias_rows, :]
        if valid is not None:
            neg = jnp.where(valid, 0.0, NEG)
            s = s + jnp.concatenate([neg] * G, axis=1)
        m_old = m_ref[0:1, :]
        m_new = jnp.maximum(m_old, jnp.max(s, axis=0, keepdims=True))
        p = jnp.exp(s - m_new)
        alpha = jnp.exp(m_old - m_new)
        l_ref[0:1, :] = alpha * l_ref[0:1, :] + jnp.sum(p, axis=0, keepdims=True)
        acc_ref[...] = alpha * acc_ref[...] + jnp.dot(vt, p.astype(BF16), preferred_element_type=F32)
        m_ref[0:1, :] = m_new

    def finish(kh, gate_row):
        for g in range(G):
            h = kh * G + g
            cols = slice(g * tq, (g + 1) * tq)
            rows = slice(h * HEAD_DIM, (h + 1) * HEAD_DIM)
            o = acc_ref[:, cols] / jnp.maximum(l_ref[0:1, cols], 1e-30)
            r = 3 * h + gate_row
            ot_ref[rows, :] += gtt_ref[r:r + 1, :] * o

    for h in range(NSA_HEADS):
        rows = slice(h * HEAD_DIM, (h + 1) * HEAD_DIM)
        ot_ref[rows, :] = gtt_ref[3 * h:3 * h + 1, :] * ocmpt_ref[rows, :]

    for kh in range(NSA_KV_HEADS):
        kcols = slice(kh * HEAD_DIM, (kh + 1) * HEAD_DIM)
        vrows = slice(kh * HEAD_DIM, (kh + 1) * HEAD_DIM)
        for g in range(G):
            h = kh * G + g
            qt4_ref[:, g * tq:(g + 1) * tq] = qt_ref[h * HEAD_DIM:(h + 1) * HEAD_DIM, :]

        def sel_valid(j):
            parts = [jnp.broadcast_to(mskt_ref[kh, pl.ds(j * blocks_per_tile + i, 1), :], (SEL_BLOCK, tq))
                     for i in range(blocks_per_tile)]
            return jnp.concatenate(parts, axis=0) > 0.5

        def sel_update(j0, n_tiles, bias_rows, last_is_diag):
            start = pl.multiple_of(j0 * tq, tq)
            parts = [sel_valid(j0 + i) for i in range(n_tiles)]
            if last_is_diag:
                parts[-1] = parts[-1] & causal
            update(kh, selk_ref[pl.ds(start, n_tiles * tq), kcols], selvt_ref[vrows, pl.ds(start, n_tiles * tq)],
                   bias_rows, jnp.concatenate(parts, axis=0))

        def win_update(j0, valid_parts, bias_rows):
            n = len(valid_parts) * tq
            start = pl.multiple_of(j0 * tq, tq)
            update(kh, wink_ref[pl.ds(start, n), kcols], winvt_ref[vrows, pl.ds(start, n)], bias_rows,
                   jnp.concatenate(valid_parts, axis=0))

        init()

        def far_group(jg, c):
            sel_update(jg * FAR_GROUP, FAR_GROUP, None, False)
            return c

        def far_single(j, c):
            sel_update(j, 1, None, False)
            return c

        n_far = jnp.maximum(qi - 1, 0)
        n_grp = n_far // FAR_GROUP
        lax.fori_loop(0, n_grp, far_group, 0)
        lax.fori_loop(n_grp * FAR_GROUP, n_far, far_single, 0)

        @pl.when(qi >= 1)
        def _():
            sel_update(qi - 1, 2, near_rows, True)

        @pl.when(qi == 0)
        def _():
            sel_update(0, 1, diag_rows, True)

        finish(kh, 1)

        init()
        all_valid = s_rel >= 0

        @pl.when(qi >= 2)
        def _():
            win_update(qi - 2, [s_rel >= t_rel, all_valid, causal], all_rows)

        @pl.when(qi == 1)
        def _():
            win_update(0, [all_valid, causal], near_rows)

        @pl.when(qi == 0)
        def _():
            win_update(0, [causal], diag_rows)

        finish(kh, 2)

    o_ref[...] = ot_ref[...].T


def _selwin_bias_tables(tbl, tq):
    sr = jnp.arange(tq)[:, None]
    tr = jnp.arange(tq)[None, :]
    far = tbl[:, NUM_BUCKETS - 1][:, None, None]
    near0 = _bias_lookup(tbl, tr - sr) - far
    near1 = _bias_lookup(tbl, tr - sr + tq) - far
    b = jnp.concatenate([jnp.zeros_like(near1), near1, near0], axis=1)
    b = b.reshape(NSA_KV_HEADS, NSA_GROUP, 3 * tq, tq).transpose(0, 2, 1, 3)
    return b.reshape(NSA_KV_HEADS, 3 * tq, NSA_GROUP * tq)


def _selwin_prompt(nqst, selkv, selvt, winkv, winvt, mskt, tbl, ocmp, gtt, b, t, tq):
    assert tq >= NSA_WINDOW // 2 and tq >= MAX_DISTANCE and tq % SEL_BLOCK == 0
    bias = _selwin_bias_tables(tbl, tq)
    nq = t // tq
    col_tile = lambda h: pl.BlockSpec((h, tq), lambda bi, i: (0, bi * nq + i))
    row_tile = lambda w: pl.BlockSpec((tq, w), lambda bi, i: (bi * nq + i, 0))
    return pl.pallas_call(
        functools.partial(_selwin_kernel, tq=tq),
        grid=(b, nq),
        in_specs=[col_tile(NSA_WIDTH),
                  pl.BlockSpec((t, LANES), lambda bi, i: (bi, 0)), pl.BlockSpec((LANES, t), lambda bi, i: (0, bi)),
                  pl.BlockSpec((t, LANES), lambda bi, i: (bi, 0)), pl.BlockSpec((LANES, t), lambda bi, i: (0, bi)),
                  pl.BlockSpec((None, NSA_KV_HEADS, N_SELBLK, tq), lambda bi, i: (bi, 0, 0, i)),
                  pl.BlockSpec(bias.shape, lambda bi, i: (0, 0, 0)),
                  col_tile(NSA_WIDTH), col_tile(32)],
        out_specs=row_tile(NSA_WIDTH),
        out_shape=jax.ShapeDtypeStruct((b * t, NSA_WIDTH), F32),
        scratch_shapes=[pltpu.VMEM((8, NSA_GROUP * tq), F32), pltpu.VMEM((8, NSA_GROUP * tq), F32),
                        pltpu.VMEM((HEAD_DIM, NSA_GROUP * tq), F32), pltpu.VMEM((NSA_WIDTH, tq), F32),
                        pltpu.VMEM((HEAD_DIM, NSA_GROUP * tq), BF16)],
        compiler_params=_cparams(("arbitrary", "arbitrary"), VMEM_LIMIT_BYTES),
        name="nsa_selwin",
    )(nqst, selkv, selvt, winkv, winvt, mskt, bias, ocmp, gtt)


PAGES_PER_STEP = 16
PAGE = 128


def _feature_major(cache):
    nd = cache.ndim
    t = jnp.transpose(cache, (0, 1) + tuple(range(3, nd)) + (2,))
    return t.reshape(cache.shape[0], cache.shape[1], -1, cache.shape[2])


def _page_specs(layer, rows, row_block, reverse_steps=None):
    group = (lambda j: j) if reverse_steps is None else (lambda j: reverse_steps - 1 - j)
    return [pl.BlockSpec((None, None, rows, PAGE), functools.partial(
        lambda bi, j, pt, r: (layer, pt[bi, group(j) * PAGES_PER_STEP + r], row_block, 0), r=r))
        for r in range(PAGES_PER_STEP)]


def _compress_sample_kernel(pt_ref, *refs, nseg):
    pages = refs[:PAGES_PER_STEP]
    pe_ref, w1_ref, w2_ref, kn_ref, ck_ref, cv_ref, srck_ref, srcv_ref, sh_ref = refs[PAGES_PER_STEP:]
    j = pl.program_id(1)
    for r, p_ref in enumerate(pages):
        rows = pl.ds(pl.multiple_of((j * PAGES_PER_STEP + r) * PAGE, PAGE), PAGE)
        srck_ref[rows, :] = p_ref[0:LANES, :].T
        srcv_ref[rows, :] = p_ref[LANES:2 * LANES, :].T

    @pl.when(j == pl.num_programs(1) - 1)
    def _():
        _compress_compute((srck_ref, srcv_ref), pe_ref, w1_ref, w2_ref, kn_ref, ck_ref, cv_ref, sh_ref, nseg)


def _compress_sample(cache_t, layer, page_table, pe, w1, w2, kn):
    sb, n_pages = page_table.shape
    past = n_pages * PAGE
    nseg = past // CMP_STRIDE
    full = lambda a: pl.BlockSpec(a.shape, lambda bi, j, pt: (0,) * a.ndim)
    o_spec = pl.BlockSpec((None, NSA_KV_HEADS, nseg, HEAD_DIM), lambda bi, j, pt: (bi, 0, 0, 0))
    o_shape = jax.ShapeDtypeStruct((sb, NSA_KV_HEADS, nseg, HEAD_DIM), BF16)
    return pl.pallas_call(
        functools.partial(_compress_sample_kernel, nseg=nseg),
        grid_spec=pltpu.PrefetchScalarGridSpec(
            num_scalar_prefetch=1, grid=(sb, n_pages // PAGES_PER_STEP),
            in_specs=_page_specs(layer, 256, 0) + [full(pe), full(w1), full(w2), full(kn)],
            out_specs=[o_spec, o_spec],
            scratch_shapes=[pltpu.VMEM((past, LANES), F32), pltpu.VMEM((past, LANES), F32),
                            pltpu.VMEM((nseg + 8, CMP_HIDDEN), F32)]),
        out_shape=[o_shape, o_shape],
        compiler_params=_cparams(("arbitrary", "arbitrary"), VMEM_LIMIT_BYTES),
        name="nsa_compress_sample",
    )(page_table, *([cache_t] * PAGES_PER_STEP), pe, w1, w2, kn)


def _selwin_sample_kernel(pt_ref, qs_ref, *refs):
    pages = refs[:PAGES_PER_STEP]
    (msk_ref, newkv_ref, winst_ref, newwr_ref, bsel_ref, bnew_ref, bwin_ref, expand_ref, ocmp_ref, gt_ref, o_ref,
     kv_ref, m_ref, l_ref, acc_ref, osel_ref, owin_ref) = refs[PAGES_PER_STEP:]
    j = pl.program_id(1)
    G = NSA_GROUP
    R = SAMPLE_ROWS
    for r, p_ref in enumerate(pages):
        kv_ref[:, pl.ds(pl.multiple_of((j * PAGES_PER_STEP + r) * PAGE, PAGE), PAGE)] = p_ref[...].astype(BF16)

    @pl.when(j == pl.num_programs(1) - 1)
    def _():
        rq = lax.broadcasted_iota(jnp.int32, (G * R, 1), 0) & (R - 1)
        new_valid = lax.broadcasted_iota(jnp.int32, (G * R, R), 1) <= rq
        win_valid = lax.broadcasted_iota(jnp.int32, (G * R, NSA_WINDOW), 1) >= rq

        def init():
            m_ref[...] = jnp.full_like(m_ref, NEG)
            l_ref[...] = jnp.zeros_like(l_ref)
            acc_ref[...] = jnp.zeros_like(acc_ref)

        def finish(dst_ref, kh):
            o = acc_ref[...] / jnp.maximum(l_ref[...], 1e-30)
            for g in range(G):
                h = kh * G + g
                dst_ref[:, h * HEAD_DIM:(h + 1) * HEAD_DIM] = o[g * R:(g + 1) * R]

        for kh in range(NSA_KV_HEADS):
            kcols = slice(kh * HEAD_DIM, (kh + 1) * HEAD_DIM)
            vcols = slice(128 + kh * HEAD_DIM, 128 + (kh + 1) * HEAD_DIM)
            q4 = _stack_heads(qs_ref, kh)
            mskb = msk_ref[kh].astype(BF16)

            init()
            mt = jnp.dot(mskb, expand_ref[...], preferred_element_type=F32) > 0.5
            valid = jnp.concatenate([mt] * G, axis=0)
            s = jnp.dot(q4, kv_ref[kcols, :], preferred_element_type=F32) + bsel_ref[kh]
            _online_update(s, valid, kv_ref[vcols, :], m_ref, l_ref, acc_ref, v_t=True)
            knew = newkv_ref[:, 256 + kh * HEAD_DIM:256 + (kh + 1) * HEAD_DIM].astype(BF16)
            vnew = newkv_ref[:, 384 + kh * HEAD_DIM:384 + (kh + 1) * HEAD_DIM].astype(BF16)
            _online_update(_dot_nt(q4, knew) + bnew_ref[kh], new_valid, vnew, m_ref, l_ref, acc_ref)
            finish(osel_ref, kh)

            init()
            kwin = winst_ref[kcols, :].astype(BF16)
            vwin = winst_ref[vcols, :].astype(BF16)
            _online_update(jnp.dot(q4, kwin, preferred_element_type=F32) + bwin_ref[kh], win_valid, vwin,
                           m_ref, l_ref, acc_ref, v_t=True)
            knew = newwr_ref[:, kcols].astype(BF16)
            vnew = newwr_ref[:, vcols].astype(BF16)
            _online_update(_dot_nt(q4, knew) + bnew_ref[kh], new_valid, vnew, m_ref, l_ref, acc_ref)
            finish(owin_ref, kh)

        for h in range(NSA_HEADS):
            cols = slice(h * HEAD_DIM, (h + 1) * HEAD_DIM)
            o_ref[:, cols] = (gt_ref[:, 3 * h:3 * h + 1] * ocmp_ref[:, cols]
                              + gt_ref[:, 3 * h + 1:3 * h + 2] * osel_ref[:, cols]
                              + gt_ref[:, 3 * h + 2:3 * h + 3] * owin_ref[:, cols])


def _sample_bias_tables(tbl, past):
    R = SAMPLE_ROWS
    r = jnp.arange(R)[:, None]
    stack = lambda a: a.reshape(NSA_KV_HEADS, NSA_GROUP * R, a.shape[-1])
    cached = _bias_lookup(tbl, past + r - jnp.arange(past)[None, :])
    new = _bias_lookup(tbl, r - jnp.arange(R)[None, :])
    win = _bias_lookup(tbl, NSA_WINDOW + r - jnp.arange(NSA_WINDOW)[None, :])
    return stack(cached), stack(new), stack(win)


def _selwin_sample(qs, cache_t, layer, page_table, msk, newkv, win_t, newwr, tbl, ocmp, gt):
    sb, n_pages = page_table.shape
    past = n_pages * PAGE
    assert win_t.shape[-1] == NSA_WINDOW and past >= NSA_WINDOW and past // SEL_BLOCK <= N_SELBLK
    bsel, bnew, bwin = _sample_bias_tables(tbl, past)
    expand = jnp.asarray((np.arange(N_SELBLK)[:, None] == np.arange(past)[None, :] // SEL_BLOCK).astype(np.float32),
                         dtype=BF16)
    R = SAMPLE_ROWS
    full = lambda a: pl.BlockSpec(a.shape, lambda bi, j, pt: (0,) * a.ndim)
    seq = lambda a: pl.BlockSpec((None,) + a.shape[1:], lambda bi, j, pt: (bi,) + (0,) * (a.ndim - 1))
    win_spec = pl.BlockSpec((None, None) + win_t.shape[2:], lambda bi, j, pt: (layer, bi, 0, 0))
    return pl.pallas_call(
        _selwin_sample_kernel,
        grid_spec=pltpu.PrefetchScalarGridSpec(
            num_scalar_prefetch=1, grid=(sb, n_pages // PAGES_PER_STEP),
            in_specs=[seq(qs)] + _page_specs(layer, 256, 1) + [seq(msk), seq(newkv), win_spec, seq(newwr),
                                                               full(bsel), full(bnew), full(bwin), full(expand),
                                                               seq(ocmp), seq(gt)],
            out_specs=pl.BlockSpec((None, R, NSA_WIDTH), lambda bi, j, pt: (bi, 0, 0)),
            scratch_shapes=[pltpu.VMEM((256, past), BF16),
                            pltpu.VMEM((NSA_GROUP * R, 1), F32), pltpu.VMEM((NSA_GROUP * R, 1), F32),
                            pltpu.VMEM((NSA_GROUP * R, HEAD_DIM), F32),
                            pltpu.VMEM((R, NSA_WIDTH), F32), pltpu.VMEM((R, NSA_WIDTH), F32)]),
        out_shape=jax.ShapeDtypeStruct((sb, R, NSA_WIDTH), F32),
        compiler_params=_cparams(("arbitrary", "arbitrary"), VMEM_LIMIT_BYTES),
        name="nsa_selwin_sample",
    )(page_table, qs, *([cache_t] * PAGES_PER_STEP), msk, newkv, win_t, newwr, bsel, bnew, bwin, expand, ocmp, gt)


def _fox_sample_kernel(pt_ref, qs_ref, *refs):
    P = PAGES_PER_STEP
    pages = refs[:P]
    lf_pages = refs[P:2 * P]
    (sl_ref, newkv_ref, lfnew_ref, hmask_ref, o_ref,
     qbd_ref, crel_ref, m_ref, l_ref, acc_ref, carry_ref) = refs[2 * P:]
    j = pl.program_id(1)
    R = SAMPLE_ROWS
    H = FOX_HEADS
    tk = P * PAGE

    @pl.when(j == 0)
    def _():
        carry_ref[...] = jnp.zeros_like(carry_ref)
        q = qs_ref[...].astype(F32)
        qbd_ref[...] = (jnp.concatenate([q] * H, axis=0) * hmask_ref[...]).astype(BF16)
        tri = jnp.where(lax.broadcasted_iota(jnp.int32, (R, R), 1) <= lax.broadcasted_iota(jnp.int32, (R, R), 0), 1.0, 0.0)
        crel = _dot_f32(tri, lfnew_ref[...])
        crel_ref[...] = jnp.concatenate([crel[:, h:h + 1] for h in range(H)], axis=0)
        m_ref[...] = jnp.full_like(m_ref, NEG)
        l_ref[...] = jnp.zeros_like(l_ref)
        acc_ref[...] = jnp.zeros_like(acc_ref)

    carry = carry_ref[:, 0:1]
    d_parts = [None] * P
    for r in reversed(range(P)):
        lf = lf_pages[r][...]
        d_parts[r] = _dot_f32(lf, sl_ref[...]) + carry
        carry = carry + jnp.sum(lf, axis=1, keepdims=True)
    carry_ref[...] = jnp.broadcast_to(carry, carry_ref.shape)
    dt = jnp.concatenate(d_parts, axis=1)

    kt = jnp.concatenate([p[0:FOX_WIDTH, :] for p in pages], axis=1).astype(BF16)
    vt = jnp.concatenate([p[FOX_WIDTH:, :] for p in pages], axis=1).astype(BF16)
    drows = jnp.concatenate([jnp.broadcast_to(dt[h:h + 1, :], (R, tk)) for h in range(H)], axis=0)
    s = jnp.dot(qbd_ref[...], kt, preferred_element_type=F32) + crel_ref[...] + drows
    _online_update(s, s > 2 * NEG, vt, m_ref, l_ref, acc_ref, v_t=True)

    @pl.when(j == pl.num_programs(1) - 1)
    def _():
        lf = lfnew_ref[...]
        iu = lax.broadcasted_iota(jnp.int32, (R, R), 0)
        ir = lax.broadcasted_iota(jnp.int32, (R, R), 1)
        a_le = jnp.where(ir <= iu, 1.0, 0.0)
        b_gt = jnp.where(iu > ir, 1.0, 0.0)
        dnew = jnp.concatenate([_dot_f32(a_le, lf[:, h:h + 1] * b_gt) for h in range(H)], axis=0)
        rq = lax.broadcasted_iota(jnp.int32, (H * R, 1), 0) & (R - 1)
        valid = lax.broadcasted_iota(jnp.int32, (H * R, R), 1) <= rq
        knew = newkv_ref[:, 0:FOX_WIDTH].astype(BF16)
        vnew = newkv_ref[:, FOX_WIDTH:].astype(BF16)
        _online_update(_dot_nt(qbd_ref[...], knew) + dnew, valid, vnew, m_ref, l_ref, acc_ref)
        o = (acc_ref[...] / jnp.maximum(l_ref[...], 1e-30)) * hmask_ref[...]
        out = o[0:R]
        for h in range(1, H):
            out = out + o[h * R:(h + 1) * R]
        o_ref[...] = out


def _fox_sample(qs, kv_t, logf_t, layer, page_table, newkv, lfnew):
    sb, n_pages = page_table.shape
    R = SAMPLE_ROWS
    n_steps = n_pages // PAGES_PER_STEP
    hmask = jnp.asarray((np.arange(FOX_HEADS * R)[:, None] // R == np.arange(FOX_WIDTH)[None, :] // HEAD_DIM)
                        .astype(np.float32))
    sl = jnp.asarray(np.tril(np.ones((PAGE, PAGE), np.float32), -1))
    seq = lambda a: pl.BlockSpec((None,) + a.shape[1:], lambda bi, j, pt: (bi,) + (0,) * (a.ndim - 1))
    full = lambda a: pl.BlockSpec(a.shape, lambda bi, j, pt: (0,) * a.ndim)
    return pl.pallas_call(
        _fox_sample_kernel,
        grid_spec=pltpu.PrefetchScalarGridSpec(
            num_scalar_prefetch=1, grid=(sb, n_steps),
            in_specs=[seq(qs)] + _page_specs(layer, 2 * FOX_WIDTH, 0, n_steps) + _page_specs(layer, FOX_HEADS, 0, n_steps)
                     + [full(sl), seq(newkv), seq(lfnew), full(hmask)],
            out_specs=pl.BlockSpec((None, R, FOX_WIDTH), lambda bi, j, pt: (bi, 0, 0)),
            scratch_shapes=[pltpu.VMEM((FOX_HEADS * R, FOX_WIDTH), BF16), pltpu.VMEM((FOX_HEADS * R, 1), F32),
                            pltpu.VMEM((FOX_HEADS * R, 1), F32), pltpu.VMEM((FOX_HEADS * R, 1), F32),
                            pltpu.VMEM((FOX_HEADS * R, FOX_WIDTH), F32), pltpu.VMEM((FOX_HEADS, LANES), F32)]),
        out_shape=jax.ShapeDtypeStruct((sb, R, FOX_WIDTH), F32),
        compiler_params=_cparams(("arbitrary", "arbitrary"), VMEM_LIMIT_BYTES),
        name="fox_sample",
    )(page_table, qs, *([kv_t] * PAGES_PER_STEP), *([logf_t] * PAGES_PER_STEP), sl, newkv, lfnew, hmask)


def _pad_cols(w, width):
    return jnp.pad(w, ((0, 0), (0, width - w.shape[1])))


def _prep_even_w(w):
    gq, gk, gv, glr, gog, nq, nkv, ng = _split(w, EVEN_SIZES)
    return jnp.concatenate([gq, gk, gv, gog, nq, nkv, _pad_cols(glr, LANES), _pad_cols(ng, LANES)], axis=1).astype(BF16)


def _prep_odd_w(w):
    fq, fk, fv, ff, cg = _split(w, ODD_SIZES)
    return jnp.concatenate([fq, fk, fv, cg, _pad_cols(ff, LANES)], axis=1).astype(BF16)


def _row(v):
    return v.reshape(1, -1).astype(F32)


def _tile_row(v, reps):
    return jnp.tile(v.astype(F32), reps).reshape(1, -1)


def kernel(x_prompt, x_sample, cache_nsa_kv, state_nsa_win, state_gla, cache_fox_kv, cache_fox_logf, state_conv, cache_mem_kv, page_table, mem_prompt, rel_bias, norm_mix, norm_xattn, norm_ffn, even_w_in, even_w_out, gla_w_gate, gla_b_gate, gla_out_norm, nsa_q_norm, nsa_k_norm, nsa_cmp_pe, nsa_cmp_w1, nsa_cmp_w2, odd_w_in, odd_w_out, fox_q_norm, fox_k_norm, fox_b_f, conv_w, conv_b, conv_ln_g, conv_ln_b, mem_norm, xa_wq, xa_wkv, xa_wo, xa_q_norm, xa_k_norm, ffn_w_in, ffn_w_out):
    B, T, _ = x_prompt.shape
    SB, SQ, _ = x_sample.shape
    depth = norm_mix.shape[0]
    past_len = page_table.shape[1] * cache_nsa_kv.shape[2]
    MP = B * T
    SR = SAMPLE_ROWS
    MS = SB * SR

    yp = x_prompt.reshape(MP, D_MODEL)
    ys = jnp.pad(x_sample, ((0, 0), (0, SR - SQ), (0, 0))).reshape(MS, D_MODEL)

    nsa_cache_t = _feature_major(cache_nsa_kv)
    fox_cache_t = _feature_major(cache_fox_kv)
    logf_t = _feature_major(cache_fox_logf)
    win_t = _feature_major(state_nsa_win)
    mem_cache_t = _feature_major(cache_mem_kv)

    nsa_kv_p, nsa_kv_s, win_p, win_s, gla_p, gla_s = [], [], [], [], [], []
    fox_kv_p, fox_kv_s, logf_p, logf_s, conv_p, conv_s, mem_kv_p = [], [], [], [], [], [], []

    for layer in range(depth):
        if layer % 2 == 0:
            e = layer // 2
            w_pad = _prep_even_w(even_w_in[e])
            wg_pad = jnp.pad(gla_w_gate[e], ((0, LANES - GLA_RANK), (0, 0))).astype(BF16)
            bg = _row(gla_b_gate[e])
            qn = _tile_row(nsa_q_norm[e], NSA_HEADS)
            kn1 = _tile_row(nsa_k_norm[e, 1], NSA_KV_HEADS)
            kn2 = _tile_row(nsa_k_norm[e, 2], NSA_KV_HEADS)
            gn = _row(gla_out_norm[e])
            w_out = even_w_out[e].astype(BF16)
            g_mix = _row(norm_mix[layer])
            pe = jnp.tile(nsa_cmp_pe[e].astype(F32), (1, 1, NSA_KV_HEADS))
            w1b = _compress_weights(nsa_cmp_w1[e])
            w2b = nsa_cmp_w2[e].astype(BF16)
            kn0 = _row(nsa_k_norm[e, 0])
            tbl = rel_bias.astype(F32).T
            (q, k, v, la, og, nqs, kvr, wr, gt, selkv, winkv, nqst, selvt, winvt, gtt) = _even_in(
                yp, g_mix, w_pad, wg_pad, bg, qn, kn1, kn2, 512)
            r3 = lambda a: a.reshape(B, T, a.shape[-1])
            s0t = jnp.zeros((B, GLA_HEADS, GLA_DV, GLA_DK), F32)
            o_gla, sfin_t = _gla(r3(q), r3(k), r3(v), r3(la), r3(og), gn, s0t, 256, B)
            kvr5 = kvr.reshape(B, T, 4, NSA_KV_HEADS, HEAD_DIM)
            wr5 = wr.reshape(B, T, 2, NSA_KV_HEADS, HEAD_DIM)
            ck, cv = _compress_prompt(r3(kvr), pe, w1b, w2b, kn0)
            ocmp_t, mskt = _cmp_topk_prompt(nqst, ck, cv, tbl, B, T, 256, SEL_TOPN)
            o_nsa = _selwin_prompt(nqst, selkv, selvt, winkv, winvt, mskt, tbl, ocmp_t, gtt, B, T, 256)
            yp = _out_proj(yp, o_gla.reshape(MP, GLA_WIDTH), o_nsa.reshape(MP, NSA_WIDTH),
                           w_out[:GLA_WIDTH], w_out[GLA_WIDTH:], 1024)
            nsa_kv_p.append(kvr5)
            win_p.append(wr5[:, -min(NSA_WINDOW, T):])
            gla_p.append(jnp.swapaxes(sfin_t, -1, -2))
            (q, k, v, la, og, nqs, kvr, wr, gt) = _even_in(ys, g_mix, w_pad, wg_pad, bg, qn, kn1, kn2, MS)[:9]
            pad16 = lambda a: jnp.pad(a.reshape(SB, SR, a.shape[-1]), ((0, 0), (0, SUB - SR), (0, 0)))
            s0t = jnp.swapaxes(state_gla[e], -1, -2)
            o_gla, snew_t = _gla(pad16(q), pad16(k), pad16(v), pad16(la), pad16(og), gn, s0t, SUB, 4, n_valid=SQ)
            o_gla = o_gla[:, :SR]
            s3 = lambda a: a.reshape(SB, SR, a.shape[-1])
            kvr5 = kvr.reshape(SB, SR, 4, NSA_KV_HEADS, HEAD_DIM)[:, :SQ]
            wr5 = wr.reshape(SB, SR, 2, NSA_KV_HEADS, HEAD_DIM)[:, :SQ]
            ck, cv = _compress_sample(nsa_cache_t, e, page_table, pe, w1b, w2b, kn0)
            ocmp, msk = _cmp_topk_sample(s3(nqs), ck, cv, tbl, SR, past_len, SEL_TOPN - 1, past_len // SEL_BLOCK)
            o_nsa = _selwin_sample(s3(nqs), nsa_cache_t, e, page_table, msk, s3(kvr), win_t, s3(wr), tbl, ocmp, s3(gt))
            ys = _out_proj(ys, o_gla.reshape(MS, GLA_WIDTH), o_nsa.reshape(MS, NSA_WIDTH),
                           w_out[:GLA_WIDTH], w_out[GLA_WIDTH:], MS)
            nsa_kv_s.append(kvr5)
            win_s.append(jnp.concatenate([state_nsa_win[e][:, SQ:], wr5], axis=1))
            gla_s.append(jnp.swapaxes(snew_t, -1, -2))
        else:
            j = layer // 2
            w_pad = _prep_odd_w(odd_w_in[j])
            qn = _tile_row(fox_q_norm[j], FOX_HEADS)
            kn = _tile_row(fox_k_norm[j], FOX_HEADS)
            bf_pad = jnp.pad(fox_b_f[j].astype(F32), (0, LANES - FOX_HEADS)).reshape(1, LANES)
            w_out = odd_w_out[j].astype(BF16)
            g_mix = _row(norm_mix[layer])
            cw = conv_w[j].astype(F32)
            cb, cg_, cbeta = _row(conv_b[j]), _row(conv_ln_g[j]), _row(conv_ln_b[j])
            qs, kv, kb, lf, c, u, qst, vbt = _odd_in(yp.reshape(B, T, D_MODEL), g_mix, w_pad, qn, kn, bf_pad, 256)
            c8 = c[:, :, :FOX_HEADS]
            o_fox = _fox_prompt(qst, kb, vbt, c8, jnp.swapaxes(c8, 1, 2), 512, 1024)
            o_conv, cst = _conv(u, jnp.zeros((B, CONV_WIDTH - 1, CONV_CH), F32), cw, cb, cg_, cbeta, 1024, 1024)
            yp = _out_proj(yp, o_fox.reshape(MP, FOX_WIDTH), o_conv.reshape(MP, CONV_CH),
                           w_out[:FOX_WIDTH], w_out[FOX_WIDTH:], 1024)
            fox_kv_p.append(kv.reshape(B, T, 2, FOX_HEADS, HEAD_DIM))
            logf_p.append(lf[:, :, :FOX_HEADS])
            conv_p.append(cst)
            qs, kv, kb, lf, c, u = _odd_in(ys.reshape(1, MS, D_MODEL), g_mix, w_pad, qn, kn, bf_pad, MS)[:6]
            s3 = lambda a: a.reshape(SB, SR, a.shape[-1])
            new_kv = kv.reshape(SB, SR, 2, FOX_HEADS, HEAD_DIM)[:, :SQ]
            lf_new = lf.reshape(SB, SR, LANES)[:, :SQ, :FOX_HEADS]
            o_fox = _fox_sample(s3(qs), fox_cache_t, logf_t, j, page_table, s3(kv), s3(lf))
            o_conv, cst = _conv(u.reshape(SB, SR, CONV_CH), state_conv[j], cw, cb, cg_, cbeta, SR, SQ)
            ys = _out_proj(ys, o_fox.reshape(MS, FOX_WIDTH), o_conv.reshape(MS, CONV_CH),
                           w_out[:FOX_WIDTH], w_out[FOX_WIDTH:], MS)
            fox_kv_s.append(new_kv)
            logf_s.append(lf_new)
            conv_s.append(cst)
        g_xa = _row(norm_xattn[layer])
        wq = xa_wq[layer].astype(BF16)
        wo = xa_wo[layer].astype(BF16)
        xqn = _tile_row(xa_q_norm[layer], XA_HEADS)
        mkv_t = _mem_kv(mem_prompt, _row(mem_norm[layer]), xa_wkv[layer].astype(BF16), _tile_row(xa_k_norm[layer], XA_HEADS))
        mem_kv_p.append(jnp.transpose(mkv_t.reshape(B, 2, XA_HEADS, HEAD_DIM, MEM_LEN), (0, 4, 1, 2, 3)))
        yp = _xattn(yp.reshape(B, T, D_MODEL), mkv_t[None], 0, g_xa, wq, wo, xqn, 1024).reshape(MP, D_MODEL)
        ys = _xattn(ys.reshape(SB, SR, D_MODEL), mem_cache_t, layer, g_xa, wq, wo, xqn, SR).reshape(MS, D_MODEL)
        g_ffn = _row(norm_ffn[layer])
        w_in = ffn_w_in[layer].astype(BF16)
        w_o = ffn_w_out[layer].astype(BF16)
        yp = _ffn(yp, g_ffn, w_in, w_o, 512)
        ys = _ffn(ys, g_ffn, w_in, w_o, MS)

    yp = yp.reshape(B, T, D_MODEL)
    ys = ys.reshape(SB, SR, D_MODEL)[:, :SQ]
    return (yp, ys,
            jnp.stack(nsa_kv_p), jnp.stack(nsa_kv_s), jnp.stack(win_p), jnp.stack(win_s),
            jnp.stack(gla_p), jnp.stack(gla_s), jnp.stack(fox_kv_p), jnp.stack(fox_kv_s),
            jnp.stack(logf_p), jnp.stack(logf_s), jnp.stack(conv_p), jnp.stack(conv_s),
            jnp.stack(mem_kv_p))
```

```python
import functools
import math

import jax
import jax.numpy as jnp
import numpy as np
from jax import lax
from jax.experimental import pallas as pl
from jax.experimental.pallas import tpu as pltpu

F32 = jnp.float32
BF16 = jnp.bfloat16

D_MODEL = 1024
HEAD_DIM = 64
GLA_WIDTH = 512
GLA_HEADS = 4
GLA_DV = 128
GLA_DK = 64
GLA_RANK = 16
GLA_TAU = 16.0
NSA_WIDTH = 512
NSA_HEADS = 8
NSA_KV_HEADS = 2
NSA_GROUP = 4
CMP_STRIDE = 16
CMP_BLOCK = 32
CMP_HIDDEN = 256
SEL_BLOCK = 64
SEL_RATIO = 4
SEL_TOPN = 16
NSA_WINDOW = 512
FOX_WIDTH = 512
FOX_HEADS = 8
CONV_CH = 512
CONV_WIDTH = 31
MEM_LEN = 256
XA_HEADS = 4
XA_WIDTH = 256
FFN_HIDDEN = 2816
NUM_BUCKETS = 32
MAX_DISTANCE = 128
EPS = 1e-6
SCALE = HEAD_DIM ** -0.5
NEG = -1e30

EVEN_SIZES = (256, 256, 512, 16, 512, 512, 768, 24)
ODD_SIZES = (512, 512, 512, 8, 1024)

LANES = 128
VMEM_LIMIT_BYTES = 56 * 1024 * 1024
SAMPLE_ROWS = 8
SUB = 16
FAR_GROUP = 4
CMP_CHUNK = 128
N_SELBLK = 128


def _cparams(sem, vmem=None):
    return pltpu.CompilerParams(dimension_semantics=sem, vmem_limit_bytes=vmem)


def _split(h, sizes):
    return jnp.split(h, np.cumsum(sizes)[:-1].tolist(), axis=-1)


def _rms_rows(x, g):
    return x * lax.rsqrt(jnp.mean(x * x, axis=-1, keepdims=True) + EPS) * g


def _group_rms(x, gmat, gs):
    x2 = x * x
    hi = x2.astype(BF16)
    lo = (x2 - hi.astype(F32)).astype(BF16)
    ms = (jnp.dot(hi, gmat, preferred_element_type=F32) + jnp.dot(lo, gmat, preferred_element_type=F32)) * (1.0 / gs)
    return x * lax.rsqrt(ms + EPS)


def _log_sigmoid(z):
    return -(jnp.maximum(-z, 0.0) + jnp.log1p(jnp.exp(-jnp.abs(z))))


def _sigmoid(z):
    return 1.0 / (1.0 + jnp.exp(-z))


def _dot_nt(a, b):
    return lax.dot_general(a, b, (((1,), (1,)), ((), ())), preferred_element_type=F32)


def _dot_tn(a, b):
    return lax.dot_general(a, b, (((0,), (0,)), ((), ())), preferred_element_type=F32)


def _dot_f32(a, b):
    return jnp.dot(a, b, preferred_element_type=F32, precision=lax.Precision.HIGHEST)


def _block_ones(width, gs):
    r = np.arange(width) // gs
    return jnp.asarray((r[:, None] == r[None, :]).astype(np.float32), dtype=BF16)


def _even_in_kernel(x_ref, g_ref, w_ref, wg_ref, bg_ref, qn_ref, kn1_ref, kn2_ref, gm512_ref, gm128_ref,
                    q_ref, k_ref, v_ref, la_ref, og_ref, nqs_ref, kvr_ref, wr_ref, gt_ref, selkv_ref, winkv_ref,
                    nqst_ref, selvt_ref, winvt_ref, gtt_ref):
    xb = _rms_rows(x_ref[...], g_ref[...]).astype(BF16)

    def proj(lo, hi):
        return jnp.dot(xb, w_ref[:, lo:hi], preferred_element_type=F32)

    q_ref[...] = proj(0, 256) * (GLA_DK ** -0.5)
    k_ref[...] = proj(256, 512)
    v_ref[...] = proj(512, 1024)
    og = proj(1024, 1536)
    og_ref[...] = og * _sigmoid(og)
    nq = _group_rms(proj(1536, 2048), gm512_ref[...], HEAD_DIM) * qn_ref[...] * SCALE
    nqs_ref[...] = nq.astype(BF16)
    nqst_ref[...] = nq.T.astype(BF16)
    kvr_ref[:, 0:256] = proj(2048, 2304)
    selk = _group_rms(proj(2304, 2432), gm128_ref[...], HEAD_DIM) * kn1_ref[...]
    selv = proj(2432, 2560)
    kvr_ref[:, 256:384] = selk
    kvr_ref[:, 384:512] = selv
    selkv_ref[:, 0:128] = selk.astype(BF16)
    selkv_ref[:, 128:256] = selv.astype(BF16)
    selvt_ref[...] = selv.T.astype(BF16)
    wink = _group_rms(proj(2560, 2688), gm128_ref[...], HEAD_DIM) * kn2_ref[...]
    winv = proj(2688, 2816)
    wr_ref[:, 0:128] = wink
    wr_ref[:, 128:256] = winv
    winkv_ref[:, 0:128] = wink.astype(BF16)
    winkv_ref[:, 128:256] = winv.astype(BF16)
    winvt_ref[...] = winv.T.astype(BF16)
    glr = proj(2816, 2944).astype(BF16)
    z = jnp.dot(glr, wg_ref[...], preferred_element_type=F32) + bg_ref[...]
    la_ref[...] = _log_sigmoid(z) * (1.0 / GLA_TAU)
    gates = _sigmoid(proj(2944, 3072))
    gt_ref[...] = gates
    gtt_ref[...] = gates.T[0:32, :]


def _even_in(x2d, g, w_pad, wg_pad, bg, qn, kn1, kn2, tm):
    m = x2d.shape[0]
    widths = (256, 256, 512, 256, 512, 512, 512, 256, 128, 256, 256)
    dtypes = (F32, F32, F32, F32, F32, BF16, F32, F32, F32, BF16, BF16)
    t_heights = (512, 128, 128, 32)
    t_dtypes = (BF16, BF16, BF16, F32)
    full = lambda a: pl.BlockSpec(a.shape, lambda i: (0,) * a.ndim)
    gm512 = _block_ones(512, HEAD_DIM)
    gm128 = _block_ones(128, HEAD_DIM)
    ins = (x2d, g, w_pad, wg_pad, bg, qn, kn1, kn2, gm512, gm128)
    return pl.pallas_call(
        _even_in_kernel,
        grid=(m // tm,),
        in_specs=[pl.BlockSpec((tm, D_MODEL), lambda i: (i, 0))] + [full(a) for a in ins[1:]],
        out_specs=[pl.BlockSpec((tm, w), lambda i: (i, 0)) for w in widths]
                  + [pl.BlockSpec((h, tm), lambda i: (0, i)) for h in t_heights],
        out_shape=[jax.ShapeDtypeStruct((m, w), d) for w, d in zip(widths, dtypes)]
                  + [jax.ShapeDtypeStruct((h, m), d) for h, d in zip(t_heights, t_dtypes)],
        compiler_params=_cparams(("arbitrary",), VMEM_LIMIT_BYTES),
        name="even_in",
    )(*ins)


def _odd_in_kernel(x_ref, g_ref, w_ref, qn_ref, kn_ref, bf_ref, gm512_ref, tri_ref,
                   qs_ref, kv_ref, kb_ref, lf_ref, c_ref, u_ref, qst_ref, vbt_ref, carry_ref):
    @pl.when(pl.program_id(1) == 0)
    def _():
        carry_ref[...] = jnp.zeros_like(carry_ref)

    xb = _rms_rows(x_ref[...], g_ref[...]).astype(BF16)

    def proj(lo, hi):
        return jnp.dot(xb, w_ref[:, lo:hi], preferred_element_type=F32)

    q = _group_rms(proj(0, 512), gm512_ref[...], HEAD_DIM) * qn_ref[...] * SCALE
    qs_ref[...] = q.astype(BF16)
    qst_ref[...] = q.T.astype(BF16)
    k = _group_rms(proj(512, 1024), gm512_ref[...], HEAD_DIM) * kn_ref[...]
    v = proj(1024, 1536)
    kv_ref[:, 0:512] = k
    kv_ref[:, 512:1024] = v
    kb_ref[...] = k.astype(BF16)
    vbt_ref[...] = v.T.astype(BF16)
    u_ref[...] = proj(1536, 2048) * _sigmoid(proj(2048, 2560))
    lf = _log_sigmoid(proj(2560, 2688) + bf_ref[...])
    lf_ref[...] = lf
    c = _dot_f32(tri_ref[...], lf) + carry_ref[0:1, :]
    c_ref[...] = c
    carry_ref[0:1, :] = c[-1:, :]


def _odd_in(x3d, g, w_pad, qn, kn, bf_pad, tm):
    b, t, _ = x3d.shape
    widths = (512, 1024, 512, 128, 128, 512)
    dtypes = (BF16, F32, BF16, F32, F32, F32)
    gm512 = _block_ones(512, HEAD_DIM)
    tri = jnp.asarray(np.tril(np.ones((tm, tm), np.float32)))
    ins = (x3d, g, w_pad, qn, kn, bf_pad, gm512, tri)
    full = lambda a: pl.BlockSpec(a.shape, lambda bi, i: (0,) * a.ndim)
    nt = t // tm
    t_spec = pl.BlockSpec((FOX_WIDTH, tm), lambda bi, i: (0, bi * nt + i))
    t_shape = jax.ShapeDtypeStruct((FOX_WIDTH, b * t), BF16)
    return pl.pallas_call(
        _odd_in_kernel,
        grid=(b, nt),
        in_specs=[pl.BlockSpec((None, tm, D_MODEL), lambda bi, i: (bi, i, 0))] + [full(a) for a in ins[1:]],
        out_specs=[pl.BlockSpec((None, tm, w), lambda bi, i: (bi, i, 0)) for w in widths] + [t_spec, t_spec],
        out_shape=[jax.ShapeDtypeStruct((b, t, w), d) for w, d in zip(widths, dtypes)] + [t_shape, t_shape],
        scratch_shapes=[pltpu.VMEM((8, 128), F32)],
        compiler_params=_cparams(("arbitrary", "arbitrary"), VMEM_LIMIT_BYTES),
        name="odd_in",
    )(*ins)


def _out_proj_kernel(res_ref, a1_ref, a2_ref, w1_ref, w2_ref, o_ref):
    acc = jnp.dot(a1_ref[...].astype(BF16), w1_ref[...], preferred_element_type=F32)
    acc = acc + jnp.dot(a2_ref[...].astype(BF16), w2_ref[...], preferred_element_type=F32)
    o_ref[...] = res_ref[...] + acc


def _out_proj(res, a1, a2, w1, w2, tm):
    m = res.shape[0]
    row = lambda a: pl.BlockSpec((tm, a.shape[1]), lambda i: (i, 0))
    full = lambda a: pl.BlockSpec(a.shape, lambda i: (0, 0))
    return pl.pallas_call(
        _out_proj_kernel,
        grid=(m // tm,),
        in_specs=[row(res), row(a1), row(a2), full(w1), full(w2)],
        out_specs=row(res),
        out_shape=jax.ShapeDtypeStruct(res.shape, F32),
        compiler_params=_cparams(("arbitrary",), VMEM_LIMIT_BYTES),
        name="out_proj",
    )(res, a1, a2, w1, w2)


def _ffn_kernel(x_ref, g_ref, wg_ref, wu_ref, wo_ref, o_ref, xn_ref, acc_ref):
    j = pl.program_id(1)

    @pl.when(j == 0)
    def _():
        xn_ref[...] = _rms_rows(x_ref[...], g_ref[...]).astype(BF16)
        acc_ref[...] = jnp.zeros_like(acc_ref)

    xb = xn_ref[...]
    gate = jnp.dot(xb, wg_ref[...], preferred_element_type=F32)
    up = jnp.dot(xb, wu_ref[...], preferred_element_type=F32)
    h = (gate * _sigmoid(gate) * up).astype(BF16)
    acc_ref[...] += jnp.dot(h, wo_ref[...], preferred_element_type=F32)

    @pl.when(j == pl.num_programs(1) - 1)
    def _():
        o_ref[...] = x_ref[...] + acc_ref[...]


def _ffn(x2d, g, w_in, w_out, tm, n_chunks=2):
    m = x2d.shape[0]
    th = FFN_HIDDEN // n_chunks
    return pl.pallas_call(
        _ffn_kernel,
        grid=(m // tm, n_chunks),
        in_specs=[pl.BlockSpec((tm, D_MODEL), lambda i, j: (i, 0)),
                  pl.BlockSpec((1, D_MODEL), lambda i, j: (0, 0)),
                  pl.BlockSpec((D_MODEL, th), lambda i, j: (0, j)),
                  pl.BlockSpec((D_MODEL, th), lambda i, j: (0, n_chunks + j)),
                  pl.BlockSpec((th, D_MODEL), lambda i, j: (j, 0))],
        out_specs=pl.BlockSpec((tm, D_MODEL), lambda i, j: (i, 0)),
        out_shape=jax.ShapeDtypeStruct(x2d.shape, F32),
        scratch_shapes=[pltpu.VMEM((tm, D_MODEL), BF16), pltpu.VMEM((tm, D_MODEL), F32)],
        compiler_params=_cparams(("arbitrary", "arbitrary"), VMEM_LIMIT_BYTES),
        name="ffn",
    )(x2d, g, w_in, w_in, w_out)


def _mem_kv_kernel(m_ref, g_ref, w_ref, kn_ref, gm_ref, o_ref):
    xb = _rms_rows(m_ref[...], g_ref[...]).astype(BF16)
    kv = jnp.dot(xb, w_ref[...], preferred_element_type=F32)
    o_ref[0:XA_WIDTH, :] = (_group_rms(kv[:, 0:XA_WIDTH], gm_ref[...], HEAD_DIM) * kn_ref[...]).T
    o_ref[XA_WIDTH:, :] = kv[:, XA_WIDTH:].T


def _mem_kv(mem, g, wkv, kn):
    b = mem.shape[0]
    gm = _block_ones(XA_WIDTH, HEAD_DIM)
    full = lambda a: pl.BlockSpec(a.shape, lambda i: (0,) * a.ndim)
    return pl.pallas_call(
        _mem_kv_kernel,
        grid=(b,),
        in_specs=[pl.BlockSpec((None, MEM_LEN, D_MODEL), lambda i: (i, 0, 0)), full(g), full(wkv), full(kn), full(gm)],
        out_specs=pl.BlockSpec((None, 2 * XA_WIDTH, MEM_LEN), lambda i: (i, 0, 0)),
        out_shape=jax.ShapeDtypeStruct((b, 2 * XA_WIDTH, MEM_LEN), F32),
        compiler_params=_cparams(("arbitrary",)),
        name="mem_kv",
    )(mem, g, wkv, kn, gm)


def _xattn_kernel(x_ref, mkv_ref, g_ref, wq_ref, wo_ref, qn_ref, gm_ref, o_ref):
    x = x_ref[...]
    xb = _rms_rows(x, g_ref[...]).astype(BF16)
    q = jnp.dot(xb, wq_ref[...], preferred_element_type=F32)
    q = _group_rms(q, gm_ref[...], HEAD_DIM) * qn_ref[...]
    qb = (q * SCALE).astype(BF16)
    outs = []
    for h in range(XA_HEADS):
        kt = mkv_ref[h * HEAD_DIM:(h + 1) * HEAD_DIM, :].astype(BF16)
        vt = mkv_ref[XA_WIDTH + h * HEAD_DIM:XA_WIDTH + (h + 1) * HEAD_DIM, :].astype(BF16)
        s = jnp.dot(qb[:, h * HEAD_DIM:(h + 1) * HEAD_DIM], kt, preferred_element_type=F32)
        e = jnp.exp(s - jnp.max(s, axis=-1, keepdims=True))
        p = (e / jnp.sum(e, axis=-1, keepdims=True)).astype(BF16)
        outs.append(_dot_nt(p, vt))
    o = jnp.concatenate(outs, axis=-1).astype(BF16)
    o_ref[...] = x + jnp.dot(o, wo_ref[...], preferred_element_type=F32)


def _xattn(x3d, mkv_t, layer, g, wq, wo, qn, tm):
    b, t, _ = x3d.shape
    gm = _block_ones(XA_WIDTH, HEAD_DIM)
    full = lambda a: pl.BlockSpec(a.shape, lambda bi, i: (0,) * a.ndim)
    return pl.pallas_call(
        _xattn_kernel,
        grid=(b, t // tm),
        in_specs=[pl.BlockSpec((None, tm, D_MODEL), lambda bi, i: (bi, i, 0)),
                  pl.BlockSpec((None, None, 2 * XA_WIDTH, MEM_LEN), lambda bi, i: (layer, bi, 0, 0)),
                  full(g), full(wq), full(wo), full(qn), full(gm)],
        out_specs=pl.BlockSpec((None, tm, D_MODEL), lambda bi, i: (bi, i, 0)),
        out_shape=jax.ShapeDtypeStruct(x3d.shape, F32),
        compiler_params=_cparams(("arbitrary", "arbitrary"), VMEM_LIMIT_BYTES),
        name="xattn",
    )(x3d, mkv_t, g, wq, wo, qn, gm)


def _gla_kernel(q_ref, k_ref, v_ref, la_ref, og_ref, gn_ref, s0_ref, tri_ref, hsel_ref,
                o_ref, sfin_ref, st_ref, *, n_sub, n_valid, bb):
    ti = pl.program_id(1)

    @pl.when(ti == 0)
    def _():
        st_ref[...] = s0_ref[...]

    tri = tri_ref[...]
    hsel = hsel_ref[...]
    gn = gn_ref[...]
    row = lax.broadcasted_iota(jnp.int32, (SUB, 1), 0)

    def sub_block(i, carry):
        for bi in range(bb):
            one_sequence(i, bi)
        return carry

    def one_sequence(i, bi):
        r0 = pl.multiple_of(i * SUB, SUB)
        rows = pl.ds(r0, SUB)
        q = q_ref[bi, rows, :]
        k = k_ref[bi, rows, :]
        v = v_ref[bi, rows, :]
        la = la_ref[bi, rows, :]
        if n_valid is not None:
            live = (row + r0) < n_valid
            la = jnp.where(live, la, 0.0)
            k = jnp.where(live, k, 0.0)
        b = _dot_f32(tri, la)
        b_end = b[SUB - 1:SUB, :]
        qd = (q * jnp.exp(b)).astype(BF16)
        kd = (k * jnp.exp(b_end - b)).astype(BF16)
        vb = v.astype(BF16)
        tiles = []
        for s in range(SUB):
            e = jnp.exp(jnp.minimum(b - b[s:s + 1, :], 0.0))
            z = (q * k[s:s + 1, :]) * e
            tiles.append(jnp.where(row >= s, z, 0.0))
        att = jnp.dot(jnp.concatenate(tiles, axis=0).astype(BF16), hsel, preferred_element_type=F32)
        dec = jnp.exp(b_end)
        outs = []
        for h in range(GLA_HEADS):
            dk = slice(h * GLA_DK, (h + 1) * GLA_DK)
            dv = slice(h * GLA_DV, (h + 1) * GLA_DV)
            st = st_ref[bi, h]
            o = _dot_nt(qd[:, dk], st.astype(BF16))
            for s in range(SUB):
                o = o + att[s * SUB:(s + 1) * SUB, h:h + 1] * v[s:s + 1, dv]
            st_ref[bi, h] = st * dec[:, dk] + _dot_tn(vb[:, dv], kd[:, dk])
            outs.append(_rms_rows(o, gn))
        o_ref[bi, rows, :] = jnp.concatenate(outs, axis=-1) * og_ref[bi, rows, :]

    lax.fori_loop(0, n_sub, sub_block, 0)

    @pl.when(ti == pl.num_programs(1) - 1)
    def _():
        sfin_ref[...] = st_ref[...]


def _gla(q, k, v, la, og, gn, s0t, tt, bb, n_valid=None):
    b, t, _ = q.shape
    tri = jnp.asarray(np.tril(np.ones((SUB, SUB), np.float32)))
    hsel = jnp.asarray((np.arange(256)[:, None] // GLA_DK == np.arange(128)[None, :]).astype(np.float32), dtype=BF16)
    seq = lambda w: pl.BlockSpec((bb, tt, w), lambda bi, i: (bi, i, 0))
    full = lambda a: pl.BlockSpec(a.shape, lambda bi, i: (0,) * a.ndim)
    st_spec = pl.BlockSpec((bb, GLA_HEADS, GLA_DV, GLA_DK), lambda bi, i: (bi, 0, 0, 0))
    return pl.pallas_call(
        functools.partial(_gla_kernel, n_sub=tt // SUB, n_valid=n_valid, bb=bb),
        grid=(b // bb, t // tt),
        in_specs=[seq(256), seq(256), seq(512), seq(256), seq(512), full(gn), st_spec, full(tri), full(hsel)],
        out_specs=[seq(512), st_spec],
        out_shape=[jax.ShapeDtypeStruct((b, t, GLA_WIDTH), F32),
                   jax.ShapeDtypeStruct((b, GLA_HEADS, GLA_DV, GLA_DK), F32)],
        scratch_shapes=[pltpu.VMEM((bb, GLA_HEADS, GLA_DV, GLA_DK), F32)],
        compiler_params=_cparams(("arbitrary", "arbitrary")),
        name="gla",
    )(q, k, v, la, og, gn, s0t, tri, hsel)


def _conv_kernel(u_ref, st0_ref, w_ref, b_ref, g_ref, beta_ref, o_ref, st_ref, ext_ref, *, tt, n_valid):
    ti = pl.program_id(1)
    ctx = CONV_WIDTH - 1

    @pl.when(ti == 0)
    def _():
        ext_ref[0:8, :] = jnp.zeros((8, CONV_CH), F32)
        ext_ref[pl.ds(2, ctx), :] = st0_ref[...]

    ext_ref[pl.ds(32, tt), :] = u_ref[...]
    acc = jnp.zeros((tt, CONV_CH), F32)
    for w in range(CONV_WIDTH):
        acc = acc + ext_ref[pl.ds(2 + w, tt), :] * w_ref[w:w + 1, :]
    y = acc + b_ref[...]
    mu = jnp.mean(y, axis=-1, keepdims=True)
    var = jnp.mean(jnp.square(y - mu), axis=-1, keepdims=True)
    ln = (y - mu) * lax.rsqrt(var + EPS) * g_ref[...] + beta_ref[...]
    o_ref[...] = ln * _sigmoid(ln)

    @pl.when(ti == pl.num_programs(1) - 1)
    def _():
        st_ref[...] = ext_ref[pl.ds(32 + n_valid - ctx, ctx), :]

    ext_ref[0:32, :] = ext_ref[pl.ds(tt, 32), :]


def _conv(u, st0, w, b, g, beta, tt, n_valid):
    bsz, t, _ = u.shape
    ctx = CONV_WIDTH - 1
    full = lambda a: pl.BlockSpec(a.shape, lambda bi, i: (0,) * a.ndim)
    st_spec = pl.BlockSpec((None, ctx, CONV_CH), lambda bi, i: (bi, 0, 0))
    return pl.pallas_call(
        functools.partial(_conv_kernel, tt=tt, n_valid=n_valid),
        grid=(bsz, t // tt),
        in_specs=[pl.BlockSpec((None, tt, CONV_CH), lambda bi, i: (bi, i, 0)), st_spec,
                  full(w), full(b), full(g), full(beta)],
        out_specs=[pl.BlockSpec((None, tt, CONV_CH), lambda bi, i: (bi, i, 0)), st_spec],
        out_shape=[jax.ShapeDtypeStruct(u.shape, F32), jax.ShapeDtypeStruct((bsz, ctx, CONV_CH), F32)],
        scratch_shapes=[pltpu.VMEM((32 + max(tt, 32), CONV_CH), F32)],
        compiler_params=_cparams(("arbitrary", "arbitrary")),
        name="conv",
    )(u, st0, w, b, g, beta)


def _fox_kernel(qt_ref, k_ref, vt_ref, ck_ref, cqt_ref, o_ref, m_ref, l_ref, acc_ref, *, tq, tk):
    qi = pl.program_id(1)
    ki = pl.program_id(2)
    last = (qi * tq + tq - 1) // tk

    @pl.when(ki == 0)
    def _():
        m_ref[...] = jnp.full_like(m_ref, NEG)
        l_ref[...] = jnp.zeros_like(l_ref)
        acc_ref[...] = jnp.zeros_like(acc_ref)

    def tile(neg):
        for h in range(FOX_HEADS):
            cols = slice(h * HEAD_DIM, (h + 1) * HEAD_DIM)
            s = jnp.dot(k_ref[:, cols], qt_ref[cols, :], preferred_element_type=F32)
            s = s + cqt_ref[h:h + 1, :] - ck_ref[:, h:h + 1]
            if neg is not None:
                s = s + neg
            m_old = m_ref[h:h + 1, :]
            m_new = jnp.maximum(m_old, jnp.max(s, axis=0, keepdims=True))
            p = jnp.exp(s - m_new)
            alpha = jnp.exp(m_old - m_new)
            l_ref[h:h + 1, :] = alpha * l_ref[h:h + 1, :] + jnp.sum(p, axis=0, keepdims=True)
            acc_ref[cols, :] = alpha * acc_ref[cols, :] + jnp.dot(vt_ref[cols, :], p.astype(BF16),
                                                                  preferred_element_type=F32)
            m_ref[h:h + 1, :] = m_new

    @pl.when(ki < last)
    def _():
        tile(None)

    @pl.when(ki == last)
    def _():
        s_pos = ki * tk + lax.broadcasted_iota(jnp.int32, (tk, tq), 0)
        t_pos = qi * tq + lax.broadcasted_iota(jnp.int32, (tk, tq), 1)
        tile(jnp.where(s_pos <= t_pos, 0.0, NEG))

    @pl.when(ki == pl.num_programs(2) - 1)
    def _():
        for h in range(FOX_HEADS):
            cols = slice(h * HEAD_DIM, (h + 1) * HEAD_DIM)
            acc_ref[cols, :] = acc_ref[cols, :] / jnp.maximum(l_ref[h:h + 1, :], 1e-30)
        o_ref[...] = acc_ref[...].T


def _fox_prompt(qst, kb, vbt, c8, ct, tq, tk):
    b, t, _ = kb.shape
    nq, nk = t // tq, t // tk
    kmin = lambda qi, ki: jnp.minimum(ki, (qi * tq + tq - 1) // tk)
    return pl.pallas_call(
        functools.partial(_fox_kernel, tq=tq, tk=tk),
        grid=(b, nq, nk),
        in_specs=[pl.BlockSpec((FOX_WIDTH, tq), lambda bi, qi, ki: (0, bi * nq + qi)),
                  pl.BlockSpec((None, tk, FOX_WIDTH), lambda bi, qi, ki: (bi, kmin(qi, ki), 0)),
                  pl.BlockSpec((FOX_WIDTH, tk), lambda bi, qi, ki: (0, bi * nk + kmin(qi, ki))),
                  pl.BlockSpec((None, tk, FOX_HEADS), lambda bi, qi, ki: (bi, kmin(qi, ki), 0)),
                  pl.BlockSpec((None, FOX_HEADS, tq), lambda bi, qi, ki: (bi, 0, qi))],
        out_specs=pl.BlockSpec((None, tq, FOX_WIDTH), lambda bi, qi, ki: (bi, qi, 0)),
        out_shape=jax.ShapeDtypeStruct((b, t, FOX_WIDTH), F32),
        scratch_shapes=[pltpu.VMEM((FOX_HEADS, tq), F32), pltpu.VMEM((FOX_HEADS, tq), F32),
                        pltpu.VMEM((FOX_WIDTH, tq), F32)],
        compiler_params=_cparams(("arbitrary", "arbitrary", "arbitrary"), VMEM_LIMIT_BYTES),
        name="fox_prompt",
    )(qst, kb, vbt, c8, ct)


def _compress_weights(w1):
    w = w1.reshape(2, 2, 8, 2, HEAD_DIM, CMP_HIDDEN)
    z = jnp.zeros_like(w)
    g0 = jnp.concatenate([w, z], axis=-1)
    g1 = jnp.concatenate([z, w], axis=-1)
    wbd = jnp.stack([g0, g1], axis=4)
    return wbd.reshape(2, 2, 8, 4 * HEAD_DIM, 2 * CMP_HIDDEN).astype(BF16)


def _compress_compute(src_refs, pe_ref, w1_ref, w2_ref, kn_ref, ck_ref, cv_ref, sh_ref, nseg):
    sh_ref[pl.ds(nseg, 8), :] = jnp.zeros((8, CMP_HIDDEN), F32)
    for j, src_ref in enumerate(src_refs):
        a = jnp.zeros((nseg, 2 * CMP_HIDDEN), F32)
        bm = jnp.zeros((nseg, 2 * CMP_HIDDEN), F32)
        for q in range(CMP_STRIDE // 2):
            p0, p1 = 2 * q, 2 * q + 1
            x0 = src_ref[pl.ds(p0, nseg, stride=CMP_STRIDE), :]
            x1 = src_ref[pl.ds(p1, nseg, stride=CMP_STRIDE), :]
            xa = jnp.concatenate([x0 + pe_ref[j, p0:p0 + 1, :], x1 + pe_ref[j, p1:p1 + 1, :]], axis=1)
            a = a + jnp.dot(xa.astype(BF16), w1_ref[j, 0, q], preferred_element_type=F32)
            p0, p1 = p0 + CMP_STRIDE, p1 + CMP_STRIDE
            xb = jnp.concatenate([x0 + pe_ref[j, p0:p0 + 1, :], x1 + pe_ref[j, p1:p1 + 1, :]], axis=1)
            bm = bm + jnp.dot(xb.astype(BF16), w1_ref[j, 1, q], preferred_element_type=F32)
        for g in range(NSA_KV_HEADS):
            sh_ref[pl.ds(0, nseg), :] = bm[:, g * CMP_HIDDEN:(g + 1) * CMP_HIDDEN]
            x = a[:, g * CMP_HIDDEN:(g + 1) * CMP_HIDDEN] + sh_ref[pl.ds(1, nseg), :]
            hid = x * (0.5 * (1.0 + jnp.tanh(math.sqrt(2.0 / math.pi) * (x + 0.044715 * (x * x * x)))))
            ckv = jnp.dot(hid.astype(BF16), w2_ref[j], preferred_element_type=F32)
            if j == 0:
                ck_ref[g] = _rms_rows(ckv, kn_ref[...]).astype(BF16)
            else:
                cv_ref[g] = ckv.astype(BF16)


def _compress_prompt_kernel(xk_ref, xv_ref, pe_ref, w1_ref, w2_ref, kn_ref, ck_ref, cv_ref, sh_ref, *, nseg):
    _compress_compute((xk_ref, xv_ref), pe_ref, w1_ref, w2_ref, kn_ref, ck_ref, cv_ref, sh_ref, nseg)


def _compress_prompt(kvr, pe, w1, w2, kn):
    b, t, _ = kvr.shape
    nseg = t // CMP_STRIDE
    full = lambda a: pl.BlockSpec(a.shape, lambda i: (0,) * a.ndim)
    o_spec = pl.BlockSpec((None, NSA_KV_HEADS, nseg, HEAD_DIM), lambda i: (i, 0, 0, 0))
    o_shape = jax.ShapeDtypeStruct((b, NSA_KV_HEADS, nseg, HEAD_DIM), BF16)
    return pl.pallas_call(
        functools.partial(_compress_prompt_kernel, nseg=nseg),
        grid=(b,),
        in_specs=[pl.BlockSpec((None, t, LANES), lambda i: (i, 0, 0)), pl.BlockSpec((None, t, LANES), lambda i: (i, 0, 1)),
                  full(pe), full(w1), full(w2), full(kn)],
        out_specs=[o_spec, o_spec],
        out_shape=[o_shape, o_shape],
        scratch_shapes=[pltpu.VMEM((nseg + 8, CMP_HIDDEN), F32)],
        compiler_params=_cparams(("arbitrary",), VMEM_LIMIT_BYTES),
        name="nsa_compress",
    )(kvr, kvr, pe, w1, w2, kn)


def _stack_heads(qs_ref, kh):
    parts = [qs_ref[:, (kh * NSA_GROUP + g) * HEAD_DIM:(kh * NSA_GROUP + g + 1) * HEAD_DIM].astype(F32)
             for g in range(NSA_GROUP)]
    return jnp.concatenate(parts, axis=0).astype(BF16)


def _cmp_topk_kernel(qs_ref, ck_ref, cv_ref, farcol_ref, chi_ref, clo_ref, pool_ref, ocmp_ref, msk_ref,
                     *, tq, nseg, q_base, n_pick, n_blk):
    qi = pl.program_id(1)
    G = NSA_GROUP
    q0 = q_base + qi * tq
    nbase = q0 // CMP_STRIDE - 16
    place = (lax.broadcasted_iota(jnp.int32, (32, nseg), 1) - lax.broadcasted_iota(jnp.int32, (32, nseg), 0)) == nbase
    place = jnp.where(place, 1.0, 0.0).astype(BF16)
    t1 = q0 + lax.broadcasted_iota(jnp.int32, (tq, 1), 0)
    t4 = jnp.concatenate([t1] * G, axis=0)
    n_i = lax.broadcasted_iota(jnp.int32, (G * tq, nseg), 1)
    valid = (n_i * CMP_STRIDE + (CMP_BLOCK - 1) <= t4) & (n_i <= nseg - 2)
    blk = lax.broadcasted_iota(jnp.int32, (tq, N_SELBLK), 1)
    cur = lax.shift_right_logical(t1, 6)
    forced = (blk == 0) | (blk == cur) | (blk == cur - 1)
    for kh in range(NSA_KV_HEADS):
        q4 = _stack_heads(qs_ref, kh)
        s = _dot_nt(q4, ck_ref[kh]) + farcol_ref[kh]
        s = s + jnp.dot(chi_ref[kh], place, preferred_element_type=F32) + jnp.dot(clo_ref[kh], place, preferred_element_type=F32)
        s = jnp.where(valid, s, NEG)
        e = jnp.where(valid, jnp.exp(s - jnp.max(s, axis=-1, keepdims=True)), 0.0)
        p = e / jnp.maximum(jnp.sum(e, axis=-1, keepdims=True), 1e-30)
        o = jnp.dot(p.astype(BF16), cv_ref[kh], preferred_element_type=F32)
        for g in range(G):
            h = kh * G + g
            ocmp_ref[:, h * HEAD_DIM:(h + 1) * HEAD_DIM] = o[g * tq:(g + 1) * tq]
        imp = p[0:tq] + p[tq:2 * tq] + p[2 * tq:3 * tq] + p[3 * tq:4 * tq]
        pooled = _dot_f32(imp, pool_ref[...])
        score = jnp.where((blk > cur) | (blk >= n_blk), -1e30, jnp.where(forced, 1e30, pooled))
        sel = jnp.zeros((tq, N_SELBLK), F32)
        for _ in range(n_pick):
            mx = jnp.max(score, axis=-1, keepdims=True)
            first = jnp.min(jnp.where(score == mx, blk, N_SELBLK), axis=-1, keepdims=True)
            pick = blk == first
            sel = jnp.where(pick, 1.0, sel)
            score = jnp.where(pick, -3e38, score)
        msk_ref[kh] = sel


def _cmp_topk_t_kernel(qt_ref, ck_ref, cv_ref, farrow_ref, chit_ref, clot_ref, poolt_ref, ocmpt_ref, mskt_ref,
                       *, tq, nseg, n_pick, n_blk):
    qi = pl.program_id(1)
    G = NSA_GROUP
    q0 = qi * tq
    nbase = q0 // CMP_STRIDE - 16
    t1 = q0 + lax.broadcasted_iota(jnp.int32, (1, tq), 1)
    blk = lax.broadcasted_iota(jnp.int32, (N_SELBLK, tq), 0)
    cur = lax.shift_right_logical(t1, 6)
    forced = (blk == 0) | (blk == cur) | (blk == cur - 1)

    def body(nv):
        place_t = (lax.broadcasted_iota(jnp.int32, (nv, 32), 0) - lax.broadcasted_iota(jnp.int32, (nv, 32), 1)) == nbase
        place_t = jnp.where(place_t, 1.0, 0.0).astype(BF16)
        n_i = lax.broadcasted_iota(jnp.int32, (nv, tq), 0)
        valid1 = (n_i * CMP_STRIDE + (CMP_BLOCK - 1) <= t1) & (n_i <= nseg - 2)
        valid = jnp.concatenate([valid1] * G, axis=1)
        for kh in range(NSA_KV_HEADS):
            qt4 = jnp.concatenate([qt_ref[(kh * G + g) * HEAD_DIM:(kh * G + g + 1) * HEAD_DIM, :] for g in range(G)],
                                  axis=1)
            s = jnp.dot(ck_ref[kh, 0:nv, :], qt4, preferred_element_type=F32) + farrow_ref[kh]
            s = s + jnp.dot(place_t, chit_ref[kh], preferred_element_type=F32) \
                  + jnp.dot(place_t, clot_ref[kh], preferred_element_type=F32)
            s = jnp.where(valid, s, NEG)
            e = jnp.where(valid, jnp.exp(s - jnp.max(s, axis=0, keepdims=True)), 0.0)
            p = e * (1.0 / jnp.maximum(jnp.sum(e, axis=0, keepdims=True), 1e-30))
            o_t = _dot_tn(cv_ref[kh, 0:nv, :], p.astype(BF16))
            for g in range(G):
                h = kh * G + g
                ocmpt_ref[h * HEAD_DIM:(h + 1) * HEAD_DIM, :] = o_t[:, g * tq:(g + 1) * tq]
            imp = p[:, 0:tq] + p[:, tq:2 * tq] + p[:, 2 * tq:3 * tq] + p[:, 3 * tq:4 * tq]
            pooled = _dot_f32(poolt_ref[:, 0:nv], imp)
            score = jnp.where((blk > cur) | (blk >= n_blk), -1e30, jnp.where(forced, 1e30, pooled))
            sel = jnp.zeros((N_SELBLK, tq), F32)
            for _ in range(n_pick):
                mx = jnp.max(score, axis=0, keepdims=True)
                first = jnp.min(jnp.where(score == mx, blk, N_SELBLK), axis=0, keepdims=True)
                pick = blk == first
                sel = jnp.where(pick, 1.0, sel)
                score = jnp.where(pick, -3e38, score)
            mskt_ref[kh] = sel

    n_chunks = nseg // CMP_CHUNK
    need = jnp.minimum(jnp.maximum(q0 + tq - CMP_BLOCK, 0) // CMP_STRIDE // CMP_CHUNK + 1, n_chunks)
    for v in range(1, n_chunks + 1):
        pl.when(need == v)(functools.partial(body, v * CMP_CHUNK))


def _cmp_topk_prompt(nqst, ck, cv, tbl, b, t, tq, n_pick):
    nseg = ck.shape[2]
    assert tq <= 256 and ck.shape[0] == b and nseg % CMP_CHUNK == 0
    farcol, chi, clo = _cmp_bias_tables(tbl, tq)
    farrow, chit, clot = (jnp.swapaxes(a, 1, 2) for a in (farcol, chi, clo))
    poolt = jnp.asarray((np.arange(N_SELBLK)[:, None] == np.arange(nseg)[None, :] // SEL_RATIO).astype(np.float32))
    full = lambda a: pl.BlockSpec(a.shape, lambda bi, i: (0,) * a.ndim)
    c_spec = pl.BlockSpec((None, NSA_KV_HEADS, nseg, HEAD_DIM), lambda bi, i: (bi, 0, 0, 0))
    nq = t // tq
    col_tile = pl.BlockSpec((NSA_WIDTH, tq), lambda bi, i: (0, bi * nq + i))
    return pl.pallas_call(
        functools.partial(_cmp_topk_t_kernel, tq=tq, nseg=nseg, n_pick=n_pick, n_blk=t // SEL_BLOCK),
        grid=(b, nq),
        in_specs=[col_tile, c_spec, c_spec, full(farrow), full(chit), full(clot), full(poolt)],
        out_specs=[col_tile, pl.BlockSpec((None, NSA_KV_HEADS, N_SELBLK, tq), lambda bi, i: (bi, 0, 0, i))],
        out_shape=[jax.ShapeDtypeStruct((NSA_WIDTH, b * t), F32),
                   jax.ShapeDtypeStruct((b, NSA_KV_HEADS, N_SELBLK, t), F32)],
        compiler_params=_cparams(("arbitrary", "arbitrary"), VMEM_LIMIT_BYTES),
        name="nsa_cmp_topk",
    )(nqst, ck, cv, farrow, chit, clot, poolt)


def _rel_bucket(dist):
    exact = NUM_BUCKETS // 2
    d = jnp.maximum(dist, 0)
    log_ratio = jnp.log(jnp.maximum(d, 1).astype(jnp.float32) / exact) / math.log(MAX_DISTANCE / exact)
    large = jnp.minimum(exact + (log_ratio * (NUM_BUCKETS - exact)).astype(jnp.int32), NUM_BUCKETS - 1)
    return jnp.where(d < exact, d, large)


def _bias_lookup(tbl, dist):
    onehot = _rel_bucket(dist)[..., None] == jnp.arange(NUM_BUCKETS)
    t = tbl.reshape((tbl.shape[0],) + (1,) * dist.ndim + (NUM_BUCKETS,))
    return jnp.sum(jnp.where(onehot[None], t, 0.0), axis=-1)


def _cmp_bias_tables(tbl, tq):
    tr = jnp.arange(tq)[:, None]
    i = jnp.arange(32)[None, :]
    dist = tr + 16 * CMP_STRIDE - CMP_STRIDE * i - (CMP_BLOCK - 1)
    near = _bias_lookup(tbl, dist)
    far = tbl[:, NUM_BUCKETS - 1]
    corr = (near - far[:, None, None]).reshape(NSA_KV_HEADS, NSA_GROUP * tq, 32)
    hi = corr.astype(BF16)
    lo = (corr - hi.astype(F32)).astype(BF16)
    farcol = jnp.broadcast_to(far[:, None, None], (NSA_HEADS, tq, 1)).reshape(NSA_KV_HEADS, NSA_GROUP * tq, 1)
    return farcol, hi, lo


def _cmp_topk_sample(qs, ck, cv, tbl, tq, q_base, n_pick, n_blk):
    b, t, _ = qs.shape
    nseg = ck.shape[2]
    assert tq <= 256
    farcol, chi, clo = _cmp_bias_tables(tbl, tq)
    pool = jnp.asarray((np.arange(nseg)[:, None] // SEL_RATIO == np.arange(N_SELBLK)[None, :]).astype(np.float32))
    full = lambda a: pl.BlockSpec(a.shape, lambda bi, i: (0,) * a.ndim)
    c_spec = pl.BlockSpec((None, NSA_KV_HEADS, nseg, HEAD_DIM), lambda bi, i: (bi, 0, 0, 0))
    m_spec = pl.BlockSpec((None, NSA_KV_HEADS, tq, N_SELBLK), lambda bi, i: (bi, 0, i, 0))
    m_shape = (b, NSA_KV_HEADS, t, N_SELBLK)
    return pl.pallas_call(
        functools.partial(_cmp_topk_kernel, tq=tq, nseg=nseg, q_base=q_base, n_pick=n_pick, n_blk=n_blk),
        grid=(b, t // tq),
        in_specs=[pl.BlockSpec((None, tq, NSA_WIDTH), lambda bi, i: (bi, i, 0)), c_spec, c_spec,
                  full(farcol), full(chi), full(clo), full(pool)],
        out_specs=[pl.BlockSpec((None, tq, NSA_WIDTH), lambda bi, i: (bi, i, 0)), m_spec],
        out_shape=[jax.ShapeDtypeStruct((b, t, NSA_WIDTH), F32), jax.ShapeDtypeStruct(m_shape, F32)],
        compiler_params=_cparams(("arbitrary", "arbitrary"), VMEM_LIMIT_BYTES),
        name="nsa_cmp_topk",
    )(qs, ck, cv, farcol, chi, clo, pool)


def _online_update(s, valid, v, m_ref, l_ref, acc_ref, v_t=False):
    s = jnp.where(valid, s, NEG)
    m_old = m_ref[...]
    m_new = jnp.maximum(m_old, jnp.max(s, axis=-1, keepdims=True))
    p = jnp.where(valid, jnp.exp(s - m_new), 0.0)
    alpha = jnp.exp(m_old - m_new)
    l_ref[...] = alpha * l_ref[...] + jnp.sum(p, axis=-1, keepdims=True)
    pb = p.astype(BF16)
    pv = _dot_nt(pb, v) if v_t else jnp.dot(pb, v, preferred_element_type=F32)
    acc_ref[...] = alpha * acc_ref[...] + pv
    m_ref[...] = m_new


def _selwin_kernel(qt_ref, selk_ref, selvt_ref, wink_ref, winvt_ref, mskt_ref, bias_ref, ocmpt_ref, gtt_ref, o_ref,
                   m_ref, l_ref, acc_ref, ot_ref, qt4_ref, *, tq):
    qi = pl.program_id(1)
    G = NSA_GROUP
    blocks_per_tile = tq // SEL_BLOCK
    s_rel = lax.broadcasted_iota(jnp.int32, (tq, tq), 0)
    t_rel = lax.broadcasted_iota(jnp.int32, (tq, tq), 1)
    causal = s_rel <= t_rel
    all_rows, near_rows, diag_rows = slice(0, 3 * tq), slice(tq, 3 * tq), slice(2 * tq, 3 * tq)

    def init():
        m_ref[...] = jnp.full_like(m_ref, NEG)
        l_ref[...] = jnp.zeros_like(l_ref)
        acc_ref[...] = jnp.zeros_like(acc_ref)

    def update(kh, k, vt, bias_rows, valid):
        s = jnp.dot(k, qt4_ref[...], preferred_element_type=F32)
        if bias_rows is not None:
            s = s + bias_ref[kh, bias_rows, :]
        if valid is not None:
            neg = jnp.where(valid, 0.0, NEG)
            s = s + jnp.concatenate([neg] * G, axis=1)
        m_old = m_ref[0:1, :]
        m_new = jnp.maximum(m_old, jnp.max(s, axis=0, keepdims=True))
        p = jnp.exp(s - m_new)
        alpha = jnp.exp(m_old - m_new)
        l_ref[0:1, :] = alpha * l_ref[0:1, :] + jnp.sum(p, axis=0, keepdims=True)
        acc_ref[...] = alpha * acc_ref[...] + jnp.dot(vt, p.astype(BF16), preferred_element_type=F32)
        m_ref[0:1, :] = m_new

    def finish(kh, gate_row):
        for g in range(G):
            h = kh * G + g
            cols = slice(g * tq, (g + 1) * tq)
            rows = slice(h * HEAD_DIM, (h + 1) * HEAD_DIM)
            o = acc_ref[:, cols] / jnp.maximum(l_ref[0:1, cols], 1e-30)
            r = 3 * h + gate_row
            ot_ref[rows, :] += gtt_ref[r:r + 1, :] * o

    for h in range(NSA_HEADS):
        rows = slice(h * HEAD_DIM, (h + 1) * HEAD_DIM)
        ot_ref[rows, :] = gtt_ref[3 * h:3 * h + 1, :] * ocmpt_ref[rows, :]

    for kh in range(NSA_KV_HEADS):
        kcols = slice(kh * HEAD_DIM, (kh + 1) * HEAD_DIM)
        vrows = slice(kh * HEAD_DIM, (kh + 1) * HEAD_DIM)
        for g in range(G):
            h = kh * G + g
            qt4_ref[:, g * tq:(g + 1) * tq] = qt_ref[h * HEAD_DIM:(h + 1) * HEAD_DIM, :]

        def sel_valid(j):
            parts = [jnp.broadcast_to(mskt_ref[kh, pl.ds(j * blocks_per_tile + i, 1), :], (SEL_BLOCK, tq))
                     for i in range(blocks_per_tile)]
            return jnp.concatenate(parts, axis=0) > 0.5

        def sel_update(j0, n_tiles, bias_rows, last_is_diag):
            start = pl.multiple_of(j0 * tq, tq)
            parts = [sel_valid(j0 + i) for i in range(n_tiles)]
            if last_is_diag:
                parts[-1] = parts[-1] & causal
            update(kh, selk_ref[pl.ds(start, n_tiles * tq), kcols], selvt_ref[vrows, pl.ds(start, n_tiles * tq)],
                   bias_rows, jnp.concatenate(parts, axis=0))

        def win_update(j0, valid_parts, bias_rows):
            n = len(valid_parts) * tq
            start = pl.multiple_of(j0 * tq, tq)
            update(kh, wink_ref[pl.ds(start, n), kcols], winvt_ref[vrows, pl.ds(start, n)], bias_rows,
                   jnp.concatenate(valid_parts, axis=0))

        init()

        def far_group(jg, c):
            sel_update(jg * FAR_GROUP, FAR_GROUP, None, False)
            return c

        def far_single(j, c):
            sel_update(j, 1, None, False)
            return c

        n_far = jnp.maximum(qi - 1, 0)
        n_grp = n_far // FAR_GROUP
        lax.fori_loop(0, n_grp, far_group, 0)
        lax.fori_loop(n_grp * FAR_GROUP, n_far, far_single, 0)

        @pl.when(qi >= 1)
        def _():
            sel_update(qi - 1, 2, near_rows, True)

        @pl.when(qi == 0)
        def _():
            sel_update(0, 1, diag_rows, True)

        finish(kh, 1)

        init()
        all_valid = s_rel >= 0

        @pl.when(qi >= 2)
        def _():
            win_update(qi - 2, [s_rel >= t_rel, all_valid, causal], all_rows)

        @pl.when(qi == 1)
        def _():
            win_update(0, [all_valid, causal], near_rows)

        @pl.when(qi == 0)
        def _():
            win_update(0, [causal], diag_rows)

        finish(kh, 2)

    o_ref[...] = ot_ref[...].T


def _selwin_bias_tables(tbl, tq):
    sr = jnp.arange(tq)[:, None]
    tr = jnp.arange(tq)[None, :]
    far = tbl[:, NUM_BUCKETS - 1][:, None, None]
    near0 = _bias_lookup(tbl, tr - sr) - far
    near1 = _bias_lookup(tbl, tr - sr + tq) - far
    b = jnp.concatenate([jnp.zeros_like(near1), near1, near0], axis=1)
    b = b.reshape(NSA_KV_HEADS, NSA_GROUP, 3 * tq, tq).transpose(0, 2, 1, 3)
    return b.reshape(NSA_KV_HEADS, 3 * tq, NSA_GROUP * tq)


def _selwin_prompt(nqst, selkv, selvt, winkv, winvt, mskt, tbl, ocmp, gtt, b, t, tq):
    assert tq >= NSA_WINDOW // 2 and tq >= MAX_DISTANCE and tq % SEL_BLOCK == 0
    bias = _selwin_bias_tables(tbl, tq)
    nq = t // tq
    col_tile = lambda h: pl.BlockSpec((h, tq), lambda bi, i: (0, bi * nq + i))
    row_tile = lambda w: pl.BlockSpec((tq, w), lambda bi, i: (bi * nq + i, 0))
    return pl.pallas_call(
        functools.partial(_selwin_kernel, tq=tq),
        grid=(b, nq),
        in_specs=[col_tile(NSA_WIDTH),
                  pl.BlockSpec((t, LANES), lambda bi, i: (bi, 0)), pl.BlockSpec((LANES, t), lambda bi, i: (0, bi)),
                  pl.BlockSpec((t, LANES), lambda bi, i: (bi, 0)), pl.BlockSpec((LANES, t), lambda bi, i: (0, bi)),
                  pl.BlockSpec((None, NSA_KV_HEADS, N_SELBLK, tq), lambda bi, i: (bi, 0, 0, i)),
                  pl.BlockSpec(bias.shape, lambda bi, i: (0, 0, 0)),
                  col_tile(NSA_WIDTH), col_tile(32)],
        out_specs=row_tile(NSA_WIDTH),
        out_shape=jax.ShapeDtypeStruct((b * t, NSA_WIDTH), F32),
        scratch_shapes=[pltpu.VMEM((8, NSA_GROUP * tq), F32), pltpu.VMEM((8, NSA_GROUP * tq), F32),
                        pltpu.VMEM((HEAD_DIM, NSA_GROUP * tq), F32), pltpu.VMEM((NSA_WIDTH, tq), F32),
                        pltpu.VMEM((HEAD_DIM, NSA_GROUP * tq), BF16)],
        compiler_params=_cparams(("arbitrary", "arbitrary"), VMEM_LIMIT_BYTES),
        name="nsa_selwin",
    )(nqst, selkv, selvt, winkv, winvt, mskt, bias, ocmp, gtt)


PAGES_PER_STEP = 16
PAGE = 128


def _feature_major(cache):
    nd = cache.ndim
    t = jnp.transpose(cache, (0, 1) + tuple(range(3, nd)) + (2,))
    return t.reshape(cache.shape[0], cache.shape[1], -1, cache.shape[2])


def _page_specs(layer, rows, row_block, reverse_steps=None):
    group = (lambda j: j) if reverse_steps is None else (lambda j: reverse_steps - 1 - j)
    return [pl.BlockSpec((None, None, rows, PAGE), functools.partial(
        lambda bi, j, pt, r: (layer, pt[bi, group(j) * PAGES_PER_STEP + r], row_block, 0), r=r))
        for r in range(PAGES_PER_STEP)]


def _compress_sample_kernel(pt_ref, *refs, nseg):
    pages = refs[:PAGES_PER_STEP]
    pe_ref, w1_ref, w2_ref, kn_ref, ck_ref, cv_ref, srck_ref, srcv_ref, sh_ref = refs[PAGES_PER_STEP:]
    j = pl.program_id(1)
    for r, p_ref in enumerate(pages):
        rows = pl.ds(pl.multiple_of((j * PAGES_PER_STEP + r) * PAGE, PAGE), PAGE)
        srck_ref[rows, :] = p_ref[0:LANES, :].T
        srcv_ref[rows, :] = p_ref[LANES:2 * LANES, :].T

    @pl.when(j == pl.num_programs(1) - 1)
    def _():
        _compress_compute((srck_ref, srcv_ref), pe_ref, w1_ref, w2_ref, kn_ref, ck_ref, cv_ref, sh_ref, nseg)


def _compress_sample(cache_t, layer, page_table, pe, w1, w2, kn):
    sb, n_pages = page_table.shape
    past = n_pages * PAGE
    nseg = past // CMP_STRIDE
    full = lambda a: pl.BlockSpec(a.shape, lambda bi, j, pt: (0,) * a.ndim)
    o_spec = pl.BlockSpec((None, NSA_KV_HEADS, nseg, HEAD_DIM), lambda bi, j, pt: (bi, 0, 0, 0))
    o_shape = jax.ShapeDtypeStruct((sb, NSA_KV_HEADS, nseg, HEAD_DIM), BF16)
    return pl.pallas_call(
        functools.partial(_compress_sample_kernel, nseg=nseg),
        grid_spec=pltpu.PrefetchScalarGridSpec(
            num_scalar_prefetch=1, grid=(sb, n_pages // PAGES_PER_STEP),
            in_specs=_page_specs(layer, 256, 0) + [full(pe), full(w1), full(w2), full(kn)],
            out_specs=[o_spec, o_spec],
            scratch_shapes=[pltpu.VMEM((past, LANES), F32), pltpu.VMEM((past, LANES), F32),
                            pltpu.VMEM((nseg + 8, CMP_HIDDEN), F32)]),
        out_shape=[o_shape, o_shape],
        compiler_params=_cparams(("arbitrary", "arbitrary"), VMEM_LIMIT_BYTES),
        name="nsa_compress_sample",
    )(page_table, *([cache_t] * PAGES_PER_STEP), pe, w1, w2, kn)


def _selwin_sample_kernel(pt_ref, qs_ref, *refs):
    pages = refs[:PAGES_PER_STEP]
    (msk_ref, newkv_ref, winst_ref, newwr_ref, bsel_ref, bnew_ref, bwin_ref, expand_ref, ocmp_ref, gt_ref, o_ref,
     kv_ref, m_ref, l_ref, acc_ref, osel_ref, owin_ref) = refs[PAGES_PER_STEP:]
    j = pl.program_id(1)
    G = NSA_GROUP
    R = SAMPLE_ROWS
    for r, p_ref in enumerate(pages):
        kv_ref[:, pl.ds(pl.multiple_of((j * PAGES_PER_STEP + r) * PAGE, PAGE), PAGE)] = p_ref[...].astype(BF16)

    @pl.when(j == pl.num_programs(1) - 1)
    def _():
        rq = lax.broadcasted_iota(jnp.int32, (G * R, 1), 0) & (R - 1)
        new_valid = lax.broadcasted_iota(jnp.int32, (G * R, R), 1) <= rq
        win_valid = lax.broadcasted_iota(jnp.int32, (G * R, NSA_WINDOW), 1) >= rq

        def init():
            m_ref[...] = jnp.full_like(m_ref, NEG)
            l_ref[...] = jnp.zeros_like(l_ref)
            acc_ref[...] = jnp.zeros_like(acc_ref)

        def finish(dst_ref, kh):
            o = acc_ref[...] / jnp.maximum(l_ref[...], 1e-30)
            for g in range(G):
                h = kh * G + g
                dst_ref[:, h * HEAD_DIM:(h + 1) * HEAD_DIM] = o[g * R:(g + 1) * R]

        for kh in range(NSA_KV_HEADS):
            kcols = slice(kh * HEAD_DIM, (kh + 1) * HEAD_DIM)
            vcols = slice(128 + kh * HEAD_DIM, 128 + (kh + 1) * HEAD_DIM)
            q4 = _stack_heads(qs_ref, kh)
            mskb = msk_ref[kh].astype(BF16)

            init()
            mt = jnp.dot(mskb, expand_ref[...], preferred_element_type=F32) > 0.5
            valid = jnp.concatenate([mt] * G, axis=0)
            s = jnp.dot(q4, kv_ref[kcols, :], preferred_element_type=F32) + bsel_ref[kh]
            _online_update(s, valid, kv_ref[vcols, :], m_ref, l_ref, acc_ref, v_t=True)
            knew = newkv_ref[:, 256 + kh * HEAD_DIM:256 + (kh + 1) * HEAD_DIM].astype(BF16)
            vnew = newkv_ref[:, 384 + kh * HEAD_DIM:384 + (kh + 1) * HEAD_DIM].astype(BF16)
            _online_update(_dot_nt(q4, knew) + bnew_ref[kh], new_valid, vnew, m_ref, l_ref, acc_ref)
            finish(osel_ref, kh)

            init()
            kwin = winst_ref[kcols, :].astype(BF16)
            vwin = winst_ref[vcols, :].astype(BF16)
            _online_update(jnp.dot(q4, kwin, preferred_element_type=F32) + bwin_ref[kh], win_valid, vwin,
                           m_ref, l_ref, acc_ref, v_t=True)
            knew = newwr_ref[:, kcols].astype(BF16)
            vnew = newwr_ref[:, vcols].astype(BF16)
            _online_update(_dot_nt(q4, knew) + bnew_ref[kh], new_valid, vnew, m_ref, l_ref, acc_ref)
            finish(owin_ref, kh)

        for h in range(NSA_HEADS):
            cols = slice(h * HEAD_DIM, (h + 1) * HEAD_DIM)
            o_ref[:, cols] = (gt_ref[:, 3 * h:3 * h + 1] * ocmp_ref[:, cols]
                              + gt_ref[:, 3 * h + 1:3 * h + 2] * osel_ref[:, cols]
                              + gt_ref[:, 3 * h + 2:3 * h + 3] * owin_ref[:, cols])


def _sample_bias_tables(tbl, past):
    R = SAMPLE_ROWS
    r = jnp.arange(R)[:, None]
    stack = lambda a: a.reshape(NSA_KV_HEADS, NSA_GROUP * R, a.shape[-1])
    cached = _bias_lookup(tbl, past + r - jnp.arange(past)[None, :])
    new = _bias_lookup(tbl, r - jnp.arange(R)[None, :])
    win = _bias_lookup(tbl, NSA_WINDOW + r - jnp.arange(NSA_WINDOW)[None, :])
    return stack(cached), stack(new), stack(win)


def _selwin_sample(qs, cache_t, layer, page_table, msk, newkv, win_t, newwr, tbl, ocmp, gt):
    sb, n_pages = page_table.shape
    past = n_pages * PAGE
    assert win_t.shape[-1] == NSA_WINDOW and past >= NSA_WINDOW and past // SEL_BLOCK <= N_SELBLK
    bsel, bnew, bwin = _sample_bias_tables(tbl, past)
    expand = jnp.asarray((np.arange(N_SELBLK)[:, None] == np.arange(past)[None, :] // SEL_BLOCK).astype(np.float32),
                         dtype=BF16)
    R = SAMPLE_ROWS
    full = lambda a: pl.BlockSpec(a.shape, lambda bi, j, pt: (0,) * a.ndim)
    seq = lambda a: pl.BlockSpec((None,) + a.shape[1:], lambda bi, j, pt: (bi,) + (0,) * (a.ndim - 1))
    win_spec = pl.BlockSpec((None, None) + win_t.shape[2:], lambda bi, j, pt: (layer, bi, 0, 0))
    return pl.pallas_call(
        _selwin_sample_kernel,
        grid_spec=pltpu.PrefetchScalarGridSpec(
            num_scalar_prefetch=1, grid=(sb, n_pages // PAGES_PER_STEP),
            in_specs=[seq(qs)] + _page_specs(layer, 256, 1) + [seq(msk), seq(newkv), win_spec, seq(newwr),
                                                               full(bsel), full(bnew), full(bwin), full(expand),
                                                               seq(ocmp), seq(gt)],
            out_specs=pl.BlockSpec((None, R, NSA_WIDTH), lambda bi, j, pt: (bi, 0, 0)),
            scratch_shapes=[pltpu.VMEM((256, past), BF16),
                            pltpu.VMEM((NSA_GROUP * R, 1), F32), pltpu.VMEM((NSA_GROUP * R, 1), F32),
                            pltpu.VMEM((NSA_GROUP * R, HEAD_DIM), F32),
                            pltpu.VMEM((R, NSA_WIDTH), F32), pltpu.VMEM((R, NSA_WIDTH), F32)]),
        out_shape=jax.ShapeDtypeStruct((sb, R, NSA_WIDTH), F32),
        compiler_params=_cparams(("arbitrary", "arbitrary"), VMEM_LIMIT_BYTES),
        name="nsa_selwin_sample",
    )(page_table, qs, *([cache_t] * PAGES_PER_STEP), msk, newkv, win_t, newwr, bsel, bnew, bwin, expand, ocmp, gt)


def _fox_sample_kernel(pt_ref, qs_ref, *refs):
    P = PAGES_PER_STEP
    pages = refs[:P]
    lf_pages = refs[P:2 * P]
    (sl_ref, newkv_ref, lfnew_ref, hmask_ref, o_ref,
     qbd_ref, crel_ref, m_ref, l_ref, acc_ref, carry_ref) = refs[2 * P:]
    j = pl.program_id(1)
    R = SAMPLE_ROWS
    H = FOX_HEADS
    tk = P * PAGE

    @pl.when(j == 0)
    def _():
        carry_ref[...] = jnp.zeros_like(carry_ref)
        q = qs_ref[...].astype(F32)
        qbd_ref[...] = (jnp.concatenate([q] * H, axis=0) * hmask_ref[...]).astype(BF16)
        tri = jnp.where(lax.broadcasted_iota(jnp.int32, (R, R), 1) <= lax.broadcasted_iota(jnp.int32, (R, R), 0), 1.0, 0.0)
        crel = _dot_f32(tri, lfnew_ref[...])
        crel_ref[...] = jnp.concatenate([crel[:, h:h + 1] for h in range(H)], axis=0)
        m_ref[...] = jnp.full_like(m_ref, NEG)
        l_ref[...] = jnp.zeros_like(l_ref)
        acc_ref[...] = jnp.zeros_like(acc_ref)

    carry = carry_ref[:, 0:1]
    d_parts = [None] * P
    for r in reversed(range(P)):
        lf = lf_pages[r][...]
        d_parts[r] = _dot_f32(lf, sl_ref[...]) + carry
        carry = carry + jnp.sum(lf, axis=1, keepdims=True)
    carry_ref[...] = jnp.broadcast_to(carry, carry_ref.shape)
    dt = jnp.concatenate(d_parts, axis=1)

    kt = jnp.concatenate([p[0:FOX_WIDTH, :] for p in pages], axis=1).astype(BF16)
    vt = jnp.concatenate([p[FOX_WIDTH:, :] for p in pages], axis=1).astype(BF16)
    drows = jnp.concatenate([jnp.broadcast_to(dt[h:h + 1, :], (R, tk)) for h in range(H)], axis=0)
    s = jnp.dot(qbd_ref[...], kt, preferred_element_type=F32) + crel_ref[...] + drows
    _online_update(s, s > 2 * NEG, vt, m_ref, l_ref, acc_ref, v_t=True)

    @pl.when(j == pl.num_programs(1) - 1)
    def _():
        lf = lfnew_ref[...]
        iu = lax.broadcasted_iota(jnp.int32, (R, R), 0)
        ir = lax.broadcasted_iota(jnp.int32, (R, R), 1)
        a_le = jnp.where(ir <= iu, 1.0, 0.0)
        b_gt = jnp.where(iu > ir, 1.0, 0.0)
        dnew = jnp.concatenate([_dot_f32(a_le, lf[:, h:h + 1] * b_gt) for h in range(H)], axis=0)
        rq = lax.broadcasted_iota(jnp.int32, (H * R, 1), 0) & (R - 1)
        valid = lax.broadcasted_iota(jnp.int32, (H * R, R), 1) <= rq
        knew = newkv_ref[:, 0:FOX_WIDTH].astype(BF16)
        vnew = newkv_ref[:, FOX_WIDTH:].astype(BF16)
        _online_update(_dot_nt(qbd_ref[...], knew) + dnew, valid, vnew, m_ref, l_ref, acc_ref)
        o = (acc_ref[...] / jnp.maximum(l_ref[...], 1e-30)) * hmask_ref[...]
        out = o[0:R]
        for h in range(1, H):
            out = out + o[h * R:(h + 1) * R]
        o_ref[...] = out


def _fox_sample(qs, kv_t, logf_t, layer, page_table, newkv, lfnew):
    sb, n_pages = page_table.shape
    R = SAMPLE_ROWS
    n_steps = n_pages // PAGES_PER_STEP
    hmask = jnp.asarray((np.arange(FOX_HEADS * R)[:, None] // R == np.arange(FOX_WIDTH)[None, :] // HEAD_DIM)
                        .astype(np.float32))
    sl = jnp.asarray(np.tril(np.ones((PAGE, PAGE), np.float32), -1))
    seq = lambda a: pl.BlockSpec((None,) + a.shape[1:], lambda bi, j, pt: (bi,) + (0,) * (a.ndim - 1))
    full = lambda a: pl.BlockSpec(a.shape, lambda bi, j, pt: (0,) * a.ndim)
    return pl.pallas_call(
        _fox_sample_kernel,
        grid_spec=pltpu.PrefetchScalarGridSpec(
            num_scalar_prefetch=1, grid=(sb, n_steps),
            in_specs=[seq(qs)] + _page_specs(layer, 2 * FOX_WIDTH, 0, n_steps) + _page_specs(layer, FOX_HEADS, 0, n_steps)
                     + [full(sl), seq(newkv), seq(lfnew), full(hmask)],
            out_specs=pl.BlockSpec((None, R, FOX_WIDTH), lambda bi, j, pt: (bi, 0, 0)),
            scratch_shapes=[pltpu.VMEM((FOX_HEADS * R, FOX_WIDTH), BF16), pltpu.VMEM((FOX_HEADS * R, 1), F32),
                            pltpu.VMEM((FOX_HEADS * R, 1), F32), pltpu.VMEM((FOX_HEADS * R, 1), F32),
                            pltpu.VMEM((FOX_HEADS * R, FOX_WIDTH), F32), pltpu.VMEM((FOX_HEADS, LANES), F32)]),
        out_shape=jax.ShapeDtypeStruct((sb, R, FOX_WIDTH), F32),
        compiler_params=_cparams(("arbitrary", "arbitrary"), VMEM_LIMIT_BYTES),
        name="fox_sample",
    )(page_table, qs, *([kv_t] * PAGES_PER_STEP), *([logf_t] * PAGES_PER_STEP), sl, newkv, lfnew, hmask)


def _pad_cols(w, width):
    return jnp.pad(w, ((0, 0), (0, width - w.shape[1])))


def _prep_even_w(w):
    gq, gk, gv, glr, gog, nq, nkv, ng = _split(w, EVEN_SIZES)
    return jnp.concatenate([gq, gk, gv, gog, nq, nkv, _pad_cols(glr, LANES), _pad_cols(ng, LANES)], axis=1).astype(BF16)


def _prep_odd_w(w):
    fq, fk, fv, ff, cg = _split(w, ODD_SIZES)
    return jnp.concatenate([fq, fk, fv, cg, _pad_cols(ff, LANES)], axis=1).astype(BF16)


def _row(v):
    return v.reshape(1, -1).astype(F32)


def _tile_row(v, reps):
    return jnp.tile(v.astype(F32), reps).reshape(1, -1)


def kernel(x_prompt, x_sample, cache_nsa_kv, state_nsa_win, state_gla, cache_fox_kv, cache_fox_logf, state_conv, cache_mem_kv, page_table, mem_prompt, rel_bias, norm_mix, norm_xattn, norm_ffn, even_w_in, even_w_out, gla_w_gate, gla_b_gate, gla_out_norm, nsa_q_norm, nsa_k_norm, nsa_cmp_pe, nsa_cmp_w1, nsa_cmp_w2, odd_w_in, odd_w_out, fox_q_norm, fox_k_norm, fox_b_f, conv_w, conv_b, conv_ln_g, conv_ln_b, mem_norm, xa_wq, xa_wkv, xa_wo, xa_q_norm, xa_k_norm, ffn_w_in, ffn_w_out):
    B, T, _ = x_prompt.shape
    SB, SQ, _ = x_sample.shape
    depth = norm_mix.shape[0]
    past_len = page_table.shape[1] * cache_nsa_kv.shape[2]
    MP = B * T
    SR = SAMPLE_ROWS
    MS = SB * SR

    yp = x_prompt.reshape(MP, D_MODEL)
    ys = jnp.pad(x_sample, ((0, 0), (0, SR - SQ), (0, 0))).reshape(MS, D_MODEL)

    nsa_cache_t = _feature_major(cache_nsa_kv)
    fox_cache_t = _feature_major(cache_fox_kv)
    logf_t = _feature_major(cache_fox_logf)
    win_t = _feature_major(state_nsa_win)
    mem_cache_t = _feature_major(cache_mem_kv)

    nsa_kv_p, nsa_kv_s, win_p, win_s, gla_p, gla_s = [], [], [], [], [], []
    fox_kv_p, fox_kv_s, logf_p, logf_s, conv_p, conv_s, mem_kv_p = [], [], [], [], [], [], []

    for layer in range(depth):
        if layer % 2 == 0:
            e = layer // 2
            w_pad = _prep_even_w(even_w_in[e])
            wg_pad = jnp.pad(gla_w_gate[e], ((0, LANES - GLA_RANK), (0, 0))).astype(BF16)
            bg = _row(gla_b_gate[e])
            qn = _tile_row(nsa_q_norm[e], NSA_HEADS)
            kn1 = _tile_row(nsa_k_norm[e, 1], NSA_KV_HEADS)
            kn2 = _tile_row(nsa_k_norm[e, 2], NSA_KV_HEADS)
            gn = _row(gla_out_norm[e])
            w_out = even_w_out[e].astype(BF16)
            g_mix = _row(norm_mix[layer])
            pe = jnp.tile(nsa_cmp_pe[e].astype(F32), (1, 1, NSA_KV_HEADS))
            w1b = _compress_weights(nsa_cmp_w1[e])
            w2b = nsa_cmp_w2[e].astype(BF16)
            kn0 = _row(nsa_k_norm[e, 0])
            tbl = rel_bias.astype(F32).T
            (q, k, v, la, og, nqs, kvr, wr, gt, selkv, winkv, nqst, selvt, winvt, gtt) = _even_in(
                yp, g_mix, w_pad, wg_pad, bg, qn, kn1, kn2, 512)
            r3 = lambda a: a.reshape(B, T, a.shape[-1])
            s0t = jnp.zeros((B, GLA_HEADS, GLA_DV, GLA_DK), F32)
            o_gla, sfin_t = _gla(r3(q), r3(k), r3(v), r3(la), r3(og), gn, s0t, 256, B)
            kvr5 = kvr.reshape(B, T, 4, NSA_KV_HEADS, HEAD_DIM)
            wr5 = wr.reshape(B, T, 2, NSA_KV_HEADS, HEAD_DIM)
            ck, cv = _compress_prompt(r3(kvr), pe, w1b, w2b, kn0)
            ocmp_t, mskt = _cmp_topk_prompt(nqst, ck, cv, tbl, B, T, 256, SEL_TOPN)
            o_nsa = _selwin_prompt(nqst, selkv, selvt, winkv, winvt, mskt, tbl, ocmp_t, gtt, B, T, 256)
            yp = _out_proj(yp, o_gla.reshape(MP, GLA_WIDTH), o_nsa.reshape(MP, NSA_WIDTH),
                           w_out[:GLA_WIDTH], w_out[GLA_WIDTH:], 1024)
            nsa_kv_p.append(kvr5)
            win_p.append(wr5[:, -min(NSA_WINDOW, T):])
            gla_p.append(jnp.swapaxes(sfin_t, -1, -2))
            (q, k, v, la, og, nqs, kvr, wr, gt) = _even_in(ys, g_mix, w_pad, wg_pad, bg, qn, kn1, kn2, MS)[:9]
            pad16 = lambda a: jnp.pad(a.reshape(SB, SR, a.shape[-1]), ((0, 0), (0, SUB - SR), (0, 0)))
            s0t = jnp.swapaxes(state_gla[e], -1, -2)
            o_gla, snew_t = _gla(pad16(q), pad16(k), pad16(v), pad16(la), pad16(og), gn, s0t, SUB, 4, n_valid=SQ)
            o_gla = o_gla[:, :SR]
            s3 = lambda a: a.reshape(SB, SR, a.shape[-1])
            kvr5 = kvr.reshape(SB, SR, 4, NSA_KV_HEADS, HEAD_DIM)[:, :SQ]
            wr5 = wr.reshape(SB, SR, 2, NSA_KV_HEADS, HEAD_DIM)[:, :SQ]
            ck, cv = _compress_sample(nsa_cache_t, e, page_table, pe, w1b, w2b, kn0)
            ocmp, msk = _cmp_topk_sample(s3(nqs), ck, cv, tbl, SR, past_len, SEL_TOPN - 1, past_len // SEL_BLOCK)
            o_nsa = _selwin_sample(s3(nqs), nsa_cache_t, e, page_table, msk, s3(kvr), win_t, s3(wr), tbl, ocmp, s3(gt))
            ys = _out_proj(ys, o_gla.reshape(MS, GLA_WIDTH), o_nsa.reshape(MS, NSA_WIDTH),
                           w_out[:GLA_WIDTH], w_out[GLA_WIDTH:], MS)
            nsa_kv_s.append(kvr5)
            win_s.append(jnp.concatenate([state_nsa_win[e][:, SQ:], wr5], axis=1))
            gla_s.append(jnp.swapaxes(snew_t, -1, -2))
        else:
            j = layer // 2
            w_pad = _prep_odd_w(odd_w_in[j])
            qn = _tile_row(fox_q_norm[j], FOX_HEADS)
            kn = _tile_row(fox_k_norm[j], FOX_HEADS)
            bf_pad = jnp.pad(fox_b_f[j].astype(F32), (0, LANES - FOX_HEADS)).reshape(1, LANES)
            w_out = odd_w_out[j].astype(BF16)
            g_mix = _row(norm_mix[layer])
            cw = conv_w[j].astype(F32)
            cb, cg_, cbeta = _row(conv_b[j]), _row(conv_ln_g[j]), _row(conv_ln_b[j])
            qs, kv, kb, lf, c, u, qst, vbt = _odd_in(yp.reshape(B, T, D_MODEL), g_mix, w_pad, qn, kn, bf_pad, 256)
            c8 = c[:, :, :FOX_HEADS]
            o_fox = _fox_prompt(qst, kb, vbt, c8, jnp.swapaxes(c8, 1, 2), 512, 1024)
            o_conv, cst = _conv(u, jnp.zeros((B, CONV_WIDTH - 1, CONV_CH), F32), cw, cb, cg_, cbeta, 1024, 1024)
            yp = _out_proj(yp, o_fox.reshape(MP, FOX_WIDTH), o_conv.reshape(MP, CONV_CH),
                           w_out[:FOX_WIDTH], w_out[FOX_WIDTH:], 1024)
            fox_kv_p.append(kv.reshape(B, T, 2, FOX_HEADS, HEAD_DIM))
            logf_p.append(lf[:, :, :FOX_HEADS])
            conv_p.append(cst)
            qs, kv, kb, lf, c, u = _odd_in(ys.reshape(1, MS, D_MODEL), g_mix, w_pad, qn, kn, bf_pad, MS)[:6]
            s3 = lambda a: a.reshape(SB, SR, a.shape[-1])
            new_kv = kv.reshape(SB, SR, 2, FOX_HEADS, HEAD_DIM)[:, :SQ]
            lf_new = lf.reshape(SB, SR, LANES)[:, :SQ, :FOX_HEADS]
            o_fox = _fox_sample(s3(qs), fox_cache_t, logf_t, j, page_table, s3(kv), s3(lf))
            o_conv, cst = _conv(u.reshape(SB, SR, CONV_CH), state_conv[j], cw, cb, cg_, cbeta, SR, SQ)
            ys = _out_proj(ys, o_fox.reshape(MS, FOX_WIDTH), o_conv.reshape(MS, CONV_CH),
                           w_out[:FOX_WIDTH], w_out[FOX_WIDTH:], MS)
            fox_kv_s.append(new_kv)
            logf_s.append(lf_new)
            conv_s.append(cst)
        g_xa = _row(norm_xattn[layer])
        wq = xa_wq[layer].astype(BF16)
        wo = xa_wo[layer].astype(BF16)
        xqn = _tile_row(xa_q_norm[layer], XA_HEADS)
        mkv_t = _mem_kv(mem_prompt, _row(mem_norm[layer]), xa_wkv[layer].astype(BF16), _tile_row(xa_k_norm[layer], XA_HEADS))
        mem_kv_p.append(jnp.transpose(mkv_t.reshape(B, 2, XA_HEADS, HEAD_DIM, MEM_LEN), (0, 4, 1, 2, 3)))
        yp = _xattn(yp.reshape(B, T, D_MODEL), mkv_t[None], 0, g_xa, wq, wo, xqn, 1024).reshape(MP, D_MODEL)
        ys = _xattn(ys.reshape(SB, SR, D_MODEL), mem_cache_t, layer, g_xa, wq, wo, xqn, SR).reshape(MS, D_MODEL)
        g_ffn = _row(norm_ffn[layer])
        w_in = ffn_w_in[layer].astype(BF16)
        w_o = ffn_w_out[layer].astype(BF16)
        yp = _ffn(yp, g_ffn, w_in, w_o, 1024)
        ys = _ffn(ys, g_ffn, w_in, w_o, MS)

    yp = yp.reshape(B, T, D_MODEL)
    ys = ys.reshape(SB, SR, D_MODEL)[:, :SQ]
    return (yp, ys,
            jnp.stack(nsa_kv_p), jnp.stack(nsa_kv_s), jnp.stack(win_p), jnp.stack(win_s),
            jnp.stack(gla_p), jnp.stack(gla_s), jnp.stack(fox_kv_p), jnp.stack(fox_kv_s),
            jnp.stack(logf_p), jnp.stack(logf_s), jnp.stack(conv_p), jnp.stack(conv_s),
            jnp.stack(mem_kv_p))
```
